```python
import math
import jax, jax.numpy as jnp
from jax import lax
import numpy as np

D_MODEL = 1024
BATCH = 8
SEQ = 4096
DEPTH = 4

GRID_W = 64
EPS = 1e-6
NA_HEADS = 8
NA_HEAD_DIM = 64
NA_WIN_R = 8
NA_WIN_C = 16
NA_QCOLS = 16
NA_KCOLS = 2 * NA_WIN_C
GLA_HEADS = 4
GLA_DK = 64
GLA_DV = 128
GLA_GATE_RANK = 16
GLA_GATE_TAU = 16.0
GLA_CHUNK = 64
MLA_HEADS = 4
MLA_Q_RANK = 256
MLA_KV_RANK = 256
MLA_NOPE = 128
MLA_ROPE = 64
MLA_V = 128
MLA_QBLOCK = 128
ROPE_THETA = 10000.0
D_FF = 2816
N_BRANCH = 3
NA_W = NA_HEADS * NA_HEAD_DIM
GLA_QK_W = GLA_HEADS * GLA_DK
GLA_V_W = GLA_HEADS * GLA_DV
MLA_QK_HEAD = MLA_NOPE + MLA_ROPE
MLA_V_W = MLA_HEADS * MLA_V
IN_SPLITS = (NA_W, NA_W, NA_W,
             GLA_QK_W, GLA_QK_W, GLA_V_W, GLA_V_W, GLA_GATE_RANK, GLA_GATE_RANK,
             MLA_Q_RANK, MLA_KV_RANK, MLA_ROPE,
             N_BRANCH * D_MODEL)
D_IN = 6752

kernel_name = "hybrid_na_gla_mla_macaron_encoder"


def rms_norm(x, g):
    xf = x.astype(jnp.float32)
    y = xf * lax.rsqrt(jnp.mean(xf * xf, axis=-1, keepdims=True) + EPS)
    return (y * g.astype(jnp.float32)).astype(x.dtype)


def swiglu(h, w1, w3, w2):
    return (jax.nn.silu(h @ w1) * (h @ w3)) @ w2


def split_cols(z, sizes):
    out, start = [], 0
    for n in sizes:
        out.append(z[..., start:start + n])
        start += n
    return out


def neighborhood_attention(q, k, v, rpb):
    B, S, H, d = q.shape
    rows = S // GRID_W
    win_r = min(NA_WIN_R, rows)
    nj = GRID_W // NA_QCOLS
    r = np.arange(rows)
    r0 = np.clip(r - win_r // 2, 0, rows - win_r)
    ridx = r0[:, None] + np.arange(win_r)
    j = np.arange(nj)
    k0 = np.clip(j * NA_QCOLS - NA_WIN_C // 2, 0, GRID_W - NA_KCOLS)
    cidx = k0[:, None] + np.arange(NA_KCOLS)
    qc = j[:, None] * NA_QCOLS + np.arange(NA_QCOLS)
    c0 = np.clip(qc - NA_WIN_C // 2, 0, GRID_W - NA_WIN_C)
    col_ok = (cidx[:, None, :] >= c0[..., None]) & (cidx[:, None, :] < c0[..., None] + NA_WIN_C)
    mask = np.broadcast_to(col_ok[:, :, None, :], (nj, NA_QCOLS, win_r, NA_KCOLS)).reshape(nj, NA_QCOLS, win_r * NA_KCOLS)
    dr = ridx - r[:, None] + (NA_WIN_R - 1)
    dc = np.clip(cidx[:, None, :] - qc[:, :, None] + (NA_WIN_C - 1), 0, 2 * NA_WIN_C - 2)
    bias = rpb[:, dr[:, None, None, :, None], dc[None, :, :, None, :]]
    bias = bias.reshape(H, rows, nj, NA_QCOLS, win_r * NA_KCOLS)

    ri = ridx[:, None, :, None]
    ci = cidx[None, :, None, :]
    kg = k.reshape(B, rows, GRID_W, H, d)[:, ri, ci].reshape(B, rows, nj, win_r * NA_KCOLS, H, d)
    vg = v.reshape(B, rows, GRID_W, H, d)[:, ri, ci].reshape(B, rows, nj, win_r * NA_KCOLS, H, d)
    qg = q.reshape(B, rows, nj, NA_QCOLS, H, d)
    s = jnp.einsum('brjqhd,brjkhd->bhrjqk', qg, kg).astype(jnp.float32) * (d ** -0.5)
    s = jnp.where(mask, s + bias.astype(jnp.float32), -1e30)
    p = jax.nn.softmax(s, axis=-1).astype(v.dtype)
    o = jnp.einsum('bhrjqk,brjkhd->brjqhd', p, vg)
    return o.reshape(B, S, H * d)


def gla_chunked(q, k, v, g, strict):
    B, H, S, dk = q.shape
    dv = v.shape[-1]
    n = S // GLA_CHUNK
    q = q.reshape(B, H, n, GLA_CHUNK, dk)
    k = k.reshape(B, H, n, GLA_CHUNK, dk)
    g = g.reshape(B, H, n, GLA_CHUNK, dk)
    v = v.reshape(B, H, n, GLA_CHUNK, dv)
    b = jnp.cumsum(g, axis=3)
    b_last = b[:, :, :, -1:, :]
    qe = q * jnp.exp(b)
    ke = k * jnp.exp(-b)
    k_end = k * jnp.exp(b_last - b)
    tri = np.tril(np.ones((GLA_CHUNK, GLA_CHUNK), dtype=bool), -1 if strict else 0)
    a = jnp.where(tri, jnp.einsum('bhnid,bhnjd->bhnij', qe, ke), 0.0)
    o_intra = jnp.einsum('bhnij,bhnjv->bhniv', a, v)
    upd = jnp.einsum('bhncd,bhncv->bhndv', k_end, v)
    decay = jnp.exp(b_last[:, :, :, 0, :])

    def step(state, inp):
        dec, u = inp
        return dec[..., None] * state + u, state

    init = jnp.zeros((B, H, dk, dv), q.dtype)
    _, s_prev = lax.scan(step, init, (jnp.moveaxis(decay, 2, 0), jnp.moveaxis(upd, 2, 0)))
    s_prev = jnp.moveaxis(s_prev, 0, 2)
    o = o_intra + jnp.einsum('bhnid,bhndv->bhniv', qe, s_prev)
    return o.reshape(B, H, S, dv)


def gla_bidirectional(q, k, v, g_fwd, g_bwd):
    t = lambda a: jnp.swapaxes(a.astype(jnp.float32), 1, 2)
    q, k, v, g_fwd, g_bwd = t(q), t(k), t(v), t(g_fwd), t(g_bwd)
    q = q * (GLA_DK ** -0.5)
    flip = lambda a: a[:, :, ::-1]
    o_f = gla_chunked(q, k, v, g_fwd, False)
    o_b = flip(gla_chunked(flip(q), flip(k), flip(v), flip(g_bwd), True))
    return jnp.swapaxes(o_f + o_b, 1, 2)


def rope_tables(S):
    half = MLA_ROPE // 2
    inv = ROPE_THETA ** (-jnp.arange(half, dtype=jnp.float32) / half)
    ang = jnp.arange(S, dtype=jnp.float32)[:, None] * inv[None, :]
    return jnp.cos(ang), jnp.sin(ang)


def apply_rope(x, cos, sin):
    xf = x.astype(jnp.float32)
    x1, x2 = xf[..., :MLA_ROPE // 2], xf[..., MLA_ROPE // 2:]
    c, s = cos[None, :, None, :], sin[None, :, None, :]
    return jnp.concatenate([x1 * c - x2 * s, x1 * s + x2 * c], axis=-1).astype(x.dtype)


def blocked_softmax_attention(q, k, v, scale):
    B, S, H, dq = q.shape
    nb = S // MLA_QBLOCK
    qb = jnp.moveaxis(q.reshape(B, nb, MLA_QBLOCK, H, dq), 1, 0)

    def one(qblk):
        s = jnp.einsum('bqhd,bkhd->bhqk', qblk, k).astype(jnp.float32) * scale
        p = jax.nn.softmax(s, axis=-1).astype(v.dtype)
        return jnp.einsum('bhqk,bkhd->bqhd', p, v)

    o = lax.map(one, qb)
    return jnp.moveaxis(o, 0, 1).reshape(B, S, H * v.shape[-1])


def mla_attention(c_q, c_kv, k_rope, cq_norm, ckv_norm, w_uq, w_ukv, q_norm, k_norm, cos, sin):
    B, S, _ = c_q.shape
    q = (rms_norm(c_q, cq_norm) @ w_uq).reshape(B, S, MLA_HEADS, MLA_QK_HEAD)
    kv = (rms_norm(c_kv, ckv_norm) @ w_ukv).reshape(B, S, MLA_HEADS, MLA_NOPE + MLA_V)
    k_nope, v = kv[..., :MLA_NOPE], kv[..., MLA_NOPE:]
    k_r = jnp.broadcast_to(k_rope[:, :, None, :], (B, S, MLA_HEADS, MLA_ROPE))
    k = jnp.concatenate([k_nope, k_r], axis=-1)
    q = rms_norm(q, q_norm)
    k = rms_norm(k, k_norm)
    q = jnp.concatenate([q[..., :MLA_NOPE], apply_rope(q[..., MLA_NOPE:], cos, sin)], axis=-1)
    k = jnp.concatenate([k[..., :MLA_NOPE], apply_rope(k[..., MLA_NOPE:], cos, sin)], axis=-1)
    return blocked_softmax_attention(q, k, v, MLA_QK_HEAD ** -0.5)


def _fwd_setup_inputs(seed: int = 0) -> dict:
    key = jax.random.key(seed)
    ks = iter(jax.random.split(key, 32))
    L = DEPTH
    f32 = jnp.float32

    def w(shape, fan_in):
        return jax.random.normal(next(ks), shape, f32) * (fan_in ** -0.5)

    def gain(shape):
        return 1.0 + 0.05 * jax.random.normal(next(ks), shape, f32)

    def small(shape, scale, offset=0.0):
        return offset + scale * jax.random.normal(next(ks), shape, f32)

    return {
        "x": jax.random.normal(next(ks), (BATCH, SEQ, D_MODEL), f32),
        "ffn1_norm": gain((L, D_MODEL)),
        "ffn1_w1": w((L, D_MODEL, D_FF), D_MODEL),
        "ffn1_w3": w((L, D_MODEL, D_FF), D_MODEL),
        "ffn1_w2": w((L, D_FF, D_MODEL), D_FF),
        "mix_norm": gain((L, D_MODEL)),
        "w_in": w((L, D_MODEL, D_IN), D_MODEL),
        "na_q_norm": gain((L, NA_HEAD_DIM)),
        "na_k_norm": gain((L, NA_HEAD_DIM)),
        "na_rpb": small((L, NA_HEADS, 2 * NA_WIN_R - 1, 2 * NA_WIN_C - 1), 0.1),
        "gla_gf_up": w((L, GLA_GATE_RANK, GLA_QK_W), GLA_GATE_RANK),
        "gla_gf_bias": small((L, GLA_QK_W), 0.1, 2.0),
        "gla_gb_up": w((L, GLA_GATE_RANK, GLA_QK_W), GLA_GATE_RANK),
        "gla_gb_bias": small((L, GLA_QK_W), 0.1, 2.0),
        "gla_out_norm": gain((L, GLA_DV)),
        "mla_cq_norm": gain((L, MLA_Q_RANK)),
        "mla_ckv_norm": gain((L, MLA_KV_RANK)),
        "mla_w_uq": w((L, MLA_Q_RANK, MLA_HEADS * MLA_QK_HEAD), MLA_Q_RANK),
        "mla_w_ukv": w((L, MLA_KV_RANK, MLA_HEADS * (MLA_NOPE + MLA_V)), MLA_KV_RANK),
        "mla_q_norm": gain((L, MLA_QK_HEAD)),
        "mla_k_norm": gain((L, MLA_QK_HEAD)),
        "w_br_na": w((L, NA_W, D_MODEL), NA_W),
        "w_br_gla": w((L, GLA_V_W, D_MODEL), GLA_V_W),
        "w_br_mla": w((L, MLA_V_W, D_MODEL), MLA_V_W),
        "w_out": w((L, D_MODEL, D_MODEL), D_MODEL),
        "ffn2_norm": gain((L, D_MODEL)),
        "ffn2_w1": w((L, D_MODEL, D_FF), D_MODEL),
        "ffn2_w3": w((L, D_MODEL, D_FF), D_MODEL),
        "ffn2_w2": w((L, D_FF, D_MODEL), D_FF),
    }


def _fwd_reference(x, ffn1_norm, ffn1_w1, ffn1_w3, ffn1_w2, mix_norm, w_in,
              na_q_norm, na_k_norm, na_rpb,
              gla_gf_up, gla_gf_bias, gla_gb_up, gla_gb_bias, gla_out_norm,
              mla_cq_norm, mla_ckv_norm, mla_w_uq, mla_w_ukv, mla_q_norm, mla_k_norm,
              w_br_na, w_br_gla, w_br_mla, w_out,
              ffn2_norm, ffn2_w1, ffn2_w3, ffn2_w2):
    B, S, D = x.shape
    cos, sin = rope_tables(S)
    for l in range(DEPTH):
        x = x + 0.5 * swiglu(rms_norm(x, ffn1_norm[l]), ffn1_w1[l], ffn1_w3[l], ffn1_w2[l])

        h = rms_norm(x, mix_norm[l])
        z = h @ w_in[l]
        (na_q, na_k, na_v, gq, gk, gv, gr, gfl, gbl, c_q, c_kv, k_rope, gates) = split_cols(z, IN_SPLITS)

        qa = rms_norm(na_q.reshape(B, S, NA_HEADS, NA_HEAD_DIM), na_q_norm[l])
        ka = rms_norm(na_k.reshape(B, S, NA_HEADS, NA_HEAD_DIM), na_k_norm[l])
        va = na_v.reshape(B, S, NA_HEADS, NA_HEAD_DIM)
        y_na = neighborhood_attention(qa, ka, va, na_rpb[l])

        g_f = jax.nn.log_sigmoid((gfl @ gla_gf_up[l] + gla_gf_bias[l]).astype(jnp.float32)) / GLA_GATE_TAU
        g_b = jax.nn.log_sigmoid((gbl @ gla_gb_up[l] + gla_gb_bias[l]).astype(jnp.float32)) / GLA_GATE_TAU
        o_gla = gla_bidirectional(gq.reshape(B, S, GLA_HEADS, GLA_DK), gk.reshape(B, S, GLA_HEADS, GLA_DK),
                                  gv.reshape(B, S, GLA_HEADS, GLA_DV),
                                  g_f.reshape(B, S, GLA_HEADS, GLA_DK), g_b.reshape(B, S, GLA_HEADS, GLA_DK))
        o_gla = rms_norm(o_gla, gla_out_norm[l]).astype(x.dtype).reshape(B, S, GLA_V_W)
        y_gla = o_gla * jax.nn.silu(gr)

        y_mla = mla_attention(c_q, c_kv, k_rope, mla_cq_norm[l], mla_ckv_norm[l], mla_w_uq[l], mla_w_ukv[l],
                              mla_q_norm[l], mla_k_norm[l], cos, sin)

        gt = jax.nn.sigmoid(gates.reshape(B, S, N_BRANCH, D))
        mixed = (gt[:, :, 0] * (y_na @ w_br_na[l])
                 + gt[:, :, 1] * (y_gla @ w_br_gla[l])
                 + gt[:, :, 2] * (y_mla @ w_br_mla[l]))
        x = x + mixed @ w_out[l]

        x = x + 0.5 * swiglu(rms_norm(x, ffn2_norm[l]), ffn2_w1[l], ffn2_w3[l], ffn2_w2[l])
    return x


import jax as _jax
import jax.numpy as _jnp

TWIN_FORMAT = 'train_step'
FWD_PARAMS = ['x', 'ffn1_norm', 'ffn1_w1', 'ffn1_w3', 'ffn1_w2', 'mix_norm', 'w_in', 'na_q_norm', 'na_k_norm', 'na_rpb', 'gla_gf_up', 'gla_gf_bias', 'gla_gb_up', 'gla_gb_bias', 'gla_out_norm', 'mla_cq_norm', 'mla_ckv_norm', 'mla_w_uq', 'mla_w_ukv', 'mla_q_norm', 'mla_k_norm', 'w_br_na', 'w_br_gla', 'w_br_mla', 'w_out', 'ffn2_norm', 'ffn2_w1', 'ffn2_w3', 'ffn2_w2']
TWIN_WEIGHTS = ['ffn1_norm', 'ffn1_w1', 'ffn1_w3', 'ffn1_w2', 'mix_norm', 'w_in', 'na_q_norm', 'na_k_norm', 'na_rpb', 'gla_gf_up', 'gla_gf_bias', 'gla_gb_up', 'gla_gb_bias', 'gla_out_norm', 'mla_cq_norm', 'mla_ckv_norm', 'mla_w_uq', 'mla_w_ukv', 'mla_q_norm', 'mla_k_norm', 'w_br_na', 'w_br_gla', 'w_br_mla', 'w_out', 'ffn2_norm', 'ffn2_w1', 'ffn2_w3', 'ffn2_w2']
TWIN_DIFF_INPUT = 'x'
TWIN_INPUTS = ['x', 'ffn1_norm', 'ffn1_w1', 'ffn1_w3', 'ffn1_w2', 'mix_norm', 'w_in', 'na_q_norm', 'na_k_norm', 'na_rpb', 'gla_gf_up', 'gla_gf_bias', 'gla_gb_up', 'gla_gb_bias', 'gla_out_norm', 'mla_cq_norm', 'mla_ckv_norm', 'mla_w_uq', 'mla_w_ukv', 'mla_q_norm', 'mla_k_norm', 'w_br_na', 'w_br_gla', 'w_br_mla', 'w_out', 'ffn2_norm', 'ffn2_w1', 'ffn2_w3', 'ffn2_w2', 'loss_target', 'm_ffn1_norm', 'm_ffn1_w1', 'm_ffn1_w3', 'm_ffn1_w2', 'm_mix_norm', 'm_w_in', 'm_na_q_norm', 'm_na_k_norm', 'm_na_rpb', 'm_gla_gf_up', 'm_gla_gf_bias', 'm_gla_gb_up', 'm_gla_gb_bias', 'm_gla_out_norm', 'm_mla_cq_norm', 'm_mla_ckv_norm', 'm_mla_w_uq', 'm_mla_w_ukv', 'm_mla_q_norm', 'm_mla_k_norm', 'm_w_br_na', 'm_w_br_gla', 'm_w_br_mla', 'm_w_out', 'm_ffn2_norm', 'm_ffn2_w1', 'm_ffn2_w3', 'm_ffn2_w2', 'v_ffn1_norm', 'v_ffn1_w1', 'v_ffn1_w3', 'v_ffn1_w2', 'v_mix_norm', 'v_w_in', 'v_na_q_norm', 'v_na_k_norm', 'v_na_rpb', 'v_gla_gf_up', 'v_gla_gf_bias', 'v_gla_gb_up', 'v_gla_gb_bias', 'v_gla_out_norm', 'v_mla_cq_norm', 'v_mla_ckv_norm', 'v_mla_w_uq', 'v_mla_w_ukv', 'v_mla_q_norm', 'v_mla_k_norm', 'v_w_br_na', 'v_w_br_gla', 'v_w_br_mla', 'v_w_out', 'v_ffn2_norm', 'v_ffn2_w1', 'v_ffn2_w3', 'v_ffn2_w2']
TWIN_OUTPUTS = ['loss', 'grad_x', 'grad_ffn1_norm', 'grad_ffn1_w1', 'grad_ffn1_w3', 'grad_ffn1_w2', 'grad_mix_norm', 'grad_w_in', 'grad_na_q_norm', 'grad_na_k_norm', 'grad_na_rpb', 'grad_gla_gf_up', 'grad_gla_gf_bias', 'grad_gla_gb_up', 'grad_gla_gb_bias', 'grad_gla_out_norm', 'grad_mla_cq_norm', 'grad_mla_ckv_norm', 'grad_mla_w_uq', 'grad_mla_w_ukv', 'grad_mla_q_norm', 'grad_mla_k_norm', 'grad_w_br_na', 'grad_w_br_gla', 'grad_w_br_mla', 'grad_w_out', 'grad_ffn2_norm', 'grad_ffn2_w1', 'grad_ffn2_w3', 'grad_ffn2_w2', 'delta_ffn1_norm', 'delta_ffn1_w1', 'delta_ffn1_w3', 'delta_ffn1_w2', 'delta_mix_norm', 'delta_w_in', 'delta_na_q_norm', 'delta_na_k_norm', 'delta_na_rpb', 'delta_gla_gf_up', 'delta_gla_gf_bias', 'delta_gla_gb_up', 'delta_gla_gb_bias', 'delta_gla_out_norm', 'delta_mla_cq_norm', 'delta_mla_ckv_norm', 'delta_mla_w_uq', 'delta_mla_w_ukv', 'delta_mla_q_norm', 'delta_mla_k_norm', 'delta_w_br_na', 'delta_w_br_gla', 'delta_w_br_mla', 'delta_w_out', 'delta_ffn2_norm', 'delta_ffn2_w1', 'delta_ffn2_w3', 'delta_ffn2_w2', 'new_m_ffn1_norm', 'new_m_ffn1_w1', 'new_m_ffn1_w3', 'new_m_ffn1_w2', 'new_m_mix_norm', 'new_m_w_in', 'new_m_na_q_norm', 'new_m_na_k_norm', 'new_m_na_rpb', 'new_m_gla_gf_up', 'new_m_gla_gf_bias', 'new_m_gla_gb_up', 'new_m_gla_gb_bias', 'new_m_gla_out_norm', 'new_m_mla_cq_norm', 'new_m_mla_ckv_norm', 'new_m_mla_w_uq', 'new_m_mla_w_ukv', 'new_m_mla_q_norm', 'new_m_mla_k_norm', 'new_m_w_br_na', 'new_m_w_br_gla', 'new_m_w_br_mla', 'new_m_w_out', 'new_m_ffn2_norm', 'new_m_ffn2_w1', 'new_m_ffn2_w3', 'new_m_ffn2_w2', 'new_v_ffn1_norm', 'new_v_ffn1_w1', 'new_v_ffn1_w3', 'new_v_ffn1_w2', 'new_v_mix_norm', 'new_v_w_in', 'new_v_na_q_norm', 'new_v_na_k_norm', 'new_v_na_rpb', 'new_v_gla_gf_up', 'new_v_gla_gf_bias', 'new_v_gla_gb_up', 'new_v_gla_gb_bias', 'new_v_gla_out_norm', 'new_v_mla_cq_norm', 'new_v_mla_ckv_norm', 'new_v_mla_w_uq', 'new_v_mla_w_ukv', 'new_v_mla_q_norm', 'new_v_mla_k_norm', 'new_v_w_br_na', 'new_v_w_br_gla', 'new_v_w_br_mla', 'new_v_w_out', 'new_v_ffn2_norm', 'new_v_ffn2_w1', 'new_v_ffn2_w3', 'new_v_ffn2_w2']
TWIN_LEAF_KINDS = {'loss': 'loss', 'grad_x': 'grad_x', 'grad_ffn1_norm': 'grad_w', 'grad_ffn1_w1': 'grad_w', 'grad_ffn1_w3': 'grad_w', 'grad_ffn1_w2': 'grad_w', 'grad_mix_norm': 'grad_w', 'grad_w_in': 'grad_w', 'grad_na_q_norm': 'grad_w', 'grad_na_k_norm': 'grad_w', 'grad_na_rpb': 'grad_w', 'grad_gla_gf_up': 'grad_w', 'grad_gla_gf_bias': 'grad_w', 'grad_gla_gb_up': 'grad_w', 'grad_gla_gb_bias': 'grad_w', 'grad_gla_out_norm': 'grad_w', 'grad_mla_cq_norm': 'grad_w', 'grad_mla_ckv_norm': 'grad_w', 'grad_mla_w_uq': 'grad_w', 'grad_mla_w_ukv': 'grad_w', 'grad_mla_q_norm': 'grad_w', 'grad_mla_k_norm': 'grad_w', 'grad_w_br_na': 'grad_w', 'grad_w_br_gla': 'grad_w', 'grad_w_br_mla': 'grad_w', 'grad_w_out': 'grad_w', 'grad_ffn2_norm': 'grad_w', 'grad_ffn2_w1': 'grad_w', 'grad_ffn2_w3': 'grad_w', 'grad_ffn2_w2': 'grad_w', 'delta_ffn1_norm': 'delta_w', 'delta_ffn1_w1': 'delta_w', 'delta_ffn1_w3': 'delta_w', 'delta_ffn1_w2': 'delta_w', 'delta_mix_norm': 'delta_w', 'delta_w_in': 'delta_w', 'delta_na_q_norm': 'delta_w', 'delta_na_k_norm': 'delta_w', 'delta_na_rpb': 'delta_w', 'delta_gla_gf_up': 'delta_w', 'delta_gla_gf_bias': 'delta_w', 'delta_gla_gb_up': 'delta_w', 'delta_gla_gb_bias': 'delta_w', 'delta_gla_out_norm': 'delta_w', 'delta_mla_cq_norm': 'delta_w', 'delta_mla_ckv_norm': 'delta_w', 'delta_mla_w_uq': 'delta_w', 'delta_mla_w_ukv': 'delta_w', 'delta_mla_q_norm': 'delta_w', 'delta_mla_k_norm': 'delta_w', 'delta_w_br_na': 'delta_w', 'delta_w_br_gla': 'delta_w', 'delta_w_br_mla': 'delta_w', 'delta_w_out': 'delta_w', 'delta_ffn2_norm': 'delta_w', 'delta_ffn2_w1': 'delta_w', 'delta_ffn2_w3': 'delta_w', 'delta_ffn2_w2': 'delta_w', 'new_m_ffn1_norm': 'new_m', 'new_m_ffn1_w1': 'new_m', 'new_m_ffn1_w3': 'new_m', 'new_m_ffn1_w2': 'new_m', 'new_m_mix_norm': 'new_m', 'new_m_w_in': 'new_m', 'new_m_na_q_norm': 'new_m', 'new_m_na_k_norm': 'new_m', 'new_m_na_rpb': 'new_m', 'new_m_gla_gf_up': 'new_m', 'new_m_gla_gf_bias': 'new_m', 'new_m_gla_gb_up': 'new_m', 'new_m_gla_gb_bias': 'new_m', 'new_m_gla_out_norm': 'new_m', 'new_m_mla_cq_norm': 'new_m', 'new_m_mla_ckv_norm': 'new_m', 'new_m_mla_w_uq': 'new_m', 'new_m_mla_w_ukv': 'new_m', 'new_m_mla_q_norm': 'new_m', 'new_m_mla_k_norm': 'new_m', 'new_m_w_br_na': 'new_m', 'new_m_w_br_gla': 'new_m', 'new_m_w_br_mla': 'new_m', 'new_m_w_out': 'new_m', 'new_m_ffn2_norm': 'new_m', 'new_m_ffn2_w1': 'new_m', 'new_m_ffn2_w3': 'new_m', 'new_m_ffn2_w2': 'new_m', 'new_v_ffn1_norm': 'new_v', 'new_v_ffn1_w1': 'new_v', 'new_v_ffn1_w3': 'new_v', 'new_v_ffn1_w2': 'new_v', 'new_v_mix_norm': 'new_v', 'new_v_w_in': 'new_v', 'new_v_na_q_norm': 'new_v', 'new_v_na_k_norm': 'new_v', 'new_v_na_rpb': 'new_v', 'new_v_gla_gf_up': 'new_v', 'new_v_gla_gf_bias': 'new_v', 'new_v_gla_gb_up': 'new_v', 'new_v_gla_gb_bias': 'new_v', 'new_v_gla_out_norm': 'new_v', 'new_v_mla_cq_norm': 'new_v', 'new_v_mla_ckv_norm': 'new_v', 'new_v_mla_w_uq': 'new_v', 'new_v_mla_w_ukv': 'new_v', 'new_v_mla_q_norm': 'new_v', 'new_v_mla_k_norm': 'new_v', 'new_v_w_br_na': 'new_v', 'new_v_w_br_gla': 'new_v', 'new_v_w_br_mla': 'new_v', 'new_v_w_out': 'new_v', 'new_v_ffn2_norm': 'new_v', 'new_v_ffn2_w1': 'new_v', 'new_v_ffn2_w3': 'new_v', 'new_v_ffn2_w2': 'new_v'}


def _forward(args):
    return _fwd_reference(*[args[k] for k in FWD_PARAMS])


def _output_shape():
    def fwd():
        inp = _fwd_setup_inputs(0)
        return _fwd_reference(*[inp[k] for k in FWD_PARAMS])
    out = _jax.eval_shape(fwd)
    return out.shape, out.dtype

N_MICROBATCH = 1
ADAM_LR = 0.001
ADAM_B1 = 0.9
ADAM_B2 = 0.999
ADAM_EPS = 1e-08
ADAM_WD = 0.01
ADAM_STEP = 10
PER_EXAMPLE_BATCH_AXIS = {'x': 0, 'loss_target': 0}
SHARED_INPUTS = []
_WEIGHT_DTYPES = {'ffn1_norm': _jnp.float32, 'ffn1_w1': _jnp.float32, 'ffn1_w3': _jnp.float32, 'ffn1_w2': _jnp.float32, 'mix_norm': _jnp.float32, 'w_in': _jnp.float32, 'na_q_norm': _jnp.float32, 'na_k_norm': _jnp.float32, 'na_rpb': _jnp.float32, 'gla_gf_up': _jnp.float32, 'gla_gf_bias': _jnp.float32, 'gla_gb_up': _jnp.float32, 'gla_gb_bias': _jnp.float32, 'gla_out_norm': _jnp.float32, 'mla_cq_norm': _jnp.float32, 'mla_ckv_norm': _jnp.float32, 'mla_w_uq': _jnp.float32, 'mla_w_ukv': _jnp.float32, 'mla_q_norm': _jnp.float32, 'mla_k_norm': _jnp.float32, 'w_br_na': _jnp.float32, 'w_br_gla': _jnp.float32, 'w_br_mla': _jnp.float32, 'w_out': _jnp.float32, 'ffn2_norm': _jnp.float32, 'ffn2_w1': _jnp.float32, 'ffn2_w3': _jnp.float32, 'ffn2_w2': _jnp.float32}
MOMENT_SCALE = {'ffn1_norm': 6.200143e+00, 'ffn1_w1': 1.209667e-01, 'ffn1_w3': 1.255217e-01, 'ffn1_w2': 2.066763e-01, 'mix_norm': 4.213263e+00, 'w_in': 1.624779e-01, 'na_q_norm': 1.436578e+00, 'na_k_norm': 1.426041e+00, 'na_rpb': 2.841670e-02, 'gla_gf_up': 3.742050e-02, 'gla_gf_bias': 1.848276e-01, 'gla_gb_up': 3.840525e-02, 'gla_gb_bias': 1.845145e-01, 'gla_out_norm': 2.701047e+01, 'mla_cq_norm': 6.932987e-02, 'mla_ckv_norm': 2.274555e-01, 'mla_w_uq': 4.081521e-02, 'mla_w_ukv': 6.097135e-02, 'mla_q_norm': 1.776025e-01, 'mla_k_norm': 1.759913e-01, 'w_br_na': 7.196196e-02, 'w_br_gla': 2.244229e-01, 'w_br_mla': 5.045219e-02, 'w_out': 2.149724e-01, 'ffn2_norm': 6.165667e+00, 'ffn2_w1': 9.693867e-02, 'ffn2_w3': 1.096560e-01, 'ffn2_w2': 1.802314e-01}


def _to_microbatches(a, axis):
    t = _jnp.moveaxis(a, axis, 0)
    t = t.reshape((N_MICROBATCH, t.shape[0] // N_MICROBATCH) + t.shape[1:])
    return _jnp.moveaxis(t, 1, axis + 1)


def setup_inputs(seed: int = 0) -> dict:
    inp = _fwd_setup_inputs(seed)
    key = _jax.random.fold_in(_jax.random.key(seed), 7919)
    shape, _ = _output_shape()
    out = dict(inp)
    out["loss_target"] = _jax.random.normal(_jax.random.fold_in(key, 0), shape, _jnp.float32)
    for i, name in enumerate(TWIN_WEIGHTS):
        w = inp[name].astype(_jnp.float32)
        if MOMENT_SCALE is None:
            s = _jnp.sqrt(_jnp.mean(_jnp.square(w)) + 1e-30)
        else:
            s = MOMENT_SCALE[name]
        km, kv = _jax.random.split(_jax.random.fold_in(key, i + 1))
        out[name] = w
        out["m_" + name] = s * _jax.random.normal(km, w.shape, _jnp.float32)
        out["v_" + name] = (s * s) * _jax.random.uniform(kv, w.shape, _jnp.float32, 0.5, 1.5)
    if N_MICROBATCH > 1:
        for name, axis in PER_EXAMPLE_BATCH_AXIS.items():
            out[name] = _to_microbatches(out[name], axis)
    return {'x': out['x'], 'ffn1_norm': out['ffn1_norm'], 'ffn1_w1': out['ffn1_w1'], 'ffn1_w3': out['ffn1_w3'], 'ffn1_w2': out['ffn1_w2'], 'mix_norm': out['mix_norm'], 'w_in': out['w_in'], 'na_q_norm': out['na_q_norm'], 'na_k_norm': out['na_k_norm'], 'na_rpb': out['na_rpb'], 'gla_gf_up': out['gla_gf_up'], 'gla_gf_bias': out['gla_gf_bias'], 'gla_gb_up': out['gla_gb_up'], 'gla_gb_bias': out['gla_gb_bias'], 'gla_out_norm': out['gla_out_norm'], 'mla_cq_norm': out['mla_cq_norm'], 'mla_ckv_norm': out['mla_ckv_norm'], 'mla_w_uq': out['mla_w_uq'], 'mla_w_ukv': out['mla_w_ukv'], 'mla_q_norm': out['mla_q_norm'], 'mla_k_norm': out['mla_k_norm'], 'w_br_na': out['w_br_na'], 'w_br_gla': out['w_br_gla'], 'w_br_mla': out['w_br_mla'], 'w_out': out['w_out'], 'ffn2_norm': out['ffn2_norm'], 'ffn2_w1': out['ffn2_w1'], 'ffn2_w3': out['ffn2_w3'], 'ffn2_w2': out['ffn2_w2'], 'loss_target': out['loss_target'], 'm_ffn1_norm': out['m_ffn1_norm'], 'm_ffn1_w1': out['m_ffn1_w1'], 'm_ffn1_w3': out['m_ffn1_w3'], 'm_ffn1_w2': out['m_ffn1_w2'], 'm_mix_norm': out['m_mix_norm'], 'm_w_in': out['m_w_in'], 'm_na_q_norm': out['m_na_q_norm'], 'm_na_k_norm': out['m_na_k_norm'], 'm_na_rpb': out['m_na_rpb'], 'm_gla_gf_up': out['m_gla_gf_up'], 'm_gla_gf_bias': out['m_gla_gf_bias'], 'm_gla_gb_up': out['m_gla_gb_up'], 'm_gla_gb_bias': out['m_gla_gb_bias'], 'm_gla_out_norm': out['m_gla_out_norm'], 'm_mla_cq_norm': out['m_mla_cq_norm'], 'm_mla_ckv_norm': out['m_mla_ckv_norm'], 'm_mla_w_uq': out['m_mla_w_uq'], 'm_mla_w_ukv': out['m_mla_w_ukv'], 'm_mla_q_norm': out['m_mla_q_norm'], 'm_mla_k_norm': out['m_mla_k_norm'], 'm_w_br_na': out['m_w_br_na'], 'm_w_br_gla': out['m_w_br_gla'], 'm_w_br_mla': out['m_w_br_mla'], 'm_w_out': out['m_w_out'], 'm_ffn2_norm': out['m_ffn2_norm'], 'm_ffn2_w1': out['m_ffn2_w1'], 'm_ffn2_w3': out['m_ffn2_w3'], 'm_ffn2_w2': out['m_ffn2_w2'], 'v_ffn1_norm': out['v_ffn1_norm'], 'v_ffn1_w1': out['v_ffn1_w1'], 'v_ffn1_w3': out['v_ffn1_w3'], 'v_ffn1_w2': out['v_ffn1_w2'], 'v_mix_norm': out['v_mix_norm'], 'v_w_in': out['v_w_in'], 'v_na_q_norm': out['v_na_q_norm'], 'v_na_k_norm': out['v_na_k_norm'], 'v_na_rpb': out['v_na_rpb'], 'v_gla_gf_up': out['v_gla_gf_up'], 'v_gla_gf_bias': out['v_gla_gf_bias'], 'v_gla_gb_up': out['v_gla_gb_up'], 'v_gla_gb_bias': out['v_gla_gb_bias'], 'v_gla_out_norm': out['v_gla_out_norm'], 'v_mla_cq_norm': out['v_mla_cq_norm'], 'v_mla_ckv_norm': out['v_mla_ckv_norm'], 'v_mla_w_uq': out['v_mla_w_uq'], 'v_mla_w_ukv': out['v_mla_w_ukv'], 'v_mla_q_norm': out['v_mla_q_norm'], 'v_mla_k_norm': out['v_mla_k_norm'], 'v_w_br_na': out['v_w_br_na'], 'v_w_br_gla': out['v_w_br_gla'], 'v_w_br_mla': out['v_w_br_mla'], 'v_w_out': out['v_w_out'], 'v_ffn2_norm': out['v_ffn2_norm'], 'v_ffn2_w1': out['v_ffn2_w1'], 'v_ffn2_w3': out['v_ffn2_w3'], 'v_ffn2_w2': out['v_ffn2_w2']}


def _loss(weights, diff, rest, loss_target):
    with _jax.named_scope("forward"):
        args = {**rest, TWIN_DIFF_INPUT: diff, **{k: w.astype(_WEIGHT_DTYPES[k]) for k, w in weights.items()}}
        y = _forward(args)
    with _jax.named_scope("loss_head"):
        err = _jnp.square(y.astype(_jnp.float32) - loss_target)
        return 0.5 * _jnp.sum(_jnp.mean(err, axis=-1)) if err.ndim else 0.5 * err


def _adamw(w, g, m, v):
    m = ADAM_B1 * m + (1.0 - ADAM_B1) * g
    v = ADAM_B2 * v + (1.0 - ADAM_B2) * _jnp.square(g)
    m_hat = m / (1.0 - ADAM_B1 ** ADAM_STEP)
    v_hat = v / (1.0 - ADAM_B2 ** ADAM_STEP)
    delta = -ADAM_LR * (m_hat / (_jnp.sqrt(v_hat) + ADAM_EPS) + ADAM_WD * w)
    return delta, m, v


def reference(x, ffn1_norm, ffn1_w1, ffn1_w3, ffn1_w2, mix_norm, w_in, na_q_norm, na_k_norm, na_rpb, gla_gf_up, gla_gf_bias, gla_gb_up, gla_gb_bias, gla_out_norm, mla_cq_norm, mla_ckv_norm, mla_w_uq, mla_w_ukv, mla_q_norm, mla_k_norm, w_br_na, w_br_gla, w_br_mla, w_out, ffn2_norm, ffn2_w1, ffn2_w3, ffn2_w2, loss_target, m_ffn1_norm, m_ffn1_w1, m_ffn1_w3, m_ffn1_w2, m_mix_norm, m_w_in, m_na_q_norm, m_na_k_norm, m_na_rpb, m_gla_gf_up, m_gla_gf_bias, m_gla_gb_up, m_gla_gb_bias, m_gla_out_norm, m_mla_cq_norm, m_mla_ckv_norm, m_mla_w_uq, m_mla_w_ukv, m_mla_q_norm, m_mla_k_norm, m_w_br_na, m_w_br_gla, m_w_br_mla, m_w_out, m_ffn2_norm, m_ffn2_w1, m_ffn2_w3, m_ffn2_w2, v_ffn1_norm, v_ffn1_w1, v_ffn1_w3, v_ffn1_w2, v_mix_norm, v_w_in, v_na_q_norm, v_na_k_norm, v_na_rpb, v_gla_gf_up, v_gla_gf_bias, v_gla_gb_up, v_gla_gb_bias, v_gla_out_norm, v_mla_cq_norm, v_mla_ckv_norm, v_mla_w_uq, v_mla_w_ukv, v_mla_q_norm, v_mla_k_norm, v_w_br_na, v_w_br_gla, v_w_br_mla, v_w_out, v_ffn2_norm, v_ffn2_w1, v_ffn2_w3, v_ffn2_w2):
    given = dict(x=x, ffn1_norm=ffn1_norm, ffn1_w1=ffn1_w1, ffn1_w3=ffn1_w3, ffn1_w2=ffn1_w2, mix_norm=mix_norm, w_in=w_in, na_q_norm=na_q_norm, na_k_norm=na_k_norm, na_rpb=na_rpb, gla_gf_up=gla_gf_up, gla_gf_bias=gla_gf_bias, gla_gb_up=gla_gb_up, gla_gb_bias=gla_gb_bias, gla_out_norm=gla_out_norm, mla_cq_norm=mla_cq_norm, mla_ckv_norm=mla_ckv_norm, mla_w_uq=mla_w_uq, mla_w_ukv=mla_w_ukv, mla_q_norm=mla_q_norm, mla_k_norm=mla_k_norm, w_br_na=w_br_na, w_br_gla=w_br_gla, w_br_mla=w_br_mla, w_out=w_out, ffn2_norm=ffn2_norm, ffn2_w1=ffn2_w1, ffn2_w3=ffn2_w3, ffn2_w2=ffn2_w2, loss_target=loss_target, m_ffn1_norm=m_ffn1_norm, m_ffn1_w1=m_ffn1_w1, m_ffn1_w3=m_ffn1_w3, m_ffn1_w2=m_ffn1_w2, m_mix_norm=m_mix_norm, m_w_in=m_w_in, m_na_q_norm=m_na_q_norm, m_na_k_norm=m_na_k_norm, m_na_rpb=m_na_rpb, m_gla_gf_up=m_gla_gf_up, m_gla_gf_bias=m_gla_gf_bias, m_gla_gb_up=m_gla_gb_up, m_gla_gb_bias=m_gla_gb_bias, m_gla_out_norm=m_gla_out_norm, m_mla_cq_norm=m_mla_cq_norm, m_mla_ckv_norm=m_mla_ckv_norm, m_mla_w_uq=m_mla_w_uq, m_mla_w_ukv=m_mla_w_ukv, m_mla_q_norm=m_mla_q_norm, m_mla_k_norm=m_mla_k_norm, m_w_br_na=m_w_br_na, m_w_br_gla=m_w_br_gla, m_w_br_mla=m_w_br_mla, m_w_out=m_w_out, m_ffn2_norm=m_ffn2_norm, m_ffn2_w1=m_ffn2_w1, m_ffn2_w3=m_ffn2_w3, m_ffn2_w2=m_ffn2_w2, v_ffn1_norm=v_ffn1_norm, v_ffn1_w1=v_ffn1_w1, v_ffn1_w3=v_ffn1_w3, v_ffn1_w2=v_ffn1_w2, v_mix_norm=v_mix_norm, v_w_in=v_w_in, v_na_q_norm=v_na_q_norm, v_na_k_norm=v_na_k_norm, v_na_rpb=v_na_rpb, v_gla_gf_up=v_gla_gf_up, v_gla_gf_bias=v_gla_gf_bias, v_gla_gb_up=v_gla_gb_up, v_gla_gb_bias=v_gla_gb_bias, v_gla_out_norm=v_gla_out_norm, v_mla_cq_norm=v_mla_cq_norm, v_mla_ckv_norm=v_mla_ckv_norm, v_mla_w_uq=v_mla_w_uq, v_mla_w_ukv=v_mla_w_ukv, v_mla_q_norm=v_mla_q_norm, v_mla_k_norm=v_mla_k_norm, v_w_br_na=v_w_br_na, v_w_br_gla=v_w_br_gla, v_w_br_mla=v_w_br_mla, v_w_out=v_w_out, v_ffn2_norm=v_ffn2_norm, v_ffn2_w1=v_ffn2_w1, v_ffn2_w3=v_ffn2_w3, v_ffn2_w2=v_ffn2_w2)
    weights = {n: given[n] for n in TWIN_WEIGHTS}
    shared = {n: given[n] for n in SHARED_INPUTS}
    per_example = {n: given[n] for n in ['x']}
    grad_fn = _jax.value_and_grad(_loss, argnums=(0, 1))

    def one_microbatch(ex, loss_target):
        ex = dict(ex)
        diff = ex.pop(TWIN_DIFF_INPUT)
        return grad_fn(weights, diff, {**shared, **ex}, loss_target)

    if N_MICROBATCH == 1:
        loss, (grad_w, grad_x) = one_microbatch(per_example, given["loss_target"])
    else:
        def body(carry, xs):
            loss_sum, grad_sum = carry
            l_k, (gw_k, gx_k) = one_microbatch(xs[0], xs[1])
            with _jax.named_scope("update"):
                return (loss_sum + l_k, _jax.tree.map(_jnp.add, grad_sum, gw_k)), gx_k

        init = (_jnp.zeros((), _jnp.float32), _jax.tree.map(_jnp.zeros_like, weights))
        (loss, grad_w), grad_x = _jax.lax.scan(body, init, (per_example, given["loss_target"]))
    with _jax.named_scope("update"):
        delta_w, new_m, new_v = {}, {}, {}
        for n in TWIN_WEIGHTS:
            delta_w[n], new_m[n], new_v[n] = _adamw(weights[n], grad_w[n], given["m_" + n], given["v_" + n])
    return (loss, grad_x, *[grad_w[n] for n in TWIN_WEIGHTS], *[delta_w[n] for n in TWIN_WEIGHTS],
            *[new_m[n] for n in TWIN_WEIGHTS], *[new_v[n] for n in TWIN_WEIGHTS])
```

```python
import functools
import math

import numpy as np
import jax
import jax.numpy as jnp
from jax import lax
from jax.experimental import pallas as pl
from jax.experimental.pallas import tpu as pltpu

F32 = jnp.float32
BF16 = jnp.bfloat16
HI = lax.Precision.HIGHEST
MESH = pl.DeviceIdType.MESH

EPS = 1e-6
GRID_W = 64
NA_HEADS, NA_DH, NA_WIN_R, NA_WIN_C = 8, 64, 8, 16
NA_W = NA_HEADS * NA_DH
GLA_HEADS, GLA_DK, GLA_DV, GLA_RANK, GLA_TAU, GLA_CHUNK = 4, 64, 128, 16, 16.0, 64
MLA_HEADS, MLA_RANK, MLA_NOPE, MLA_ROPE, MLA_V = 4, 256, 128, 64, 128
MLA_QK = MLA_NOPE + MLA_ROPE
MLA_SLOT = 256
ROPE_THETA = 10000.0
ADAM_LR, ADAM_B1, ADAM_B2, ADAM_EPS, ADAM_WD, ADAM_STEP = 0.001, 0.9, 0.999, 1e-08, 0.01, 10

V7X_VMEM_BYTES = 64 * 2**20
VMEM_LIMIT = V7X_VMEM_BYTES - 12 * 2**20
NEG = -1e30

O_GQ, O_GFL, O_CQ, O_KR, O_GATES = 1536, 3072, 3104, 3616, 3680


_ANY = pl.BlockSpec(memory_space=pl.ANY)


def _cparams(*sem):
    return pltpu.CompilerParams(dimension_semantics=sem, vmem_limit_bytes=VMEM_LIMIT)


class _V:
    def __init__(self, arr, c0=0, w=None, lead=()):
        self.arr, self.c0, self.lead = arr, c0, tuple(lead)
        assert arr.ndim == 2 + len(self.lead), (arr.shape, lead)
        self.w = arr.shape[-1] if w is None else w

    @property
    def rows(self):
        return self.arr.shape[-2]

    def spec(self, br, bc, rfn, cfn):
        assert self.c0 % bc == 0 and self.w % bc == 0, (self.c0, self.w, bc)
        off, lead = self.c0 // bc, self.lead

        def index(*g):
            return tuple(g[0] if e == "b" else e for e in lead) + (rfn(*g), off + cfn(*g))

        return pl.BlockSpec((None,) * len(lead) + (br, bc), index)


def _v(x):
    return x if isinstance(x, _V) else _V(x)


_DN = {"nn": (((1,), (0,)), ((), ())), "nt": (((1,), (1,)), ((), ())), "tn": (((0,), (0,)), ((), ()))}


def _dot(a, b, mode="nn", prec=None):
    return lax.dot_general(a, b, _DN[mode], preferred_element_type=F32, precision=prec)


def _tile(n, cap):
    if n <= cap:
        return n
    for t in range(cap - cap % 128, 0, -128):
        if n % t == 0:
            return t
    return n


def _mm(pairs, mode, out_dtype, name, *, tm, tn, res=None, scale=None, batch=1, into=None):
    pairs = [(_v(a), _v(b)) for a, b in pairs]
    a0, b0 = pairs[0]
    M = a0.w if mode == "tn" else a0.rows
    N = b0.rows if mode == "nt" else b0.w
    tm, tn = _tile(M, tm), _tile(N, tn)
    assert M % tm == 0 and N % tn == 0, (name, M, N, tm, tn)
    n = len(pairs)

    def body(*refs):
        o_ref = refs[-1]
        acc = None
        for i in range(n):
            d = _dot(refs[2 * i][...].astype(BF16), refs[2 * i + 1][...].astype(BF16), mode)
            acc = d if acc is None else acc + d
        if scale is not None:
            acc = acc * scale
        if res is not None:
            acc = acc + refs[2 * n][...]
        o_ref[...] = acc.astype(o_ref.dtype)

    zero = lambda b, i, j: 0
    row = lambda b, i, j: i
    col = lambda b, i, j: j
    in_specs, args = [], []
    for a, b in pairs:
        in_specs.append(a.spec(a.rows, tm, zero, row) if mode == "tn" else a.spec(tm, a.w, row, zero))
        in_specs.append(b.spec(tn, b.w, col, zero) if mode == "nt" else b.spec(b.rows, tn, zero, col))
        args += [a.arr, b.arr]
    if res is not None:
        in_specs.append(pl.BlockSpec((tm, tn), lambda b, i, j: (i, j)))
        args.append(res)
    aliases = {}
    if into is None:
        out = jax.ShapeDtypeStruct(((batch,) if batch > 1 else ()) + (M, N), out_dtype)
        out_view = _V(out, lead=("b",) if batch > 1 else ())
    else:
        buf, lead = into
        assert buf.shape[-2:] == (M, N) and buf.dtype == out_dtype, (name, buf.shape, M, N)
        out = jax.ShapeDtypeStruct(buf.shape, buf.dtype)
        out_view = _V(out, lead=lead)
        aliases = {len(args): 0}
        in_specs.append(_ANY)
        args.append(buf)
    return pl.pallas_call(
        body, name=name, grid=(batch, M // tm, N // tn), in_specs=in_specs, out_specs=out_view.spec(tm, tn, row, col),
        out_shape=out, input_output_aliases=aliases, compiler_params=_cparams("parallel", "parallel", "parallel"),
    )(*args)


def _rms_fwd(x, g, name, tm=512):
    x = _v(x)
    S, D = x.rows, x.w
    tm = min(tm, S)

    def body(x_ref, g_ref, o_ref):
        xv = x_ref[...]
        y = xv * lax.rsqrt(jnp.mean(xv * xv, axis=-1, keepdims=True) + EPS)
        o_ref[...] = (y * g_ref[...]).astype(o_ref.dtype)

    return pl.pallas_call(
        body, name=name, grid=(S // tm,),
        in_specs=[x.spec(tm, D, lambda i: i, lambda i: 0), pl.BlockSpec((1, D), lambda i: (0, 0))],
        out_specs=pl.BlockSpec((tm, D), lambda i: (i, 0)),
        out_shape=jax.ShapeDtypeStruct((S, D), BF16), compiler_params=_cparams("parallel"),
    )(x.arr, g)


def _rms_bwd(x, g, dh, name, dres=None, out_dtype=F32, tm=512):
    x = _v(x)
    S, D = x.rows, x.w
    tm = min(tm, S)

    def body(*refs):
        if dres is None:
            x_ref, g_ref, dh_ref, dx_ref, dg_ref = refs
        else:
            x_ref, g_ref, dh_ref, dr_ref, dx_ref, dg_ref = refs
        xv = x_ref[...]
        rstd = lax.rsqrt(jnp.mean(xv * xv, axis=-1, keepdims=True) + EPS)
        xhat = xv * rstd
        dhv = dh_ref[...].astype(F32)
        dxhat = dhv * g_ref[...]
        dx = rstd * (dxhat - xhat * jnp.mean(dxhat * xhat, axis=-1, keepdims=True))
        if dres is not None:
            dx = dx + dr_ref[...]
        dx_ref[...] = dx.astype(dx_ref.dtype)

        @pl.when(pl.program_id(0) == 0)
        def _():
            dg_ref[...] = jnp.zeros_like(dg_ref)

        dg_ref[0:1, :] += jnp.sum(dhv * xhat, axis=0, keepdims=True)

    in_specs = [x.spec(tm, D, lambda i: i, lambda i: 0), pl.BlockSpec((1, D), lambda i: (0, 0)),
                pl.BlockSpec((tm, D), lambda i: (i, 0))]
    args = [x.arr, g, dh]
    if dres is not None:
        in_specs.append(pl.BlockSpec((tm, D), lambda i: (i, 0)))
        args.append(dres)
    return pl.pallas_call(
        body, name=name, grid=(S // tm,), in_specs=in_specs,
        out_specs=[pl.BlockSpec((tm, D), lambda i: (i, 0)), pl.BlockSpec((8, D), lambda i: (0, 0))],
        out_shape=[jax.ShapeDtypeStruct((S, D), out_dtype), jax.ShapeDtypeStruct((8, D), F32)],
        compiler_params=_cparams("arbitrary"),
    )(*args)


def _ffn_up(h, w1, w3, name, tm=512):
    S, D = h.shape
    NC, _, F4 = w1.shape
    tm = min(tm, S)

    def body(h_ref, w1_ref, w3_ref, a_ref, b_ref, u_ref):
        hv = h_ref[...]
        a = _dot(hv, w1_ref[...])
        b = _dot(hv, w3_ref[...])
        a_ref[...] = a.astype(BF16)
        b_ref[...] = b.astype(BF16)
        u_ref[...] = (a * jax.nn.sigmoid(a) * b).astype(BF16)

    blk = pl.BlockSpec((None, tm, F4), lambda i, j: (j, i, 0))
    wblk = pl.BlockSpec((None, D, F4), lambda i, j: (j, 0, 0))
    return pl.pallas_call(
        body, name=name, grid=(S // tm, NC), in_specs=[pl.BlockSpec((tm, D), lambda i, j: (i, 0)), wblk, wblk],
        out_specs=[blk, blk, blk], out_shape=[jax.ShapeDtypeStruct((NC, S, F4), BF16)] * 3,
        compiler_params=_cparams("parallel", "parallel"),
    )(h, w1, w3)


def _ffn_down_bwd(dxo, w2, a, b, name, tm=512):
    S, D = dxo.shape
    NC, F4, _ = w2.shape
    tm = min(tm, S)

    def body(dx_ref, w2_ref, a_ref, b_ref, da_ref, db_ref):
        du = _dot(dx_ref[...].astype(BF16), w2_ref[...], "nt") * 0.5
        av = a_ref[...].astype(F32)
        sig = jax.nn.sigmoid(av)
        da_ref[...] = (du * b_ref[...].astype(F32) * (sig * (1.0 + av * (1.0 - sig)))).astype(BF16)
        db_ref[...] = (du * av * sig).astype(BF16)

    blk = pl.BlockSpec((None, tm, F4), lambda i, j: (j, i, 0))
    return pl.pallas_call(
        body, name=name, grid=(S // tm, NC),
        in_specs=[pl.BlockSpec((tm, D), lambda i, j: (i, 0)), pl.BlockSpec((None, F4, D), lambda i, j: (j, 0, 0)), blk, blk],
        out_specs=[blk, blk], out_shape=[jax.ShapeDtypeStruct((NC, S, F4), BF16)] * 2,
        compiler_params=_cparams("parallel", "parallel"),
    )(dxo, w2, a, b)


def _ffn_fwd(x, g, w1, w3, w2, tag):
    h = _rms_fwd(x, g, f"{tag}_rms")
    a, b, u = _ffn_up(h, w1, w3, f"{tag}_up")
    nc = w2.shape[0]
    y = _mm([(_V(u, lead=(j,)), _V(w2, lead=(j,))) for j in range(nc)], "nn", F32, f"{tag}_down", tm=512, tn=1024, res=x, scale=0.5)
    return y, (h, a, b, u)


def _ffn_bwd(dxo, x, g, w1, w3, w2, saved, tag, bufs, layer):
    h, a, b, u = saved
    nc, D, F4 = w1.shape
    da, db = _ffn_down_bwd(dxo, w2, a, b, f"{tag}_down_bwd")
    into = lambda k: (bufs[k], ("b", layer))
    dw2 = _mm([(_V(u, lead=("b",)), dxo)], "tn", BF16, f"{tag}_dw2", tm=F4, tn=512, scale=0.5, batch=nc, into=into(2))
    dw1 = _mm([(h, _V(da, lead=("b",)))], "tn", BF16, f"{tag}_dw1", tm=D, tn=F4, batch=nc, into=into(0))
    dw3 = _mm([(h, _V(db, lead=("b",)))], "tn", BF16, f"{tag}_dw3", tm=D, tn=F4, batch=nc, into=into(1))
    pairs = [(_V(da, lead=(j,)), _V(w1, lead=(j,))) for j in range(nc)] + [(_V(db, lead=(j,)), _V(w3, lead=(j,))) for j in range(nc)]
    dh = _mm(pairs, "nt", F32, f"{tag}_dh", tm=512, tn=512)
    dx, dg = _rms_bwd(x, g, dh, f"{tag}_rms_bwd", dres=dxo)
    return dx, dg[0:1], (dw1, dw3, dw2)


def _iota(shape, dim):
    return lax.broadcasted_iota(jnp.int32, shape, dim)


def _head_block_ones(n, shift):
    return jnp.where((_iota((n, n), 0) >> shift) == (_iota((n, n), 1) >> shift), 1.0, 0.0).astype(F32)


def _lane_mask(width, lo, size):
    l = _iota((1, width), 1)
    return jnp.where((l >= lo) & (l < lo + size), 1.0, 0.0).astype(F32)


def _acc_rows(acc_ref, val, first):
    r = val.shape[0]
    part = jnp.sum(val.reshape(r // 8, 8, val.shape[1]), axis=0)

    @pl.when(first)
    def _():
        acc_ref[...] = part

    @pl.when(jnp.logical_not(first))
    def _():
        acc_ref[...] += part


def _na_prep(z, c0, gq, gk, name, tm=512):
    S = z.shape[0]
    tm = min(tm, S)
    zv = _V(z, c0, 3 * NA_W)

    def body(z_ref, gq_ref, gk_ref, q_ref, k_ref, v_ref):
        bd = _head_block_ones(NA_W, 6)

        def norm(xv, gv):
            ms = _dot(xv * xv, bd, prec=HI) * (1.0 / NA_DH)
            return xv * lax.rsqrt(ms + EPS) * gv

        q_ref[...] = (norm(z_ref[:, 0:NA_W], gq_ref[...]) * (NA_DH ** -0.5)).astype(BF16)
        k_ref[...] = norm(z_ref[:, NA_W:2 * NA_W], gk_ref[...]).astype(BF16)
        v_ref[...] = z_ref[:, 2 * NA_W:3 * NA_W].astype(BF16)

    blk = pl.BlockSpec((tm, NA_W), lambda i: (i, 0))
    gspec = pl.BlockSpec((1, NA_W), lambda i: (0, 0))
    return pl.pallas_call(
        body, name=name, grid=(S // tm,),
        in_specs=[zv.spec(tm, 3 * NA_W, lambda i: i, lambda i: 0), gspec, gspec],
        out_specs=[blk, blk, blk], out_shape=[jax.ShapeDtypeStruct((S, NA_W), BF16)] * 3,
        compiler_params=_cparams("parallel"),
    )(z, gq, gk)


def _na_prep_bwd(z, c0, gq, gk, dqn, dkn, dv, name, tm=512):
    S = z.shape[0]
    tm = min(tm, S)
    zv = _V(z, c0, 3 * NA_W)

    def body(z_ref, gq_ref, gk_ref, dq_ref, dk_ref, dv_ref, dz_ref, dgq_ref, dgk_ref):
        bd = _head_block_ones(NA_W, 6)
        first = pl.program_id(0) == 0

        def norm_bwd(xv, gv, dy, dg_ref):
            ms = _dot(xv * xv, bd, prec=HI) * (1.0 / NA_DH)
            rstd = lax.rsqrt(ms + EPS)
            xhat = xv * rstd
            dxhat = dy * gv
            proj = _dot(dxhat * xhat, bd, prec=HI) * (1.0 / NA_DH)
            _acc_rows(dg_ref, dy * xhat, first)
            return rstd * (dxhat - xhat * proj)

        dz_ref[:, 0:NA_W] = norm_bwd(z_ref[:, 0:NA_W], gq_ref[...], dq_ref[...] * (NA_DH ** -0.5), dgq_ref).astype(BF16)
        dz_ref[:, NA_W:2 * NA_W] = norm_bwd(z_ref[:, NA_W:2 * NA_W], gk_ref[...], dk_ref[...], dgk_ref).astype(BF16)
        dz_ref[:, 2 * NA_W:3 * NA_W] = dv_ref[...].astype(BF16)

    blk = pl.BlockSpec((tm, NA_W), lambda i: (i, 0))
    gspec = pl.BlockSpec((1, NA_W), lambda i: (0, 0))
    acc = pl.BlockSpec((8, NA_W), lambda i: (0, 0))
    return pl.pallas_call(
        body, name=name, grid=(S // tm,),
        in_specs=[zv.spec(tm, 3 * NA_W, lambda i: i, lambda i: 0), gspec, gspec, blk, blk, blk],
        out_specs=[pl.BlockSpec((tm, 3 * NA_W), lambda i: (i, 0)), acc, acc],
        out_shape=[jax.ShapeDtypeStruct((S, 3 * NA_W), BF16), jax.ShapeDtypeStruct((8, NA_W), F32),
                   jax.ShapeDtypeStruct((8, NA_W), F32)],
        compiler_params=_cparams("arbitrary"),
    )(z, gq, gk, dqn, dkn, dv)


def _na_onehot():
    qc = np.arange(GRID_W)[:, None]
    kc = np.arange(GRID_W)[None, :]
    c0 = np.clip(qc - NA_WIN_C // 2, 0, GRID_W - NA_WIN_C)
    valid = (kc >= c0) & (kc < c0 + NA_WIN_C)
    dc = kc - qc + (NA_WIN_C - 1)
    e = np.zeros((32, GRID_W, GRID_W), np.float32)
    for d in range(2 * NA_WIN_C - 1):
        e[d] = valid & (dc == d)
    return e.reshape(32, GRID_W * GRID_W), valid.reshape(1, -1)


def _rpb_expand(rpb, name):
    e, valid = _na_onehot()
    negmask = np.where(valid, 0.0, NEG).astype(np.float32)
    nd = 2 * NA_WIN_R - 1
    r2 = jnp.pad(rpb.reshape(NA_HEADS * nd, 2 * NA_WIN_C - 1), ((0, 128 - NA_HEADS * nd), (0, 1)))

    def body(r_ref, e_ref, m_ref, o_ref):
        o_ref[...] = _dot(r_ref[...], e_ref[...], prec=HI) + m_ref[...]

    t = pl.pallas_call(body, name=name, out_shape=jax.ShapeDtypeStruct((128, GRID_W * GRID_W), F32))(
        r2, jnp.asarray(e), jnp.asarray(negmask))
    t = t[:NA_HEADS * nd].reshape(NA_HEADS, nd, GRID_W, GRID_W)
    return jnp.stack([jnp.concatenate([t[:, b + w] for w in range(NA_WIN_R)], axis=-1) for b in range(NA_WIN_R)], axis=1)


def _rpb_reduce(dbias, name):
    e, _ = _na_onehot()
    nd = 2 * NA_WIN_R - 1
    et = np.zeros((GRID_W * GRID_W, 128), np.float32)
    et[:, :32] = e.T
    sel = np.zeros((128, NA_HEADS * NA_WIN_R * NA_WIN_R), np.float32)
    for h in range(NA_HEADS):
        for b in range(NA_WIN_R):
            for w in range(NA_WIN_R):
                sel[h * nd + b + w, (h * NA_WIN_R + b) * NA_WIN_R + w] = 1.0
    x = dbias.reshape(NA_HEADS, NA_WIN_R, GRID_W, NA_WIN_R, GRID_W).transpose(0, 1, 3, 2, 4).reshape(-1, GRID_W * GRID_W)

    def body(x_ref, et_ref, sel_ref, o_ref):
        g = _dot(x_ref[...], et_ref[...], prec=HI)
        o_ref[...] = _dot(sel_ref[...], g, prec=HI)

    out = pl.pallas_call(body, name=name, out_shape=jax.ShapeDtypeStruct((128, 128), F32))(x, jnp.asarray(et), jnp.asarray(sel))
    return out[:NA_HEADS * nd, :2 * NA_WIN_C - 1].reshape(NA_HEADS, nd, 2 * NA_WIN_C - 1)


def _na_base(r, rows):
    return jnp.clip(r - NA_WIN_R // 2, 0, rows - NA_WIN_R) - r + (NA_WIN_R - 1)


def _na_scores(q_ref, k_ref, bias_ref, hh, r0w):
    m = _lane_mask(128, 64 * hh, 64)
    qm = (q_ref[...].astype(F32) * m).astype(BF16)
    kw = k_ref[r0w, :]
    s = _dot(qm, kw, "nt") + bias_ref[hh, 0]
    s = s - jnp.max(s, axis=-1, keepdims=True)
    p = jnp.exp(s)
    p = p / jnp.sum(p, axis=-1, keepdims=True)
    return m, qm, kw, p


def _na_attn(qn, kn, vb, bias, name):
    S = qn.shape[0]
    rows = S // GRID_W
    nk = NA_WIN_R * GRID_W

    def body(q_ref, k_ref, v_ref, b_ref, o_ref):
        r = pl.program_id(1)
        r0w = pl.ds(pl.multiple_of(jnp.clip(r - NA_WIN_R // 2, 0, rows - NA_WIN_R) * GRID_W, GRID_W), nk)
        acc = jnp.zeros((GRID_W, 128), F32)
        for hh in range(2):
            m, _, _, p = _na_scores(q_ref, k_ref, b_ref, hh, r0w)
            acc = acc + _dot(p.astype(BF16), v_ref[r0w, :]) * m
        o_ref[...] = acc.astype(BF16)

    return pl.pallas_call(
        body, name=name, grid=(NA_HEADS // 2, rows),
        in_specs=[pl.BlockSpec((GRID_W, 128), lambda p, r: (r, p)), pl.BlockSpec((S, 128), lambda p, r: (0, p)),
                  pl.BlockSpec((S, 128), lambda p, r: (0, p)),
                  pl.BlockSpec((2, 1, GRID_W, nk), lambda p, r: (p, _na_base(r, rows), 0, 0))],
        out_specs=pl.BlockSpec((GRID_W, 128), lambda p, r: (r, p)),
        out_shape=jax.ShapeDtypeStruct((S, NA_W), BF16), compiler_params=_cparams("parallel", "arbitrary"),
    )(qn, kn, vb, bias)


def _na_attn_bwd(qn, kn, vb, bias, do, name):
    S = qn.shape[0]
    rows = S // GRID_W
    nk = NA_WIN_R * GRID_W

    def body(q_ref, k_ref, v_ref, b_ref, do_ref, dq_ref, dk_ref, dv_ref, db_ref):
        r = pl.program_id(1)

        @pl.when(r == 0)
        def _():
            dk_ref[...] = jnp.zeros_like(dk_ref)
            dv_ref[...] = jnp.zeros_like(dv_ref)

        r0w = pl.ds(pl.multiple_of(jnp.clip(r - NA_WIN_R // 2, 0, rows - NA_WIN_R) * GRID_W, GRID_W), nk)
        fresh = jnp.logical_or(r <= NA_WIN_R // 2, r > rows - NA_WIN_R // 2)
        dq = jnp.zeros((GRID_W, 128), F32)
        for hh in range(2):
            m, qm, kw, p = _na_scores(q_ref, k_ref, b_ref, hh, r0w)
            dom = (do_ref[...].astype(F32) * m).astype(BF16)
            dp = _dot(dom, v_ref[r0w, :], "nt")
            ds = p * (dp - jnp.sum(p * dp, axis=-1, keepdims=True))

            @pl.when(fresh)
            def _():
                db_ref[hh, 0] = ds

            @pl.when(jnp.logical_not(fresh))
            def _():
                db_ref[hh, 0] += ds

            dsb = ds.astype(BF16)
            dq = dq + _dot(dsb, kw) * m
            dk_ref[r0w, :] += _dot(dsb, qm, "tn")
            dv_ref[r0w, :] += _dot(p.astype(BF16), dom, "tn")
        dq_ref[...] = dq

    qblk = pl.BlockSpec((GRID_W, 128), lambda p, r: (r, p))
    full = pl.BlockSpec((S, 128), lambda p, r: (0, p))
    bblk = pl.BlockSpec((2, 1, GRID_W, nk), lambda p, r: (p, _na_base(r, rows), 0, 0))
    return pl.pallas_call(
        body, name=name, grid=(NA_HEADS // 2, rows),
        in_specs=[qblk, full, full, bblk, qblk], out_specs=[qblk, full, full, bblk],
        out_shape=[jax.ShapeDtypeStruct((S, NA_W), F32)] * 3 + [jax.ShapeDtypeStruct((NA_HEADS, NA_WIN_R, GRID_W, nk), F32)],
        compiler_params=_cparams("parallel", "arbitrary"),
    )(qn, kn, vb, bias, do)


def _logsig(x):
    return jnp.minimum(x, 0.0) - jnp.log(1.0 + jnp.exp(-jnp.abs(x)))


def _gla_gates(z, c0, wg, bias, name, tm=512):
    S = z.shape[0]
    tm = min(tm, S)
    zv = _V(z, c0, 128)
    W = 2 * GLA_HEADS * GLA_DK

    def body(z_ref, w_ref, b_ref, o_ref):
        pre = _dot(z_ref[...].astype(BF16), w_ref[...]) + b_ref[...]
        o_ref[...] = _logsig(pre) * (1.0 / GLA_TAU)

    return pl.pallas_call(
        body, name=name, grid=(S // tm,),
        in_specs=[zv.spec(tm, 128, lambda i: i, lambda i: 0), pl.BlockSpec((128, W), lambda i: (0, 0)),
                  pl.BlockSpec((1, W), lambda i: (0, 0))],
        out_specs=pl.BlockSpec((tm, W), lambda i: (i, 0)), out_shape=jax.ShapeDtypeStruct((S, W), F32),
        compiler_params=_cparams("parallel"),
    )(z, wg, bias)


def _gla_gates_bwd(z, c0, wg, bias, dg_f, dg_b, name, tm=512):
    S = z.shape[0]
    tm = min(tm, S)
    zv = _V(z, c0, 128)
    W = 2 * GLA_HEADS * GLA_DK

    def body(z_ref, w_ref, b_ref, dgf_ref, dgb_ref, dp_ref, db_ref):
        pre = _dot(z_ref[...].astype(BF16), w_ref[...]) + b_ref[...]
        dg = jnp.concatenate([dgf_ref[...], dgb_ref[...]], axis=-1)
        dpre = dg * (1.0 / GLA_TAU) * jax.nn.sigmoid(-pre)
        dp_ref[...] = dpre.astype(BF16)
        _acc_rows(db_ref, dpre, pl.program_id(0) == 0)

    half = pl.BlockSpec((tm, W // 2), lambda i: (i, 0))
    return pl.pallas_call(
        body, name=name, grid=(S // tm,),
        in_specs=[zv.spec(tm, 128, lambda i: i, lambda i: 0), pl.BlockSpec((128, W), lambda i: (0, 0)),
                  pl.BlockSpec((1, W), lambda i: (0, 0)), half, half],
        out_specs=[pl.BlockSpec((tm, W), lambda i: (i, 0)), pl.BlockSpec((8, W), lambda i: (0, 0))],
        out_shape=[jax.ShapeDtypeStruct((S, W), BF16), jax.ShapeDtypeStruct((8, W), F32)],
        compiler_params=_cparams("arbitrary"),
    )(z, wg, bias, dg_f, dg_b)


def _gla_chunk_terms(zqk, g, p, rev):
    C = GLA_CHUNK
    i, j = _iota((C, C), 0), _iota((C, C), 1)
    cum = jnp.where((j >= i) if rev else (j <= i), 1.0, 0.0).astype(F32)
    q2 = zqk[:, 128 * p:128 * p + 128] * (GLA_DK ** -0.5)
    k2 = zqk[:, 256 + 128 * p:256 + 128 * p + 128]
    b2 = _dot(cum, g[:, 128 * p:128 * p + 128], prec=HI)
    bl2 = b2[0:1] if rev else b2[C - 1:C]
    eb = jnp.exp(b2)
    qe2 = q2 * eb
    ke2 = k2 * jnp.exp(-b2)
    kend2 = k2 * jnp.exp(bl2 - b2)
    dec2 = jnp.exp(bl2)
    tri = (j > i) if rev else (j <= i)
    return b2, bl2, eb, qe2, ke2, kend2, dec2, tri


def _row_to_col(row):
    eye = _iota((128, 128), 0) == _iota((128, 128), 1)
    return jnp.sum(jnp.where(eye, row, 0.0), axis=1, keepdims=True)


def _col_to_row(col):
    eye = _iota((128, 128), 0) == _iota((128, 128), 1)
    return jnp.sum(jnp.where(eye, col, 0.0), axis=0, keepdims=True)


def _gla_fwd(z, c_qk, c_v, gfb, name):
    S = z.shape[0]
    C = GLA_CHUNK
    n = S // C
    WQK = 2 * GLA_HEADS * GLA_DK
    WV = GLA_HEADS * GLA_DV
    zqk, zvv = _V(z, c_qk, WQK), _V(z, c_v, WV)

    def body(qkf_ref, vf_ref, gf_ref, qkb_ref, vb_ref, gb_ref, of_ref, ob_ref, sf_ref, sb_ref, stf, stb):
        @pl.when(pl.program_id(0) == 0)
        def _():
            stf[...] = jnp.zeros_like(stf)
            stb[...] = jnp.zeros_like(stb)

        for rev, qk_ref, v_ref, g_ref, o_ref, s_ref, st in ((False, qkf_ref, vf_ref, gf_ref, of_ref, sf_ref, stf),
                                                            (True, qkb_ref, vb_ref, gb_ref, ob_ref, sb_ref, stb)):
            zqkv, gv = qk_ref[...], g_ref[...]
            for p in range(GLA_HEADS // 2):
                _, _, _, qe2, ke2, kend2, dec2, tri = _gla_chunk_terms(zqkv, gv, p, rev)
                dec_col = _row_to_col(dec2)
                keb = ke2.astype(BF16)
                for hh in range(2):
                    h = 2 * p + hh
                    m = _lane_mask(128, 64 * hh, 64)
                    qm = (qe2 * m).astype(BF16)
                    a = jnp.where(tri, _dot(qm, keb, "nt"), 0.0)
                    vh = v_ref[:, 128 * h:128 * h + 128].astype(BF16)
                    sp = st[h]
                    o_ref[:, 128 * h:128 * h + 128] = _dot(a.astype(BF16), vh) + _dot(qm, sp.astype(BF16))
                    s_ref[0, h] = sp
                    st[h] = dec_col * sp + _dot((kend2 * m).astype(BF16), vh, "tn")

    fw = lambda i: i
    bw = lambda i: n - 1 - i
    zero = lambda i: 0
    in_specs = []
    for ix, col in ((fw, 0), (bw, 1)):
        in_specs += [zqk.spec(C, WQK, ix, zero), zvv.spec(C, WV, ix, zero),
                     pl.BlockSpec((C, WQK // 2), functools.partial(lambda i, ix, col: (ix(i), col), ix=ix, col=col))]
    return pl.pallas_call(
        body, name=name, grid=(n,), in_specs=in_specs,
        out_specs=[pl.BlockSpec((C, WV), lambda i: (i, 0)), pl.BlockSpec((C, WV), lambda i: (n - 1 - i, 0)),
                   pl.BlockSpec((1, GLA_HEADS, 128, 128), lambda i: (i, 0, 0, 0)),
                   pl.BlockSpec((1, GLA_HEADS, 128, 128), lambda i: (n - 1 - i, 0, 0, 0))],
        out_shape=[jax.ShapeDtypeStruct((S, WV), F32)] * 2 + [jax.ShapeDtypeStruct((n, GLA_HEADS, 128, 128), F32)] * 2,
        scratch_shapes=[pltpu.VMEM((GLA_HEADS, 128, 128), F32)] * 2, compiler_params=_cparams("arbitrary"),
    )(z, z, gfb, z, z, gfb)


def _gla_bwd(z, c_qk, c_v, gfb, do, s_f, s_b, name):
    S = z.shape[0]
    C = GLA_CHUNK
    n = S // C
    WQK = 2 * GLA_HEADS * GLA_DK
    WV = GLA_HEADS * GLA_DV
    zqk, zvv = _V(z, c_qk, WQK), _V(z, c_v, WV)

    def body(qkf_ref, vf_ref, gf_ref, dof_ref, sf_ref, qkb_ref, vb_ref, gb_ref, dob_ref, sb_ref,
             dqkf_ref, dvf_ref, dgf_ref, dqkb_ref, dvb_ref, dgb_ref, dstf, dstb):
        @pl.when(pl.program_id(0) == 0)
        def _():
            dstf[...] = jnp.zeros_like(dstf)
            dstb[...] = jnp.zeros_like(dstb)

        dirs = ((False, qkf_ref, vf_ref, gf_ref, dof_ref, sf_ref, dqkf_ref, dvf_ref, dgf_ref, dstf),
                (True, qkb_ref, vb_ref, gb_ref, dob_ref, sb_ref, dqkb_ref, dvb_ref, dgb_ref, dstb))
        for rev, qk_ref, v_ref, g_ref, do_ref, s_ref, dqk_ref, dv_ref, dg_ref, dst in dirs:
            zqkv, gv = qk_ref[...], g_ref[...]
            i, j = _iota((C, C), 0), _iota((C, C), 1)
            cum_t = jnp.where((j <= i) if rev else (j >= i), 1.0, 0.0).astype(F32)
            edge = _iota((C, 128), 0) == (0 if rev else C - 1)
            for p in range(GLA_HEADS // 2):
                b2, bl2, eb, qe2, ke2, kend2, dec2, tri = _gla_chunk_terms(zqkv, gv, p, rev)
                dec_col = _row_to_col(dec2)
                keb = ke2.astype(BF16)
                dqe2 = jnp.zeros((C, 128), F32)
                dke2 = jnp.zeros((C, 128), F32)
                dkend2 = jnp.zeros((C, 128), F32)
                ddec2 = jnp.zeros((1, 128), F32)
                for hh in range(2):
                    h = 2 * p + hh
                    m = _lane_mask(128, 64 * hh, 64)
                    qm = (qe2 * m).astype(BF16)
                    kem = (ke2 * m).astype(BF16)
                    kendm = (kend2 * m).astype(BF16)
                    a = jnp.where(tri, _dot(qm, keb, "nt"), 0.0).astype(BF16)
                    vh = v_ref[:, 128 * h:128 * h + 128].astype(BF16)
                    doh = do_ref[:, 128 * h:128 * h + 128].astype(BF16)
                    sp = s_ref[0, h]
                    spb = sp.astype(BF16)
                    ds = dst[h]
                    dsb = ds.astype(BF16)
                    da = jnp.where(tri, _dot(doh, vh, "nt"), 0.0).astype(BF16)
                    dqe2 = dqe2 + _dot(da, kem) + _dot(doh, spb, "nt")
                    dke2 = dke2 + _dot(da, qm, "tn")
                    dv_ref[:, 128 * h:128 * h + 128] = _dot(a, doh, "tn") + _dot(kendm, dsb)
                    dkend2 = dkend2 + _dot(vh, dsb, "nt") * m
                    ddec2 = ddec2 + _col_to_row(jnp.sum(ds * sp, axis=1, keepdims=True))
                    dst[h] = dec_col * ds + _dot(qm, doh, "tn")
                dqk_ref[:, 128 * p:128 * p + 128] = dqe2 * eb * (GLA_DK ** -0.5)
                dqk_ref[:, 256 + 128 * p:256 + 128 * p + 128] = dke2 * jnp.exp(-b2) + dkend2 * jnp.exp(bl2 - b2)
                dkk = dkend2 * kend2
                db2 = dqe2 * qe2 - dke2 * ke2 - dkk
                dbl2 = jnp.sum(dkk, axis=0, keepdims=True) + ddec2 * dec2
                db2 = db2 + jnp.where(edge, dbl2, 0.0)
                dg_ref[:, 128 * p:128 * p + 128] = _dot(cum_t, db2, prec=HI)

    fw = lambda i: n - 1 - i
    bw = lambda i: i
    zero = lambda i: 0
    in_specs, out_specs = [], []
    for ix, col in ((fw, 0), (bw, 1)):
        blk = functools.partial(lambda i, ix: (ix(i), 0), ix=ix)
        in_specs += [zqk.spec(C, WQK, ix, zero), zvv.spec(C, WV, ix, zero),
                     pl.BlockSpec((C, WQK // 2), functools.partial(lambda i, ix, col: (ix(i), col), ix=ix, col=col)),
                     pl.BlockSpec((C, WV), blk),
                     pl.BlockSpec((1, GLA_HEADS, 128, 128), functools.partial(lambda i, ix: (ix(i), 0, 0, 0), ix=ix))]
        out_specs += [pl.BlockSpec((C, WQK), blk), pl.BlockSpec((C, WV), blk), pl.BlockSpec((C, WQK // 2), blk)]
    shapes = [jax.ShapeDtypeStruct((S, WQK), F32), jax.ShapeDtypeStruct((S, WV), F32), jax.ShapeDtypeStruct((S, WQK // 2), F32)]
    return pl.pallas_call(
        body, name=name, grid=(n,), in_specs=in_specs, out_specs=out_specs, out_shape=shapes * 2,
        scratch_shapes=[pltpu.VMEM((GLA_HEADS, 128, 128), F32)] * 2, compiler_params=_cparams("arbitrary"),
    )(z, z, gfb, do, s_f, z, z, gfb, do, s_b)


def _gla_post(o_f, o_b, z, c_r, gn, name, tm=512):
    S, WV = o_f.shape
    tm = min(tm, S)
    zr = _V(z, c_r, WV)

    def body(of_ref, ob_ref, r_ref, g_ref, y_ref):
        gr = r_ref[...]
        sil = gr * jax.nn.sigmoid(gr)
        for h in range(GLA_HEADS):
            sl = slice(GLA_DV * h, GLA_DV * (h + 1))
            o = of_ref[:, sl] + ob_ref[:, sl]
            on = o * lax.rsqrt(jnp.mean(o * o, axis=-1, keepdims=True) + EPS) * g_ref[...]
            y_ref[:, sl] = (on * sil[:, sl]).astype(BF16)

    blk = pl.BlockSpec((tm, WV), lambda i: (i, 0))
    return pl.pallas_call(
        body, name=name, grid=(S // tm,),
        in_specs=[blk, blk, zr.spec(tm, WV, lambda i: i, lambda i: 0), pl.BlockSpec((1, GLA_DV), lambda i: (0, 0))],
        out_specs=blk, out_shape=jax.ShapeDtypeStruct((S, WV), BF16), compiler_params=_cparams("parallel"),
    )(o_f, o_b, z, gn)


def _gla_post_bwd(o_f, o_b, z, c_r, gn, dy, name, tm=512):
    S, WV = o_f.shape
    tm = min(tm, S)
    zr = _V(z, c_r, WV)

    def body(of_ref, ob_ref, r_ref, g_ref, dy_ref, do_ref, dr_ref, dg_ref):
        gr = r_ref[...]
        sig = jax.nn.sigmoid(gr)
        sil = gr * sig
        dyv = dy_ref[...].astype(F32)
        dgn = jnp.zeros((tm, GLA_DV), F32)
        for h in range(GLA_HEADS):
            sl = slice(GLA_DV * h, GLA_DV * (h + 1))
            o = of_ref[:, sl] + ob_ref[:, sl]
            rstd = lax.rsqrt(jnp.mean(o * o, axis=-1, keepdims=True) + EPS)
            xhat = o * rstd
            don = dyv[:, sl] * sil[:, sl]
            dr_ref[:, sl] = (dyv[:, sl] * xhat * g_ref[...] * (sig[:, sl] * (1.0 + gr[:, sl] * (1.0 - sig[:, sl])))).astype(BF16)
            dxhat = don * g_ref[...]
            do_ref[:, sl] = rstd * (dxhat - xhat * jnp.mean(dxhat * xhat, axis=-1, keepdims=True))
            dgn = dgn + don * xhat
        _acc_rows(dg_ref, dgn, pl.program_id(0) == 0)

    blk = pl.BlockSpec((tm, WV), lambda i: (i, 0))
    return pl.pallas_call(
        body, name=name, grid=(S // tm,),
        in_specs=[blk, blk, zr.spec(tm, WV, lambda i: i, lambda i: 0), pl.BlockSpec((1, GLA_DV), lambda i: (0, 0)), blk],
        out_specs=[blk, blk, pl.BlockSpec((8, GLA_DV), lambda i: (0, 0))],
        out_shape=[jax.ShapeDtypeStruct((S, WV), F32), jax.ShapeDtypeStruct((S, WV), BF16), jax.ShapeDtypeStruct((8, GLA_DV), F32)],
        compiler_params=_cparams("arbitrary"),
    )(o_f, o_b, z, gn, dy)


def _gla_assemble(dqk_f, dqk_b, dv_f, dv_b, dgr, name, tm=512):
    S = dqk_f.shape[0]
    tm = min(tm, S)

    def body(a_ref, b_ref, c_ref, d_ref, r_ref, o_ref):
        o_ref[:, 0:512] = (a_ref[...] + b_ref[...]).astype(BF16)
        o_ref[:, 512:1024] = (c_ref[...] + d_ref[...]).astype(BF16)
        o_ref[:, 1024:1536] = r_ref[...]

    blk = pl.BlockSpec((tm, 512), lambda i: (i, 0))
    return pl.pallas_call(
        body, name=name, grid=(S // tm,), in_specs=[blk] * 5, out_specs=pl.BlockSpec((tm, 1536), lambda i: (i, 0)),
        out_shape=jax.ShapeDtypeStruct((S, 1536), BF16), compiler_params=_cparams("parallel"),
    )(dqk_f, dqk_b, dv_f, dv_b, dgr)


def _rope(r, cos, sg):
    return r * cos + pltpu.roll(r, 64, 1) * sg


def _unrope(dy, cos, sg):
    return dy * cos + pltpu.roll(dy * sg, 64, 1)


def _mla_prep(z, c_q, c_kr, wuq, wukv, g_cq, g_ckv, g_q, g_k, cos, sg, name, tm=256):
    S = z.shape[0]
    tm = min(tm, S)
    zc, zk = _V(z, c_q, 2 * MLA_RANK), _V(z, c_kr, 128)
    inv = 1.0 / MLA_QK

    def body(zc_ref, zk_ref, wuq_ref, wukv_ref, gcq_ref, gckv_ref, gq_ref, gk_ref, cos_ref, sg_ref,
             q_ref, k_ref, v_ref, cqn_ref, ckvn_ref):
        def norm(xv, gv):
            return (xv * lax.rsqrt(jnp.mean(xv * xv, axis=-1, keepdims=True) + EPS) * gv).astype(BF16)

        cqn = norm(zc_ref[:, 0:MLA_RANK], gcq_ref[...])
        ckvn = norm(zc_ref[:, MLA_RANK:2 * MLA_RANK], gckv_ref[...])
        cqn_ref[...] = cqn
        ckvn_ref[...] = ckvn
        qf = _dot(cqn, wuq_ref[...])
        kv = _dot(ckvn, wukv_ref[...])
        kr = zk_ref[...]
        krss = jnp.sum(kr * kr, axis=-1, keepdims=True)
        cosv, sgv = cos_ref[...], sg_ref[...]
        gq, gk = gq_ref[...], gk_ref[...]
        for h in range(MLA_HEADS):
            qh = qf[:, MLA_SLOT * h:MLA_SLOT * (h + 1)]
            qhn = qh * lax.rsqrt(jnp.sum(qh * qh, axis=-1, keepdims=True) * inv + EPS) * gq
            q_ref[:, MLA_SLOT * h:MLA_SLOT * h + 128] = qhn[:, 0:128].astype(BF16)
            q_ref[:, MLA_SLOT * h + 128:MLA_SLOT * (h + 1)] = _rope(qhn[:, 128:256], cosv, sgv).astype(BF16)
            kn = kv[:, 256 * h:256 * h + 128]
            rstd = lax.rsqrt((jnp.sum(kn * kn, axis=-1, keepdims=True) + krss) * inv + EPS)
            k_ref[:, MLA_SLOT * h:MLA_SLOT * h + 128] = (kn * rstd * gk[:, 0:128]).astype(BF16)
            k_ref[:, MLA_SLOT * h + 128:MLA_SLOT * (h + 1)] = _rope(kr * rstd * gk[:, 128:256], cosv, sgv).astype(BF16)
            v_ref[:, 128 * h:128 * (h + 1)] = kv[:, 256 * h + 128:256 * (h + 1)].astype(BF16)

    row = lambda w: pl.BlockSpec((tm, w), lambda i: (i, 0))
    const = lambda r, w: pl.BlockSpec((r, w), lambda i: (0, 0))
    W = MLA_HEADS * MLA_SLOT
    return pl.pallas_call(
        body, name=name, grid=(S // tm,),
        in_specs=[zc.spec(tm, 2 * MLA_RANK, lambda i: i, lambda i: 0), zk.spec(tm, 128, lambda i: i, lambda i: 0),
                  const(MLA_RANK, W), const(MLA_RANK, W), const(1, MLA_RANK), const(1, MLA_RANK), const(1, MLA_SLOT),
                  const(1, MLA_SLOT), row(128), row(128)],
        out_specs=[row(W), row(W), row(MLA_HEADS * MLA_V), row(MLA_RANK), row(MLA_RANK)],
        out_shape=[jax.ShapeDtypeStruct((S, W), BF16), jax.ShapeDtypeStruct((S, W), BF16),
                   jax.ShapeDtypeStruct((S, MLA_HEADS * MLA_V), BF16), jax.ShapeDtypeStruct((S, MLA_RANK), BF16),
                   jax.ShapeDtypeStruct((S, MLA_RANK), BF16)],
        compiler_params=_cparams("parallel"),
    )(z, z, wuq, wukv, g_cq, g_ckv, g_q, g_k, cos, sg)


def _mla_prep_bwd(z, c_kr, cqn, ckvn, wuq, wukv, g_q, g_k, cos, sg, dq, dk, dv, name, tm=256):
    S = z.shape[0]
    tm = min(tm, S)
    zk = _V(z, c_kr, 128)
    inv = 1.0 / MLA_QK

    def body(zk_ref, cqn_ref, ckvn_ref, wuq_ref, wukv_ref, gq_ref, gk_ref, cos_ref, sg_ref, dq_ref, dk_ref, dv_ref,
             dqf_ref, dkv_ref, dkr_ref, dgq_ref, dgk_ref):
        first = pl.program_id(0) == 0
        qf = _dot(cqn_ref[...], wuq_ref[...])
        kv = _dot(ckvn_ref[...], wukv_ref[...])
        kr = zk_ref[...]
        krss = jnp.sum(kr * kr, axis=-1, keepdims=True)
        cosv, sgv = cos_ref[...], sg_ref[...]
        gq, gk = gq_ref[...], gk_ref[...]
        dkr = jnp.zeros((tm, 128), F32)
        dgq = jnp.zeros((tm, MLA_SLOT), F32)
        dgkn = jnp.zeros((tm, 128), F32)
        dgkr = jnp.zeros((tm, 128), F32)
        for h in range(MLA_HEADS):
            qh = qf[:, MLA_SLOT * h:MLA_SLOT * (h + 1)]
            rstd = lax.rsqrt(jnp.sum(qh * qh, axis=-1, keepdims=True) * inv + EPS)
            xhat = qh * rstd
            dyn = jnp.concatenate([dq_ref[:, MLA_SLOT * h:MLA_SLOT * h + 128],
                                   _unrope(dq_ref[:, MLA_SLOT * h + 128:MLA_SLOT * (h + 1)], cosv, sgv)], axis=-1)
            dxhat = dyn * gq
            dqf_ref[:, MLA_SLOT * h:MLA_SLOT * (h + 1)] = (
                rstd * (dxhat - xhat * (jnp.sum(dxhat * xhat, axis=-1, keepdims=True) * inv))).astype(BF16)
            dgq = dgq + dyn * xhat

            kn = kv[:, 256 * h:256 * h + 128]
            rstd = lax.rsqrt((jnp.sum(kn * kn, axis=-1, keepdims=True) + krss) * inv + EPS)
            xn, xr = kn * rstd, kr * rstd
            dyn_n = dk_ref[:, MLA_SLOT * h:MLA_SLOT * h + 128]
            dyn_r = _unrope(dk_ref[:, MLA_SLOT * h + 128:MLA_SLOT * (h + 1)], cosv, sgv)
            dxn, dxr = dyn_n * gk[:, 0:128], dyn_r * gk[:, 128:256]
            proj = (jnp.sum(dxn * xn, axis=-1, keepdims=True) + jnp.sum(dxr * xr, axis=-1, keepdims=True)) * inv
            dkv_ref[:, 256 * h:256 * h + 128] = (rstd * (dxn - xn * proj)).astype(BF16)
            dkv_ref[:, 256 * h + 128:256 * (h + 1)] = dv_ref[:, 128 * h:128 * (h + 1)].astype(BF16)
            dkr = dkr + rstd * (dxr - xr * proj)
            dgkn = dgkn + dyn_n * xn
            dgkr = dgkr + dyn_r * xr
        dkr_ref[...] = dkr.astype(BF16)
        _acc_rows(dgq_ref, dgq, first)
        _acc_rows(dgk_ref, jnp.concatenate([dgkn, dgkr], axis=-1), first)

    row = lambda w: pl.BlockSpec((tm, w), lambda i: (i, 0))
    const = lambda r, w: pl.BlockSpec((r, w), lambda i: (0, 0))
    W = MLA_HEADS * MLA_SLOT
    return pl.pallas_call(
        body, name=name, grid=(S // tm,),
        in_specs=[zk.spec(tm, 128, lambda i: i, lambda i: 0), row(MLA_RANK), row(MLA_RANK), const(MLA_RANK, W),
                  const(MLA_RANK, W), const(1, MLA_SLOT), const(1, MLA_SLOT), row(128), row(128), row(W), row(W),
                  row(MLA_HEADS * MLA_V)],
        out_specs=[row(W), row(W), row(128), const(8, MLA_SLOT), const(8, MLA_SLOT)],
        out_shape=[jax.ShapeDtypeStruct((S, W), BF16), jax.ShapeDtypeStruct((S, W), BF16), jax.ShapeDtypeStruct((S, 128), BF16),
                   jax.ShapeDtypeStruct((8, MLA_SLOT), F32), jax.ShapeDtypeStruct((8, MLA_SLOT), F32)],
        compiler_params=_cparams("arbitrary"),
    )(z, cqn, ckvn, wuq, wukv, g_q, g_k, cos, sg, dq, dk, dv)


def _softmax_rows(s):
    s = s - jnp.max(s, axis=-1, keepdims=True)
    p = jnp.exp(s)
    return p / jnp.sum(p, axis=-1, keepdims=True)


def _mla_attn(q, k, v, name, tq=256):
    S = q.shape[0]
    tq = min(tq, S)
    scale = MLA_QK ** -0.5

    def body(q_ref, k_ref, v_ref, o_ref):
        p = _softmax_rows(_dot(q_ref[...], k_ref[...], "nt") * scale)
        o_ref[...] = _dot(p.astype(BF16), v_ref[...]).astype(BF16)

    return pl.pallas_call(
        body, name=name, grid=(MLA_HEADS, S // tq),
        in_specs=[pl.BlockSpec((tq, MLA_SLOT), lambda h, i: (i, h)), pl.BlockSpec((S, MLA_SLOT), lambda h, i: (0, h)),
                  pl.BlockSpec((S, MLA_V), lambda h, i: (0, h))],
        out_specs=pl.BlockSpec((tq, MLA_V), lambda h, i: (i, h)),
        out_shape=jax.ShapeDtypeStruct((S, MLA_HEADS * MLA_V), BF16), compiler_params=_cparams("parallel", "parallel"),
    )(q, k, v)


def _mla_attn_bwd(q, k, v, do, name, tq=256):
    S = q.shape[0]
    tq = min(tq, S)
    scale = MLA_QK ** -0.5

    def body(q_ref, k_ref, v_ref, do_ref, dq_ref, dk_ref, dv_ref):
        @pl.when(pl.program_id(1) == 0)
        def _():
            dk_ref[...] = jnp.zeros_like(dk_ref)
            dv_ref[...] = jnp.zeros_like(dv_ref)

        qv, kvv, dov = q_ref[...], k_ref[...], do_ref[...]
        p = _softmax_rows(_dot(qv, kvv, "nt") * scale)
        dp = _dot(dov, v_ref[...], "nt")
        ds = (p * (dp - jnp.sum(p * dp, axis=-1, keepdims=True)) * scale).astype(BF16)
        dq_ref[...] = _dot(ds, kvv)
        dk_ref[...] += _dot(ds, qv, "tn")
        dv_ref[...] += _dot(p.astype(BF16), dov, "tn")

    W = MLA_HEADS * MLA_SLOT
    return pl.pallas_call(
        body, name=name, grid=(MLA_HEADS, S // tq),
        in_specs=[pl.BlockSpec((tq, MLA_SLOT), lambda h, i: (i, h)), pl.BlockSpec((S, MLA_SLOT), lambda h, i: (0, h)),
                  pl.BlockSpec((S, MLA_V), lambda h, i: (0, h)), pl.BlockSpec((tq, MLA_V), lambda h, i: (i, h))],
        out_specs=[pl.BlockSpec((tq, MLA_SLOT), lambda h, i: (i, h)), pl.BlockSpec((S, MLA_SLOT), lambda h, i: (0, h)),
                   pl.BlockSpec((S, MLA_V), lambda h, i: (0, h))],
        out_shape=[jax.ShapeDtypeStruct((S, W), F32), jax.ShapeDtypeStruct((S, W), F32),
                   jax.ShapeDtypeStruct((S, MLA_HEADS * MLA_V), F32)],
        compiler_params=_cparams("parallel", "arbitrary"),
    )(q, k, v, do)


def _merge(ys, ws, z, name, tm=256):
    S = z.shape[0]
    D = ws[0].shape[1]
    tm = min(tm, S)
    zg = _V(z, 0, 3 * D)

    def body(y0, y1, y2, w0, w1, w2, g_ref, m_ref, p0, p1, p2):
        acc = jnp.zeros((tm, D), F32)
        for i, (y_ref, w_ref, p_ref) in enumerate(((y0, w0, p0), (y1, w1, p1), (y2, w2, p2))):
            pv = _dot(y_ref[...], w_ref[...])
            p_ref[...] = pv.astype(BF16)
            acc = acc + jax.nn.sigmoid(g_ref[:, D * i:D * (i + 1)]) * pv
        m_ref[...] = acc.astype(BF16)

    yb = pl.BlockSpec((tm, ys[0].shape[1]), lambda i: (i, 0))
    wb = pl.BlockSpec(ws[0].shape, lambda i: (0, 0))
    ob = pl.BlockSpec((tm, D), lambda i: (i, 0))
    return pl.pallas_call(
        body, name=name, grid=(S // tm,), in_specs=[yb] * 3 + [wb] * 3 + [zg.spec(tm, 3 * D, lambda i: i, lambda i: 0)],
        out_specs=[ob] * 4, out_shape=[jax.ShapeDtypeStruct((S, D), BF16)] * 4, compiler_params=_cparams("parallel"),
    )(*ys, *ws, z)


def _merge_bwd(dmixed, ps, z, name, tm=256):
    S, D = dmixed.shape
    tm = min(tm, S)
    zg = _V(z, 0, 3 * D)

    def body(dm_ref, p0, p1, p2, g_ref, d0, d1, d2, dg_ref):
        dm = dm_ref[...]
        for i, (p_ref, d_ref) in enumerate(((p0, d0), (p1, d1), (p2, d2))):
            gt = jax.nn.sigmoid(g_ref[:, D * i:D * (i + 1)])
            d_ref[...] = (dm * gt).astype(BF16)
            dg_ref[:, D * i:D * (i + 1)] = (dm * p_ref[...].astype(F32) * gt * (1.0 - gt)).astype(BF16)

    ob = pl.BlockSpec((tm, D), lambda i: (i, 0))
    return pl.pallas_call(
        body, name=name, grid=(S // tm,), in_specs=[ob] * 4 + [zg.spec(tm, 3 * D, lambda i: i, lambda i: 0)],
        out_specs=[ob] * 3 + [pl.BlockSpec((tm, 3 * D), lambda i: (i, 0))],
        out_shape=[jax.ShapeDtypeStruct((S, D), BF16)] * 3 + [jax.ShapeDtypeStruct((S, 3 * D), BF16)],
        compiler_params=_cparams("parallel"),
    )(dmixed, *ps, z)


def _loss_head(y, target, name, tm=512):
    S, D = y.shape
    tm = min(tm, S)

    def body(y_ref, t_ref, dy_ref, l_ref):
        e = y_ref[...] - t_ref[...]
        dy_ref[...] = e * (1.0 / D)
        sq = e * e
        part = jnp.sum(sq.reshape(tm // 8, 8, D), axis=0)
        part = jnp.sum(part.reshape(8, D // 128, 128), axis=1) * (0.5 / D)

        @pl.when(pl.program_id(0) == 0)
        def _():
            l_ref[...] = part

        @pl.when(pl.program_id(0) != 0)
        def _():
            l_ref[...] += part

    blk = pl.BlockSpec((tm, D), lambda i: (i, 0))
    return pl.pallas_call(
        body, name=name, grid=(S // tm,), in_specs=[blk, blk], out_specs=[blk, pl.BlockSpec((8, 128), lambda i: (0, 0))],
        out_shape=[jax.ShapeDtypeStruct((S, D), F32), jax.ShapeDtypeStruct((8, 128), F32)],
        compiler_params=_cparams("arbitrary"),
    )(y, target)


def _fold(parts, name, fold=None):
    L, _, W = parts.shape
    assert L <= 8

    def body(*refs):
        p_ref, o_ref = refs[0], refs[-1]
        rows = [jnp.sum(p_ref[l], axis=0, keepdims=True) for l in range(L)]
        rows += [jnp.zeros((1, W), F32)] * (8 - L)
        sums = jnp.concatenate(rows, axis=0)
        o_ref[...] = sums if fold is None else _dot(sums, refs[1][...], prec=HI)

    args = (parts,) if fold is None else (parts, jnp.asarray(fold))
    wout = W if fold is None else 128
    return pl.pallas_call(body, name=name, out_shape=jax.ShapeDtypeStruct((8, wout), F32))(*args)[:L]


def _adamw(w, g, m, v, name, q=None):
    R, C = w.shape
    tr = R
    for cand in (512, 256, 128, 64, 32, 16, 8):
        if R % cand == 0 and cand * C * 4 <= 2 * 2**20:
            tr = cand
            break

    def body(*refs):
        if q is None:
            w_ref, g_ref, m_ref, v_ref, d_ref, nm_ref, nv_ref = refs
            gv = g_ref[...]
        else:
            w_ref, g_ref, q_ref, m_ref, v_ref, go_ref, d_ref, nm_ref, nv_ref = refs
            gv = g_ref[...] + q_ref[...]
            go_ref[...] = gv
        mn = ADAM_B1 * m_ref[...] + (1.0 - ADAM_B1) * gv
        vn = ADAM_B2 * v_ref[...] + (1.0 - ADAM_B2) * (gv * gv)
        nm_ref[...] = mn
        nv_ref[...] = vn
        m_hat = mn / (1.0 - ADAM_B1 ** ADAM_STEP)
        v_hat = vn / (1.0 - ADAM_B2 ** ADAM_STEP)
        d_ref[...] = -ADAM_LR * (m_hat / (jnp.sqrt(v_hat) + ADAM_EPS) + ADAM_WD * w_ref[...])

    blk = pl.BlockSpec((tr, C), lambda i: (i, 0))
    args = (w, g, m, v) if q is None else (w, g, q, m, v)
    nout = 3 if q is None else 4
    return pl.pallas_call(
        body, name=name, grid=(R // tr,), in_specs=[blk] * len(args), out_specs=[blk] * nout,
        out_shape=[jax.ShapeDtypeStruct((R, C), F32)] * nout, compiler_params=_cparams("parallel"),
    )(*args)


_FLIPS = ((1, 0), (0, 1), (1, 1))


def _chip_exchange(srcs, name, gather):
    n = len(srcs)
    shapes = [s.shape if gather else s.shape[1:] for s in srcs]

    def body(*refs):
        src_refs, dst_refs = refs[:n], refs[n:2 * n]
        send_sems, recv_sems, local_sems = refs[2 * n:]
        x, y, c = lax.axis_index("x"), lax.axis_index("y"), lax.axis_index("c")
        me = 2 * x + y
        pending = []
        for t in range(n):
            own = pltpu.make_async_copy(src_refs[t] if gather else src_refs[t].at[me], dst_refs[t].at[me], local_sems.at[t])
            own.start()
            pending.append(own)
        sent = []
        for r, (fx, fy) in enumerate(_FLIPS):
            px, py = (1 - x) if fx else x, (1 - y) if fy else y
            peer = 2 * px + py
            for t in range(n):
                def copy(block_src, block_dst, r=r, t=t, px=px, py=py):
                    return pltpu.make_async_remote_copy(
                        src_ref=src_refs[t] if gather else src_refs[t].at[block_src], dst_ref=dst_refs[t].at[block_dst],
                        send_sem=send_sems.at[r, t], recv_sem=recv_sems.at[r, t], device_id=(px, py, c), device_id_type=MESH)

                cp = copy(peer, me)
                cp.start()
                sent.append((cp, copy(me, peer)))
        for cp, arrival in sent:
            cp.wait_send()
            arrival.wait_recv()
        for own in pending:
            own.wait()

    return pl.pallas_call(
        body, name=name, in_specs=[_ANY] * n, out_specs=[_ANY] * n,
        out_shape=[jax.ShapeDtypeStruct((4,) + tuple(sh), s.dtype) for sh, s in zip(shapes, srcs)],
        scratch_shapes=[pltpu.SemaphoreType.DMA((3, n)), pltpu.SemaphoreType.DMA((3, n)), pltpu.SemaphoreType.DMA((n,))],
    )(*srcs)


def _sibling_exchange(srcs, name):
    n = len(srcs)

    def body(*refs):
        src_refs, dst_refs = refs[:n], refs[n:2 * n]
        send_sems, recv_sems = refs[2 * n:]
        x, y, c = lax.axis_index("x"), lax.axis_index("y"), lax.axis_index("c")
        copies = [pltpu.make_async_remote_copy(src_ref=src_refs[t], dst_ref=dst_refs[t], send_sem=send_sems.at[t],
                                               recv_sem=recv_sems.at[t], device_id=(x, y, 1 - c), device_id_type=MESH)
                  for t in range(n)]
        for cp in copies:
            cp.start()
        for cp in copies:
            cp.wait()

    return pl.pallas_call(
        body, name=name, in_specs=[_ANY] * n, out_specs=[_ANY] * n,
        out_shape=[jax.ShapeDtypeStruct(s.shape, s.dtype) for s in srcs],
        scratch_shapes=[pltpu.SemaphoreType.DMA((n,)), pltpu.SemaphoreType.DMA((n,))],
    )(*srcs)


def _allreduce_small(v, name):
    R = v.shape[0]

    def body(v_ref, o_ref, slots, send_sems, recv_sems):
        x, y, c = lax.axis_index("x"), lax.axis_index("y"), lax.axis_index("c")
        me = 4 * x + 2 * y + c
        slots[me] = v_ref[...]
        sent = []
        for r in range(1, 8):
            fx, fy, fc = (r >> 2) & 1, (r >> 1) & 1, r & 1
            px, py, pc = (1 - x) if fx else x, (1 - y) if fy else y, (1 - c) if fc else c
            peer = 4 * px + 2 * py + pc

            def copy(slot, r=r, px=px, py=py, pc=pc):
                return pltpu.make_async_remote_copy(
                    src_ref=v_ref, dst_ref=slots.at[slot], send_sem=send_sems.at[r - 1], recv_sem=recv_sems.at[r - 1],
                    device_id=(px, py, pc), device_id_type=MESH)

            cp = copy(me)
            cp.start()
            sent.append((cp, copy(peer)))
        for cp, arrival in sent:
            cp.wait_send()
            arrival.wait_recv()
        acc = slots[0]
        for k in range(1, 8):
            acc = acc + slots[k]
        o_ref[...] = acc

    vm = pl.BlockSpec(memory_space=pltpu.VMEM)
    return pl.pallas_call(
        body, name=name, in_specs=[vm], out_specs=vm, out_shape=jax.ShapeDtypeStruct((R, 128), F32),
        scratch_shapes=[pltpu.VMEM((8, R, 128), F32), pltpu.SemaphoreType.DMA((7,)), pltpu.SemaphoreType.DMA((7,))],
    )(v)


def _sum4(recv, name, tr=512):
    _, R, W = recv.shape
    tr = _tile(R, tr)
    assert R % tr == 0

    def body(r_ref, o_ref):
        o_ref[...] = ((r_ref[0].astype(F32) + r_ref[1].astype(F32)) + r_ref[2].astype(F32)) + r_ref[3].astype(F32)

    return pl.pallas_call(
        body, name=name, grid=(R // tr,), in_specs=[pl.BlockSpec((4, tr, W), lambda i: (0, i, 0))],
        out_specs=pl.BlockSpec((tr, W), lambda i: (i, 0)), out_shape=jax.ShapeDtypeStruct((R, W), F32),
        compiler_params=_cparams("parallel"),
    )(recv)


W_NAMES = ("ffn1_norm", "ffn1_w1", "ffn1_w3", "ffn1_w2", "mix_norm", "w_in", "na_q_norm", "na_k_norm", "na_rpb",
           "gla_gf_up", "gla_gf_bias", "gla_gb_up", "gla_gb_bias", "gla_out_norm", "mla_cq_norm", "mla_ckv_norm",
           "mla_w_uq", "mla_w_ukv", "mla_q_norm", "mla_k_norm", "w_br_na", "w_br_gla", "w_br_mla", "w_out",
           "ffn2_norm", "ffn2_w1", "ffn2_w3", "ffn2_w2")
SHARDED = {"ffn1_w1": 2, "ffn1_w3": 2, "ffn1_w2": 1, "w_in": 2, "gla_gf_up": 2, "gla_gb_up": 2, "mla_w_uq": 2,
           "mla_w_ukv": 2, "w_br_na": 2, "w_br_gla": 2, "w_br_mla": 2, "w_out": 1, "ffn2_w1": 2, "ffn2_w3": 2,
           "ffn2_w2": 1}
REPLICATED = tuple(n for n in W_NAMES if n not in SHARDED)
FFN_W = ("ffn1_w1", "ffn1_w3", "ffn1_w2", "ffn2_w1", "ffn2_w3", "ffn2_w2")


def _win_layout(w, D):
    z = lambda n: jnp.zeros(w.shape[:-1] + (n,), w.dtype)
    return jnp.concatenate([w[..., O_GATES:], w[..., :O_GFL], w[..., O_CQ:O_KR], w[..., O_GFL:O_CQ], z(96),
                            w[..., O_KR:O_KR + 32], z(32), w[..., O_KR + 32:O_KR + 64], z(32)], axis=-1)


def _win_unlayout(dw, D):
    g = 3 * D
    return jnp.concatenate([dw[..., g:g + O_GFL], dw[..., g + 3584:g + 3616], dw[..., g + 3072:g + 3584],
                            dw[..., g + 3712:g + 3744], dw[..., g + 3776:g + 3808], dw[..., :g]], axis=-1)


def _uq_layout(w):
    s = w.shape[:-1]
    w = w.reshape(s + (MLA_HEADS, MLA_QK))
    z = jnp.zeros(s + (MLA_HEADS, 32), w.dtype)
    return jnp.concatenate([w[..., :160], z, w[..., 160:], z], axis=-1).reshape(s + (MLA_HEADS * MLA_SLOT,))


def _uq_unlayout(dw):
    s = dw.shape[:-1]
    dw = dw.reshape(s + (MLA_HEADS, MLA_SLOT))
    return jnp.concatenate([dw[..., :160], dw[..., 192:224]], axis=-1).reshape(s + (MLA_HEADS * MLA_QK,))


def _slot_layout(g):
    z = jnp.zeros(g.shape[:-1] + (32,), g.dtype)
    return jnp.concatenate([g[..., :160], z, g[..., 160:], z], axis=-1)


def _slot_unlayout(g):
    return jnp.concatenate([g[..., :160], g[..., 192:224]], axis=-1)


def _layer_fwd(x, w, cos, sg):
    D = x.shape[1]
    NA, GL, ML, LR, KR = 3 * D, 3 * D + 1536, 3 * D + 3072, 3 * D + 3584, 3 * D + 3712
    x1, f1 = _ffn_fwd(x, w["ffn1_norm"], w["ffn1_w1"], w["ffn1_w3"], w["ffn1_w2"], "ffn1")
    h = _rms_fwd(x1, w["mix_norm"], "mix_rms")
    nz = w["w_in"].shape[1]
    z = _mm([(h, w["w_in"])], "nn", F32, "w_in", tm=512, tn=_tile(nz, 1280))
    qn, kn, vb = _na_prep(z, NA, w["na_gq"], w["na_gk"], "na_prep")
    bias = _rpb_expand(w["na_rpb"], "rpb_expand")
    y_na = _na_attn(qn, kn, vb, bias, "na_attn")
    gfb = _gla_gates(z, LR, w["gla_wg"], w["gla_gbias"], "gla_gates")
    o_f, o_b, s_f, s_b = _gla_fwd(z, GL, GL + 512, gfb, "gla_fwd")
    y_gla = _gla_post(o_f, o_b, z, GL + 1024, w["gla_out_norm"], "gla_post")
    q, k, v, cqn, ckvn = _mla_prep(z, ML, KR, w["mla_wuq"], w["mla_w_ukv"], w["mla_cq_norm"], w["mla_ckv_norm"],
                                   w["mla_gq"], w["mla_gk"], cos, sg, "mla_prep")
    y_mla = _mla_attn(q, k, v, "mla_attn")
    mixed, p0, p1, p2 = _merge([y_na, y_gla, y_mla], [w["w_br_na"], w["w_br_gla"], w["w_br_mla"]], z, "merge")
    x2 = _mm([(mixed, w["w_out"])], "nn", F32, "w_out", tm=512, tn=1024, res=x1)
    x3, f2 = _ffn_fwd(x2, w["ffn2_norm"], w["ffn2_w1"], w["ffn2_w3"], w["ffn2_w2"], "ffn2")
    saved = dict(x=x, x1=x1, x2=x2, f1=f1, f2=f2, h=h, z=z, qn=qn, kn=kn, vb=vb, bias=bias, y_na=y_na, gfb=gfb, o_f=o_f,
                 o_b=o_b, s_f=s_f, s_b=s_b, y_gla=y_gla, q=q, k=k, v=v, cqn=cqn, ckvn=ckvn, y_mla=y_mla, mixed=mixed,
                 p0=p0, p1=p1, p2=p2)
    return x3, saved


def _layer_bwd(dx3, w, sv, cos, sg, bufs, layer):
    D = dx3.shape[1]
    NA, GL, ML, LR, KR = 3 * D, 3 * D + 1536, 3 * D + 3072, 3 * D + 3584, 3 * D + 3712
    z = sv["z"]
    g = {}
    dx2, g["ffn2_norm"], (g["ffn2_w1"], g["ffn2_w3"], g["ffn2_w2"]) = _ffn_bwd(
        dx3, sv["x2"], w["ffn2_norm"], w["ffn2_w1"], w["ffn2_w3"], w["ffn2_w2"], sv["f2"], "ffn2",
        (bufs["ffn2_w1"], bufs["ffn2_w3"], bufs["ffn2_w2"]), layer)
    dmixed = _mm([(dx2, w["w_out"])], "nt", F32, "w_out_dx", tm=512, tn=512)
    g["w_out"] = _mm([(sv["mixed"], dx2)], "tn", F32, "w_out_dw", tm=D, tn=256)
    d0, d1, d2, dgates = _merge_bwd(dmixed, [sv["p0"], sv["p1"], sv["p2"]], z, "merge_bwd")
    dys = []
    for d, y, nm, dt in ((d0, sv["y_na"], "w_br_na", BF16), (d1, sv["y_gla"], "w_br_gla", F32), (d2, sv["y_mla"], "w_br_mla", BF16)):
        dys.append(_mm([(d, w[nm])], "nt", dt, nm + "_dy", tm=512, tn=512))
        g[nm] = _mm([(y, d)], "tn", F32, nm + "_dw", tm=512, tn=512)
    dqn, dkn, dvn, dbias = _na_attn_bwd(sv["qn"], sv["kn"], sv["vb"], sv["bias"], dys[0], "na_attn_bwd")
    dz_na, g["na_gq"], g["na_gk"] = _na_prep_bwd(z, NA, w["na_gq"], w["na_gk"], dqn, dkn, dvn, "na_prep_bwd")
    g["na_rpb"] = _rpb_reduce(dbias, "rpb_reduce")
    do, dgr, g["gla_out_norm"] = _gla_post_bwd(sv["o_f"], sv["o_b"], z, GL + 1024, w["gla_out_norm"], dys[1], "gla_post_bwd")
    dqk_f, dv_f, dg_f, dqk_b, dv_b, dg_b = _gla_bwd(z, GL, GL + 512, sv["gfb"], do, sv["s_f"], sv["s_b"], "gla_bwd")
    dz_gla = _gla_assemble(dqk_f, dqk_b, dv_f, dv_b, dgr, "gla_assemble")
    dpre, g["gla_gbias"] = _gla_gates_bwd(z, LR, w["gla_wg"], w["gla_gbias"], dg_f, dg_b, "gla_gates_bwd")
    g["gla_wg"] = _mm([(_V(z, LR, 128), dpre)], "tn", F32, "gla_wg_dw", tm=128, tn=512)
    dz_lr = _mm([(dpre, w["gla_wg"])], "nt", BF16, "gla_wg_dz", tm=512, tn=128)
    dq, dk, dv = _mla_attn_bwd(sv["q"], sv["k"], sv["v"], dys[2], "mla_attn_bwd")
    dqf, dkv, dz_kr, g["mla_gq"], g["mla_gk"] = _mla_prep_bwd(
        z, KR, sv["cqn"], sv["ckvn"], w["mla_wuq"], w["mla_w_ukv"], w["mla_gq"], w["mla_gk"], cos, sg, dq, dk, dv, "mla_prep_bwd")
    g["mla_wuq"] = _mm([(sv["cqn"], dqf)], "tn", F32, "mla_wuq_dw", tm=256, tn=512)
    g["mla_w_ukv"] = _mm([(sv["ckvn"], dkv)], "tn", F32, "mla_wukv_dw", tm=256, tn=512)
    dcqn = _mm([(dqf, w["mla_wuq"])], "nt", F32, "mla_wuq_dx", tm=512, tn=256)
    dckvn = _mm([(dkv, w["mla_w_ukv"])], "nt", F32, "mla_wukv_dx", tm=512, tn=256)
    dz_cq, dg_cq = _rms_bwd(_V(z, ML, MLA_RANK), w["mla_cq_norm"], dcqn, "mla_cq_rms_bwd", out_dtype=BF16)
    dz_ckv, dg_ckv = _rms_bwd(_V(z, ML + MLA_RANK, MLA_RANK), w["mla_ckv_norm"], dckvn, "mla_ckv_rms_bwd", out_dtype=BF16)
    g["mla_cq_norm"], g["mla_ckv_norm"] = dg_cq[0:1], dg_ckv[0:1]
    segs = ((dgates, 0, 3 * D), (dz_na, NA, 1536), (dz_gla, GL, 1536), (dz_cq, ML, MLA_RANK), (dz_ckv, ML + MLA_RANK, MLA_RANK),
            (dz_lr, LR, 128), (dz_kr, KR, 128))
    dh = _mm([(dz, _V(w["w_in"], c0, wd)) for dz, c0, wd in segs], "nt", F32, "w_in_dx", tm=512, tn=512)
    g["w_in"] = jnp.concatenate(
        [_mm([(sv["h"], dz)], "tn", F32, f"w_in_dw{i}", tm=D, tn=_tile(wd, 256)) for i, (dz, _, wd) in enumerate(segs)], axis=1)
    dx1, dg_mix = _rms_bwd(sv["x1"], w["mix_norm"], dh, "mix_rms_bwd", dres=dx2)
    g["mix_norm"] = dg_mix[0:1]
    dx, g["ffn1_norm"], (g["ffn1_w1"], g["ffn1_w3"], g["ffn1_w2"]) = _ffn_bwd(
        dx1, sv["x"], w["ffn1_norm"], w["ffn1_w1"], w["ffn1_w3"], w["ffn1_w2"], sv["f1"], "ffn1",
        (bufs["ffn1_w1"], bufs["ffn1_w3"], bufs["ffn1_w2"]), layer)
    return dx, g


def _head_fold(width, period, lo=0):
    f = np.zeros((width, 128), np.float32)
    f[np.arange(width), lo + np.arange(width) % period] = 1.0
    return f


def kernel(x, ffn1_norm, ffn1_w1, ffn1_w3, ffn1_w2, mix_norm, w_in, na_q_norm, na_k_norm, na_rpb, gla_gf_up, gla_gf_bias,
           gla_gb_up, gla_gb_bias, gla_out_norm, mla_cq_norm, mla_ckv_norm, mla_w_uq, mla_w_ukv, mla_q_norm, mla_k_norm,
           w_br_na, w_br_gla, w_br_mla, w_out, ffn2_norm, ffn2_w1, ffn2_w3, ffn2_w2, loss_target, m_ffn1_norm, m_ffn1_w1,
           m_ffn1_w3, m_ffn1_w2, m_mix_norm, m_w_in, m_na_q_norm, m_na_k_norm, m_na_rpb, m_gla_gf_up, m_gla_gf_bias,
           m_gla_gb_up, m_gla_gb_bias, m_gla_out_norm, m_mla_cq_norm, m_mla_ckv_norm, m_mla_w_uq, m_mla_w_ukv,
           m_mla_q_norm, m_mla_k_norm, m_w_br_na, m_w_br_gla, m_w_br_mla, m_w_out, m_ffn2_norm, m_ffn2_w1, m_ffn2_w3,
           m_ffn2_w2, v_ffn1_norm, v_ffn1_w1, v_ffn1_w3, v_ffn1_w2, v_mix_norm, v_w_in, v_na_q_norm, v_na_k_norm,
           v_na_rpb, v_gla_gf_up, v_gla_gf_bias, v_gla_gb_up, v_gla_gb_bias, v_gla_out_norm, v_mla_cq_norm,
           v_mla_ckv_norm, v_mla_w_uq, v_mla_w_ukv, v_mla_q_norm, v_mla_k_norm, v_w_br_na, v_w_br_gla, v_w_br_mla,
           v_w_out, v_ffn2_norm, v_ffn2_w1, v_ffn2_w3, v_ffn2_w2):
    given = dict(locals())
    wts = {n: given[n] for n in W_NAMES}
    mom = {n: given["m_" + n] for n in W_NAMES}
    var = {n: given["v_" + n] for n in W_NAMES}
    xs, target = x[0], loss_target[0]
    S, D = xs.shape
    L = ffn1_norm.shape[0]

    sh_names = tuple(SHARDED)
    gathered = [dict(zip(sh_names, _chip_exchange([wts[n][l].astype(BF16) for n in sh_names], "weights_all_gather", True)))
                for l in range(L)]

    cols = lambda p: jnp.concatenate([p[j] for j in range(4)], axis=-1)
    layers = []
    for l in range(L):
        gl = gathered[l]
        r1 = lambda a, l=l: a[l][None]
        wg = jnp.zeros((128, 2 * GLA_HEADS * GLA_DK), BF16)
        wg = wg.at[0:GLA_RANK, 0:256].set(cols(gl["gla_gf_up"])).at[GLA_RANK:2 * GLA_RANK, 256:512].set(cols(gl["gla_gb_up"]))
        layers.append(dict(
            ffn1_norm=r1(ffn1_norm), ffn1_w1=gl["ffn1_w1"], ffn1_w3=gl["ffn1_w3"], ffn1_w2=gl["ffn1_w2"],
            mix_norm=r1(mix_norm), w_in=_win_layout(cols(gl["w_in"]), D),
            na_gq=jnp.tile(na_q_norm[l], NA_HEADS)[None], na_gk=jnp.tile(na_k_norm[l], NA_HEADS)[None], na_rpb=na_rpb[l],
            gla_wg=wg, gla_gbias=jnp.concatenate([gla_gf_bias[l], gla_gb_bias[l]])[None], gla_out_norm=r1(gla_out_norm),
            mla_cq_norm=r1(mla_cq_norm), mla_ckv_norm=r1(mla_ckv_norm), mla_wuq=_uq_layout(cols(gl["mla_w_uq"])),
            mla_w_ukv=cols(gl["mla_w_ukv"]), mla_gq=_slot_layout(mla_q_norm[l])[None], mla_gk=_slot_layout(mla_k_norm[l])[None],
            w_br_na=cols(gl["w_br_na"]), w_br_gla=cols(gl["w_br_gla"]), w_br_mla=cols(gl["w_br_mla"]),
            w_out=gl["w_out"].reshape(D, D),
            ffn2_norm=r1(ffn2_norm), ffn2_w1=gl["ffn2_w1"], ffn2_w3=gl["ffn2_w3"], ffn2_w2=gl["ffn2_w2"]))
    half = MLA_ROPE // 2
    inv = ROPE_THETA ** (-jnp.arange(half, dtype=F32) / half)
    ang = jnp.arange(S, dtype=F32)[:, None] * inv[None, :]
    cos = jnp.tile(jnp.cos(ang), (1, 4))
    sg = jnp.concatenate([-jnp.sin(ang), -jnp.sin(ang), jnp.sin(ang), jnp.sin(ang)], axis=1)

    xc, saved = xs, []
    for l in range(L):
        xc, sv = _layer_fwd(xc, layers[l], cos, sg)
        saved.append(sv)
    dy, loss_part = _loss_head(xc, target, "loss_head")
    bufs = {n: lax.empty((4, L) + tuple(gathered[0][n].shape[1:]), BF16) for n in FFN_W}
    dx, g = dy, [None] * L
    for l in reversed(range(L)):
        dx, g[l] = _layer_bwd(dx, layers[l], saved[l], cos, sg, bufs, l)
        bufs = {n: g[l][n] for n in FFN_W}

    def contrib(fn, axis):
        def split(a):
            n = a.shape[axis] // 4
            return jnp.stack([lax.slice_in_dim(a, j * n, (j + 1) * n, axis=axis) for j in range(4)])
        return jnp.stack([split(fn(g[l])) for l in range(L)], axis=1).astype(BF16)

    cb = dict(bufs)
    cb["w_in"] = contrib(lambda gl: _win_unlayout(gl["w_in"], D), 1)
    cb["mla_w_uq"] = contrib(lambda gl: _uq_unlayout(gl["mla_wuq"]), 1)
    cb["gla_gf_up"] = contrib(lambda gl: gl["gla_wg"][0:GLA_RANK, 0:256], 1)
    cb["gla_gb_up"] = contrib(lambda gl: gl["gla_wg"][GLA_RANK:2 * GLA_RANK, 256:512], 1)
    for n in ("mla_w_ukv", "w_br_na", "w_br_gla", "w_br_mla", "w_out"):
        cb[n] = contrib(lambda gl, n=n: gl[n], SHARDED[n] - 1)
    stk = lambda n: jnp.stack([g[l][n] for l in range(L)])
    gs = {n: stk(n)[:, 0] for n in ("ffn1_norm", "mix_norm", "mla_cq_norm", "mla_ckv_norm", "ffn2_norm")}
    gs["na_q_norm"] = _fold(stk("na_gq"), "na_gq_fold", _head_fold(NA_W, NA_DH))[:, :NA_DH]
    gs["na_k_norm"] = _fold(stk("na_gk"), "na_gk_fold", _head_fold(NA_W, NA_DH))[:, :NA_DH]
    gs["na_rpb"] = stk("na_rpb")
    gbias = _fold(stk("gla_gbias"), "gla_gbias_fold")
    gs["gla_gf_bias"], gs["gla_gb_bias"] = gbias[:, :256], gbias[:, 256:]
    gs["gla_out_norm"] = _fold(stk("gla_out_norm"), "gla_out_norm_fold")
    gs["mla_q_norm"] = _slot_unlayout(_fold(stk("mla_gq"), "mla_gq_fold"))
    gs["mla_k_norm"] = _slot_unlayout(_fold(stk("mla_gk"), "mla_gk_fold"))

    recv = _chip_exchange([cb[n] for n in sh_names], "grads_chip_exchange", False)
    mine = [_sum4(r.reshape(4, -1, r.shape[-1]), "grads_chip_sum_" + n) for n, r in zip(sh_names, recv)]
    other = _sibling_exchange(mine, "grads_sibling_exchange")
    gsh = {}

    small_shapes = [wts[n].shape[1:] for n in REPLICATED]
    n_small = sum(int(np.prod(s)) for s in small_shapes) * L
    flat = jnp.concatenate([gs[n].reshape(-1) for n in REPLICATED] + [loss_part.reshape(-1)])
    pad = -flat.shape[0] % 1024
    red = _allreduce_small(jnp.pad(flat, (0, pad)).reshape(-1, 128), "small_all_reduce").reshape(-1)
    loss = jnp.sum(red[n_small:n_small + 1024])
    off = 0
    for n, s in zip(REPLICATED, small_shapes):
        cnt = int(np.prod(s)) * L
        gsh[n] = red[off:off + cnt].reshape((L,) + tuple(s))
        off += cnt

    as2d = lambda a: a.reshape(-1, a.shape[-1])
    upd = {}
    for n, p, q in zip(sh_names, mine, other):
        outs = [o.reshape(wts[n].shape) for o in _adamw(as2d(wts[n]), p, as2d(mom[n]), as2d(var[n]), "adamw_" + n, q=q)]
        gsh[n], upd[n] = outs[0], outs[1:]
    pk = lambda d: jnp.pad(jnp.concatenate([d[n].reshape(-1) for n in REPLICATED]), (0, -n_small % 1024)).reshape(-1, 128)
    small = _adamw(pk(wts), pk(gsh), pk(mom), pk(var), "adamw_replicated")
    off = 0
    for n, s in zip(REPLICATED, small_shapes):
        cnt = int(np.prod(s)) * L
        upd[n] = [o.reshape(-1)[off:off + cnt].reshape((L,) + tuple(s)) for o in small]
        off += cnt

    return (loss, dx[None], *[gsh[n] for n in W_NAMES], *[upd[n][0] for n in W_NAMES], *[upd[n][1] for n in W_NAMES],
            *[upd[n][2] for n in W_NAMES])
```

```python
import functools
import math

import numpy as np
import jax
import jax.numpy as jnp
from jax import lax
from jax.experimental import pallas as pl
from jax.experimental.pallas import tpu as pltpu

F32 = jnp.float32
BF16 = jnp.bfloat16
HI = lax.Precision.HIGHEST
MESH = pl.DeviceIdType.MESH

EPS = 1e-6
GRID_W = 64
NA_HEADS, NA_DH, NA_WIN_R, NA_WIN_C = 8, 64, 8, 16
NA_W = NA_HEADS * NA_DH
GLA_HEADS, GLA_DK, GLA_DV, GLA_RANK, GLA_TAU, GLA_CHUNK = 4, 64, 128, 16, 16.0, 64
MLA_HEADS, MLA_RANK, MLA_NOPE, MLA_ROPE, MLA_V = 4, 256, 128, 64, 128
MLA_QK = MLA_NOPE + MLA_ROPE
MLA_SLOT = 256
ROPE_THETA = 10000.0
ADAM_LR, ADAM_B1, ADAM_B2, ADAM_EPS, ADAM_WD, ADAM_STEP = 0.001, 0.9, 0.999, 1e-08, 0.01, 10

V7X_VMEM_BYTES = 64 * 2**20
VMEM_LIMIT = V7X_VMEM_BYTES - 12 * 2**20
NEG = -1e30

O_GQ, O_GFL, O_CQ, O_KR, O_GATES = 1536, 3072, 3104, 3616, 3680


_ANY = pl.BlockSpec(memory_space=pl.ANY)


def _cparams(*sem):
    return pltpu.CompilerParams(dimension_semantics=sem, vmem_limit_bytes=VMEM_LIMIT)


class _V:
    def __init__(self, arr, c0=0, w=None, lead=()):
        self.arr, self.c0, self.lead = arr, c0, tuple(lead)
        assert arr.ndim == 2 + len(self.lead), (arr.shape, lead)
        self.w = arr.shape[-1] if w is None else w

    @property
    def rows(self):
        return self.arr.shape[-2]

    def spec(self, br, bc, rfn, cfn):
        assert self.c0 % bc == 0 and self.w % bc == 0, (self.c0, self.w, bc)
        off, lead = self.c0 // bc, self.lead

        def index(*g):
            return tuple(g[0] if e == "b" else e for e in lead) + (rfn(*g), off + cfn(*g))

        return pl.BlockSpec((None,) * len(lead) + (br, bc), index)


def _v(x):
    return x if isinstance(x, _V) else _V(x)


_DN = {"nn": (((1,), (0,)), ((), ())), "nt": (((1,), (1,)), ((), ())), "tn": (((0,), (0,)), ((), ()))}


def _dot(a, b, mode="nn", prec=None):
    return lax.dot_general(a, b, _DN[mode], preferred_element_type=F32, precision=prec)


def _tile(n, cap):
    if n <= cap:
        return n
    for t in range(cap - cap % 128, 0, -128):
        if n % t == 0:
            return t
    return n


def _mm(pairs, mode, out_dtype, name, *, tm, tn, res=None, scale=None, batch=1, into=None):
    pairs = [(_v(a), _v(b)) for a, b in pairs]
    a0, b0 = pairs[0]
    M = a0.w if mode == "tn" else a0.rows
    N = b0.rows if mode == "nt" else b0.w
    tm, tn = _tile(M, tm), _tile(N, tn)
    assert M % tm == 0 and N % tn == 0, (name, M, N, tm, tn)
    n = len(pairs)

    def body(*refs):
        o_ref = refs[-1]
        acc = None
        for i in range(n):
            d = _dot(refs[2 * i][...].astype(BF16), refs[2 * i + 1][...].astype(BF16), mode)
            acc = d if acc is None else acc + d
        if scale is not None:
            acc = acc * scale
        if res is not None:
            acc = acc + refs[2 * n][...]
        o_ref[...] = acc.astype(o_ref.dtype)

    zero = lambda b, i, j: 0
    row = lambda b, i, j: i
    col = lambda b, i, j: j
    in_specs, args = [], []
    for a, b in pairs:
        in_specs.append(a.spec(a.rows, tm, zero, row) if mode == "tn" else a.spec(tm, a.w, row, zero))
        in_specs.append(b.spec(tn, b.w, col, zero) if mode == "nt" else b.spec(b.rows, tn, zero, col))
        args += [a.arr, b.arr]
    if res is not None:
        in_specs.append(pl.BlockSpec((tm, tn), lambda b, i, j: (i, j)))
        args.append(res)
    aliases = {}
    if into is None:
        out = jax.ShapeDtypeStruct(((batch,) if batch > 1 else ()) + (M, N), out_dtype)
        out_view = _V(out, lead=("b",) if batch > 1 else ())
    else:
        buf, lead = into
        assert buf.shape[-2:] == (M, N) and buf.dtype == out_dtype, (name, buf.shape, M, N)
        out = jax.ShapeDtypeStruct(buf.shape, buf.dtype)
        out_view = _V(out, lead=lead)
        aliases = {len(args): 0}
        in_specs.append(_ANY)
        args.append(buf)
    return pl.pallas_call(
        body, name=name, grid=(batch, M // tm, N // tn), in_specs=in_specs, out_specs=out_view.spec(tm, tn, row, col),
        out_shape=out, input_output_aliases=aliases, compiler_params=_cparams("parallel", "parallel", "parallel"),
    )(*args)


def _rms_fwd(x, g, name, tm=512):
    x = _v(x)
    S, D = x.rows, x.w
    tm = min(tm, S)

    def body(x_ref, g_ref, o_ref):
        xv = x_ref[...]
        y = xv * lax.rsqrt(jnp.mean(xv * xv, axis=-1, keepdims=True) + EPS)
        o_ref[...] = (y * g_ref[...]).astype(o_ref.dtype)

    return pl.pallas_call(
        body, name=name, grid=(S // tm,),
        in_specs=[x.spec(tm, D, lambda i: i, lambda i: 0), pl.BlockSpec((1, D), lambda i: (0, 0))],
        out_specs=pl.BlockSpec((tm, D), lambda i: (i, 0)),
        out_shape=jax.ShapeDtypeStruct((S, D), BF16), compiler_params=_cparams("parallel"),
    )(x.arr, g)


def _rms_bwd(x, g, dh, name, dres=None, out_dtype=F32, tm=512):
    x = _v(x)
    S, D = x.rows, x.w
    tm = min(tm, S)

    def body(*refs):
        if dres is None:
            x_ref, g_ref, dh_ref, dx_ref, dg_ref = refs
        else:
            x_ref, g_ref, dh_ref, dr_ref, dx_ref, dg_ref = refs
        xv = x_ref[...]
        rstd = lax.rsqrt(jnp.mean(xv * xv, axis=-1, keepdims=True) + EPS)
        xhat = xv * rstd
        dhv = dh_ref[...].astype(F32)
        dxhat = dhv * g_ref[...]
        dx = rstd * (dxhat - xhat * jnp.mean(dxhat * xhat, axis=-1, keepdims=True))
        if dres is not None:
            dx = dx + dr_ref[...]
        dx_ref[...] = dx.astype(dx_ref.dtype)

        @pl.when(pl.program_id(0) == 0)
        def _():
            dg_ref[...] = jnp.zeros_like(dg_ref)

        dg_ref[0:1, :] += jnp.sum(dhv * xhat, axis=0, keepdims=True)

    in_specs = [x.spec(tm, D, lambda i: i, lambda i: 0), pl.BlockSpec((1, D), lambda i: (0, 0)),
                pl.BlockSpec((tm, D), lambda i: (i, 0))]
    args = [x.arr, g, dh]
    if dres is not None:
        in_specs.append(pl.BlockSpec((tm, D), lambda i: (i, 0)))
        args.append(dres)
    return pl.pallas_call(
        body, name=name, grid=(S // tm,), in_specs=in_specs,
        out_specs=[pl.BlockSpec((tm, D), lambda i: (i, 0)), pl.BlockSpec((8, D), lambda i: (0, 0))],
        out_shape=[jax.ShapeDtypeStruct((S, D), out_dtype), jax.ShapeDtypeStruct((8, D), F32)],
        compiler_params=_cparams("arbitrary"),
    )(*args)


def _ffn_up(h, w1, w3, name, tm=512):
    S, D = h.shape
    NC, _, F4 = w1.shape
    tm = min(tm, S)

    def body(h_ref, w1_ref, w3_ref, a_ref, b_ref, u_ref):
        hv = h_ref[...]
        a = _dot(hv, w1_ref[...])
        b = _dot(hv, w3_ref[...])
        a_ref[...] = a.astype(BF16)
        b_ref[...] = b.astype(BF16)
        u_ref[...] = (a * jax.nn.sigmoid(a) * b).astype(BF16)

    blk = pl.BlockSpec((None, tm, F4), lambda i, j: (j, i, 0))
    wblk = pl.BlockSpec((None, D, F4), lambda i, j: (j, 0, 0))
    return pl.pallas_call(
        body, name=name, grid=(S // tm, NC), in_specs=[pl.BlockSpec((tm, D), lambda i, j: (i, 0)), wblk, wblk],
        out_specs=[blk, blk, blk], out_shape=[jax.ShapeDtypeStruct((NC, S, F4), BF16)] * 3,
        compiler_params=_cparams("parallel", "parallel"),
    )(h, w1, w3)


def _ffn_down_bwd(dxo, w2, a, b, name, tm=512):
    S, D = dxo.shape
    NC, F4, _ = w2.shape
    tm = min(tm, S)

    def body(dx_ref, w2_ref, a_ref, b_ref, da_ref, db_ref):
        du = _dot(dx_ref[...].astype(BF16), w2_ref[...], "nt") * 0.5
        av = a_ref[...].astype(F32)
        sig = jax.nn.sigmoid(av)
        da_ref[...] = (du * b_ref[...].astype(F32) * (sig * (1.0 + av * (1.0 - sig)))).astype(BF16)
        db_ref[...] = (du * av * sig).astype(BF16)

    blk = pl.BlockSpec((None, tm, F4), lambda i, j: (j, i, 0))
    return pl.pallas_call(
        body, name=name, grid=(S // tm, NC),
        in_specs=[pl.BlockSpec((tm, D), lambda i, j: (i, 0)), pl.BlockSpec((None, F4, D), lambda i, j: (j, 0, 0)), blk, blk],
        out_specs=[blk, blk], out_shape=[jax.ShapeDtypeStruct((NC, S, F4), BF16)] * 2,
        compiler_params=_cparams("parallel", "parallel"),
    )(dxo, w2, a, b)


def _ffn_fwd(x, g, w1, w3, w2, tag):
    h = _rms_fwd(x, g, f"{tag}_rms")
    a, b, u = _ffn_up(h, w1, w3, f"{tag}_up")
    nc = w2.shape[0]
    y = _mm([(_V(u, lead=(j,)), _V(w2, lead=(j,))) for j in range(nc)], "nn", F32, f"{tag}_down", tm=512, tn=1024, res=x, scale=0.5)
    return y, (h, a, b, u)


def _ffn_bwd(dxo, x, g, w1, w3, w2, saved, tag, bufs, layer):
    h, a, b, u = saved
    nc, D, F4 = w1.shape
    da, db = _ffn_down_bwd(dxo, w2, a, b, f"{tag}_down_bwd")
    into = lambda k: (bufs[k], ("b", layer))
    dw2 = _mm([(_V(u, lead=("b",)), dxo)], "tn", BF16, f"{tag}_dw2", tm=F4, tn=512, scale=0.5, batch=nc, into=into(2))
    dw1 = _mm([(h, _V(da, lead=("b",)))], "tn", BF16, f"{tag}_dw1", tm=D, tn=F4, batch=nc, into=into(0))
    dw3 = _mm([(h, _V(db, lead=("b",)))], "tn", BF16, f"{tag}_dw3", tm=D, tn=F4, batch=nc, into=into(1))
    pairs = [(_V(da, lead=(j,)), _V(w1, lead=(j,))) for j in range(nc)] + [(_V(db, lead=(j,)), _V(w3, lead=(j,))) for j in range(nc)]
    dh = _mm(pairs, "nt", F32, f"{tag}_dh", tm=512, tn=512)
    dx, dg = _rms_bwd(x, g, dh, f"{tag}_rms_bwd", dres=dxo)
    return dx, dg[0:1], (dw1, dw3, dw2)


def _iota(shape, dim):
    return lax.broadcasted_iota(jnp.int32, shape, dim)


def _head_block_ones(n, shift):
    return jnp.where((_iota((n, n), 0) >> shift) == (_iota((n, n), 1) >> shift), 1.0, 0.0).astype(F32)


def _lane_mask(width, lo, size):
    l = _iota((1, width), 1)
    return jnp.where((l >= lo) & (l < lo + size), 1.0, 0.0).astype(F32)


def _acc_rows(acc_ref, val, first):
    r = val.shape[0]
    part = jnp.sum(val.reshape(r // 8, 8, val.shape[1]), axis=0)

    @pl.when(first)
    def _():
        acc_ref[...] = part

    @pl.when(jnp.logical_not(first))
    def _():
        acc_ref[...] += part


_FLIPS = ((1, 0), (0, 1), (1, 1))


class _Ride:
    def __init__(self, name, src, src_at, dst, dst_at):
        self.name, self.src, self.src_at, self.dst, self.dst_at = name, src, src_at, dst, dst_at


def _ride_ops(ride, srcs, dsts, send_sems, recv_sems, local_sems):
    x, y, c = lax.axis_index("x"), lax.axis_index("y"), lax.axis_index("c")
    me = 2 * x + y
    at = lambda ref, idx: ref.at[idx] if idx else ref
    starts, waits = [], []
    for t, it in enumerate(ride):
        own = pltpu.make_async_copy(at(srcs[t], it.src_at(me)), at(dsts[t], it.dst_at(me)), local_sems.at[t])
        starts.append(own.start)
        waits.append(own.wait)
    for r, (fx, fy) in enumerate(_FLIPS):
        px, py = (1 - x) if fx else x, (1 - y) if fy else y
        peer = 2 * px + py
        for t, it in enumerate(ride):
            def copy(dst_block, r=r, t=t, it=it, px=px, py=py, peer=peer):
                return pltpu.make_async_remote_copy(
                    src_ref=at(srcs[t], it.src_at(peer)), dst_ref=at(dsts[t], it.dst_at(dst_block)),
                    send_sem=send_sems.at[r, t], recv_sem=recv_sems.at[r, t], device_id=(px, py, c), device_id_type=MESH)

            out, arrival = copy(me), copy(peer)
            starts.append(out.start)
            waits += [out.wait_send, arrival.wait_recv]
    return starts, waits


def _ride_call(body, ride, edges, *, name, grid, in_specs, out_specs, out_shape, args, scratch_shapes=(), semantics=()):
    scratch_shapes = list(scratch_shapes)
    if not ride:
        outs = pl.pallas_call(body, name=name, grid=grid, in_specs=in_specs, out_specs=out_specs, out_shape=out_shape,
                              scratch_shapes=scratch_shapes, compiler_params=_cparams(*semantics))(*args)
        return outs, {}
    n_in, n_out, n_sc, n = len(in_specs), len(out_specs), len(scratch_shapes), len(ride)

    def wrapped(*refs):
        ins, srcs = refs[:n_in], refs[n_in:n_in + n]
        o0 = n_in + 2 * n
        outs, dsts = refs[o0:o0 + n_out], refs[o0 + n_out:o0 + n_out + n]
        scratch = refs[o0 + n_out + n:o0 + n_out + n + n_sc]
        starts, waits = _ride_ops(ride, srcs, dsts, *refs[o0 + n_out + n + n_sc:])
        first, last = edges()

        @pl.when(first)
        def _():
            for s in starts:
                s()

        body(*ins, *outs, *scratch)

        @pl.when(last)
        def _():
            for w in waits:
                w()

    res = pl.pallas_call(
        wrapped, name=name, grid=grid, in_specs=list(in_specs) + [_ANY] * (2 * n), out_specs=list(out_specs) + [_ANY] * n,
        out_shape=list(out_shape) + [jax.ShapeDtypeStruct(it.dst.shape, it.dst.dtype) for it in ride],
        input_output_aliases={n_in + n + t: n_out + t for t in range(n)},
        scratch_shapes=scratch_shapes + [pltpu.SemaphoreType.DMA((3, n)), pltpu.SemaphoreType.DMA((3, n)),
                                         pltpu.SemaphoreType.DMA((n,))],
        compiler_params=_cparams(*(["arbitrary"] * len(grid))),
    )(*args, *[it.src for it in ride], *[it.dst for it in ride])
    return res[:n_out], {it.name: o for it, o in zip(ride, res[n_out:])}


def _exchange(ride, name):
    n = len(ride)

    def body(*refs):
        starts, waits = _ride_ops(ride, refs[:n], refs[2 * n:3 * n], *refs[3 * n:])
        for s in starts:
            s()
        for w in waits:
            w()

    res = pl.pallas_call(
        body, name=name, in_specs=[_ANY] * (2 * n), out_specs=[_ANY] * n,
        out_shape=[jax.ShapeDtypeStruct(it.dst.shape, it.dst.dtype) for it in ride],
        input_output_aliases={n + t: t for t in range(n)},
        scratch_shapes=[pltpu.SemaphoreType.DMA((3, n)), pltpu.SemaphoreType.DMA((3, n)), pltpu.SemaphoreType.DMA((n,))],
    )(*[it.src for it in ride], *[it.dst for it in ride])
    return {it.name: o for it, o in zip(ride, res)}


def _na_prep(z, c0, gq, gk, name, tm=512):
    S = z.shape[0]
    tm = min(tm, S)
    zv = _V(z, c0, 3 * NA_W)

    def body(z_ref, gq_ref, gk_ref, q_ref, k_ref, v_ref):
        bd = _head_block_ones(NA_W, 6)

        def norm(xv, gv):
            ms = _dot(xv * xv, bd, prec=HI) * (1.0 / NA_DH)
            return xv * lax.rsqrt(ms + EPS) * gv

        q_ref[...] = (norm(z_ref[:, 0:NA_W], gq_ref[...]) * (NA_DH ** -0.5)).astype(BF16)
        k_ref[...] = norm(z_ref[:, NA_W:2 * NA_W], gk_ref[...]).astype(BF16)
        v_ref[...] = z_ref[:, 2 * NA_W:3 * NA_W].astype(BF16)

    blk = pl.BlockSpec((tm, NA_W), lambda i: (i, 0))
    gspec = pl.BlockSpec((1, NA_W), lambda i: (0, 0))
    return pl.pallas_call(
        body, name=name, grid=(S // tm,),
        in_specs=[zv.spec(tm, 3 * NA_W, lambda i: i, lambda i: 0), gspec, gspec],
        out_specs=[blk, blk, blk], out_shape=[jax.ShapeDtypeStruct((S, NA_W), BF16)] * 3,
        compiler_params=_cparams("parallel"),
    )(z, gq, gk)


def _na_prep_bwd(z, c0, gq, gk, dqn, dkn, dv, name, tm=512):
    S = z.shape[0]
    tm = min(tm, S)
    zv = _V(z, c0, 3 * NA_W)

    def body(z_ref, gq_ref, gk_ref, dq_ref, dk_ref, dv_ref, dz_ref, dgq_ref, dgk_ref):
        bd = _head_block_ones(NA_W, 6)
        first = pl.program_id(0) == 0

        def norm_bwd(xv, gv, dy, dg_ref):
            ms = _dot(xv * xv, bd, prec=HI) * (1.0 / NA_DH)
            rstd = lax.rsqrt(ms + EPS)
            xhat = xv * rstd
            dxhat = dy * gv
            proj = _dot(dxhat * xhat, bd, prec=HI) * (1.0 / NA_DH)
            _acc_rows(dg_ref, dy * xhat, first)
            return rstd * (dxhat - xhat * proj)

        dz_ref[:, 0:NA_W] = norm_bwd(z_ref[:, 0:NA_W], gq_ref[...], dq_ref[...] * (NA_DH ** -0.5), dgq_ref).astype(BF16)
        dz_ref[:, NA_W:2 * NA_W] = norm_bwd(z_ref[:, NA_W:2 * NA_W], gk_ref[...], dk_ref[...], dgk_ref).astype(BF16)
        dz_ref[:, 2 * NA_W:3 * NA_W] = dv_ref[...].astype(BF16)

    blk = pl.BlockSpec((tm, NA_W), lambda i: (i, 0))
    gspec = pl.BlockSpec((1, NA_W), lambda i: (0, 0))
    acc = pl.BlockSpec((8, NA_W), lambda i: (0, 0))
    return pl.pallas_call(
        body, name=name, grid=(S // tm,),
        in_specs=[zv.spec(tm, 3 * NA_W, lambda i: i, lambda i: 0), gspec, gspec, blk, blk, blk],
        out_specs=[pl.BlockSpec((tm, 3 * NA_W), lambda i: (i, 0)), acc, acc],
        out_shape=[jax.ShapeDtypeStruct((S, 3 * NA_W), BF16), jax.ShapeDtypeStruct((8, NA_W), F32),
                   jax.ShapeDtypeStruct((8, NA_W), F32)],
        compiler_params=_cparams("arbitrary"),
    )(z, gq, gk, dqn, dkn, dv)


def _na_onehot():
    qc = np.arange(GRID_W)[:, None]
    kc = np.arange(GRID_W)[None, :]
    c0 = np.clip(qc - NA_WIN_C // 2, 0, GRID_W - NA_WIN_C)
    valid = (kc >= c0) & (kc < c0 + NA_WIN_C)
    dc = kc - qc + (NA_WIN_C - 1)
    e = np.zeros((32, GRID_W, GRID_W), np.float32)
    for d in range(2 * NA_WIN_C - 1):
        e[d] = valid & (dc == d)
    return e.reshape(32, GRID_W * GRID_W), valid.reshape(1, -1)


def _rpb_expand(rpb, name):
    e, valid = _na_onehot()
    negmask = np.where(valid, 0.0, NEG).astype(np.float32)
    nd = 2 * NA_WIN_R - 1
    r2 = jnp.pad(rpb.reshape(NA_HEADS * nd, 2 * NA_WIN_C - 1), ((0, 128 - NA_HEADS * nd), (0, 1)))

    def body(r_ref, e_ref, m_ref, o_ref):
        o_ref[...] = _dot(r_ref[...], e_ref[...], prec=HI) + m_ref[...]

    t = pl.pallas_call(body, name=name, out_shape=jax.ShapeDtypeStruct((128, GRID_W * GRID_W), F32))(
        r2, jnp.asarray(e), jnp.asarray(negmask))
    t = t[:NA_HEADS * nd].reshape(NA_HEADS, nd, GRID_W, GRID_W)
    return jnp.stack([jnp.concatenate([t[:, b + w] for w in range(NA_WIN_R)], axis=-1) for b in range(NA_WIN_R)], axis=1)


def _rpb_reduce(dbias, name):
    e, _ = _na_onehot()
    nd = 2 * NA_WIN_R - 1
    et = np.zeros((GRID_W * GRID_W, 128), np.float32)
    et[:, :32] = e.T
    sel = np.zeros((128, NA_HEADS * NA_WIN_R * NA_WIN_R), np.float32)
    for h in range(NA_HEADS):
        for b in range(NA_WIN_R):
            for w in range(NA_WIN_R):
                sel[h * nd + b + w, (h * NA_WIN_R + b) * NA_WIN_R + w] = 1.0
    x = dbias.reshape(NA_HEADS, NA_WIN_R, GRID_W, NA_WIN_R, GRID_W).transpose(0, 1, 3, 2, 4).reshape(-1, GRID_W * GRID_W)

    def body(x_ref, et_ref, sel_ref, o_ref):
        g = _dot(x_ref[...], et_ref[...], prec=HI)
        o_ref[...] = _dot(sel_ref[...], g, prec=HI)

    out = pl.pallas_call(body, name=name, out_shape=jax.ShapeDtypeStruct((128, 128), F32))(x, jnp.asarray(et), jnp.asarray(sel))
    return out[:NA_HEADS * nd, :2 * NA_WIN_C - 1].reshape(NA_HEADS, nd, 2 * NA_WIN_C - 1)


def _na_base(r, rows):
    return jnp.clip(r - NA_WIN_R // 2, 0, rows - NA_WIN_R) - r + (NA_WIN_R - 1)


def _na_scores(q_ref, k_ref, bias_ref, hh, r0w):
    m = _lane_mask(128, 64 * hh, 64)
    qm = (q_ref[...].astype(F32) * m).astype(BF16)
    kw = k_ref[r0w, :]
    s = _dot(qm, kw, "nt") + bias_ref[hh, 0]
    s = s - jnp.max(s, axis=-1, keepdims=True)
    p = jnp.exp(s)
    p = p / jnp.sum(p, axis=-1, keepdims=True)
    return m, qm, kw, p


def _grid_edges(n0, n1):
    def edges():
        i, j = pl.program_id(0), pl.program_id(1)
        return jnp.logical_and(i == 0, j == 0), jnp.logical_and(i == n0 - 1, j == n1 - 1)
    return edges


def _na_attn(qn, kn, vb, bias, name, ride=None):
    S = qn.shape[0]
    rows = S // GRID_W
    nk = NA_WIN_R * GRID_W

    def body(q_ref, k_ref, v_ref, b_ref, o_ref):
        r = pl.program_id(1)
        r0w = pl.ds(pl.multiple_of(jnp.clip(r - NA_WIN_R // 2, 0, rows - NA_WIN_R) * GRID_W, GRID_W), nk)
        acc = jnp.zeros((GRID_W, 128), F32)
        for hh in range(2):
            m, _, _, p = _na_scores(q_ref, k_ref, b_ref, hh, r0w)
            acc = acc + _dot(p.astype(BF16), v_ref[r0w, :]) * m
        o_ref[...] = acc.astype(BF16)

    (o,), got = _ride_call(
        body, ride, _grid_edges(NA_HEADS // 2, rows), name=name, grid=(NA_HEADS // 2, rows),
        in_specs=[pl.BlockSpec((GRID_W, 128), lambda p, r: (r, p)), pl.BlockSpec((S, 128), lambda p, r: (0, p)),
                  pl.BlockSpec((S, 128), lambda p, r: (0, p)),
                  pl.BlockSpec((2, 1, GRID_W, nk), lambda p, r: (p, _na_base(r, rows), 0, 0))],
        out_specs=[pl.BlockSpec((GRID_W, 128), lambda p, r: (r, p))],
        out_shape=[jax.ShapeDtypeStruct((S, NA_W), BF16)], args=(qn, kn, vb, bias), semantics=("parallel", "arbitrary"))
    return o, got


def _na_attn_bwd(qn, kn, vb, bias, do, name, ride=None):
    S = qn.shape[0]
    rows = S // GRID_W
    nk = NA_WIN_R * GRID_W

    def body(q_ref, k_ref, v_ref, b_ref, do_ref, dq_ref, dk_ref, dv_ref, db_ref):
        r = pl.program_id(1)

        @pl.when(r == 0)
        def _():
            dk_ref[...] = jnp.zeros_like(dk_ref)
            dv_ref[...] = jnp.zeros_like(dv_ref)

        r0w = pl.ds(pl.multiple_of(jnp.clip(r - NA_WIN_R // 2, 0, rows - NA_WIN_R) * GRID_W, GRID_W), nk)
        fresh = jnp.logical_or(r <= NA_WIN_R // 2, r > rows - NA_WIN_R // 2)
        dq = jnp.zeros((GRID_W, 128), F32)
        for hh in range(2):
            m, qm, kw, p = _na_scores(q_ref, k_ref, b_ref, hh, r0w)
            dom = (do_ref[...].astype(F32) * m).astype(BF16)
            dp = _dot(dom, v_ref[r0w, :], "nt")
            ds = p * (dp - jnp.sum(p * dp, axis=-1, keepdims=True))

            @pl.when(fresh)
            def _():
                db_ref[hh, 0] = ds

            @pl.when(jnp.logical_not(fresh))
            def _():
                db_ref[hh, 0] += ds

            dsb = ds.astype(BF16)
            dq = dq + _dot(dsb, kw) * m
            dk_ref[r0w, :] += _dot(dsb, qm, "tn")
            dv_ref[r0w, :] += _dot(p.astype(BF16), dom, "tn")
        dq_ref[...] = dq

    qblk = pl.BlockSpec((GRID_W, 128), lambda p, r: (r, p))
    full = pl.BlockSpec((S, 128), lambda p, r: (0, p))
    bblk = pl.BlockSpec((2, 1, GRID_W, nk), lambda p, r: (p, _na_base(r, rows), 0, 0))
    return _ride_call(
        body, ride, _grid_edges(NA_HEADS // 2, rows), name=name, grid=(NA_HEADS // 2, rows),
        in_specs=[qblk, full, full, bblk, qblk], out_specs=[qblk, full, full, bblk],
        out_shape=[jax.ShapeDtypeStruct((S, NA_W), F32)] * 3 + [jax.ShapeDtypeStruct((NA_HEADS, NA_WIN_R, GRID_W, nk), F32)],
        args=(qn, kn, vb, bias, do), semantics=("parallel", "arbitrary"))


def _logsig(x):
    return jnp.minimum(x, 0.0) - jnp.log(1.0 + jnp.exp(-jnp.abs(x)))


def _gla_gates(z, c0, wg, bias, name, tm=512):
    S = z.shape[0]
    tm = min(tm, S)
    zv = _V(z, c0, 128)
    W = 2 * GLA_HEADS * GLA_DK

    def body(z_ref, w_ref, b_ref, o_ref):
        pre = _dot(z_ref[...].astype(BF16), w_ref[...]) + b_ref[...]
        o_ref[...] = _logsig(pre) * (1.0 / GLA_TAU)

    return pl.pallas_call(
        body, name=name, grid=(S // tm,),
        in_specs=[zv.spec(tm, 128, lambda i: i, lambda i: 0), pl.BlockSpec((128, W), lambda i: (0, 0)),
                  pl.BlockSpec((1, W), lambda i: (0, 0))],
        out_specs=pl.BlockSpec((tm, W), lambda i: (i, 0)), out_shape=jax.ShapeDtypeStruct((S, W), F32),
        compiler_params=_cparams("parallel"),
    )(z, wg, bias)


def _gla_gates_bwd(z, c0, wg, bias, dg_f, dg_b, name, tm=512):
    S = z.shape[0]
    tm = min(tm, S)
    zv = _V(z, c0, 128)
    W = 2 * GLA_HEADS * GLA_DK

    def body(z_ref, w_ref, b_ref, dgf_ref, dgb_ref, dp_ref, db_ref):
        pre = _dot(z_ref[...].astype(BF16), w_ref[...]) + b_ref[...]
        dg = jnp.concatenate([dgf_ref[...], dgb_ref[...]], axis=-1)
        dpre = dg * (1.0 / GLA_TAU) * jax.nn.sigmoid(-pre)
        dp_ref[...] = dpre.astype(BF16)
        _acc_rows(db_ref, dpre, pl.program_id(0) == 0)

    half = pl.BlockSpec((tm, W // 2), lambda i: (i, 0))
    return pl.pallas_call(
        body, name=name, grid=(S // tm,),
        in_specs=[zv.spec(tm, 128, lambda i: i, lambda i: 0), pl.BlockSpec((128, W), lambda i: (0, 0)),
                  pl.BlockSpec((1, W), lambda i: (0, 0)), half, half],
        out_specs=[pl.BlockSpec((tm, W), lambda i: (i, 0)), pl.BlockSpec((8, W), lambda i: (0, 0))],
        out_shape=[jax.ShapeDtypeStruct((S, W), BF16), jax.ShapeDtypeStruct((8, W), F32)],
        compiler_params=_cparams("arbitrary"),
    )(z, wg, bias, dg_f, dg_b)


def _gla_chunk_terms(zqk, g, p, rev):
    C = GLA_CHUNK
    i, j = _iota((C, C), 0), _iota((C, C), 1)
    cum = jnp.where((j >= i) if rev else (j <= i), 1.0, 0.0).astype(F32)
    q2 = zqk[:, 128 * p:128 * p + 128] * (GLA_DK ** -0.5)
    k2 = zqk[:, 256 + 128 * p:256 + 128 * p + 128]
    b2 = _dot(cum, g[:, 128 * p:128 * p + 128], prec=HI)
    bl2 = b2[0:1] if rev else b2[C - 1:C]
    eb = jnp.exp(b2)
    qe2 = q2 * eb
    ke2 = k2 * jnp.exp(-b2)
    kend2 = k2 * jnp.exp(bl2 - b2)
    dec2 = jnp.exp(bl2)
    tri = (j > i) if rev else (j <= i)
    return b2, bl2, eb, qe2, ke2, kend2, dec2, tri


def _row_to_col(row):
    eye = _iota((128, 128), 0) == _iota((128, 128), 1)
    return jnp.sum(jnp.where(eye, row, 0.0), axis=1, keepdims=True)


def _col_to_row(col):
    eye = _iota((128, 128), 0) == _iota((128, 128), 1)
    return jnp.sum(jnp.where(eye, col, 0.0), axis=0, keepdims=True)


def _gla_fwd(z, c_qk, c_v, gfb, name):
    S = z.shape[0]
    C = GLA_CHUNK
    n = S // C
    WQK = 2 * GLA_HEADS * GLA_DK
    WV = GLA_HEADS * GLA_DV
    zqk, zvv = _V(z, c_qk, WQK), _V(z, c_v, WV)

    def body(qkf_ref, vf_ref, gf_ref, qkb_ref, vb_ref, gb_ref, of_ref, ob_ref, sf_ref, sb_ref, stf, stb):
        @pl.when(pl.program_id(0) == 0)
        def _():
            stf[...] = jnp.zeros_like(stf)
            stb[...] = jnp.zeros_like(stb)

        for rev, qk_ref, v_ref, g_ref, o_ref, s_ref, st in ((False, qkf_ref, vf_ref, gf_ref, of_ref, sf_ref, stf),
                                                            (True, qkb_ref, vb_ref, gb_ref, ob_ref, sb_ref, stb)):
            zqkv, gv = qk_ref[...], g_ref[...]
            for p in range(GLA_HEADS // 2):
                _, _, _, qe2, ke2, kend2, dec2, tri = _gla_chunk_terms(zqkv, gv, p, rev)
                dec_col = _row_to_col(dec2)
                keb = ke2.astype(BF16)
                for hh in range(2):
                    h = 2 * p + hh
                    m = _lane_mask(128, 64 * hh, 64)
                    qm = (qe2 * m).astype(BF16)
                    a = jnp.where(tri, _dot(qm, keb, "nt"), 0.0)
                    vh = v_ref[:, 128 * h:128 * h + 128].astype(BF16)
                    sp = st[h]
                    o_ref[:, 128 * h:128 * h + 128] = _dot(a.astype(BF16), vh) + _dot(qm, sp.astype(BF16))
                    s_ref[0, h] = sp
                    st[h] = dec_col * sp + _dot((kend2 * m).astype(BF16), vh, "tn")

    fw = lambda i: i
    bw = lambda i: n - 1 - i
    zero = lambda i: 0
    in_specs = []
    for ix, col in ((fw, 0), (bw, 1)):
        in_specs += [zqk.spec(C, WQK, ix, zero), zvv.spec(C, WV, ix, zero),
                     pl.BlockSpec((C, WQK // 2), functools.partial(lambda i, ix, col: (ix(i), col), ix=ix, col=col))]
    return pl.pallas_call(
        body, name=name, grid=(n,), in_specs=in_specs,
        out_specs=[pl.BlockSpec((C, WV), lambda i: (i, 0)), pl.BlockSpec((C, WV), lambda i: (n - 1 - i, 0)),
                   pl.BlockSpec((1, GLA_HEADS, 128, 128), lambda i: (i, 0, 0, 0)),
                   pl.BlockSpec((1, GLA_HEADS, 128, 128), lambda i: (n - 1 - i, 0, 0, 0))],
        out_shape=[jax.ShapeDtypeStruct((S, WV), F32)] * 2 + [jax.ShapeDtypeStruct((n, GLA_HEADS, 128, 128), F32)] * 2,
        scratch_shapes=[pltpu.VMEM((GLA_HEADS, 128, 128), F32)] * 2, compiler_params=_cparams("arbitrary"),
    )(z, z, gfb, z, z, gfb)


def _gla_bwd(z, c_qk, c_v, gfb, do, s_f, s_b, name):
    S = z.shape[0]
    C = GLA_CHUNK
    n = S // C
    WQK = 2 * GLA_HEADS * GLA_DK
    WV = GLA_HEADS * GLA_DV
    zqk, zvv = _V(z, c_qk, WQK), _V(z, c_v, WV)

    def body(qkf_ref, vf_ref, gf_ref, dof_ref, sf_ref, qkb_ref, vb_ref, gb_ref, dob_ref, sb_ref,
             dqkf_ref, dvf_ref, dgf_ref, dqkb_ref, dvb_ref, dgb_ref, dstf, dstb):
        @pl.when(pl.program_id(0) == 0)
        def _():
            dstf[...] = jnp.zeros_like(dstf)
            dstb[...] = jnp.zeros_like(dstb)

        dirs = ((False, qkf_ref, vf_ref, gf_ref, dof_ref, sf_ref, dqkf_ref, dvf_ref, dgf_ref, dstf),
                (True, qkb_ref, vb_ref, gb_ref, dob_ref, sb_ref, dqkb_ref, dvb_ref, dgb_ref, dstb))
        for rev, qk_ref, v_ref, g_ref, do_ref, s_ref, dqk_ref, dv_ref, dg_ref, dst in dirs:
            zqkv, gv = qk_ref[...], g_ref[...]
            i, j = _iota((C, C), 0), _iota((C, C), 1)
            cum_t = jnp.where((j <= i) if rev else (j >= i), 1.0, 0.0).astype(F32)
            edge = _iota((C, 128), 0) == (0 if rev else C - 1)
            for p in range(GLA_HEADS // 2):
                b2, bl2, eb, qe2, ke2, kend2, dec2, tri = _gla_chunk_terms(zqkv, gv, p, rev)
                dec_col = _row_to_col(dec2)
                keb = ke2.astype(BF16)
                dqe2 = jnp.zeros((C, 128), F32)
                dke2 = jnp.zeros((C, 128), F32)
                dkend2 = jnp.zeros((C, 128), F32)
                ddec2 = jnp.zeros((1, 128), F32)
                for hh in range(2):
                    h = 2 * p + hh
                    m = _lane_mask(128, 64 * hh, 64)
                    qm = (qe2 * m).astype(BF16)
                    kem = (ke2 * m).astype(BF16)
                    kendm = (kend2 * m).astype(BF16)
                    a = jnp.where(tri, _dot(qm, keb, "nt"), 0.0).astype(BF16)
                    vh = v_ref[:, 128 * h:128 * h + 128].astype(BF16)
                    doh = do_ref[:, 128 * h:128 * h + 128].astype(BF16)
                    sp = s_ref[0, h]
                    spb = sp.astype(BF16)
                    ds = dst[h]
                    dsb = ds.astype(BF16)
                    da = jnp.where(tri, _dot(doh, vh, "nt"), 0.0).astype(BF16)
                    dqe2 = dqe2 + _dot(da, kem) + _dot(doh, spb, "nt")
                    dke2 = dke2 + _dot(da, qm, "tn")
                    dv_ref[:, 128 * h:128 * h + 128] = _dot(a, doh, "tn") + _dot(kendm, dsb)
                    dkend2 = dkend2 + _dot(vh, dsb, "nt") * m
                    ddec2 = ddec2 + _col_to_row(jnp.sum(ds * sp, axis=1, keepdims=True))
                    dst[h] = dec_col * ds + _dot(qm, doh, "tn")
                dqk_ref[:, 128 * p:128 * p + 128] = dqe2 * eb * (GLA_DK ** -0.5)
                dqk_ref[:, 256 + 128 * p:256 + 128 * p + 128] = dke2 * jnp.exp(-b2) + dkend2 * jnp.exp(bl2 - b2)
                dkk = dkend2 * kend2
                db2 = dqe2 * qe2 - dke2 * ke2 - dkk
                dbl2 = jnp.sum(dkk, axis=0, keepdims=True) + ddec2 * dec2
                db2 = db2 + jnp.where(edge, dbl2, 0.0)
                dg_ref[:, 128 * p:128 * p + 128] = _dot(cum_t, db2, prec=HI)

    fw = lambda i: n - 1 - i
    bw = lambda i: i
    zero = lambda i: 0
    in_specs, out_specs = [], []
    for ix, col in ((fw, 0), (bw, 1)):
        blk = functools.partial(lambda i, ix: (ix(i), 0), ix=ix)
        in_specs += [zqk.spec(C, WQK, ix, zero), zvv.spec(C, WV, ix, zero),
                     pl.BlockSpec((C, WQK // 2), functools.partial(lambda i, ix, col: (ix(i), col), ix=ix, col=col)),
                     pl.BlockSpec((C, WV), blk),
                     pl.BlockSpec((1, GLA_HEADS, 128, 128), functools.partial(lambda i, ix: (ix(i), 0, 0, 0), ix=ix))]
        out_specs += [pl.BlockSpec((C, WQK), blk), pl.BlockSpec((C, WV), blk), pl.BlockSpec((C, WQK // 2), blk)]
    shapes = [jax.ShapeDtypeStruct((S, WQK), F32), jax.ShapeDtypeStruct((S, WV), F32), jax.ShapeDtypeStruct((S, WQK // 2), F32)]
    return pl.pallas_call(
        body, name=name, grid=(n,), in_specs=in_specs, out_specs=out_specs, out_shape=shapes * 2,
        scratch_shapes=[pltpu.VMEM((GLA_HEADS, 128, 128), F32)] * 2, compiler_params=_cparams("arbitrary"),
    )(z, z, gfb, do, s_f, z, z, gfb, do, s_b)


def _gla_post(o_f, o_b, z, c_r, gn, name, tm=512):
    S, WV = o_f.shape
    tm = min(tm, S)
    zr = _V(z, c_r, WV)

    def body(of_ref, ob_ref, r_ref, g_ref, y_ref):
        gr = r_ref[...]
        sil = gr * jax.nn.sigmoid(gr)
        for h in range(GLA_HEADS):
            sl = slice(GLA_DV * h, GLA_DV * (h + 1))
            o = of_ref[:, sl] + ob_ref[:, sl]
            on = o * lax.rsqrt(jnp.mean(o * o, axis=-1, keepdims=True) + EPS) * g_ref[...]
            y_ref[:, sl] = (on * sil[:, sl]).astype(BF16)

    blk = pl.BlockSpec((tm, WV), lambda i: (i, 0))
    return pl.pallas_call(
        body, name=name, grid=(S // tm,),
        in_specs=[blk, blk, zr.spec(tm, WV, lambda i: i, lambda i: 0), pl.BlockSpec((1, GLA_DV), lambda i: (0, 0))],
        out_specs=blk, out_shape=jax.ShapeDtypeStruct((S, WV), BF16), compiler_params=_cparams("parallel"),
    )(o_f, o_b, z, gn)


def _gla_post_bwd(o_f, o_b, z, c_r, gn, dy, name, tm=512):
    S, WV = o_f.shape
    tm = min(tm, S)
    zr = _V(z, c_r, WV)

    def body(of_ref, ob_ref, r_ref, g_ref, dy_ref, do_ref, dr_ref, dg_ref):
        gr = r_ref[...]
        sig = jax.nn.sigmoid(gr)
        sil = gr * sig
        dyv = dy_ref[...].astype(F32)
        dgn = jnp.zeros((tm, GLA_DV), F32)
        for h in range(GLA_HEADS):
            sl = slice(GLA_DV * h, GLA_DV * (h + 1))
            o = of_ref[:, sl] + ob_ref[:, sl]
            rstd = lax.rsqrt(jnp.mean(o * o, axis=-1, keepdims=True) + EPS)
            xhat = o * rstd
            don = dyv[:, sl] * sil[:, sl]
            dr_ref[:, sl] = (dyv[:, sl] * xhat * g_ref[...] * (sig[:, sl] * (1.0 + gr[:, sl] * (1.0 - sig[:, sl])))).astype(BF16)
            dxhat = don * g_ref[...]
            do_ref[:, sl] = rstd * (dxhat - xhat * jnp.mean(dxhat * xhat, axis=-1, keepdims=True))
            dgn = dgn + don * xhat
        _acc_rows(dg_ref, dgn, pl.program_id(0) == 0)

    blk = pl.BlockSpec((tm, WV), lambda i: (i, 0))
    return pl.pallas_call(
        body, name=name, grid=(S // tm,),
        in_specs=[blk, blk, zr.spec(tm, WV, lambda i: i, lambda i: 0), pl.BlockSpec((1, GLA_DV), lambda i: (0, 0)), blk],
        out_specs=[blk, blk, pl.BlockSpec((8, GLA_DV), lambda i: (0, 0))],
        out_shape=[jax.ShapeDtypeStruct((S, WV), F32), jax.ShapeDtypeStruct((S, WV), BF16), jax.ShapeDtypeStruct((8, GLA_DV), F32)],
        compiler_params=_cparams("arbitrary"),
    )(o_f, o_b, z, gn, dy)


def _gla_assemble(dqk_f, dqk_b, dv_f, dv_b, dgr, name, tm=512):
    S = dqk_f.shape[0]
    tm = min(tm, S)

    def body(a_ref, b_ref, c_ref, d_ref, r_ref, o_ref):
        o_ref[:, 0:512] = (a_ref[...] + b_ref[...]).astype(BF16)
        o_ref[:, 512:1024] = (c_ref[...] + d_ref[...]).astype(BF16)
        o_ref[:, 1024:1536] = r_ref[...]

    blk = pl.BlockSpec((tm, 512), lambda i: (i, 0))
    return pl.pallas_call(
        body, name=name, grid=(S // tm,), in_specs=[blk] * 5, out_specs=pl.BlockSpec((tm, 1536), lambda i: (i, 0)),
        out_shape=jax.ShapeDtypeStruct((S, 1536), BF16), compiler_params=_cparams("parallel"),
    )(dqk_f, dqk_b, dv_f, dv_b, dgr)


def _rope(r, cos, sg):
    return r * cos + pltpu.roll(r, 64, 1) * sg


def _unrope(dy, cos, sg):
    return dy * cos + pltpu.roll(dy * sg, 64, 1)


def _mla_prep(z, c_q, c_kr, wuq, wukv, g_cq, g_ckv, g_q, g_k, cos, sg, name, tm=256):
    S = z.shape[0]
    tm = min(tm, S)
    zc, zk = _V(z, c_q, 2 * MLA_RANK), _V(z, c_kr, 128)
    inv = 1.0 / MLA_QK

    def body(zc_ref, zk_ref, wuq_ref, wukv_ref, gcq_ref, gckv_ref, gq_ref, gk_ref, cos_ref, sg_ref,
             q_ref, k_ref, v_ref, cqn_ref, ckvn_ref):
        def norm(xv, gv):
            return (xv * lax.rsqrt(jnp.mean(xv * xv, axis=-1, keepdims=True) + EPS) * gv).astype(BF16)

        cqn = norm(zc_ref[:, 0:MLA_RANK], gcq_ref[...])
        ckvn = norm(zc_ref[:, MLA_RANK:2 * MLA_RANK], gckv_ref[...])
        cqn_ref[...] = cqn
        ckvn_ref[...] = ckvn
        qf = _dot(cqn, wuq_ref[...])
        kv = _dot(ckvn, wukv_ref[...])
        kr = zk_ref[...]
        krss = jnp.sum(kr * kr, axis=-1, keepdims=True)
        cosv, sgv = cos_ref[...], sg_ref[...]
        gq, gk = gq_ref[...], gk_ref[...]
        for h in range(MLA_HEADS):
            qh = qf[:, MLA_SLOT * h:MLA_SLOT * (h + 1)]
            qhn = qh * lax.rsqrt(jnp.sum(qh * qh, axis=-1, keepdims=True) * inv + EPS) * gq
            q_ref[:, MLA_SLOT * h:MLA_SLOT * h + 128] = qhn[:, 0:128].astype(BF16)
            q_ref[:, MLA_SLOT * h + 128:MLA_SLOT * (h + 1)] = _rope(qhn[:, 128:256], cosv, sgv).astype(BF16)
            kn = kv[:, 256 * h:256 * h + 128]
            rstd = lax.rsqrt((jnp.sum(kn * kn, axis=-1, keepdims=True) + krss) * inv + EPS)
            k_ref[:, MLA_SLOT * h:MLA_SLOT * h + 128] = (kn * rstd * gk[:, 0:128]).astype(BF16)
            k_ref[:, MLA_SLOT * h + 128:MLA_SLOT * (h + 1)] = _rope(kr * rstd * gk[:, 128:256], cosv, sgv).astype(BF16)
            v_ref[:, 128 * h:128 * (h + 1)] = kv[:, 256 * h + 128:256 * (h + 1)].astype(BF16)

    row = lambda w: pl.BlockSpec((tm, w), lambda i: (i, 0))
    const = lambda r, w: pl.BlockSpec((r, w), lambda i: (0, 0))
    W = MLA_HEADS * MLA_SLOT
    return pl.pallas_call(
        body, name=name, grid=(S // tm,),
        in_specs=[zc.spec(tm, 2 * MLA_RANK, lambda i: i, lambda i: 0), zk.spec(tm, 128, lambda i: i, lambda i: 0),
                  const(MLA_RANK, W), const(MLA_RANK, W), const(1, MLA_RANK), const(1, MLA_RANK), const(1, MLA_SLOT),
                  const(1, MLA_SLOT), row(128), row(128)],
        out_specs=[row(W), row(W), row(MLA_HEADS * MLA_V), row(MLA_RANK), row(MLA_RANK)],
        out_shape=[jax.ShapeDtypeStruct((S, W), BF16), jax.ShapeDtypeStruct((S, W), BF16),
                   jax.ShapeDtypeStruct((S, MLA_HEADS * MLA_V), BF16), jax.ShapeDtypeStruct((S, MLA_RANK), BF16),
                   jax.ShapeDtypeStruct((S, MLA_RANK), BF16)],
        compiler_params=_cparams("parallel"),
    )(z, z, wuq, wukv, g_cq, g_ckv, g_q, g_k, cos, sg)


def _mla_prep_bwd(z, c_kr, cqn, ckvn, wuq, wukv, g_q, g_k, cos, sg, dq, dk, dv, name, tm=256):
    S = z.shape[0]
    tm = min(tm, S)
    zk = _V(z, c_kr, 128)
    inv = 1.0 / MLA_QK

    def body(zk_ref, cqn_ref, ckvn_ref, wuq_ref, wukv_ref, gq_ref, gk_ref, cos_ref, sg_ref, dq_ref, dk_ref, dv_ref,
             dqf_ref, dkv_ref, dkr_ref, dgq_ref, dgk_ref):
        first = pl.program_id(0) == 0
        qf = _dot(cqn_ref[...], wuq_ref[...])
        kv = _dot(ckvn_ref[...], wukv_ref[...])
        kr = zk_ref[...]
        krss = jnp.sum(kr * kr, axis=-1, keepdims=True)
        cosv, sgv = cos_ref[...], sg_ref[...]
        gq, gk = gq_ref[...], gk_ref[...]
        dkr = jnp.zeros((tm, 128), F32)
        dgq = jnp.zeros((tm, MLA_SLOT), F32)
        dgkn = jnp.zeros((tm, 128), F32)
        dgkr = jnp.zeros((tm, 128), F32)
        for h in range(MLA_HEADS):
            qh = qf[:, MLA_SLOT * h:MLA_SLOT * (h + 1)]
            rstd = lax.rsqrt(jnp.sum(qh * qh, axis=-1, keepdims=True) * inv + EPS)
            xhat = qh * rstd
            dyn = jnp.concatenate([dq_ref[:, MLA_SLOT * h:MLA_SLOT * h + 128],
                                   _unrope(dq_ref[:, MLA_SLOT * h + 128:MLA_SLOT * (h + 1)], cosv, sgv)], axis=-1)
            dxhat = dyn * gq
            dqf_ref[:, MLA_SLOT * h:MLA_SLOT * (h + 1)] = (
                rstd * (dxhat - xhat * (jnp.sum(dxhat * xhat, axis=-1, keepdims=True) * inv))).astype(BF16)
            dgq = dgq + dyn * xhat

            kn = kv[:, 256 * h:256 * h + 128]
            rstd = lax.rsqrt((jnp.sum(kn * kn, axis=-1, keepdims=True) + krss) * inv + EPS)
            xn, xr = kn * rstd, kr * rstd
            dyn_n = dk_ref[:, MLA_SLOT * h:MLA_SLOT * h + 128]
            dyn_r = _unrope(dk_ref[:, MLA_SLOT * h + 128:MLA_SLOT * (h + 1)], cosv, sgv)
            dxn, dxr = dyn_n * gk[:, 0:128], dyn_r * gk[:, 128:256]
            proj = (jnp.sum(dxn * xn, axis=-1, keepdims=True) + jnp.sum(dxr * xr, axis=-1, keepdims=True)) * inv
            dkv_ref[:, 256 * h:256 * h + 128] = (rstd * (dxn - xn * proj)).astype(BF16)
            dkv_ref[:, 256 * h + 128:256 * (h + 1)] = dv_ref[:, 128 * h:128 * (h + 1)].astype(BF16)
            dkr = dkr + rstd * (dxr - xr * proj)
            dgkn = dgkn + dyn_n * xn
            dgkr = dgkr + dyn_r * xr
        dkr_ref[...] = dkr.astype(BF16)
        _acc_rows(dgq_ref, dgq, first)
        _acc_rows(dgk_ref, jnp.concatenate([dgkn, dgkr], axis=-1), first)

    row = lambda w: pl.BlockSpec((tm, w), lambda i: (i, 0))
    const = lambda r, w: pl.BlockSpec((r, w), lambda i: (0, 0))
    W = MLA_HEADS * MLA_SLOT
    return pl.pallas_call(
        body, name=name, grid=(S // tm,),
        in_specs=[zk.spec(tm, 128, lambda i: i, lambda i: 0), row(MLA_RANK), row(MLA_RANK), const(MLA_RANK, W),
                  const(MLA_RANK, W), const(1, MLA_SLOT), const(1, MLA_SLOT), row(128), row(128), row(W), row(W),
                  row(MLA_HEADS * MLA_V)],
        out_specs=[row(W), row(W), row(128), const(8, MLA_SLOT), const(8, MLA_SLOT)],
        out_shape=[jax.ShapeDtypeStruct((S, W), BF16), jax.ShapeDtypeStruct((S, W), BF16), jax.ShapeDtypeStruct((S, 128), BF16),
                   jax.ShapeDtypeStruct((8, MLA_SLOT), F32), jax.ShapeDtypeStruct((8, MLA_SLOT), F32)],
        compiler_params=_cparams("arbitrary"),
    )(z, cqn, ckvn, wuq, wukv, g_q, g_k, cos, sg, dq, dk, dv)


def _softmax_rows(s):
    s = s - jnp.max(s, axis=-1, keepdims=True)
    p = jnp.exp(s)
    return p / jnp.sum(p, axis=-1, keepdims=True)


def _mla_attn(q, k, v, name, tq=256, ride=None):
    S = q.shape[0]
    tq = min(tq, S)
    scale = MLA_QK ** -0.5

    def body(q_ref, k_ref, v_ref, o_ref):
        p = _softmax_rows(_dot(q_ref[...], k_ref[...], "nt") * scale)
        o_ref[...] = _dot(p.astype(BF16), v_ref[...]).astype(BF16)

    (o,), got = _ride_call(
        body, ride, _grid_edges(MLA_HEADS, S // tq), name=name, grid=(MLA_HEADS, S // tq),
        in_specs=[pl.BlockSpec((tq, MLA_SLOT), lambda h, i: (i, h)), pl.BlockSpec((S, MLA_SLOT), lambda h, i: (0, h)),
                  pl.BlockSpec((S, MLA_V), lambda h, i: (0, h))],
        out_specs=[pl.BlockSpec((tq, MLA_V), lambda h, i: (i, h))],
        out_shape=[jax.ShapeDtypeStruct((S, MLA_HEADS * MLA_V), BF16)], args=(q, k, v), semantics=("parallel", "parallel"))
    return o, got


def _mla_attn_bwd(q, k, v, do, name, tq=256, ride=None):
    S = q.shape[0]
    tq = min(tq, S)
    scale = MLA_QK ** -0.5

    def body(q_ref, k_ref, v_ref, do_ref, dq_ref, dk_ref, dv_ref):
        @pl.when(pl.program_id(1) == 0)
        def _():
            dk_ref[...] = jnp.zeros_like(dk_ref)
            dv_ref[...] = jnp.zeros_like(dv_ref)

        qv, kvv, dov = q_ref[...], k_ref[...], do_ref[...]
        p = _softmax_rows(_dot(qv, kvv, "nt") * scale)
        dp = _dot(dov, v_ref[...], "nt")
        ds = (p * (dp - jnp.sum(p * dp, axis=-1, keepdims=True)) * scale).astype(BF16)
        dq_ref[...] = _dot(ds, kvv)
        dk_ref[...] += _dot(ds, qv, "tn")
        dv_ref[...] += _dot(p.astype(BF16), dov, "tn")

    W = MLA_HEADS * MLA_SLOT
    return _ride_call(
        body, ride, _grid_edges(MLA_HEADS, S // tq), name=name, grid=(MLA_HEADS, S // tq),
        in_specs=[pl.BlockSpec((tq, MLA_SLOT), lambda h, i: (i, h)), pl.BlockSpec((S, MLA_SLOT), lambda h, i: (0, h)),
                  pl.BlockSpec((S, MLA_V), lambda h, i: (0, h)), pl.BlockSpec((tq, MLA_V), lambda h, i: (i, h))],
        out_specs=[pl.BlockSpec((tq, MLA_SLOT), lambda h, i: (i, h)), pl.BlockSpec((S, MLA_SLOT), lambda h, i: (0, h)),
                   pl.BlockSpec((S, MLA_V), lambda h, i: (0, h))],
        out_shape=[jax.ShapeDtypeStruct((S, W), F32), jax.ShapeDtypeStruct((S, W), F32),
                   jax.ShapeDtypeStruct((S, MLA_HEADS * MLA_V), F32)],
        args=(q, k, v, do), semantics=("parallel", "arbitrary"))


def _merge(ys, ws, z, name, tm=256):
    S = z.shape[0]
    D = ws[0].shape[1]
    tm = min(tm, S)
    zg = _V(z, 0, 3 * D)

    def body(y0, y1, y2, w0, w1, w2, g_ref, m_ref, p0, p1, p2):
        acc = jnp.zeros((tm, D), F32)
        for i, (y_ref, w_ref, p_ref) in enumerate(((y0, w0, p0), (y1, w1, p1), (y2, w2, p2))):
            pv = _dot(y_ref[...], w_ref[...])
            p_ref[...] = pv.astype(BF16)
            acc = acc + jax.nn.sigmoid(g_ref[:, D * i:D * (i + 1)]) * pv
        m_ref[...] = acc.astype(BF16)

    yb = pl.BlockSpec((tm, ys[0].shape[1]), lambda i: (i, 0))
    wb = pl.BlockSpec(ws[0].shape, lambda i: (0, 0))
    ob = pl.BlockSpec((tm, D), lambda i: (i, 0))
    return pl.pallas_call(
        body, name=name, grid=(S // tm,), in_specs=[yb] * 3 + [wb] * 3 + [zg.spec(tm, 3 * D, lambda i: i, lambda i: 0)],
        out_specs=[ob] * 4, out_shape=[jax.ShapeDtypeStruct((S, D), BF16)] * 4, compiler_params=_cparams("parallel"),
    )(*ys, *ws, z)


def _merge_bwd(dmixed, ps, z, name, tm=256):
    S, D = dmixed.shape
    tm = min(tm, S)
    zg = _V(z, 0, 3 * D)

    def body(dm_ref, p0, p1, p2, g_ref, d0, d1, d2, dg_ref):
        dm = dm_ref[...]
        for i, (p_ref, d_ref) in enumerate(((p0, d0), (p1, d1), (p2, d2))):
            gt = jax.nn.sigmoid(g_ref[:, D * i:D * (i + 1)])
            d_ref[...] = (dm * gt).astype(BF16)
            dg_ref[:, D * i:D * (i + 1)] = (dm * p_ref[...].astype(F32) * gt * (1.0 - gt)).astype(BF16)

    ob = pl.BlockSpec((tm, D), lambda i: (i, 0))
    return pl.pallas_call(
        body, name=name, grid=(S // tm,), in_specs=[ob] * 4 + [zg.spec(tm, 3 * D, lambda i: i, lambda i: 0)],
        out_specs=[ob] * 3 + [pl.BlockSpec((tm, 3 * D), lambda i: (i, 0))],
        out_shape=[jax.ShapeDtypeStruct((S, D), BF16)] * 3 + [jax.ShapeDtypeStruct((S, 3 * D), BF16)],
        compiler_params=_cparams("parallel"),
    )(dmixed, *ps, z)


def _loss_head(y, target, name, tm=512):
    S, D = y.shape
    tm = min(tm, S)

    def body(y_ref, t_ref, dy_ref, l_ref):
        e = y_ref[...] - t_ref[...]
        dy_ref[...] = e * (1.0 / D)
        sq = e * e
        part = jnp.sum(sq.reshape(tm // 8, 8, D), axis=0)
        part = jnp.sum(part.reshape(8, D // 128, 128), axis=1) * (0.5 / D)

        @pl.when(pl.program_id(0) == 0)
        def _():
            l_ref[...] = part

        @pl.when(pl.program_id(0) != 0)
        def _():
            l_ref[...] += part

    blk = pl.BlockSpec((tm, D), lambda i: (i, 0))
    return pl.pallas_call(
        body, name=name, grid=(S // tm,), in_specs=[blk, blk], out_specs=[blk, pl.BlockSpec((8, 128), lambda i: (0, 0))],
        out_shape=[jax.ShapeDtypeStruct((S, D), F32), jax.ShapeDtypeStruct((8, 128), F32)],
        compiler_params=_cparams("arbitrary"),
    )(y, target)


def _fold(parts, name, fold=None):
    L, _, W = parts.shape
    assert L <= 8

    def body(*refs):
        p_ref, o_ref = refs[0], refs[-1]
        rows = [jnp.sum(p_ref[l], axis=0, keepdims=True) for l in range(L)]
        rows += [jnp.zeros((1, W), F32)] * (8 - L)
        sums = jnp.concatenate(rows, axis=0)
        o_ref[...] = sums if fold is None else _dot(sums, refs[1][...], prec=HI)

    args = (parts,) if fold is None else (parts, jnp.asarray(fold))
    wout = W if fold is None else 128
    return pl.pallas_call(body, name=name, out_shape=jax.ShapeDtypeStruct((8, wout), F32))(*args)[:L]


def _adamw(w, g, m, v, name, q=None):
    R, C = w.shape
    tr = R
    for cand in (512, 256, 128, 64, 32, 16, 8):
        if R % cand == 0 and cand * C * 4 <= 2 * 2**20:
            tr = cand
            break

    def body(*refs):
        if q is None:
            w_ref, g_ref, m_ref, v_ref, d_ref, nm_ref, nv_ref = refs
            gv = g_ref[...]
        else:
            w_ref, g_ref, q_ref, m_ref, v_ref, go_ref, d_ref, nm_ref, nv_ref = refs
            gv = g_ref[...] + q_ref[...]
            go_ref[...] = gv
        mn = ADAM_B1 * m_ref[...] + (1.0 - ADAM_B1) * gv
        vn = ADAM_B2 * v_ref[...] + (1.0 - ADAM_B2) * (gv * gv)
        nm_ref[...] = mn
        nv_ref[...] = vn
        m_hat = mn / (1.0 - ADAM_B1 ** ADAM_STEP)
        v_hat = vn / (1.0 - ADAM_B2 ** ADAM_STEP)
        d_ref[...] = -ADAM_LR * (m_hat / (jnp.sqrt(v_hat) + ADAM_EPS) + ADAM_WD * w_ref[...])

    blk = pl.BlockSpec((tr, C), lambda i: (i, 0))
    args = (w, g, m, v) if q is None else (w, g, q, m, v)
    nout = 3 if q is None else 4
    return pl.pallas_call(
        body, name=name, grid=(R // tr,), in_specs=[blk] * len(args), out_specs=[blk] * nout,
        out_shape=[jax.ShapeDtypeStruct((R, C), F32)] * nout, compiler_params=_cparams("parallel"),
    )(*args)


def _sibling_exchange(srcs, name):
    n = len(srcs)

    def body(*refs):
        src_refs, dst_refs = refs[:n], refs[n:2 * n]
        send_sems, recv_sems = refs[2 * n:]
        x, y, c = lax.axis_index("x"), lax.axis_index("y"), lax.axis_index("c")
        copies = [pltpu.make_async_remote_copy(src_ref=src_refs[t], dst_ref=dst_refs[t], send_sem=send_sems.at[t],
                                               recv_sem=recv_sems.at[t], device_id=(x, y, 1 - c), device_id_type=MESH)
                  for t in range(n)]
        for cp in copies:
            cp.start()
        for cp in copies:
            cp.wait()

    return pl.pallas_call(
        body, name=name, in_specs=[_ANY] * n, out_specs=[_ANY] * n,
        out_shape=[jax.ShapeDtypeStruct(s.shape, s.dtype) for s in srcs],
        scratch_shapes=[pltpu.SemaphoreType.DMA((n,)), pltpu.SemaphoreType.DMA((n,))],
    )(*srcs)


def _allreduce_small(v, name):
    R = v.shape[0]

    def body(v_ref, o_ref, slots, send_sems, recv_sems):
        x, y, c = lax.axis_index("x"), lax.axis_index("y"), lax.axis_index("c")
        me = 4 * x + 2 * y + c
        slots[me] = v_ref[...]
        sent = []
        for r in range(1, 8):
            fx, fy, fc = (r >> 2) & 1, (r >> 1) & 1, r & 1
            px, py, pc = (1 - x) if fx else x, (1 - y) if fy else y, (1 - c) if fc else c
            peer = 4 * px + 2 * py + pc

            def copy(slot, r=r, px=px, py=py, pc=pc):
                return pltpu.make_async_remote_copy(
                    src_ref=v_ref, dst_ref=slots.at[slot], send_sem=send_sems.at[r - 1], recv_sem=recv_sems.at[r - 1],
                    device_id=(px, py, pc), device_id_type=MESH)

            cp = copy(me)
            cp.start()
            sent.append((cp, copy(peer)))
        for cp, arrival in sent:
            cp.wait_send()
            arrival.wait_recv()
        acc = slots[0]
        for k in range(1, 8):
            acc = acc + slots[k]
        o_ref[...] = acc

    vm = pl.BlockSpec(memory_space=pltpu.VMEM)
    return pl.pallas_call(
        body, name=name, in_specs=[vm], out_specs=vm, out_shape=jax.ShapeDtypeStruct((R, 128), F32),
        scratch_shapes=[pltpu.VMEM((8, R, 128), F32), pltpu.SemaphoreType.DMA((7,)), pltpu.SemaphoreType.DMA((7,))],
    )(v)


def _sum4(recv, name, tr=512):
    _, R, W = recv.shape
    tr = _tile(R, tr)
    assert R % tr == 0

    def body(r_ref, o_ref):
        o_ref[...] = ((r_ref[0].astype(F32) + r_ref[1].astype(F32)) + r_ref[2].astype(F32)) + r_ref[3].astype(F32)

    return pl.pallas_call(
        body, name=name, grid=(R // tr,), in_specs=[pl.BlockSpec((4, tr, W), lambda i: (0, i, 0))],
        out_specs=pl.BlockSpec((tr, W), lambda i: (i, 0)), out_shape=jax.ShapeDtypeStruct((R, W), F32),
        compiler_params=_cparams("parallel"),
    )(recv)


W_NAMES = ("ffn1_norm", "ffn1_w1", "ffn1_w3", "ffn1_w2", "mix_norm", "w_in", "na_q_norm", "na_k_norm", "na_rpb",
           "gla_gf_up", "gla_gf_bias", "gla_gb_up", "gla_gb_bias", "gla_out_norm", "mla_cq_norm", "mla_ckv_norm",
           "mla_w_uq", "mla_w_ukv", "mla_q_norm", "mla_k_norm", "w_br_na", "w_br_gla", "w_br_mla", "w_out",
           "ffn2_norm", "ffn2_w1", "ffn2_w3", "ffn2_w2")
SHARDED = {"ffn1_w1": 2, "ffn1_w3": 2, "ffn1_w2": 1, "w_in": 2, "gla_gf_up": 2, "gla_gb_up": 2, "mla_w_uq": 2,
           "mla_w_ukv": 2, "w_br_na": 2, "w_br_gla": 2, "w_br_mla": 2, "w_out": 1, "ffn2_w1": 2, "ffn2_w3": 2,
           "ffn2_w2": 1}
REPLICATED = tuple(n for n in W_NAMES if n not in SHARDED)
FFN_W = ("ffn1_w1", "ffn1_w3", "ffn1_w2", "ffn2_w1", "ffn2_w3", "ffn2_w2")


def _win_layout(w, D):
    z = lambda n: jnp.zeros(w.shape[:-1] + (n,), w.dtype)
    return jnp.concatenate([w[..., O_GATES:], w[..., :O_GFL], w[..., O_CQ:O_KR], w[..., O_GFL:O_CQ], z(96),
                            w[..., O_KR:O_KR + 32], z(32), w[..., O_KR + 32:O_KR + 64], z(32)], axis=-1)


def _win_unlayout(dw, D):
    g = 3 * D
    return jnp.concatenate([dw[..., g:g + O_GFL], dw[..., g + 3584:g + 3616], dw[..., g + 3072:g + 3584],
                            dw[..., g + 3712:g + 3744], dw[..., g + 3776:g + 3808], dw[..., :g]], axis=-1)


def _uq_layout(w):
    s = w.shape[:-1]
    w = w.reshape(s + (MLA_HEADS, MLA_QK))
    z = jnp.zeros(s + (MLA_HEADS, 32), w.dtype)
    return jnp.concatenate([w[..., :160], z, w[..., 160:], z], axis=-1).reshape(s + (MLA_HEADS * MLA_SLOT,))


def _uq_unlayout(dw):
    s = dw.shape[:-1]
    dw = dw.reshape(s + (MLA_HEADS, MLA_SLOT))
    return jnp.concatenate([dw[..., :160], dw[..., 192:224]], axis=-1).reshape(s + (MLA_HEADS * MLA_QK,))


def _slot_layout(g):
    z = jnp.zeros(g.shape[:-1] + (32,), g.dtype)
    return jnp.concatenate([g[..., :160], z, g[..., 160:], z], axis=-1)


def _slot_unlayout(g):
    return jnp.concatenate([g[..., :160], g[..., 192:224]], axis=-1)


def _layer_fwd(x, w, cos, sg, ride_na=None, ride_mla=None):
    D = x.shape[1]
    NA, GL, ML, LR, KR = 3 * D, 3 * D + 1536, 3 * D + 3072, 3 * D + 3584, 3 * D + 3712
    x1, f1 = _ffn_fwd(x, w["ffn1_norm"], w["ffn1_w1"], w["ffn1_w3"], w["ffn1_w2"], "ffn1")
    h = _rms_fwd(x1, w["mix_norm"], "mix_rms")
    nz = w["w_in"].shape[1]
    z = _mm([(h, w["w_in"])], "nn", F32, "w_in", tm=512, tn=_tile(nz, 1280))
    qn, kn, vb = _na_prep(z, NA, w["na_gq"], w["na_gk"], "na_prep")
    bias = _rpb_expand(w["na_rpb"], "rpb_expand")
    y_na, got = _na_attn(qn, kn, vb, bias, "na_attn", ride=ride_na)
    gfb = _gla_gates(z, LR, w["gla_wg"], w["gla_gbias"], "gla_gates")
    o_f, o_b, s_f, s_b = _gla_fwd(z, GL, GL + 512, gfb, "gla_fwd")
    y_gla = _gla_post(o_f, o_b, z, GL + 1024, w["gla_out_norm"], "gla_post")
    q, k, v, cqn, ckvn = _mla_prep(z, ML, KR, w["mla_wuq"], w["mla_w_ukv"], w["mla_cq_norm"], w["mla_ckv_norm"],
                                   w["mla_gq"], w["mla_gk"], cos, sg, "mla_prep")
    y_mla, got_mla = _mla_attn(q, k, v, "mla_attn", ride=ride_mla)
    got = {**got, **got_mla}
    mixed, p0, p1, p2 = _merge([y_na, y_gla, y_mla], [w["w_br_na"], w["w_br_gla"], w["w_br_mla"]], z, "merge")
    x2 = _mm([(mixed, w["w_out"])], "nn", F32, "w_out", tm=512, tn=1024, res=x1)
    x3, f2 = _ffn_fwd(x2, w["ffn2_norm"], got["ffn2_w1"], got["ffn2_w3"], got["ffn2_w2"], "ffn2")
    saved = dict(x=x, x1=x1, x2=x2, f1=f1, f2=f2, h=h, z=z, qn=qn, kn=kn, vb=vb, bias=bias, y_na=y_na, gfb=gfb, o_f=o_f,
                 o_b=o_b, s_f=s_f, s_b=s_b, y_gla=y_gla, q=q, k=k, v=v, cqn=cqn, ckvn=ckvn, y_mla=y_mla, mixed=mixed,
                 p0=p0, p1=p1, p2=p2)
    return x3, saved, got


def _split4(a, axis):
    n = a.shape[axis] // 4
    return jnp.stack([lax.slice_in_dim(a, j * n, (j + 1) * n, axis=axis) for j in range(4)]).astype(BF16)


def _layer_bwd(dx3, w, sv, cos, sg, bufs, recv, layer, ride_na):
    D = dx3.shape[1]
    at_layer = lambda chip: (chip, layer)
    NA, GL, ML, LR, KR = 3 * D, 3 * D + 1536, 3 * D + 3072, 3 * D + 3584, 3 * D + 3712
    z = sv["z"]
    g = {}
    dx2, g["ffn2_norm"], (g["ffn2_w1"], g["ffn2_w3"], g["ffn2_w2"]) = _ffn_bwd(
        dx3, sv["x2"], w["ffn2_norm"], w["ffn2_w1"], w["ffn2_w3"], w["ffn2_w2"], sv["f2"], "ffn2",
        (bufs["ffn2_w1"], bufs["ffn2_w3"], bufs["ffn2_w2"]), layer)
    dmixed = _mm([(dx2, w["w_out"])], "nt", F32, "w_out_dx", tm=512, tn=512)
    g["w_out"] = _mm([(sv["mixed"], dx2)], "tn", F32, "w_out_dw", tm=D, tn=256)
    d0, d1, d2, dgates = _merge_bwd(dmixed, [sv["p0"], sv["p1"], sv["p2"]], z, "merge_bwd")
    dys = []
    for d, y, nm, dt in ((d0, sv["y_na"], "w_br_na", BF16), (d1, sv["y_gla"], "w_br_gla", F32), (d2, sv["y_mla"], "w_br_mla", BF16)):
        dys.append(_mm([(d, w[nm])], "nt", dt, nm + "_dy", tm=512, tn=512))
        g[nm] = _mm([(y, d)], "tn", F32, nm + "_dw", tm=512, tn=512)
    (dqn, dkn, dvn, dbias), got = _na_attn_bwd(sv["qn"], sv["kn"], sv["vb"], sv["bias"], dys[0], "na_attn_bwd", ride=ride_na)
    recv = {**recv, **got}
    dz_na, g["na_gq"], g["na_gk"] = _na_prep_bwd(z, NA, w["na_gq"], w["na_gk"], dqn, dkn, dvn, "na_prep_bwd")
    g["na_rpb"] = _rpb_reduce(dbias, "rpb_reduce")
    do, dgr, g["gla_out_norm"] = _gla_post_bwd(sv["o_f"], sv["o_b"], z, GL + 1024, w["gla_out_norm"], dys[1], "gla_post_bwd")
    dqk_f, dv_f, dg_f, dqk_b, dv_b, dg_b = _gla_bwd(z, GL, GL + 512, sv["gfb"], do, sv["s_f"], sv["s_b"], "gla_bwd")
    dz_gla = _gla_assemble(dqk_f, dqk_b, dv_f, dv_b, dgr, "gla_assemble")
    dpre, g["gla_gbias"] = _gla_gates_bwd(z, LR, w["gla_wg"], w["gla_gbias"], dg_f, dg_b, "gla_gates_bwd")
    g["gla_wg"] = _mm([(_V(z, LR, 128), dpre)], "tn", F32, "gla_wg_dw", tm=128, tn=512)
    dz_lr = _mm([(dpre, w["gla_wg"])], "nt", BF16, "gla_wg_dz", tm=512, tn=128)
    ride = [_Ride(n, g[n], at_layer, recv[n], at_layer) for n in ("ffn2_w1", "ffn2_w3", "ffn2_w2")]
    ride += [_Ride(n, _split4(g[n], SHARDED[n] - 1), lambda chip: (chip,), recv[n], at_layer)
             for n in ("w_out", "w_br_na", "w_br_gla", "w_br_mla")]
    (dq, dk, dv), got = _mla_attn_bwd(sv["q"], sv["k"], sv["v"], dys[2], "mla_attn_bwd", ride=ride)
    recv = {**recv, **got}
    dqf, dkv, dz_kr, g["mla_gq"], g["mla_gk"] = _mla_prep_bwd(
        z, KR, sv["cqn"], sv["ckvn"], w["mla_wuq"], w["mla_w_ukv"], w["mla_gq"], w["mla_gk"], cos, sg, dq, dk, dv, "mla_prep_bwd")
    g["mla_wuq"] = _mm([(sv["cqn"], dqf)], "tn", F32, "mla_wuq_dw", tm=256, tn=512)
    g["mla_w_ukv"] = _mm([(sv["ckvn"], dkv)], "tn", F32, "mla_wukv_dw", tm=256, tn=512)
    dcqn = _mm([(dqf, w["mla_wuq"])], "nt", F32, "mla_wuq_dx", tm=512, tn=256)
    dckvn = _mm([(dkv, w["mla_w_ukv"])], "nt", F32, "mla_wukv_dx", tm=512, tn=256)
    dz_cq, dg_cq = _rms_bwd(_V(z, ML, MLA_RANK), w["mla_cq_norm"], dcqn, "mla_cq_rms_bwd", out_dtype=BF16)
    dz_ckv, dg_ckv = _rms_bwd(_V(z, ML + MLA_RANK, MLA_RANK), w["mla_ckv_norm"], dckvn, "mla_ckv_rms_bwd", out_dtype=BF16)
    g["mla_cq_norm"], g["mla_ckv_norm"] = dg_cq[0:1], dg_ckv[0:1]
    segs = ((dgates, 0, 3 * D), (dz_na, NA, 1536), (dz_gla, GL, 1536), (dz_cq, ML, MLA_RANK), (dz_ckv, ML + MLA_RANK, MLA_RANK),
            (dz_lr, LR, 128), (dz_kr, KR, 128))
    dh = _mm([(dz, _V(w["w_in"], c0, wd)) for dz, c0, wd in segs], "nt", F32, "w_in_dx", tm=512, tn=512)
    g["w_in"] = jnp.concatenate(
        [_mm([(sv["h"], dz)], "tn", F32, f"w_in_dw{i}", tm=D, tn=_tile(wd, 256)) for i, (dz, _, wd) in enumerate(segs)], axis=1)
    dx1, dg_mix = _rms_bwd(sv["x1"], w["mix_norm"], dh, "mix_rms_bwd", dres=dx2)
    g["mix_norm"] = dg_mix[0:1]
    dx, g["ffn1_norm"], (g["ffn1_w1"], g["ffn1_w3"], g["ffn1_w2"]) = _ffn_bwd(
        dx1, sv["x"], w["ffn1_norm"], w["ffn1_w1"], w["ffn1_w3"], w["ffn1_w2"], sv["f1"], "ffn1",
        (bufs["ffn1_w1"], bufs["ffn1_w3"], bufs["ffn1_w2"]), layer)
    late = dict(w_in=_split4(_win_unlayout(g["w_in"], D), 1), mla_w_uq=_split4(_uq_unlayout(g["mla_wuq"]), 1),
                mla_w_ukv=_split4(g["mla_w_ukv"], 1), gla_gf_up=_split4(g["gla_wg"][0:GLA_RANK, 0:256], 1),
                gla_gb_up=_split4(g["gla_wg"][GLA_RANK:2 * GLA_RANK, 256:512], 1))
    return dx, g, late, recv


def _head_fold(width, period, lo=0):
    f = np.zeros((width, 128), np.float32)
    f[np.arange(width), lo + np.arange(width) % period] = 1.0
    return f


def kernel(x, ffn1_norm, ffn1_w1, ffn1_w3, ffn1_w2, mix_norm, w_in, na_q_norm, na_k_norm, na_rpb, gla_gf_up, gla_gf_bias,
           gla_gb_up, gla_gb_bias, gla_out_norm, mla_cq_norm, mla_ckv_norm, mla_w_uq, mla_w_ukv, mla_q_norm, mla_k_norm,
           w_br_na, w_br_gla, w_br_mla, w_out, ffn2_norm, ffn2_w1, ffn2_w3, ffn2_w2, loss_target, m_ffn1_norm, m_ffn1_w1,
           m_ffn1_w3, m_ffn1_w2, m_mix_norm, m_w_in, m_na_q_norm, m_na_k_norm, m_na_rpb, m_gla_gf_up, m_gla_gf_bias,
           m_gla_gb_up, m_gla_gb_bias, m_gla_out_norm, m_mla_cq_norm, m_mla_ckv_norm, m_mla_w_uq, m_mla_w_ukv,
           m_mla_q_norm, m_mla_k_norm, m_w_br_na, m_w_br_gla, m_w_br_mla, m_w_out, m_ffn2_norm, m_ffn2_w1, m_ffn2_w3,
           m_ffn2_w2, v_ffn1_norm, v_ffn1_w1, v_ffn1_w3, v_ffn1_w2, v_mix_norm, v_w_in, v_na_q_norm, v_na_k_norm,
           v_na_rpb, v_gla_gf_up, v_gla_gf_bias, v_gla_gb_up, v_gla_gb_bias, v_gla_out_norm, v_mla_cq_norm,
           v_mla_ckv_norm, v_mla_w_uq, v_mla_w_ukv, v_mla_q_norm, v_mla_k_norm, v_w_br_na, v_w_br_gla, v_w_br_mla,
           v_w_out, v_ffn2_norm, v_ffn2_w1, v_ffn2_w3, v_ffn2_w2):
    given = dict(locals())
    wts = {n: given[n] for n in W_NAMES}
    mom = {n: given["m_" + n] for n in W_NAMES}
    var = {n: given["v_" + n] for n in W_NAMES}
    xs, target = x[0], loss_target[0]
    S, D = xs.shape
    L = ffn1_norm.shape[0]

    sh_names = tuple(SHARDED)
    LATE = ("ffn2_w1", "ffn2_w3", "ffn2_w2")
    HEAVY = ("ffn1_w1", "ffn1_w3", "ffn1_w2", "w_in")
    LIGHT = tuple(n for n in sh_names if n not in LATE + HEAVY)
    shard_shape = lambda n: tuple(wts[n].shape[1:])

    def gather_items(names, l):
        return [_Ride(n, wts[n][l].astype(BF16), lambda chip: (), lax.empty((4,) + shard_shape(n), BF16), lambda chip: (chip,))
                for n in names]

    cols = lambda p: jnp.concatenate([p[j] for j in range(4)], axis=-1)

    def layer_weights(gl, l):
        r1 = lambda a: a[l][None]
        wg = jnp.zeros((128, 2 * GLA_HEADS * GLA_DK), BF16)
        wg = wg.at[0:GLA_RANK, 0:256].set(cols(gl["gla_gf_up"])).at[GLA_RANK:2 * GLA_RANK, 256:512].set(cols(gl["gla_gb_up"]))
        return dict(
            ffn1_norm=r1(ffn1_norm), ffn1_w1=gl["ffn1_w1"], ffn1_w3=gl["ffn1_w3"], ffn1_w2=gl["ffn1_w2"],
            mix_norm=r1(mix_norm), w_in=_win_layout(cols(gl["w_in"]), D),
            na_gq=jnp.tile(na_q_norm[l], NA_HEADS)[None], na_gk=jnp.tile(na_k_norm[l], NA_HEADS)[None], na_rpb=na_rpb[l],
            gla_wg=wg, gla_gbias=jnp.concatenate([gla_gf_bias[l], gla_gb_bias[l]])[None], gla_out_norm=r1(gla_out_norm),
            mla_cq_norm=r1(mla_cq_norm), mla_ckv_norm=r1(mla_ckv_norm), mla_wuq=_uq_layout(cols(gl["mla_w_uq"])),
            mla_w_ukv=cols(gl["mla_w_ukv"]), mla_gq=_slot_layout(mla_q_norm[l])[None], mla_gk=_slot_layout(mla_k_norm[l])[None],
            w_br_na=cols(gl["w_br_na"]), w_br_gla=cols(gl["w_br_gla"]), w_br_mla=cols(gl["w_br_mla"]),
            w_out=gl["w_out"].reshape(D, D), ffn2_norm=r1(ffn2_norm))

    half = MLA_ROPE // 2
    inv = ROPE_THETA ** (-jnp.arange(half, dtype=F32) / half)
    ang = jnp.arange(S, dtype=F32)[:, None] * inv[None, :]
    cos = jnp.tile(jnp.cos(ang), (1, 4))
    sg = jnp.concatenate([-jnp.sin(ang), -jnp.sin(ang), jnp.sin(ang), jnp.sin(ang)], axis=1)

    arrived = _exchange(gather_items(HEAVY + LIGHT, 0), "weights_all_gather")
    xc, saved, layers = xs, [], []
    for l in range(L):
        w = layer_weights(arrived, l)
        more = l + 1 < L
        xc, sv, arrived = _layer_fwd(xc, w, cos, sg, gather_items(LATE, l) + (gather_items(LIGHT, l + 1) if more else []),
                                     gather_items(HEAVY, l + 1) if more else None)
        saved.append(sv)
        layers.append({**w, **{n: arrived[n] for n in LATE}})
    dy, loss_part = _loss_head(xc, target, "loss_head")

    bufs = {n: lax.empty((4, L) + shard_shape(n), BF16) for n in FFN_W}
    recv = {n: lax.empty((4, L) + shard_shape(n), BF16) for n in sh_names}
    dx, g, ride = dy, [None] * L, None
    for l in reversed(range(L)):
        dx, g[l], late, recv = _layer_bwd(dx, layers[l], saved[l], cos, sg, bufs, recv, l, ride)
        bufs = {n: g[l][n] for n in FFN_W}
        at_l = functools.partial(lambda chip, l: (chip, l), l=l)
        ride = [_Ride(n, bufs[n], at_l, recv[n], at_l) for n in ("ffn1_w1", "ffn1_w3", "ffn1_w2")]
        ride += [_Ride(n, late[n], lambda chip: (chip,), recv[n], at_l) for n in late]
    recv = {**recv, **_exchange(ride, "grads_chip_exchange")}

    stk = lambda n: jnp.stack([g[l][n] for l in range(L)])
    gs = {n: stk(n)[:, 0] for n in ("ffn1_norm", "mix_norm", "mla_cq_norm", "mla_ckv_norm", "ffn2_norm")}
    gs["na_q_norm"] = _fold(stk("na_gq"), "na_gq_fold", _head_fold(NA_W, NA_DH))[:, :NA_DH]
    gs["na_k_norm"] = _fold(stk("na_gk"), "na_gk_fold", _head_fold(NA_W, NA_DH))[:, :NA_DH]
    gs["na_rpb"] = stk("na_rpb")
    gbias = _fold(stk("gla_gbias"), "gla_gbias_fold")
    gs["gla_gf_bias"], gs["gla_gb_bias"] = gbias[:, :256], gbias[:, 256:]
    gs["gla_out_norm"] = _fold(stk("gla_out_norm"), "gla_out_norm_fold")
    gs["mla_q_norm"] = _slot_unlayout(_fold(stk("mla_gq"), "mla_gq_fold"))
    gs["mla_k_norm"] = _slot_unlayout(_fold(stk("mla_gk"), "mla_gk_fold"))

    mine = [_sum4(recv[n].reshape(4, -1, recv[n].shape[-1]), "grads_chip_sum_" + n) for n in sh_names]
    other = _sibling_exchange(mine, "grads_sibling_exchange")
    gsh = {}

    small_shapes = [wts[n].shape[1:] for n in REPLICATED]
    n_small = sum(int(np.prod(s)) for s in small_shapes) * L
    flat = jnp.concatenate([gs[n].reshape(-1) for n in REPLICATED] + [loss_part.reshape(-1)])
    pad = -flat.shape[0] % 1024
    red = _allreduce_small(jnp.pad(flat, (0, pad)).reshape(-1, 128), "small_all_reduce").reshape(-1)
    loss = jnp.sum(red[n_small:n_small + 1024])
    off = 0
    for n, s in zip(REPLICATED, small_shapes):
        cnt = int(np.prod(s)) * L
        gsh[n] = red[off:off + cnt].reshape((L,) + tuple(s))
        off += cnt

    as2d = lambda a: a.reshape(-1, a.shape[-1])
    upd = {}
    for n, p, q in zip(sh_names, mine, other):
        outs = [o.reshape(wts[n].shape) for o in _adamw(as2d(wts[n]), p, as2d(mom[n]), as2d(var[n]), "adamw_" + n, q=q)]
        gsh[n], upd[n] = outs[0], outs[1:]
    pk = lambda d: jnp.pad(jnp.concatenate([d[n].reshape(-1) for n in REPLICATED]), (0, -n_small % 1024)).reshape(-1, 128)
    small = _adamw(pk(wts), pk(gsh), pk(mom), pk(var), "adamw_replicated")
    off = 0
    for n, s in zip(REPLICATED, small_shapes):
        cnt = int(np.prod(s)) * L
        upd[n] = [o.reshape(-1)[off:off + cnt].reshape((L,) + tuple(s)) for o in small]
        off += cnt

    return (loss, dx[None], *[gsh[n] for n in W_NAMES], *[upd[n][0] for n in W_NAMES], *[upd[n][1] for n in W_NAMES],
            *[upd[n][2] for n in W_NAMES])
```

```python
import functools
import math

import numpy as np
import jax
import jax.numpy as jnp
from jax import lax
from jax.experimental import pallas as pl
from jax.experimental.pallas import tpu as pltpu

F32 = jnp.float32
BF16 = jnp.bfloat16
HI = lax.Precision.HIGHEST
MESH = pl.DeviceIdType.MESH

EPS = 1e-6
GRID_W = 64
NA_HEADS, NA_DH, NA_WIN_R, NA_WIN_C = 8, 64, 8, 16
NA_W = NA_HEADS * NA_DH
GLA_HEADS, GLA_DK, GLA_DV, GLA_RANK, GLA_TAU, GLA_CHUNK = 4, 64, 128, 16, 16.0, 64
MLA_HEADS, MLA_RANK, MLA_NOPE, MLA_ROPE, MLA_V = 4, 256, 128, 64, 128
MLA_QK = MLA_NOPE + MLA_ROPE
MLA_SLOT = 256
ROPE_THETA = 10000.0
ADAM_LR, ADAM_B1, ADAM_B2, ADAM_EPS, ADAM_WD, ADAM_STEP = 0.001, 0.9, 0.999, 1e-08, 0.01, 10

V7X_VMEM_BYTES = 64 * 2**20
VMEM_LIMIT = V7X_VMEM_BYTES - 12 * 2**20
NEG = -1e30

O_GQ, O_GFL, O_CQ, O_KR, O_GATES = 1536, 3072, 3104, 3616, 3680


_ANY = pl.BlockSpec(memory_space=pl.ANY)


def _cparams(*sem):
    return pltpu.CompilerParams(dimension_semantics=sem, vmem_limit_bytes=VMEM_LIMIT)


class _V:
    def __init__(self, arr, c0=0, w=None, lead=()):
        self.arr, self.c0, self.lead = arr, c0, tuple(lead)
        assert arr.ndim == 2 + len(self.lead), (arr.shape, lead)
        self.w = arr.shape[-1] if w is None else w

    @property
    def rows(self):
        return self.arr.shape[-2]

    def spec(self, br, bc, rfn, cfn):
        assert self.c0 % bc == 0 and self.w % bc == 0, (self.c0, self.w, bc)
        off, lead = self.c0 // bc, self.lead

        def index(*g):
            return tuple(g[0] if e == "b" else e for e in lead) + (rfn(*g), off + cfn(*g))

        return pl.BlockSpec((None,) * len(lead) + (br, bc), index)


def _v(x):
    return x if isinstance(x, _V) else _V(x)


_DN = {"nn": (((1,), (0,)), ((), ())), "nt": (((1,), (1,)), ((), ())), "tn": (((0,), (0,)), ((), ()))}


def _dot(a, b, mode="nn", prec=None):
    return lax.dot_general(a, b, _DN[mode], preferred_element_type=F32, precision=prec)


def _tile(n, cap):
    if n <= cap:
        return n
    for t in range(cap - cap % 128, 0, -128):
        if n % t == 0:
            return t
    return n


def _mm(pairs, mode, out_dtype, name, *, tm, tn, res=None, scale=None, batch=1, into=None, ride=None):
    pairs = [(_v(a), _v(b)) for a, b in pairs]
    a0, b0 = pairs[0]
    M = a0.w if mode == "tn" else a0.rows
    N = b0.rows if mode == "nt" else b0.w
    tm, tn = _tile(M, tm), _tile(N, tn)
    assert M % tm == 0 and N % tn == 0, (name, M, N, tm, tn)
    n = len(pairs)

    def body(*refs):
        o_ref = refs[-1]
        acc = None
        for i in range(n):
            d = _dot(refs[2 * i][...].astype(BF16), refs[2 * i + 1][...].astype(BF16), mode)
            acc = d if acc is None else acc + d
        if scale is not None:
            acc = acc * scale
        if res is not None:
            acc = acc + refs[2 * n][...]
        o_ref[...] = acc.astype(o_ref.dtype)

    zero = lambda b, i, j: 0
    row = lambda b, i, j: i
    col = lambda b, i, j: j
    in_specs, args = [], []
    for a, b in pairs:
        in_specs.append(a.spec(a.rows, tm, zero, row) if mode == "tn" else a.spec(tm, a.w, row, zero))
        in_specs.append(b.spec(tn, b.w, col, zero) if mode == "nt" else b.spec(b.rows, tn, zero, col))
        args += [a.arr, b.arr]
    if res is not None:
        in_specs.append(pl.BlockSpec((tm, tn), lambda b, i, j: (i, j)))
        args.append(res)
    aliases = {}
    if into is None:
        out = jax.ShapeDtypeStruct(((batch,) if batch > 1 else ()) + (M, N), out_dtype)
        out_view = _V(out, lead=("b",) if batch > 1 else ())
    else:
        buf, lead = into
        assert buf.shape[-2:] == (M, N) and buf.dtype == out_dtype, (name, buf.shape, M, N)
        out = jax.ShapeDtypeStruct(buf.shape, buf.dtype)
        out_view = _V(out, lead=lead)
        aliases = {len(args): 0}
        in_specs.append(_ANY)
        args.append(buf)
    (res,), got = _ride_call(
        body, ride, name=name, grid=(batch, M // tm, N // tn), in_specs=in_specs, out_specs=[out_view.spec(tm, tn, row, col)],
        out_shape=[out], aliases=aliases, args=args, semantics=("parallel", "parallel", "parallel"))
    return res if ride is None else (res, got)


def _rms_fwd(x, g, name, tm=512):
    x = _v(x)
    S, D = x.rows, x.w
    tm = min(tm, S)

    def body(x_ref, g_ref, o_ref):
        xv = x_ref[...]
        y = xv * lax.rsqrt(jnp.mean(xv * xv, axis=-1, keepdims=True) + EPS)
        o_ref[...] = (y * g_ref[...]).astype(o_ref.dtype)

    return pl.pallas_call(
        body, name=name, grid=(S // tm,),
        in_specs=[x.spec(tm, D, lambda i: i, lambda i: 0), pl.BlockSpec((1, D), lambda i: (0, 0))],
        out_specs=pl.BlockSpec((tm, D), lambda i: (i, 0)),
        out_shape=jax.ShapeDtypeStruct((S, D), BF16), compiler_params=_cparams("parallel"),
    )(x.arr, g)


def _rms_bwd(x, g, dh, name, dres=None, out_dtype=F32, tm=512):
    x = _v(x)
    S, D = x.rows, x.w
    tm = min(tm, S)

    def body(*refs):
        if dres is None:
            x_ref, g_ref, dh_ref, dx_ref, dg_ref = refs
        else:
            x_ref, g_ref, dh_ref, dr_ref, dx_ref, dg_ref = refs
        xv = x_ref[...]
        rstd = lax.rsqrt(jnp.mean(xv * xv, axis=-1, keepdims=True) + EPS)
        xhat = xv * rstd
        dhv = dh_ref[...].astype(F32)
        dxhat = dhv * g_ref[...]
        dx = rstd * (dxhat - xhat * jnp.mean(dxhat * xhat, axis=-1, keepdims=True))
        if dres is not None:
            dx = dx + dr_ref[...]
        dx_ref[...] = dx.astype(dx_ref.dtype)

        @pl.when(pl.program_id(0) == 0)
        def _():
            dg_ref[...] = jnp.zeros_like(dg_ref)

        dg_ref[0:1, :] += jnp.sum(dhv * xhat, axis=0, keepdims=True)

    in_specs = [x.spec(tm, D, lambda i: i, lambda i: 0), pl.BlockSpec((1, D), lambda i: (0, 0)),
                pl.BlockSpec((tm, D), lambda i: (i, 0))]
    args = [x.arr, g, dh]
    if dres is not None:
        in_specs.append(pl.BlockSpec((tm, D), lambda i: (i, 0)))
        args.append(dres)
    return pl.pallas_call(
        body, name=name, grid=(S // tm,), in_specs=in_specs,
        out_specs=[pl.BlockSpec((tm, D), lambda i: (i, 0)), pl.BlockSpec((8, D), lambda i: (0, 0))],
        out_shape=[jax.ShapeDtypeStruct((S, D), out_dtype), jax.ShapeDtypeStruct((8, D), F32)],
        compiler_params=_cparams("arbitrary"),
    )(*args)


def _ffn_up(h, w1, w3, name, tm=512, ride=None):
    S, D = h.shape
    NC, _, F4 = w1.shape
    tm = min(tm, S)

    def body(h_ref, w1_ref, w3_ref, a_ref, b_ref, u_ref):
        hv = h_ref[...]
        a = _dot(hv, w1_ref[...])
        b = _dot(hv, w3_ref[...])
        a_ref[...] = a.astype(BF16)
        b_ref[...] = b.astype(BF16)
        u_ref[...] = (a * jax.nn.sigmoid(a) * b).astype(BF16)

    blk = pl.BlockSpec((None, tm, F4), lambda i, j: (j, i, 0))
    wblk = pl.BlockSpec((None, D, F4), lambda i, j: (j, 0, 0))
    return _ride_call(
        body, ride, name=name, grid=(S // tm, NC), in_specs=[pl.BlockSpec((tm, D), lambda i, j: (i, 0)), wblk, wblk],
        out_specs=[blk, blk, blk], out_shape=[jax.ShapeDtypeStruct((NC, S, F4), BF16)] * 3, args=(h, w1, w3),
        semantics=("parallel", "parallel"))


def _ffn_down_bwd(dxo, w2, a, b, name, tm=512, ride=None):
    S, D = dxo.shape
    NC, F4, _ = w2.shape
    tm = min(tm, S)

    def body(dx_ref, w2_ref, a_ref, b_ref, da_ref, db_ref):
        du = _dot(dx_ref[...].astype(BF16), w2_ref[...], "nt") * 0.5
        av = a_ref[...].astype(F32)
        sig = jax.nn.sigmoid(av)
        da_ref[...] = (du * b_ref[...].astype(F32) * (sig * (1.0 + av * (1.0 - sig)))).astype(BF16)
        db_ref[...] = (du * av * sig).astype(BF16)

    blk = pl.BlockSpec((None, tm, F4), lambda i, j: (j, i, 0))
    return _ride_call(
        body, ride, name=name, grid=(S // tm, NC),
        in_specs=[pl.BlockSpec((tm, D), lambda i, j: (i, 0)), pl.BlockSpec((None, F4, D), lambda i, j: (j, 0, 0)), blk, blk],
        out_specs=[blk, blk], out_shape=[jax.ShapeDtypeStruct((NC, S, F4), BF16)] * 2, args=(dxo, w2, a, b),
        semantics=("parallel", "parallel"))


def _ffn_fwd(x, g, w1, w3, w2, tag, ride=None):
    h = _rms_fwd(x, g, f"{tag}_rms")
    (a, b, u), got = _ffn_up(h, w1, w3, f"{tag}_up", ride=ride)
    nc = w2.shape[0]
    y = _mm([(_V(u, lead=(j,)), _V(w2, lead=(j,))) for j in range(nc)], "nn", F32, f"{tag}_down", tm=512, tn=1024, res=x, scale=0.5)
    return y, (h, a, b, u), got


def _ffn_bwd(dxo, x, g, w1, w3, w2, saved, tag, bufs, layer, ride_down=None, ride_dh=None):
    h, a, b, u = saved
    nc, D, F4 = w1.shape
    (da, db), got = _ffn_down_bwd(dxo, w2, a, b, f"{tag}_down_bwd", ride=ride_down)
    into = lambda k: (bufs[k], ("b", layer))
    dw2 = _mm([(_V(u, lead=("b",)), dxo)], "tn", BF16, f"{tag}_dw2", tm=F4, tn=512, scale=0.5, batch=nc, into=into(2))
    dw1 = _mm([(h, _V(da, lead=("b",)))], "tn", BF16, f"{tag}_dw1", tm=D, tn=F4, batch=nc, into=into(0))
    dw3 = _mm([(h, _V(db, lead=("b",)))], "tn", BF16, f"{tag}_dw3", tm=D, tn=F4, batch=nc, into=into(1))
    pairs = [(_V(da, lead=(j,)), _V(w1, lead=(j,))) for j in range(nc)] + [(_V(db, lead=(j,)), _V(w3, lead=(j,))) for j in range(nc)]
    dh, got_dh = _mm(pairs, "nt", F32, f"{tag}_dh", tm=512, tn=512, ride=ride_dh or [])
    dx, dg = _rms_bwd(x, g, dh, f"{tag}_rms_bwd", dres=dxo)
    return dx, dg[0:1], (dw1, dw3, dw2), {**got, **got_dh}


def _iota(shape, dim):
    return lax.broadcasted_iota(jnp.int32, shape, dim)


def _head_block_ones(n, shift):
    return jnp.where((_iota((n, n), 0) >> shift) == (_iota((n, n), 1) >> shift), 1.0, 0.0).astype(F32)


def _lane_mask(width, lo, size):
    l = _iota((1, width), 1)
    return jnp.where((l >= lo) & (l < lo + size), 1.0, 0.0).astype(F32)


def _acc_rows(acc_ref, val, first):
    r = val.shape[0]
    part = jnp.sum(val.reshape(r // 8, 8, val.shape[1]), axis=0)

    @pl.when(first)
    def _():
        acc_ref[...] = part

    @pl.when(jnp.logical_not(first))
    def _():
        acc_ref[...] += part


_FLIPS = ((1, 0), (0, 1), (1, 1))


class _Ride:
    def __init__(self, name, src, src_at, dst, dst_at, halves=False):
        self.name, self.src, self.src_at, self.dst, self.dst_at, self.halves = name, src, src_at, dst, dst_at, halves
        assert not halves or (src.ndim == 2 and src.shape[0] % 32 == 0), (name, src.shape)


_RIDE_SEMS = lambda n: [pltpu.SemaphoreType.DMA((6, n)), pltpu.SemaphoreType.DMA((6, n)), pltpu.SemaphoreType.DMA((n,))]


def _ride_ops(ride, srcs, dsts, send_sems, recv_sems, local_sems):
    x, y, c = lax.axis_index("x"), lax.axis_index("y"), lax.axis_index("c")
    me = 2 * x + y
    at = lambda ref, idx: ref.at[idx] if idx else ref
    local, sends, arrivals, passes = [], [], [], []
    for t, it in enumerate(ride):
        local.append(pltpu.make_async_copy(at(srcs[t], it.src_at(me)), at(dsts[t], it.dst_at(me)), local_sems.at[t]))
    for r, (fx, fy) in enumerate(_FLIPS):
        px, py = (1 - x) if fx else x, (1 - y) if fy else y
        peer = 2 * px + py
        for t, it in enumerate(ride):
            if it.halves:
                h = it.src.shape[0] // 2
                mine = pl.ds(pl.multiple_of(c * h, 16), h)
                theirs = pl.ds(pl.multiple_of((1 - c) * h, 16), h)
                far = dict(send_sem=send_sems.at[r, t], recv_sem=recv_sems.at[r, t], device_id=(px, py, c), device_id_type=MESH)
                near = dict(send_sem=send_sems.at[3 + r, t], recv_sem=recv_sems.at[3 + r, t], device_id=(x, y, 1 - c),
                            device_id_type=MESH)
                sends.append(pltpu.make_async_remote_copy(src_ref=srcs[t].at[mine], dst_ref=dsts[t].at[me, mine], **far))
                arrivals.append(pltpu.make_async_remote_copy(src_ref=srcs[t].at[mine], dst_ref=dsts[t].at[peer, mine], **far))
                passes.append((pltpu.make_async_remote_copy(src_ref=dsts[t].at[peer, mine], dst_ref=dsts[t].at[peer, mine], **near),
                               pltpu.make_async_remote_copy(src_ref=dsts[t].at[peer, theirs], dst_ref=dsts[t].at[peer, theirs], **near)))
            else:
                far = dict(src_ref=at(srcs[t], it.src_at(peer)), send_sem=send_sems.at[r, t], recv_sem=recv_sems.at[r, t],
                           device_id=(px, py, c), device_id_type=MESH)
                sends.append(pltpu.make_async_remote_copy(dst_ref=at(dsts[t], it.dst_at(me)), **far))
                arrivals.append(pltpu.make_async_remote_copy(dst_ref=at(dsts[t], it.dst_at(peer)), **far))
                passes.append(None)

    def start():
        for cp in local + sends:
            cp.start()

    def finish():
        for cp, arrival, onward in zip(sends, arrivals, passes):
            cp.wait_send()
            arrival.wait_recv()
            if onward is not None:
                onward[0].start()
        for onward in passes:
            if onward is not None:
                onward[0].wait_send()
                onward[1].wait_recv()
        for cp in local:
            cp.wait()

    return start, finish


def _grid_edges(*ns):
    def edges():
        first = last = None
        for d, n in enumerate(ns):
            i = pl.program_id(d)
            f, l = i == 0, i == n - 1
            first = f if first is None else jnp.logical_and(first, f)
            last = l if last is None else jnp.logical_and(last, l)
        return first, last
    return edges


def _ride_call(body, ride, *, name, grid, in_specs, out_specs, out_shape, args, scratch_shapes=(), semantics=(), aliases=None):
    scratch_shapes, aliases = list(scratch_shapes), dict(aliases or {})
    if not ride:
        outs = pl.pallas_call(body, name=name, grid=grid, in_specs=in_specs, out_specs=out_specs, out_shape=out_shape,
                              scratch_shapes=scratch_shapes, input_output_aliases=aliases,
                              compiler_params=_cparams(*semantics))(*args)
        return outs, {}
    n_in, n_out, n_sc, n = len(in_specs), len(out_specs), len(scratch_shapes), len(ride)
    edges = _grid_edges(*grid)

    def wrapped(*refs):
        ins, srcs = refs[:n_in], refs[n_in:n_in + n]
        o0 = n_in + 2 * n
        outs, dsts = refs[o0:o0 + n_out], refs[o0 + n_out:o0 + n_out + n]
        scratch = refs[o0 + n_out + n:o0 + n_out + n + n_sc]
        start, finish = _ride_ops(ride, srcs, dsts, *refs[o0 + n_out + n + n_sc:])
        first, last = edges()
        pl.when(first)(start)
        body(*ins, *outs, *scratch)
        pl.when(last)(finish)

    aliases.update({n_in + n + t: n_out + t for t in range(n)})
    res = pl.pallas_call(
        wrapped, name=name, grid=grid, in_specs=list(in_specs) + [_ANY] * (2 * n), out_specs=list(out_specs) + [_ANY] * n,
        out_shape=list(out_shape) + [jax.ShapeDtypeStruct(it.dst.shape, it.dst.dtype) for it in ride],
        input_output_aliases=aliases, scratch_shapes=scratch_shapes + _RIDE_SEMS(n),
        compiler_params=_cparams(*(["arbitrary"] * len(grid))),
    )(*args, *[it.src for it in ride], *[it.dst for it in ride])
    return res[:n_out], {it.name: o for it, o in zip(ride, res[n_out:])}


def _exchange(ride, name):
    n = len(ride)

    def body(*refs):
        start, finish = _ride_ops(ride, refs[:n], refs[2 * n:3 * n], *refs[3 * n:])
        start()
        finish()

    res = pl.pallas_call(
        body, name=name, in_specs=[_ANY] * (2 * n), out_specs=[_ANY] * n,
        out_shape=[jax.ShapeDtypeStruct(it.dst.shape, it.dst.dtype) for it in ride],
        input_output_aliases={n + t: t for t in range(n)}, scratch_shapes=_RIDE_SEMS(n),
    )(*[it.src for it in ride], *[it.dst for it in ride])
    return {it.name: o for it, o in zip(ride, res)}


def _na_prep(z, c0, gq, gk, name, tm=512):
    S = z.shape[0]
    tm = min(tm, S)
    zv = _V(z, c0, 3 * NA_W)

    def body(z_ref, gq_ref, gk_ref, q_ref, k_ref, v_ref):
        bd = _head_block_ones(NA_W, 6)

        def norm(xv, gv):
            ms = _dot(xv * xv, bd, prec=HI) * (1.0 / NA_DH)
            return xv * lax.rsqrt(ms + EPS) * gv

        q_ref[...] = (norm(z_ref[:, 0:NA_W], gq_ref[...]) * (NA_DH ** -0.5)).astype(BF16)
        k_ref[...] = norm(z_ref[:, NA_W:2 * NA_W], gk_ref[...]).astype(BF16)
        v_ref[...] = z_ref[:, 2 * NA_W:3 * NA_W].astype(BF16)

    blk = pl.BlockSpec((tm, NA_W), lambda i: (i, 0))
    gspec = pl.BlockSpec((1, NA_W), lambda i: (0, 0))
    return pl.pallas_call(
        body, name=name, grid=(S // tm,),
        in_specs=[zv.spec(tm, 3 * NA_W, lambda i: i, lambda i: 0), gspec, gspec],
        out_specs=[blk, blk, blk], out_shape=[jax.ShapeDtypeStruct((S, NA_W), BF16)] * 3,
        compiler_params=_cparams("parallel"),
    )(z, gq, gk)


def _na_prep_bwd(z, c0, gq, gk, dqn, dkn, dv, name, tm=512):
    S = z.shape[0]
    tm = min(tm, S)
    zv = _V(z, c0, 3 * NA_W)

    def body(z_ref, gq_ref, gk_ref, dq_ref, dk_ref, dv_ref, dz_ref, dgq_ref, dgk_ref):
        bd = _head_block_ones(NA_W, 6)
        first = pl.program_id(0) == 0

        def norm_bwd(xv, gv, dy, dg_ref):
            ms = _dot(xv * xv, bd, prec=HI) * (1.0 / NA_DH)
            rstd = lax.rsqrt(ms + EPS)
            xhat = xv * rstd
            dxhat = dy * gv
            proj = _dot(dxhat * xhat, bd, prec=HI) * (1.0 / NA_DH)
            _acc_rows(dg_ref, dy * xhat, first)
            return rstd * (dxhat - xhat * proj)

        dz_ref[:, 0:NA_W] = norm_bwd(z_ref[:, 0:NA_W], gq_ref[...], dq_ref[...] * (NA_DH ** -0.5), dgq_ref).astype(BF16)
        dz_ref[:, NA_W:2 * NA_W] = norm_bwd(z_ref[:, NA_W:2 * NA_W], gk_ref[...], dk_ref[...], dgk_ref).astype(BF16)
        dz_ref[:, 2 * NA_W:3 * NA_W] = dv_ref[...].astype(BF16)

    blk = pl.BlockSpec((tm, NA_W), lambda i: (i, 0))
    gspec = pl.BlockSpec((1, NA_W), lambda i: (0, 0))
    acc = pl.BlockSpec((8, NA_W), lambda i: (0, 0))
    return pl.pallas_call(
        body, name=name, grid=(S // tm,),
        in_specs=[zv.spec(tm, 3 * NA_W, lambda i: i, lambda i: 0), gspec, gspec, blk, blk, blk],
        out_specs=[pl.BlockSpec((tm, 3 * NA_W), lambda i: (i, 0)), acc, acc],
        out_shape=[jax.ShapeDtypeStruct((S, 3 * NA_W), BF16), jax.ShapeDtypeStruct((8, NA_W), F32),
                   jax.ShapeDtypeStruct((8, NA_W), F32)],
        compiler_params=_cparams("arbitrary"),
    )(z, gq, gk, dqn, dkn, dv)


def _na_onehot():
    qc = np.arange(GRID_W)[:, None]
    kc = np.arange(GRID_W)[None, :]
    c0 = np.clip(qc - NA_WIN_C // 2, 0, GRID_W - NA_WIN_C)
    valid = (kc >= c0) & (kc < c0 + NA_WIN_C)
    dc = kc - qc + (NA_WIN_C - 1)
    e = np.zeros((32, GRID_W, GRID_W), np.float32)
    for d in range(2 * NA_WIN_C - 1):
        e[d] = valid & (dc == d)
    return e.reshape(32, GRID_W * GRID_W), valid.reshape(1, -1)


def _rpb_expand(rpb, name):
    e, valid = _na_onehot()
    negmask = np.where(valid, 0.0, NEG).astype(np.float32)
    nd = 2 * NA_WIN_R - 1
    r2 = jnp.pad(rpb.reshape(NA_HEADS * nd, 2 * NA_WIN_C - 1), ((0, 128 - NA_HEADS * nd), (0, 1)))

    def body(r_ref, e_ref, m_ref, o_ref):
        o_ref[...] = _dot(r_ref[...], e_ref[...], prec=HI) + m_ref[...]

    t = pl.pallas_call(body, name=name, out_shape=jax.ShapeDtypeStruct((128, GRID_W * GRID_W), F32))(
        r2, jnp.asarray(e), jnp.asarray(negmask))
    t = t[:NA_HEADS * nd].reshape(NA_HEADS, nd, GRID_W, GRID_W)
    return jnp.stack([jnp.concatenate([t[:, b + w] for w in range(NA_WIN_R)], axis=-1) for b in range(NA_WIN_R)], axis=1)


def _rpb_reduce(dbias, name):
    e, _ = _na_onehot()
    nd = 2 * NA_WIN_R - 1
    et = np.zeros((GRID_W * GRID_W, 128), np.float32)
    et[:, :32] = e.T
    sel = np.zeros((128, NA_HEADS * NA_WIN_R * NA_WIN_R), np.float32)
    for h in range(NA_HEADS):
        for b in range(NA_WIN_R):
            for w in range(NA_WIN_R):
                sel[h * nd + b + w, (h * NA_WIN_R + b) * NA_WIN_R + w] = 1.0
    x = dbias.reshape(NA_HEADS, NA_WIN_R, GRID_W, NA_WIN_R, GRID_W).transpose(0, 1, 3, 2, 4).reshape(-1, GRID_W * GRID_W)

    def body(x_ref, et_ref, sel_ref, o_ref):
        g = _dot(x_ref[...], et_ref[...], prec=HI)
        o_ref[...] = _dot(sel_ref[...], g, prec=HI)

    out = pl.pallas_call(body, name=name, out_shape=jax.ShapeDtypeStruct((128, 128), F32))(x, jnp.asarray(et), jnp.asarray(sel))
    return out[:NA_HEADS * nd, :2 * NA_WIN_C - 1].reshape(NA_HEADS, nd, 2 * NA_WIN_C - 1)


def _na_base(r, rows):
    return jnp.clip(r - NA_WIN_R // 2, 0, rows - NA_WIN_R) - r + (NA_WIN_R - 1)


def _na_scores(q_ref, k_ref, bias_ref, pp, hh, r0w):
    sl = slice(128 * pp, 128 * pp + 128)
    m = _lane_mask(128, 64 * hh, 64)
    qm = (q_ref[:, sl].astype(F32) * m).astype(BF16)
    kw = k_ref[r0w, sl]
    s = _dot(qm, kw, "nt") + bias_ref[2 * pp + hh, 0]
    s = s - jnp.max(s, axis=-1, keepdims=True)
    p = jnp.exp(s)
    p = p / jnp.sum(p, axis=-1, keepdims=True)
    return sl, m, qm, kw, p


NA_FWD_PAIRS = 4
NA_BWD_PAIRS = 2


def _na_attn(qn, kn, vb, bias, name, ride=None):
    S = qn.shape[0]
    rows = S // GRID_W
    nk = NA_WIN_R * GRID_W
    P = NA_FWD_PAIRS
    W = 128 * P

    def body(q_ref, k_ref, v_ref, b_ref, o_ref):
        r = pl.program_id(1)
        r0w = pl.ds(pl.multiple_of(jnp.clip(r - NA_WIN_R // 2, 0, rows - NA_WIN_R) * GRID_W, GRID_W), nk)
        for pp in range(P):
            acc = jnp.zeros((GRID_W, 128), F32)
            for hh in range(2):
                sl, m, _, _, p = _na_scores(q_ref, k_ref, b_ref, pp, hh, r0w)
                acc = acc + _dot(p.astype(BF16), v_ref[r0w, sl]) * m
            o_ref[:, sl] = acc.astype(BF16)

    full = pl.BlockSpec((S, W), lambda g, r: (0, g))
    (o,), got = _ride_call(
        body, ride, name=name, grid=(NA_HEADS // (2 * P), rows),
        in_specs=[pl.BlockSpec((GRID_W, W), lambda g, r: (r, g)), full, full,
                  pl.BlockSpec((2 * P, 1, GRID_W, nk), lambda g, r: (g, _na_base(r, rows), 0, 0))],
        out_specs=[pl.BlockSpec((GRID_W, W), lambda g, r: (r, g))],
        out_shape=[jax.ShapeDtypeStruct((S, NA_W), BF16)], args=(qn, kn, vb, bias), semantics=("parallel", "arbitrary"))
    return o, got


def _na_attn_bwd(qn, kn, vb, bias, do, name, ride=None):
    S = qn.shape[0]
    rows = S // GRID_W
    nk = NA_WIN_R * GRID_W
    P = NA_BWD_PAIRS
    W = 128 * P

    def body(q_ref, k_ref, v_ref, b_ref, do_ref, dq_ref, dk_ref, dv_ref, db_ref):
        r = pl.program_id(1)

        @pl.when(r == 0)
        def _():
            dk_ref[...] = jnp.zeros_like(dk_ref)
            dv_ref[...] = jnp.zeros_like(dv_ref)

        r0w = pl.ds(pl.multiple_of(jnp.clip(r - NA_WIN_R // 2, 0, rows - NA_WIN_R) * GRID_W, GRID_W), nk)
        fresh = jnp.logical_or(r <= NA_WIN_R // 2, r > rows - NA_WIN_R // 2)
        for pp in range(P):
            dq = jnp.zeros((GRID_W, 128), F32)
            dkw = jnp.zeros((nk, 128), F32)
            dvw = jnp.zeros((nk, 128), F32)
            for hh in range(2):
                sl, m, qm, kw, p = _na_scores(q_ref, k_ref, b_ref, pp, hh, r0w)
                dom = (do_ref[:, sl].astype(F32) * m).astype(BF16)
                dp = _dot(dom, v_ref[r0w, sl], "nt")
                ds = p * (dp - jnp.sum(p * dp, axis=-1, keepdims=True))
                h = 2 * pp + hh

                @pl.when(fresh)
                def _():
                    db_ref[h, 0] = ds

                @pl.when(jnp.logical_not(fresh))
                def _():
                    db_ref[h, 0] += ds

                dsb = ds.astype(BF16)
                dq = dq + _dot(dsb, kw) * m
                dkw = dkw + _dot(dsb, qm, "tn")
                dvw = dvw + _dot(p.astype(BF16), dom, "tn")
            dq_ref[:, sl] = dq
            dk_ref[r0w, sl] += dkw
            dv_ref[r0w, sl] += dvw

    qblk = pl.BlockSpec((GRID_W, W), lambda g, r: (r, g))
    full = pl.BlockSpec((S, W), lambda g, r: (0, g))
    bblk = pl.BlockSpec((2 * P, 1, GRID_W, nk), lambda g, r: (g, _na_base(r, rows), 0, 0))
    return _ride_call(
        body, ride, name=name, grid=(NA_HEADS // (2 * P), rows),
        in_specs=[qblk, full, full, bblk, qblk], out_specs=[qblk, full, full, bblk],
        out_shape=[jax.ShapeDtypeStruct((S, NA_W), F32)] * 3 + [jax.ShapeDtypeStruct((NA_HEADS, NA_WIN_R, GRID_W, nk), F32)],
        args=(qn, kn, vb, bias, do), semantics=("parallel", "arbitrary"))


def _logsig(x):
    return jnp.minimum(x, 0.0) - jnp.log(1.0 + jnp.exp(-jnp.abs(x)))


def _gla_gates(z, c0, wg, bias, name, tm=512):
    S = z.shape[0]
    tm = min(tm, S)
    zv = _V(z, c0, 128)
    W = 2 * GLA_HEADS * GLA_DK

    def body(z_ref, w_ref, b_ref, o_ref):
        pre = _dot(z_ref[...].astype(BF16), w_ref[...]) + b_ref[...]
        o_ref[...] = _logsig(pre) * (1.0 / GLA_TAU)

    return pl.pallas_call(
        body, name=name, grid=(S // tm,),
        in_specs=[zv.spec(tm, 128, lambda i: i, lambda i: 0), pl.BlockSpec((128, W), lambda i: (0, 0)),
                  pl.BlockSpec((1, W), lambda i: (0, 0))],
        out_specs=pl.BlockSpec((tm, W), lambda i: (i, 0)), out_shape=jax.ShapeDtypeStruct((S, W), F32),
        compiler_params=_cparams("parallel"),
    )(z, wg, bias)


def _gla_gates_bwd(z, c0, wg, bias, dg_f, dg_b, name, tm=512):
    S = z.shape[0]
    tm = min(tm, S)
    zv = _V(z, c0, 128)
    W = 2 * GLA_HEADS * GLA_DK

    def body(z_ref, w_ref, b_ref, dgf_ref, dgb_ref, dp_ref, db_ref):
        pre = _dot(z_ref[...].astype(BF16), w_ref[...]) + b_ref[...]
        dg = jnp.concatenate([dgf_ref[...], dgb_ref[...]], axis=-1)
        dpre = dg * (1.0 / GLA_TAU) * jax.nn.sigmoid(-pre)
        dp_ref[...] = dpre.astype(BF16)
        _acc_rows(db_ref, dpre, pl.program_id(0) == 0)

    half = pl.BlockSpec((tm, W // 2), lambda i: (i, 0))
    return pl.pallas_call(
        body, name=name, grid=(S // tm,),
        in_specs=[zv.spec(tm, 128, lambda i: i, lambda i: 0), pl.BlockSpec((128, W), lambda i: (0, 0)),
                  pl.BlockSpec((1, W), lambda i: (0, 0)), half, half],
        out_specs=[pl.BlockSpec((tm, W), lambda i: (i, 0)), pl.BlockSpec((8, W), lambda i: (0, 0))],
        out_shape=[jax.ShapeDtypeStruct((S, W), BF16), jax.ShapeDtypeStruct((8, W), F32)],
        compiler_params=_cparams("arbitrary"),
    )(z, wg, bias, dg_f, dg_b)


def _gla_chunk_terms(zqk, g, p, rev):
    C = GLA_CHUNK
    i, j = _iota((C, C), 0), _iota((C, C), 1)
    cum = jnp.where((j >= i) if rev else (j <= i), 1.0, 0.0).astype(F32)
    q2 = zqk[:, 128 * p:128 * p + 128] * (GLA_DK ** -0.5)
    k2 = zqk[:, 256 + 128 * p:256 + 128 * p + 128]
    b2 = _dot(cum, g[:, 128 * p:128 * p + 128], prec=HI)
    bl2 = b2[0:1] if rev else b2[C - 1:C]
    eb = jnp.exp(b2)
    qe2 = q2 * eb
    ke2 = k2 * jnp.exp(-b2)
    kend2 = k2 * jnp.exp(bl2 - b2)
    dec2 = jnp.exp(bl2)
    tri = (j > i) if rev else (j <= i)
    return b2, bl2, eb, qe2, ke2, kend2, dec2, tri


def _row_to_col(row):
    eye = _iota((128, 128), 0) == _iota((128, 128), 1)
    return jnp.sum(jnp.where(eye, row, 0.0), axis=1, keepdims=True)


def _col_to_row(col):
    eye = _iota((128, 128), 0) == _iota((128, 128), 1)
    return jnp.sum(jnp.where(eye, col, 0.0), axis=0, keepdims=True)


GLA_GROUP = 4


def _gla_fwd(z, c_qk, c_v, gfb, name, ride=None):
    S = z.shape[0]
    C = GLA_CHUNK
    n = S // C
    G = math.gcd(GLA_GROUP, n)
    nb, GC = n // G, G * C
    WQK = 2 * GLA_HEADS * GLA_DK
    WV = GLA_HEADS * GLA_DV
    zqk, zvv = _V(z, c_qk, WQK), _V(z, c_v, WV)

    def body(qkf_ref, vf_ref, gf_ref, qkb_ref, vb_ref, gb_ref, of_ref, ob_ref, sf_ref, sb_ref, stf, stb):
        @pl.when(pl.program_id(0) == 0)
        def _():
            stf[...] = jnp.zeros_like(stf)
            stb[...] = jnp.zeros_like(stb)

        for rev, qk_ref, v_ref, g_ref, o_ref, s_ref, st in ((False, qkf_ref, vf_ref, gf_ref, of_ref, sf_ref, stf),
                                                            (True, qkb_ref, vb_ref, gb_ref, ob_ref, sb_ref, stb)):
            state = [st[h] for h in range(GLA_HEADS)]
            for gi in (reversed(range(G)) if rev else range(G)):
                rs = slice(gi * C, (gi + 1) * C)
                zqkv, gv = qk_ref[rs, :], g_ref[rs, :]
                for p in range(GLA_HEADS // 2):
                    _, _, _, qe2, ke2, kend2, dec2, tri = _gla_chunk_terms(zqkv, gv, p, rev)
                    dec_col = _row_to_col(dec2)
                    keb = ke2.astype(BF16)
                    for hh in range(2):
                        h = 2 * p + hh
                        m = _lane_mask(128, 64 * hh, 64)
                        qm = (qe2 * m).astype(BF16)
                        a = jnp.where(tri, _dot(qm, keb, "nt"), 0.0)
                        vh = v_ref[rs, 128 * h:128 * h + 128].astype(BF16)
                        sp = state[h]
                        o_ref[rs, 128 * h:128 * h + 128] = _dot(a.astype(BF16), vh) + _dot(qm, sp.astype(BF16))
                        s_ref[gi, h] = sp
                        state[h] = dec_col * sp + _dot((kend2 * m).astype(BF16), vh, "tn")
            for h in range(GLA_HEADS):
                st[h] = state[h]

    fw = lambda i: i
    bw = lambda i: nb - 1 - i
    zero = lambda i: 0
    in_specs = []
    for ix, col in ((fw, 0), (bw, 1)):
        in_specs += [zqk.spec(GC, WQK, ix, zero), zvv.spec(GC, WV, ix, zero),
                     pl.BlockSpec((GC, WQK // 2), functools.partial(lambda i, ix, col: (ix(i), col), ix=ix, col=col))]
    return _ride_call(
        body, ride, name=name, grid=(nb,), in_specs=in_specs,
        out_specs=[pl.BlockSpec((GC, WV), lambda i: (i, 0)), pl.BlockSpec((GC, WV), lambda i: (nb - 1 - i, 0)),
                   pl.BlockSpec((G, GLA_HEADS, 128, 128), lambda i: (i, 0, 0, 0)),
                   pl.BlockSpec((G, GLA_HEADS, 128, 128), lambda i: (nb - 1 - i, 0, 0, 0))],
        out_shape=[jax.ShapeDtypeStruct((S, WV), F32)] * 2 + [jax.ShapeDtypeStruct((n, GLA_HEADS, 128, 128), F32)] * 2,
        scratch_shapes=[pltpu.VMEM((GLA_HEADS, 128, 128), F32)] * 2, args=(z, z, gfb, z, z, gfb), semantics=("arbitrary",))


def _gla_bwd(z, c_qk, c_v, gfb, do, s_f, s_b, name, ride=None):
    S = z.shape[0]
    C = GLA_CHUNK
    n = S // C
    G = math.gcd(GLA_GROUP, n)
    nb, GC = n // G, G * C
    WQK = 2 * GLA_HEADS * GLA_DK
    WV = GLA_HEADS * GLA_DV
    zqk, zvv = _V(z, c_qk, WQK), _V(z, c_v, WV)

    def body(qkf_ref, vf_ref, gf_ref, dof_ref, sf_ref, qkb_ref, vb_ref, gb_ref, dob_ref, sb_ref,
             dqkf_ref, dvf_ref, dgf_ref, dqkb_ref, dvb_ref, dgb_ref, dstf, dstb):
        @pl.when(pl.program_id(0) == 0)
        def _():
            dstf[...] = jnp.zeros_like(dstf)
            dstb[...] = jnp.zeros_like(dstb)

        dirs = ((False, qkf_ref, vf_ref, gf_ref, dof_ref, sf_ref, dqkf_ref, dvf_ref, dgf_ref, dstf),
                (True, qkb_ref, vb_ref, gb_ref, dob_ref, sb_ref, dqkb_ref, dvb_ref, dgb_ref, dstb))
        for rev, qk_ref, v_ref, g_ref, do_ref, s_ref, dqk_ref, dv_ref, dg_ref, dst_ref in dirs:
          dst = [dst_ref[h] for h in range(GLA_HEADS)]
          for gi in (range(G) if rev else reversed(range(G))):
            rs = slice(gi * C, (gi + 1) * C)
            zqkv, gv = qk_ref[rs, :], g_ref[rs, :]
            i, j = _iota((C, C), 0), _iota((C, C), 1)
            cum_t = jnp.where((j <= i) if rev else (j >= i), 1.0, 0.0).astype(F32)
            edge = _iota((C, 128), 0) == (0 if rev else C - 1)
            for p in range(GLA_HEADS // 2):
                b2, bl2, eb, qe2, ke2, kend2, dec2, tri = _gla_chunk_terms(zqkv, gv, p, rev)
                dec_col = _row_to_col(dec2)
                keb = ke2.astype(BF16)
                dqe2 = jnp.zeros((C, 128), F32)
                dke2 = jnp.zeros((C, 128), F32)
                dkend2 = jnp.zeros((C, 128), F32)
                ddec2 = jnp.zeros((1, 128), F32)
                for hh in range(2):
                    h = 2 * p + hh
                    m = _lane_mask(128, 64 * hh, 64)
                    qm = (qe2 * m).astype(BF16)
                    kem = (ke2 * m).astype(BF16)
                    kendm = (kend2 * m).astype(BF16)
                    a = jnp.where(tri, _dot(qm, keb, "nt"), 0.0).astype(BF16)
                    vh = v_ref[rs, 128 * h:128 * h + 128].astype(BF16)
                    doh = do_ref[rs, 128 * h:128 * h + 128].astype(BF16)
                    sp = s_ref[gi, h]
                    spb = sp.astype(BF16)
                    ds = dst[h]
                    dsb = ds.astype(BF16)
                    da = jnp.where(tri, _dot(doh, vh, "nt"), 0.0).astype(BF16)
                    dqe2 = dqe2 + _dot(da, kem) + _dot(doh, spb, "nt")
                    dke2 = dke2 + _dot(da, qm, "tn")
                    dv_ref[rs, 128 * h:128 * h + 128] = _dot(a, doh, "tn") + _dot(kendm, dsb)
                    dkend2 = dkend2 + _dot(vh, dsb, "nt") * m
                    ddec2 = ddec2 + _col_to_row(jnp.sum(ds * sp, axis=1, keepdims=True))
                    dst[h] = dec_col * ds + _dot(qm, doh, "tn")
                dqk_ref[rs, 128 * p:128 * p + 128] = dqe2 * eb * (GLA_DK ** -0.5)
                dqk_ref[rs, 256 + 128 * p:256 + 128 * p + 128] = dke2 * jnp.exp(-b2) + dkend2 * jnp.exp(bl2 - b2)
                dkk = dkend2 * kend2
                db2 = dqe2 * qe2 - dke2 * ke2 - dkk
                dbl2 = jnp.sum(dkk, axis=0, keepdims=True) + ddec2 * dec2
                db2 = db2 + jnp.where(edge, dbl2, 0.0)
                dg_ref[rs, 128 * p:128 * p + 128] = _dot(cum_t, db2, prec=HI)
          for h in range(GLA_HEADS):
            dst_ref[h] = dst[h]

    fw = lambda i: nb - 1 - i
    bw = lambda i: i
    zero = lambda i: 0
    in_specs, out_specs = [], []
    for ix, col in ((fw, 0), (bw, 1)):
        blk = functools.partial(lambda i, ix: (ix(i), 0), ix=ix)
        in_specs += [zqk.spec(GC, WQK, ix, zero), zvv.spec(GC, WV, ix, zero),
                     pl.BlockSpec((GC, WQK // 2), functools.partial(lambda i, ix, col: (ix(i), col), ix=ix, col=col)),
                     pl.BlockSpec((GC, WV), blk),
                     pl.BlockSpec((G, GLA_HEADS, 128, 128), functools.partial(lambda i, ix: (ix(i), 0, 0, 0), ix=ix))]
        out_specs += [pl.BlockSpec((GC, WQK), blk), pl.BlockSpec((GC, WV), blk), pl.BlockSpec((GC, WQK // 2), blk)]
    shapes = [jax.ShapeDtypeStruct((S, WQK), F32), jax.ShapeDtypeStruct((S, WV), F32), jax.ShapeDtypeStruct((S, WQK // 2), F32)]
    return _ride_call(
        body, ride, name=name, grid=(nb,), in_specs=in_specs, out_specs=out_specs, out_shape=shapes * 2,
        scratch_shapes=[pltpu.VMEM((GLA_HEADS, 128, 128), F32)] * 2, args=(z, z, gfb, do, s_f, z, z, gfb, do, s_b),
        semantics=("arbitrary",))


def _gla_post(o_f, o_b, z, c_r, gn, name, tm=512):
    S, WV = o_f.shape
    tm = min(tm, S)
    zr = _V(z, c_r, WV)

    def body(of_ref, ob_ref, r_ref, g_ref, y_ref):
        gr = r_ref[...]
        sil = gr * jax.nn.sigmoid(gr)
        for h in range(GLA_HEADS):
            sl = slice(GLA_DV * h, GLA_DV * (h + 1))
            o = of_ref[:, sl] + ob_ref[:, sl]
            on = o * lax.rsqrt(jnp.mean(o * o, axis=-1, keepdims=True) + EPS) * g_ref[...]
            y_ref[:, sl] = (on * sil[:, sl]).astype(BF16)

    blk = pl.BlockSpec((tm, WV), lambda i: (i, 0))
    return pl.pallas_call(
        body, name=name, grid=(S // tm,),
        in_specs=[blk, blk, zr.spec(tm, WV, lambda i: i, lambda i: 0), pl.BlockSpec((1, GLA_DV), lambda i: (0, 0))],
        out_specs=blk, out_shape=jax.ShapeDtypeStruct((S, WV), BF16), compiler_params=_cparams("parallel"),
    )(o_f, o_b, z, gn)


def _gla_post_bwd(o_f, o_b, z, c_r, gn, dy, name, tm=512):
    S, WV = o_f.shape
    tm = min(tm, S)
    zr = _V(z, c_r, WV)

    def body(of_ref, ob_ref, r_ref, g_ref, dy_ref, do_ref, dr_ref, dg_ref):
        gr = r_ref[...]
        sig = jax.nn.sigmoid(gr)
        sil = gr * sig
        dyv = dy_ref[...].astype(F32)
        dgn = jnp.zeros((tm, GLA_DV), F32)
        for h in range(GLA_HEADS):
            sl = slice(GLA_DV * h, GLA_DV * (h + 1))
            o = of_ref[:, sl] + ob_ref[:, sl]
            rstd = lax.rsqrt(jnp.mean(o * o, axis=-1, keepdims=True) + EPS)
            xhat = o * rstd
            don = dyv[:, sl] * sil[:, sl]
            dr_ref[:, sl] = (dyv[:, sl] * xhat * g_ref[...] * (sig[:, sl] * (1.0 + gr[:, sl] * (1.0 - sig[:, sl])))).astype(BF16)
            dxhat = don * g_ref[...]
            do_ref[:, sl] = rstd * (dxhat - xhat * jnp.mean(dxhat * xhat, axis=-1, keepdims=True))
            dgn = dgn + don * xhat
        _acc_rows(dg_ref, dgn, pl.program_id(0) == 0)

    blk = pl.BlockSpec((tm, WV), lambda i: (i, 0))
    return pl.pallas_call(
        body, name=name, grid=(S // tm,),
        in_specs=[blk, blk, zr.spec(tm, WV, lambda i: i, lambda i: 0), pl.BlockSpec((1, GLA_DV), lambda i: (0, 0)), blk],
        out_specs=[blk, blk, pl.BlockSpec((8, GLA_DV), lambda i: (0, 0))],
        out_shape=[jax.ShapeDtypeStruct((S, WV), F32), jax.ShapeDtypeStruct((S, WV), BF16), jax.ShapeDtypeStruct((8, GLA_DV), F32)],
        compiler_params=_cparams("arbitrary"),
    )(o_f, o_b, z, gn, dy)


def _gla_assemble(dqk_f, dqk_b, dv_f, dv_b, dgr, name, tm=512):
    S = dqk_f.shape[0]
    tm = min(tm, S)

    def body(a_ref, b_ref, c_ref, d_ref, r_ref, o_ref):
        o_ref[:, 0:512] = (a_ref[...] + b_ref[...]).astype(BF16)
        o_ref[:, 512:1024] = (c_ref[...] + d_ref[...]).astype(BF16)
        o_ref[:, 1024:1536] = r_ref[...]

    blk = pl.BlockSpec((tm, 512), lambda i: (i, 0))
    return pl.pallas_call(
        body, name=name, grid=(S // tm,), in_specs=[blk] * 5, out_specs=pl.BlockSpec((tm, 1536), lambda i: (i, 0)),
        out_shape=jax.ShapeDtypeStruct((S, 1536), BF16), compiler_params=_cparams("parallel"),
    )(dqk_f, dqk_b, dv_f, dv_b, dgr)


def _rope(r, cos, sg):
    return r * cos + pltpu.roll(r, 64, 1) * sg


def _unrope(dy, cos, sg):
    return dy * cos + pltpu.roll(dy * sg, 64, 1)


def _mla_prep(z, c_q, c_kr, wuq, wukv, g_cq, g_ckv, g_q, g_k, cos, sg, name, tm=256):
    S = z.shape[0]
    tm = min(tm, S)
    zc, zk = _V(z, c_q, 2 * MLA_RANK), _V(z, c_kr, 128)
    inv = 1.0 / MLA_QK

    def body(zc_ref, zk_ref, wuq_ref, wukv_ref, gcq_ref, gckv_ref, gq_ref, gk_ref, cos_ref, sg_ref,
             q_ref, k_ref, v_ref, cqn_ref, ckvn_ref):
        def norm(xv, gv):
            return (xv * lax.rsqrt(jnp.mean(xv * xv, axis=-1, keepdims=True) + EPS) * gv).astype(BF16)

        cqn = norm(zc_ref[:, 0:MLA_RANK], gcq_ref[...])
        ckvn = norm(zc_ref[:, MLA_RANK:2 * MLA_RANK], gckv_ref[...])
        cqn_ref[...] = cqn
        ckvn_ref[...] = ckvn
        qf = _dot(cqn, wuq_ref[...])
        kv = _dot(ckvn, wukv_ref[...])
        kr = zk_ref[...]
        krss = jnp.sum(kr * kr, axis=-1, keepdims=True)
        cosv, sgv = cos_ref[...], sg_ref[...]
        gq, gk = gq_ref[...], gk_ref[...]
        for h in range(MLA_HEADS):
            qh = qf[:, MLA_SLOT * h:MLA_SLOT * (h + 1)]
            qhn = qh * lax.rsqrt(jnp.sum(qh * qh, axis=-1, keepdims=True) * inv + EPS) * gq
            q_ref[:, MLA_SLOT * h:MLA_SLOT * h + 128] = qhn[:, 0:128].astype(BF16)
            q_ref[:, MLA_SLOT * h + 128:MLA_SLOT * (h + 1)] = _rope(qhn[:, 128:256], cosv, sgv).astype(BF16)
            kn = kv[:, 256 * h:256 * h + 128]
            rstd = lax.rsqrt((jnp.sum(kn * kn, axis=-1, keepdims=True) + krss) * inv + EPS)
            k_ref[:, MLA_SLOT * h:MLA_SLOT * h + 128] = (kn * rstd * gk[:, 0:128]).astype(BF16)
            k_ref[:, MLA_SLOT * h + 128:MLA_SLOT * (h + 1)] = _rope(kr * rstd * gk[:, 128:256], cosv, sgv).astype(BF16)
            v_ref[:, 128 * h:128 * (h + 1)] = kv[:, 256 * h + 128:256 * (h + 1)].astype(BF16)

    row = lambda w: pl.BlockSpec((tm, w), lambda i: (i, 0))
    const = lambda r, w: pl.BlockSpec((r, w), lambda i: (0, 0))
    W = MLA_HEADS * MLA_SLOT
    return pl.pallas_call(
        body, name=name, grid=(S // tm,),
        in_specs=[zc.spec(tm, 2 * MLA_RANK, lambda i: i, lambda i: 0), zk.spec(tm, 128, lambda i: i, lambda i: 0),
                  const(MLA_RANK, W), const(MLA_RANK, W), const(1, MLA_RANK), const(1, MLA_RANK), const(1, MLA_SLOT),
                  const(1, MLA_SLOT), row(128), row(128)],
        out_specs=[row(W), row(W), row(MLA_HEADS * MLA_V), row(MLA_RANK), row(MLA_RANK)],
        out_shape=[jax.ShapeDtypeStruct((S, W), BF16), jax.ShapeDtypeStruct((S, W), BF16),
                   jax.ShapeDtypeStruct((S, MLA_HEADS * MLA_V), BF16), jax.ShapeDtypeStruct((S, MLA_RANK), BF16),
                   jax.ShapeDtypeStruct((S, MLA_RANK), BF16)],
        compiler_params=_cparams("parallel"),
    )(z, z, wuq, wukv, g_cq, g_ckv, g_q, g_k, cos, sg)


def _mla_prep_bwd(z, c_kr, cqn, ckvn, wuq, wukv, g_q, g_k, cos, sg, dq, dk, dv, name, tm=256):
    S = z.shape[0]
    tm = min(tm, S)
    zk = _V(z, c_kr, 128)
    inv = 1.0 / MLA_QK

    def body(zk_ref, cqn_ref, ckvn_ref, wuq_ref, wukv_ref, gq_ref, gk_ref, cos_ref, sg_ref, dq_ref, dk_ref, dv_ref,
             dqf_ref, dkv_ref, dkr_ref, dgq_ref, dgk_ref):
        first = pl.program_id(0) == 0
        qf = _dot(cqn_ref[...], wuq_ref[...])
        kv = _dot(ckvn_ref[...], wukv_ref[...])
        kr = zk_ref[...]
        krss = jnp.sum(kr * kr, axis=-1, keepdims=True)
        cosv, sgv = cos_ref[...], sg_ref[...]
        gq, gk = gq_ref[...], gk_ref[...]
        dkr = jnp.zeros((tm, 128), F32)
        dgq = jnp.zeros((tm, MLA_SLOT), F32)
        dgkn = jnp.zeros((tm, 128), F32)
        dgkr = jnp.zeros((tm, 128), F32)
        for h in range(MLA_HEADS):
            qh = qf[:, MLA_SLOT * h:MLA_SLOT * (h + 1)]
            rstd = lax.rsqrt(jnp.sum(qh * qh, axis=-1, keepdims=True) * inv + EPS)
            xhat = qh * rstd
            dyn = jnp.concatenate([dq_ref[:, MLA_SLOT * h:MLA_SLOT * h + 128],
                                   _unrope(dq_ref[:, MLA_SLOT * h + 128:MLA_SLOT * (h + 1)], cosv, sgv)], axis=-1)
            dxhat = dyn * gq
            dqf_ref[:, MLA_SLOT * h:MLA_SLOT * (h + 1)] = (
                rstd * (dxhat - xhat * (jnp.sum(dxhat * xhat, axis=-1, keepdims=True) * inv))).astype(BF16)
            dgq = dgq + dyn * xhat

            kn = kv[:, 256 * h:256 * h + 128]
            rstd = lax.rsqrt((jnp.sum(kn * kn, axis=-1, keepdims=True) + krss) * inv + EPS)
            xn, xr = kn * rstd, kr * rstd
            dyn_n = dk_ref[:, MLA_SLOT * h:MLA_SLOT * h + 128]
            dyn_r = _unrope(dk_ref[:, MLA_SLOT * h + 128:MLA_SLOT * (h + 1)], cosv, sgv)
            dxn, dxr = dyn_n * gk[:, 0:128], dyn_r * gk[:, 128:256]
            proj = (jnp.sum(dxn * xn, axis=-1, keepdims=True) + jnp.sum(dxr * xr, axis=-1, keepdims=True)) * inv
            dkv_ref[:, 256 * h:256 * h + 128] = (rstd * (dxn - xn * proj)).astype(BF16)
            dkv_ref[:, 256 * h + 128:256 * (h + 1)] = dv_ref[:, 128 * h:128 * (h + 1)].astype(BF16)
            dkr = dkr + rstd * (dxr - xr * proj)
            dgkn = dgkn + dyn_n * xn
            dgkr = dgkr + dyn_r * xr
        dkr_ref[...] = dkr.astype(BF16)
        _acc_rows(dgq_ref, dgq, first)
        _acc_rows(dgk_ref, jnp.concatenate([dgkn, dgkr], axis=-1), first)

    row = lambda w: pl.BlockSpec((tm, w), lambda i: (i, 0))
    const = lambda r, w: pl.BlockSpec((r, w), lambda i: (0, 0))
    W = MLA_HEADS * MLA_SLOT
    return pl.pallas_call(
        body, name=name, grid=(S // tm,),
        in_specs=[zk.spec(tm, 128, lambda i: i, lambda i: 0), row(MLA_RANK), row(MLA_RANK), const(MLA_RANK, W),
                  const(MLA_RANK, W), const(1, MLA_SLOT), const(1, MLA_SLOT), row(128), row(128), row(W), row(W),
                  row(MLA_HEADS * MLA_V)],
        out_specs=[row(W), row(W), row(128), const(8, MLA_SLOT), const(8, MLA_SLOT)],
        out_shape=[jax.ShapeDtypeStruct((S, W), BF16), jax.ShapeDtypeStruct((S, W), BF16), jax.ShapeDtypeStruct((S, 128), BF16),
                   jax.ShapeDtypeStruct((8, MLA_SLOT), F32), jax.ShapeDtypeStruct((8, MLA_SLOT), F32)],
        compiler_params=_cparams("arbitrary"),
    )(z, cqn, ckvn, wuq, wukv, g_q, g_k, cos, sg, dq, dk, dv)


def _softmax_rows(s):
    s = s - jnp.max(s, axis=-1, keepdims=True)
    p = jnp.exp(s)
    return p / jnp.sum(p, axis=-1, keepdims=True)


def _mla_attn(q, k, v, name, tq=256, ride=None):
    S = q.shape[0]
    tq = min(tq, S)
    scale = MLA_QK ** -0.5

    def body(q_ref, k_ref, v_ref, o_ref):
        p = _softmax_rows(_dot(q_ref[...], k_ref[...], "nt") * scale)
        o_ref[...] = _dot(p.astype(BF16), v_ref[...]).astype(BF16)

    (o,), got = _ride_call(
        body, ride, name=name, grid=(MLA_HEADS, S // tq),
        in_specs=[pl.BlockSpec((tq, MLA_SLOT), lambda h, i: (i, h)), pl.BlockSpec((S, MLA_SLOT), lambda h, i: (0, h)),
                  pl.BlockSpec((S, MLA_V), lambda h, i: (0, h))],
        out_specs=[pl.BlockSpec((tq, MLA_V), lambda h, i: (i, h))],
        out_shape=[jax.ShapeDtypeStruct((S, MLA_HEADS * MLA_V), BF16)], args=(q, k, v), semantics=("parallel", "parallel"))
    return o, got


def _mla_attn_bwd(q, k, v, do, name, tq=256, ride=None):
    S = q.shape[0]
    tq = min(tq, S)
    scale = MLA_QK ** -0.5

    def body(q_ref, k_ref, v_ref, do_ref, dq_ref, dk_ref, dv_ref):
        @pl.when(pl.program_id(1) == 0)
        def _():
            dk_ref[...] = jnp.zeros_like(dk_ref)
            dv_ref[...] = jnp.zeros_like(dv_ref)

        qv, kvv, dov = q_ref[...], k_ref[...], do_ref[...]
        p = _softmax_rows(_dot(qv, kvv, "nt") * scale)
        dp = _dot(dov, v_ref[...], "nt")
        ds = (p * (dp - jnp.sum(p * dp, axis=-1, keepdims=True)) * scale).astype(BF16)
        dq_ref[...] = _dot(ds, kvv)
        dk_ref[...] += _dot(ds, qv, "tn")
        dv_ref[...] += _dot(p.astype(BF16), dov, "tn")

    W = MLA_HEADS * MLA_SLOT
    return _ride_call(
        body, ride, name=name, grid=(MLA_HEADS, S // tq),
        in_specs=[pl.BlockSpec((tq, MLA_SLOT), lambda h, i: (i, h)), pl.BlockSpec((S, MLA_SLOT), lambda h, i: (0, h)),
                  pl.BlockSpec((S, MLA_V), lambda h, i: (0, h)), pl.BlockSpec((tq, MLA_V), lambda h, i: (i, h))],
        out_specs=[pl.BlockSpec((tq, MLA_SLOT), lambda h, i: (i, h)), pl.BlockSpec((S, MLA_SLOT), lambda h, i: (0, h)),
                   pl.BlockSpec((S, MLA_V), lambda h, i: (0, h))],
        out_shape=[jax.ShapeDtypeStruct((S, W), F32), jax.ShapeDtypeStruct((S, W), F32),
                   jax.ShapeDtypeStruct((S, MLA_HEADS * MLA_V), F32)],
        args=(q, k, v, do), semantics=("parallel", "arbitrary"))


def _merge(ys, ws, z, name, tm=256):
    S = z.shape[0]
    D = ws[0].shape[1]
    tm = min(tm, S)
    zg = _V(z, 0, 3 * D)

    def body(y0, y1, y2, w0, w1, w2, g_ref, m_ref, p0, p1, p2):
        acc = jnp.zeros((tm, D), F32)
        for i, (y_ref, w_ref, p_ref) in enumerate(((y0, w0, p0), (y1, w1, p1), (y2, w2, p2))):
            pv = _dot(y_ref[...], w_ref[...])
            p_ref[...] = pv.astype(BF16)
            acc = acc + jax.nn.sigmoid(g_ref[:, D * i:D * (i + 1)]) * pv
        m_ref[...] = acc.astype(BF16)

    yb = pl.BlockSpec((tm, ys[0].shape[1]), lambda i: (i, 0))
    wb = pl.BlockSpec(ws[0].shape, lambda i: (0, 0))
    ob = pl.BlockSpec((tm, D), lambda i: (i, 0))
    return pl.pallas_call(
        body, name=name, grid=(S // tm,), in_specs=[yb] * 3 + [wb] * 3 + [zg.spec(tm, 3 * D, lambda i: i, lambda i: 0)],
        out_specs=[ob] * 4, out_shape=[jax.ShapeDtypeStruct((S, D), BF16)] * 4, compiler_params=_cparams("parallel"),
    )(*ys, *ws, z)


def _merge_bwd(dmixed, ps, z, name, tm=256):
    S, D = dmixed.shape
    tm = min(tm, S)
    zg = _V(z, 0, 3 * D)

    def body(dm_ref, p0, p1, p2, g_ref, d0, d1, d2, dg_ref):
        dm = dm_ref[...]
        for i, (p_ref, d_ref) in enumerate(((p0, d0), (p1, d1), (p2, d2))):
            gt = jax.nn.sigmoid(g_ref[:, D * i:D * (i + 1)])
            d_ref[...] = (dm * gt).astype(BF16)
            dg_ref[:, D * i:D * (i + 1)] = (dm * p_ref[...].astype(F32) * gt * (1.0 - gt)).astype(BF16)

    ob = pl.BlockSpec((tm, D), lambda i: (i, 0))
    return pl.pallas_call(
        body, name=name, grid=(S // tm,), in_specs=[ob] * 4 + [zg.spec(tm, 3 * D, lambda i: i, lambda i: 0)],
        out_specs=[ob] * 3 + [pl.BlockSpec((tm, 3 * D), lambda i: (i, 0))],
        out_shape=[jax.ShapeDtypeStruct((S, D), BF16)] * 3 + [jax.ShapeDtypeStruct((S, 3 * D), BF16)],
        compiler_params=_cparams("parallel"),
    )(dmixed, *ps, z)


def _loss_head(y, target, name, tm=512):
    S, D = y.shape
    tm = min(tm, S)

    def body(y_ref, t_ref, dy_ref, l_ref):
        e = y_ref[...] - t_ref[...]
        dy_ref[...] = e * (1.0 / D)
        sq = e * e
        part = jnp.sum(sq.reshape(tm // 8, 8, D), axis=0)
        part = jnp.sum(part.reshape(8, D // 128, 128), axis=1) * (0.5 / D)

        @pl.when(pl.program_id(0) == 0)
        def _():
            l_ref[...] = part

        @pl.when(pl.program_id(0) != 0)
        def _():
            l_ref[...] += part

    blk = pl.BlockSpec((tm, D), lambda i: (i, 0))
    return pl.pallas_call(
        body, name=name, grid=(S // tm,), in_specs=[blk, blk], out_specs=[blk, pl.BlockSpec((8, 128), lambda i: (0, 0))],
        out_shape=[jax.ShapeDtypeStruct((S, D), F32), jax.ShapeDtypeStruct((8, 128), F32)],
        compiler_params=_cparams("arbitrary"),
    )(y, target)


def _fold(parts, name, fold=None):
    L, _, W = parts.shape
    assert L <= 8

    def body(*refs):
        p_ref, o_ref = refs[0], refs[-1]
        rows = [jnp.sum(p_ref[l], axis=0, keepdims=True) for l in range(L)]
        rows += [jnp.zeros((1, W), F32)] * (8 - L)
        sums = jnp.concatenate(rows, axis=0)
        o_ref[...] = sums if fold is None else _dot(sums, refs[1][...], prec=HI)

    args = (parts,) if fold is None else (parts, jnp.asarray(fold))
    wout = W if fold is None else 128
    return pl.pallas_call(body, name=name, out_shape=jax.ShapeDtypeStruct((8, wout), F32))(*args)[:L]


def _adamw(w, g, m, v, name, q=None):
    R, C = w.shape
    tr = R
    for cand in (512, 256, 128, 64, 32, 16, 8):
        if R % cand == 0 and cand * C * 4 <= 2 * 2**20:
            tr = cand
            break

    def body(*refs):
        if q is None:
            w_ref, g_ref, m_ref, v_ref, d_ref, nm_ref, nv_ref = refs
            gv = g_ref[...]
        else:
            w_ref, g_ref, q_ref, m_ref, v_ref, go_ref, d_ref, nm_ref, nv_ref = refs
            gv = g_ref[...] + q_ref[...]
            go_ref[...] = gv
        mn = ADAM_B1 * m_ref[...] + (1.0 - ADAM_B1) * gv
        vn = ADAM_B2 * v_ref[...] + (1.0 - ADAM_B2) * (gv * gv)
        nm_ref[...] = mn
        nv_ref[...] = vn
        m_hat = mn / (1.0 - ADAM_B1 ** ADAM_STEP)
        v_hat = vn / (1.0 - ADAM_B2 ** ADAM_STEP)
        d_ref[...] = -ADAM_LR * (m_hat / (jnp.sqrt(v_hat) + ADAM_EPS) + ADAM_WD * w_ref[...])

    blk = pl.BlockSpec((tr, C), lambda i: (i, 0))
    args = (w, g, m, v) if q is None else (w, g, q, m, v)
    nout = 3 if q is None else 4
    return pl.pallas_call(
        body, name=name, grid=(R // tr,), in_specs=[blk] * len(args), out_specs=[blk] * nout,
        out_shape=[jax.ShapeDtypeStruct((R, C), F32)] * nout, compiler_params=_cparams("parallel"),
    )(*args)


def _sibling_exchange(srcs, name):
    n = len(srcs)

    def body(*refs):
        src_refs, dst_refs = refs[:n], refs[n:2 * n]
        send_sems, recv_sems = refs[2 * n:]
        x, y, c = lax.axis_index("x"), lax.axis_index("y"), lax.axis_index("c")
        copies = [pltpu.make_async_remote_copy(src_ref=src_refs[t], dst_ref=dst_refs[t], send_sem=send_sems.at[t],
                                               recv_sem=recv_sems.at[t], device_id=(x, y, 1 - c), device_id_type=MESH)
                  for t in range(n)]
        for cp in copies:
            cp.start()
        for cp in copies:
            cp.wait()

    return pl.pallas_call(
        body, name=name, in_specs=[_ANY] * n, out_specs=[_ANY] * n,
        out_shape=[jax.ShapeDtypeStruct(s.shape, s.dtype) for s in srcs],
        scratch_shapes=[pltpu.SemaphoreType.DMA((n,)), pltpu.SemaphoreType.DMA((n,))],
    )(*srcs)


def _allreduce_small(v, name):
    R = v.shape[0]

    def body(v_ref, o_ref, slots, send_sems, recv_sems):
        x, y, c = lax.axis_index("x"), lax.axis_index("y"), lax.axis_index("c")
        me = 4 * x + 2 * y + c
        slots[me] = v_ref[...]
        sent = []
        for r in range(1, 8):
            fx, fy, fc = (r >> 2) & 1, (r >> 1) & 1, r & 1
            px, py, pc = (1 - x) if fx else x, (1 - y) if fy else y, (1 - c) if fc else c
            peer = 4 * px + 2 * py + pc

            def copy(slot, r=r, px=px, py=py, pc=pc):
                return pltpu.make_async_remote_copy(
                    src_ref=v_ref, dst_ref=slots.at[slot], send_sem=send_sems.at[r - 1], recv_sem=recv_sems.at[r - 1],
                    device_id=(px, py, pc), device_id_type=MESH)

            cp = copy(me)
            cp.start()
            sent.append((cp, copy(peer)))
        for cp, arrival in sent:
            cp.wait_send()
            arrival.wait_recv()
        acc = slots[0]
        for k in range(1, 8):
            acc = acc + slots[k]
        o_ref[...] = acc

    vm = pl.BlockSpec(memory_space=pltpu.VMEM)
    return pl.pallas_call(
        body, name=name, in_specs=[vm], out_specs=vm, out_shape=jax.ShapeDtypeStruct((R, 128), F32),
        scratch_shapes=[pltpu.VMEM((8, R, 128), F32), pltpu.SemaphoreType.DMA((7,)), pltpu.SemaphoreType.DMA((7,))],
    )(v)


def _sum4(recv, name, tr=512):
    _, R, W = recv.shape
    tr = _tile(R, tr)
    assert R % tr == 0

    def body(r_ref, o_ref):
        o_ref[...] = ((r_ref[0].astype(F32) + r_ref[1].astype(F32)) + r_ref[2].astype(F32)) + r_ref[3].astype(F32)

    return pl.pallas_call(
        body, name=name, grid=(R // tr,), in_specs=[pl.BlockSpec((4, tr, W), lambda i: (0, i, 0))],
        out_specs=pl.BlockSpec((tr, W), lambda i: (i, 0)), out_shape=jax.ShapeDtypeStruct((R, W), F32),
        compiler_params=_cparams("parallel"),
    )(recv)


W_NAMES = ("ffn1_norm", "ffn1_w1", "ffn1_w3", "ffn1_w2", "mix_norm", "w_in", "na_q_norm", "na_k_norm", "na_rpb",
           "gla_gf_up", "gla_gf_bias", "gla_gb_up", "gla_gb_bias", "gla_out_norm", "mla_cq_norm", "mla_ckv_norm",
           "mla_w_uq", "mla_w_ukv", "mla_q_norm", "mla_k_norm", "w_br_na", "w_br_gla", "w_br_mla", "w_out",
           "ffn2_norm", "ffn2_w1", "ffn2_w3", "ffn2_w2")
SHARDED = {"ffn1_w1": 2, "ffn1_w3": 2, "ffn1_w2": 1, "w_in": 2, "gla_gf_up": 2, "gla_gb_up": 2, "mla_w_uq": 2,
           "mla_w_ukv": 2, "w_br_na": 2, "w_br_gla": 2, "w_br_mla": 2, "w_out": 1, "ffn2_w1": 2, "ffn2_w3": 2,
           "ffn2_w2": 1}
REPLICATED = tuple(n for n in W_NAMES if n not in SHARDED)
FFN_W = ("ffn1_w1", "ffn1_w3", "ffn1_w2", "ffn2_w1", "ffn2_w3", "ffn2_w2")


def _win_layout(w, D):
    z = lambda n: jnp.zeros(w.shape[:-1] + (n,), w.dtype)
    return jnp.concatenate([w[..., O_GATES:], w[..., :O_GFL], w[..., O_CQ:O_KR], w[..., O_GFL:O_CQ], z(96),
                            w[..., O_KR:O_KR + 32], z(32), w[..., O_KR + 32:O_KR + 64], z(32)], axis=-1)


def _win_unlayout(dw, D):
    g = 3 * D
    return jnp.concatenate([dw[..., g:g + O_GFL], dw[..., g + 3584:g + 3616], dw[..., g + 3072:g + 3584],
                            dw[..., g + 3712:g + 3744], dw[..., g + 3776:g + 3808], dw[..., :g]], axis=-1)


def _uq_layout(w):
    s = w.shape[:-1]
    w = w.reshape(s + (MLA_HEADS, MLA_QK))
    z = jnp.zeros(s + (MLA_HEADS, 32), w.dtype)
    return jnp.concatenate([w[..., :160], z, w[..., 160:], z], axis=-1).reshape(s + (MLA_HEADS * MLA_SLOT,))


def _uq_unlayout(dw):
    s = dw.shape[:-1]
    dw = dw.reshape(s + (MLA_HEADS, MLA_SLOT))
    return jnp.concatenate([dw[..., :160], dw[..., 192:224]], axis=-1).reshape(s + (MLA_HEADS * MLA_QK,))


def _slot_layout(g):
    z = jnp.zeros(g.shape[:-1] + (32,), g.dtype)
    return jnp.concatenate([g[..., :160], z, g[..., 160:], z], axis=-1)


def _slot_unlayout(g):
    return jnp.concatenate([g[..., :160], g[..., 192:224]], axis=-1)


def _layer_fwd(x, w, cos, sg, rides):
    D = x.shape[1]
    NA, GL, ML, LR, KR = 3 * D, 3 * D + 1536, 3 * D + 3072, 3 * D + 3584, 3 * D + 3712
    got = {}
    x1, f1, arrived = _ffn_fwd(x, w["ffn1_norm"], w["ffn1_w1"], w["ffn1_w3"], w["ffn1_w2"], "ffn1", ride=rides.get("ffn1_up"))
    got.update(arrived)
    h = _rms_fwd(x1, w["mix_norm"], "mix_rms")
    nz = w["w_in"].shape[1]
    z, arrived = _mm([(h, w["w_in"])], "nn", F32, "w_in", tm=512, tn=_tile(nz, 1280), ride=rides.get("w_in", []))
    got.update(arrived)
    qn, kn, vb = _na_prep(z, NA, w["na_gq"], w["na_gk"], "na_prep")
    bias = _rpb_expand(w["na_rpb"], "rpb_expand")
    y_na, arrived = _na_attn(qn, kn, vb, bias, "na_attn", ride=rides.get("na_attn"))
    got.update(arrived)
    gfb = _gla_gates(z, LR, w["gla_wg"], w["gla_gbias"], "gla_gates")
    (o_f, o_b, s_f, s_b), arrived = _gla_fwd(z, GL, GL + 512, gfb, "gla_fwd", ride=rides.get("gla_fwd"))
    got.update(arrived)
    y_gla = _gla_post(o_f, o_b, z, GL + 1024, w["gla_out_norm"], "gla_post")
    q, k, v, cqn, ckvn = _mla_prep(z, ML, KR, w["mla_wuq"], w["mla_w_ukv"], w["mla_cq_norm"], w["mla_ckv_norm"],
                                   w["mla_gq"], w["mla_gk"], cos, sg, "mla_prep")
    y_mla, arrived = _mla_attn(q, k, v, "mla_attn", ride=rides.get("mla_attn"))
    got.update(arrived)
    mixed, p0, p1, p2 = _merge([y_na, y_gla, y_mla], [w["w_br_na"], w["w_br_gla"], w["w_br_mla"]], z, "merge")
    x2 = _mm([(mixed, w["w_out"])], "nn", F32, "w_out", tm=512, tn=1024, res=x1)
    x3, f2, _ = _ffn_fwd(x2, w["ffn2_norm"], got["ffn2_w1"], got["ffn2_w3"], got["ffn2_w2"], "ffn2")
    saved = dict(x=x, x1=x1, x2=x2, f1=f1, f2=f2, h=h, z=z, qn=qn, kn=kn, vb=vb, bias=bias, y_na=y_na, gfb=gfb, o_f=o_f,
                 o_b=o_b, s_f=s_f, s_b=s_b, y_gla=y_gla, q=q, k=k, v=v, cqn=cqn, ckvn=ckvn, y_mla=y_mla, mixed=mixed,
                 p0=p0, p1=p1, p2=p2)
    return x3, saved, got


def _split4(a, axis):
    n = a.shape[axis] // 4
    return jnp.stack([lax.slice_in_dim(a, j * n, (j + 1) * n, axis=axis) for j in range(4)]).astype(BF16)


def _layer_bwd(dx3, w, sv, cos, sg, bufs, recv, layer, prev):
    D = dx3.shape[1]
    at_layer = lambda chip: (chip, layer)
    pick = lambda *names: [prev[n] for n in names if n in prev]
    recv = dict(recv)
    NA, GL, ML, LR, KR = 3 * D, 3 * D + 1536, 3 * D + 3072, 3 * D + 3584, 3 * D + 3712
    z = sv["z"]
    g = {}
    dx2, g["ffn2_norm"], (g["ffn2_w1"], g["ffn2_w3"], g["ffn2_w2"]), got = _ffn_bwd(
        dx3, sv["x2"], w["ffn2_norm"], w["ffn2_w1"], w["ffn2_w3"], w["ffn2_w2"], sv["f2"], "ffn2",
        (bufs["ffn2_w1"], bufs["ffn2_w3"], bufs["ffn2_w2"]), layer, ride_down=pick("ffn1_w1"), ride_dh=pick("ffn1_w3"))
    recv.update(got)
    dmixed = _mm([(dx2, w["w_out"])], "nt", F32, "w_out_dx", tm=512, tn=512)
    g["w_out"] = _mm([(sv["mixed"], dx2)], "tn", F32, "w_out_dw", tm=D, tn=256)
    d0, d1, d2, dgates = _merge_bwd(dmixed, [sv["p0"], sv["p1"], sv["p2"]], z, "merge_bwd")
    dys = []
    for d, y, nm, dt in ((d0, sv["y_na"], "w_br_na", BF16), (d1, sv["y_gla"], "w_br_gla", F32), (d2, sv["y_mla"], "w_br_mla", BF16)):
        dys.append(_mm([(d, w[nm])], "nt", dt, nm + "_dy", tm=512, tn=512))
        g[nm] = _mm([(y, d)], "tn", F32, nm + "_dw", tm=512, tn=512)
    (dqn, dkn, dvn, dbias), got = _na_attn_bwd(sv["qn"], sv["kn"], sv["vb"], sv["bias"], dys[0], "na_attn_bwd",
                                               ride=pick("ffn1_w2", "mla_w_uq", "mla_w_ukv", "gla_gf_up", "gla_gb_up"))
    recv.update(got)
    dz_na, g["na_gq"], g["na_gk"] = _na_prep_bwd(z, NA, w["na_gq"], w["na_gk"], dqn, dkn, dvn, "na_prep_bwd")
    g["na_rpb"] = _rpb_reduce(dbias, "rpb_reduce")
    do, dgr, g["gla_out_norm"] = _gla_post_bwd(sv["o_f"], sv["o_b"], z, GL + 1024, w["gla_out_norm"], dys[1], "gla_post_bwd")
    (dqk_f, dv_f, dg_f, dqk_b, dv_b, dg_b), _ = _gla_bwd(z, GL, GL + 512, sv["gfb"], do, sv["s_f"], sv["s_b"], "gla_bwd")
    dz_gla = _gla_assemble(dqk_f, dqk_b, dv_f, dv_b, dgr, "gla_assemble")
    dpre, g["gla_gbias"] = _gla_gates_bwd(z, LR, w["gla_wg"], w["gla_gbias"], dg_f, dg_b, "gla_gates_bwd")
    g["gla_wg"] = _mm([(_V(z, LR, 128), dpre)], "tn", F32, "gla_wg_dw", tm=128, tn=512)
    dz_lr = _mm([(dpre, w["gla_wg"])], "nt", BF16, "gla_wg_dz", tm=512, tn=128)
    own = lambda n: _Ride(n, g[n], at_layer, recv[n], at_layer)
    (dq, dk, dv), got = _mla_attn_bwd(sv["q"], sv["k"], sv["v"], dys[2], "mla_attn_bwd",
                                      ride=pick("w_in") + [own("ffn2_w1"), own("ffn2_w3")])
    recv.update(got)
    dqf, dkv, dz_kr, g["mla_gq"], g["mla_gk"] = _mla_prep_bwd(
        z, KR, sv["cqn"], sv["ckvn"], w["mla_wuq"], w["mla_w_ukv"], w["mla_gq"], w["mla_gk"], cos, sg, dq, dk, dv, "mla_prep_bwd")
    g["mla_wuq"] = _mm([(sv["cqn"], dqf)], "tn", F32, "mla_wuq_dw", tm=256, tn=512)
    g["mla_w_ukv"] = _mm([(sv["ckvn"], dkv)], "tn", F32, "mla_wukv_dw", tm=256, tn=512)
    dcqn = _mm([(dqf, w["mla_wuq"])], "nt", F32, "mla_wuq_dx", tm=512, tn=256)
    dckvn = _mm([(dkv, w["mla_w_ukv"])], "nt", F32, "mla_wukv_dx", tm=512, tn=256)
    dz_cq, dg_cq = _rms_bwd(_V(z, ML, MLA_RANK), w["mla_cq_norm"], dcqn, "mla_cq_rms_bwd", out_dtype=BF16)
    dz_ckv, dg_ckv = _rms_bwd(_V(z, ML + MLA_RANK, MLA_RANK), w["mla_ckv_norm"], dckvn, "mla_ckv_rms_bwd", out_dtype=BF16)
    g["mla_cq_norm"], g["mla_ckv_norm"] = dg_cq[0:1], dg_ckv[0:1]
    segs = ((dgates, 0, 3 * D), (dz_na, NA, 1536), (dz_gla, GL, 1536), (dz_cq, ML, MLA_RANK), (dz_ckv, ML + MLA_RANK, MLA_RANK),
            (dz_lr, LR, 128), (dz_kr, KR, 128))
    dh, got = _mm([(dz, _V(w["w_in"], c0, wd)) for dz, c0, wd in segs], "nt", F32, "w_in_dx", tm=512, tn=512,
                  ride=[own("ffn2_w2")])
    recv.update(got)
    g["w_in"] = jnp.concatenate(
        [_mm([(sv["h"], dz)], "tn", F32, f"w_in_dw{i}", tm=D, tn=_tile(wd, 256)) for i, (dz, _, wd) in enumerate(segs)], axis=1)
    dx1, dg_mix = _rms_bwd(sv["x1"], w["mix_norm"], dh, "mix_rms_bwd", dres=dx2)
    g["mix_norm"] = dg_mix[0:1]
    ride = [_Ride(n, _split4(g[n], SHARDED[n] - 1), lambda chip: (chip,), recv[n], at_layer)
            for n in ("w_out", "w_br_na", "w_br_gla", "w_br_mla")]
    dx, g["ffn1_norm"], (g["ffn1_w1"], g["ffn1_w3"], g["ffn1_w2"]), got = _ffn_bwd(
        dx1, sv["x"], w["ffn1_norm"], w["ffn1_w1"], w["ffn1_w3"], w["ffn1_w2"], sv["f1"], "ffn1",
        (bufs["ffn1_w1"], bufs["ffn1_w3"], bufs["ffn1_w2"]), layer, ride_down=ride)
    recv.update(got)
    late = dict(w_in=_split4(_win_unlayout(g["w_in"], D), 1), mla_w_uq=_split4(_uq_unlayout(g["mla_wuq"]), 1),
                mla_w_ukv=_split4(g["mla_w_ukv"], 1), gla_gf_up=_split4(g["gla_wg"][0:GLA_RANK, 0:256], 1),
                gla_gb_up=_split4(g["gla_wg"][GLA_RANK:2 * GLA_RANK, 256:512], 1))
    return dx, g, late, recv


def _head_fold(width, period, lo=0):
    f = np.zeros((width, 128), np.float32)
    f[np.arange(width), lo + np.arange(width) % period] = 1.0
    return f


def kernel(x, ffn1_norm, ffn1_w1, ffn1_w3, ffn1_w2, mix_norm, w_in, na_q_norm, na_k_norm, na_rpb, gla_gf_up, gla_gf_bias,
           gla_gb_up, gla_gb_bias, gla_out_norm, mla_cq_norm, mla_ckv_norm, mla_w_uq, mla_w_ukv, mla_q_norm, mla_k_norm,
           w_br_na, w_br_gla, w_br_mla, w_out, ffn2_norm, ffn2_w1, ffn2_w3, ffn2_w2, loss_target, m_ffn1_norm, m_ffn1_w1,
           m_ffn1_w3, m_ffn1_w2, m_mix_norm, m_w_in, m_na_q_norm, m_na_k_norm, m_na_rpb, m_gla_gf_up, m_gla_gf_bias,
           m_gla_gb_up, m_gla_gb_bias, m_gla_out_norm, m_mla_cq_norm, m_mla_ckv_norm, m_mla_w_uq, m_mla_w_ukv,
           m_mla_q_norm, m_mla_k_norm, m_w_br_na, m_w_br_gla, m_w_br_mla, m_w_out, m_ffn2_norm, m_ffn2_w1, m_ffn2_w3,
           m_ffn2_w2, v_ffn1_norm, v_ffn1_w1, v_ffn1_w3, v_ffn1_w2, v_mix_norm, v_w_in, v_na_q_norm, v_na_k_norm,
           v_na_rpb, v_gla_gf_up, v_gla_gf_bias, v_gla_gb_up, v_gla_gb_bias, v_gla_out_norm, v_mla_cq_norm,
           v_mla_ckv_norm, v_mla_w_uq, v_mla_w_ukv, v_mla_q_norm, v_mla_k_norm, v_w_br_na, v_w_br_gla, v_w_br_mla,
           v_w_out, v_ffn2_norm, v_ffn2_w1, v_ffn2_w3, v_ffn2_w2):
    given = dict(locals())
    wts = {n: given[n] for n in W_NAMES}
    mom = {n: given["m_" + n] for n in W_NAMES}
    var = {n: given["v_" + n] for n in W_NAMES}
    xs, target = x[0], loss_target[0]
    S, D = xs.shape
    L = ffn1_norm.shape[0]

    sh_names = tuple(SHARDED)
    LATE = ("ffn2_w1", "ffn2_w3", "ffn2_w2")
    HEAVY = ("ffn1_w1", "ffn1_w3", "ffn1_w2", "w_in")
    LIGHT = tuple(n for n in sh_names if n not in LATE + HEAVY)
    shard_shape = lambda n: tuple(wts[n].shape[1:])

    def gather_items(names, l):
        return [_Ride(n, wts[n][l].astype(BF16), lambda chip: (), lax.empty((4,) + shard_shape(n), BF16), lambda chip: (chip,),
                      halves=n in HEAVY + LATE) for n in names]

    cols = lambda p: jnp.concatenate([p[j] for j in range(4)], axis=-1)

    def layer_weights(gl, l):
        r1 = lambda a: a[l][None]
        wg = jnp.zeros((128, 2 * GLA_HEADS * GLA_DK), BF16)
        wg = wg.at[0:GLA_RANK, 0:256].set(cols(gl["gla_gf_up"])).at[GLA_RANK:2 * GLA_RANK, 256:512].set(cols(gl["gla_gb_up"]))
        return dict(
            ffn1_norm=r1(ffn1_norm), ffn1_w1=gl["ffn1_w1"], ffn1_w3=gl["ffn1_w3"], ffn1_w2=gl["ffn1_w2"],
            mix_norm=r1(mix_norm), w_in=_win_layout(cols(gl["w_in"]), D),
            na_gq=jnp.tile(na_q_norm[l], NA_HEADS)[None], na_gk=jnp.tile(na_k_norm[l], NA_HEADS)[None], na_rpb=na_rpb[l],
            gla_wg=wg, gla_gbias=jnp.concatenate([gla_gf_bias[l], gla_gb_bias[l]])[None], gla_out_norm=r1(gla_out_norm),
            mla_cq_norm=r1(mla_cq_norm), mla_ckv_norm=r1(mla_ckv_norm), mla_wuq=_uq_layout(cols(gl["mla_w_uq"])),
            mla_w_ukv=cols(gl["mla_w_ukv"]), mla_gq=_slot_layout(mla_q_norm[l])[None], mla_gk=_slot_layout(mla_k_norm[l])[None],
            w_br_na=cols(gl["w_br_na"]), w_br_gla=cols(gl["w_br_gla"]), w_br_mla=cols(gl["w_br_mla"]),
            w_out=gl["w_out"].reshape(D, D), ffn2_norm=r1(ffn2_norm))

    half = MLA_ROPE // 2
    inv = ROPE_THETA ** (-jnp.arange(half, dtype=F32) / half)
    ang = jnp.arange(S, dtype=F32)[:, None] * inv[None, :]
    cos = jnp.tile(jnp.cos(ang), (1, 4))
    sg = jnp.concatenate([-jnp.sin(ang), -jnp.sin(ang), jnp.sin(ang), jnp.sin(ang)], axis=1)

    arrived = _exchange(gather_items(HEAVY + LIGHT, 0), "weights_all_gather")
    xc, saved, layers = xs, [], []
    for l in range(L):
        w = layer_weights(arrived, l)
        rides = {"ffn1_up": gather_items(("ffn2_w1", "ffn2_w3"), l), "w_in": gather_items(("ffn2_w2",), l)}
        if l + 1 < L:
            rides["w_in"] += gather_items(LIGHT, l + 1)
            rides["na_attn"] = gather_items(("ffn1_w1", "ffn1_w3"), l + 1)
            rides["gla_fwd"] = gather_items(("ffn1_w2",), l + 1)
            rides["mla_attn"] = gather_items(("w_in",), l + 1)
        xc, sv, arrived = _layer_fwd(xc, w, cos, sg, rides)
        saved.append(sv)
        layers.append({**w, **{n: arrived[n] for n in LATE}})
    dy, loss_part = _loss_head(xc, target, "loss_head")

    bufs = {n: lax.empty((4, L) + shard_shape(n), BF16) for n in FFN_W}
    recv = {n: lax.empty((4, L) + shard_shape(n), BF16) for n in sh_names}
    dx, g, prev = dy, [None] * L, {}
    for l in reversed(range(L)):
        dx, g[l], late, recv = _layer_bwd(dx, layers[l], saved[l], cos, sg, bufs, recv, l, prev)
        bufs = {n: g[l][n] for n in FFN_W}
        at_l = functools.partial(lambda chip, l: (chip, l), l=l)
        prev = {n: _Ride(n, bufs[n], at_l, recv[n], at_l) for n in ("ffn1_w1", "ffn1_w3", "ffn1_w2")}
        prev.update({n: _Ride(n, late[n], lambda chip: (chip,), recv[n], at_l) for n in late})
    recv = {**recv, **_exchange(list(prev.values()), "grads_chip_exchange")}

    stk = lambda n: jnp.stack([g[l][n] for l in range(L)])
    gs = {n: stk(n)[:, 0] for n in ("ffn1_norm", "mix_norm", "mla_cq_norm", "mla_ckv_norm", "ffn2_norm")}
    gs["na_q_norm"] = _fold(stk("na_gq"), "na_gq_fold", _head_fold(NA_W, NA_DH))[:, :NA_DH]
    gs["na_k_norm"] = _fold(stk("na_gk"), "na_gk_fold", _head_fold(NA_W, NA_DH))[:, :NA_DH]
    gs["na_rpb"] = stk("na_rpb")
    gbias = _fold(stk("gla_gbias"), "gla_gbias_fold")
    gs["gla_gf_bias"], gs["gla_gb_bias"] = gbias[:, :256], gbias[:, 256:]
    gs["gla_out_norm"] = _fold(stk("gla_out_norm"), "gla_out_norm_fold")
    gs["mla_q_norm"] = _slot_unlayout(_fold(stk("mla_gq"), "mla_gq_fold"))
    gs["mla_k_norm"] = _slot_unlayout(_fold(stk("mla_gk"), "mla_gk_fold"))

    mine = [_sum4(recv[n].reshape(4, -1, recv[n].shape[-1]), "grads_chip_sum_" + n) for n in sh_names]
    other = _sibling_exchange(mine, "grads_sibling_exchange")
    gsh = {}

    small_shapes = [wts[n].shape[1:] for n in REPLICATED]
    n_small = sum(int(np.prod(s)) for s in small_shapes) * L
    flat = jnp.concatenate([gs[n].reshape(-1) for n in REPLICATED] + [loss_part.reshape(-1)])
    pad = -flat.shape[0] % 1024
    red = _allreduce_small(jnp.pad(flat, (0, pad)).reshape(-1, 128), "small_all_reduce").reshape(-1)
    loss = jnp.sum(red[n_small:n_small + 1024])
    off = 0
    for n, s in zip(REPLICATED, small_shapes):
        cnt = int(np.prod(s)) * L
        gsh[n] = red[off:off + cnt].reshape((L,) + tuple(s))
        off += cnt

    as2d = lambda a: a.reshape(-1, a.shape[-1])
    upd = {}
    for n, p, q in zip(sh_names, mine, other):
        outs = [o.reshape(wts[n].shape) for o in _adamw(as2d(wts[n]), p, as2d(mom[n]), as2d(var[n]), "adamw_" + n, q=q)]
        gsh[n], upd[n] = outs[0], outs[1:]
    pk = lambda d: jnp.pad(jnp.concatenate([d[n].reshape(-1) for n in REPLICATED]), (0, -n_small % 1024)).reshape(-1, 128)
    small = _adamw(pk(wts), pk(gsh), pk(mom), pk(var), "adamw_replicated")
    off = 0
    for n, s in zip(REPLICATED, small_shapes):
        cnt = int(np.prod(s)) * L
        upd[n] = [o.reshape(-1)[off:off + cnt].reshape((L,) + tuple(s)) for o in small]
        off += cnt

    return (loss, dx[None], *[gsh[n] for n in W_NAMES], *[upd[n][0] for n in W_NAMES], *[upd[n][1] for n in W_NAMES],
            *[upd[n][2] for n in W_NAMES])
```

```python
import functools
import math

import numpy as np
import jax
import jax.numpy as jnp
from jax import lax
from jax.experimental import pallas as pl
from jax.experimental.pallas import tpu as pltpu

F32 = jnp.float32
BF16 = jnp.bfloat16
HI = lax.Precision.HIGHEST
MESH = pl.DeviceIdType.MESH

EPS = 1e-6
GRID_W = 64
NA_HEADS, NA_DH, NA_WIN_R, NA_WIN_C = 8, 64, 8, 16
NA_W = NA_HEADS * NA_DH
GLA_HEADS, GLA_DK, GLA_DV, GLA_RANK, GLA_TAU, GLA_CHUNK = 4, 64, 128, 16, 16.0, 64
MLA_HEADS, MLA_RANK, MLA_NOPE, MLA_ROPE, MLA_V = 4, 256, 128, 64, 128
MLA_QK = MLA_NOPE + MLA_ROPE
MLA_SLOT = 256
MLA_QSCALE = MLA_QK ** -0.5 * math.log2(math.e)
ROPE_THETA = 10000.0
ADAM_LR, ADAM_B1, ADAM_B2, ADAM_EPS, ADAM_WD, ADAM_STEP = 0.001, 0.9, 0.999, 1e-08, 0.01, 10

V7X_VMEM_BYTES = 64 * 2**20
VMEM_LIMIT = V7X_VMEM_BYTES - 12 * 2**20
NEG = -1e30

O_GQ, O_GFL, O_CQ, O_KR, O_GATES = 1536, 3072, 3104, 3616, 3680


_ANY = pl.BlockSpec(memory_space=pl.ANY)


def _cparams(*sem):
    return pltpu.CompilerParams(dimension_semantics=sem, vmem_limit_bytes=VMEM_LIMIT)


class _V:
    def __init__(self, arr, c0=0, w=None, lead=()):
        self.arr, self.c0, self.lead = arr, c0, tuple(lead)
        assert arr.ndim == 2 + len(self.lead), (arr.shape, lead)
        self.w = arr.shape[-1] if w is None else w

    @property
    def rows(self):
        return self.arr.shape[-2]

    def spec(self, br, bc, rfn, cfn):
        assert self.c0 % bc == 0 and self.w % bc == 0, (self.c0, self.w, bc)
        off, lead = self.c0 // bc, self.lead

        def index(*g):
            return tuple(g[0] if e == "b" else e for e in lead) + (rfn(*g), off + cfn(*g))

        return pl.BlockSpec((None,) * len(lead) + (br, bc), index)


def _v(x):
    return x if isinstance(x, _V) else _V(x)


_DN = {"nn": (((1,), (0,)), ((), ())), "nt": (((1,), (1,)), ((), ())), "tn": (((0,), (0,)), ((), ()))}


def _dot(a, b, mode="nn", prec=None):
    return lax.dot_general(a, b, _DN[mode], preferred_element_type=F32, precision=prec)


def _tile(n, cap):
    if n <= cap:
        return n
    for t in range(cap - cap % 128, 0, -128):
        if n % t == 0:
            return t
    return n


def _mm(pairs, mode, out_dtype, name, *, tm, tn, res=None, scale=None, batch=1, into=None, ride=None):
    pairs = [(_v(a), _v(b)) for a, b in pairs]
    a0, b0 = pairs[0]
    M = a0.w if mode == "tn" else a0.rows
    N = b0.rows if mode == "nt" else b0.w
    tm, tn = _tile(M, tm), _tile(N, tn)
    assert M % tm == 0 and N % tn == 0, (name, M, N, tm, tn)
    n = len(pairs)

    def body(*refs):
        o_ref = refs[-1]
        acc = None
        for i in range(n):
            d = _dot(refs[2 * i][...].astype(BF16), refs[2 * i + 1][...].astype(BF16), mode)
            acc = d if acc is None else acc + d
        if scale is not None:
            acc = acc * scale
        if res is not None:
            acc = acc + refs[2 * n][...]
        o_ref[...] = acc.astype(o_ref.dtype)

    zero = lambda b, i, j: 0
    row = lambda b, i, j: i
    col = lambda b, i, j: j
    in_specs, args = [], []
    for a, b in pairs:
        in_specs.append(a.spec(a.rows, tm, zero, row) if mode == "tn" else a.spec(tm, a.w, row, zero))
        in_specs.append(b.spec(tn, b.w, col, zero) if mode == "nt" else b.spec(b.rows, tn, zero, col))
        args += [a.arr, b.arr]
    if res is not None:
        in_specs.append(pl.BlockSpec((tm, tn), lambda b, i, j: (i, j)))
        args.append(res)
    aliases = {}
    if into is None:
        out = jax.ShapeDtypeStruct(((batch,) if batch > 1 else ()) + (M, N), out_dtype)
        out_view = _V(out, lead=("b",) if batch > 1 else ())
    else:
        buf, lead = into
        assert buf.shape[-2:] == (M, N) and buf.dtype == out_dtype, (name, buf.shape, M, N)
        out = jax.ShapeDtypeStruct(buf.shape, buf.dtype)
        out_view = _V(out, lead=lead)
        aliases = {len(args): 0}
        in_specs.append(_ANY)
        args.append(buf)
    (res,), got = _ride_call(
        body, ride, name=name, grid=(batch, M // tm, N // tn), in_specs=in_specs, out_specs=[out_view.spec(tm, tn, row, col)],
        out_shape=[out], aliases=aliases, args=args, semantics=("parallel", "parallel", "parallel"))
    return res if ride is None else (res, got)


def _rms_fwd(x, g, name, tm=512):
    x = _v(x)
    S, D = x.rows, x.w
    tm = min(tm, S)

    def body(x_ref, g_ref, o_ref):
        xv = x_ref[...]
        y = xv * lax.rsqrt(jnp.mean(xv * xv, axis=-1, keepdims=True) + EPS)
        o_ref[...] = (y * g_ref[...]).astype(o_ref.dtype)

    return pl.pallas_call(
        body, name=name, grid=(S // tm,),
        in_specs=[x.spec(tm, D, lambda i: i, lambda i: 0), pl.BlockSpec((1, D), lambda i: (0, 0))],
        out_specs=pl.BlockSpec((tm, D), lambda i: (i, 0)),
        out_shape=jax.ShapeDtypeStruct((S, D), BF16), compiler_params=_cparams("parallel"),
    )(x.arr, g)


def _rms_bwd(x, g, dh, name, dres=None, out_dtype=F32, tm=512):
    x = _v(x)
    S, D = x.rows, x.w
    tm = min(tm, S)

    def body(*refs):
        if dres is None:
            x_ref, g_ref, dh_ref, dx_ref, dg_ref = refs
        else:
            x_ref, g_ref, dh_ref, dr_ref, dx_ref, dg_ref = refs
        xv = x_ref[...]
        rstd = lax.rsqrt(jnp.mean(xv * xv, axis=-1, keepdims=True) + EPS)
        xhat = xv * rstd
        dhv = dh_ref[...].astype(F32)
        dxhat = dhv * g_ref[...]
        dx = rstd * (dxhat - xhat * jnp.mean(dxhat * xhat, axis=-1, keepdims=True))
        if dres is not None:
            dx = dx + dr_ref[...]
        dx_ref[...] = dx.astype(dx_ref.dtype)

        @pl.when(pl.program_id(0) == 0)
        def _():
            dg_ref[...] = jnp.zeros_like(dg_ref)

        dg_ref[0:1, :] += jnp.sum(dhv * xhat, axis=0, keepdims=True)

    in_specs = [x.spec(tm, D, lambda i: i, lambda i: 0), pl.BlockSpec((1, D), lambda i: (0, 0)),
                pl.BlockSpec((tm, D), lambda i: (i, 0))]
    args = [x.arr, g, dh]
    if dres is not None:
        in_specs.append(pl.BlockSpec((tm, D), lambda i: (i, 0)))
        args.append(dres)
    return pl.pallas_call(
        body, name=name, grid=(S // tm,), in_specs=in_specs,
        out_specs=[pl.BlockSpec((tm, D), lambda i: (i, 0)), pl.BlockSpec((8, D), lambda i: (0, 0))],
        out_shape=[jax.ShapeDtypeStruct((S, D), out_dtype), jax.ShapeDtypeStruct((8, D), F32)],
        compiler_params=_cparams("arbitrary"),
    )(*args)


def _ffn_up(h, w1, w3, name, tm=512, ride=None):
    S, D = h.shape
    NC, _, F4 = w1.shape
    tm = min(tm, S)

    def body(h_ref, w1_ref, w3_ref, a_ref, b_ref, u_ref):
        hv = h_ref[...]
        a = _dot(hv, w1_ref[...])
        b = _dot(hv, w3_ref[...])
        a_ref[...] = a.astype(BF16)
        b_ref[...] = b.astype(BF16)
        u_ref[...] = (a * jax.nn.sigmoid(a) * b).astype(BF16)

    blk = pl.BlockSpec((None, tm, F4), lambda i, j: (j, i, 0))
    wblk = pl.BlockSpec((None, D, F4), lambda i, j: (j, 0, 0))
    return _ride_call(
        body, ride, name=name, grid=(S // tm, NC), in_specs=[pl.BlockSpec((tm, D), lambda i, j: (i, 0)), wblk, wblk],
        out_specs=[blk, blk, blk], out_shape=[jax.ShapeDtypeStruct((NC, S, F4), BF16)] * 3, args=(h, w1, w3),
        semantics=("parallel", "parallel"))


def _ffn_down_bwd(dxo, w2, a, b, name, tm=512, ride=None):
    S, D = dxo.shape
    NC, F4, _ = w2.shape
    tm = min(tm, S)

    def body(dx_ref, w2_ref, a_ref, b_ref, da_ref, db_ref):
        du = _dot(dx_ref[...].astype(BF16), w2_ref[...], "nt") * 0.5
        av = a_ref[...].astype(F32)
        sig = jax.nn.sigmoid(av)
        da_ref[...] = (du * b_ref[...].astype(F32) * (sig * (1.0 + av * (1.0 - sig)))).astype(BF16)
        db_ref[...] = (du * av * sig).astype(BF16)

    blk = pl.BlockSpec((None, tm, F4), lambda i, j: (j, i, 0))
    return _ride_call(
        body, ride, name=name, grid=(S // tm, NC),
        in_specs=[pl.BlockSpec((tm, D), lambda i, j: (i, 0)), pl.BlockSpec((None, F4, D), lambda i, j: (j, 0, 0)), blk, blk],
        out_specs=[blk, blk], out_shape=[jax.ShapeDtypeStruct((NC, S, F4), BF16)] * 2, args=(dxo, w2, a, b),
        semantics=("parallel", "parallel"))


def _ffn_fwd(x, g, w1, w3, w2, tag, ride=None):
    h = _rms_fwd(x, g, f"{tag}_rms")
    (a, b, u), got = _ffn_up(h, w1, w3, f"{tag}_up", ride=ride)
    nc = w2.shape[0]
    y = _mm([(_V(u, lead=(j,)), _V(w2, lead=(j,))) for j in range(nc)], "nn", F32, f"{tag}_down", tm=512, tn=1024, res=x, scale=0.5)
    return y, (h, a, b, u), got


def _ffn_bwd(dxo, x, g, w1, w3, w2, saved, tag, bufs, layer, ride_down=None, ride_dh=None):
    h, a, b, u = saved
    nc, D, F4 = w1.shape
    (da, db), got = _ffn_down_bwd(dxo, w2, a, b, f"{tag}_down_bwd", ride=ride_down)
    into = lambda k: (bufs[k], ("b", layer))
    dw2 = _mm([(_V(u, lead=("b",)), dxo)], "tn", BF16, f"{tag}_dw2", tm=F4, tn=512, scale=0.5, batch=nc, into=into(2))
    dw1 = _mm([(h, _V(da, lead=("b",)))], "tn", BF16, f"{tag}_dw1", tm=D, tn=F4, batch=nc, into=into(0))
    dw3 = _mm([(h, _V(db, lead=("b",)))], "tn", BF16, f"{tag}_dw3", tm=D, tn=F4, batch=nc, into=into(1))
    pairs = [(_V(da, lead=(j,)), _V(w1, lead=(j,))) for j in range(nc)] + [(_V(db, lead=(j,)), _V(w3, lead=(j,))) for j in range(nc)]
    dh, got_dh = _mm(pairs, "nt", F32, f"{tag}_dh", tm=512, tn=512, ride=ride_dh or [])
    dx, dg = _rms_bwd(x, g, dh, f"{tag}_rms_bwd", dres=dxo)
    return dx, dg[0:1], (dw1, dw3, dw2), {**got, **got_dh}


def _iota(shape, dim):
    return lax.broadcasted_iota(jnp.int32, shape, dim)


def _head_block_ones(n, shift):
    return jnp.where((_iota((n, n), 0) >> shift) == (_iota((n, n), 1) >> shift), 1.0, 0.0).astype(F32)


def _lane_mask(width, lo, size):
    l = _iota((1, width), 1)
    return jnp.where((l >= lo) & (l < lo + size), 1.0, 0.0).astype(F32)


def _acc_rows(acc_ref, val, first):
    r = val.shape[0]
    part = jnp.sum(val.reshape(r // 8, 8, val.shape[1]), axis=0)

    @pl.when(first)
    def _():
        acc_ref[...] = part

    @pl.when(jnp.logical_not(first))
    def _():
        acc_ref[...] += part


_FLIPS = ((1, 0), (0, 1), (1, 1))


class _Ride:
    def __init__(self, name, src, src_at, dst, dst_at, halves=False):
        self.name, self.src, self.src_at, self.dst, self.dst_at, self.halves = name, src, src_at, dst, dst_at, halves
        assert not halves or (src.ndim == 2 and src.shape[0] % 32 == 0), (name, src.shape)


_RIDE_SEMS = lambda n: [pltpu.SemaphoreType.DMA((6, n)), pltpu.SemaphoreType.DMA((6, n)), pltpu.SemaphoreType.DMA((n,))]


def _ride_ops(ride, srcs, dsts, send_sems, recv_sems, local_sems):
    x, y, c = lax.axis_index("x"), lax.axis_index("y"), lax.axis_index("c")
    me = 2 * x + y
    at = lambda ref, idx: ref.at[idx] if idx else ref
    local, sends, arrivals, passes = [], [], [], []
    for t, it in enumerate(ride):
        local.append(pltpu.make_async_copy(at(srcs[t], it.src_at(me)), at(dsts[t], it.dst_at(me)), local_sems.at[t]))
    for r, (fx, fy) in enumerate(_FLIPS):
        px, py = (1 - x) if fx else x, (1 - y) if fy else y
        peer = 2 * px + py
        for t, it in enumerate(ride):
            if it.halves:
                h = it.src.shape[0] // 2
                mine = pl.ds(pl.multiple_of(c * h, 16), h)
                theirs = pl.ds(pl.multiple_of((1 - c) * h, 16), h)
                far = dict(send_sem=send_sems.at[r, t], recv_sem=recv_sems.at[r, t], device_id=(px, py, c), device_id_type=MESH)
                near = dict(send_sem=send_sems.at[3 + r, t], recv_sem=recv_sems.at[3 + r, t], device_id=(x, y, 1 - c),
                            device_id_type=MESH)
                sends.append(pltpu.make_async_remote_copy(src_ref=srcs[t].at[mine], dst_ref=dsts[t].at[me, mine], **far))
                arrivals.append(pltpu.make_async_remote_copy(src_ref=srcs[t].at[mine], dst_ref=dsts[t].at[peer, mine], **far))
                passes.append((pltpu.make_async_remote_copy(src_ref=dsts[t].at[peer, mine], dst_ref=dsts[t].at[peer, mine], **near),
                               pltpu.make_async_remote_copy(src_ref=dsts[t].at[peer, theirs], dst_ref=dsts[t].at[peer, theirs], **near)))
            else:
                far = dict(src_ref=at(srcs[t], it.src_at(peer)), send_sem=send_sems.at[r, t], recv_sem=recv_sems.at[r, t],
                           device_id=(px, py, c), device_id_type=MESH)
                sends.append(pltpu.make_async_remote_copy(dst_ref=at(dsts[t], it.dst_at(me)), **far))
                arrivals.append(pltpu.make_async_remote_copy(dst_ref=at(dsts[t], it.dst_at(peer)), **far))
                passes.append(None)

    def start():
        for cp in local + sends:
            cp.start()

    def finish():
        for cp, arrival, onward in zip(sends, arrivals, passes):
            cp.wait_send()
            arrival.wait_recv()
            if onward is not None:
                onward[0].start()
        for onward in passes:
            if onward is not None:
                onward[0].wait_send()
                onward[1].wait_recv()
        for cp in local:
            cp.wait()

    return start, finish


def _grid_edges(*ns):
    def edges():
        first = last = None
        for d, n in enumerate(ns):
            i = pl.program_id(d)
            f, l = i == 0, i == n - 1
            first = f if first is None else jnp.logical_and(first, f)
            last = l if last is None else jnp.logical_and(last, l)
        return first, last
    return edges


def _ride_call(body, ride, *, name, grid, in_specs, out_specs, out_shape, args, scratch_shapes=(), semantics=(), aliases=None):
    scratch_shapes, aliases = list(scratch_shapes), dict(aliases or {})
    if not ride:
        outs = pl.pallas_call(body, name=name, grid=grid, in_specs=in_specs, out_specs=out_specs, out_shape=out_shape,
                              scratch_shapes=scratch_shapes, input_output_aliases=aliases,
                              compiler_params=_cparams(*semantics))(*args)
        return outs, {}
    n_in, n_out, n_sc, n = len(in_specs), len(out_specs), len(scratch_shapes), len(ride)
    edges = _grid_edges(*grid)

    def wrapped(*refs):
        ins, srcs = refs[:n_in], refs[n_in:n_in + n]
        o0 = n_in + 2 * n
        outs, dsts = refs[o0:o0 + n_out], refs[o0 + n_out:o0 + n_out + n]
        scratch = refs[o0 + n_out + n:o0 + n_out + n + n_sc]
        start, finish = _ride_ops(ride, srcs, dsts, *refs[o0 + n_out + n + n_sc:])
        first, last = edges()
        pl.when(first)(start)
        body(*ins, *outs, *scratch)
        pl.when(last)(finish)

    aliases.update({n_in + n + t: n_out + t for t in range(n)})
    res = pl.pallas_call(
        wrapped, name=name, grid=grid, in_specs=list(in_specs) + [_ANY] * (2 * n), out_specs=list(out_specs) + [_ANY] * n,
        out_shape=list(out_shape) + [jax.ShapeDtypeStruct(it.dst.shape, it.dst.dtype) for it in ride],
        input_output_aliases=aliases, scratch_shapes=scratch_shapes + _RIDE_SEMS(n),
        compiler_params=_cparams(*(["arbitrary"] * len(grid))),
    )(*args, *[it.src for it in ride], *[it.dst for it in ride])
    return res[:n_out], {it.name: o for it, o in zip(ride, res[n_out:])}


def _exchange(ride, name):
    n = len(ride)

    def body(*refs):
        start, finish = _ride_ops(ride, refs[:n], refs[2 * n:3 * n], *refs[3 * n:])
        start()
        finish()

    res = pl.pallas_call(
        body, name=name, in_specs=[_ANY] * (2 * n), out_specs=[_ANY] * n,
        out_shape=[jax.ShapeDtypeStruct(it.dst.shape, it.dst.dtype) for it in ride],
        input_output_aliases={n + t: t for t in range(n)}, scratch_shapes=_RIDE_SEMS(n),
    )(*[it.src for it in ride], *[it.dst for it in ride])
    return {it.name: o for it, o in zip(ride, res)}


def _na_prep(z, c0, gq, gk, name, tm=512):
    S = z.shape[0]
    tm = min(tm, S)
    zv = _V(z, c0, 3 * NA_W)

    def body(z_ref, gq_ref, gk_ref, q_ref, k_ref, v_ref):
        bd = _head_block_ones(NA_W, 6)

        def norm(xv, gv):
            ms = _dot(xv * xv, bd, prec=HI) * (1.0 / NA_DH)
            return xv * lax.rsqrt(ms + EPS) * gv

        q_ref[...] = (norm(z_ref[:, 0:NA_W], gq_ref[...]) * (NA_DH ** -0.5)).astype(BF16)
        k_ref[...] = norm(z_ref[:, NA_W:2 * NA_W], gk_ref[...]).astype(BF16)
        v_ref[...] = z_ref[:, 2 * NA_W:3 * NA_W].astype(BF16)

    blk = pl.BlockSpec((tm, NA_W), lambda i: (i, 0))
    gspec = pl.BlockSpec((1, NA_W), lambda i: (0, 0))
    return pl.pallas_call(
        body, name=name, grid=(S // tm,),
        in_specs=[zv.spec(tm, 3 * NA_W, lambda i: i, lambda i: 0), gspec, gspec],
        out_specs=[blk, blk, blk], out_shape=[jax.ShapeDtypeStruct((S, NA_W), BF16)] * 3,
        compiler_params=_cparams("parallel"),
    )(z, gq, gk)


def _na_prep_bwd(z, c0, gq, gk, dqn, dkn, dv, name, tm=512):
    S = z.shape[0]
    tm = min(tm, S)
    zv = _V(z, c0, 3 * NA_W)

    def body(z_ref, gq_ref, gk_ref, dq_ref, dk_ref, dv_ref, dz_ref, dgq_ref, dgk_ref):
        bd = _head_block_ones(NA_W, 6)
        first = pl.program_id(0) == 0

        def norm_bwd(xv, gv, dy, dg_ref):
            ms = _dot(xv * xv, bd, prec=HI) * (1.0 / NA_DH)
            rstd = lax.rsqrt(ms + EPS)
            xhat = xv * rstd
            dxhat = dy * gv
            proj = _dot(dxhat * xhat, bd, prec=HI) * (1.0 / NA_DH)
            _acc_rows(dg_ref, dy * xhat, first)
            return rstd * (dxhat - xhat * proj)

        dz_ref[:, 0:NA_W] = norm_bwd(z_ref[:, 0:NA_W], gq_ref[...], dq_ref[...] * (NA_DH ** -0.5), dgq_ref).astype(BF16)
        dz_ref[:, NA_W:2 * NA_W] = norm_bwd(z_ref[:, NA_W:2 * NA_W], gk_ref[...], dk_ref[...], dgk_ref).astype(BF16)
        dz_ref[:, 2 * NA_W:3 * NA_W] = dv_ref[...].astype(BF16)

    blk = pl.BlockSpec((tm, NA_W), lambda i: (i, 0))
    gspec = pl.BlockSpec((1, NA_W), lambda i: (0, 0))
    acc = pl.BlockSpec((8, NA_W), lambda i: (0, 0))
    return pl.pallas_call(
        body, name=name, grid=(S // tm,),
        in_specs=[zv.spec(tm, 3 * NA_W, lambda i: i, lambda i: 0), gspec, gspec, blk, blk, blk],
        out_specs=[pl.BlockSpec((tm, 3 * NA_W), lambda i: (i, 0)), acc, acc],
        out_shape=[jax.ShapeDtypeStruct((S, 3 * NA_W), BF16), jax.ShapeDtypeStruct((8, NA_W), F32),
                   jax.ShapeDtypeStruct((8, NA_W), F32)],
        compiler_params=_cparams("arbitrary"),
    )(z, gq, gk, dqn, dkn, dv)


def _na_onehot():
    qc = np.arange(GRID_W)[:, None]
    kc = np.arange(GRID_W)[None, :]
    c0 = np.clip(qc - NA_WIN_C // 2, 0, GRID_W - NA_WIN_C)
    valid = (kc >= c0) & (kc < c0 + NA_WIN_C)
    dc = kc - qc + (NA_WIN_C - 1)
    e = np.zeros((32, GRID_W, GRID_W), np.float32)
    for d in range(2 * NA_WIN_C - 1):
        e[d] = valid & (dc == d)
    return e.reshape(32, GRID_W * GRID_W), valid.reshape(1, -1)


def _rpb_expand(rpb, name):
    e, valid = _na_onehot()
    negmask = np.where(valid, 0.0, NEG).astype(np.float32)
    nd = 2 * NA_WIN_R - 1
    r2 = jnp.pad(rpb.reshape(NA_HEADS * nd, 2 * NA_WIN_C - 1), ((0, 128 - NA_HEADS * nd), (0, 1)))

    def body(r_ref, e_ref, m_ref, o_ref):
        o_ref[...] = _dot(r_ref[...], e_ref[...], prec=HI) + m_ref[...]

    t = pl.pallas_call(body, name=name, out_shape=jax.ShapeDtypeStruct((128, GRID_W * GRID_W), F32))(
        r2, jnp.asarray(e), jnp.asarray(negmask))
    t = t[:NA_HEADS * nd].reshape(NA_HEADS, nd, GRID_W, GRID_W)
    return jnp.stack([jnp.concatenate([t[:, b + w] for w in range(NA_WIN_R)], axis=-1) for b in range(NA_WIN_R)], axis=1)


def _rpb_reduce(dbias, name):
    e, _ = _na_onehot()
    nd = 2 * NA_WIN_R - 1
    et = np.zeros((GRID_W * GRID_W, 128), np.float32)
    et[:, :32] = e.T
    sel = np.zeros((128, NA_HEADS * NA_WIN_R * NA_WIN_R), np.float32)
    for h in range(NA_HEADS):
        for b in range(NA_WIN_R):
            for w in range(NA_WIN_R):
                sel[h * nd + b + w, (h * NA_WIN_R + b) * NA_WIN_R + w] = 1.0
    x = dbias.reshape(NA_HEADS, NA_WIN_R, GRID_W, NA_WIN_R, GRID_W).transpose(0, 1, 3, 2, 4).reshape(-1, GRID_W * GRID_W)

    def body(x_ref, et_ref, sel_ref, o_ref):
        g = _dot(x_ref[...], et_ref[...], prec=HI)
        o_ref[...] = _dot(sel_ref[...], g, prec=HI)

    out = pl.pallas_call(body, name=name, out_shape=jax.ShapeDtypeStruct((128, 128), F32))(x, jnp.asarray(et), jnp.asarray(sel))
    return out[:NA_HEADS * nd, :2 * NA_WIN_C - 1].reshape(NA_HEADS, nd, 2 * NA_WIN_C - 1)


def _na_base(r, rows):
    return jnp.clip(r - NA_WIN_R // 2, 0, rows - NA_WIN_R) - r + (NA_WIN_R - 1)


def _na_probs(q_ref, k_ref, bias_ref, P, r0w):
    sl = [slice(128 * pp, 128 * pp + 128) for pp in range(P)]
    m = [_lane_mask(128, 64 * hh, 64) for hh in range(2)]
    kw = [k_ref[r0w, sl[pp]] for pp in range(P)]
    units = [(pp, hh) for pp in range(P) for hh in range(2)]
    qm = {u: (q_ref[:, sl[u[0]]].astype(F32) * m[u[1]]).astype(BF16) for u in units}
    s = {u: _dot(qm[u], kw[u[0]], "nt") + bias_ref[2 * u[0] + u[1], 0] for u in units}
    p = {}
    for u in units:
        e = jnp.exp(s[u] - jnp.max(s[u], axis=-1, keepdims=True))
        p[u] = e / jnp.sum(e, axis=-1, keepdims=True)
    return sl, m, kw, units, qm, p


NA_FWD_PAIRS = 4
NA_BWD_PAIRS = 2


def _na_attn(qn, kn, vb, bias, name, ride=None):
    S = qn.shape[0]
    rows = S // GRID_W
    nk = NA_WIN_R * GRID_W
    P = NA_FWD_PAIRS
    W = 128 * P

    def body(q_ref, k_ref, v_ref, b_ref, o_ref):
        r = pl.program_id(1)
        r0w = pl.ds(pl.multiple_of(jnp.clip(r - NA_WIN_R // 2, 0, rows - NA_WIN_R) * GRID_W, GRID_W), nk)
        sl, m, _, units, _, p = _na_probs(q_ref, k_ref, b_ref, P, r0w)
        o = {u: _dot(p[u].astype(BF16), v_ref[r0w, sl[u[0]]]) for u in units}
        for pp in range(P):
            o_ref[:, sl[pp]] = (o[pp, 0] * m[0] + o[pp, 1] * m[1]).astype(BF16)

    full = pl.BlockSpec((S, W), lambda g, r: (0, g))
    (o,), got = _ride_call(
        body, ride, name=name, grid=(NA_HEADS // (2 * P), rows),
        in_specs=[pl.BlockSpec((GRID_W, W), lambda g, r: (r, g)), full, full,
                  pl.BlockSpec((2 * P, 1, GRID_W, nk), lambda g, r: (g, _na_base(r, rows), 0, 0))],
        out_specs=[pl.BlockSpec((GRID_W, W), lambda g, r: (r, g))],
        out_shape=[jax.ShapeDtypeStruct((S, NA_W), BF16)], args=(qn, kn, vb, bias), semantics=("parallel", "arbitrary"))
    return o, got


def _na_attn_bwd(qn, kn, vb, bias, do, name, ride=None):
    S = qn.shape[0]
    rows = S // GRID_W
    nk = NA_WIN_R * GRID_W
    P = NA_BWD_PAIRS
    W = 128 * P

    def body(q_ref, k_ref, v_ref, b_ref, do_ref, dq_ref, dk_ref, dv_ref, db_ref):
        r = pl.program_id(1)

        @pl.when(r == 0)
        def _():
            dk_ref[...] = jnp.zeros_like(dk_ref)
            dv_ref[...] = jnp.zeros_like(dv_ref)

        r0w = pl.ds(pl.multiple_of(jnp.clip(r - NA_WIN_R // 2, 0, rows - NA_WIN_R) * GRID_W, GRID_W), nk)
        fresh = jnp.logical_or(r <= NA_WIN_R // 2, r > rows - NA_WIN_R // 2)
        sl, m, kw, units, qm, p = _na_probs(q_ref, k_ref, b_ref, P, r0w)
        dom = {u: (do_ref[:, sl[u[0]]].astype(F32) * m[u[1]]).astype(BF16) for u in units}
        dp = {u: _dot(dom[u], v_ref[r0w, sl[u[0]]], "nt") for u in units}
        dvw = {u: _dot(p[u].astype(BF16), dom[u], "tn") for u in units}
        ds = {u: p[u] * (dp[u] - jnp.sum(p[u] * dp[u], axis=-1, keepdims=True)) for u in units}

        @pl.when(fresh)
        def _():
            for u in units:
                db_ref[2 * u[0] + u[1], 0] = ds[u]

        @pl.when(jnp.logical_not(fresh))
        def _():
            for u in units:
                db_ref[2 * u[0] + u[1], 0] += ds[u]

        dsb = {u: ds[u].astype(BF16) for u in units}
        dq = {u: _dot(dsb[u], kw[u[0]]) for u in units}
        dkw = {u: _dot(dsb[u], qm[u], "tn") for u in units}
        for pp in range(P):
            dq_ref[:, sl[pp]] = dq[pp, 0] * m[0] + dq[pp, 1] * m[1]
            dk_ref[r0w, sl[pp]] += dkw[pp, 0] + dkw[pp, 1]
            dv_ref[r0w, sl[pp]] += dvw[pp, 0] + dvw[pp, 1]

    qblk = pl.BlockSpec((GRID_W, W), lambda g, r: (r, g))
    full = pl.BlockSpec((S, W), lambda g, r: (0, g))
    bblk = pl.BlockSpec((2 * P, 1, GRID_W, nk), lambda g, r: (g, _na_base(r, rows), 0, 0))
    return _ride_call(
        body, ride, name=name, grid=(NA_HEADS // (2 * P), rows),
        in_specs=[qblk, full, full, bblk, qblk], out_specs=[qblk, full, full, bblk],
        out_shape=[jax.ShapeDtypeStruct((S, NA_W), F32)] * 3 + [jax.ShapeDtypeStruct((NA_HEADS, NA_WIN_R, GRID_W, nk), F32)],
        args=(qn, kn, vb, bias, do), semantics=("parallel", "arbitrary"))


def _logsig(x):
    return jnp.minimum(x, 0.0) - jnp.log(1.0 + jnp.exp(-jnp.abs(x)))


def _gla_gates(z, c0, wg, bias, name, tm=512):
    S = z.shape[0]
    tm = min(tm, S)
    zv = _V(z, c0, 128)
    W = 2 * GLA_HEADS * GLA_DK

    def body(z_ref, w_ref, b_ref, o_ref):
        pre = _dot(z_ref[...].astype(BF16), w_ref[...]) + b_ref[...]
        o_ref[...] = _logsig(pre) * (1.0 / GLA_TAU)

    return pl.pallas_call(
        body, name=name, grid=(S // tm,),
        in_specs=[zv.spec(tm, 128, lambda i: i, lambda i: 0), pl.BlockSpec((128, W), lambda i: (0, 0)),
                  pl.BlockSpec((1, W), lambda i: (0, 0))],
        out_specs=pl.BlockSpec((tm, W), lambda i: (i, 0)), out_shape=jax.ShapeDtypeStruct((S, W), F32),
        compiler_params=_cparams("parallel"),
    )(z, wg, bias)


def _gla_gates_bwd(z, c0, wg, bias, dg_f, dg_b, name, tm=512):
    S = z.shape[0]
    tm = min(tm, S)
    zv = _V(z, c0, 128)
    W = 2 * GLA_HEADS * GLA_DK

    def body(z_ref, w_ref, b_ref, dgf_ref, dgb_ref, dp_ref, db_ref):
        pre = _dot(z_ref[...].astype(BF16), w_ref[...]) + b_ref[...]
        dg = jnp.concatenate([dgf_ref[...], dgb_ref[...]], axis=-1)
        dpre = dg * (1.0 / GLA_TAU) * jax.nn.sigmoid(-pre)
        dp_ref[...] = dpre.astype(BF16)
        _acc_rows(db_ref, dpre, pl.program_id(0) == 0)

    half = pl.BlockSpec((tm, W // 2), lambda i: (i, 0))
    return pl.pallas_call(
        body, name=name, grid=(S // tm,),
        in_specs=[zv.spec(tm, 128, lambda i: i, lambda i: 0), pl.BlockSpec((128, W), lambda i: (0, 0)),
                  pl.BlockSpec((1, W), lambda i: (0, 0)), half, half],
        out_specs=[pl.BlockSpec((tm, W), lambda i: (i, 0)), pl.BlockSpec((8, W), lambda i: (0, 0))],
        out_shape=[jax.ShapeDtypeStruct((S, W), BF16), jax.ShapeDtypeStruct((8, W), F32)],
        compiler_params=_cparams("arbitrary"),
    )(z, wg, bias, dg_f, dg_b)


def _gla_chunk_terms(zqk, g, p, rev):
    C = GLA_CHUNK
    i, j = _iota((C, C), 0), _iota((C, C), 1)
    cum = jnp.where((j >= i) if rev else (j <= i), 1.0, 0.0).astype(F32)
    q2 = zqk[:, 128 * p:128 * p + 128] * (GLA_DK ** -0.5)
    k2 = zqk[:, 256 + 128 * p:256 + 128 * p + 128]
    b2 = _dot(cum, g[:, 128 * p:128 * p + 128], prec=HI)
    bl2 = b2[0:1] if rev else b2[C - 1:C]
    eb = jnp.exp(b2)
    qe2 = q2 * eb
    ke2 = k2 * jnp.exp(-b2)
    kend2 = k2 * jnp.exp(bl2 - b2)
    dec2 = jnp.exp(bl2)
    tri = (j > i) if rev else (j <= i)
    return b2, bl2, eb, qe2, ke2, kend2, dec2, tri


def _row_to_col(row):
    eye = _iota((128, 128), 0) == _iota((128, 128), 1)
    return jnp.sum(jnp.where(eye, row, 0.0), axis=1, keepdims=True)


def _col_to_row(col):
    eye = _iota((128, 128), 0) == _iota((128, 128), 1)
    return jnp.sum(jnp.where(eye, col, 0.0), axis=0, keepdims=True)


GLA_GROUP = 4


def _gla_fwd(z, c_qk, c_v, gfb, name, ride=None):
    S = z.shape[0]
    C = GLA_CHUNK
    n = S // C
    G = math.gcd(GLA_GROUP, n)
    nb, GC = n // G, G * C
    WQK = 2 * GLA_HEADS * GLA_DK
    WV = GLA_HEADS * GLA_DV
    zqk, zvv = _V(z, c_qk, WQK), _V(z, c_v, WV)

    def body(qkf_ref, vf_ref, gf_ref, qkb_ref, vb_ref, gb_ref, of_ref, ob_ref, sf_ref, sb_ref, stf, stb):
        @pl.when(pl.program_id(0) == 0)
        def _():
            stf[...] = jnp.zeros_like(stf)
            stb[...] = jnp.zeros_like(stb)

        dirs = ((False, qkf_ref, vf_ref, gf_ref, of_ref, sf_ref, stf), (True, qkb_ref, vb_ref, gb_ref, ob_ref, sb_ref, stb))
        rows = lambda gi: slice(gi * C, (gi + 1) * C)
        pairs = [(d, gi, p) for d in range(2) for gi in range(G) for p in range(GLA_HEADS // 2)]
        heads = [(d, gi, p, hh) for d, gi, p in pairs for hh in range(2)]
        mask = [_lane_mask(128, 64 * hh, 64) for hh in range(2)]
        terms = {(d, gi, p): _gla_chunk_terms(dirs[d][1][rows(gi), :], dirs[d][3][rows(gi), :], p, dirs[d][0])
                 for d, gi, p in pairs}
        dec_col = {k: _row_to_col(t[6]) for k, t in terms.items()}
        vh = {(d, gi, h): dirs[d][2][rows(gi), 128 * h:128 * h + 128].astype(BF16)
              for d in range(2) for gi in range(G) for h in range(GLA_HEADS)}
        qm = {(d, gi, p, hh): (terms[d, gi, p][3] * mask[hh]).astype(BF16) for d, gi, p, hh in heads}
        a_raw = {(d, gi, p, hh): _dot(qm[d, gi, p, hh], terms[d, gi, p][4].astype(BF16), "nt") for d, gi, p, hh in heads}
        upd = {(d, gi, p, hh): _dot((terms[d, gi, p][5] * mask[hh]).astype(BF16), vh[d, gi, 2 * p + hh], "tn")
               for d, gi, p, hh in heads}
        intra = {(d, gi, p, hh): _dot(jnp.where(terms[d, gi, p][7], a_raw[d, gi, p, hh], 0.0).astype(BF16), vh[d, gi, 2 * p + hh])
                 for d, gi, p, hh in heads}
        state = {(d, h): dirs[d][6][h] for d in range(2) for h in range(GLA_HEADS)}
        for k in range(G):
            for d in range(2):
                gi = G - 1 - k if dirs[d][0] else k
                for p in range(GLA_HEADS // 2):
                    for hh in range(2):
                        h = 2 * p + hh
                        sp = state[d, h]
                        dirs[d][4][rows(gi), 128 * h:128 * h + 128] = intra[d, gi, p, hh] + _dot(qm[d, gi, p, hh], sp.astype(BF16))
                        dirs[d][5][gi, h] = sp
                        state[d, h] = dec_col[d, gi, p] * sp + upd[d, gi, p, hh]
        for d in range(2):
            for h in range(GLA_HEADS):
                dirs[d][6][h] = state[d, h]

    fw = lambda i: i
    bw = lambda i: nb - 1 - i
    zero = lambda i: 0
    in_specs = []
    for ix, col in ((fw, 0), (bw, 1)):
        in_specs += [zqk.spec(GC, WQK, ix, zero), zvv.spec(GC, WV, ix, zero),
                     pl.BlockSpec((GC, WQK // 2), functools.partial(lambda i, ix, col: (ix(i), col), ix=ix, col=col))]
    return _ride_call(
        body, ride, name=name, grid=(nb,), in_specs=in_specs,
        out_specs=[pl.BlockSpec((GC, WV), lambda i: (i, 0)), pl.BlockSpec((GC, WV), lambda i: (nb - 1 - i, 0)),
                   pl.BlockSpec((G, GLA_HEADS, 128, 128), lambda i: (i, 0, 0, 0)),
                   pl.BlockSpec((G, GLA_HEADS, 128, 128), lambda i: (nb - 1 - i, 0, 0, 0))],
        out_shape=[jax.ShapeDtypeStruct((S, WV), F32)] * 2 + [jax.ShapeDtypeStruct((n, GLA_HEADS, 128, 128), F32)] * 2,
        scratch_shapes=[pltpu.VMEM((GLA_HEADS, 128, 128), F32)] * 2, args=(z, z, gfb, z, z, gfb), semantics=("arbitrary",))


def _gla_bwd(z, c_qk, c_v, gfb, do, s_f, s_b, name, ride=None):
    S = z.shape[0]
    C = GLA_CHUNK
    n = S // C
    G = math.gcd(GLA_GROUP, n)
    nb, GC = n // G, G * C
    WQK = 2 * GLA_HEADS * GLA_DK
    WV = GLA_HEADS * GLA_DV
    zqk, zvv = _V(z, c_qk, WQK), _V(z, c_v, WV)

    def body(qkf_ref, vf_ref, gf_ref, dof_ref, sf_ref, qkb_ref, vb_ref, gb_ref, dob_ref, sb_ref,
             dqkf_ref, dvf_ref, dgf_ref, dqkb_ref, dvb_ref, dgb_ref, dstf, dstb):
        @pl.when(pl.program_id(0) == 0)
        def _():
            dstf[...] = jnp.zeros_like(dstf)
            dstb[...] = jnp.zeros_like(dstb)

        dirs = ((False, qkf_ref, vf_ref, gf_ref, dof_ref, sf_ref, dqkf_ref, dvf_ref, dgf_ref, dstf),
                (True, qkb_ref, vb_ref, gb_ref, dob_ref, sb_ref, dqkb_ref, dvb_ref, dgb_ref, dstb))
        rows = lambda gi: slice(gi * C, (gi + 1) * C)
        pairs = [(d, gi, p) for d in range(2) for gi in range(G) for p in range(GLA_HEADS // 2)]
        heads = [(d, gi, p, hh) for d, gi, p in pairs for hh in range(2)]
        mask = [_lane_mask(128, 64 * hh, 64) for hh in range(2)]
        T = {(d, gi, p): _gla_chunk_terms(dirs[d][1][rows(gi), :], dirs[d][3][rows(gi), :], p, dirs[d][0]) for d, gi, p in pairs}
        dec_col = {k: _row_to_col(t[6]) for k, t in T.items()}
        hd = lambda d, gi, p, hh: (d, gi, 2 * p + hh)
        vh = {(d, gi, h): dirs[d][2][rows(gi), 128 * h:128 * h + 128].astype(BF16)
              for d in range(2) for gi in range(G) for h in range(GLA_HEADS)}
        doh = {(d, gi, h): dirs[d][4][rows(gi), 128 * h:128 * h + 128].astype(BF16)
               for d in range(2) for gi in range(G) for h in range(GLA_HEADS)}
        sp = {(d, gi, h): dirs[d][5][gi, h] for d in range(2) for gi in range(G) for h in range(GLA_HEADS)}
        qm = {u: (T[u[:3]][3] * mask[u[3]]).astype(BF16) for u in heads}
        kem = {u: (T[u[:3]][4] * mask[u[3]]).astype(BF16) for u in heads}
        kendm = {u: (T[u[:3]][5] * mask[u[3]]).astype(BF16) for u in heads}
        a_raw = {u: _dot(qm[u], T[u[:3]][4].astype(BF16), "nt") for u in heads}
        da_raw = {u: _dot(doh[hd(*u)], vh[hd(*u)], "nt") for u in heads}
        w_upd = {u: _dot(qm[u], doh[hd(*u)], "tn") for u in heads}
        dqe_s = {u: _dot(doh[hd(*u)], sp[hd(*u)].astype(BF16), "nt") for u in heads}
        a = {u: jnp.where(T[u[:3]][7], a_raw[u], 0.0).astype(BF16) for u in heads}
        da = {u: jnp.where(T[u[:3]][7], da_raw[u], 0.0).astype(BF16) for u in heads}
        dqe = {u: _dot(da[u], kem[u]) + dqe_s[u] for u in heads}
        dke = {u: _dot(da[u], qm[u], "tn") for u in heads}
        dv_a = {u: _dot(a[u], doh[hd(*u)], "tn") for u in heads}
        ds = {}
        for d in range(2):
            cur = [dirs[d][9][h] for h in range(GLA_HEADS)]
            for gi in (range(G) if dirs[d][0] else reversed(range(G))):
                for p in range(GLA_HEADS // 2):
                    for hh in range(2):
                        h = 2 * p + hh
                        ds[d, gi, p, hh] = cur[h]
                        cur[h] = dec_col[d, gi, p] * cur[h] + w_upd[d, gi, p, hh]
            for h in range(GLA_HEADS):
                dirs[d][9][h] = cur[h]
        dsb = {u: ds[u].astype(BF16) for u in heads}
        dv_b = {u: _dot(kendm[u], dsb[u]) for u in heads}
        dkend = {u: _dot(vh[hd(*u)], dsb[u], "nt") * mask[u[3]] for u in heads}
        ddec = {u: _col_to_row(jnp.sum(ds[u] * sp[hd(*u)], axis=1, keepdims=True)) for u in heads}
        for u in heads:
            d, gi, h = hd(*u)
            dirs[d][7][rows(gi), 128 * h:128 * h + 128] = dv_a[u] + dv_b[u]
        i, j = _iota((C, C), 0), _iota((C, C), 1)
        for d, gi, p in pairs:
            rev = dirs[d][0]
            b2, bl2, eb, qe2, ke2, kend2, dec2, _ = T[d, gi, p]
            u0, u1 = (d, gi, p, 0), (d, gi, p, 1)
            dqe2, dke2, dkend2, ddec2 = dqe[u0] + dqe[u1], dke[u0] + dke[u1], dkend[u0] + dkend[u1], ddec[u0] + ddec[u1]
            dirs[d][6][rows(gi), 128 * p:128 * p + 128] = dqe2 * eb * (GLA_DK ** -0.5)
            dirs[d][6][rows(gi), 256 + 128 * p:256 + 128 * p + 128] = dke2 * jnp.exp(-b2) + dkend2 * jnp.exp(bl2 - b2)
            dkk = dkend2 * kend2
            dbl2 = jnp.sum(dkk, axis=0, keepdims=True) + ddec2 * dec2
            edge = _iota((C, 128), 0) == (0 if rev else C - 1)
            db2 = dqe2 * qe2 - dke2 * ke2 - dkk + jnp.where(edge, dbl2, 0.0)
            cum_t = jnp.where((j <= i) if rev else (j >= i), 1.0, 0.0).astype(F32)
            dirs[d][8][rows(gi), 128 * p:128 * p + 128] = _dot(cum_t, db2, prec=HI)

    fw = lambda i: nb - 1 - i
    bw = lambda i: i
    zero = lambda i: 0
    in_specs, out_specs = [], []
    for ix, col in ((fw, 0), (bw, 1)):
        blk = functools.partial(lambda i, ix: (ix(i), 0), ix=ix)
        in_specs += [zqk.spec(GC, WQK, ix, zero), zvv.spec(GC, WV, ix, zero),
                     pl.BlockSpec((GC, WQK // 2), functools.partial(lambda i, ix, col: (ix(i), col), ix=ix, col=col)),
                     pl.BlockSpec((GC, WV), blk),
                     pl.BlockSpec((G, GLA_HEADS, 128, 128), functools.partial(lambda i, ix: (ix(i), 0, 0, 0), ix=ix))]
        out_specs += [pl.BlockSpec((GC, WQK), blk), pl.BlockSpec((GC, WV), blk), pl.BlockSpec((GC, WQK // 2), blk)]
    shapes = [jax.ShapeDtypeStruct((S, WQK), F32), jax.ShapeDtypeStruct((S, WV), F32), jax.ShapeDtypeStruct((S, WQK // 2), F32)]
    return _ride_call(
        body, ride, name=name, grid=(nb,), in_specs=in_specs, out_specs=out_specs, out_shape=shapes * 2,
        scratch_shapes=[pltpu.VMEM((GLA_HEADS, 128, 128), F32)] * 2, args=(z, z, gfb, do, s_f, z, z, gfb, do, s_b),
        semantics=("arbitrary",))


def _gla_post(o_f, o_b, z, c_r, gn, name, tm=512):
    S, WV = o_f.shape
    tm = min(tm, S)
    zr = _V(z, c_r, WV)

    def body(of_ref, ob_ref, r_ref, g_ref, y_ref):
        gr = r_ref[...]
        sil = gr * jax.nn.sigmoid(gr)
        for h in range(GLA_HEADS):
            sl = slice(GLA_DV * h, GLA_DV * (h + 1))
            o = of_ref[:, sl] + ob_ref[:, sl]
            on = o * lax.rsqrt(jnp.mean(o * o, axis=-1, keepdims=True) + EPS) * g_ref[...]
            y_ref[:, sl] = (on * sil[:, sl]).astype(BF16)

    blk = pl.BlockSpec((tm, WV), lambda i: (i, 0))
    return pl.pallas_call(
        body, name=name, grid=(S // tm,),
        in_specs=[blk, blk, zr.spec(tm, WV, lambda i: i, lambda i: 0), pl.BlockSpec((1, GLA_DV), lambda i: (0, 0))],
        out_specs=blk, out_shape=jax.ShapeDtypeStruct((S, WV), BF16), compiler_params=_cparams("parallel"),
    )(o_f, o_b, z, gn)


def _gla_post_bwd(o_f, o_b, z, c_r, gn, dy, name, tm=512):
    S, WV = o_f.shape
    tm = min(tm, S)
    zr = _V(z, c_r, WV)

    def body(of_ref, ob_ref, r_ref, g_ref, dy_ref, do_ref, dr_ref, dg_ref):
        gr = r_ref[...]
        sig = jax.nn.sigmoid(gr)
        sil = gr * sig
        dyv = dy_ref[...].astype(F32)
        dgn = jnp.zeros((tm, GLA_DV), F32)
        for h in range(GLA_HEADS):
            sl = slice(GLA_DV * h, GLA_DV * (h + 1))
            o = of_ref[:, sl] + ob_ref[:, sl]
            rstd = lax.rsqrt(jnp.mean(o * o, axis=-1, keepdims=True) + EPS)
            xhat = o * rstd
            don = dyv[:, sl] * sil[:, sl]
            dr_ref[:, sl] = (dyv[:, sl] * xhat * g_ref[...] * (sig[:, sl] * (1.0 + gr[:, sl] * (1.0 - sig[:, sl])))).astype(BF16)
            dxhat = don * g_ref[...]
            do_ref[:, sl] = rstd * (dxhat - xhat * jnp.mean(dxhat * xhat, axis=-1, keepdims=True))
            dgn = dgn + don * xhat
        _acc_rows(dg_ref, dgn, pl.program_id(0) == 0)

    blk = pl.BlockSpec((tm, WV), lambda i: (i, 0))
    return pl.pallas_call(
        body, name=name, grid=(S // tm,),
        in_specs=[blk, blk, zr.spec(tm, WV, lambda i: i, lambda i: 0), pl.BlockSpec((1, GLA_DV), lambda i: (0, 0)), blk],
        out_specs=[blk, blk, pl.BlockSpec((8, GLA_DV), lambda i: (0, 0))],
        out_shape=[jax.ShapeDtypeStruct((S, WV), F32), jax.ShapeDtypeStruct((S, WV), BF16), jax.ShapeDtypeStruct((8, GLA_DV), F32)],
        compiler_params=_cparams("arbitrary"),
    )(o_f, o_b, z, gn, dy)


def _gla_assemble(dqk_f, dqk_b, dv_f, dv_b, dgr, name, tm=512):
    S = dqk_f.shape[0]
    tm = min(tm, S)

    def body(a_ref, b_ref, c_ref, d_ref, r_ref, o_ref):
        o_ref[:, 0:512] = (a_ref[...] + b_ref[...]).astype(BF16)
        o_ref[:, 512:1024] = (c_ref[...] + d_ref[...]).astype(BF16)
        o_ref[:, 1024:1536] = r_ref[...]

    blk = pl.BlockSpec((tm, 512), lambda i: (i, 0))
    return pl.pallas_call(
        body, name=name, grid=(S // tm,), in_specs=[blk] * 5, out_specs=pl.BlockSpec((tm, 1536), lambda i: (i, 0)),
        out_shape=jax.ShapeDtypeStruct((S, 1536), BF16), compiler_params=_cparams("parallel"),
    )(dqk_f, dqk_b, dv_f, dv_b, dgr)


def _rope(r, cos, sg):
    return r * cos + pltpu.roll(r, 64, 1) * sg


def _unrope(dy, cos, sg):
    return dy * cos + pltpu.roll(dy * sg, 64, 1)


def _mla_prep(z, c_q, c_kr, wuq, wukv, g_cq, g_ckv, g_q, g_k, cos, sg, name, tm=256):
    S = z.shape[0]
    tm = min(tm, S)
    zc, zk = _V(z, c_q, 2 * MLA_RANK), _V(z, c_kr, 128)
    inv = 1.0 / MLA_QK

    def body(zc_ref, zk_ref, wuq_ref, wukv_ref, gcq_ref, gckv_ref, gq_ref, gk_ref, cos_ref, sg_ref,
             q_ref, k_ref, v_ref, cqn_ref, ckvn_ref):
        def norm(xv, gv):
            return (xv * lax.rsqrt(jnp.mean(xv * xv, axis=-1, keepdims=True) + EPS) * gv).astype(BF16)

        cqn = norm(zc_ref[:, 0:MLA_RANK], gcq_ref[...])
        ckvn = norm(zc_ref[:, MLA_RANK:2 * MLA_RANK], gckv_ref[...])
        cqn_ref[...] = cqn
        ckvn_ref[...] = ckvn
        qf = _dot(cqn, wuq_ref[...])
        kv = _dot(ckvn, wukv_ref[...])
        kr = zk_ref[...]
        krss = jnp.sum(kr * kr, axis=-1, keepdims=True)
        cosv, sgv = cos_ref[...], sg_ref[...]
        gq, gk = gq_ref[...], gk_ref[...]
        for h in range(MLA_HEADS):
            qh = qf[:, MLA_SLOT * h:MLA_SLOT * (h + 1)]
            qhn = qh * lax.rsqrt(jnp.sum(qh * qh, axis=-1, keepdims=True) * inv + EPS) * gq
            q_ref[:, MLA_SLOT * h:MLA_SLOT * h + 128] = (qhn[:, 0:128] * MLA_QSCALE).astype(BF16)
            q_ref[:, MLA_SLOT * h + 128:MLA_SLOT * (h + 1)] = (_rope(qhn[:, 128:256], cosv, sgv) * MLA_QSCALE).astype(BF16)
            kn = kv[:, 256 * h:256 * h + 128]
            rstd = lax.rsqrt((jnp.sum(kn * kn, axis=-1, keepdims=True) + krss) * inv + EPS)
            k_ref[:, MLA_SLOT * h:MLA_SLOT * h + 128] = (kn * rstd * gk[:, 0:128]).astype(BF16)
            k_ref[:, MLA_SLOT * h + 128:MLA_SLOT * (h + 1)] = _rope(kr * rstd * gk[:, 128:256], cosv, sgv).astype(BF16)
            v_ref[:, 128 * h:128 * (h + 1)] = kv[:, 256 * h + 128:256 * (h + 1)].astype(BF16)

    row = lambda w: pl.BlockSpec((tm, w), lambda i: (i, 0))
    const = lambda r, w: pl.BlockSpec((r, w), lambda i: (0, 0))
    W = MLA_HEADS * MLA_SLOT
    return pl.pallas_call(
        body, name=name, grid=(S // tm,),
        in_specs=[zc.spec(tm, 2 * MLA_RANK, lambda i: i, lambda i: 0), zk.spec(tm, 128, lambda i: i, lambda i: 0),
                  const(MLA_RANK, W), const(MLA_RANK, W), const(1, MLA_RANK), const(1, MLA_RANK), const(1, MLA_SLOT),
                  const(1, MLA_SLOT), row(128), row(128)],
        out_specs=[row(W), row(W), row(MLA_HEADS * MLA_V), row(MLA_RANK), row(MLA_RANK)],
        out_shape=[jax.ShapeDtypeStruct((S, W), BF16), jax.ShapeDtypeStruct((S, W), BF16),
                   jax.ShapeDtypeStruct((S, MLA_HEADS * MLA_V), BF16), jax.ShapeDtypeStruct((S, MLA_RANK), BF16),
                   jax.ShapeDtypeStruct((S, MLA_RANK), BF16)],
        compiler_params=_cparams("parallel"),
    )(z, z, wuq, wukv, g_cq, g_ckv, g_q, g_k, cos, sg)


def _mla_prep_bwd(z, c_kr, cqn, ckvn, wuq, wukv, g_q, g_k, cos, sg, dq, dk, dv, name, tm=256):
    S = z.shape[0]
    tm = min(tm, S)
    zk = _V(z, c_kr, 128)
    inv = 1.0 / MLA_QK

    def body(zk_ref, cqn_ref, ckvn_ref, wuq_ref, wukv_ref, gq_ref, gk_ref, cos_ref, sg_ref, dq_ref, dk_ref, dv_ref,
             dqf_ref, dkv_ref, dkr_ref, dgq_ref, dgk_ref):
        first = pl.program_id(0) == 0
        qf = _dot(cqn_ref[...], wuq_ref[...])
        kv = _dot(ckvn_ref[...], wukv_ref[...])
        kr = zk_ref[...]
        krss = jnp.sum(kr * kr, axis=-1, keepdims=True)
        cosv, sgv = cos_ref[...], sg_ref[...]
        gq, gk = gq_ref[...], gk_ref[...]
        dkr = jnp.zeros((tm, 128), F32)
        dgq = jnp.zeros((tm, MLA_SLOT), F32)
        dgkn = jnp.zeros((tm, 128), F32)
        dgkr = jnp.zeros((tm, 128), F32)
        for h in range(MLA_HEADS):
            qh = qf[:, MLA_SLOT * h:MLA_SLOT * (h + 1)]
            rstd = lax.rsqrt(jnp.sum(qh * qh, axis=-1, keepdims=True) * inv + EPS)
            xhat = qh * rstd
            dyn = jnp.concatenate([dq_ref[:, MLA_SLOT * h:MLA_SLOT * h + 128],
                                   _unrope(dq_ref[:, MLA_SLOT * h + 128:MLA_SLOT * (h + 1)], cosv, sgv)], axis=-1)
            dxhat = dyn * gq
            dqf_ref[:, MLA_SLOT * h:MLA_SLOT * (h + 1)] = (
                rstd * (dxhat - xhat * (jnp.sum(dxhat * xhat, axis=-1, keepdims=True) * inv))).astype(BF16)
            dgq = dgq + dyn * xhat

            kn = kv[:, 256 * h:256 * h + 128]
            rstd = lax.rsqrt((jnp.sum(kn * kn, axis=-1, keepdims=True) + krss) * inv + EPS)
            xn, xr = kn * rstd, kr * rstd
            dyn_n = dk_ref[:, MLA_SLOT * h:MLA_SLOT * h + 128] * (1.0 / MLA_QSCALE)
            dyn_r = _unrope(dk_ref[:, MLA_SLOT * h + 128:MLA_SLOT * (h + 1)] * (1.0 / MLA_QSCALE), cosv, sgv)
            dxn, dxr = dyn_n * gk[:, 0:128], dyn_r * gk[:, 128:256]
            proj = (jnp.sum(dxn * xn, axis=-1, keepdims=True) + jnp.sum(dxr * xr, axis=-1, keepdims=True)) * inv
            dkv_ref[:, 256 * h:256 * h + 128] = (rstd * (dxn - xn * proj)).astype(BF16)
            dkv_ref[:, 256 * h + 128:256 * (h + 1)] = dv_ref[:, 128 * h:128 * (h + 1)].astype(BF16)
            dkr = dkr + rstd * (dxr - xr * proj)
            dgkn = dgkn + dyn_n * xn
            dgkr = dgkr + dyn_r * xr
        dkr_ref[...] = dkr.astype(BF16)
        _acc_rows(dgq_ref, dgq, first)
        _acc_rows(dgk_ref, jnp.concatenate([dgkn, dgkr], axis=-1), first)

    row = lambda w: pl.BlockSpec((tm, w), lambda i: (i, 0))
    const = lambda r, w: pl.BlockSpec((r, w), lambda i: (0, 0))
    W = MLA_HEADS * MLA_SLOT
    return pl.pallas_call(
        body, name=name, grid=(S // tm,),
        in_specs=[zk.spec(tm, 128, lambda i: i, lambda i: 0), row(MLA_RANK), row(MLA_RANK), const(MLA_RANK, W),
                  const(MLA_RANK, W), const(1, MLA_SLOT), const(1, MLA_SLOT), row(128), row(128), row(W), row(W),
                  row(MLA_HEADS * MLA_V)],
        out_specs=[row(W), row(W), row(128), const(8, MLA_SLOT), const(8, MLA_SLOT)],
        out_shape=[jax.ShapeDtypeStruct((S, W), BF16), jax.ShapeDtypeStruct((S, W), BF16), jax.ShapeDtypeStruct((S, 128), BF16),
                   jax.ShapeDtypeStruct((8, MLA_SLOT), F32), jax.ShapeDtypeStruct((8, MLA_SLOT), F32)],
        compiler_params=_cparams("arbitrary"),
    )(z, cqn, ckvn, wuq, wukv, g_q, g_k, cos, sg, dq, dk, dv)


def _exp2_rows(s2):
    e = jnp.exp2(s2 - jnp.max(s2, axis=-1, keepdims=True))
    return e, 1.0 / jnp.sum(e, axis=-1, keepdims=True)


def _mla_attn(q, k, v, name, tq=256, ride=None):
    S = q.shape[0]
    tq = min(tq, S)

    def body(q_ref, k_ref, v_ref, o_ref):
        e, rl = _exp2_rows(_dot(q_ref[...], k_ref[...], "nt"))
        o_ref[...] = (_dot(e.astype(BF16), v_ref[...]) * rl).astype(BF16)

    (o,), got = _ride_call(
        body, ride, name=name, grid=(MLA_HEADS, S // tq),
        in_specs=[pl.BlockSpec((tq, MLA_SLOT), lambda h, i: (i, h)), pl.BlockSpec((S, MLA_SLOT), lambda h, i: (0, h)),
                  pl.BlockSpec((S, MLA_V), lambda h, i: (0, h))],
        out_specs=[pl.BlockSpec((tq, MLA_V), lambda h, i: (i, h))],
        out_shape=[jax.ShapeDtypeStruct((S, MLA_HEADS * MLA_V), BF16)], args=(q, k, v), semantics=("parallel", "parallel"))
    return o, got


def _mla_attn_bwd(q, k, v, do, name, tq=256, ride=None):
    S = q.shape[0]
    tq = min(tq, S)
    scale = MLA_QK ** -0.5

    def body(q_ref, k_ref, v_ref, do_ref, dq_ref, dk_ref, dv_ref):
        @pl.when(pl.program_id(1) == 0)
        def _():
            dk_ref[...] = jnp.zeros_like(dk_ref)
            dv_ref[...] = jnp.zeros_like(dv_ref)

        qv, kvv, dov = q_ref[...], k_ref[...], do_ref[...].astype(F32)
        e, rl = _exp2_rows(_dot(qv, kvv, "nt"))
        dp = _dot((dov * (scale * rl)).astype(BF16), v_ref[...], "nt")
        ds = (e * (dp - jnp.sum(e * dp, axis=-1, keepdims=True) * rl)).astype(BF16)
        dq_ref[...] = _dot(ds, kvv)
        dk_ref[...] += _dot(ds, qv, "tn")
        dv_ref[...] += _dot(e.astype(BF16), (dov * rl).astype(BF16), "tn")

    W = MLA_HEADS * MLA_SLOT
    return _ride_call(
        body, ride, name=name, grid=(MLA_HEADS, S // tq),
        in_specs=[pl.BlockSpec((tq, MLA_SLOT), lambda h, i: (i, h)), pl.BlockSpec((S, MLA_SLOT), lambda h, i: (0, h)),
                  pl.BlockSpec((S, MLA_V), lambda h, i: (0, h)), pl.BlockSpec((tq, MLA_V), lambda h, i: (i, h))],
        out_specs=[pl.BlockSpec((tq, MLA_SLOT), lambda h, i: (i, h)), pl.BlockSpec((S, MLA_SLOT), lambda h, i: (0, h)),
                   pl.BlockSpec((S, MLA_V), lambda h, i: (0, h))],
        out_shape=[jax.ShapeDtypeStruct((S, W), F32), jax.ShapeDtypeStruct((S, W), F32),
                   jax.ShapeDtypeStruct((S, MLA_HEADS * MLA_V), F32)],
        args=(q, k, v, do), semantics=("parallel", "arbitrary"))


def _merge(ys, ws, z, name, tm=256):
    S = z.shape[0]
    D = ws[0].shape[1]
    tm = min(tm, S)
    zg = _V(z, 0, 3 * D)

    def body(y0, y1, y2, w0, w1, w2, g_ref, m_ref, p0, p1, p2):
        acc = jnp.zeros((tm, D), F32)
        for i, (y_ref, w_ref, p_ref) in enumerate(((y0, w0, p0), (y1, w1, p1), (y2, w2, p2))):
            pv = _dot(y_ref[...], w_ref[...])
            p_ref[...] = pv.astype(BF16)
            acc = acc + jax.nn.sigmoid(g_ref[:, D * i:D * (i + 1)]) * pv
        m_ref[...] = acc.astype(BF16)

    yb = pl.BlockSpec((tm, ys[0].shape[1]), lambda i: (i, 0))
    wb = pl.BlockSpec(ws[0].shape, lambda i: (0, 0))
    ob = pl.BlockSpec((tm, D), lambda i: (i, 0))
    return pl.pallas_call(
        body, name=name, grid=(S // tm,), in_specs=[yb] * 3 + [wb] * 3 + [zg.spec(tm, 3 * D, lambda i: i, lambda i: 0)],
        out_specs=[ob] * 4, out_shape=[jax.ShapeDtypeStruct((S, D), BF16)] * 4, compiler_params=_cparams("parallel"),
    )(*ys, *ws, z)


def _merge_bwd(dmixed, ps, z, name, tm=256):
    S, D = dmixed.shape
    tm = min(tm, S)
    zg = _V(z, 0, 3 * D)

    def body(dm_ref, p0, p1, p2, g_ref, d0, d1, d2, dg_ref):
        dm = dm_ref[...]
        for i, (p_ref, d_ref) in enumerate(((p0, d0), (p1, d1), (p2, d2))):
            gt = jax.nn.sigmoid(g_ref[:, D * i:D * (i + 1)])
            d_ref[...] = (dm * gt).astype(BF16)
            dg_ref[:, D * i:D * (i + 1)] = (dm * p_ref[...].astype(F32) * gt * (1.0 - gt)).astype(BF16)

    ob = pl.BlockSpec((tm, D), lambda i: (i, 0))
    return pl.pallas_call(
        body, name=name, grid=(S // tm,), in_specs=[ob] * 4 + [zg.spec(tm, 3 * D, lambda i: i, lambda i: 0)],
        out_specs=[ob] * 3 + [pl.BlockSpec((tm, 3 * D), lambda i: (i, 0))],
        out_shape=[jax.ShapeDtypeStruct((S, D), BF16)] * 3 + [jax.ShapeDtypeStruct((S, 3 * D), BF16)],
        compiler_params=_cparams("parallel"),
    )(dmixed, *ps, z)


def _loss_head(y, target, name, tm=512):
    S, D = y.shape
    tm = min(tm, S)

    def body(y_ref, t_ref, dy_ref, l_ref):
        e = y_ref[...] - t_ref[...]
        dy_ref[...] = e * (1.0 / D)
        sq = e * e
        part = jnp.sum(sq.reshape(tm // 8, 8, D), axis=0)
        part = jnp.sum(part.reshape(8, D // 128, 128), axis=1) * (0.5 / D)

        @pl.when(pl.program_id(0) == 0)
        def _():
            l_ref[...] = part

        @pl.when(pl.program_id(0) != 0)
        def _():
            l_ref[...] += part

    blk = pl.BlockSpec((tm, D), lambda i: (i, 0))
    return pl.pallas_call(
        body, name=name, grid=(S // tm,), in_specs=[blk, blk], out_specs=[blk, pl.BlockSpec((8, 128), lambda i: (0, 0))],
        out_shape=[jax.ShapeDtypeStruct((S, D), F32), jax.ShapeDtypeStruct((8, 128), F32)],
        compiler_params=_cparams("arbitrary"),
    )(y, target)


def _fold(parts, name, fold=None):
    L, _, W = parts.shape
    assert L <= 8

    def body(*refs):
        p_ref, o_ref = refs[0], refs[-1]
        rows = [jnp.sum(p_ref[l], axis=0, keepdims=True) for l in range(L)]
        rows += [jnp.zeros((1, W), F32)] * (8 - L)
        sums = jnp.concatenate(rows, axis=0)
        o_ref[...] = sums if fold is None else _dot(sums, refs[1][...], prec=HI)

    args = (parts,) if fold is None else (parts, jnp.asarray(fold))
    wout = W if fold is None else 128
    return pl.pallas_call(body, name=name, out_shape=jax.ShapeDtypeStruct((8, wout), F32))(*args)[:L]


def _adamw(w, g, m, v, name, q=None):
    R, C = w.shape
    tr = R
    for cand in (512, 256, 128, 64, 32, 16, 8):
        if R % cand == 0 and cand * C * 4 <= 2 * 2**20:
            tr = cand
            break

    def body(*refs):
        if q is None:
            w_ref, g_ref, m_ref, v_ref, d_ref, nm_ref, nv_ref = refs
            gv = g_ref[...]
        else:
            w_ref, g_ref, q_ref, m_ref, v_ref, go_ref, d_ref, nm_ref, nv_ref = refs
            gv = g_ref[...] + q_ref[...]
            go_ref[...] = gv
        mn = ADAM_B1 * m_ref[...] + (1.0 - ADAM_B1) * gv
        vn = ADAM_B2 * v_ref[...] + (1.0 - ADAM_B2) * (gv * gv)
        nm_ref[...] = mn
        nv_ref[...] = vn
        m_hat = mn / (1.0 - ADAM_B1 ** ADAM_STEP)
        v_hat = vn / (1.0 - ADAM_B2 ** ADAM_STEP)
        d_ref[...] = -ADAM_LR * (m_hat / (jnp.sqrt(v_hat) + ADAM_EPS) + ADAM_WD * w_ref[...])

    blk = pl.BlockSpec((tr, C), lambda i: (i, 0))
    args = (w, g, m, v) if q is None else (w, g, q, m, v)
    nout = 3 if q is None else 4
    return pl.pallas_call(
        body, name=name, grid=(R // tr,), in_specs=[blk] * len(args), out_specs=[blk] * nout,
        out_shape=[jax.ShapeDtypeStruct((R, C), F32)] * nout, compiler_params=_cparams("parallel"),
    )(*args)


def _sibling_exchange(srcs, name):
    n = len(srcs)

    def body(*refs):
        src_refs, dst_refs = refs[:n], refs[n:2 * n]
        send_sems, recv_sems = refs[2 * n:]
        x, y, c = lax.axis_index("x"), lax.axis_index("y"), lax.axis_index("c")
        copies = [pltpu.make_async_remote_copy(src_ref=src_refs[t], dst_ref=dst_refs[t], send_sem=send_sems.at[t],
                                               recv_sem=recv_sems.at[t], device_id=(x, y, 1 - c), device_id_type=MESH)
                  for t in range(n)]
        for cp in copies:
            cp.start()
        for cp in copies:
            cp.wait()

    return pl.pallas_call(
        body, name=name, in_specs=[_ANY] * n, out_specs=[_ANY] * n,
        out_shape=[jax.ShapeDtypeStruct(s.shape, s.dtype) for s in srcs],
        scratch_shapes=[pltpu.SemaphoreType.DMA((n,)), pltpu.SemaphoreType.DMA((n,))],
    )(*srcs)


def _allreduce_small(v, name):
    R = v.shape[0]

    def body(v_ref, o_ref, slots, send_sems, recv_sems):
        x, y, c = lax.axis_index("x"), lax.axis_index("y"), lax.axis_index("c")
        me = 4 * x + 2 * y + c
        slots[me] = v_ref[...]
        sent = []
        for r in range(1, 8):
            fx, fy, fc = (r >> 2) & 1, (r >> 1) & 1, r & 1
            px, py, pc = (1 - x) if fx else x, (1 - y) if fy else y, (1 - c) if fc else c
            peer = 4 * px + 2 * py + pc

            def copy(slot, r=r, px=px, py=py, pc=pc):
                return pltpu.make_async_remote_copy(
                    src_ref=v_ref, dst_ref=slots.at[slot], send_sem=send_sems.at[r - 1], recv_sem=recv_sems.at[r - 1],
                    device_id=(px, py, pc), device_id_type=MESH)

            cp = copy(me)
            cp.start()
            sent.append((cp, copy(peer)))
        for cp, arrival in sent:
            cp.wait_send()
            arrival.wait_recv()
        acc = slots[0]
        for k in range(1, 8):
            acc = acc + slots[k]
        o_ref[...] = acc

    vm = pl.BlockSpec(memory_space=pltpu.VMEM)
    return pl.pallas_call(
        body, name=name, in_specs=[vm], out_specs=vm, out_shape=jax.ShapeDtypeStruct((R, 128), F32),
        scratch_shapes=[pltpu.VMEM((8, R, 128), F32), pltpu.SemaphoreType.DMA((7,)), pltpu.SemaphoreType.DMA((7,))],
    )(v)


def _sum4(recv, name, tr=512):
    _, R, W = recv.shape
    tr = _tile(R, tr)
    assert R % tr == 0

    def body(r_ref, o_ref):
        o_ref[...] = ((r_ref[0].astype(F32) + r_ref[1].astype(F32)) + r_ref[2].astype(F32)) + r_ref[3].astype(F32)

    return pl.pallas_call(
        body, name=name, grid=(R // tr,), in_specs=[pl.BlockSpec((4, tr, W), lambda i: (0, i, 0))],
        out_specs=pl.BlockSpec((tr, W), lambda i: (i, 0)), out_shape=jax.ShapeDtypeStruct((R, W), F32),
        compiler_params=_cparams("parallel"),
    )(recv)


W_NAMES = ("ffn1_norm", "ffn1_w1", "ffn1_w3", "ffn1_w2", "mix_norm", "w_in", "na_q_norm", "na_k_norm", "na_rpb",
           "gla_gf_up", "gla_gf_bias", "gla_gb_up", "gla_gb_bias", "gla_out_norm", "mla_cq_norm", "mla_ckv_norm",
           "mla_w_uq", "mla_w_ukv", "mla_q_norm", "mla_k_norm", "w_br_na", "w_br_gla", "w_br_mla", "w_out",
           "ffn2_norm", "ffn2_w1", "ffn2_w3", "ffn2_w2")
SHARDED = {"ffn1_w1": 2, "ffn1_w3": 2, "ffn1_w2": 1, "w_in": 2, "gla_gf_up": 2, "gla_gb_up": 2, "mla_w_uq": 2,
           "mla_w_ukv": 2, "w_br_na": 2, "w_br_gla": 2, "w_br_mla": 2, "w_out": 1, "ffn2_w1": 2, "ffn2_w3": 2,
           "ffn2_w2": 1}
REPLICATED = tuple(n for n in W_NAMES if n not in SHARDED)
FFN_W = ("ffn1_w1", "ffn1_w3", "ffn1_w2", "ffn2_w1", "ffn2_w3", "ffn2_w2")


def _win_layout(w, D):
    z = lambda n: jnp.zeros(w.shape[:-1] + (n,), w.dtype)
    return jnp.concatenate([w[..., O_GATES:], w[..., :O_GFL], w[..., O_CQ:O_KR], w[..., O_GFL:O_CQ], z(96),
                            w[..., O_KR:O_KR + 32], z(32), w[..., O_KR + 32:O_KR + 64], z(32)], axis=-1)


def _win_unlayout(dw, D):
    g = 3 * D
    return jnp.concatenate([dw[..., g:g + O_GFL], dw[..., g + 3584:g + 3616], dw[..., g + 3072:g + 3584],
                            dw[..., g + 3712:g + 3744], dw[..., g + 3776:g + 3808], dw[..., :g]], axis=-1)


def _uq_layout(w):
    s = w.shape[:-1]
    w = w.reshape(s + (MLA_HEADS, MLA_QK))
    z = jnp.zeros(s + (MLA_HEADS, 32), w.dtype)
    return jnp.concatenate([w[..., :160], z, w[..., 160:], z], axis=-1).reshape(s + (MLA_HEADS * MLA_SLOT,))


def _uq_unlayout(dw):
    s = dw.shape[:-1]
    dw = dw.reshape(s + (MLA_HEADS, MLA_SLOT))
    return jnp.concatenate([dw[..., :160], dw[..., 192:224]], axis=-1).reshape(s + (MLA_HEADS * MLA_QK,))


def _slot_layout(g):
    z = jnp.zeros(g.shape[:-1] + (32,), g.dtype)
    return jnp.concatenate([g[..., :160], z, g[..., 160:], z], axis=-1)


def _slot_unlayout(g):
    return jnp.concatenate([g[..., :160], g[..., 192:224]], axis=-1)


def _layer_fwd(x, w, cos, sg, rides):
    D = x.shape[1]
    NA, GL, ML, LR, KR = 3 * D, 3 * D + 1536, 3 * D + 3072, 3 * D + 3584, 3 * D + 3712
    got = {}
    x1, f1, arrived = _ffn_fwd(x, w["ffn1_norm"], w["ffn1_w1"], w["ffn1_w3"], w["ffn1_w2"], "ffn1", ride=rides.get("ffn1_up"))
    got.update(arrived)
    h = _rms_fwd(x1, w["mix_norm"], "mix_rms")
    nz = w["w_in"].shape[1]
    z, arrived = _mm([(h, w["w_in"])], "nn", F32, "w_in", tm=512, tn=_tile(nz, 1280), ride=rides.get("w_in", []))
    got.update(arrived)
    qn, kn, vb = _na_prep(z, NA, w["na_gq"], w["na_gk"], "na_prep")
    bias = _rpb_expand(w["na_rpb"], "rpb_expand")
    y_na, arrived = _na_attn(qn, kn, vb, bias, "na_attn", ride=rides.get("na_attn"))
    got.update(arrived)
    gfb = _gla_gates(z, LR, w["gla_wg"], w["gla_gbias"], "gla_gates")
    (o_f, o_b, s_f, s_b), arrived = _gla_fwd(z, GL, GL + 512, gfb, "gla_fwd", ride=rides.get("gla_fwd"))
    got.update(arrived)
    y_gla = _gla_post(o_f, o_b, z, GL + 1024, w["gla_out_norm"], "gla_post")
    q, k, v, cqn, ckvn = _mla_prep(z, ML, KR, w["mla_wuq"], w["mla_w_ukv"], w["mla_cq_norm"], w["mla_ckv_norm"],
                                   w["mla_gq"], w["mla_gk"], cos, sg, "mla_prep")
    y_mla, arrived = _mla_attn(q, k, v, "mla_attn", ride=rides.get("mla_attn"))
    got.update(arrived)
    mixed, p0, p1, p2 = _merge([y_na, y_gla, y_mla], [w["w_br_na"], w["w_br_gla"], w["w_br_mla"]], z, "merge")
    x2 = _mm([(mixed, w["w_out"])], "nn", F32, "w_out", tm=512, tn=1024, res=x1)
    x3, f2, _ = _ffn_fwd(x2, w["ffn2_norm"], got["ffn2_w1"], got["ffn2_w3"], got["ffn2_w2"], "ffn2")
    saved = dict(x=x, x1=x1, x2=x2, f1=f1, f2=f2, h=h, z=z, qn=qn, kn=kn, vb=vb, bias=bias, y_na=y_na, gfb=gfb, o_f=o_f,
                 o_b=o_b, s_f=s_f, s_b=s_b, y_gla=y_gla, q=q, k=k, v=v, cqn=cqn, ckvn=ckvn, y_mla=y_mla, mixed=mixed,
                 p0=p0, p1=p1, p2=p2)
    return x3, saved, got


def _split4(a, axis):
    n = a.shape[axis] // 4
    return jnp.stack([lax.slice_in_dim(a, j * n, (j + 1) * n, axis=axis) for j in range(4)]).astype(BF16)


def _layer_bwd(dx3, w, sv, cos, sg, bufs, recv, layer, prev):
    D = dx3.shape[1]
    at_layer = lambda chip: (chip, layer)
    pick = lambda *names: [prev[n] for n in names if n in prev]
    recv = dict(recv)
    NA, GL, ML, LR, KR = 3 * D, 3 * D + 1536, 3 * D + 3072, 3 * D + 3584, 3 * D + 3712
    z = sv["z"]
    g = {}
    dx2, g["ffn2_norm"], (g["ffn2_w1"], g["ffn2_w3"], g["ffn2_w2"]), got = _ffn_bwd(
        dx3, sv["x2"], w["ffn2_norm"], w["ffn2_w1"], w["ffn2_w3"], w["ffn2_w2"], sv["f2"], "ffn2",
        (bufs["ffn2_w1"], bufs["ffn2_w3"], bufs["ffn2_w2"]), layer, ride_down=pick("ffn1_w1"), ride_dh=pick("ffn1_w3"))
    recv.update(got)
    dmixed = _mm([(dx2, w["w_out"])], "nt", F32, "w_out_dx", tm=512, tn=512)
    g["w_out"] = _mm([(sv["mixed"], dx2)], "tn", F32, "w_out_dw", tm=D, tn=256)
    d0, d1, d2, dgates = _merge_bwd(dmixed, [sv["p0"], sv["p1"], sv["p2"]], z, "merge_bwd")
    dys = []
    for d, y, nm, dt in ((d0, sv["y_na"], "w_br_na", BF16), (d1, sv["y_gla"], "w_br_gla", F32), (d2, sv["y_mla"], "w_br_mla", BF16)):
        dys.append(_mm([(d, w[nm])], "nt", dt, nm + "_dy", tm=512, tn=512))
        g[nm] = _mm([(y, d)], "tn", F32, nm + "_dw", tm=512, tn=512)
    (dqn, dkn, dvn, dbias), got = _na_attn_bwd(sv["qn"], sv["kn"], sv["vb"], sv["bias"], dys[0], "na_attn_bwd",
                                               ride=pick("ffn1_w2", "mla_w_uq", "mla_w_ukv", "gla_gf_up", "gla_gb_up"))
    recv.update(got)
    dz_na, g["na_gq"], g["na_gk"] = _na_prep_bwd(z, NA, w["na_gq"], w["na_gk"], dqn, dkn, dvn, "na_prep_bwd")
    g["na_rpb"] = _rpb_reduce(dbias, "rpb_reduce")
    do, dgr, g["gla_out_norm"] = _gla_post_bwd(sv["o_f"], sv["o_b"], z, GL + 1024, w["gla_out_norm"], dys[1], "gla_post_bwd")
    (dqk_f, dv_f, dg_f, dqk_b, dv_b, dg_b), _ = _gla_bwd(z, GL, GL + 512, sv["gfb"], do, sv["s_f"], sv["s_b"], "gla_bwd")
    dz_gla = _gla_assemble(dqk_f, dqk_b, dv_f, dv_b, dgr, "gla_assemble")
    dpre, g["gla_gbias"] = _gla_gates_bwd(z, LR, w["gla_wg"], w["gla_gbias"], dg_f, dg_b, "gla_gates_bwd")
    g["gla_wg"] = _mm([(_V(z, LR, 128), dpre)], "tn", F32, "gla_wg_dw", tm=128, tn=512)
    dz_lr = _mm([(dpre, w["gla_wg"])], "nt", BF16, "gla_wg_dz", tm=512, tn=128)
    own = lambda n: _Ride(n, g[n], at_layer, recv[n], at_layer)
    (dq, dk, dv), got = _mla_attn_bwd(sv["q"], sv["k"], sv["v"], dys[2], "mla_attn_bwd",
                                      ride=pick("w_in") + [own("ffn2_w1"), own("ffn2_w3")])
    recv.update(got)
    dqf, dkv, dz_kr, g["mla_gq"], g["mla_gk"] = _mla_prep_bwd(
        z, KR, sv["cqn"], sv["ckvn"], w["mla_wuq"], w["mla_w_ukv"], w["mla_gq"], w["mla_gk"], cos, sg, dq, dk, dv, "mla_prep_bwd")
    g["mla_wuq"] = _mm([(sv["cqn"], dqf)], "tn", F32, "mla_wuq_dw", tm=256, tn=512)
    g["mla_w_ukv"] = _mm([(sv["ckvn"], dkv)], "tn", F32, "mla_wukv_dw", tm=256, tn=512)
    dcqn = _mm([(dqf, w["mla_wuq"])], "nt", F32, "mla_wuq_dx", tm=512, tn=256)
    dckvn = _mm([(dkv, w["mla_w_ukv"])], "nt", F32, "mla_wukv_dx", tm=512, tn=256)
    dz_cq, dg_cq = _rms_bwd(_V(z, ML, MLA_RANK), w["mla_cq_norm"], dcqn, "mla_cq_rms_bwd", out_dtype=BF16)
    dz_ckv, dg_ckv = _rms_bwd(_V(z, ML + MLA_RANK, MLA_RANK), w["mla_ckv_norm"], dckvn, "mla_ckv_rms_bwd", out_dtype=BF16)
    g["mla_cq_norm"], g["mla_ckv_norm"] = dg_cq[0:1], dg_ckv[0:1]
    segs = ((dgates, 0, 3 * D), (dz_na, NA, 1536), (dz_gla, GL, 1536), (dz_cq, ML, MLA_RANK), (dz_ckv, ML + MLA_RANK, MLA_RANK),
            (dz_lr, LR, 128), (dz_kr, KR, 128))
    dh, got = _mm([(dz, _V(w["w_in"], c0, wd)) for dz, c0, wd in segs], "nt", F32, "w_in_dx", tm=512, tn=512,
                  ride=[own("ffn2_w2")])
    recv.update(got)
    g["w_in"] = jnp.concatenate(
        [_mm([(sv["h"], dz)], "tn", F32, f"w_in_dw{i}", tm=D, tn=_tile(wd, 256)) for i, (dz, _, wd) in enumerate(segs)], axis=1)
    dx1, dg_mix = _rms_bwd(sv["x1"], w["mix_norm"], dh, "mix_rms_bwd", dres=dx2)
    g["mix_norm"] = dg_mix[0:1]
    ride = [_Ride(n, _split4(g[n], SHARDED[n] - 1), lambda chip: (chip,), recv[n], at_layer)
            for n in ("w_out", "w_br_na", "w_br_gla", "w_br_mla")]
    dx, g["ffn1_norm"], (g["ffn1_w1"], g["ffn1_w3"], g["ffn1_w2"]), got = _ffn_bwd(
        dx1, sv["x"], w["ffn1_norm"], w["ffn1_w1"], w["ffn1_w3"], w["ffn1_w2"], sv["f1"], "ffn1",
        (bufs["ffn1_w1"], bufs["ffn1_w3"], bufs["ffn1_w2"]), layer, ride_down=ride)
    recv.update(got)
    late = dict(w_in=_split4(_win_unlayout(g["w_in"], D), 1), mla_w_uq=_split4(_uq_unlayout(g["mla_wuq"]), 1),
                mla_w_ukv=_split4(g["mla_w_ukv"], 1), gla_gf_up=_split4(g["gla_wg"][0:GLA_RANK, 0:256], 1),
                gla_gb_up=_split4(g["gla_wg"][GLA_RANK:2 * GLA_RANK, 256:512], 1))
    return dx, g, late, recv


def _head_fold(width, period, lo=0):
    f = np.zeros((width, 128), np.float32)
    f[np.arange(width), lo + np.arange(width) % period] = 1.0
    return f


def kernel(x, ffn1_norm, ffn1_w1, ffn1_w3, ffn1_w2, mix_norm, w_in, na_q_norm, na_k_norm, na_rpb, gla_gf_up, gla_gf_bias,
           gla_gb_up, gla_gb_bias, gla_out_norm, mla_cq_norm, mla_ckv_norm, mla_w_uq, mla_w_ukv, mla_q_norm, mla_k_norm,
           w_br_na, w_br_gla, w_br_mla, w_out, ffn2_norm, ffn2_w1, ffn2_w3, ffn2_w2, loss_target, m_ffn1_norm, m_ffn1_w1,
           m_ffn1_w3, m_ffn1_w2, m_mix_norm, m_w_in, m_na_q_norm, m_na_k_norm, m_na_rpb, m_gla_gf_up, m_gla_gf_bias,
           m_gla_gb_up, m_gla_gb_bias, m_gla_out_norm, m_mla_cq_norm, m_mla_ckv_norm, m_mla_w_uq, m_mla_w_ukv,
           m_mla_q_norm, m_mla_k_norm, m_w_br_na, m_w_br_gla, m_w_br_mla, m_w_out, m_ffn2_norm, m_ffn2_w1, m_ffn2_w3,
           m_ffn2_w2, v_ffn1_norm, v_ffn1_w1, v_ffn1_w3, v_ffn1_w2, v_mix_norm, v_w_in, v_na_q_norm, v_na_k_norm,
           v_na_rpb, v_gla_gf_up, v_gla_gf_bias, v_gla_gb_up, v_gla_gb_bias, v_gla_out_norm, v_mla_cq_norm,
           v_mla_ckv_norm, v_mla_w_uq, v_mla_w_ukv, v_mla_q_norm, v_mla_k_norm, v_w_br_na, v_w_br_gla, v_w_br_mla,
           v_w_out, v_ffn2_norm, v_ffn2_w1, v_ffn2_w3, v_ffn2_w2):
    given = dict(locals())
    wts = {n: given[n] for n in W_NAMES}
    mom = {n: given["m_" + n] for n in W_NAMES}
    var = {n: given["v_" + n] for n in W_NAMES}
    xs, target = x[0], loss_target[0]
    S, D = xs.shape
    L = ffn1_norm.shape[0]

    sh_names = tuple(SHARDED)
    LATE = ("ffn2_w1", "ffn2_w3", "ffn2_w2")
    HEAVY = ("ffn1_w1", "ffn1_w3", "ffn1_w2", "w_in")
    LIGHT = tuple(n for n in sh_names if n not in LATE + HEAVY)
    shard_shape = lambda n: tuple(wts[n].shape[1:])

    def gather_items(names, l):
        return [_Ride(n, wts[n][l].astype(BF16), lambda chip: (), lax.empty((4,) + shard_shape(n), BF16), lambda chip: (chip,),
                      halves=n in HEAVY + LATE) for n in names]

    cols = lambda p: jnp.concatenate([p[j] for j in range(4)], axis=-1)

    def layer_weights(gl, l):
        r1 = lambda a: a[l][None]
        wg = jnp.zeros((128, 2 * GLA_HEADS * GLA_DK), BF16)
        wg = wg.at[0:GLA_RANK, 0:256].set(cols(gl["gla_gf_up"])).at[GLA_RANK:2 * GLA_RANK, 256:512].set(cols(gl["gla_gb_up"]))
        return dict(
            ffn1_norm=r1(ffn1_norm), ffn1_w1=gl["ffn1_w1"], ffn1_w3=gl["ffn1_w3"], ffn1_w2=gl["ffn1_w2"],
            mix_norm=r1(mix_norm), w_in=_win_layout(cols(gl["w_in"]), D),
            na_gq=jnp.tile(na_q_norm[l], NA_HEADS)[None], na_gk=jnp.tile(na_k_norm[l], NA_HEADS)[None], na_rpb=na_rpb[l],
            gla_wg=wg, gla_gbias=jnp.concatenate([gla_gf_bias[l], gla_gb_bias[l]])[None], gla_out_norm=r1(gla_out_norm),
            mla_cq_norm=r1(mla_cq_norm), mla_ckv_norm=r1(mla_ckv_norm), mla_wuq=_uq_layout(cols(gl["mla_w_uq"])),
            mla_w_ukv=cols(gl["mla_w_ukv"]), mla_gq=_slot_layout(mla_q_norm[l])[None], mla_gk=_slot_layout(mla_k_norm[l])[None],
            w_br_na=cols(gl["w_br_na"]), w_br_gla=cols(gl["w_br_gla"]), w_br_mla=cols(gl["w_br_mla"]),
            w_out=gl["w_out"].reshape(D, D), ffn2_norm=r1(ffn2_norm))

    half = MLA_ROPE // 2
    inv = ROPE_THETA ** (-jnp.arange(half, dtype=F32) / half)
    ang = jnp.arange(S, dtype=F32)[:, None] * inv[None, :]
    cos = jnp.tile(jnp.cos(ang), (1, 4))
    sg = jnp.concatenate([-jnp.sin(ang), -jnp.sin(ang), jnp.sin(ang), jnp.sin(ang)], axis=1)

    arrived = _exchange(gather_items(HEAVY + LIGHT, 0), "weights_all_gather")
    xc, saved, layers = xs, [], []
    for l in range(L):
        w = layer_weights(arrived, l)
        rides = {"ffn1_up": gather_items(("ffn2_w1", "ffn2_w3"), l), "w_in": gather_items(("ffn2_w2",), l)}
        if l + 1 < L:
            rides["w_in"] += gather_items(LIGHT, l + 1)
            rides["na_attn"] = gather_items(("ffn1_w1", "ffn1_w3"), l + 1)
            rides["gla_fwd"] = gather_items(("ffn1_w2",), l + 1)
            rides["mla_attn"] = gather_items(("w_in",), l + 1)
        xc, sv, arrived = _layer_fwd(xc, w, cos, sg, rides)
        saved.append(sv)
        layers.append({**w, **{n: arrived[n] for n in LATE}})
    dy, loss_part = _loss_head(xc, target, "loss_head")

    bufs = {n: lax.empty((4, L) + shard_shape(n), BF16) for n in FFN_W}
    recv = {n: lax.empty((4, L) + shard_shape(n), BF16) for n in sh_names}
    dx, g, prev = dy, [None] * L, {}
    for l in reversed(range(L)):
        dx, g[l], late, recv = _layer_bwd(dx, layers[l], saved[l], cos, sg, bufs, recv, l, prev)
        bufs = {n: g[l][n] for n in FFN_W}
        at_l = functools.partial(lambda chip, l: (chip, l), l=l)
        prev = {n: _Ride(n, bufs[n], at_l, recv[n], at_l) for n in ("ffn1_w1", "ffn1_w3", "ffn1_w2")}
        prev.update({n: _Ride(n, late[n], lambda chip: (chip,), recv[n], at_l) for n in late})
    recv = {**recv, **_exchange(list(prev.values()), "grads_chip_exchange")}

    stk = lambda n: jnp.stack([g[l][n] for l in range(L)])
    gs = {n: stk(n)[:, 0] for n in ("ffn1_norm", "mix_norm", "mla_cq_norm", "mla_ckv_norm", "ffn2_norm")}
    gs["na_q_norm"] = _fold(stk("na_gq"), "na_gq_fold", _head_fold(NA_W, NA_DH))[:, :NA_DH]
    gs["na_k_norm"] = _fold(stk("na_gk"), "na_gk_fold", _head_fold(NA_W, NA_DH))[:, :NA_DH]
    gs["na_rpb"] = stk("na_rpb")
    gbias = _fold(stk("gla_gbias"), "gla_gbias_fold")
    gs["gla_gf_bias"], gs["gla_gb_bias"] = gbias[:, :256], gbias[:, 256:]
    gs["gla_out_norm"] = _fold(stk("gla_out_norm"), "gla_out_norm_fold")
    gs["mla_q_norm"] = _slot_unlayout(_fold(stk("mla_gq"), "mla_gq_fold"))
    gs["mla_k_norm"] = _slot_unlayout(_fold(stk("mla_gk"), "mla_gk_fold"))

    mine = [_sum4(recv[n].reshape(4, -1, recv[n].shape[-1]), "grads_chip_sum_" + n) for n in sh_names]
    other = _sibling_exchange(mine, "grads_sibling_exchange")
    gsh = {}

    small_shapes = [wts[n].shape[1:] for n in REPLICATED]
    n_small = sum(int(np.prod(s)) for s in small_shapes) * L
    flat = jnp.concatenate([gs[n].reshape(-1) for n in REPLICATED] + [loss_part.reshape(-1)])
    pad = -flat.shape[0] % 1024
    red = _allreduce_small(jnp.pad(flat, (0, pad)).reshape(-1, 128), "small_all_reduce").reshape(-1)
    loss = jnp.sum(red[n_small:n_small + 1024])
    off = 0
    for n, s in zip(REPLICATED, small_shapes):
        cnt = int(np.prod(s)) * L
        gsh[n] = red[off:off + cnt].reshape((L,) + tuple(s))
        off += cnt

    as2d = lambda a: a.reshape(-1, a.shape[-1])
    upd = {}
    for n, p, q in zip(sh_names, mine, other):
        outs = [o.reshape(wts[n].shape) for o in _adamw(as2d(wts[n]), p, as2d(mom[n]), as2d(var[n]), "adamw_" + n, q=q)]
        gsh[n], upd[n] = outs[0], outs[1:]
    pk = lambda d: jnp.pad(jnp.concatenate([d[n].reshape(-1) for n in REPLICATED]), (0, -n_small % 1024)).reshape(-1, 128)
    small = _adamw(pk(wts), pk(gsh), pk(mom), pk(var), "adamw_replicated")
    off = 0
    for n, s in zip(REPLICATED, small_shapes):
        cnt = int(np.prod(s)) * L
        upd[n] = [o.reshape(-1)[off:off + cnt].reshape((L,) + tuple(s)) for o in small]
        off += cnt

    return (loss, dx[None], *[gsh[n] for n in W_NAMES], *[upd[n][0] for n in W_NAMES], *[upd[n][1] for n in W_NAMES],
            *[upd[n][2] for n in W_NAMES])
```

```python
import functools
import math

import numpy as np
import jax
import jax.numpy as jnp
from jax import lax
from jax.experimental import pallas as pl
from jax.experimental.pallas import tpu as pltpu

F32 = jnp.float32
BF16 = jnp.bfloat16
HI = lax.Precision.HIGHEST
MESH = pl.DeviceIdType.MESH

EPS = 1e-6
GRID_W = 64
NA_HEADS, NA_DH, NA_WIN_R, NA_WIN_C = 8, 64, 8, 16
NA_W = NA_HEADS * NA_DH
GLA_HEADS, GLA_DK, GLA_DV, GLA_RANK, GLA_TAU, GLA_CHUNK = 4, 64, 128, 16, 16.0, 64
MLA_HEADS, MLA_RANK, MLA_NOPE, MLA_ROPE, MLA_V = 4, 256, 128, 64, 128
MLA_QK = MLA_NOPE + MLA_ROPE
MLA_SLOT = 256
MLA_QSCALE = MLA_QK ** -0.5 * math.log2(math.e)
ROPE_THETA = 10000.0
ADAM_LR, ADAM_B1, ADAM_B2, ADAM_EPS, ADAM_WD, ADAM_STEP = 0.001, 0.9, 0.999, 1e-08, 0.01, 10

V7X_VMEM_BYTES = 64 * 2**20
VMEM_LIMIT = V7X_VMEM_BYTES - 12 * 2**20
NEG = -1e30

O_GQ, O_GFL, O_CQ, O_KR, O_GATES = 1536, 3072, 3104, 3616, 3680


_ANY = pl.BlockSpec(memory_space=pl.ANY)


def _cparams(*sem):
    return pltpu.CompilerParams(dimension_semantics=sem, vmem_limit_bytes=VMEM_LIMIT)


class _V:
    def __init__(self, arr, c0=0, w=None, lead=()):
        self.arr, self.c0, self.lead = arr, c0, tuple(lead)
        assert arr.ndim == 2 + len(self.lead), (arr.shape, lead)
        self.w = arr.shape[-1] if w is None else w

    @property
    def rows(self):
        return self.arr.shape[-2]

    def spec(self, br, bc, rfn, cfn):
        assert self.c0 % bc == 0 and self.w % bc == 0, (self.c0, self.w, bc)
        off, lead = self.c0 // bc, self.lead

        def index(*g):
            return tuple(g[0] if e == "b" else e for e in lead) + (rfn(*g), off + cfn(*g))

        return pl.BlockSpec((None,) * len(lead) + (br, bc), index)


def _v(x):
    return x if isinstance(x, _V) else _V(x)


_DN = {"nn": (((1,), (0,)), ((), ())), "nt": (((1,), (1,)), ((), ())), "tn": (((0,), (0,)), ((), ()))}


def _dot(a, b, mode="nn", prec=None):
    return lax.dot_general(a, b, _DN[mode], preferred_element_type=F32, precision=prec)


def _tile(n, cap):
    if n <= cap:
        return n
    for t in range(cap - cap % 128, 0, -128):
        if n % t == 0:
            return t
    return n


def _mm(pairs, mode, out_dtype, name, *, tm, tn, res=None, scale=None, batch=1, into=None, ride=None):
    pairs = [(_v(a), _v(b)) for a, b in pairs]
    a0, b0 = pairs[0]
    M = a0.w if mode == "tn" else a0.rows
    N = b0.rows if mode == "nt" else b0.w
    tm, tn = _tile(M, tm), _tile(N, tn)
    assert M % tm == 0 and N % tn == 0, (name, M, N, tm, tn)
    n = len(pairs)

    def body(*refs):
        o_ref = refs[-1]
        acc = None
        for i in range(n):
            d = _dot(refs[2 * i][...].astype(BF16), refs[2 * i + 1][...].astype(BF16), mode)
            acc = d if acc is None else acc + d
        if scale is not None:
            acc = acc * scale
        if res is not None:
            acc = acc + refs[2 * n][...]
        o_ref[...] = acc.astype(o_ref.dtype)

    zero = lambda b, i, j: 0
    row = lambda b, i, j: i
    col = lambda b, i, j: j
    in_specs, args = [], []
    for a, b in pairs:
        in_specs.append(a.spec(a.rows, tm, zero, row) if mode == "tn" else a.spec(tm, a.w, row, zero))
        in_specs.append(b.spec(tn, b.w, col, zero) if mode == "nt" else b.spec(b.rows, tn, zero, col))
        args += [a.arr, b.arr]
    if res is not None:
        in_specs.append(pl.BlockSpec((tm, tn), lambda b, i, j: (i, j)))
        args.append(res)
    aliases = {}
    if into is None:
        out = jax.ShapeDtypeStruct(((batch,) if batch > 1 else ()) + (M, N), out_dtype)
        out_view = _V(out, lead=("b",) if batch > 1 else ())
    else:
        buf, lead = into
        assert buf.shape[-2:] == (M, N) and buf.dtype == out_dtype, (name, buf.shape, M, N)
        out = jax.ShapeDtypeStruct(buf.shape, buf.dtype)
        out_view = _V(out, lead=lead)
        aliases = {len(args): 0}
        in_specs.append(_ANY)
        args.append(buf)
    (res,), got = _ride_call(
        body, ride, name=name, grid=(batch, M // tm, N // tn), in_specs=in_specs, out_specs=[out_view.spec(tm, tn, row, col)],
        out_shape=[out], aliases=aliases, args=args, semantics=("parallel", "parallel", "parallel"))
    return res if ride is None else (res, got)


def _rms_fwd(x, g, name, tm=512):
    x = _v(x)
    S, D = x.rows, x.w
    tm = min(tm, S)

    def body(x_ref, g_ref, o_ref):
        xv = x_ref[...]
        y = xv * lax.rsqrt(jnp.mean(xv * xv, axis=-1, keepdims=True) + EPS)
        o_ref[...] = (y * g_ref[...]).astype(o_ref.dtype)

    return pl.pallas_call(
        body, name=name, grid=(S // tm,),
        in_specs=[x.spec(tm, D, lambda i: i, lambda i: 0), pl.BlockSpec((1, D), lambda i: (0, 0))],
        out_specs=pl.BlockSpec((tm, D), lambda i: (i, 0)),
        out_shape=jax.ShapeDtypeStruct((S, D), BF16), compiler_params=_cparams("parallel"),
    )(x.arr, g)


def _rms_bwd(x, g, dh, name, dres=None, out_dtype=F32, tm=512):
    x = _v(x)
    S, D = x.rows, x.w
    tm = min(tm, S)

    def body(*refs):
        if dres is None:
            x_ref, g_ref, dh_ref, dx_ref, dg_ref = refs
        else:
            x_ref, g_ref, dh_ref, dr_ref, dx_ref, dg_ref = refs
        xv = x_ref[...]
        rstd = lax.rsqrt(jnp.mean(xv * xv, axis=-1, keepdims=True) + EPS)
        xhat = xv * rstd
        dhv = dh_ref[...].astype(F32)
        dxhat = dhv * g_ref[...]
        dx = rstd * (dxhat - xhat * jnp.mean(dxhat * xhat, axis=-1, keepdims=True))
        if dres is not None:
            dx = dx + dr_ref[...]
        dx_ref[...] = dx.astype(dx_ref.dtype)

        @pl.when(pl.program_id(0) == 0)
        def _():
            dg_ref[...] = jnp.zeros_like(dg_ref)

        dg_ref[0:1, :] += jnp.sum(dhv * xhat, axis=0, keepdims=True)

    in_specs = [x.spec(tm, D, lambda i: i, lambda i: 0), pl.BlockSpec((1, D), lambda i: (0, 0)),
                pl.BlockSpec((tm, D), lambda i: (i, 0))]
    args = [x.arr, g, dh]
    if dres is not None:
        in_specs.append(pl.BlockSpec((tm, D), lambda i: (i, 0)))
        args.append(dres)
    return pl.pallas_call(
        body, name=name, grid=(S // tm,), in_specs=in_specs,
        out_specs=[pl.BlockSpec((tm, D), lambda i: (i, 0)), pl.BlockSpec((8, D), lambda i: (0, 0))],
        out_shape=[jax.ShapeDtypeStruct((S, D), out_dtype), jax.ShapeDtypeStruct((8, D), F32)],
        compiler_params=_cparams("arbitrary"),
    )(*args)


FFN_SUB = 256


def _one_ahead(n, matmuls, rest):
    res = matmuls(0)
    for i in range(1, n):
        nxt = matmuls(i)
        rest(i - 1, res)
        res = nxt
    rest(n - 1, res)


def _ffn_up(h, w1, w3, name, tm=1024, ride=None):
    S, D = h.shape
    NC, _, F4 = w1.shape
    tm = min(tm, S)
    sub = math.gcd(FFN_SUB, tm)
    rows = lambda i: slice(i * sub, (i + 1) * sub)

    def body(h_ref, w1_ref, w3_ref, a_ref, b_ref, u_ref):
        def matmuls(i):
            hv = h_ref[rows(i), :]
            return _dot(hv, w1_ref[...]), _dot(hv, w3_ref[...])

        def rest(i, ab):
            a, b = ab
            a_ref[rows(i), :] = a.astype(BF16)
            b_ref[rows(i), :] = b.astype(BF16)
            u_ref[rows(i), :] = (a * jax.nn.sigmoid(a) * b).astype(BF16)

        _one_ahead(tm // sub, matmuls, rest)

    blk = pl.BlockSpec((None, tm, F4), lambda i, j: (j, i, 0))
    wblk = pl.BlockSpec((None, D, F4), lambda i, j: (j, 0, 0))
    return _ride_call(
        body, ride, name=name, grid=(S // tm, NC), in_specs=[pl.BlockSpec((tm, D), lambda i, j: (i, 0)), wblk, wblk],
        out_specs=[blk, blk, blk], out_shape=[jax.ShapeDtypeStruct((NC, S, F4), BF16)] * 3, args=(h, w1, w3),
        semantics=("parallel", "parallel"))


def _ffn_down_bwd(dxo, w2, a, b, name, tm=1024, ride=None):
    S, D = dxo.shape
    NC, F4, _ = w2.shape
    tm = min(tm, S)
    sub = math.gcd(FFN_SUB, tm)
    rows = lambda i: slice(i * sub, (i + 1) * sub)

    def body(dx_ref, w2_ref, a_ref, b_ref, da_ref, db_ref):
        def matmuls(i):
            return _dot(dx_ref[rows(i), :].astype(BF16), w2_ref[...], "nt")

        def rest(i, du):
            du = du * 0.5
            av = a_ref[rows(i), :].astype(F32)
            sig = jax.nn.sigmoid(av)
            da_ref[rows(i), :] = (du * b_ref[rows(i), :].astype(F32) * (sig * (1.0 + av * (1.0 - sig)))).astype(BF16)
            db_ref[rows(i), :] = (du * av * sig).astype(BF16)

        _one_ahead(tm // sub, matmuls, rest)

    blk = pl.BlockSpec((None, tm, F4), lambda i, j: (j, i, 0))
    return _ride_call(
        body, ride, name=name, grid=(S // tm, NC),
        in_specs=[pl.BlockSpec((tm, D), lambda i, j: (i, 0)), pl.BlockSpec((None, F4, D), lambda i, j: (j, 0, 0)), blk, blk],
        out_specs=[blk, blk], out_shape=[jax.ShapeDtypeStruct((NC, S, F4), BF16)] * 2, args=(dxo, w2, a, b),
        semantics=("parallel", "parallel"))


def _ffn_fwd(x, g, w1, w3, w2, tag, ride=None):
    h = _rms_fwd(x, g, f"{tag}_rms")
    (a, b, u), got = _ffn_up(h, w1, w3, f"{tag}_up", ride=ride)
    nc = w2.shape[0]
    y = _mm([(_V(u, lead=(j,)), _V(w2, lead=(j,))) for j in range(nc)], "nn", F32, f"{tag}_down", tm=512, tn=1024, res=x, scale=0.5)
    return y, (h, a, b, u), got


def _ffn_bwd(dxo, x, g, w1, w3, w2, saved, tag, bufs, layer, ride_down=None, ride_dh=None):
    h, a, b, u = saved
    nc, D, F4 = w1.shape
    (da, db), got = _ffn_down_bwd(dxo, w2, a, b, f"{tag}_down_bwd", ride=ride_down)
    into = lambda k: (bufs[k], ("b", layer))
    dw2 = _mm([(_V(u, lead=("b",)), dxo)], "tn", BF16, f"{tag}_dw2", tm=F4, tn=512, scale=0.5, batch=nc, into=into(2))
    dw1 = _mm([(h, _V(da, lead=("b",)))], "tn", BF16, f"{tag}_dw1", tm=D, tn=F4, batch=nc, into=into(0))
    dw3 = _mm([(h, _V(db, lead=("b",)))], "tn", BF16, f"{tag}_dw3", tm=D, tn=F4, batch=nc, into=into(1))
    pairs = [(_V(da, lead=(j,)), _V(w1, lead=(j,))) for j in range(nc)] + [(_V(db, lead=(j,)), _V(w3, lead=(j,))) for j in range(nc)]
    dh, got_dh = _mm(pairs, "nt", F32, f"{tag}_dh", tm=512, tn=512, ride=ride_dh or [])
    dx, dg = _rms_bwd(x, g, dh, f"{tag}_rms_bwd", dres=dxo)
    return dx, dg[0:1], (dw1, dw3, dw2), {**got, **got_dh}


def _iota(shape, dim):
    return lax.broadcasted_iota(jnp.int32, shape, dim)


def _head_block_ones(n, shift):
    return jnp.where((_iota((n, n), 0) >> shift) == (_iota((n, n), 1) >> shift), 1.0, 0.0).astype(F32)


def _lane_mask(width, lo, size):
    l = _iota((1, width), 1)
    return jnp.where((l >= lo) & (l < lo + size), 1.0, 0.0).astype(F32)


def _acc_rows(acc_ref, val, first):
    r = val.shape[0]
    part = jnp.sum(val.reshape(r // 8, 8, val.shape[1]), axis=0)

    @pl.when(first)
    def _():
        acc_ref[...] = part

    @pl.when(jnp.logical_not(first))
    def _():
        acc_ref[...] += part


_FLIPS = ((1, 0), (0, 1), (1, 1))


class _Ride:
    def __init__(self, name, src, src_at, dst, dst_at, halves=False):
        self.name, self.src, self.src_at, self.dst, self.dst_at, self.halves = name, src, src_at, dst, dst_at, halves
        assert not halves or (src.ndim == 2 and src.shape[0] % 32 == 0), (name, src.shape)


_RIDE_SEMS = lambda n: [pltpu.SemaphoreType.DMA((6, n)), pltpu.SemaphoreType.DMA((6, n)), pltpu.SemaphoreType.DMA((n,))]


def _ride_ops(ride, srcs, dsts, send_sems, recv_sems, local_sems):
    x, y, c = lax.axis_index("x"), lax.axis_index("y"), lax.axis_index("c")
    me = 2 * x + y
    at = lambda ref, idx: ref.at[idx] if idx else ref
    local, sends, arrivals, passes = [], [], [], []
    for t, it in enumerate(ride):
        local.append(pltpu.make_async_copy(at(srcs[t], it.src_at(me)), at(dsts[t], it.dst_at(me)), local_sems.at[t]))
    for r, (fx, fy) in enumerate(_FLIPS):
        px, py = (1 - x) if fx else x, (1 - y) if fy else y
        peer = 2 * px + py
        for t, it in enumerate(ride):
            if it.halves:
                h = it.src.shape[0] // 2
                mine = pl.ds(pl.multiple_of(c * h, 16), h)
                theirs = pl.ds(pl.multiple_of((1 - c) * h, 16), h)
                far = dict(send_sem=send_sems.at[r, t], recv_sem=recv_sems.at[r, t], device_id=(px, py, c), device_id_type=MESH)
                near = dict(send_sem=send_sems.at[3 + r, t], recv_sem=recv_sems.at[3 + r, t], device_id=(x, y, 1 - c),
                            device_id_type=MESH)
                sends.append(pltpu.make_async_remote_copy(src_ref=srcs[t].at[mine], dst_ref=dsts[t].at[me, mine], **far))
                arrivals.append(pltpu.make_async_remote_copy(src_ref=srcs[t].at[mine], dst_ref=dsts[t].at[peer, mine], **far))
                passes.append((pltpu.make_async_remote_copy(src_ref=dsts[t].at[peer, mine], dst_ref=dsts[t].at[peer, mine], **near),
                               pltpu.make_async_remote_copy(src_ref=dsts[t].at[peer, theirs], dst_ref=dsts[t].at[peer, theirs], **near)))
            else:
                far = dict(src_ref=at(srcs[t], it.src_at(peer)), send_sem=send_sems.at[r, t], recv_sem=recv_sems.at[r, t],
                           device_id=(px, py, c), device_id_type=MESH)
                sends.append(pltpu.make_async_remote_copy(dst_ref=at(dsts[t], it.dst_at(me)), **far))
                arrivals.append(pltpu.make_async_remote_copy(dst_ref=at(dsts[t], it.dst_at(peer)), **far))
                passes.append(None)

    def start():
        for cp in local + sends:
            cp.start()

    def finish():
        for cp, arrival, onward in zip(sends, arrivals, passes):
            cp.wait_send()
            arrival.wait_recv()
            if onward is not None:
                onward[0].start()
        for onward in passes:
            if onward is not None:
                onward[0].wait_send()
                onward[1].wait_recv()
        for cp in local:
            cp.wait()

    return start, finish


def _grid_edges(*ns):
    def edges():
        first = last = None
        for d, n in enumerate(ns):
            i = pl.program_id(d)
            f, l = i == 0, i == n - 1
            first = f if first is None else jnp.logical_and(first, f)
            last = l if last is None else jnp.logical_and(last, l)
        return first, last
    return edges


def _ride_call(body, ride, *, name, grid, in_specs, out_specs, out_shape, args, scratch_shapes=(), semantics=(), aliases=None):
    scratch_shapes, aliases = list(scratch_shapes), dict(aliases or {})
    if not ride:
        outs = pl.pallas_call(body, name=name, grid=grid, in_specs=in_specs, out_specs=out_specs, out_shape=out_shape,
                              scratch_shapes=scratch_shapes, input_output_aliases=aliases,
                              compiler_params=_cparams(*semantics))(*args)
        return outs, {}
    n_in, n_out, n_sc, n = len(in_specs), len(out_specs), len(scratch_shapes), len(ride)
    edges = _grid_edges(*grid)

    def wrapped(*refs):
        ins, srcs = refs[:n_in], refs[n_in:n_in + n]
        o0 = n_in + 2 * n
        outs, dsts = refs[o0:o0 + n_out], refs[o0 + n_out:o0 + n_out + n]
        scratch = refs[o0 + n_out + n:o0 + n_out + n + n_sc]
        start, finish = _ride_ops(ride, srcs, dsts, *refs[o0 + n_out + n + n_sc:])
        first, last = edges()
        pl.when(first)(start)
        body(*ins, *outs, *scratch)
        pl.when(last)(finish)

    aliases.update({n_in + n + t: n_out + t for t in range(n)})
    res = pl.pallas_call(
        wrapped, name=name, grid=grid, in_specs=list(in_specs) + [_ANY] * (2 * n), out_specs=list(out_specs) + [_ANY] * n,
        out_shape=list(out_shape) + [jax.ShapeDtypeStruct(it.dst.shape, it.dst.dtype) for it in ride],
        input_output_aliases=aliases, scratch_shapes=scratch_shapes + _RIDE_SEMS(n),
        compiler_params=_cparams(*(["arbitrary"] * len(grid))),
    )(*args, *[it.src for it in ride], *[it.dst for it in ride])
    return res[:n_out], {it.name: o for it, o in zip(ride, res[n_out:])}


def _exchange(ride, name):
    n = len(ride)

    def body(*refs):
        start, finish = _ride_ops(ride, refs[:n], refs[2 * n:3 * n], *refs[3 * n:])
        start()
        finish()

    res = pl.pallas_call(
        body, name=name, in_specs=[_ANY] * (2 * n), out_specs=[_ANY] * n,
        out_shape=[jax.ShapeDtypeStruct(it.dst.shape, it.dst.dtype) for it in ride],
        input_output_aliases={n + t: t for t in range(n)}, scratch_shapes=_RIDE_SEMS(n),
    )(*[it.src for it in ride], *[it.dst for it in ride])
    return {it.name: o for it, o in zip(ride, res)}


def _na_prep(z, c0, gq, gk, name, tm=512):
    S = z.shape[0]
    tm = min(tm, S)
    zv = _V(z, c0, 3 * NA_W)

    def body(z_ref, gq_ref, gk_ref, q_ref, k_ref, v_ref):
        bd = _head_block_ones(NA_W, 6)

        def norm(xv, gv):
            ms = _dot(xv * xv, bd, prec=HI) * (1.0 / NA_DH)
            return xv * lax.rsqrt(ms + EPS) * gv

        q_ref[...] = (norm(z_ref[:, 0:NA_W], gq_ref[...]) * (NA_DH ** -0.5)).astype(BF16)
        k_ref[...] = norm(z_ref[:, NA_W:2 * NA_W], gk_ref[...]).astype(BF16)
        v_ref[...] = z_ref[:, 2 * NA_W:3 * NA_W].astype(BF16)

    blk = pl.BlockSpec((tm, NA_W), lambda i: (i, 0))
    gspec = pl.BlockSpec((1, NA_W), lambda i: (0, 0))
    return pl.pallas_call(
        body, name=name, grid=(S // tm,),
        in_specs=[zv.spec(tm, 3 * NA_W, lambda i: i, lambda i: 0), gspec, gspec],
        out_specs=[blk, blk, blk], out_shape=[jax.ShapeDtypeStruct((S, NA_W), BF16)] * 3,
        compiler_params=_cparams("parallel"),
    )(z, gq, gk)


def _na_prep_bwd(z, c0, gq, gk, dqn, dkn, dv, name, tm=512):
    S = z.shape[0]
    tm = min(tm, S)
    zv = _V(z, c0, 3 * NA_W)

    def body(z_ref, gq_ref, gk_ref, dq_ref, dk_ref, dv_ref, dz_ref, dgq_ref, dgk_ref):
        bd = _head_block_ones(NA_W, 6)
        first = pl.program_id(0) == 0

        def norm_bwd(xv, gv, dy, dg_ref):
            ms = _dot(xv * xv, bd, prec=HI) * (1.0 / NA_DH)
            rstd = lax.rsqrt(ms + EPS)
            xhat = xv * rstd
            dxhat = dy * gv
            proj = _dot(dxhat * xhat, bd, prec=HI) * (1.0 / NA_DH)
            _acc_rows(dg_ref, dy * xhat, first)
            return rstd * (dxhat - xhat * proj)

        dz_ref[:, 0:NA_W] = norm_bwd(z_ref[:, 0:NA_W], gq_ref[...], dq_ref[...] * (NA_DH ** -0.5), dgq_ref).astype(BF16)
        dz_ref[:, NA_W:2 * NA_W] = norm_bwd(z_ref[:, NA_W:2 * NA_W], gk_ref[...], dk_ref[...], dgk_ref).astype(BF16)
        dz_ref[:, 2 * NA_W:3 * NA_W] = dv_ref[...].astype(BF16)

    blk = pl.BlockSpec((tm, NA_W), lambda i: (i, 0))
    gspec = pl.BlockSpec((1, NA_W), lambda i: (0, 0))
    acc = pl.BlockSpec((8, NA_W), lambda i: (0, 0))
    return pl.pallas_call(
        body, name=name, grid=(S // tm,),
        in_specs=[zv.spec(tm, 3 * NA_W, lambda i: i, lambda i: 0), gspec, gspec, blk, blk, blk],
        out_specs=[pl.BlockSpec((tm, 3 * NA_W), lambda i: (i, 0)), acc, acc],
        out_shape=[jax.ShapeDtypeStruct((S, 3 * NA_W), BF16), jax.ShapeDtypeStruct((8, NA_W), F32),
                   jax.ShapeDtypeStruct((8, NA_W), F32)],
        compiler_params=_cparams("arbitrary"),
    )(z, gq, gk, dqn, dkn, dv)


def _na_onehot():
    qc = np.arange(GRID_W)[:, None]
    kc = np.arange(GRID_W)[None, :]
    c0 = np.clip(qc - NA_WIN_C // 2, 0, GRID_W - NA_WIN_C)
    valid = (kc >= c0) & (kc < c0 + NA_WIN_C)
    dc = kc - qc + (NA_WIN_C - 1)
    e = np.zeros((32, GRID_W, GRID_W), np.float32)
    for d in range(2 * NA_WIN_C - 1):
        e[d] = valid & (dc == d)
    return e.reshape(32, GRID_W * GRID_W), valid.reshape(1, -1)


def _rpb_expand(rpb, name):
    e, valid = _na_onehot()
    negmask = np.where(valid, 0.0, NEG).astype(np.float32)
    nd = 2 * NA_WIN_R - 1
    r2 = jnp.pad(rpb.reshape(NA_HEADS * nd, 2 * NA_WIN_C - 1), ((0, 128 - NA_HEADS * nd), (0, 1)))

    def body(r_ref, e_ref, m_ref, o_ref):
        o_ref[...] = _dot(r_ref[...], e_ref[...], prec=HI) + m_ref[...]

    t = pl.pallas_call(body, name=name, out_shape=jax.ShapeDtypeStruct((128, GRID_W * GRID_W), F32))(
        r2, jnp.asarray(e), jnp.asarray(negmask))
    t = t[:NA_HEADS * nd].reshape(NA_HEADS, nd, GRID_W, GRID_W)
    return jnp.stack([jnp.concatenate([t[:, b + w] for w in range(NA_WIN_R)], axis=-1) for b in range(NA_WIN_R)], axis=1)


def _rpb_reduce(dbias, name):
    e, _ = _na_onehot()
    nd = 2 * NA_WIN_R - 1
    et = np.zeros((GRID_W * GRID_W, 128), np.float32)
    et[:, :32] = e.T
    sel = np.zeros((128, NA_HEADS * NA_WIN_R * NA_WIN_R), np.float32)
    for h in range(NA_HEADS):
        for b in range(NA_WIN_R):
            for w in range(NA_WIN_R):
                sel[h * nd + b + w, (h * NA_WIN_R + b) * NA_WIN_R + w] = 1.0
    x = dbias.reshape(NA_HEADS, NA_WIN_R, GRID_W, NA_WIN_R, GRID_W).transpose(0, 1, 3, 2, 4).reshape(-1, GRID_W * GRID_W)

    def body(x_ref, et_ref, sel_ref, o_ref):
        g = _dot(x_ref[...], et_ref[...], prec=HI)
        o_ref[...] = _dot(sel_ref[...], g, prec=HI)

    out = pl.pallas_call(body, name=name, out_shape=jax.ShapeDtypeStruct((128, 128), F32))(x, jnp.asarray(et), jnp.asarray(sel))
    return out[:NA_HEADS * nd, :2 * NA_WIN_C - 1].reshape(NA_HEADS, nd, 2 * NA_WIN_C - 1)


def _na_base(r, rows):
    return jnp.clip(r - NA_WIN_R // 2, 0, rows - NA_WIN_R) - r + (NA_WIN_R - 1)


def _na_probs(q_ref, k_ref, bias_ref, P, r0w):
    sl = [slice(128 * pp, 128 * pp + 128) for pp in range(P)]
    m = [_lane_mask(128, 64 * hh, 64) for hh in range(2)]
    kw = [k_ref[r0w, sl[pp]] for pp in range(P)]
    units = [(pp, hh) for pp in range(P) for hh in range(2)]
    qm = {u: (q_ref[:, sl[u[0]]].astype(F32) * m[u[1]]).astype(BF16) for u in units}
    s = {u: _dot(qm[u], kw[u[0]], "nt") + bias_ref[2 * u[0] + u[1], 0] for u in units}
    p = {}
    for u in units:
        e = jnp.exp(s[u] - jnp.max(s[u], axis=-1, keepdims=True))
        p[u] = e / jnp.sum(e, axis=-1, keepdims=True)
    return sl, m, kw, units, qm, p


NA_FWD_PAIRS = 4
NA_BWD_PAIRS = 2


def _na_attn(qn, kn, vb, bias, name, ride=None):
    S = qn.shape[0]
    rows = S // GRID_W
    nk = NA_WIN_R * GRID_W
    P = NA_FWD_PAIRS
    W = 128 * P

    def body(q_ref, k_ref, v_ref, b_ref, o_ref):
        r = pl.program_id(1)
        r0w = pl.ds(pl.multiple_of(jnp.clip(r - NA_WIN_R // 2, 0, rows - NA_WIN_R) * GRID_W, GRID_W), nk)
        sl, m, _, units, _, p = _na_probs(q_ref, k_ref, b_ref, P, r0w)
        o = {u: _dot(p[u].astype(BF16), v_ref[r0w, sl[u[0]]]) for u in units}
        for pp in range(P):
            o_ref[:, sl[pp]] = (o[pp, 0] * m[0] + o[pp, 1] * m[1]).astype(BF16)

    full = pl.BlockSpec((S, W), lambda g, r: (0, g))
    (o,), got = _ride_call(
        body, ride, name=name, grid=(NA_HEADS // (2 * P), rows),
        in_specs=[pl.BlockSpec((GRID_W, W), lambda g, r: (r, g)), full, full,
                  pl.BlockSpec((2 * P, 1, GRID_W, nk), lambda g, r: (g, _na_base(r, rows), 0, 0))],
        out_specs=[pl.BlockSpec((GRID_W, W), lambda g, r: (r, g))],
        out_shape=[jax.ShapeDtypeStruct((S, NA_W), BF16)], args=(qn, kn, vb, bias), semantics=("parallel", "arbitrary"))
    return o, got


def _na_attn_bwd(qn, kn, vb, bias, do, name, ride=None):
    S = qn.shape[0]
    rows = S // GRID_W
    nk = NA_WIN_R * GRID_W
    P = NA_BWD_PAIRS
    W = 128 * P

    def body(q_ref, k_ref, v_ref, b_ref, do_ref, dq_ref, dk_ref, dv_ref, db_ref):
        r = pl.program_id(1)

        @pl.when(r == 0)
        def _():
            dk_ref[...] = jnp.zeros_like(dk_ref)
            dv_ref[...] = jnp.zeros_like(dv_ref)

        r0w = pl.ds(pl.multiple_of(jnp.clip(r - NA_WIN_R // 2, 0, rows - NA_WIN_R) * GRID_W, GRID_W), nk)
        fresh = jnp.logical_or(r <= NA_WIN_R // 2, r > rows - NA_WIN_R // 2)
        sl, m, kw, units, qm, p = _na_probs(q_ref, k_ref, b_ref, P, r0w)
        dom = {u: (do_ref[:, sl[u[0]]].astype(F32) * m[u[1]]).astype(BF16) for u in units}
        dp = {u: _dot(dom[u], v_ref[r0w, sl[u[0]]], "nt") for u in units}
        dvw = {u: _dot(p[u].astype(BF16), dom[u], "tn") for u in units}
        ds = {u: p[u] * (dp[u] - jnp.sum(p[u] * dp[u], axis=-1, keepdims=True)) for u in units}

        @pl.when(fresh)
        def _():
            for u in units:
                db_ref[2 * u[0] + u[1], 0] = ds[u]

        @pl.when(jnp.logical_not(fresh))
        def _():
            for u in units:
                db_ref[2 * u[0] + u[1], 0] += ds[u]

        dsb = {u: ds[u].astype(BF16) for u in units}
        dq = {u: _dot(dsb[u], kw[u[0]]) for u in units}
        dkw = {u: _dot(dsb[u], qm[u], "tn") for u in units}
        for pp in range(P):
            dq_ref[:, sl[pp]] = dq[pp, 0] * m[0] + dq[pp, 1] * m[1]
            dk_ref[r0w, sl[pp]] += dkw[pp, 0] + dkw[pp, 1]
            dv_ref[r0w, sl[pp]] += dvw[pp, 0] + dvw[pp, 1]

    qblk = pl.BlockSpec((GRID_W, W), lambda g, r: (r, g))
    full = pl.BlockSpec((S, W), lambda g, r: (0, g))
    bblk = pl.BlockSpec((2 * P, 1, GRID_W, nk), lambda g, r: (g, _na_base(r, rows), 0, 0))
    return _ride_call(
        body, ride, name=name, grid=(NA_HEADS // (2 * P), rows),
        in_specs=[qblk, full, full, bblk, qblk], out_specs=[qblk, full, full, bblk],
        out_shape=[jax.ShapeDtypeStruct((S, NA_W), F32)] * 3 + [jax.ShapeDtypeStruct((NA_HEADS, NA_WIN_R, GRID_W, nk), F32)],
        args=(qn, kn, vb, bias, do), semantics=("parallel", "arbitrary"))


def _logsig(x):
    return jnp.minimum(x, 0.0) - jnp.log(1.0 + jnp.exp(-jnp.abs(x)))


def _gla_gates(z, c0, wg, bias, name, tm=512):
    S = z.shape[0]
    tm = min(tm, S)
    zv = _V(z, c0, 128)
    W = 2 * GLA_HEADS * GLA_DK

    def body(z_ref, w_ref, b_ref, o_ref):
        pre = _dot(z_ref[...].astype(BF16), w_ref[...]) + b_ref[...]
        o_ref[...] = _logsig(pre) * (1.0 / GLA_TAU)

    return pl.pallas_call(
        body, name=name, grid=(S // tm,),
        in_specs=[zv.spec(tm, 128, lambda i: i, lambda i: 0), pl.BlockSpec((128, W), lambda i: (0, 0)),
                  pl.BlockSpec((1, W), lambda i: (0, 0))],
        out_specs=pl.BlockSpec((tm, W), lambda i: (i, 0)), out_shape=jax.ShapeDtypeStruct((S, W), F32),
        compiler_params=_cparams("parallel"),
    )(z, wg, bias)


def _gla_gates_bwd(z, c0, wg, bias, dg_f, dg_b, name, tm=512):
    S = z.shape[0]
    tm = min(tm, S)
    zv = _V(z, c0, 128)
    W = 2 * GLA_HEADS * GLA_DK

    def body(z_ref, w_ref, b_ref, dgf_ref, dgb_ref, dp_ref, db_ref):
        pre = _dot(z_ref[...].astype(BF16), w_ref[...]) + b_ref[...]
        dg = jnp.concatenate([dgf_ref[...], dgb_ref[...]], axis=-1)
        dpre = dg * (1.0 / GLA_TAU) * jax.nn.sigmoid(-pre)
        dp_ref[...] = dpre.astype(BF16)
        _acc_rows(db_ref, dpre, pl.program_id(0) == 0)

    half = pl.BlockSpec((tm, W // 2), lambda i: (i, 0))
    return pl.pallas_call(
        body, name=name, grid=(S // tm,),
        in_specs=[zv.spec(tm, 128, lambda i: i, lambda i: 0), pl.BlockSpec((128, W), lambda i: (0, 0)),
                  pl.BlockSpec((1, W), lambda i: (0, 0)), half, half],
        out_specs=[pl.BlockSpec((tm, W), lambda i: (i, 0)), pl.BlockSpec((8, W), lambda i: (0, 0))],
        out_shape=[jax.ShapeDtypeStruct((S, W), BF16), jax.ShapeDtypeStruct((8, W), F32)],
        compiler_params=_cparams("arbitrary"),
    )(z, wg, bias, dg_f, dg_b)


def _gla_chunk_terms(zqk, g, p, rev):
    C = GLA_CHUNK
    i, j = _iota((C, C), 0), _iota((C, C), 1)
    cum = jnp.where((j >= i) if rev else (j <= i), 1.0, 0.0).astype(F32)
    q2 = zqk[:, 128 * p:128 * p + 128] * (GLA_DK ** -0.5)
    k2 = zqk[:, 256 + 128 * p:256 + 128 * p + 128]
    b2 = _dot(cum, g[:, 128 * p:128 * p + 128], prec=HI)
    bl2 = b2[0:1] if rev else b2[C - 1:C]
    eb = jnp.exp(b2)
    qe2 = q2 * eb
    ke2 = k2 * jnp.exp(-b2)
    kend2 = k2 * jnp.exp(bl2 - b2)
    dec2 = jnp.exp(bl2)
    tri = (j > i) if rev else (j <= i)
    return b2, bl2, eb, qe2, ke2, kend2, dec2, tri


def _row_to_col(row):
    eye = _iota((128, 128), 0) == _iota((128, 128), 1)
    return jnp.sum(jnp.where(eye, row, 0.0), axis=1, keepdims=True)


def _col_to_row(col):
    eye = _iota((128, 128), 0) == _iota((128, 128), 1)
    return jnp.sum(jnp.where(eye, col, 0.0), axis=0, keepdims=True)


GLA_GROUP = 4


def _gla_fwd(z, c_qk, c_v, gfb, name, ride=None):
    S = z.shape[0]
    C = GLA_CHUNK
    n = S // C
    G = math.gcd(GLA_GROUP, n)
    nb, GC = n // G, G * C
    WQK = 2 * GLA_HEADS * GLA_DK
    WV = GLA_HEADS * GLA_DV
    zqk, zvv = _V(z, c_qk, WQK), _V(z, c_v, WV)

    def body(qkf_ref, vf_ref, gf_ref, qkb_ref, vb_ref, gb_ref, of_ref, ob_ref, sf_ref, sb_ref, stf, stb):
        @pl.when(pl.program_id(0) == 0)
        def _():
            stf[...] = jnp.zeros_like(stf)
            stb[...] = jnp.zeros_like(stb)

        dirs = ((False, qkf_ref, vf_ref, gf_ref, of_ref, sf_ref, stf), (True, qkb_ref, vb_ref, gb_ref, ob_ref, sb_ref, stb))
        rows = lambda gi: slice(gi * C, (gi + 1) * C)
        pairs = [(d, gi, p) for d in range(2) for gi in range(G) for p in range(GLA_HEADS // 2)]
        heads = [(d, gi, p, hh) for d, gi, p in pairs for hh in range(2)]
        mask = [_lane_mask(128, 64 * hh, 64) for hh in range(2)]
        terms = {(d, gi, p): _gla_chunk_terms(dirs[d][1][rows(gi), :], dirs[d][3][rows(gi), :], p, dirs[d][0])
                 for d, gi, p in pairs}
        dec_col = {k: _row_to_col(t[6]) for k, t in terms.items()}
        vh = {(d, gi, h): dirs[d][2][rows(gi), 128 * h:128 * h + 128].astype(BF16)
              for d in range(2) for gi in range(G) for h in range(GLA_HEADS)}
        qm = {(d, gi, p, hh): (terms[d, gi, p][3] * mask[hh]).astype(BF16) for d, gi, p, hh in heads}
        a_raw = {(d, gi, p, hh): _dot(qm[d, gi, p, hh], terms[d, gi, p][4].astype(BF16), "nt") for d, gi, p, hh in heads}
        upd = {(d, gi, p, hh): _dot((terms[d, gi, p][5] * mask[hh]).astype(BF16), vh[d, gi, 2 * p + hh], "tn")
               for d, gi, p, hh in heads}
        intra = {(d, gi, p, hh): _dot(jnp.where(terms[d, gi, p][7], a_raw[d, gi, p, hh], 0.0).astype(BF16), vh[d, gi, 2 * p + hh])
                 for d, gi, p, hh in heads}
        state = {(d, h): dirs[d][6][h] for d in range(2) for h in range(GLA_HEADS)}
        for k in range(G):
            for d in range(2):
                gi = G - 1 - k if dirs[d][0] else k
                for p in range(GLA_HEADS // 2):
                    for hh in range(2):
                        h = 2 * p + hh
                        sp = state[d, h]
                        dirs[d][4][rows(gi), 128 * h:128 * h + 128] = intra[d, gi, p, hh] + _dot(qm[d, gi, p, hh], sp.astype(BF16))
                        dirs[d][5][gi, h] = sp
                        state[d, h] = dec_col[d, gi, p] * sp + upd[d, gi, p, hh]
        for d in range(2):
            for h in range(GLA_HEADS):
                dirs[d][6][h] = state[d, h]

    fw = lambda i: i
    bw = lambda i: nb - 1 - i
    zero = lambda i: 0
    in_specs = []
    for ix, col in ((fw, 0), (bw, 1)):
        in_specs += [zqk.spec(GC, WQK, ix, zero), zvv.spec(GC, WV, ix, zero),
                     pl.BlockSpec((GC, WQK // 2), functools.partial(lambda i, ix, col: (ix(i), col), ix=ix, col=col))]
    return _ride_call(
        body, ride, name=name, grid=(nb,), in_specs=in_specs,
        out_specs=[pl.BlockSpec((GC, WV), lambda i: (i, 0)), pl.BlockSpec((GC, WV), lambda i: (nb - 1 - i, 0)),
                   pl.BlockSpec((G, GLA_HEADS, 128, 128), lambda i: (i, 0, 0, 0)),
                   pl.BlockSpec((G, GLA_HEADS, 128, 128), lambda i: (nb - 1 - i, 0, 0, 0))],
        out_shape=[jax.ShapeDtypeStruct((S, WV), F32)] * 2 + [jax.ShapeDtypeStruct((n, GLA_HEADS, 128, 128), F32)] * 2,
        scratch_shapes=[pltpu.VMEM((GLA_HEADS, 128, 128), F32)] * 2, args=(z, z, gfb, z, z, gfb), semantics=("arbitrary",))


def _gla_bwd(z, c_qk, c_v, gfb, do, s_f, s_b, name, ride=None):
    S = z.shape[0]
    C = GLA_CHUNK
    n = S // C
    G = math.gcd(GLA_GROUP, n)
    nb, GC = n // G, G * C
    WQK = 2 * GLA_HEADS * GLA_DK
    WV = GLA_HEADS * GLA_DV
    zqk, zvv = _V(z, c_qk, WQK), _V(z, c_v, WV)

    def body(qkf_ref, vf_ref, gf_ref, dof_ref, sf_ref, qkb_ref, vb_ref, gb_ref, dob_ref, sb_ref,
             dqkf_ref, dvf_ref, dgf_ref, dqkb_ref, dvb_ref, dgb_ref, dstf, dstb):
        @pl.when(pl.program_id(0) == 0)
        def _():
            dstf[...] = jnp.zeros_like(dstf)
            dstb[...] = jnp.zeros_like(dstb)

        dirs = ((False, qkf_ref, vf_ref, gf_ref, dof_ref, sf_ref, dqkf_ref, dvf_ref, dgf_ref, dstf),
                (True, qkb_ref, vb_ref, gb_ref, dob_ref, sb_ref, dqkb_ref, dvb_ref, dgb_ref, dstb))
        rows = lambda gi: slice(gi * C, (gi + 1) * C)
        pairs = [(d, gi, p) for d in range(2) for gi in range(G) for p in range(GLA_HEADS // 2)]
        heads = [(d, gi, p, hh) for d, gi, p in pairs for hh in range(2)]
        mask = [_lane_mask(128, 64 * hh, 64) for hh in range(2)]
        T = {(d, gi, p): _gla_chunk_terms(dirs[d][1][rows(gi), :], dirs[d][3][rows(gi), :], p, dirs[d][0]) for d, gi, p in pairs}
        dec_col = {k: _row_to_col(t[6]) for k, t in T.items()}
        hd = lambda d, gi, p, hh: (d, gi, 2 * p + hh)
        vh = {(d, gi, h): dirs[d][2][rows(gi), 128 * h:128 * h + 128].astype(BF16)
              for d in range(2) for gi in range(G) for h in range(GLA_HEADS)}
        doh = {(d, gi, h): dirs[d][4][rows(gi), 128 * h:128 * h + 128].astype(BF16)
               for d in range(2) for gi in range(G) for h in range(GLA_HEADS)}
        sp = {(d, gi, h): dirs[d][5][gi, h] for d in range(2) for gi in range(G) for h in range(GLA_HEADS)}
        qm = {u: (T[u[:3]][3] * mask[u[3]]).astype(BF16) for u in heads}
        kem = {u: (T[u[:3]][4] * mask[u[3]]).astype(BF16) for u in heads}
        kendm = {u: (T[u[:3]][5] * mask[u[3]]).astype(BF16) for u in heads}
        a_raw = {u: _dot(qm[u], T[u[:3]][4].astype(BF16), "nt") for u in heads}
        da_raw = {u: _dot(doh[hd(*u)], vh[hd(*u)], "nt") for u in heads}
        w_upd = {u: _dot(qm[u], doh[hd(*u)], "tn") for u in heads}
        dqe_s = {u: _dot(doh[hd(*u)], sp[hd(*u)].astype(BF16), "nt") for u in heads}
        a = {u: jnp.where(T[u[:3]][7], a_raw[u], 0.0).astype(BF16) for u in heads}
        da = {u: jnp.where(T[u[:3]][7], da_raw[u], 0.0).astype(BF16) for u in heads}
        dqe = {u: _dot(da[u], kem[u]) + dqe_s[u] for u in heads}
        dke = {u: _dot(da[u], qm[u], "tn") for u in heads}
        dv_a = {u: _dot(a[u], doh[hd(*u)], "tn") for u in heads}
        ds = {}
        for d in range(2):
            cur = [dirs[d][9][h] for h in range(GLA_HEADS)]
            for gi in (range(G) if dirs[d][0] else reversed(range(G))):
                for p in range(GLA_HEADS // 2):
                    for hh in range(2):
                        h = 2 * p + hh
                        ds[d, gi, p, hh] = cur[h]
                        cur[h] = dec_col[d, gi, p] * cur[h] + w_upd[d, gi, p, hh]
            for h in range(GLA_HEADS):
                dirs[d][9][h] = cur[h]
        dsb = {u: ds[u].astype(BF16) for u in heads}
        dv_b = {u: _dot(kendm[u], dsb[u]) for u in heads}
        dkend = {u: _dot(vh[hd(*u)], dsb[u], "nt") * mask[u[3]] for u in heads}
        ddec = {u: _col_to_row(jnp.sum(ds[u] * sp[hd(*u)], axis=1, keepdims=True)) for u in heads}
        for u in heads:
            d, gi, h = hd(*u)
            dirs[d][7][rows(gi), 128 * h:128 * h + 128] = dv_a[u] + dv_b[u]
        i, j = _iota((C, C), 0), _iota((C, C), 1)
        for d, gi, p in pairs:
            rev = dirs[d][0]
            b2, bl2, eb, qe2, ke2, kend2, dec2, _ = T[d, gi, p]
            u0, u1 = (d, gi, p, 0), (d, gi, p, 1)
            dqe2, dke2, dkend2, ddec2 = dqe[u0] + dqe[u1], dke[u0] + dke[u1], dkend[u0] + dkend[u1], ddec[u0] + ddec[u1]
            dirs[d][6][rows(gi), 128 * p:128 * p + 128] = dqe2 * eb * (GLA_DK ** -0.5)
            dirs[d][6][rows(gi), 256 + 128 * p:256 + 128 * p + 128] = dke2 * jnp.exp(-b2) + dkend2 * jnp.exp(bl2 - b2)
            dkk = dkend2 * kend2
            dbl2 = jnp.sum(dkk, axis=0, keepdims=True) + ddec2 * dec2
            edge = _iota((C, 128), 0) == (0 if rev else C - 1)
            db2 = dqe2 * qe2 - dke2 * ke2 - dkk + jnp.where(edge, dbl2, 0.0)
            cum_t = jnp.where((j <= i) if rev else (j >= i), 1.0, 0.0).astype(F32)
            dirs[d][8][rows(gi), 128 * p:128 * p + 128] = _dot(cum_t, db2, prec=HI)

    fw = lambda i: nb - 1 - i
    bw = lambda i: i
    zero = lambda i: 0
    in_specs, out_specs = [], []
    for ix, col in ((fw, 0), (bw, 1)):
        blk = functools.partial(lambda i, ix: (ix(i), 0), ix=ix)
        in_specs += [zqk.spec(GC, WQK, ix, zero), zvv.spec(GC, WV, ix, zero),
                     pl.BlockSpec((GC, WQK // 2), functools.partial(lambda i, ix, col: (ix(i), col), ix=ix, col=col)),
                     pl.BlockSpec((GC, WV), blk),
                     pl.BlockSpec((G, GLA_HEADS, 128, 128), functools.partial(lambda i, ix: (ix(i), 0, 0, 0), ix=ix))]
        out_specs += [pl.BlockSpec((GC, WQK), blk), pl.BlockSpec((GC, WV), blk), pl.BlockSpec((GC, WQK // 2), blk)]
    shapes = [jax.ShapeDtypeStruct((S, WQK), F32), jax.ShapeDtypeStruct((S, WV), F32), jax.ShapeDtypeStruct((S, WQK // 2), F32)]
    return _ride_call(
        body, ride, name=name, grid=(nb,), in_specs=in_specs, out_specs=out_specs, out_shape=shapes * 2,
        scratch_shapes=[pltpu.VMEM((GLA_HEADS, 128, 128), F32)] * 2, args=(z, z, gfb, do, s_f, z, z, gfb, do, s_b),
        semantics=("arbitrary",))


def _gla_post(o_f, o_b, z, c_r, gn, name, tm=512):
    S, WV = o_f.shape
    tm = min(tm, S)
    zr = _V(z, c_r, WV)

    def body(of_ref, ob_ref, r_ref, g_ref, y_ref):
        gr = r_ref[...]
        sil = gr * jax.nn.sigmoid(gr)
        for h in range(GLA_HEADS):
            sl = slice(GLA_DV * h, GLA_DV * (h + 1))
            o = of_ref[:, sl] + ob_ref[:, sl]
            on = o * lax.rsqrt(jnp.mean(o * o, axis=-1, keepdims=True) + EPS) * g_ref[...]
            y_ref[:, sl] = (on * sil[:, sl]).astype(BF16)

    blk = pl.BlockSpec((tm, WV), lambda i: (i, 0))
    return pl.pallas_call(
        body, name=name, grid=(S // tm,),
        in_specs=[blk, blk, zr.spec(tm, WV, lambda i: i, lambda i: 0), pl.BlockSpec((1, GLA_DV), lambda i: (0, 0))],
        out_specs=blk, out_shape=jax.ShapeDtypeStruct((S, WV), BF16), compiler_params=_cparams("parallel"),
    )(o_f, o_b, z, gn)


def _gla_post_bwd(o_f, o_b, z, c_r, gn, dy, name, tm=512):
    S, WV = o_f.shape
    tm = min(tm, S)
    zr = _V(z, c_r, WV)

    def body(of_ref, ob_ref, r_ref, g_ref, dy_ref, do_ref, dr_ref, dg_ref):
        gr = r_ref[...]
        sig = jax.nn.sigmoid(gr)
        sil = gr * sig
        dyv = dy_ref[...].astype(F32)
        dgn = jnp.zeros((tm, GLA_DV), F32)
        for h in range(GLA_HEADS):
            sl = slice(GLA_DV * h, GLA_DV * (h + 1))
            o = of_ref[:, sl] + ob_ref[:, sl]
            rstd = lax.rsqrt(jnp.mean(o * o, axis=-1, keepdims=True) + EPS)
            xhat = o * rstd
            don = dyv[:, sl] * sil[:, sl]
            dr_ref[:, sl] = (dyv[:, sl] * xhat * g_ref[...] * (sig[:, sl] * (1.0 + gr[:, sl] * (1.0 - sig[:, sl])))).astype(BF16)
            dxhat = don * g_ref[...]
            do_ref[:, sl] = rstd * (dxhat - xhat * jnp.mean(dxhat * xhat, axis=-1, keepdims=True))
            dgn = dgn + don * xhat
        _acc_rows(dg_ref, dgn, pl.program_id(0) == 0)

    blk = pl.BlockSpec((tm, WV), lambda i: (i, 0))
    return pl.pallas_call(
        body, name=name, grid=(S // tm,),
        in_specs=[blk, blk, zr.spec(tm, WV, lambda i: i, lambda i: 0), pl.BlockSpec((1, GLA_DV), lambda i: (0, 0)), blk],
        out_specs=[blk, blk, pl.BlockSpec((8, GLA_DV), lambda i: (0, 0))],
        out_shape=[jax.ShapeDtypeStruct((S, WV), F32), jax.ShapeDtypeStruct((S, WV), BF16), jax.ShapeDtypeStruct((8, GLA_DV), F32)],
        compiler_params=_cparams("arbitrary"),
    )(o_f, o_b, z, gn, dy)


def _gla_assemble(dqk_f, dqk_b, dv_f, dv_b, dgr, name, tm=512):
    S = dqk_f.shape[0]
    tm = min(tm, S)

    def body(a_ref, b_ref, c_ref, d_ref, r_ref, o_ref):
        o_ref[:, 0:512] = (a_ref[...] + b_ref[...]).astype(BF16)
        o_ref[:, 512:1024] = (c_ref[...] + d_ref[...]).astype(BF16)
        o_ref[:, 1024:1536] = r_ref[...]

    blk = pl.BlockSpec((tm, 512), lambda i: (i, 0))
    return pl.pallas_call(
        body, name=name, grid=(S // tm,), in_specs=[blk] * 5, out_specs=pl.BlockSpec((tm, 1536), lambda i: (i, 0)),
        out_shape=jax.ShapeDtypeStruct((S, 1536), BF16), compiler_params=_cparams("parallel"),
    )(dqk_f, dqk_b, dv_f, dv_b, dgr)


def _rope(r, cos, sg):
    return r * cos + pltpu.roll(r, 64, 1) * sg


def _unrope(dy, cos, sg):
    return dy * cos + pltpu.roll(dy * sg, 64, 1)


def _mla_prep(z, c_q, c_kr, wuq, wukv, g_cq, g_ckv, g_q, g_k, cos, sg, name, tm=256):
    S = z.shape[0]
    tm = min(tm, S)
    zc, zk = _V(z, c_q, 2 * MLA_RANK), _V(z, c_kr, 128)
    inv = 1.0 / MLA_QK

    def body(zc_ref, zk_ref, wuq_ref, wukv_ref, gcq_ref, gckv_ref, gq_ref, gk_ref, cos_ref, sg_ref,
             q_ref, k_ref, v_ref, cqn_ref, ckvn_ref):
        def norm(xv, gv):
            return (xv * lax.rsqrt(jnp.mean(xv * xv, axis=-1, keepdims=True) + EPS) * gv).astype(BF16)

        cqn = norm(zc_ref[:, 0:MLA_RANK], gcq_ref[...])
        ckvn = norm(zc_ref[:, MLA_RANK:2 * MLA_RANK], gckv_ref[...])
        cqn_ref[...] = cqn
        ckvn_ref[...] = ckvn
        qf = _dot(cqn, wuq_ref[...])
        kv = _dot(ckvn, wukv_ref[...])
        kr = zk_ref[...]
        krss = jnp.sum(kr * kr, axis=-1, keepdims=True)
        cosv, sgv = cos_ref[...], sg_ref[...]
        gq, gk = gq_ref[...], gk_ref[...]
        for h in range(MLA_HEADS):
            qh = qf[:, MLA_SLOT * h:MLA_SLOT * (h + 1)]
            qhn = qh * lax.rsqrt(jnp.sum(qh * qh, axis=-1, keepdims=True) * inv + EPS) * gq
            q_ref[:, MLA_SLOT * h:MLA_SLOT * h + 128] = (qhn[:, 0:128] * MLA_QSCALE).astype(BF16)
            q_ref[:, MLA_SLOT * h + 128:MLA_SLOT * (h + 1)] = (_rope(qhn[:, 128:256], cosv, sgv) * MLA_QSCALE).astype(BF16)
            kn = kv[:, 256 * h:256 * h + 128]
            rstd = lax.rsqrt((jnp.sum(kn * kn, axis=-1, keepdims=True) + krss) * inv + EPS)
            k_ref[:, MLA_SLOT * h:MLA_SLOT * h + 128] = (kn * rstd * gk[:, 0:128]).astype(BF16)
            k_ref[:, MLA_SLOT * h + 128:MLA_SLOT * (h + 1)] = _rope(kr * rstd * gk[:, 128:256], cosv, sgv).astype(BF16)
            v_ref[:, 128 * h:128 * (h + 1)] = kv[:, 256 * h + 128:256 * (h + 1)].astype(BF16)

    row = lambda w: pl.BlockSpec((tm, w), lambda i: (i, 0))
    const = lambda r, w: pl.BlockSpec((r, w), lambda i: (0, 0))
    W = MLA_HEADS * MLA_SLOT
    return pl.pallas_call(
        body, name=name, grid=(S // tm,),
        in_specs=[zc.spec(tm, 2 * MLA_RANK, lambda i: i, lambda i: 0), zk.spec(tm, 128, lambda i: i, lambda i: 0),
                  const(MLA_RANK, W), const(MLA_RANK, W), const(1, MLA_RANK), const(1, MLA_RANK), const(1, MLA_SLOT),
                  const(1, MLA_SLOT), row(128), row(128)],
        out_specs=[row(W), row(W), row(MLA_HEADS * MLA_V), row(MLA_RANK), row(MLA_RANK)],
        out_shape=[jax.ShapeDtypeStruct((S, W), BF16), jax.ShapeDtypeStruct((S, W), BF16),
                   jax.ShapeDtypeStruct((S, MLA_HEADS * MLA_V), BF16), jax.ShapeDtypeStruct((S, MLA_RANK), BF16),
                   jax.ShapeDtypeStruct((S, MLA_RANK), BF16)],
        compiler_params=_cparams("parallel"),
    )(z, z, wuq, wukv, g_cq, g_ckv, g_q, g_k, cos, sg)


def _mla_prep_bwd(z, c_kr, cqn, ckvn, wuq, wukv, g_q, g_k, cos, sg, dq, dk, dv, name, tm=256):
    S = z.shape[0]
    tm = min(tm, S)
    zk = _V(z, c_kr, 128)
    inv = 1.0 / MLA_QK

    def body(zk_ref, cqn_ref, ckvn_ref, wuq_ref, wukv_ref, gq_ref, gk_ref, cos_ref, sg_ref, dq_ref, dk_ref, dv_ref,
             dqf_ref, dkv_ref, dkr_ref, dgq_ref, dgk_ref):
        first = pl.program_id(0) == 0
        qf = _dot(cqn_ref[...], wuq_ref[...])
        kv = _dot(ckvn_ref[...], wukv_ref[...])
        kr = zk_ref[...]
        krss = jnp.sum(kr * kr, axis=-1, keepdims=True)
        cosv, sgv = cos_ref[...], sg_ref[...]
        gq, gk = gq_ref[...], gk_ref[...]
        dkr = jnp.zeros((tm, 128), F32)
        dgq = jnp.zeros((tm, MLA_SLOT), F32)
        dgkn = jnp.zeros((tm, 128), F32)
        dgkr = jnp.zeros((tm, 128), F32)
        for h in range(MLA_HEADS):
            qh = qf[:, MLA_SLOT * h:MLA_SLOT * (h + 1)]
            rstd = lax.rsqrt(jnp.sum(qh * qh, axis=-1, keepdims=True) * inv + EPS)
            xhat = qh * rstd
            dyn = jnp.concatenate([dq_ref[:, MLA_SLOT * h:MLA_SLOT * h + 128],
                                   _unrope(dq_ref[:, MLA_SLOT * h + 128:MLA_SLOT * (h + 1)], cosv, sgv)], axis=-1)
            dxhat = dyn * gq
            dqf_ref[:, MLA_SLOT * h:MLA_SLOT * (h + 1)] = (
                rstd * (dxhat - xhat * (jnp.sum(dxhat * xhat, axis=-1, keepdims=True) * inv))).astype(BF16)
            dgq = dgq + dyn * xhat

            kn = kv[:, 256 * h:256 * h + 128]
            rstd = lax.rsqrt((jnp.sum(kn * kn, axis=-1, keepdims=True) + krss) * inv + EPS)
            xn, xr = kn * rstd, kr * rstd
            dyn_n = dk_ref[:, MLA_SLOT * h:MLA_SLOT * h + 128] * (1.0 / MLA_QSCALE)
            dyn_r = _unrope(dk_ref[:, MLA_SLOT * h + 128:MLA_SLOT * (h + 1)] * (1.0 / MLA_QSCALE), cosv, sgv)
            dxn, dxr = dyn_n * gk[:, 0:128], dyn_r * gk[:, 128:256]
            proj = (jnp.sum(dxn * xn, axis=-1, keepdims=True) + jnp.sum(dxr * xr, axis=-1, keepdims=True)) * inv
            dkv_ref[:, 256 * h:256 * h + 128] = (rstd * (dxn - xn * proj)).astype(BF16)
            dkv_ref[:, 256 * h + 128:256 * (h + 1)] = dv_ref[:, 128 * h:128 * (h + 1)].astype(BF16)
            dkr = dkr + rstd * (dxr - xr * proj)
            dgkn = dgkn + dyn_n * xn
            dgkr = dgkr + dyn_r * xr
        dkr_ref[...] = dkr.astype(BF16)
        _acc_rows(dgq_ref, dgq, first)
        _acc_rows(dgk_ref, jnp.concatenate([dgkn, dgkr], axis=-1), first)

    row = lambda w: pl.BlockSpec((tm, w), lambda i: (i, 0))
    const = lambda r, w: pl.BlockSpec((r, w), lambda i: (0, 0))
    W = MLA_HEADS * MLA_SLOT
    return pl.pallas_call(
        body, name=name, grid=(S // tm,),
        in_specs=[zk.spec(tm, 128, lambda i: i, lambda i: 0), row(MLA_RANK), row(MLA_RANK), const(MLA_RANK, W),
                  const(MLA_RANK, W), const(1, MLA_SLOT), const(1, MLA_SLOT), row(128), row(128), row(W), row(W),
                  row(MLA_HEADS * MLA_V)],
        out_specs=[row(W), row(W), row(128), const(8, MLA_SLOT), const(8, MLA_SLOT)],
        out_shape=[jax.ShapeDtypeStruct((S, W), BF16), jax.ShapeDtypeStruct((S, W), BF16), jax.ShapeDtypeStruct((S, 128), BF16),
                   jax.ShapeDtypeStruct((8, MLA_SLOT), F32), jax.ShapeDtypeStruct((8, MLA_SLOT), F32)],
        compiler_params=_cparams("arbitrary"),
    )(z, cqn, ckvn, wuq, wukv, g_q, g_k, cos, sg, dq, dk, dv)


def _exp2_rows(s2):
    e = jnp.exp2(s2 - jnp.max(s2, axis=-1, keepdims=True))
    return e, 1.0 / jnp.sum(e, axis=-1, keepdims=True)


def _mla_attn(q, k, v, name, tq=256, ride=None):
    S = q.shape[0]
    tq = min(tq, S)

    def body(q_ref, k_ref, v_ref, o_ref):
        e, rl = _exp2_rows(_dot(q_ref[...], k_ref[...], "nt"))
        o_ref[...] = (_dot(e.astype(BF16), v_ref[...]) * rl).astype(BF16)

    (o,), got = _ride_call(
        body, ride, name=name, grid=(MLA_HEADS, S // tq),
        in_specs=[pl.BlockSpec((tq, MLA_SLOT), lambda h, i: (i, h)), pl.BlockSpec((S, MLA_SLOT), lambda h, i: (0, h)),
                  pl.BlockSpec((S, MLA_V), lambda h, i: (0, h))],
        out_specs=[pl.BlockSpec((tq, MLA_V), lambda h, i: (i, h))],
        out_shape=[jax.ShapeDtypeStruct((S, MLA_HEADS * MLA_V), BF16)], args=(q, k, v), semantics=("parallel", "parallel"))
    return o, got


def _mla_attn_bwd(q, k, v, do, name, tq=256, ride=None):
    S = q.shape[0]
    tq = min(tq, S)
    scale = MLA_QK ** -0.5

    def body(q_ref, k_ref, v_ref, do_ref, dq_ref, dk_ref, dv_ref):
        @pl.when(pl.program_id(1) == 0)
        def _():
            dk_ref[...] = jnp.zeros_like(dk_ref)
            dv_ref[...] = jnp.zeros_like(dv_ref)

        qv, kvv, dov = q_ref[...], k_ref[...], do_ref[...].astype(F32)
        e, rl = _exp2_rows(_dot(qv, kvv, "nt"))
        dp = _dot((dov * (scale * rl)).astype(BF16), v_ref[...], "nt")
        ds = (e * (dp - jnp.sum(e * dp, axis=-1, keepdims=True) * rl)).astype(BF16)
        dq_ref[...] = _dot(ds, kvv)
        dk_ref[...] += _dot(ds, qv, "tn")
        dv_ref[...] += _dot(e.astype(BF16), (dov * rl).astype(BF16), "tn")

    W = MLA_HEADS * MLA_SLOT
    return _ride_call(
        body, ride, name=name, grid=(MLA_HEADS, S // tq),
        in_specs=[pl.BlockSpec((tq, MLA_SLOT), lambda h, i: (i, h)), pl.BlockSpec((S, MLA_SLOT), lambda h, i: (0, h)),
                  pl.BlockSpec((S, MLA_V), lambda h, i: (0, h)), pl.BlockSpec((tq, MLA_V), lambda h, i: (i, h))],
        out_specs=[pl.BlockSpec((tq, MLA_SLOT), lambda h, i: (i, h)), pl.BlockSpec((S, MLA_SLOT), lambda h, i: (0, h)),
                   pl.BlockSpec((S, MLA_V), lambda h, i: (0, h))],
        out_shape=[jax.ShapeDtypeStruct((S, W), F32), jax.ShapeDtypeStruct((S, W), F32),
                   jax.ShapeDtypeStruct((S, MLA_HEADS * MLA_V), F32)],
        args=(q, k, v, do), semantics=("parallel", "arbitrary"))


def _merge(ys, ws, z, name, tm=256):
    S = z.shape[0]
    D = ws[0].shape[1]
    tm = min(tm, S)
    zg = _V(z, 0, 3 * D)

    def body(y0, y1, y2, w0, w1, w2, g_ref, m_ref, p0, p1, p2):
        acc = jnp.zeros((tm, D), F32)
        for i, (y_ref, w_ref, p_ref) in enumerate(((y0, w0, p0), (y1, w1, p1), (y2, w2, p2))):
            pv = _dot(y_ref[...], w_ref[...])
            p_ref[...] = pv.astype(BF16)
            acc = acc + jax.nn.sigmoid(g_ref[:, D * i:D * (i + 1)]) * pv
        m_ref[...] = acc.astype(BF16)

    yb = pl.BlockSpec((tm, ys[0].shape[1]), lambda i: (i, 0))
    wb = pl.BlockSpec(ws[0].shape, lambda i: (0, 0))
    ob = pl.BlockSpec((tm, D), lambda i: (i, 0))
    return pl.pallas_call(
        body, name=name, grid=(S // tm,), in_specs=[yb] * 3 + [wb] * 3 + [zg.spec(tm, 3 * D, lambda i: i, lambda i: 0)],
        out_specs=[ob] * 4, out_shape=[jax.ShapeDtypeStruct((S, D), BF16)] * 4, compiler_params=_cparams("parallel"),
    )(*ys, *ws, z)


def _merge_bwd(dmixed, ps, z, name, tm=256):
    S, D = dmixed.shape
    tm = min(tm, S)
    zg = _V(z, 0, 3 * D)

    def body(dm_ref, p0, p1, p2, g_ref, d0, d1, d2, dg_ref):
        dm = dm_ref[...]
        for i, (p_ref, d_ref) in enumerate(((p0, d0), (p1, d1), (p2, d2))):
            gt = jax.nn.sigmoid(g_ref[:, D * i:D * (i + 1)])
            d_ref[...] = (dm * gt).astype(BF16)
            dg_ref[:, D * i:D * (i + 1)] = (dm * p_ref[...].astype(F32) * gt * (1.0 - gt)).astype(BF16)

    ob = pl.BlockSpec((tm, D), lambda i: (i, 0))
    return pl.pallas_call(
        body, name=name, grid=(S // tm,), in_specs=[ob] * 4 + [zg.spec(tm, 3 * D, lambda i: i, lambda i: 0)],
        out_specs=[ob] * 3 + [pl.BlockSpec((tm, 3 * D), lambda i: (i, 0))],
        out_shape=[jax.ShapeDtypeStruct((S, D), BF16)] * 3 + [jax.ShapeDtypeStruct((S, 3 * D), BF16)],
        compiler_params=_cparams("parallel"),
    )(dmixed, *ps, z)


def _loss_head(y, target, name, tm=512):
    S, D = y.shape
    tm = min(tm, S)

    def body(y_ref, t_ref, dy_ref, l_ref):
        e = y_ref[...] - t_ref[...]
        dy_ref[...] = e * (1.0 / D)
        sq = e * e
        part = jnp.sum(sq.reshape(tm // 8, 8, D), axis=0)
        part = jnp.sum(part.reshape(8, D // 128, 128), axis=1) * (0.5 / D)

        @pl.when(pl.program_id(0) == 0)
        def _():
            l_ref[...] = part

        @pl.when(pl.program_id(0) != 0)
        def _():
            l_ref[...] += part

    blk = pl.BlockSpec((tm, D), lambda i: (i, 0))
    return pl.pallas_call(
        body, name=name, grid=(S // tm,), in_specs=[blk, blk], out_specs=[blk, pl.BlockSpec((8, 128), lambda i: (0, 0))],
        out_shape=[jax.ShapeDtypeStruct((S, D), F32), jax.ShapeDtypeStruct((8, 128), F32)],
        compiler_params=_cparams("arbitrary"),
    )(y, target)


def _fold(parts, name, fold=None):
    L, _, W = parts.shape
    assert L <= 8

    def body(*refs):
        p_ref, o_ref = refs[0], refs[-1]
        rows = [jnp.sum(p_ref[l], axis=0, keepdims=True) for l in range(L)]
        rows += [jnp.zeros((1, W), F32)] * (8 - L)
        sums = jnp.concatenate(rows, axis=0)
        o_ref[...] = sums if fold is None else _dot(sums, refs[1][...], prec=HI)

    args = (parts,) if fold is None else (parts, jnp.asarray(fold))
    wout = W if fold is None else 128
    return pl.pallas_call(body, name=name, out_shape=jax.ShapeDtypeStruct((8, wout), F32))(*args)[:L]


def _adamw(w, g, m, v, name, q=None, ride=None):
    R, C = w.shape
    tr = R
    for cand in (512, 256, 128, 64, 32, 16, 8):
        if R % cand == 0 and cand * C * 4 <= 2 * 2**20:
            tr = cand
            break

    def body(*refs):
        if q is None:
            w_ref, g_ref, m_ref, v_ref, d_ref, nm_ref, nv_ref = refs
            gv = g_ref[...]
        else:
            w_ref, g_ref, q_ref, m_ref, v_ref, go_ref, d_ref, nm_ref, nv_ref = refs
            gv = g_ref[...] + q_ref[...]
            go_ref[...] = gv
        mn = ADAM_B1 * m_ref[...] + (1.0 - ADAM_B1) * gv
        vn = ADAM_B2 * v_ref[...] + (1.0 - ADAM_B2) * (gv * gv)
        nm_ref[...] = mn
        nv_ref[...] = vn
        m_hat = mn / (1.0 - ADAM_B1 ** ADAM_STEP)
        v_hat = vn / (1.0 - ADAM_B2 ** ADAM_STEP)
        d_ref[...] = -ADAM_LR * (m_hat / (jnp.sqrt(v_hat) + ADAM_EPS) + ADAM_WD * w_ref[...])

    blk = pl.BlockSpec((tr, C), lambda i: (i, 0))
    args = (w, g, m, v) if q is None else (w, g, q, m, v)
    nout = 3 if q is None else 4
    outs, got = _ride_call(
        body, ride, name=name, grid=(R // tr,), in_specs=[blk] * len(args), out_specs=[blk] * nout,
        out_shape=[jax.ShapeDtypeStruct((R, C), F32)] * nout, args=args, semantics=("parallel",))
    return outs if ride is None else (outs, got)


def _sibling_exchange(srcs, name, ride=()):
    n, k = len(srcs), len(ride)

    def body(*refs):
        src_refs, dst_refs = refs[:n], refs[n + 2 * k:2 * n + 2 * k]
        send_sems, recv_sems = refs[2 * n + 3 * k:2 * n + 3 * k + 2]
        x, y, c = lax.axis_index("x"), lax.axis_index("y"), lax.axis_index("c")
        if k:
            start, finish = _ride_ops(ride, refs[n:n + k], refs[2 * n + 2 * k:2 * n + 3 * k], *refs[2 * n + 3 * k + 2:])
            start()
        copies = [pltpu.make_async_remote_copy(src_ref=src_refs[t], dst_ref=dst_refs[t], send_sem=send_sems.at[t],
                                               recv_sem=recv_sems.at[t], device_id=(x, y, 1 - c), device_id_type=MESH)
                  for t in range(n)]
        for cp in copies:
            cp.start()
        for cp in copies:
            cp.wait()
        if k:
            finish()

    res = pl.pallas_call(
        body, name=name, in_specs=[_ANY] * (n + 2 * k), out_specs=[_ANY] * (n + k),
        out_shape=[jax.ShapeDtypeStruct(s.shape, s.dtype) for s in srcs] + [jax.ShapeDtypeStruct(it.dst.shape, it.dst.dtype) for it in ride],
        input_output_aliases={n + k + t: n + t for t in range(k)},
        scratch_shapes=[pltpu.SemaphoreType.DMA((n,)), pltpu.SemaphoreType.DMA((n,))] + (_RIDE_SEMS(k) if k else []),
    )(*srcs, *[it.src for it in ride], *[it.dst for it in ride])
    return res[:n], {it.name: o for it, o in zip(ride, res[n:])}


def _allreduce_small(v, name):
    R = v.shape[0]

    def body(v_ref, o_ref, slots, send_sems, recv_sems):
        x, y, c = lax.axis_index("x"), lax.axis_index("y"), lax.axis_index("c")
        me = 4 * x + 2 * y + c
        slots[me] = v_ref[...]
        sent = []
        for r in range(1, 8):
            fx, fy, fc = (r >> 2) & 1, (r >> 1) & 1, r & 1
            px, py, pc = (1 - x) if fx else x, (1 - y) if fy else y, (1 - c) if fc else c
            peer = 4 * px + 2 * py + pc

            def copy(slot, r=r, px=px, py=py, pc=pc):
                return pltpu.make_async_remote_copy(
                    src_ref=v_ref, dst_ref=slots.at[slot], send_sem=send_sems.at[r - 1], recv_sem=recv_sems.at[r - 1],
                    device_id=(px, py, pc), device_id_type=MESH)

            cp = copy(me)
            cp.start()
            sent.append((cp, copy(peer)))
        for cp, arrival in sent:
            cp.wait_send()
            arrival.wait_recv()
        acc = slots[0]
        for k in range(1, 8):
            acc = acc + slots[k]
        o_ref[...] = acc

    vm = pl.BlockSpec(memory_space=pltpu.VMEM)
    return pl.pallas_call(
        body, name=name, in_specs=[vm], out_specs=vm, out_shape=jax.ShapeDtypeStruct((R, 128), F32),
        scratch_shapes=[pltpu.VMEM((8, R, 128), F32), pltpu.SemaphoreType.DMA((7,)), pltpu.SemaphoreType.DMA((7,))],
    )(v)


def _sum4(recv, name, tr=512):
    _, R, W = recv.shape
    tr = _tile(R, tr)
    assert R % tr == 0

    def body(r_ref, o_ref):
        o_ref[...] = ((r_ref[0].astype(F32) + r_ref[1].astype(F32)) + r_ref[2].astype(F32)) + r_ref[3].astype(F32)

    return pl.pallas_call(
        body, name=name, grid=(R // tr,), in_specs=[pl.BlockSpec((4, tr, W), lambda i: (0, i, 0))],
        out_specs=pl.BlockSpec((tr, W), lambda i: (i, 0)), out_shape=jax.ShapeDtypeStruct((R, W), F32),
        compiler_params=_cparams("parallel"),
    )(recv)


W_NAMES = ("ffn1_norm", "ffn1_w1", "ffn1_w3", "ffn1_w2", "mix_norm", "w_in", "na_q_norm", "na_k_norm", "na_rpb",
           "gla_gf_up", "gla_gf_bias", "gla_gb_up", "gla_gb_bias", "gla_out_norm", "mla_cq_norm", "mla_ckv_norm",
           "mla_w_uq", "mla_w_ukv", "mla_q_norm", "mla_k_norm", "w_br_na", "w_br_gla", "w_br_mla", "w_out",
           "ffn2_norm", "ffn2_w1", "ffn2_w3", "ffn2_w2")
SHARDED = {"ffn1_w1": 2, "ffn1_w3": 2, "ffn1_w2": 1, "w_in": 2, "gla_gf_up": 2, "gla_gb_up": 2, "mla_w_uq": 2,
           "mla_w_ukv": 2, "w_br_na": 2, "w_br_gla": 2, "w_br_mla": 2, "w_out": 1, "ffn2_w1": 2, "ffn2_w3": 2,
           "ffn2_w2": 1}
REPLICATED = tuple(n for n in W_NAMES if n not in SHARDED)
FFN_W = ("ffn1_w1", "ffn1_w3", "ffn1_w2", "ffn2_w1", "ffn2_w3", "ffn2_w2")


def _win_layout(w, D):
    z = lambda n: jnp.zeros(w.shape[:-1] + (n,), w.dtype)
    return jnp.concatenate([w[..., O_GATES:], w[..., :O_GFL], w[..., O_CQ:O_KR], w[..., O_GFL:O_CQ], z(96),
                            w[..., O_KR:O_KR + 32], z(32), w[..., O_KR + 32:O_KR + 64], z(32)], axis=-1)


def _win_unlayout(dw, D):
    g = 3 * D
    return jnp.concatenate([dw[..., g:g + O_GFL], dw[..., g + 3584:g + 3616], dw[..., g + 3072:g + 3584],
                            dw[..., g + 3712:g + 3744], dw[..., g + 3776:g + 3808], dw[..., :g]], axis=-1)


def _uq_layout(w):
    s = w.shape[:-1]
    w = w.reshape(s + (MLA_HEADS, MLA_QK))
    z = jnp.zeros(s + (MLA_HEADS, 32), w.dtype)
    return jnp.concatenate([w[..., :160], z, w[..., 160:], z], axis=-1).reshape(s + (MLA_HEADS * MLA_SLOT,))


def _uq_unlayout(dw):
    s = dw.shape[:-1]
    dw = dw.reshape(s + (MLA_HEADS, MLA_SLOT))
    return jnp.concatenate([dw[..., :160], dw[..., 192:224]], axis=-1).reshape(s + (MLA_HEADS * MLA_QK,))


def _slot_layout(g):
    z = jnp.zeros(g.shape[:-1] + (32,), g.dtype)
    return jnp.concatenate([g[..., :160], z, g[..., 160:], z], axis=-1)


def _slot_unlayout(g):
    return jnp.concatenate([g[..., :160], g[..., 192:224]], axis=-1)


def _layer_fwd(x, w, cos, sg, rides):
    D = x.shape[1]
    NA, GL, ML, LR, KR = 3 * D, 3 * D + 1536, 3 * D + 3072, 3 * D + 3584, 3 * D + 3712
    got = {}
    x1, f1, arrived = _ffn_fwd(x, w["ffn1_norm"], w["ffn1_w1"], w["ffn1_w3"], w["ffn1_w2"], "ffn1", ride=rides.get("ffn1_up"))
    got.update(arrived)
    h = _rms_fwd(x1, w["mix_norm"], "mix_rms")
    nz = w["w_in"].shape[1]
    z, arrived = _mm([(h, w["w_in"])], "nn", F32, "w_in", tm=512, tn=_tile(nz, 1280), ride=rides.get("w_in", []))
    got.update(arrived)
    qn, kn, vb = _na_prep(z, NA, w["na_gq"], w["na_gk"], "na_prep")
    bias = _rpb_expand(w["na_rpb"], "rpb_expand")
    y_na, arrived = _na_attn(qn, kn, vb, bias, "na_attn", ride=rides.get("na_attn"))
    got.update(arrived)
    gfb = _gla_gates(z, LR, w["gla_wg"], w["gla_gbias"], "gla_gates")
    (o_f, o_b, s_f, s_b), arrived = _gla_fwd(z, GL, GL + 512, gfb, "gla_fwd", ride=rides.get("gla_fwd"))
    got.update(arrived)
    y_gla = _gla_post(o_f, o_b, z, GL + 1024, w["gla_out_norm"], "gla_post")
    q, k, v, cqn, ckvn = _mla_prep(z, ML, KR, w["mla_wuq"], w["mla_w_ukv"], w["mla_cq_norm"], w["mla_ckv_norm"],
                                   w["mla_gq"], w["mla_gk"], cos, sg, "mla_prep")
    y_mla, arrived = _mla_attn(q, k, v, "mla_attn", ride=rides.get("mla_attn"))
    got.update(arrived)
    mixed, p0, p1, p2 = _merge([y_na, y_gla, y_mla], [w["w_br_na"], w["w_br_gla"], w["w_br_mla"]], z, "merge")
    x2 = _mm([(mixed, w["w_out"])], "nn", F32, "w_out", tm=512, tn=1024, res=x1)
    x3, f2, _ = _ffn_fwd(x2, w["ffn2_norm"], got["ffn2_w1"], got["ffn2_w3"], got["ffn2_w2"], "ffn2")
    saved = dict(x=x, x1=x1, x2=x2, f1=f1, f2=f2, h=h, z=z, qn=qn, kn=kn, vb=vb, bias=bias, y_na=y_na, gfb=gfb, o_f=o_f,
                 o_b=o_b, s_f=s_f, s_b=s_b, y_gla=y_gla, q=q, k=k, v=v, cqn=cqn, ckvn=ckvn, y_mla=y_mla, mixed=mixed,
                 p0=p0, p1=p1, p2=p2)
    return x3, saved, got


def _split4(a, axis):
    n = a.shape[axis] // 4
    return jnp.stack([lax.slice_in_dim(a, j * n, (j + 1) * n, axis=axis) for j in range(4)]).astype(BF16)


def _layer_bwd(dx3, w, sv, cos, sg, bufs, recv, layer, prev):
    D = dx3.shape[1]
    at_layer = lambda chip: (chip, layer)
    pick = lambda *names: [prev[n] for n in names if n in prev]
    recv = dict(recv)
    NA, GL, ML, LR, KR = 3 * D, 3 * D + 1536, 3 * D + 3072, 3 * D + 3584, 3 * D + 3712
    z = sv["z"]
    g = {}
    dx2, g["ffn2_norm"], (g["ffn2_w1"], g["ffn2_w3"], g["ffn2_w2"]), got = _ffn_bwd(
        dx3, sv["x2"], w["ffn2_norm"], w["ffn2_w1"], w["ffn2_w3"], w["ffn2_w2"], sv["f2"], "ffn2",
        (bufs["ffn2_w1"], bufs["ffn2_w3"], bufs["ffn2_w2"]), layer, ride_down=pick("ffn1_w1"), ride_dh=pick("ffn1_w3"))
    recv.update(got)
    dmixed = _mm([(dx2, w["w_out"])], "nt", F32, "w_out_dx", tm=512, tn=512)
    g["w_out"] = _mm([(sv["mixed"], dx2)], "tn", F32, "w_out_dw", tm=D, tn=256)
    d0, d1, d2, dgates = _merge_bwd(dmixed, [sv["p0"], sv["p1"], sv["p2"]], z, "merge_bwd")
    dys = []
    for d, y, nm, dt in ((d0, sv["y_na"], "w_br_na", BF16), (d1, sv["y_gla"], "w_br_gla", F32), (d2, sv["y_mla"], "w_br_mla", BF16)):
        dys.append(_mm([(d, w[nm])], "nt", dt, nm + "_dy", tm=512, tn=512))
        g[nm] = _mm([(y, d)], "tn", F32, nm + "_dw", tm=512, tn=512)
    (dqn, dkn, dvn, dbias), got = _na_attn_bwd(sv["qn"], sv["kn"], sv["vb"], sv["bias"], dys[0], "na_attn_bwd",
                                               ride=pick("ffn1_w2", "mla_w_uq", "mla_w_ukv", "gla_gf_up", "gla_gb_up"))
    recv.update(got)
    dz_na, g["na_gq"], g["na_gk"] = _na_prep_bwd(z, NA, w["na_gq"], w["na_gk"], dqn, dkn, dvn, "na_prep_bwd")
    g["na_rpb"] = _rpb_reduce(dbias, "rpb_reduce")
    do, dgr, g["gla_out_norm"] = _gla_post_bwd(sv["o_f"], sv["o_b"], z, GL + 1024, w["gla_out_norm"], dys[1], "gla_post_bwd")
    (dqk_f, dv_f, dg_f, dqk_b, dv_b, dg_b), _ = _gla_bwd(z, GL, GL + 512, sv["gfb"], do, sv["s_f"], sv["s_b"], "gla_bwd")
    dz_gla = _gla_assemble(dqk_f, dqk_b, dv_f, dv_b, dgr, "gla_assemble")
    dpre, g["gla_gbias"] = _gla_gates_bwd(z, LR, w["gla_wg"], w["gla_gbias"], dg_f, dg_b, "gla_gates_bwd")
    g["gla_wg"] = _mm([(_V(z, LR, 128), dpre)], "tn", F32, "gla_wg_dw", tm=128, tn=512)
    dz_lr = _mm([(dpre, w["gla_wg"])], "nt", BF16, "gla_wg_dz", tm=512, tn=128)
    own = lambda n: _Ride(n, g[n], at_layer, recv[n], at_layer)
    (dq, dk, dv), got = _mla_attn_bwd(sv["q"], sv["k"], sv["v"], dys[2], "mla_attn_bwd",
                                      ride=pick("w_in") + [own("ffn2_w1"), own("ffn2_w3")])
    recv.update(got)
    dqf, dkv, dz_kr, g["mla_gq"], g["mla_gk"] = _mla_prep_bwd(
        z, KR, sv["cqn"], sv["ckvn"], w["mla_wuq"], w["mla_w_ukv"], w["mla_gq"], w["mla_gk"], cos, sg, dq, dk, dv, "mla_prep_bwd")
    g["mla_wuq"] = _mm([(sv["cqn"], dqf)], "tn", F32, "mla_wuq_dw", tm=256, tn=512)
    g["mla_w_ukv"] = _mm([(sv["ckvn"], dkv)], "tn", F32, "mla_wukv_dw", tm=256, tn=512)
    dcqn = _mm([(dqf, w["mla_wuq"])], "nt", F32, "mla_wuq_dx", tm=512, tn=256)
    dckvn = _mm([(dkv, w["mla_w_ukv"])], "nt", F32, "mla_wukv_dx", tm=512, tn=256)
    dz_cq, dg_cq = _rms_bwd(_V(z, ML, MLA_RANK), w["mla_cq_norm"], dcqn, "mla_cq_rms_bwd", out_dtype=BF16)
    dz_ckv, dg_ckv = _rms_bwd(_V(z, ML + MLA_RANK, MLA_RANK), w["mla_ckv_norm"], dckvn, "mla_ckv_rms_bwd", out_dtype=BF16)
    g["mla_cq_norm"], g["mla_ckv_norm"] = dg_cq[0:1], dg_ckv[0:1]
    segs = ((dgates, 0, 3 * D), (dz_na, NA, 1536), (dz_gla, GL, 1536), (dz_cq, ML, MLA_RANK), (dz_ckv, ML + MLA_RANK, MLA_RANK),
            (dz_lr, LR, 128), (dz_kr, KR, 128))
    dh, got = _mm([(dz, _V(w["w_in"], c0, wd)) for dz, c0, wd in segs], "nt", F32, "w_in_dx", tm=512, tn=512,
                  ride=[own("ffn2_w2")])
    recv.update(got)
    g["w_in"] = jnp.concatenate(
        [_mm([(sv["h"], dz)], "tn", F32, f"w_in_dw{i}", tm=D, tn=_tile(wd, 256)) for i, (dz, _, wd) in enumerate(segs)], axis=1)
    dx1, dg_mix = _rms_bwd(sv["x1"], w["mix_norm"], dh, "mix_rms_bwd", dres=dx2)
    g["mix_norm"] = dg_mix[0:1]
    ride = [_Ride(n, _split4(g[n], SHARDED[n] - 1), lambda chip: (chip,), recv[n], at_layer)
            for n in ("w_out", "w_br_na", "w_br_gla", "w_br_mla")]
    dx, g["ffn1_norm"], (g["ffn1_w1"], g["ffn1_w3"], g["ffn1_w2"]), got = _ffn_bwd(
        dx1, sv["x"], w["ffn1_norm"], w["ffn1_w1"], w["ffn1_w3"], w["ffn1_w2"], sv["f1"], "ffn1",
        (bufs["ffn1_w1"], bufs["ffn1_w3"], bufs["ffn1_w2"]), layer, ride_down=ride)
    recv.update(got)
    late = dict(w_in=_split4(_win_unlayout(g["w_in"], D), 1), mla_w_uq=_split4(_uq_unlayout(g["mla_wuq"]), 1),
                mla_w_ukv=_split4(g["mla_w_ukv"], 1), gla_gf_up=_split4(g["gla_wg"][0:GLA_RANK, 0:256], 1),
                gla_gb_up=_split4(g["gla_wg"][GLA_RANK:2 * GLA_RANK, 256:512], 1))
    return dx, g, late, recv


def _head_fold(width, period, lo=0):
    f = np.zeros((width, 128), np.float32)
    f[np.arange(width), lo + np.arange(width) % period] = 1.0
    return f


def kernel(x, ffn1_norm, ffn1_w1, ffn1_w3, ffn1_w2, mix_norm, w_in, na_q_norm, na_k_norm, na_rpb, gla_gf_up, gla_gf_bias,
           gla_gb_up, gla_gb_bias, gla_out_norm, mla_cq_norm, mla_ckv_norm, mla_w_uq, mla_w_ukv, mla_q_norm, mla_k_norm,
           w_br_na, w_br_gla, w_br_mla, w_out, ffn2_norm, ffn2_w1, ffn2_w3, ffn2_w2, loss_target, m_ffn1_norm, m_ffn1_w1,
           m_ffn1_w3, m_ffn1_w2, m_mix_norm, m_w_in, m_na_q_norm, m_na_k_norm, m_na_rpb, m_gla_gf_up, m_gla_gf_bias,
           m_gla_gb_up, m_gla_gb_bias, m_gla_out_norm, m_mla_cq_norm, m_mla_ckv_norm, m_mla_w_uq, m_mla_w_ukv,
           m_mla_q_norm, m_mla_k_norm, m_w_br_na, m_w_br_gla, m_w_br_mla, m_w_out, m_ffn2_norm, m_ffn2_w1, m_ffn2_w3,
           m_ffn2_w2, v_ffn1_norm, v_ffn1_w1, v_ffn1_w3, v_ffn1_w2, v_mix_norm, v_w_in, v_na_q_norm, v_na_k_norm,
           v_na_rpb, v_gla_gf_up, v_gla_gf_bias, v_gla_gb_up, v_gla_gb_bias, v_gla_out_norm, v_mla_cq_norm,
           v_mla_ckv_norm, v_mla_w_uq, v_mla_w_ukv, v_mla_q_norm, v_mla_k_norm, v_w_br_na, v_w_br_gla, v_w_br_mla,
           v_w_out, v_ffn2_norm, v_ffn2_w1, v_ffn2_w3, v_ffn2_w2):
    given = dict(locals())
    wts = {n: given[n] for n in W_NAMES}
    mom = {n: given["m_" + n] for n in W_NAMES}
    var = {n: given["v_" + n] for n in W_NAMES}
    xs, target = x[0], loss_target[0]
    S, D = xs.shape
    L = ffn1_norm.shape[0]

    sh_names = tuple(SHARDED)
    LATE = ("ffn2_w1", "ffn2_w3", "ffn2_w2")
    HEAVY = ("ffn1_w1", "ffn1_w3", "ffn1_w2", "w_in")
    LIGHT = tuple(n for n in sh_names if n not in LATE + HEAVY)
    shard_shape = lambda n: tuple(wts[n].shape[1:])

    def gather_items(names, l):
        return [_Ride(n, wts[n][l].astype(BF16), lambda chip: (), lax.empty((4,) + shard_shape(n), BF16), lambda chip: (chip,),
                      halves=n in HEAVY + LATE) for n in names]

    cols = lambda p: jnp.concatenate([p[j] for j in range(4)], axis=-1)

    def layer_weights(gl, l):
        r1 = lambda a: a[l][None]
        wg = jnp.zeros((128, 2 * GLA_HEADS * GLA_DK), BF16)
        wg = wg.at[0:GLA_RANK, 0:256].set(cols(gl["gla_gf_up"])).at[GLA_RANK:2 * GLA_RANK, 256:512].set(cols(gl["gla_gb_up"]))
        return dict(
            ffn1_norm=r1(ffn1_norm), ffn1_w1=gl["ffn1_w1"], ffn1_w3=gl["ffn1_w3"], ffn1_w2=gl["ffn1_w2"],
            mix_norm=r1(mix_norm), w_in=_win_layout(cols(gl["w_in"]), D),
            na_gq=jnp.tile(na_q_norm[l], NA_HEADS)[None], na_gk=jnp.tile(na_k_norm[l], NA_HEADS)[None], na_rpb=na_rpb[l],
            gla_wg=wg, gla_gbias=jnp.concatenate([gla_gf_bias[l], gla_gb_bias[l]])[None], gla_out_norm=r1(gla_out_norm),
            mla_cq_norm=r1(mla_cq_norm), mla_ckv_norm=r1(mla_ckv_norm), mla_wuq=_uq_layout(cols(gl["mla_w_uq"])),
            mla_w_ukv=cols(gl["mla_w_ukv"]), mla_gq=_slot_layout(mla_q_norm[l])[None], mla_gk=_slot_layout(mla_k_norm[l])[None],
            w_br_na=cols(gl["w_br_na"]), w_br_gla=cols(gl["w_br_gla"]), w_br_mla=cols(gl["w_br_mla"]),
            w_out=gl["w_out"].reshape(D, D), ffn2_norm=r1(ffn2_norm))

    half = MLA_ROPE // 2
    inv = ROPE_THETA ** (-jnp.arange(half, dtype=F32) / half)
    ang = jnp.arange(S, dtype=F32)[:, None] * inv[None, :]
    cos = jnp.tile(jnp.cos(ang), (1, 4))
    sg = jnp.concatenate([-jnp.sin(ang), -jnp.sin(ang), jnp.sin(ang), jnp.sin(ang)], axis=1)

    arrived = _exchange(gather_items(HEAVY + LIGHT, 0), "weights_all_gather")
    xc, saved, layers = xs, [], []
    for l in range(L):
        w = layer_weights(arrived, l)
        rides = {"ffn1_up": gather_items(("ffn2_w1", "ffn2_w3"), l), "w_in": gather_items(("ffn2_w2",), l)}
        if l + 1 < L:
            rides["w_in"] += gather_items(LIGHT, l + 1)
            rides["na_attn"] = gather_items(("ffn1_w1", "ffn1_w3"), l + 1)
            rides["gla_fwd"] = gather_items(("ffn1_w2",), l + 1)
            rides["mla_attn"] = gather_items(("w_in",), l + 1)
        xc, sv, arrived = _layer_fwd(xc, w, cos, sg, rides)
        saved.append(sv)
        layers.append({**w, **{n: arrived[n] for n in LATE}})
    dy, loss_part = _loss_head(xc, target, "loss_head")

    bufs = {n: lax.empty((4, L) + shard_shape(n), BF16) for n in FFN_W}
    recv = {n: lax.empty((4, L) + shard_shape(n), BF16) for n in sh_names}
    dx, g, prev = dy, [None] * L, {}
    for l in reversed(range(L)):
        dx, g[l], late, recv = _layer_bwd(dx, layers[l], saved[l], cos, sg, bufs, recv, l, prev)
        bufs = {n: g[l][n] for n in FFN_W}
        at_l = functools.partial(lambda chip, l: (chip, l), l=l)
        prev = {n: _Ride(n, bufs[n], at_l, recv[n], at_l) for n in ("ffn1_w1", "ffn1_w3", "ffn1_w2")}
        prev.update({n: _Ride(n, late[n], lambda chip: (chip,), recv[n], at_l) for n in late})

    stk = lambda n: jnp.stack([g[l][n] for l in range(L)])
    gs = {n: stk(n)[:, 0] for n in ("ffn1_norm", "mix_norm", "mla_cq_norm", "mla_ckv_norm", "ffn2_norm")}
    gs["na_q_norm"] = _fold(stk("na_gq"), "na_gq_fold", _head_fold(NA_W, NA_DH))[:, :NA_DH]
    gs["na_k_norm"] = _fold(stk("na_gk"), "na_gk_fold", _head_fold(NA_W, NA_DH))[:, :NA_DH]
    gs["na_rpb"] = stk("na_rpb")
    gbias = _fold(stk("gla_gbias"), "gla_gbias_fold")
    gs["gla_gf_bias"], gs["gla_gb_bias"] = gbias[:, :256], gbias[:, 256:]
    gs["gla_out_norm"] = _fold(stk("gla_out_norm"), "gla_out_norm_fold")
    gs["mla_q_norm"] = _slot_unlayout(_fold(stk("mla_gq"), "mla_gq_fold"))
    gs["mla_k_norm"] = _slot_unlayout(_fold(stk("mla_gk"), "mla_gk_fold"))

    as2d = lambda a: a.reshape(-1, a.shape[-1])
    gsh, upd = {}, {}
    DONE = ("ffn2_w1", "ffn2_w3", "ffn2_w2", "w_out", "w_br_na", "w_br_gla", "w_br_mla")

    def reduce_and_update(names, with_exchange, with_update):
        got = {}
        mine = [_sum4(recv[n].reshape(4, -1, recv[n].shape[-1]), "grads_chip_sum_" + n) for n in names]
        other, arrived = _sibling_exchange(mine, "grads_sibling_exchange", ride=with_exchange)
        got.update(arrived)
        for n, p, q in zip(names, mine, other):
            outs, arrived = _adamw(as2d(wts[n]), p, as2d(mom[n]), as2d(var[n]), "adamw_" + n, q=q, ride=with_update.get(n, []))
            got.update(arrived)
            outs = [o.reshape(wts[n].shape) for o in outs]
            gsh[n], upd[n] = outs[0], outs[1:]
        return got

    recv.update(reduce_and_update(
        DONE, [prev[n] for n in ("w_in", "mla_w_uq", "mla_w_ukv", "gla_gf_up", "gla_gb_up")],
        {"ffn2_w1": [prev["ffn1_w1"]], "ffn2_w3": [prev["ffn1_w3"]], "ffn2_w2": [prev["ffn1_w2"]]}))
    reduce_and_update(tuple(n for n in sh_names if n not in DONE), [], {})

    small_shapes = [wts[n].shape[1:] for n in REPLICATED]
    n_small = sum(int(np.prod(s)) for s in small_shapes) * L
    flat = jnp.concatenate([gs[n].reshape(-1) for n in REPLICATED] + [loss_part.reshape(-1)])
    pad = -flat.shape[0] % 1024
    red = _allreduce_small(jnp.pad(flat, (0, pad)).reshape(-1, 128), "small_all_reduce").reshape(-1)
    loss = jnp.sum(red[n_small:n_small + 1024])
    off = 0
    for n, s in zip(REPLICATED, small_shapes):
        cnt = int(np.prod(s)) * L
        gsh[n] = red[off:off + cnt].reshape((L,) + tuple(s))
        off += cnt

    pk = lambda d: jnp.pad(jnp.concatenate([d[n].reshape(-1) for n in REPLICATED]), (0, -n_small % 1024)).reshape(-1, 128)
    small = _adamw(pk(wts), pk(gsh), pk(mom), pk(var), "adamw_replicated")
    off = 0
    for n, s in zip(REPLICATED, small_shapes):
        cnt = int(np.prod(s)) * L
        upd[n] = [o.reshape(-1)[off:off + cnt].reshape((L,) + tuple(s)) for o in small]
        off += cnt

    return (loss, dx[None], *[gsh[n] for n in W_NAMES], *[upd[n][0] for n in W_NAMES], *[upd[n][1] for n in W_NAMES],
            *[upd[n][2] for n in W_NAMES])
```

```python
import functools
import math

import numpy as np
import jax
import jax.numpy as jnp
from jax import lax
from jax.experimental import pallas as pl
from jax.experimental.pallas import tpu as pltpu

F32 = jnp.float32
BF16 = jnp.bfloat16
HI = lax.Precision.HIGHEST
MESH = pl.DeviceIdType.MESH

EPS = 1e-6
GRID_W = 64
NA_HEADS, NA_DH, NA_WIN_R, NA_WIN_C = 8, 64, 8, 16
NA_W = NA_HEADS * NA_DH
GLA_HEADS, GLA_DK, GLA_DV, GLA_RANK, GLA_TAU, GLA_CHUNK = 4, 64, 128, 16, 16.0, 64
MLA_HEADS, MLA_RANK, MLA_NOPE, MLA_ROPE, MLA_V = 4, 256, 128, 64, 128
MLA_QK = MLA_NOPE + MLA_ROPE
MLA_SLOT = 256
MLA_QSCALE = MLA_QK ** -0.5 * math.log2(math.e)
ROPE_THETA = 10000.0
ADAM_LR, ADAM_B1, ADAM_B2, ADAM_EPS, ADAM_WD, ADAM_STEP = 0.001, 0.9, 0.999, 1e-08, 0.01, 10

V7X_VMEM_BYTES = 64 * 2**20
VMEM_LIMIT = V7X_VMEM_BYTES - 12 * 2**20
NEG = -1e30

O_GQ, O_GFL, O_CQ, O_KR, O_GATES = 1536, 3072, 3104, 3616, 3680


_ANY = pl.BlockSpec(memory_space=pl.ANY)


def _cparams(*sem):
    return pltpu.CompilerParams(dimension_semantics=sem, vmem_limit_bytes=VMEM_LIMIT)


class _V:
    def __init__(self, arr, c0=0, w=None, lead=()):
        self.arr, self.c0, self.lead = arr, c0, tuple(lead)
        assert arr.ndim == 2 + len(self.lead), (arr.shape, lead)
        self.w = arr.shape[-1] if w is None else w

    @property
    def rows(self):
        return self.arr.shape[-2]

    def spec(self, br, bc, rfn, cfn):
        assert self.c0 % bc == 0 and self.w % bc == 0, (self.c0, self.w, bc)
        off, lead = self.c0 // bc, self.lead

        def index(*g):
            return tuple(g[0] if e == "b" else e for e in lead) + (rfn(*g), off + cfn(*g))

        return pl.BlockSpec((None,) * len(lead) + (br, bc), index)


def _v(x):
    return x if isinstance(x, _V) else _V(x)


_DN = {"nn": (((1,), (0,)), ((), ())), "nt": (((1,), (1,)), ((), ())), "tn": (((0,), (0,)), ((), ()))}


def _dot(a, b, mode="nn", prec=None):
    return lax.dot_general(a, b, _DN[mode], preferred_element_type=F32, precision=prec)


def _tile(n, cap):
    if n <= cap:
        return n
    for t in range(cap - cap % 128, 0, -128):
        if n % t == 0:
            return t
    return n


def _mm(pairs, mode, out_dtype, name, *, tm, tn, res=None, scale=None, batch=1, into=None, ride=None):
    pairs = [(_v(a), _v(b)) for a, b in pairs]
    a0, b0 = pairs[0]
    M = a0.w if mode == "tn" else a0.rows
    N = b0.rows if mode == "nt" else b0.w
    tm, tn = _tile(M, tm), _tile(N, tn)
    assert M % tm == 0 and N % tn == 0, (name, M, N, tm, tn)
    n = len(pairs)

    def body(*refs):
        o_ref = refs[-1]
        acc = None
        for i in range(n):
            d = _dot(refs[2 * i][...].astype(BF16), refs[2 * i + 1][...].astype(BF16), mode)
            acc = d if acc is None else acc + d
        if scale is not None:
            acc = acc * scale
        if res is not None:
            acc = acc + refs[2 * n][...]
        o_ref[...] = acc.astype(o_ref.dtype)

    zero = lambda b, i, j: 0
    row = lambda b, i, j: i
    col = lambda b, i, j: j
    in_specs, args = [], []
    for a, b in pairs:
        in_specs.append(a.spec(a.rows, tm, zero, row) if mode == "tn" else a.spec(tm, a.w, row, zero))
        in_specs.append(b.spec(tn, b.w, col, zero) if mode == "nt" else b.spec(b.rows, tn, zero, col))
        args += [a.arr, b.arr]
    if res is not None:
        in_specs.append(pl.BlockSpec((tm, tn), lambda b, i, j: (i, j)))
        args.append(res)
    aliases = {}
    if into is None:
        out = jax.ShapeDtypeStruct(((batch,) if batch > 1 else ()) + (M, N), out_dtype)
        out_view = _V(out, lead=("b",) if batch > 1 else ())
    else:
        buf, lead = into
        assert buf.shape[-2:] == (M, N) and buf.dtype == out_dtype, (name, buf.shape, M, N)
        out = jax.ShapeDtypeStruct(buf.shape, buf.dtype)
        out_view = _V(out, lead=lead)
        aliases = {len(args): 0}
        in_specs.append(_ANY)
        args.append(buf)
    (res,), got = _ride_call(
        body, ride, name=name, grid=(batch, M // tm, N // tn), in_specs=in_specs, out_specs=[out_view.spec(tm, tn, row, col)],
        out_shape=[out], aliases=aliases, args=args, semantics=("parallel", "parallel", "parallel"))
    return res if ride is None else (res, got)


def _rms_fwd(x, g, name, tm=512):
    x = _v(x)
    S, D = x.rows, x.w
    tm = min(tm, S)

    def body(x_ref, g_ref, o_ref):
        xv = x_ref[...]
        y = xv * lax.rsqrt(jnp.mean(xv * xv, axis=-1, keepdims=True) + EPS)
        o_ref[...] = (y * g_ref[...]).astype(o_ref.dtype)

    return pl.pallas_call(
        body, name=name, grid=(S // tm,),
        in_specs=[x.spec(tm, D, lambda i: i, lambda i: 0), pl.BlockSpec((1, D), lambda i: (0, 0))],
        out_specs=pl.BlockSpec((tm, D), lambda i: (i, 0)),
        out_shape=jax.ShapeDtypeStruct((S, D), BF16), compiler_params=_cparams("parallel"),
    )(x.arr, g)


def _rms_bwd(x, g, dh, name, dres=None, out_dtype=F32, tm=512):
    x = _v(x)
    S, D = x.rows, x.w
    tm = min(tm, S)

    def body(*refs):
        if dres is None:
            x_ref, g_ref, dh_ref, dx_ref, dg_ref = refs
        else:
            x_ref, g_ref, dh_ref, dr_ref, dx_ref, dg_ref = refs
        xv = x_ref[...]
        rstd = lax.rsqrt(jnp.mean(xv * xv, axis=-1, keepdims=True) + EPS)
        xhat = xv * rstd
        dhv = dh_ref[...].astype(F32)
        dxhat = dhv * g_ref[...]
        dx = rstd * (dxhat - xhat * jnp.mean(dxhat * xhat, axis=-1, keepdims=True))
        if dres is not None:
            dx = dx + dr_ref[...]
        dx_ref[...] = dx.astype(dx_ref.dtype)

        @pl.when(pl.program_id(0) == 0)
        def _():
            dg_ref[...] = jnp.zeros_like(dg_ref)

        dg_ref[0:1, :] += jnp.sum(dhv * xhat, axis=0, keepdims=True)

    in_specs = [x.spec(tm, D, lambda i: i, lambda i: 0), pl.BlockSpec((1, D), lambda i: (0, 0)),
                pl.BlockSpec((tm, D), lambda i: (i, 0))]
    args = [x.arr, g, dh]
    if dres is not None:
        in_specs.append(pl.BlockSpec((tm, D), lambda i: (i, 0)))
        args.append(dres)
    return pl.pallas_call(
        body, name=name, grid=(S // tm,), in_specs=in_specs,
        out_specs=[pl.BlockSpec((tm, D), lambda i: (i, 0)), pl.BlockSpec((8, D), lambda i: (0, 0))],
        out_shape=[jax.ShapeDtypeStruct((S, D), out_dtype), jax.ShapeDtypeStruct((8, D), F32)],
        compiler_params=_cparams("arbitrary"),
    )(*args)


FFN_SUB = 256


def _one_ahead(n, matmuls, rest):
    res = matmuls(0)
    for i in range(1, n):
        nxt = matmuls(i)
        rest(i - 1, res)
        res = nxt
    rest(n - 1, res)


def _ffn_up(h, w1, w3, name, tm=1024, ride=None):
    S, D = h.shape
    NC, _, F4 = w1.shape
    tm = min(tm, S)
    sub = math.gcd(FFN_SUB, tm)
    rows = lambda i: slice(i * sub, (i + 1) * sub)

    def body(h_ref, w1_ref, w3_ref, a_ref, b_ref, u_ref):
        def matmuls(i):
            hv = h_ref[rows(i), :]
            return _dot(hv, w1_ref[...]), _dot(hv, w3_ref[...])

        def rest(i, ab):
            a, b = ab
            a_ref[rows(i), :] = a.astype(BF16)
            b_ref[rows(i), :] = b.astype(BF16)
            u_ref[rows(i), :] = (a * jax.nn.sigmoid(a) * b).astype(BF16)

        _one_ahead(tm // sub, matmuls, rest)

    blk = pl.BlockSpec((None, tm, F4), lambda i, j: (j, i, 0))
    wblk = pl.BlockSpec((None, D, F4), lambda i, j: (j, 0, 0))
    return _ride_call(
        body, ride, name=name, grid=(S // tm, NC), in_specs=[pl.BlockSpec((tm, D), lambda i, j: (i, 0)), wblk, wblk],
        out_specs=[blk, blk, blk], out_shape=[jax.ShapeDtypeStruct((NC, S, F4), BF16)] * 3, args=(h, w1, w3),
        semantics=("parallel", "parallel"))


def _ffn_down_bwd(dxo, w2, a, b, name, tm=1024, ride=None):
    S, D = dxo.shape
    NC, F4, _ = w2.shape
    tm = min(tm, S)
    sub = math.gcd(FFN_SUB, tm)
    rows = lambda i: slice(i * sub, (i + 1) * sub)

    def body(dx_ref, w2_ref, a_ref, b_ref, da_ref, db_ref):
        def matmuls(i):
            return _dot(dx_ref[rows(i), :].astype(BF16), w2_ref[...], "nt")

        def rest(i, du):
            du = du * 0.5
            av = a_ref[rows(i), :].astype(F32)
            sig = jax.nn.sigmoid(av)
            da_ref[rows(i), :] = (du * b_ref[rows(i), :].astype(F32) * (sig * (1.0 + av * (1.0 - sig)))).astype(BF16)
            db_ref[rows(i), :] = (du * av * sig).astype(BF16)

        _one_ahead(tm // sub, matmuls, rest)

    blk = pl.BlockSpec((None, tm, F4), lambda i, j: (j, i, 0))
    return _ride_call(
        body, ride, name=name, grid=(S // tm, NC),
        in_specs=[pl.BlockSpec((tm, D), lambda i, j: (i, 0)), pl.BlockSpec((None, F4, D), lambda i, j: (j, 0, 0)), blk, blk],
        out_specs=[blk, blk], out_shape=[jax.ShapeDtypeStruct((NC, S, F4), BF16)] * 2, args=(dxo, w2, a, b),
        semantics=("parallel", "parallel"))


def _ffn_fwd(x, g, w1, w3, w2, tag, ride=None):
    h = _rms_fwd(x, g, f"{tag}_rms")
    (a, b, u), got = _ffn_up(h, w1, w3, f"{tag}_up", ride=ride)
    nc = w2.shape[0]
    y = _mm([(_V(u, lead=(j,)), _V(w2, lead=(j,))) for j in range(nc)], "nn", F32, f"{tag}_down", tm=512, tn=1024, res=x, scale=0.5)
    return y, (h, a, b, u), got


def _ffn_bwd(dxo, x, g, w1, w3, w2, saved, tag, bufs, layer, ride_down=None, ride_dh=None):
    h, a, b, u = saved
    nc, D, F4 = w1.shape
    (da, db), got = _ffn_down_bwd(dxo, w2, a, b, f"{tag}_down_bwd", ride=ride_down)
    into = lambda k: (bufs[k], ("b", layer))
    dw2 = _mm([(_V(u, lead=("b",)), dxo)], "tn", BF16, f"{tag}_dw2", tm=F4, tn=512, scale=0.5, batch=nc, into=into(2))
    dw1 = _mm([(h, _V(da, lead=("b",)))], "tn", BF16, f"{tag}_dw1", tm=D, tn=F4, batch=nc, into=into(0))
    dw3 = _mm([(h, _V(db, lead=("b",)))], "tn", BF16, f"{tag}_dw3", tm=D, tn=F4, batch=nc, into=into(1))
    pairs = [(_V(da, lead=(j,)), _V(w1, lead=(j,))) for j in range(nc)] + [(_V(db, lead=(j,)), _V(w3, lead=(j,))) for j in range(nc)]
    dh, got_dh = _mm(pairs, "nt", F32, f"{tag}_dh", tm=512, tn=512, ride=ride_dh or [])
    dx, dg = _rms_bwd(x, g, dh, f"{tag}_rms_bwd", dres=dxo)
    return dx, dg[0:1], (dw1, dw3, dw2), {**got, **got_dh}


def _iota(shape, dim):
    return lax.broadcasted_iota(jnp.int32, shape, dim)


def _head_block_ones(n, shift):
    return jnp.where((_iota((n, n), 0) >> shift) == (_iota((n, n), 1) >> shift), 1.0, 0.0).astype(BF16)


def _dot_split(x, ones01):
    hi = x.astype(BF16)
    lo = (x - hi.astype(F32)).astype(BF16)
    return _dot(hi, ones01) + _dot(lo, ones01)


def _lane_mask(width, lo, size):
    l = _iota((1, width), 1)
    return jnp.where((l >= lo) & (l < lo + size), 1.0, 0.0).astype(F32)


def _acc_rows(acc_ref, val, first):
    r = val.shape[0]
    part = jnp.sum(val.reshape(r // 8, 8, val.shape[1]), axis=0)

    @pl.when(first)
    def _():
        acc_ref[...] = part

    @pl.when(jnp.logical_not(first))
    def _():
        acc_ref[...] += part


_FLIPS = ((1, 0), (0, 1), (1, 1))


class _Ride:
    def __init__(self, name, src, src_at, dst, dst_at, halves=False):
        self.name, self.src, self.src_at, self.dst, self.dst_at, self.halves = name, src, src_at, dst, dst_at, halves
        assert not halves or (src.ndim == 2 and src.shape[0] % 32 == 0), (name, src.shape)


_RIDE_SEMS = lambda n: [pltpu.SemaphoreType.DMA((6, n)), pltpu.SemaphoreType.DMA((6, n)), pltpu.SemaphoreType.DMA((n,))]


def _ride_ops(ride, srcs, dsts, send_sems, recv_sems, local_sems):
    x, y, c = lax.axis_index("x"), lax.axis_index("y"), lax.axis_index("c")
    me = 2 * x + y
    at = lambda ref, idx: ref.at[idx] if idx else ref
    local, sends, arrivals, passes = [], [], [], []
    for t, it in enumerate(ride):
        local.append(pltpu.make_async_copy(at(srcs[t], it.src_at(me)), at(dsts[t], it.dst_at(me)), local_sems.at[t]))
    for r, (fx, fy) in enumerate(_FLIPS):
        px, py = (1 - x) if fx else x, (1 - y) if fy else y
        peer = 2 * px + py
        for t, it in enumerate(ride):
            if it.halves:
                h = it.src.shape[0] // 2
                mine = pl.ds(pl.multiple_of(c * h, 16), h)
                theirs = pl.ds(pl.multiple_of((1 - c) * h, 16), h)
                far = dict(send_sem=send_sems.at[r, t], recv_sem=recv_sems.at[r, t], device_id=(px, py, c), device_id_type=MESH)
                near = dict(send_sem=send_sems.at[3 + r, t], recv_sem=recv_sems.at[3 + r, t], device_id=(x, y, 1 - c),
                            device_id_type=MESH)
                sends.append(pltpu.make_async_remote_copy(src_ref=srcs[t].at[mine], dst_ref=dsts[t].at[me, mine], **far))
                arrivals.append(pltpu.make_async_remote_copy(src_ref=srcs[t].at[mine], dst_ref=dsts[t].at[peer, mine], **far))
                passes.append((pltpu.make_async_remote_copy(src_ref=dsts[t].at[peer, mine], dst_ref=dsts[t].at[peer, mine], **near),
                               pltpu.make_async_remote_copy(src_ref=dsts[t].at[peer, theirs], dst_ref=dsts[t].at[peer, theirs], **near)))
            else:
                far = dict(src_ref=at(srcs[t], it.src_at(peer)), send_sem=send_sems.at[r, t], recv_sem=recv_sems.at[r, t],
                           device_id=(px, py, c), device_id_type=MESH)
                sends.append(pltpu.make_async_remote_copy(dst_ref=at(dsts[t], it.dst_at(me)), **far))
                arrivals.append(pltpu.make_async_remote_copy(dst_ref=at(dsts[t], it.dst_at(peer)), **far))
                passes.append(None)

    def start():
        for cp in local + sends:
            cp.start()

    def finish():
        for cp, arrival, onward in zip(sends, arrivals, passes):
            cp.wait_send()
            arrival.wait_recv()
            if onward is not None:
                onward[0].start()
        for onward in passes:
            if onward is not None:
                onward[0].wait_send()
                onward[1].wait_recv()
        for cp in local:
            cp.wait()

    return start, finish


def _grid_edges(*ns):
    def edges():
        first = last = None
        for d, n in enumerate(ns):
            i = pl.program_id(d)
            f, l = i == 0, i == n - 1
            first = f if first is None else jnp.logical_and(first, f)
            last = l if last is None else jnp.logical_and(last, l)
        return first, last
    return edges


def _ride_call(body, ride, *, name, grid, in_specs, out_specs, out_shape, args, scratch_shapes=(), semantics=(), aliases=None):
    scratch_shapes, aliases = list(scratch_shapes), dict(aliases or {})
    if not ride:
        outs = pl.pallas_call(body, name=name, grid=grid, in_specs=in_specs, out_specs=out_specs, out_shape=out_shape,
                              scratch_shapes=scratch_shapes, input_output_aliases=aliases,
                              compiler_params=_cparams(*semantics))(*args)
        return outs, {}
    n_in, n_out, n_sc, n = len(in_specs), len(out_specs), len(scratch_shapes), len(ride)
    edges = _grid_edges(*grid)

    def wrapped(*refs):
        ins, srcs = refs[:n_in], refs[n_in:n_in + n]
        o0 = n_in + 2 * n
        outs, dsts = refs[o0:o0 + n_out], refs[o0 + n_out:o0 + n_out + n]
        scratch = refs[o0 + n_out + n:o0 + n_out + n + n_sc]
        start, finish = _ride_ops(ride, srcs, dsts, *refs[o0 + n_out + n + n_sc:])
        first, last = edges()
        pl.when(first)(start)
        body(*ins, *outs, *scratch)
        pl.when(last)(finish)

    aliases.update({n_in + n + t: n_out + t for t in range(n)})
    res = pl.pallas_call(
        wrapped, name=name, grid=grid, in_specs=list(in_specs) + [_ANY] * (2 * n), out_specs=list(out_specs) + [_ANY] * n,
        out_shape=list(out_shape) + [jax.ShapeDtypeStruct(it.dst.shape, it.dst.dtype) for it in ride],
        input_output_aliases=aliases, scratch_shapes=scratch_shapes + _RIDE_SEMS(n),
        compiler_params=_cparams(*(["arbitrary"] * len(grid))),
    )(*args, *[it.src for it in ride], *[it.dst for it in ride])
    return res[:n_out], {it.name: o for it, o in zip(ride, res[n_out:])}


def _exchange(ride, name):
    n = len(ride)

    def body(*refs):
        start, finish = _ride_ops(ride, refs[:n], refs[2 * n:3 * n], *refs[3 * n:])
        start()
        finish()

    res = pl.pallas_call(
        body, name=name, in_specs=[_ANY] * (2 * n), out_specs=[_ANY] * n,
        out_shape=[jax.ShapeDtypeStruct(it.dst.shape, it.dst.dtype) for it in ride],
        input_output_aliases={n + t: t for t in range(n)}, scratch_shapes=_RIDE_SEMS(n),
    )(*[it.src for it in ride], *[it.dst for it in ride])
    return {it.name: o for it, o in zip(ride, res)}


def _na_prep(z, c0, gq, gk, name, tm=512):
    S = z.shape[0]
    tm = min(tm, S)
    zv = _V(z, c0, 3 * NA_W)

    def body(z_ref, gq_ref, gk_ref, q_ref, k_ref, v_ref):
        bd = _head_block_ones(NA_W, 6)

        def norm(xv, gv):
            ms = _dot_split(xv * xv, bd) * (1.0 / NA_DH)
            return xv * lax.rsqrt(ms + EPS) * gv

        q_ref[...] = (norm(z_ref[:, 0:NA_W], gq_ref[...]) * (NA_DH ** -0.5)).astype(BF16)
        k_ref[...] = norm(z_ref[:, NA_W:2 * NA_W], gk_ref[...]).astype(BF16)
        v_ref[...] = z_ref[:, 2 * NA_W:3 * NA_W].astype(BF16)

    blk = pl.BlockSpec((tm, NA_W), lambda i: (i, 0))
    gspec = pl.BlockSpec((1, NA_W), lambda i: (0, 0))
    return pl.pallas_call(
        body, name=name, grid=(S // tm,),
        in_specs=[zv.spec(tm, 3 * NA_W, lambda i: i, lambda i: 0), gspec, gspec],
        out_specs=[blk, blk, blk], out_shape=[jax.ShapeDtypeStruct((S, NA_W), BF16)] * 3,
        compiler_params=_cparams("parallel"),
    )(z, gq, gk)


def _na_prep_bwd(z, c0, gq, gk, dqn, dkn, dv, name, tm=512):
    S = z.shape[0]
    tm = min(tm, S)
    zv = _V(z, c0, 3 * NA_W)

    def body(z_ref, gq_ref, gk_ref, dq_ref, dk_ref, dv_ref, dz_ref, dgq_ref, dgk_ref):
        bd = _head_block_ones(NA_W, 6)
        first = pl.program_id(0) == 0

        def norm_bwd(xv, gv, dy, dg_ref):
            ms = _dot_split(xv * xv, bd) * (1.0 / NA_DH)
            rstd = lax.rsqrt(ms + EPS)
            xhat = xv * rstd
            dxhat = dy * gv
            proj = _dot_split(dxhat * xhat, bd) * (1.0 / NA_DH)
            _acc_rows(dg_ref, dy * xhat, first)
            return rstd * (dxhat - xhat * proj)

        dz_ref[:, 0:NA_W] = norm_bwd(z_ref[:, 0:NA_W], gq_ref[...], dq_ref[...] * (NA_DH ** -0.5), dgq_ref).astype(BF16)
        dz_ref[:, NA_W:2 * NA_W] = norm_bwd(z_ref[:, NA_W:2 * NA_W], gk_ref[...], dk_ref[...], dgk_ref).astype(BF16)
        dz_ref[:, 2 * NA_W:3 * NA_W] = dv_ref[...].astype(BF16)

    blk = pl.BlockSpec((tm, NA_W), lambda i: (i, 0))
    gspec = pl.BlockSpec((1, NA_W), lambda i: (0, 0))
    acc = pl.BlockSpec((8, NA_W), lambda i: (0, 0))
    return pl.pallas_call(
        body, name=name, grid=(S // tm,),
        in_specs=[zv.spec(tm, 3 * NA_W, lambda i: i, lambda i: 0), gspec, gspec, blk, blk, blk],
        out_specs=[pl.BlockSpec((tm, 3 * NA_W), lambda i: (i, 0)), acc, acc],
        out_shape=[jax.ShapeDtypeStruct((S, 3 * NA_W), BF16), jax.ShapeDtypeStruct((8, NA_W), F32),
                   jax.ShapeDtypeStruct((8, NA_W), F32)],
        compiler_params=_cparams("arbitrary"),
    )(z, gq, gk, dqn, dkn, dv)


def _na_onehot():
    qc = np.arange(GRID_W)[:, None]
    kc = np.arange(GRID_W)[None, :]
    c0 = np.clip(qc - NA_WIN_C // 2, 0, GRID_W - NA_WIN_C)
    valid = (kc >= c0) & (kc < c0 + NA_WIN_C)
    dc = kc - qc + (NA_WIN_C - 1)
    e = np.zeros((32, GRID_W, GRID_W), np.float32)
    for d in range(2 * NA_WIN_C - 1):
        e[d] = valid & (dc == d)
    return e.reshape(32, GRID_W * GRID_W), valid.reshape(1, -1)


def _rpb_expand(rpb, name):
    e, valid = _na_onehot()
    negmask = np.where(valid, 0.0, NEG).astype(np.float32)
    nd = 2 * NA_WIN_R - 1
    r2 = jnp.pad(rpb.reshape(NA_HEADS * nd, 2 * NA_WIN_C - 1), ((0, 128 - NA_HEADS * nd), (0, 1)))

    def body(r_ref, e_ref, m_ref, o_ref):
        o_ref[...] = _dot(r_ref[...], e_ref[...], prec=HI) + m_ref[...]

    t = pl.pallas_call(body, name=name, out_shape=jax.ShapeDtypeStruct((128, GRID_W * GRID_W), F32))(
        r2, jnp.asarray(e), jnp.asarray(negmask))
    t = t[:NA_HEADS * nd].reshape(NA_HEADS, nd, GRID_W, GRID_W)
    return jnp.stack([jnp.concatenate([t[:, b + w] for w in range(NA_WIN_R)], axis=-1) for b in range(NA_WIN_R)], axis=1)


def _rpb_reduce(dbias, name):
    e, _ = _na_onehot()
    nd = 2 * NA_WIN_R - 1
    et = np.zeros((GRID_W * GRID_W, 128), np.float32)
    et[:, :32] = e.T
    sel = np.zeros((128, NA_HEADS * NA_WIN_R * NA_WIN_R), np.float32)
    for h in range(NA_HEADS):
        for b in range(NA_WIN_R):
            for w in range(NA_WIN_R):
                sel[h * nd + b + w, (h * NA_WIN_R + b) * NA_WIN_R + w] = 1.0
    x = dbias.reshape(NA_HEADS, NA_WIN_R, GRID_W, NA_WIN_R, GRID_W).transpose(0, 1, 3, 2, 4).reshape(-1, GRID_W * GRID_W)

    def body(x_ref, et_ref, sel_ref, o_ref):
        g = _dot(x_ref[...], et_ref[...], prec=HI)
        o_ref[...] = _dot(sel_ref[...], g, prec=HI)

    out = pl.pallas_call(body, name=name, out_shape=jax.ShapeDtypeStruct((128, 128), F32))(x, jnp.asarray(et), jnp.asarray(sel))
    return out[:NA_HEADS * nd, :2 * NA_WIN_C - 1].reshape(NA_HEADS, nd, 2 * NA_WIN_C - 1)


def _na_base(r, rows):
    return jnp.clip(r - NA_WIN_R // 2, 0, rows - NA_WIN_R) - r + (NA_WIN_R - 1)


def _na_probs(q_ref, k_ref, bias_ref, P, r0w):
    sl = [slice(128 * pp, 128 * pp + 128) for pp in range(P)]
    m = [_lane_mask(128, 64 * hh, 64) for hh in range(2)]
    kw = [k_ref[r0w, sl[pp]] for pp in range(P)]
    units = [(pp, hh) for pp in range(P) for hh in range(2)]
    qm = {u: (q_ref[:, sl[u[0]]].astype(F32) * m[u[1]]).astype(BF16) for u in units}
    s = {u: _dot(qm[u], kw[u[0]], "nt") + bias_ref[2 * u[0] + u[1], 0] for u in units}
    p = {}
    for u in units:
        e = jnp.exp(s[u] - jnp.max(s[u], axis=-1, keepdims=True))
        p[u] = e / jnp.sum(e, axis=-1, keepdims=True)
    return sl, m, kw, units, qm, p


NA_FWD_PAIRS = 4
NA_BWD_PAIRS = 2


def _na_attn(qn, kn, vb, bias, name, ride=None):
    S = qn.shape[0]
    rows = S // GRID_W
    nk = NA_WIN_R * GRID_W
    P = NA_FWD_PAIRS
    W = 128 * P

    def body(q_ref, k_ref, v_ref, b_ref, o_ref):
        r = pl.program_id(1)
        r0w = pl.ds(pl.multiple_of(jnp.clip(r - NA_WIN_R // 2, 0, rows - NA_WIN_R) * GRID_W, GRID_W), nk)
        sl, m, _, units, _, p = _na_probs(q_ref, k_ref, b_ref, P, r0w)
        o = {u: _dot(p[u].astype(BF16), v_ref[r0w, sl[u[0]]]) for u in units}
        for pp in range(P):
            o_ref[:, sl[pp]] = (o[pp, 0] * m[0] + o[pp, 1] * m[1]).astype(BF16)

    full = pl.BlockSpec((S, W), lambda g, r: (0, g))
    (o,), got = _ride_call(
        body, ride, name=name, grid=(NA_HEADS // (2 * P), rows),
        in_specs=[pl.BlockSpec((GRID_W, W), lambda g, r: (r, g)), full, full,
                  pl.BlockSpec((2 * P, 1, GRID_W, nk), lambda g, r: (g, _na_base(r, rows), 0, 0))],
        out_specs=[pl.BlockSpec((GRID_W, W), lambda g, r: (r, g))],
        out_shape=[jax.ShapeDtypeStruct((S, NA_W), BF16)], args=(qn, kn, vb, bias), semantics=("parallel", "arbitrary"))
    return o, got


def _na_attn_bwd(qn, kn, vb, bias, do, name, ride=None):
    S = qn.shape[0]
    rows = S // GRID_W
    nk = NA_WIN_R * GRID_W
    P = NA_BWD_PAIRS
    W = 128 * P

    def body(q_ref, k_ref, v_ref, b_ref, do_ref, dq_ref, dk_ref, dv_ref, db_ref):
        r = pl.program_id(1)

        @pl.when(r == 0)
        def _():
            dk_ref[...] = jnp.zeros_like(dk_ref)
            dv_ref[...] = jnp.zeros_like(dv_ref)

        r0w = pl.ds(pl.multiple_of(jnp.clip(r - NA_WIN_R // 2, 0, rows - NA_WIN_R) * GRID_W, GRID_W), nk)
        fresh = jnp.logical_or(r <= NA_WIN_R // 2, r > rows - NA_WIN_R // 2)
        sl, m, kw, units, qm, p = _na_probs(q_ref, k_ref, b_ref, P, r0w)
        dom = {u: (do_ref[:, sl[u[0]]].astype(F32) * m[u[1]]).astype(BF16) for u in units}
        dp = {u: _dot(dom[u], v_ref[r0w, sl[u[0]]], "nt") for u in units}
        dvw = {u: _dot(p[u].astype(BF16), dom[u], "tn") for u in units}
        ds = {u: p[u] * (dp[u] - jnp.sum(p[u] * dp[u], axis=-1, keepdims=True)) for u in units}

        @pl.when(fresh)
        def _():
            for u in units:
                db_ref[2 * u[0] + u[1], 0] = ds[u]

        @pl.when(jnp.logical_not(fresh))
        def _():
            for u in units:
                db_ref[2 * u[0] + u[1], 0] += ds[u]

        dsb = {u: ds[u].astype(BF16) for u in units}
        dq = {u: _dot(dsb[u], kw[u[0]]) for u in units}
        dkw = {u: _dot(dsb[u], qm[u], "tn") for u in units}
        for pp in range(P):
            dq_ref[:, sl[pp]] = dq[pp, 0] * m[0] + dq[pp, 1] * m[1]
            dk_ref[r0w, sl[pp]] += dkw[pp, 0] + dkw[pp, 1]
            dv_ref[r0w, sl[pp]] += dvw[pp, 0] + dvw[pp, 1]

    qblk = pl.BlockSpec((GRID_W, W), lambda g, r: (r, g))
    full = pl.BlockSpec((S, W), lambda g, r: (0, g))
    bblk = pl.BlockSpec((2 * P, 1, GRID_W, nk), lambda g, r: (g, _na_base(r, rows), 0, 0))
    return _ride_call(
        body, ride, name=name, grid=(NA_HEADS // (2 * P), rows),
        in_specs=[qblk, full, full, bblk, qblk], out_specs=[qblk, full, full, bblk],
        out_shape=[jax.ShapeDtypeStruct((S, NA_W), F32)] * 3 + [jax.ShapeDtypeStruct((NA_HEADS, NA_WIN_R, GRID_W, nk), F32)],
        args=(qn, kn, vb, bias, do), semantics=("parallel", "arbitrary"))


def _logsig(x):
    return jnp.minimum(x, 0.0) - jnp.log(1.0 + jnp.exp(-jnp.abs(x)))


def _gla_gates(z, c0, wg, bias, name, tm=512):
    S = z.shape[0]
    tm = min(tm, S)
    zv = _V(z, c0, 128)
    W = 2 * GLA_HEADS * GLA_DK

    def body(z_ref, w_ref, b_ref, o_ref):
        pre = _dot(z_ref[...].astype(BF16), w_ref[...]) + b_ref[...]
        o_ref[...] = _logsig(pre) * (1.0 / GLA_TAU)

    return pl.pallas_call(
        body, name=name, grid=(S // tm,),
        in_specs=[zv.spec(tm, 128, lambda i: i, lambda i: 0), pl.BlockSpec((128, W), lambda i: (0, 0)),
                  pl.BlockSpec((1, W), lambda i: (0, 0))],
        out_specs=pl.BlockSpec((tm, W), lambda i: (i, 0)), out_shape=jax.ShapeDtypeStruct((S, W), F32),
        compiler_params=_cparams("parallel"),
    )(z, wg, bias)


def _gla_gates_bwd(z, c0, wg, bias, dg_f, dg_b, name, tm=512):
    S = z.shape[0]
    tm = min(tm, S)
    zv = _V(z, c0, 128)
    W = 2 * GLA_HEADS * GLA_DK

    def body(z_ref, w_ref, b_ref, dgf_ref, dgb_ref, dp_ref, db_ref):
        pre = _dot(z_ref[...].astype(BF16), w_ref[...]) + b_ref[...]
        dg = jnp.concatenate([dgf_ref[...], dgb_ref[...]], axis=-1)
        dpre = dg * (1.0 / GLA_TAU) * jax.nn.sigmoid(-pre)
        dp_ref[...] = dpre.astype(BF16)
        _acc_rows(db_ref, dpre, pl.program_id(0) == 0)

    half = pl.BlockSpec((tm, W // 2), lambda i: (i, 0))
    return pl.pallas_call(
        body, name=name, grid=(S // tm,),
        in_specs=[zv.spec(tm, 128, lambda i: i, lambda i: 0), pl.BlockSpec((128, W), lambda i: (0, 0)),
                  pl.BlockSpec((1, W), lambda i: (0, 0)), half, half],
        out_specs=[pl.BlockSpec((tm, W), lambda i: (i, 0)), pl.BlockSpec((8, W), lambda i: (0, 0))],
        out_shape=[jax.ShapeDtypeStruct((S, W), BF16), jax.ShapeDtypeStruct((8, W), F32)],
        compiler_params=_cparams("arbitrary"),
    )(z, wg, bias, dg_f, dg_b)


def _gla_chunk_terms(zqk, g, p, rev):
    C = GLA_CHUNK
    i, j = _iota((C, C), 0), _iota((C, C), 1)
    cum = jnp.where((j >= i) if rev else (j <= i), 1.0, 0.0).astype(F32)
    q2 = zqk[:, 128 * p:128 * p + 128] * (GLA_DK ** -0.5)
    k2 = zqk[:, 256 + 128 * p:256 + 128 * p + 128]
    b2 = _dot(cum, g[:, 128 * p:128 * p + 128], prec=HI)
    bl2 = b2[0:1] if rev else b2[C - 1:C]
    eb = jnp.exp(b2)
    qe2 = q2 * eb
    ke2 = k2 * jnp.exp(-b2)
    kend2 = k2 * jnp.exp(bl2 - b2)
    dec2 = jnp.exp(bl2)
    tri = (j > i) if rev else (j <= i)
    return b2, bl2, eb, qe2, ke2, kend2, dec2, tri


def _row_to_col(row):
    eye = _iota((128, 128), 0) == _iota((128, 128), 1)
    return jnp.sum(jnp.where(eye, row, 0.0), axis=1, keepdims=True)


def _col_to_row(col):
    eye = _iota((128, 128), 0) == _iota((128, 128), 1)
    return jnp.sum(jnp.where(eye, col, 0.0), axis=0, keepdims=True)


GLA_GROUP = 4


def _gla_fwd(z, c_qk, c_v, gfb, name, ride=None):
    S = z.shape[0]
    C = GLA_CHUNK
    n = S // C
    G = math.gcd(GLA_GROUP, n)
    nb, GC = n // G, G * C
    WQK = 2 * GLA_HEADS * GLA_DK
    WV = GLA_HEADS * GLA_DV
    zqk, zvv = _V(z, c_qk, WQK), _V(z, c_v, WV)

    def body(qkf_ref, vf_ref, gf_ref, qkb_ref, vb_ref, gb_ref, of_ref, ob_ref, sf_ref, sb_ref, stf, stb):
        @pl.when(pl.program_id(0) == 0)
        def _():
            stf[...] = jnp.zeros_like(stf)
            stb[...] = jnp.zeros_like(stb)

        dirs = ((False, qkf_ref, vf_ref, gf_ref, of_ref, sf_ref, stf), (True, qkb_ref, vb_ref, gb_ref, ob_ref, sb_ref, stb))
        rows = lambda gi: slice(gi * C, (gi + 1) * C)
        pairs = [(d, gi, p) for d in range(2) for gi in range(G) for p in range(GLA_HEADS // 2)]
        heads = [(d, gi, p, hh) for d, gi, p in pairs for hh in range(2)]
        mask = [_lane_mask(128, 64 * hh, 64) for hh in range(2)]
        terms = {(d, gi, p): _gla_chunk_terms(dirs[d][1][rows(gi), :], dirs[d][3][rows(gi), :], p, dirs[d][0])
                 for d, gi, p in pairs}
        dec_col = {k: _row_to_col(t[6]) for k, t in terms.items()}
        vh = {(d, gi, h): dirs[d][2][rows(gi), 128 * h:128 * h + 128].astype(BF16)
              for d in range(2) for gi in range(G) for h in range(GLA_HEADS)}
        qm = {(d, gi, p, hh): (terms[d, gi, p][3] * mask[hh]).astype(BF16) for d, gi, p, hh in heads}
        a_raw = {(d, gi, p, hh): _dot(qm[d, gi, p, hh], terms[d, gi, p][4].astype(BF16), "nt") for d, gi, p, hh in heads}
        upd = {(d, gi, p, hh): _dot((terms[d, gi, p][5] * mask[hh]).astype(BF16), vh[d, gi, 2 * p + hh], "tn")
               for d, gi, p, hh in heads}
        intra = {(d, gi, p, hh): _dot(jnp.where(terms[d, gi, p][7], a_raw[d, gi, p, hh], 0.0).astype(BF16), vh[d, gi, 2 * p + hh])
                 for d, gi, p, hh in heads}
        state = {(d, h): dirs[d][6][h] for d in range(2) for h in range(GLA_HEADS)}
        for k in range(G):
            for d in range(2):
                gi = G - 1 - k if dirs[d][0] else k
                for p in range(GLA_HEADS // 2):
                    for hh in range(2):
                        h = 2 * p + hh
                        sp = state[d, h]
                        dirs[d][4][rows(gi), 128 * h:128 * h + 128] = intra[d, gi, p, hh] + _dot(qm[d, gi, p, hh], sp.astype(BF16))
                        dirs[d][5][gi, h] = sp
                        state[d, h] = dec_col[d, gi, p] * sp + upd[d, gi, p, hh]
        for d in range(2):
            for h in range(GLA_HEADS):
                dirs[d][6][h] = state[d, h]

    fw = lambda i: i
    bw = lambda i: nb - 1 - i
    zero = lambda i: 0
    in_specs = []
    for ix, col in ((fw, 0), (bw, 1)):
        in_specs += [zqk.spec(GC, WQK, ix, zero), zvv.spec(GC, WV, ix, zero),
                     pl.BlockSpec((GC, WQK // 2), functools.partial(lambda i, ix, col: (ix(i), col), ix=ix, col=col))]
    return _ride_call(
        body, ride, name=name, grid=(nb,), in_specs=in_specs,
        out_specs=[pl.BlockSpec((GC, WV), lambda i: (i, 0)), pl.BlockSpec((GC, WV), lambda i: (nb - 1 - i, 0)),
                   pl.BlockSpec((G, GLA_HEADS, 128, 128), lambda i: (i, 0, 0, 0)),
                   pl.BlockSpec((G, GLA_HEADS, 128, 128), lambda i: (nb - 1 - i, 0, 0, 0))],
        out_shape=[jax.ShapeDtypeStruct((S, WV), F32)] * 2 + [jax.ShapeDtypeStruct((n, GLA_HEADS, 128, 128), F32)] * 2,
        scratch_shapes=[pltpu.VMEM((GLA_HEADS, 128, 128), F32)] * 2, args=(z, z, gfb, z, z, gfb), semantics=("arbitrary",))


def _gla_bwd(z, c_qk, c_v, gfb, do, s_f, s_b, name, ride=None):
    S = z.shape[0]
    C = GLA_CHUNK
    n = S // C
    G = math.gcd(GLA_GROUP, n)
    nb, GC = n // G, G * C
    WQK = 2 * GLA_HEADS * GLA_DK
    WV = GLA_HEADS * GLA_DV
    zqk, zvv = _V(z, c_qk, WQK), _V(z, c_v, WV)

    def body(qkf_ref, vf_ref, gf_ref, dof_ref, sf_ref, qkb_ref, vb_ref, gb_ref, dob_ref, sb_ref,
             dqkf_ref, dvf_ref, dgf_ref, dqkb_ref, dvb_ref, dgb_ref, dstf, dstb):
        @pl.when(pl.program_id(0) == 0)
        def _():
            dstf[...] = jnp.zeros_like(dstf)
            dstb[...] = jnp.zeros_like(dstb)

        dirs = ((False, qkf_ref, vf_ref, gf_ref, dof_ref, sf_ref, dqkf_ref, dvf_ref, dgf_ref, dstf),
                (True, qkb_ref, vb_ref, gb_ref, dob_ref, sb_ref, dqkb_ref, dvb_ref, dgb_ref, dstb))
        rows = lambda gi: slice(gi * C, (gi + 1) * C)
        pairs = [(d, gi, p) for d in range(2) for gi in range(G) for p in range(GLA_HEADS // 2)]
        heads = [(d, gi, p, hh) for d, gi, p in pairs for hh in range(2)]
        mask = [_lane_mask(128, 64 * hh, 64) for hh in range(2)]
        T = {(d, gi, p): _gla_chunk_terms(dirs[d][1][rows(gi), :], dirs[d][3][rows(gi), :], p, dirs[d][0]) for d, gi, p in pairs}
        dec_col = {k: _row_to_col(t[6]) for k, t in T.items()}
        hd = lambda d, gi, p, hh: (d, gi, 2 * p + hh)
        vh = {(d, gi, h): dirs[d][2][rows(gi), 128 * h:128 * h + 128].astype(BF16)
              for d in range(2) for gi in range(G) for h in range(GLA_HEADS)}
        doh = {(d, gi, h): dirs[d][4][rows(gi), 128 * h:128 * h + 128].astype(BF16)
               for d in range(2) for gi in range(G) for h in range(GLA_HEADS)}
        sp = {(d, gi, h): dirs[d][5][gi, h] for d in range(2) for gi in range(G) for h in range(GLA_HEADS)}
        qm = {u: (T[u[:3]][3] * mask[u[3]]).astype(BF16) for u in heads}
        kem = {u: (T[u[:3]][4] * mask[u[3]]).astype(BF16) for u in heads}
        kendm = {u: (T[u[:3]][5] * mask[u[3]]).astype(BF16) for u in heads}
        a_raw = {u: _dot(qm[u], T[u[:3]][4].astype(BF16), "nt") for u in heads}
        da_raw = {u: _dot(doh[hd(*u)], vh[hd(*u)], "nt") for u in heads}
        w_upd = {u: _dot(qm[u], doh[hd(*u)], "tn") for u in heads}
        dqe_s = {u: _dot(doh[hd(*u)], sp[hd(*u)].astype(BF16), "nt") for u in heads}
        a = {u: jnp.where(T[u[:3]][7], a_raw[u], 0.0).astype(BF16) for u in heads}
        da = {u: jnp.where(T[u[:3]][7], da_raw[u], 0.0).astype(BF16) for u in heads}
        dqe = {u: _dot(da[u], kem[u]) + dqe_s[u] for u in heads}
        dke = {u: _dot(da[u], qm[u], "tn") for u in heads}
        dv_a = {u: _dot(a[u], doh[hd(*u)], "tn") for u in heads}
        ds = {}
        for d in range(2):
            cur = [dirs[d][9][h] for h in range(GLA_HEADS)]
            for gi in (range(G) if dirs[d][0] else reversed(range(G))):
                for p in range(GLA_HEADS // 2):
                    for hh in range(2):
                        h = 2 * p + hh
                        ds[d, gi, p, hh] = cur[h]
                        cur[h] = dec_col[d, gi, p] * cur[h] + w_upd[d, gi, p, hh]
            for h in range(GLA_HEADS):
                dirs[d][9][h] = cur[h]
        dsb = {u: ds[u].astype(BF16) for u in heads}
        dv_b = {u: _dot(kendm[u], dsb[u]) for u in heads}
        dkend = {u: _dot(vh[hd(*u)], dsb[u], "nt") * mask[u[3]] for u in heads}
        ddec = {u: _col_to_row(jnp.sum(ds[u] * sp[hd(*u)], axis=1, keepdims=True)) for u in heads}
        for u in heads:
            d, gi, h = hd(*u)
            dirs[d][7][rows(gi), 128 * h:128 * h + 128] = dv_a[u] + dv_b[u]
        i, j = _iota((C, C), 0), _iota((C, C), 1)
        for d, gi, p in pairs:
            rev = dirs[d][0]
            b2, bl2, eb, qe2, ke2, kend2, dec2, _ = T[d, gi, p]
            u0, u1 = (d, gi, p, 0), (d, gi, p, 1)
            dqe2, dke2, dkend2, ddec2 = dqe[u0] + dqe[u1], dke[u0] + dke[u1], dkend[u0] + dkend[u1], ddec[u0] + ddec[u1]
            dirs[d][6][rows(gi), 128 * p:128 * p + 128] = dqe2 * eb * (GLA_DK ** -0.5)
            dirs[d][6][rows(gi), 256 + 128 * p:256 + 128 * p + 128] = dke2 * jnp.exp(-b2) + dkend2 * jnp.exp(bl2 - b2)
            dkk = dkend2 * kend2
            dbl2 = jnp.sum(dkk, axis=0, keepdims=True) + ddec2 * dec2
            edge = _iota((C, 128), 0) == (0 if rev else C - 1)
            db2 = dqe2 * qe2 - dke2 * ke2 - dkk + jnp.where(edge, dbl2, 0.0)
            cum_t = jnp.where((j <= i) if rev else (j >= i), 1.0, 0.0).astype(F32)
            dirs[d][8][rows(gi), 128 * p:128 * p + 128] = _dot(cum_t, db2, prec=HI)

    fw = lambda i: nb - 1 - i
    bw = lambda i: i
    zero = lambda i: 0
    in_specs, out_specs = [], []
    for ix, col in ((fw, 0), (bw, 1)):
        blk = functools.partial(lambda i, ix: (ix(i), 0), ix=ix)
        in_specs += [zqk.spec(GC, WQK, ix, zero), zvv.spec(GC, WV, ix, zero),
                     pl.BlockSpec((GC, WQK // 2), functools.partial(lambda i, ix, col: (ix(i), col), ix=ix, col=col)),
                     pl.BlockSpec((GC, WV), blk),
                     pl.BlockSpec((G, GLA_HEADS, 128, 128), functools.partial(lambda i, ix: (ix(i), 0, 0, 0), ix=ix))]
        out_specs += [pl.BlockSpec((GC, WQK), blk), pl.BlockSpec((GC, WV), blk), pl.BlockSpec((GC, WQK // 2), blk)]
    shapes = [jax.ShapeDtypeStruct((S, WQK), F32), jax.ShapeDtypeStruct((S, WV), F32), jax.ShapeDtypeStruct((S, WQK // 2), F32)]
    return _ride_call(
        body, ride, name=name, grid=(nb,), in_specs=in_specs, out_specs=out_specs, out_shape=shapes * 2,
        scratch_shapes=[pltpu.VMEM((GLA_HEADS, 128, 128), F32)] * 2, args=(z, z, gfb, do, s_f, z, z, gfb, do, s_b),
        semantics=("arbitrary",))


def _gla_post(o_f, o_b, z, c_r, gn, name, tm=512):
    S, WV = o_f.shape
    tm = min(tm, S)
    zr = _V(z, c_r, WV)

    def body(of_ref, ob_ref, r_ref, g_ref, y_ref):
        gr = r_ref[...]
        sil = gr * jax.nn.sigmoid(gr)
        for h in range(GLA_HEADS):
            sl = slice(GLA_DV * h, GLA_DV * (h + 1))
            o = of_ref[:, sl] + ob_ref[:, sl]
            on = o * lax.rsqrt(jnp.mean(o * o, axis=-1, keepdims=True) + EPS) * g_ref[...]
            y_ref[:, sl] = (on * sil[:, sl]).astype(BF16)

    blk = pl.BlockSpec((tm, WV), lambda i: (i, 0))
    return pl.pallas_call(
        body, name=name, grid=(S // tm,),
        in_specs=[blk, blk, zr.spec(tm, WV, lambda i: i, lambda i: 0), pl.BlockSpec((1, GLA_DV), lambda i: (0, 0))],
        out_specs=blk, out_shape=jax.ShapeDtypeStruct((S, WV), BF16), compiler_params=_cparams("parallel"),
    )(o_f, o_b, z, gn)


def _gla_post_bwd(o_f, o_b, z, c_r, gn, dy, name, tm=512):
    S, WV = o_f.shape
    tm = min(tm, S)
    zr = _V(z, c_r, WV)

    def body(of_ref, ob_ref, r_ref, g_ref, dy_ref, do_ref, dr_ref, dg_ref):
        gr = r_ref[...]
        sig = jax.nn.sigmoid(gr)
        sil = gr * sig
        dyv = dy_ref[...].astype(F32)
        dgn = jnp.zeros((tm, GLA_DV), F32)
        for h in range(GLA_HEADS):
            sl = slice(GLA_DV * h, GLA_DV * (h + 1))
            o = of_ref[:, sl] + ob_ref[:, sl]
            rstd = lax.rsqrt(jnp.mean(o * o, axis=-1, keepdims=True) + EPS)
            xhat = o * rstd
            don = dyv[:, sl] * sil[:, sl]
            dr_ref[:, sl] = (dyv[:, sl] * xhat * g_ref[...] * (sig[:, sl] * (1.0 + gr[:, sl] * (1.0 - sig[:, sl])))).astype(BF16)
            dxhat = don * g_ref[...]
            do_ref[:, sl] = rstd * (dxhat - xhat * jnp.mean(dxhat * xhat, axis=-1, keepdims=True))
            dgn = dgn + don * xhat
        _acc_rows(dg_ref, dgn, pl.program_id(0) == 0)

    blk = pl.BlockSpec((tm, WV), lambda i: (i, 0))
    return pl.pallas_call(
        body, name=name, grid=(S // tm,),
        in_specs=[blk, blk, zr.spec(tm, WV, lambda i: i, lambda i: 0), pl.BlockSpec((1, GLA_DV), lambda i: (0, 0)), blk],
        out_specs=[blk, blk, pl.BlockSpec((8, GLA_DV), lambda i: (0, 0))],
        out_shape=[jax.ShapeDtypeStruct((S, WV), F32), jax.ShapeDtypeStruct((S, WV), BF16), jax.ShapeDtypeStruct((8, GLA_DV), F32)],
        compiler_params=_cparams("arbitrary"),
    )(o_f, o_b, z, gn, dy)


def _gla_assemble(dqk_f, dqk_b, dv_f, dv_b, dgr, name, tm=512):
    S = dqk_f.shape[0]
    tm = min(tm, S)

    def body(a_ref, b_ref, c_ref, d_ref, r_ref, o_ref):
        o_ref[:, 0:512] = (a_ref[...] + b_ref[...]).astype(BF16)
        o_ref[:, 512:1024] = (c_ref[...] + d_ref[...]).astype(BF16)
        o_ref[:, 1024:1536] = r_ref[...]

    blk = pl.BlockSpec((tm, 512), lambda i: (i, 0))
    return pl.pallas_call(
        body, name=name, grid=(S // tm,), in_specs=[blk] * 5, out_specs=pl.BlockSpec((tm, 1536), lambda i: (i, 0)),
        out_shape=jax.ShapeDtypeStruct((S, 1536), BF16), compiler_params=_cparams("parallel"),
    )(dqk_f, dqk_b, dv_f, dv_b, dgr)


def _rope(r, cos, sg):
    return r * cos + pltpu.roll(r, 64, 1) * sg


def _unrope(dy, cos, sg):
    return dy * cos + pltpu.roll(dy * sg, 64, 1)


def _mla_prep(z, c_q, c_kr, wuq, wukv, g_cq, g_ckv, g_q, g_k, cos, sg, name, tm=256):
    S = z.shape[0]
    tm = min(tm, S)
    zc, zk = _V(z, c_q, 2 * MLA_RANK), _V(z, c_kr, 128)
    inv = 1.0 / MLA_QK

    def body(zc_ref, zk_ref, wuq_ref, wukv_ref, gcq_ref, gckv_ref, gq_ref, gk_ref, cos_ref, sg_ref,
             q_ref, k_ref, v_ref, cqn_ref, ckvn_ref):
        def norm(xv, gv):
            return (xv * lax.rsqrt(jnp.mean(xv * xv, axis=-1, keepdims=True) + EPS) * gv).astype(BF16)

        cqn = norm(zc_ref[:, 0:MLA_RANK], gcq_ref[...])
        ckvn = norm(zc_ref[:, MLA_RANK:2 * MLA_RANK], gckv_ref[...])
        cqn_ref[...] = cqn
        ckvn_ref[...] = ckvn
        qf = _dot(cqn, wuq_ref[...])
        kv = _dot(ckvn, wukv_ref[...])
        kr = zk_ref[...]
        krss = jnp.sum(kr * kr, axis=-1, keepdims=True)
        cosv, sgv = cos_ref[...], sg_ref[...]
        gq, gk = gq_ref[...], gk_ref[...]
        for h in range(MLA_HEADS):
            qh = qf[:, MLA_SLOT * h:MLA_SLOT * (h + 1)]
            qhn = qh * lax.rsqrt(jnp.sum(qh * qh, axis=-1, keepdims=True) * inv + EPS) * gq
            q_ref[:, MLA_SLOT * h:MLA_SLOT * h + 128] = (qhn[:, 0:128] * MLA_QSCALE).astype(BF16)
            q_ref[:, MLA_SLOT * h + 128:MLA_SLOT * (h + 1)] = (_rope(qhn[:, 128:256], cosv, sgv) * MLA_QSCALE).astype(BF16)
            kn = kv[:, 256 * h:256 * h + 128]
            rstd = lax.rsqrt((jnp.sum(kn * kn, axis=-1, keepdims=True) + krss) * inv + EPS)
            k_ref[:, MLA_SLOT * h:MLA_SLOT * h + 128] = (kn * rstd * gk[:, 0:128]).astype(BF16)
            k_ref[:, MLA_SLOT * h + 128:MLA_SLOT * (h + 1)] = _rope(kr * rstd * gk[:, 128:256], cosv, sgv).astype(BF16)
            v_ref[:, 128 * h:128 * (h + 1)] = kv[:, 256 * h + 128:256 * (h + 1)].astype(BF16)

    row = lambda w: pl.BlockSpec((tm, w), lambda i: (i, 0))
    const = lambda r, w: pl.BlockSpec((r, w), lambda i: (0, 0))
    W = MLA_HEADS * MLA_SLOT
    return pl.pallas_call(
        body, name=name, grid=(S // tm,),
        in_specs=[zc.spec(tm, 2 * MLA_RANK, lambda i: i, lambda i: 0), zk.spec(tm, 128, lambda i: i, lambda i: 0),
                  const(MLA_RANK, W), const(MLA_RANK, W), const(1, MLA_RANK), const(1, MLA_RANK), const(1, MLA_SLOT),
                  const(1, MLA_SLOT), row(128), row(128)],
        out_specs=[row(W), row(W), row(MLA_HEADS * MLA_V), row(MLA_RANK), row(MLA_RANK)],
        out_shape=[jax.ShapeDtypeStruct((S, W), BF16), jax.ShapeDtypeStruct((S, W), BF16),
                   jax.ShapeDtypeStruct((S, MLA_HEADS * MLA_V), BF16), jax.ShapeDtypeStruct((S, MLA_RANK), BF16),
                   jax.ShapeDtypeStruct((S, MLA_RANK), BF16)],
        compiler_params=_cparams("parallel"),
    )(z, z, wuq, wukv, g_cq, g_ckv, g_q, g_k, cos, sg)


def _mla_prep_bwd(z, c_kr, cqn, ckvn, wuq, wukv, g_q, g_k, cos, sg, dq, dk, dv, name, tm=256):
    S = z.shape[0]
    tm = min(tm, S)
    zk = _V(z, c_kr, 128)
    inv = 1.0 / MLA_QK

    def body(zk_ref, cqn_ref, ckvn_ref, wuq_ref, wukv_ref, gq_ref, gk_ref, cos_ref, sg_ref, dq_ref, dk_ref, dv_ref,
             dqf_ref, dkv_ref, dkr_ref, dgq_ref, dgk_ref):
        first = pl.program_id(0) == 0
        qf = _dot(cqn_ref[...], wuq_ref[...])
        kv = _dot(ckvn_ref[...], wukv_ref[...])
        kr = zk_ref[...]
        krss = jnp.sum(kr * kr, axis=-1, keepdims=True)
        cosv, sgv = cos_ref[...], sg_ref[...]
        gq, gk = gq_ref[...], gk_ref[...]
        dkr = jnp.zeros((tm, 128), F32)
        dgq = jnp.zeros((tm, MLA_SLOT), F32)
        dgkn = jnp.zeros((tm, 128), F32)
        dgkr = jnp.zeros((tm, 128), F32)
        for h in range(MLA_HEADS):
            qh = qf[:, MLA_SLOT * h:MLA_SLOT * (h + 1)]
            rstd = lax.rsqrt(jnp.sum(qh * qh, axis=-1, keepdims=True) * inv + EPS)
            xhat = qh * rstd
            dyn = jnp.concatenate([dq_ref[:, MLA_SLOT * h:MLA_SLOT * h + 128],
                                   _unrope(dq_ref[:, MLA_SLOT * h + 128:MLA_SLOT * (h + 1)], cosv, sgv)], axis=-1)
            dxhat = dyn * gq
            dqf_ref[:, MLA_SLOT * h:MLA_SLOT * (h + 1)] = (
                rstd * (dxhat - xhat * (jnp.sum(dxhat * xhat, axis=-1, keepdims=True) * inv))).astype(BF16)
            dgq = dgq + dyn * xhat

            kn = kv[:, 256 * h:256 * h + 128]
            rstd = lax.rsqrt((jnp.sum(kn * kn, axis=-1, keepdims=True) + krss) * inv + EPS)
            xn, xr = kn * rstd, kr * rstd
            dyn_n = dk_ref[:, MLA_SLOT * h:MLA_SLOT * h + 128] * (1.0 / MLA_QSCALE)
            dyn_r = _unrope(dk_ref[:, MLA_SLOT * h + 128:MLA_SLOT * (h + 1)] * (1.0 / MLA_QSCALE), cosv, sgv)
            dxn, dxr = dyn_n * gk[:, 0:128], dyn_r * gk[:, 128:256]
            proj = (jnp.sum(dxn * xn, axis=-1, keepdims=True) + jnp.sum(dxr * xr, axis=-1, keepdims=True)) * inv
            dkv_ref[:, 256 * h:256 * h + 128] = (rstd * (dxn - xn * proj)).astype(BF16)
            dkv_ref[:, 256 * h + 128:256 * (h + 1)] = dv_ref[:, 128 * h:128 * (h + 1)].astype(BF16)
            dkr = dkr + rstd * (dxr - xr * proj)
            dgkn = dgkn + dyn_n * xn
            dgkr = dgkr + dyn_r * xr
        dkr_ref[...] = dkr.astype(BF16)
        _acc_rows(dgq_ref, dgq, first)
        _acc_rows(dgk_ref, jnp.concatenate([dgkn, dgkr], axis=-1), first)

    row = lambda w: pl.BlockSpec((tm, w), lambda i: (i, 0))
    const = lambda r, w: pl.BlockSpec((r, w), lambda i: (0, 0))
    W = MLA_HEADS * MLA_SLOT
    return pl.pallas_call(
        body, name=name, grid=(S // tm,),
        in_specs=[zk.spec(tm, 128, lambda i: i, lambda i: 0), row(MLA_RANK), row(MLA_RANK), const(MLA_RANK, W),
                  const(MLA_RANK, W), const(1, MLA_SLOT), const(1, MLA_SLOT), row(128), row(128), row(W), row(W),
                  row(MLA_HEADS * MLA_V)],
        out_specs=[row(W), row(W), row(128), const(8, MLA_SLOT), const(8, MLA_SLOT)],
        out_shape=[jax.ShapeDtypeStruct((S, W), BF16), jax.ShapeDtypeStruct((S, W), BF16), jax.ShapeDtypeStruct((S, 128), BF16),
                   jax.ShapeDtypeStruct((8, MLA_SLOT), F32), jax.ShapeDtypeStruct((8, MLA_SLOT), F32)],
        compiler_params=_cparams("arbitrary"),
    )(z, cqn, ckvn, wuq, wukv, g_q, g_k, cos, sg, dq, dk, dv)


def _exp2_rows(s2):
    e = jnp.exp2(s2 - jnp.max(s2, axis=-1, keepdims=True))
    return e, 1.0 / jnp.sum(e, axis=-1, keepdims=True)


MLA_SUB = 256


def _mla_attn(q, k, v, name, tq=512, ride=None):
    S = q.shape[0]
    tq = min(tq, S)
    sub = math.gcd(MLA_SUB, tq)
    rows = lambda i: slice(i * sub, (i + 1) * sub)

    def body(q_ref, k_ref, v_ref, o_ref):
        def rest(i, s2):
            e, rl = _exp2_rows(s2)
            o_ref[rows(i), :] = (_dot(e.astype(BF16), v_ref[...]) * rl).astype(BF16)

        _one_ahead(tq // sub, lambda i: _dot(q_ref[rows(i), :], k_ref[...], "nt"), rest)

    (o,), got = _ride_call(
        body, ride, name=name, grid=(MLA_HEADS, S // tq),
        in_specs=[pl.BlockSpec((tq, MLA_SLOT), lambda h, i: (i, h)), pl.BlockSpec((S, MLA_SLOT), lambda h, i: (0, h)),
                  pl.BlockSpec((S, MLA_V), lambda h, i: (0, h))],
        out_specs=[pl.BlockSpec((tq, MLA_V), lambda h, i: (i, h))],
        out_shape=[jax.ShapeDtypeStruct((S, MLA_HEADS * MLA_V), BF16)], args=(q, k, v), semantics=("parallel", "parallel"))
    return o, got


def _mla_attn_bwd(q, k, v, do, name, tq=512, ride=None):
    S = q.shape[0]
    tq = min(tq, S)
    sub = math.gcd(MLA_SUB, tq)
    rows = lambda i: slice(i * sub, (i + 1) * sub)
    scale = MLA_QK ** -0.5

    def body(q_ref, k_ref, v_ref, do_ref, dq_ref, dk_ref, dv_ref):
        @pl.when(pl.program_id(1) == 0)
        def _():
            dk_ref[...] = jnp.zeros_like(dk_ref)
            dv_ref[...] = jnp.zeros_like(dv_ref)

        def matmuls(i):
            return _dot(q_ref[rows(i), :], k_ref[...], "nt"), _dot(do_ref[rows(i), :], v_ref[...], "nt")

        def rest(i, s2_dp):
            s2, dp = s2_dp
            e, rl = _exp2_rows(s2)
            dp = dp * (scale * rl)
            ds = (e * (dp - jnp.sum(e * dp, axis=-1, keepdims=True) * rl)).astype(BF16)
            dq_ref[rows(i), :] = _dot(ds, k_ref[...])
            dk_ref[...] += _dot(ds, q_ref[rows(i), :], "tn")
            dv_ref[...] += _dot(e.astype(BF16), (do_ref[rows(i), :].astype(F32) * rl).astype(BF16), "tn")

        _one_ahead(tq // sub, matmuls, rest)

    W = MLA_HEADS * MLA_SLOT
    return _ride_call(
        body, ride, name=name, grid=(MLA_HEADS, S // tq),
        in_specs=[pl.BlockSpec((tq, MLA_SLOT), lambda h, i: (i, h)), pl.BlockSpec((S, MLA_SLOT), lambda h, i: (0, h)),
                  pl.BlockSpec((S, MLA_V), lambda h, i: (0, h)), pl.BlockSpec((tq, MLA_V), lambda h, i: (i, h))],
        out_specs=[pl.BlockSpec((tq, MLA_SLOT), lambda h, i: (i, h)), pl.BlockSpec((S, MLA_SLOT), lambda h, i: (0, h)),
                   pl.BlockSpec((S, MLA_V), lambda h, i: (0, h))],
        out_shape=[jax.ShapeDtypeStruct((S, W), F32), jax.ShapeDtypeStruct((S, W), F32),
                   jax.ShapeDtypeStruct((S, MLA_HEADS * MLA_V), F32)],
        args=(q, k, v, do), semantics=("parallel", "arbitrary"))


def _merge(ys, ws, z, name, tm=256):
    S = z.shape[0]
    D = ws[0].shape[1]
    tm = min(tm, S)
    zg = _V(z, 0, 3 * D)

    def body(y0, y1, y2, w0, w1, w2, g_ref, m_ref, p0, p1, p2):
        acc = jnp.zeros((tm, D), F32)
        for i, (y_ref, w_ref, p_ref) in enumerate(((y0, w0, p0), (y1, w1, p1), (y2, w2, p2))):
            pv = _dot(y_ref[...], w_ref[...])
            p_ref[...] = pv.astype(BF16)
            acc = acc + jax.nn.sigmoid(g_ref[:, D * i:D * (i + 1)]) * pv
        m_ref[...] = acc.astype(BF16)

    yb = pl.BlockSpec((tm, ys[0].shape[1]), lambda i: (i, 0))
    wb = pl.BlockSpec(ws[0].shape, lambda i: (0, 0))
    ob = pl.BlockSpec((tm, D), lambda i: (i, 0))
    return pl.pallas_call(
        body, name=name, grid=(S // tm,), in_specs=[yb] * 3 + [wb] * 3 + [zg.spec(tm, 3 * D, lambda i: i, lambda i: 0)],
        out_specs=[ob] * 4, out_shape=[jax.ShapeDtypeStruct((S, D), BF16)] * 4, compiler_params=_cparams("parallel"),
    )(*ys, *ws, z)


def _merge_bwd(dmixed, ps, z, name, tm=256):
    S, D = dmixed.shape
    tm = min(tm, S)
    zg = _V(z, 0, 3 * D)

    def body(dm_ref, p0, p1, p2, g_ref, d0, d1, d2, dg_ref):
        dm = dm_ref[...]
        for i, (p_ref, d_ref) in enumerate(((p0, d0), (p1, d1), (p2, d2))):
            gt = jax.nn.sigmoid(g_ref[:, D * i:D * (i + 1)])
            d_ref[...] = (dm * gt).astype(BF16)
            dg_ref[:, D * i:D * (i + 1)] = (dm * p_ref[...].astype(F32) * gt * (1.0 - gt)).astype(BF16)

    ob = pl.BlockSpec((tm, D), lambda i: (i, 0))
    return pl.pallas_call(
        body, name=name, grid=(S // tm,), in_specs=[ob] * 4 + [zg.spec(tm, 3 * D, lambda i: i, lambda i: 0)],
        out_specs=[ob] * 3 + [pl.BlockSpec((tm, 3 * D), lambda i: (i, 0))],
        out_shape=[jax.ShapeDtypeStruct((S, D), BF16)] * 3 + [jax.ShapeDtypeStruct((S, 3 * D), BF16)],
        compiler_params=_cparams("parallel"),
    )(dmixed, *ps, z)


def _loss_head(y, target, name, tm=512):
    S, D = y.shape
    tm = min(tm, S)

    def body(y_ref, t_ref, dy_ref, l_ref):
        e = y_ref[...] - t_ref[...]
        dy_ref[...] = e * (1.0 / D)
        sq = e * e
        part = jnp.sum(sq.reshape(tm // 8, 8, D), axis=0)
        part = jnp.sum(part.reshape(8, D // 128, 128), axis=1) * (0.5 / D)

        @pl.when(pl.program_id(0) == 0)
        def _():
            l_ref[...] = part

        @pl.when(pl.program_id(0) != 0)
        def _():
            l_ref[...] += part

    blk = pl.BlockSpec((tm, D), lambda i: (i, 0))
    return pl.pallas_call(
        body, name=name, grid=(S // tm,), in_specs=[blk, blk], out_specs=[blk, pl.BlockSpec((8, 128), lambda i: (0, 0))],
        out_shape=[jax.ShapeDtypeStruct((S, D), F32), jax.ShapeDtypeStruct((8, 128), F32)],
        compiler_params=_cparams("arbitrary"),
    )(y, target)


def _fold(parts, name, fold=None):
    L, _, W = parts.shape
    assert L <= 8

    def body(*refs):
        p_ref, o_ref = refs[0], refs[-1]
        rows = [jnp.sum(p_ref[l], axis=0, keepdims=True) for l in range(L)]
        rows += [jnp.zeros((1, W), F32)] * (8 - L)
        sums = jnp.concatenate(rows, axis=0)
        o_ref[...] = sums if fold is None else _dot(sums, refs[1][...], prec=HI)

    args = (parts,) if fold is None else (parts, jnp.asarray(fold))
    wout = W if fold is None else 128
    return pl.pallas_call(body, name=name, out_shape=jax.ShapeDtypeStruct((8, wout), F32))(*args)[:L]


def _adamw(w, g, m, v, name, q=None, ride=None):
    R, C = w.shape
    tr = R
    for cand in (512, 256, 128, 64, 32, 16, 8):
        if R % cand == 0 and cand * C * 4 <= 2 * 2**20:
            tr = cand
            break

    def body(*refs):
        if q is None:
            w_ref, g_ref, m_ref, v_ref, d_ref, nm_ref, nv_ref = refs
            gv = g_ref[...]
        else:
            w_ref, g_ref, q_ref, m_ref, v_ref, go_ref, d_ref, nm_ref, nv_ref = refs
            gv = g_ref[...] + q_ref[...]
            go_ref[...] = gv
        mn = ADAM_B1 * m_ref[...] + (1.0 - ADAM_B1) * gv
        vn = ADAM_B2 * v_ref[...] + (1.0 - ADAM_B2) * (gv * gv)
        nm_ref[...] = mn
        nv_ref[...] = vn
        m_hat = mn / (1.0 - ADAM_B1 ** ADAM_STEP)
        v_hat = vn / (1.0 - ADAM_B2 ** ADAM_STEP)
        d_ref[...] = -ADAM_LR * (m_hat / (jnp.sqrt(v_hat) + ADAM_EPS) + ADAM_WD * w_ref[...])

    blk = pl.BlockSpec((tr, C), lambda i: (i, 0))
    args = (w, g, m, v) if q is None else (w, g, q, m, v)
    nout = 3 if q is None else 4
    outs, got = _ride_call(
        body, ride, name=name, grid=(R // tr,), in_specs=[blk] * len(args), out_specs=[blk] * nout,
        out_shape=[jax.ShapeDtypeStruct((R, C), F32)] * nout, args=args, semantics=("parallel",))
    return outs if ride is None else (outs, got)


def _sibling_exchange(srcs, name, ride=()):
    n, k = len(srcs), len(ride)

    def body(*refs):
        src_refs, dst_refs = refs[:n], refs[n + 2 * k:2 * n + 2 * k]
        send_sems, recv_sems = refs[2 * n + 3 * k:2 * n + 3 * k + 2]
        x, y, c = lax.axis_index("x"), lax.axis_index("y"), lax.axis_index("c")
        if k:
            start, finish = _ride_ops(ride, refs[n:n + k], refs[2 * n + 2 * k:2 * n + 3 * k], *refs[2 * n + 3 * k + 2:])
            start()
        copies = [pltpu.make_async_remote_copy(src_ref=src_refs[t], dst_ref=dst_refs[t], send_sem=send_sems.at[t],
                                               recv_sem=recv_sems.at[t], device_id=(x, y, 1 - c), device_id_type=MESH)
                  for t in range(n)]
        for cp in copies:
            cp.start()
        for cp in copies:
            cp.wait()
        if k:
            finish()

    res = pl.pallas_call(
        body, name=name, in_specs=[_ANY] * (n + 2 * k), out_specs=[_ANY] * (n + k),
        out_shape=[jax.ShapeDtypeStruct(s.shape, s.dtype) for s in srcs] + [jax.ShapeDtypeStruct(it.dst.shape, it.dst.dtype) for it in ride],
        input_output_aliases={n + k + t: n + t for t in range(k)},
        scratch_shapes=[pltpu.SemaphoreType.DMA((n,)), pltpu.SemaphoreType.DMA((n,))] + (_RIDE_SEMS(k) if k else []),
    )(*srcs, *[it.src for it in ride], *[it.dst for it in ride])
    return res[:n], {it.name: o for it, o in zip(ride, res[n:])}


def _allreduce_small(v, name):
    R = v.shape[0]

    def body(v_ref, o_ref, slots, send_sems, recv_sems):
        x, y, c = lax.axis_index("x"), lax.axis_index("y"), lax.axis_index("c")
        me = 4 * x + 2 * y + c
        slots[me] = v_ref[...]
        sent = []
        for r in range(1, 8):
            fx, fy, fc = (r >> 2) & 1, (r >> 1) & 1, r & 1
            px, py, pc = (1 - x) if fx else x, (1 - y) if fy else y, (1 - c) if fc else c
            peer = 4 * px + 2 * py + pc

            def copy(slot, r=r, px=px, py=py, pc=pc):
                return pltpu.make_async_remote_copy(
                    src_ref=v_ref, dst_ref=slots.at[slot], send_sem=send_sems.at[r - 1], recv_sem=recv_sems.at[r - 1],
                    device_id=(px, py, pc), device_id_type=MESH)

            cp = copy(me)
            cp.start()
            sent.append((cp, copy(peer)))
        for cp, arrival in sent:
            cp.wait_send()
            arrival.wait_recv()
        acc = slots[0]
        for k in range(1, 8):
            acc = acc + slots[k]
        o_ref[...] = acc

    vm = pl.BlockSpec(memory_space=pltpu.VMEM)
    return pl.pallas_call(
        body, name=name, in_specs=[vm], out_specs=vm, out_shape=jax.ShapeDtypeStruct((R, 128), F32),
        scratch_shapes=[pltpu.VMEM((8, R, 128), F32), pltpu.SemaphoreType.DMA((7,)), pltpu.SemaphoreType.DMA((7,))],
    )(v)


def _sum4(recv, name, tr=512):
    _, R, W = recv.shape
    tr = _tile(R, tr)
    assert R % tr == 0

    def body(r_ref, o_ref):
        o_ref[...] = ((r_ref[0].astype(F32) + r_ref[1].astype(F32)) + r_ref[2].astype(F32)) + r_ref[3].astype(F32)

    return pl.pallas_call(
        body, name=name, grid=(R // tr,), in_specs=[pl.BlockSpec((4, tr, W), lambda i: (0, i, 0))],
        out_specs=pl.BlockSpec((tr, W), lambda i: (i, 0)), out_shape=jax.ShapeDtypeStruct((R, W), F32),
        compiler_params=_cparams("parallel"),
    )(recv)


W_NAMES = ("ffn1_norm", "ffn1_w1", "ffn1_w3", "ffn1_w2", "mix_norm", "w_in", "na_q_norm", "na_k_norm", "na_rpb",
           "gla_gf_up", "gla_gf_bias", "gla_gb_up", "gla_gb_bias", "gla_out_norm", "mla_cq_norm", "mla_ckv_norm",
           "mla_w_uq", "mla_w_ukv", "mla_q_norm", "mla_k_norm", "w_br_na", "w_br_gla", "w_br_mla", "w_out",
           "ffn2_norm", "ffn2_w1", "ffn2_w3", "ffn2_w2")
SHARDED = {"ffn1_w1": 2, "ffn1_w3": 2, "ffn1_w2": 1, "w_in": 2, "gla_gf_up": 2, "gla_gb_up": 2, "mla_w_uq": 2,
           "mla_w_ukv": 2, "w_br_na": 2, "w_br_gla": 2, "w_br_mla": 2, "w_out": 1, "ffn2_w1": 2, "ffn2_w3": 2,
           "ffn2_w2": 1}
REPLICATED = tuple(n for n in W_NAMES if n not in SHARDED)
FFN_W = ("ffn1_w1", "ffn1_w3", "ffn1_w2", "ffn2_w1", "ffn2_w3", "ffn2_w2")


def _win_layout(w, D):
    z = lambda n: jnp.zeros(w.shape[:-1] + (n,), w.dtype)
    return jnp.concatenate([w[..., O_GATES:], w[..., :O_GFL], w[..., O_CQ:O_KR], w[..., O_GFL:O_CQ], z(96),
                            w[..., O_KR:O_KR + 32], z(32), w[..., O_KR + 32:O_KR + 64], z(32)], axis=-1)


def _win_unlayout(dw, D):
    g = 3 * D
    return jnp.concatenate([dw[..., g:g + O_GFL], dw[..., g + 3584:g + 3616], dw[..., g + 3072:g + 3584],
                            dw[..., g + 3712:g + 3744], dw[..., g + 3776:g + 3808], dw[..., :g]], axis=-1)


def _uq_layout(w):
    s = w.shape[:-1]
    w = w.reshape(s + (MLA_HEADS, MLA_QK))
    z = jnp.zeros(s + (MLA_HEADS, 32), w.dtype)
    return jnp.concatenate([w[..., :160], z, w[..., 160:], z], axis=-1).reshape(s + (MLA_HEADS * MLA_SLOT,))


def _uq_unlayout(dw):
    s = dw.shape[:-1]
    dw = dw.reshape(s + (MLA_HEADS, MLA_SLOT))
    return jnp.concatenate([dw[..., :160], dw[..., 192:224]], axis=-1).reshape(s + (MLA_HEADS * MLA_QK,))


def _slot_layout(g):
    z = jnp.zeros(g.shape[:-1] + (32,), g.dtype)
    return jnp.concatenate([g[..., :160], z, g[..., 160:], z], axis=-1)


def _slot_unlayout(g):
    return jnp.concatenate([g[..., :160], g[..., 192:224]], axis=-1)


def _layer_fwd(x, w, cos, sg, rides):
    D = x.shape[1]
    NA, GL, ML, LR, KR = 3 * D, 3 * D + 1536, 3 * D + 3072, 3 * D + 3584, 3 * D + 3712
    got = {}
    x1, f1, arrived = _ffn_fwd(x, w["ffn1_norm"], w["ffn1_w1"], w["ffn1_w3"], w["ffn1_w2"], "ffn1", ride=rides.get("ffn1_up"))
    got.update(arrived)
    h = _rms_fwd(x1, w["mix_norm"], "mix_rms")
    nz = w["w_in"].shape[1]
    z, arrived = _mm([(h, w["w_in"])], "nn", F32, "w_in", tm=512, tn=_tile(nz, 1280), ride=rides.get("w_in", []))
    got.update(arrived)
    qn, kn, vb = _na_prep(z, NA, w["na_gq"], w["na_gk"], "na_prep")
    bias = _rpb_expand(w["na_rpb"], "rpb_expand")
    y_na, arrived = _na_attn(qn, kn, vb, bias, "na_attn", ride=rides.get("na_attn"))
    got.update(arrived)
    gfb = _gla_gates(z, LR, w["gla_wg"], w["gla_gbias"], "gla_gates")
    (o_f, o_b, s_f, s_b), arrived = _gla_fwd(z, GL, GL + 512, gfb, "gla_fwd", ride=rides.get("gla_fwd"))
    got.update(arrived)
    y_gla = _gla_post(o_f, o_b, z, GL + 1024, w["gla_out_norm"], "gla_post")
    q, k, v, cqn, ckvn = _mla_prep(z, ML, KR, w["mla_wuq"], w["mla_w_ukv"], w["mla_cq_norm"], w["mla_ckv_norm"],
                                   w["mla_gq"], w["mla_gk"], cos, sg, "mla_prep")
    y_mla, arrived = _mla_attn(q, k, v, "mla_attn", ride=rides.get("mla_attn"))
    got.update(arrived)
    mixed, p0, p1, p2 = _merge([y_na, y_gla, y_mla], [w["w_br_na"], w["w_br_gla"], w["w_br_mla"]], z, "merge")
    x2 = _mm([(mixed, w["w_out"])], "nn", F32, "w_out", tm=512, tn=1024, res=x1)
    x3, f2, _ = _ffn_fwd(x2, w["ffn2_norm"], got["ffn2_w1"], got["ffn2_w3"], got["ffn2_w2"], "ffn2")
    saved = dict(x=x, x1=x1, x2=x2, f1=f1, f2=f2, h=h, z=z, qn=qn, kn=kn, vb=vb, bias=bias, y_na=y_na, gfb=gfb, o_f=o_f,
                 o_b=o_b, s_f=s_f, s_b=s_b, y_gla=y_gla, q=q, k=k, v=v, cqn=cqn, ckvn=ckvn, y_mla=y_mla, mixed=mixed,
                 p0=p0, p1=p1, p2=p2)
    return x3, saved, got


def _split4(a, axis):
    n = a.shape[axis] // 4
    return jnp.stack([lax.slice_in_dim(a, j * n, (j + 1) * n, axis=axis) for j in range(4)]).astype(BF16)


def _layer_bwd(dx3, w, sv, cos, sg, bufs, recv, layer, prev, flush=False):
    D = dx3.shape[1]
    at_layer = lambda chip: (chip, layer)
    pick = lambda *names: [prev[n] for n in names if n in prev]
    recv = dict(recv)
    NA, GL, ML, LR, KR = 3 * D, 3 * D + 1536, 3 * D + 3072, 3 * D + 3584, 3 * D + 3712
    z = sv["z"]
    g = {}
    dx2, g["ffn2_norm"], (g["ffn2_w1"], g["ffn2_w3"], g["ffn2_w2"]), got = _ffn_bwd(
        dx3, sv["x2"], w["ffn2_norm"], w["ffn2_w1"], w["ffn2_w3"], w["ffn2_w2"], sv["f2"], "ffn2",
        (bufs["ffn2_w1"], bufs["ffn2_w3"], bufs["ffn2_w2"]), layer, ride_down=pick("ffn1_w1"), ride_dh=pick("ffn1_w3"))
    recv.update(got)
    dmixed = _mm([(dx2, w["w_out"])], "nt", F32, "w_out_dx", tm=512, tn=512)
    g["w_out"] = _mm([(sv["mixed"], dx2)], "tn", F32, "w_out_dw", tm=D, tn=256)
    d0, d1, d2, dgates = _merge_bwd(dmixed, [sv["p0"], sv["p1"], sv["p2"]], z, "merge_bwd")
    dys = []
    for d, y, nm, dt in ((d0, sv["y_na"], "w_br_na", BF16), (d1, sv["y_gla"], "w_br_gla", F32), (d2, sv["y_mla"], "w_br_mla", BF16)):
        dys.append(_mm([(d, w[nm])], "nt", dt, nm + "_dy", tm=512, tn=512))
        g[nm] = _mm([(y, d)], "tn", F32, nm + "_dw", tm=512, tn=512)
    (dqn, dkn, dvn, dbias), got = _na_attn_bwd(sv["qn"], sv["kn"], sv["vb"], sv["bias"], dys[0], "na_attn_bwd",
                                               ride=pick("ffn1_w2", "mla_w_uq", "mla_w_ukv", "gla_gf_up", "gla_gb_up"))
    recv.update(got)
    dz_na, g["na_gq"], g["na_gk"] = _na_prep_bwd(z, NA, w["na_gq"], w["na_gk"], dqn, dkn, dvn, "na_prep_bwd")
    g["na_rpb"] = _rpb_reduce(dbias, "rpb_reduce")
    do, dgr, g["gla_out_norm"] = _gla_post_bwd(sv["o_f"], sv["o_b"], z, GL + 1024, w["gla_out_norm"], dys[1], "gla_post_bwd")
    (dqk_f, dv_f, dg_f, dqk_b, dv_b, dg_b), _ = _gla_bwd(z, GL, GL + 512, sv["gfb"], do, sv["s_f"], sv["s_b"], "gla_bwd")
    dz_gla = _gla_assemble(dqk_f, dqk_b, dv_f, dv_b, dgr, "gla_assemble")
    dpre, g["gla_gbias"] = _gla_gates_bwd(z, LR, w["gla_wg"], w["gla_gbias"], dg_f, dg_b, "gla_gates_bwd")
    g["gla_wg"] = _mm([(_V(z, LR, 128), dpre)], "tn", F32, "gla_wg_dw", tm=128, tn=512)
    dz_lr = _mm([(dpre, w["gla_wg"])], "nt", BF16, "gla_wg_dz", tm=512, tn=128)
    own = lambda n: _Ride(n, g[n], at_layer, recv[n], at_layer)
    (dq, dk, dv), got = _mla_attn_bwd(sv["q"], sv["k"], sv["v"], dys[2], "mla_attn_bwd",
                                      ride=pick("w_in") + [own("ffn2_w1"), own("ffn2_w3")])
    recv.update(got)
    dqf, dkv, dz_kr, g["mla_gq"], g["mla_gk"] = _mla_prep_bwd(
        z, KR, sv["cqn"], sv["ckvn"], w["mla_wuq"], w["mla_w_ukv"], w["mla_gq"], w["mla_gk"], cos, sg, dq, dk, dv, "mla_prep_bwd")
    g["mla_wuq"] = _mm([(sv["cqn"], dqf)], "tn", F32, "mla_wuq_dw", tm=256, tn=512)
    g["mla_w_ukv"] = _mm([(sv["ckvn"], dkv)], "tn", F32, "mla_wukv_dw", tm=256, tn=512)
    dcqn = _mm([(dqf, w["mla_wuq"])], "nt", F32, "mla_wuq_dx", tm=512, tn=256)
    dckvn = _mm([(dkv, w["mla_w_ukv"])], "nt", F32, "mla_wukv_dx", tm=512, tn=256)
    dz_cq, dg_cq = _rms_bwd(_V(z, ML, MLA_RANK), w["mla_cq_norm"], dcqn, "mla_cq_rms_bwd", out_dtype=BF16)
    dz_ckv, dg_ckv = _rms_bwd(_V(z, ML + MLA_RANK, MLA_RANK), w["mla_ckv_norm"], dckvn, "mla_ckv_rms_bwd", out_dtype=BF16)
    g["mla_cq_norm"], g["mla_ckv_norm"] = dg_cq[0:1], dg_ckv[0:1]
    segs = ((dgates, 0, 3 * D), (dz_na, NA, 1536), (dz_gla, GL, 1536), (dz_cq, ML, MLA_RANK), (dz_ckv, ML + MLA_RANK, MLA_RANK),
            (dz_lr, LR, 128), (dz_kr, KR, 128))
    dh, got = _mm([(dz, _V(w["w_in"], c0, wd)) for dz, c0, wd in segs], "nt", F32, "w_in_dx", tm=512, tn=512,
                  ride=[own("ffn2_w2")])
    recv.update(got)
    g["w_in"] = jnp.concatenate(
        [_mm([(sv["h"], dz)], "tn", F32, f"w_in_dw{i}", tm=D, tn=_tile(wd, 256)) for i, (dz, _, wd) in enumerate(segs)], axis=1)
    dx1, dg_mix = _rms_bwd(sv["x1"], w["mix_norm"], dh, "mix_rms_bwd", dres=dx2)
    g["mix_norm"] = dg_mix[0:1]
    late = dict(w_in=_split4(_win_unlayout(g["w_in"], D), 1), mla_w_uq=_split4(_uq_unlayout(g["mla_wuq"]), 1),
                mla_w_ukv=_split4(g["mla_w_ukv"], 1), gla_gf_up=_split4(g["gla_wg"][0:GLA_RANK, 0:256], 1),
                gla_gb_up=_split4(g["gla_wg"][GLA_RANK:2 * GLA_RANK, 256:512], 1))
    ride = [_Ride(n, _split4(g[n], SHARDED[n] - 1), lambda chip: (chip,), recv[n], at_layer)
            for n in ("w_out", "w_br_na", "w_br_gla", "w_br_mla")]
    ride_dh = None
    if flush:
        ride_dh, late = [_Ride(n, late[n], lambda chip: (chip,), recv[n], at_layer) for n in late], {}
    dx, g["ffn1_norm"], (g["ffn1_w1"], g["ffn1_w3"], g["ffn1_w2"]), got = _ffn_bwd(
        dx1, sv["x"], w["ffn1_norm"], w["ffn1_w1"], w["ffn1_w3"], w["ffn1_w2"], sv["f1"], "ffn1",
        (bufs["ffn1_w1"], bufs["ffn1_w3"], bufs["ffn1_w2"]), layer, ride_down=ride, ride_dh=ride_dh)
    recv.update(got)
    return dx, g, late, recv


def _head_fold(width, period, lo=0):
    f = np.zeros((width, 128), np.float32)
    f[np.arange(width), lo + np.arange(width) % period] = 1.0
    return f


def kernel(x, ffn1_norm, ffn1_w1, ffn1_w3, ffn1_w2, mix_norm, w_in, na_q_norm, na_k_norm, na_rpb, gla_gf_up, gla_gf_bias,
           gla_gb_up, gla_gb_bias, gla_out_norm, mla_cq_norm, mla_ckv_norm, mla_w_uq, mla_w_ukv, mla_q_norm, mla_k_norm,
           w_br_na, w_br_gla, w_br_mla, w_out, ffn2_norm, ffn2_w1, ffn2_w3, ffn2_w2, loss_target, m_ffn1_norm, m_ffn1_w1,
           m_ffn1_w3, m_ffn1_w2, m_mix_norm, m_w_in, m_na_q_norm, m_na_k_norm, m_na_rpb, m_gla_gf_up, m_gla_gf_bias,
           m_gla_gb_up, m_gla_gb_bias, m_gla_out_norm, m_mla_cq_norm, m_mla_ckv_norm, m_mla_w_uq, m_mla_w_ukv,
           m_mla_q_norm, m_mla_k_norm, m_w_br_na, m_w_br_gla, m_w_br_mla, m_w_out, m_ffn2_norm, m_ffn2_w1, m_ffn2_w3,
           m_ffn2_w2, v_ffn1_norm, v_ffn1_w1, v_ffn1_w3, v_ffn1_w2, v_mix_norm, v_w_in, v_na_q_norm, v_na_k_norm,
           v_na_rpb, v_gla_gf_up, v_gla_gf_bias, v_gla_gb_up, v_gla_gb_bias, v_gla_out_norm, v_mla_cq_norm,
           v_mla_ckv_norm, v_mla_w_uq, v_mla_w_ukv, v_mla_q_norm, v_mla_k_norm, v_w_br_na, v_w_br_gla, v_w_br_mla,
           v_w_out, v_ffn2_norm, v_ffn2_w1, v_ffn2_w3, v_ffn2_w2):
    given = dict(locals())
    wts = {n: given[n] for n in W_NAMES}
    mom = {n: given["m_" + n] for n in W_NAMES}
    var = {n: given["v_" + n] for n in W_NAMES}
    xs, target = x[0], loss_target[0]
    S, D = xs.shape
    L = ffn1_norm.shape[0]

    sh_names = tuple(SHARDED)
    LATE = ("ffn2_w1", "ffn2_w3", "ffn2_w2")
    HEAVY = ("ffn1_w1", "ffn1_w3", "ffn1_w2", "w_in")
    LIGHT = tuple(n for n in sh_names if n not in LATE + HEAVY)
    shard_shape = lambda n: tuple(wts[n].shape[1:])

    def gather_items(names, l):
        return [_Ride(n, wts[n][l].astype(BF16), lambda chip: (), lax.empty((4,) + shard_shape(n), BF16), lambda chip: (chip,),
                      halves=n in HEAVY + LATE) for n in names]

    cols = lambda p: jnp.concatenate([p[j] for j in range(4)], axis=-1)

    def layer_weights(gl, l):
        r1 = lambda a: a[l][None]
        wg = jnp.zeros((128, 2 * GLA_HEADS * GLA_DK), BF16)
        wg = wg.at[0:GLA_RANK, 0:256].set(cols(gl["gla_gf_up"])).at[GLA_RANK:2 * GLA_RANK, 256:512].set(cols(gl["gla_gb_up"]))
        return dict(
            ffn1_norm=r1(ffn1_norm), ffn1_w1=gl["ffn1_w1"], ffn1_w3=gl["ffn1_w3"], ffn1_w2=gl["ffn1_w2"],
            mix_norm=r1(mix_norm), w_in=_win_layout(cols(gl["w_in"]), D),
            na_gq=jnp.tile(na_q_norm[l], NA_HEADS)[None], na_gk=jnp.tile(na_k_norm[l], NA_HEADS)[None], na_rpb=na_rpb[l],
            gla_wg=wg, gla_gbias=jnp.concatenate([gla_gf_bias[l], gla_gb_bias[l]])[None], gla_out_norm=r1(gla_out_norm),
            mla_cq_norm=r1(mla_cq_norm), mla_ckv_norm=r1(mla_ckv_norm), mla_wuq=_uq_layout(cols(gl["mla_w_uq"])),
            mla_w_ukv=cols(gl["mla_w_ukv"]), mla_gq=_slot_layout(mla_q_norm[l])[None], mla_gk=_slot_layout(mla_k_norm[l])[None],
            w_br_na=cols(gl["w_br_na"]), w_br_gla=cols(gl["w_br_gla"]), w_br_mla=cols(gl["w_br_mla"]),
            w_out=gl["w_out"].reshape(D, D), ffn2_norm=r1(ffn2_norm))

    half = MLA_ROPE // 2
    inv = ROPE_THETA ** (-jnp.arange(half, dtype=F32) / half)
    ang = jnp.arange(S, dtype=F32)[:, None] * inv[None, :]
    cos = jnp.tile(jnp.cos(ang), (1, 4))
    sg = jnp.concatenate([-jnp.sin(ang), -jnp.sin(ang), jnp.sin(ang), jnp.sin(ang)], axis=1)

    arrived = _exchange(gather_items(HEAVY + LIGHT, 0), "weights_all_gather")
    xc, saved, layers = xs, [], []
    for l in range(L):
        w = layer_weights(arrived, l)
        rides = {"ffn1_up": gather_items(("ffn2_w1", "ffn2_w3"), l), "w_in": gather_items(("ffn2_w2",), l)}
        if l + 1 < L:
            rides["w_in"] += gather_items(LIGHT, l + 1)
            rides["na_attn"] = gather_items(("ffn1_w1", "ffn1_w3"), l + 1)
            rides["gla_fwd"] = gather_items(("ffn1_w2",), l + 1)
            rides["mla_attn"] = gather_items(("w_in",), l + 1)
        xc, sv, arrived = _layer_fwd(xc, w, cos, sg, rides)
        saved.append(sv)
        layers.append({**w, **{n: arrived[n] for n in LATE}})
    dy, loss_part = _loss_head(xc, target, "loss_head")

    bufs = {n: lax.empty((4, L) + shard_shape(n), BF16) for n in FFN_W}
    recv = {n: lax.empty((4, L) + shard_shape(n), BF16) for n in sh_names}
    dx, g, prev = dy, [None] * L, {}
    for l in reversed(range(L)):
        dx, g[l], late, recv = _layer_bwd(dx, layers[l], saved[l], cos, sg, bufs, recv, l, prev, flush=l == 0)
        bufs = {n: g[l][n] for n in FFN_W}
        at_l = functools.partial(lambda chip, l: (chip, l), l=l)
        prev = {n: _Ride(n, bufs[n], at_l, recv[n], at_l) for n in ("ffn1_w1", "ffn1_w3", "ffn1_w2")}
        prev.update({n: _Ride(n, late[n], lambda chip: (chip,), recv[n], at_l) for n in late})

    stk = lambda n: jnp.stack([g[l][n] for l in range(L)])
    gs = {n: stk(n)[:, 0] for n in ("ffn1_norm", "mix_norm", "mla_cq_norm", "mla_ckv_norm", "ffn2_norm")}
    gs["na_q_norm"] = _fold(stk("na_gq"), "na_gq_fold", _head_fold(NA_W, NA_DH))[:, :NA_DH]
    gs["na_k_norm"] = _fold(stk("na_gk"), "na_gk_fold", _head_fold(NA_W, NA_DH))[:, :NA_DH]
    gs["na_rpb"] = stk("na_rpb")
    gbias = _fold(stk("gla_gbias"), "gla_gbias_fold")
    gs["gla_gf_bias"], gs["gla_gb_bias"] = gbias[:, :256], gbias[:, 256:]
    gs["gla_out_norm"] = _fold(stk("gla_out_norm"), "gla_out_norm_fold")
    gs["mla_q_norm"] = _slot_unlayout(_fold(stk("mla_gq"), "mla_gq_fold"))
    gs["mla_k_norm"] = _slot_unlayout(_fold(stk("mla_gk"), "mla_gk_fold"))

    as2d = lambda a: a.reshape(-1, a.shape[-1])
    gsh, upd = {}, {}
    DONE = ("ffn2_w1", "ffn2_w3", "ffn2_w2", "w_out", "w_br_na", "w_br_gla", "w_br_mla")

    def reduce_and_update(names, with_exchange):
        mine = [_sum4(recv[n].reshape(4, -1, recv[n].shape[-1]), "grads_chip_sum_" + n) for n in names]
        other, arrived = _sibling_exchange(mine, "grads_sibling_exchange", ride=with_exchange)
        for n, p, q in zip(names, mine, other):
            outs = [o.reshape(wts[n].shape) for o in _adamw(as2d(wts[n]), p, as2d(mom[n]), as2d(var[n]), "adamw_" + n, q=q)]
            gsh[n], upd[n] = outs[0], outs[1:]
        return arrived

    recv.update(reduce_and_update(DONE, list(prev.values())))
    reduce_and_update(tuple(n for n in sh_names if n not in DONE), [])

    small_shapes = [wts[n].shape[1:] for n in REPLICATED]
    n_small = sum(int(np.prod(s)) for s in small_shapes) * L
    flat = jnp.concatenate([gs[n].reshape(-1) for n in REPLICATED] + [loss_part.reshape(-1)])
    pad = -flat.shape[0] % 1024
    red = _allreduce_small(jnp.pad(flat, (0, pad)).reshape(-1, 128), "small_all_reduce").reshape(-1)
    loss = jnp.sum(red[n_small:n_small + 1024])
    off = 0
    for n, s in zip(REPLICATED, small_shapes):
        cnt = int(np.prod(s)) * L
        gsh[n] = red[off:off + cnt].reshape((L,) + tuple(s))
        off += cnt

    pk = lambda d: jnp.pad(jnp.concatenate([d[n].reshape(-1) for n in REPLICATED]), (0, -n_small % 1024)).reshape(-1, 128)
    small = _adamw(pk(wts), pk(gsh), pk(mom), pk(var), "adamw_replicated")
    off = 0
    for n, s in zip(REPLICATED, small_shapes):
        cnt = int(np.prod(s)) * L
        upd[n] = [o.reshape(-1)[off:off + cnt].reshape((L,) + tuple(s)) for o in small]
        off += cnt

    return (loss, dx[None], *[gsh[n] for n in W_NAMES], *[upd[n][0] for n in W_NAMES], *[upd[n][1] for n in W_NAMES],
            *[upd[n][2] for n in W_NAMES])
```

```python
import functools
import math

import numpy as np
import jax
import jax.numpy as jnp
from jax import lax
from jax.experimental import pallas as pl
from jax.experimental.pallas import tpu as pltpu

F32 = jnp.float32
BF16 = jnp.bfloat16
HI = lax.Precision.HIGHEST
MESH = pl.DeviceIdType.MESH

EPS = 1e-6
GRID_W = 64
NA_HEADS, NA_DH, NA_WIN_R, NA_WIN_C = 8, 64, 8, 16
NA_W = NA_HEADS * NA_DH
GLA_HEADS, GLA_DK, GLA_DV, GLA_RANK, GLA_TAU, GLA_CHUNK = 4, 64, 128, 16, 16.0, 64
MLA_HEADS, MLA_RANK, MLA_NOPE, MLA_ROPE, MLA_V = 4, 256, 128, 64, 128
MLA_QK = MLA_NOPE + MLA_ROPE
MLA_SLOT = 256
MLA_QSCALE = MLA_QK ** -0.5 * math.log2(math.e)
ROPE_THETA = 10000.0
ADAM_LR, ADAM_B1, ADAM_B2, ADAM_EPS, ADAM_WD, ADAM_STEP = 0.001, 0.9, 0.999, 1e-08, 0.01, 10

V7X_VMEM_BYTES = 64 * 2**20
VMEM_LIMIT = V7X_VMEM_BYTES - 12 * 2**20
NEG = -1e30

O_GQ, O_GFL, O_CQ, O_KR, O_GATES = 1536, 3072, 3104, 3616, 3680


_ANY = pl.BlockSpec(memory_space=pl.ANY)


def _cparams(*sem):
    return pltpu.CompilerParams(dimension_semantics=sem, vmem_limit_bytes=VMEM_LIMIT)


class _V:
    def __init__(self, arr, c0=0, w=None, lead=()):
        self.arr, self.c0, self.lead = arr, c0, tuple(lead)
        assert arr.ndim == 2 + len(self.lead), (arr.shape, lead)
        self.w = arr.shape[-1] if w is None else w

    @property
    def rows(self):
        return self.arr.shape[-2]

    def spec(self, br, bc, rfn, cfn):
        assert self.c0 % bc == 0 and self.w % bc == 0, (self.c0, self.w, bc)
        off, lead = self.c0 // bc, self.lead

        def index(*g):
            return tuple(g[0] if e == "b" else e for e in lead) + (rfn(*g), off + cfn(*g))

        return pl.BlockSpec((None,) * len(lead) + (br, bc), index)


def _v(x):
    return x if isinstance(x, _V) else _V(x)


_DN = {"nn": (((1,), (0,)), ((), ())), "nt": (((1,), (1,)), ((), ())), "tn": (((0,), (0,)), ((), ()))}


def _dot(a, b, mode="nn", prec=None):
    return lax.dot_general(a, b, _DN[mode], preferred_element_type=F32, precision=prec)


def _tile(n, cap):
    if n <= cap:
        return n
    for t in range(cap - cap % 128, 0, -128):
        if n % t == 0:
            return t
    return n


def _mm(pairs, mode, out_dtype, name, *, tm, tn, res=None, scale=None, batch=1, into=None, ride=None):
    pairs = [(_v(a), _v(b)) for a, b in pairs]
    a0, b0 = pairs[0]
    M = a0.w if mode == "tn" else a0.rows
    N = b0.rows if mode == "nt" else b0.w
    tm, tn = _tile(M, tm), _tile(N, tn)
    assert M % tm == 0 and N % tn == 0, (name, M, N, tm, tn)
    n = len(pairs)

    def body(*refs):
        o_ref = refs[-1]
        acc = None
        for i in range(n):
            d = _dot(refs[2 * i][...].astype(BF16), refs[2 * i + 1][...].astype(BF16), mode)
            acc = d if acc is None else acc + d
        if scale is not None:
            acc = acc * scale
        if res is not None:
            acc = acc + refs[2 * n][...]
        o_ref[...] = acc.astype(o_ref.dtype)

    zero = lambda b, i, j: 0
    row = lambda b, i, j: i
    col = lambda b, i, j: j
    in_specs, args = [], []
    for a, b in pairs:
        in_specs.append(a.spec(a.rows, tm, zero, row) if mode == "tn" else a.spec(tm, a.w, row, zero))
        in_specs.append(b.spec(tn, b.w, col, zero) if mode == "nt" else b.spec(b.rows, tn, zero, col))
        args += [a.arr, b.arr]
    if res is not None:
        in_specs.append(pl.BlockSpec((tm, tn), lambda b, i, j: (i, j)))
        args.append(res)
    aliases = {}
    if into is None:
        out = jax.ShapeDtypeStruct(((batch,) if batch > 1 else ()) + (M, N), out_dtype)
        out_view = _V(out, lead=("b",) if batch > 1 else ())
    else:
        buf, lead = into
        assert buf.shape[-2:] == (M, N) and buf.dtype == out_dtype, (name, buf.shape, M, N)
        out = jax.ShapeDtypeStruct(buf.shape, buf.dtype)
        out_view = _V(out, lead=lead)
        aliases = {len(args): 0}
        in_specs.append(_ANY)
        args.append(buf)
    (res,), got = _ride_call(
        body, ride, name=name, grid=(batch, M // tm, N // tn), in_specs=in_specs, out_specs=[out_view.spec(tm, tn, row, col)],
        out_shape=[out], aliases=aliases, args=args, semantics=("parallel", "parallel", "parallel"))
    return res if ride is None else (res, got)


def _rms_fwd(x, g, name, tm=512):
    x = _v(x)
    S, D = x.rows, x.w
    tm = min(tm, S)

    def body(x_ref, g_ref, o_ref):
        xv = x_ref[...]
        y = xv * lax.rsqrt(jnp.mean(xv * xv, axis=-1, keepdims=True) + EPS)
        o_ref[...] = (y * g_ref[...]).astype(o_ref.dtype)

    return pl.pallas_call(
        body, name=name, grid=(S // tm,),
        in_specs=[x.spec(tm, D, lambda i: i, lambda i: 0), pl.BlockSpec((1, D), lambda i: (0, 0))],
        out_specs=pl.BlockSpec((tm, D), lambda i: (i, 0)),
        out_shape=jax.ShapeDtypeStruct((S, D), BF16), compiler_params=_cparams("parallel"),
    )(x.arr, g)


def _rms_bwd(x, g, dh, name, dres=None, out_dtype=F32, tm=512):
    x = _v(x)
    S, D = x.rows, x.w
    tm = min(tm, S)

    def body(*refs):
        if dres is None:
            x_ref, g_ref, dh_ref, dx_ref, dg_ref = refs
        else:
            x_ref, g_ref, dh_ref, dr_ref, dx_ref, dg_ref = refs
        xv = x_ref[...]
        rstd = lax.rsqrt(jnp.mean(xv * xv, axis=-1, keepdims=True) + EPS)
        xhat = xv * rstd
        dhv = dh_ref[...].astype(F32)
        dxhat = dhv * g_ref[...]
        dx = rstd * (dxhat - xhat * jnp.mean(dxhat * xhat, axis=-1, keepdims=True))
        if dres is not None:
            dx = dx + dr_ref[...]
        dx_ref[...] = dx.astype(dx_ref.dtype)

        @pl.when(pl.program_id(0) == 0)
        def _():
            dg_ref[...] = jnp.zeros_like(dg_ref)

        dg_ref[0:1, :] += jnp.sum(dhv * xhat, axis=0, keepdims=True)

    in_specs = [x.spec(tm, D, lambda i: i, lambda i: 0), pl.BlockSpec((1, D), lambda i: (0, 0)),
                pl.BlockSpec((tm, D), lambda i: (i, 0))]
    args = [x.arr, g, dh]
    if dres is not None:
        in_specs.append(pl.BlockSpec((tm, D), lambda i: (i, 0)))
        args.append(dres)
    return pl.pallas_call(
        body, name=name, grid=(S // tm,), in_specs=in_specs,
        out_specs=[pl.BlockSpec((tm, D), lambda i: (i, 0)), pl.BlockSpec((8, D), lambda i: (0, 0))],
        out_shape=[jax.ShapeDtypeStruct((S, D), out_dtype), jax.ShapeDtypeStruct((8, D), F32)],
        compiler_params=_cparams("arbitrary"),
    )(*args)


FFN_SUB = 256


def _one_ahead(n, matmuls, rest):
    res = matmuls(0)
    for i in range(1, n):
        nxt = matmuls(i)
        rest(i - 1, res)
        res = nxt
    rest(n - 1, res)


def _ffn_up(h, w1, w3, name, tm=1024, ride=None):
    S, D = h.shape
    NC, _, F4 = w1.shape
    tm = min(tm, S)
    sub = math.gcd(FFN_SUB, tm)
    rows = lambda i: slice(i * sub, (i + 1) * sub)

    def body(h_ref, w1_ref, w3_ref, a_ref, b_ref, u_ref):
        def matmuls(i):
            hv = h_ref[rows(i), :]
            return _dot(hv, w1_ref[...]), _dot(hv, w3_ref[...])

        def rest(i, ab):
            a, b = ab
            a_ref[rows(i), :] = a.astype(BF16)
            b_ref[rows(i), :] = b.astype(BF16)
            u_ref[rows(i), :] = (a * jax.nn.sigmoid(a) * b).astype(BF16)

        _one_ahead(tm // sub, matmuls, rest)

    blk = pl.BlockSpec((None, tm, F4), lambda i, j: (j, i, 0))
    wblk = pl.BlockSpec((None, D, F4), lambda i, j: (j, 0, 0))
    return _ride_call(
        body, ride, name=name, grid=(S // tm, NC), in_specs=[pl.BlockSpec((tm, D), lambda i, j: (i, 0)), wblk, wblk],
        out_specs=[blk, blk, blk], out_shape=[jax.ShapeDtypeStruct((NC, S, F4), BF16)] * 3, args=(h, w1, w3),
        semantics=("parallel", "parallel"))


def _ffn_down_bwd(dxo, w2, a, b, name, tm=1024, ride=None):
    S, D = dxo.shape
    NC, F4, _ = w2.shape
    tm = min(tm, S)
    sub = math.gcd(FFN_SUB, tm)
    rows = lambda i: slice(i * sub, (i + 1) * sub)

    def body(dx_ref, w2_ref, a_ref, b_ref, da_ref, db_ref):
        def matmuls(i):
            return _dot(dx_ref[rows(i), :].astype(BF16), w2_ref[...], "nt")

        def rest(i, du):
            du = du * 0.5
            av = a_ref[rows(i), :].astype(F32)
            sig = jax.nn.sigmoid(av)
            da_ref[rows(i), :] = (du * b_ref[rows(i), :].astype(F32) * (sig * (1.0 + av * (1.0 - sig)))).astype(BF16)
            db_ref[rows(i), :] = (du * av * sig).astype(BF16)

        _one_ahead(tm // sub, matmuls, rest)

    blk = pl.BlockSpec((None, tm, F4), lambda i, j: (j, i, 0))
    return _ride_call(
        body, ride, name=name, grid=(S // tm, NC),
        in_specs=[pl.BlockSpec((tm, D), lambda i, j: (i, 0)), pl.BlockSpec((None, F4, D), lambda i, j: (j, 0, 0)), blk, blk],
        out_specs=[blk, blk], out_shape=[jax.ShapeDtypeStruct((NC, S, F4), BF16)] * 2, args=(dxo, w2, a, b),
        semantics=("parallel", "parallel"))


def _ffn_fwd(x, g, w1, w3, w2, tag, ride=None):
    h = _rms_fwd(x, g, f"{tag}_rms")
    (a, b, u), got = _ffn_up(h, w1, w3, f"{tag}_up", ride=ride)
    nc = w2.shape[0]
    y = _mm([(_V(u, lead=(j,)), _V(w2, lead=(j,))) for j in range(nc)], "nn", F32, f"{tag}_down", tm=512, tn=1024, res=x, scale=0.5)
    return y, (h, a, b, u), got


def _ffn_bwd(dxo, x, g, w1, w3, w2, saved, tag, bufs, layer, ride_down=None, ride_dh=None):
    h, a, b, u = saved
    nc, D, F4 = w1.shape
    (da, db), got = _ffn_down_bwd(dxo, w2, a, b, f"{tag}_down_bwd", ride=ride_down)
    into = lambda k: (bufs[k], ("b", layer))
    dw2 = _mm([(_V(u, lead=("b",)), dxo)], "tn", BF16, f"{tag}_dw2", tm=F4, tn=512, scale=0.5, batch=nc, into=into(2))
    dw1 = _mm([(h, _V(da, lead=("b",)))], "tn", BF16, f"{tag}_dw1", tm=D, tn=F4, batch=nc, into=into(0))
    dw3 = _mm([(h, _V(db, lead=("b",)))], "tn", BF16, f"{tag}_dw3", tm=D, tn=F4, batch=nc, into=into(1))
    pairs = [(_V(da, lead=(j,)), _V(w1, lead=(j,))) for j in range(nc)] + [(_V(db, lead=(j,)), _V(w3, lead=(j,))) for j in range(nc)]
    dh, got_dh = _mm(pairs, "nt", F32, f"{tag}_dh", tm=512, tn=512, ride=ride_dh or [])
    dx, dg = _rms_bwd(x, g, dh, f"{tag}_rms_bwd", dres=dxo)
    return dx, dg[0:1], (dw1, dw3, dw2), {**got, **got_dh}


def _iota(shape, dim):
    return lax.broadcasted_iota(jnp.int32, shape, dim)


def _head_block_ones(n, shift):
    return jnp.where((_iota((n, n), 0) >> shift) == (_iota((n, n), 1) >> shift), 1.0, 0.0).astype(BF16)


def _dot_split(x, ones01):
    hi = x.astype(BF16)
    lo = (x - hi.astype(F32)).astype(BF16)
    return _dot(hi, ones01) + _dot(lo, ones01)


def _lane_mask(width, lo, size):
    l = _iota((1, width), 1)
    return jnp.where((l >= lo) & (l < lo + size), 1.0, 0.0).astype(F32)


def _acc_rows(acc_ref, val, first):
    r = val.shape[0]
    part = jnp.sum(val.reshape(r // 8, 8, val.shape[1]), axis=0)

    @pl.when(first)
    def _():
        acc_ref[...] = part

    @pl.when(jnp.logical_not(first))
    def _():
        acc_ref[...] += part


_FLIPS = ((1, 0), (0, 1), (1, 1))


class _Ride:
    def __init__(self, name, src, src_at, dst, dst_at, halves=False):
        self.name, self.src, self.src_at, self.dst, self.dst_at, self.halves = name, src, src_at, dst, dst_at, halves
        assert not halves or (src.ndim == 2 and src.shape[0] % 32 == 0), (name, src.shape)


_RIDE_SEMS = lambda n: [pltpu.SemaphoreType.DMA((6, n)), pltpu.SemaphoreType.DMA((6, n)), pltpu.SemaphoreType.DMA((n,))]


def _ride_ops(ride, srcs, dsts, send_sems, recv_sems, local_sems):
    x, y, c = lax.axis_index("x"), lax.axis_index("y"), lax.axis_index("c")
    me = 2 * x + y
    at = lambda ref, idx: ref.at[idx] if idx else ref
    local, sends, arrivals, passes = [], [], [], []
    for t, it in enumerate(ride):
        local.append(pltpu.make_async_copy(at(srcs[t], it.src_at(me)), at(dsts[t], it.dst_at(me)), local_sems.at[t]))
    for r, (fx, fy) in enumerate(_FLIPS):
        px, py = (1 - x) if fx else x, (1 - y) if fy else y
        peer = 2 * px + py
        for t, it in enumerate(ride):
            if it.halves:
                h = it.src.shape[0] // 2
                mine = pl.ds(pl.multiple_of(c * h, 16), h)
                theirs = pl.ds(pl.multiple_of((1 - c) * h, 16), h)
                far = dict(send_sem=send_sems.at[r, t], recv_sem=recv_sems.at[r, t], device_id=(px, py, c), device_id_type=MESH)
                near = dict(send_sem=send_sems.at[3 + r, t], recv_sem=recv_sems.at[3 + r, t], device_id=(x, y, 1 - c),
                            device_id_type=MESH)
                sends.append(pltpu.make_async_remote_copy(src_ref=srcs[t].at[mine], dst_ref=dsts[t].at[me, mine], **far))
                arrivals.append(pltpu.make_async_remote_copy(src_ref=srcs[t].at[mine], dst_ref=dsts[t].at[peer, mine], **far))
                passes.append((pltpu.make_async_remote_copy(src_ref=dsts[t].at[peer, mine], dst_ref=dsts[t].at[peer, mine], **near),
                               pltpu.make_async_remote_copy(src_ref=dsts[t].at[peer, theirs], dst_ref=dsts[t].at[peer, theirs], **near)))
            else:
                far = dict(src_ref=at(srcs[t], it.src_at(peer)), send_sem=send_sems.at[r, t], recv_sem=recv_sems.at[r, t],
                           device_id=(px, py, c), device_id_type=MESH)
                sends.append(pltpu.make_async_remote_copy(dst_ref=at(dsts[t], it.dst_at(me)), **far))
                arrivals.append(pltpu.make_async_remote_copy(dst_ref=at(dsts[t], it.dst_at(peer)), **far))
                passes.append(None)

    def start():
        for cp in local + sends:
            cp.start()

    def finish():
        for cp, arrival, onward in zip(sends, arrivals, passes):
            cp.wait_send()
            arrival.wait_recv()
            if onward is not None:
                onward[0].start()
        for onward in passes:
            if onward is not None:
                onward[0].wait_send()
                onward[1].wait_recv()
        for cp in local:
            cp.wait()

    return start, finish


def _grid_edges(*ns):
    def edges():
        first = last = None
        for d, n in enumerate(ns):
            i = pl.program_id(d)
            f, l = i == 0, i == n - 1
            first = f if first is None else jnp.logical_and(first, f)
            last = l if last is None else jnp.logical_and(last, l)
        return first, last
    return edges


def _ride_call(body, ride, *, name, grid, in_specs, out_specs, out_shape, args, scratch_shapes=(), semantics=(), aliases=None):
    scratch_shapes, aliases = list(scratch_shapes), dict(aliases or {})
    if not ride:
        outs = pl.pallas_call(body, name=name, grid=grid, in_specs=in_specs, out_specs=out_specs, out_shape=out_shape,
                              scratch_shapes=scratch_shapes, input_output_aliases=aliases,
                              compiler_params=_cparams(*semantics))(*args)
        return outs, {}
    n_in, n_out, n_sc, n = len(in_specs), len(out_specs), len(scratch_shapes), len(ride)
    edges = _grid_edges(*grid)

    def wrapped(*refs):
        ins, srcs = refs[:n_in], refs[n_in:n_in + n]
        o0 = n_in + 2 * n
        outs, dsts = refs[o0:o0 + n_out], refs[o0 + n_out:o0 + n_out + n]
        scratch = refs[o0 + n_out + n:o0 + n_out + n + n_sc]
        start, finish = _ride_ops(ride, srcs, dsts, *refs[o0 + n_out + n + n_sc:])
        first, last = edges()
        pl.when(first)(start)
        body(*ins, *outs, *scratch)
        pl.when(last)(finish)

    aliases.update({n_in + n + t: n_out + t for t in range(n)})
    res = pl.pallas_call(
        wrapped, name=name, grid=grid, in_specs=list(in_specs) + [_ANY] * (2 * n), out_specs=list(out_specs) + [_ANY] * n,
        out_shape=list(out_shape) + [jax.ShapeDtypeStruct(it.dst.shape, it.dst.dtype) for it in ride],
        input_output_aliases=aliases, scratch_shapes=scratch_shapes + _RIDE_SEMS(n),
        compiler_params=_cparams(*(["arbitrary"] * len(grid))),
    )(*args, *[it.src for it in ride], *[it.dst for it in ride])
    return res[:n_out], {it.name: o for it, o in zip(ride, res[n_out:])}


def _exchange(ride, name):
    n = len(ride)

    def body(*refs):
        start, finish = _ride_ops(ride, refs[:n], refs[2 * n:3 * n], *refs[3 * n:])
        start()
        finish()

    res = pl.pallas_call(
        body, name=name, in_specs=[_ANY] * (2 * n), out_specs=[_ANY] * n,
        out_shape=[jax.ShapeDtypeStruct(it.dst.shape, it.dst.dtype) for it in ride],
        input_output_aliases={n + t: t for t in range(n)}, scratch_shapes=_RIDE_SEMS(n),
    )(*[it.src for it in ride], *[it.dst for it in ride])
    return {it.name: o for it, o in zip(ride, res)}


def _na_prep(z, c0, gq, gk, name, tm=512):
    S = z.shape[0]
    tm = min(tm, S)
    zv = _V(z, c0, 3 * NA_W)

    def body(z_ref, gq_ref, gk_ref, q_ref, k_ref, v_ref):
        bd = _head_block_ones(NA_W, 6)

        def norm(xv, gv):
            ms = _dot_split(xv * xv, bd) * (1.0 / NA_DH)
            return xv * lax.rsqrt(ms + EPS) * gv

        q_ref[...] = (norm(z_ref[:, 0:NA_W], gq_ref[...]) * (NA_DH ** -0.5)).astype(BF16)
        k_ref[...] = norm(z_ref[:, NA_W:2 * NA_W], gk_ref[...]).astype(BF16)
        v_ref[...] = z_ref[:, 2 * NA_W:3 * NA_W].astype(BF16)

    blk = pl.BlockSpec((tm, NA_W), lambda i: (i, 0))
    gspec = pl.BlockSpec((1, NA_W), lambda i: (0, 0))
    return pl.pallas_call(
        body, name=name, grid=(S // tm,),
        in_specs=[zv.spec(tm, 3 * NA_W, lambda i: i, lambda i: 0), gspec, gspec],
        out_specs=[blk, blk, blk], out_shape=[jax.ShapeDtypeStruct((S, NA_W), BF16)] * 3,
        compiler_params=_cparams("parallel"),
    )(z, gq, gk)


def _na_prep_bwd(z, c0, gq, gk, dqn, dkn, dv, name, tm=512):
    S = z.shape[0]
    tm = min(tm, S)
    zv = _V(z, c0, 3 * NA_W)

    def body(z_ref, gq_ref, gk_ref, dq_ref, dk_ref, dv_ref, dz_ref, dgq_ref, dgk_ref):
        bd = _head_block_ones(NA_W, 6)
        first = pl.program_id(0) == 0

        def norm_bwd(xv, gv, dy, dg_ref):
            ms = _dot_split(xv * xv, bd) * (1.0 / NA_DH)
            rstd = lax.rsqrt(ms + EPS)
            xhat = xv * rstd
            dxhat = dy * gv
            proj = _dot_split(dxhat * xhat, bd) * (1.0 / NA_DH)
            _acc_rows(dg_ref, dy * xhat, first)
            return rstd * (dxhat - xhat * proj)

        dz_ref[:, 0:NA_W] = norm_bwd(z_ref[:, 0:NA_W], gq_ref[...], dq_ref[...] * (NA_DH ** -0.5), dgq_ref).astype(BF16)
        dz_ref[:, NA_W:2 * NA_W] = norm_bwd(z_ref[:, NA_W:2 * NA_W], gk_ref[...], dk_ref[...], dgk_ref).astype(BF16)
        dz_ref[:, 2 * NA_W:3 * NA_W] = dv_ref[...].astype(BF16)

    blk = pl.BlockSpec((tm, NA_W), lambda i: (i, 0))
    gspec = pl.BlockSpec((1, NA_W), lambda i: (0, 0))
    acc = pl.BlockSpec((8, NA_W), lambda i: (0, 0))
    return pl.pallas_call(
        body, name=name, grid=(S // tm,),
        in_specs=[zv.spec(tm, 3 * NA_W, lambda i: i, lambda i: 0), gspec, gspec, blk, blk, blk],
        out_specs=[pl.BlockSpec((tm, 3 * NA_W), lambda i: (i, 0)), acc, acc],
        out_shape=[jax.ShapeDtypeStruct((S, 3 * NA_W), BF16), jax.ShapeDtypeStruct((8, NA_W), F32),
                   jax.ShapeDtypeStruct((8, NA_W), F32)],
        compiler_params=_cparams("arbitrary"),
    )(z, gq, gk, dqn, dkn, dv)


def _na_onehot():
    qc = np.arange(GRID_W)[:, None]
    kc = np.arange(GRID_W)[None, :]
    c0 = np.clip(qc - NA_WIN_C // 2, 0, GRID_W - NA_WIN_C)
    valid = (kc >= c0) & (kc < c0 + NA_WIN_C)
    dc = kc - qc + (NA_WIN_C - 1)
    e = np.zeros((32, GRID_W, GRID_W), np.float32)
    for d in range(2 * NA_WIN_C - 1):
        e[d] = valid & (dc == d)
    return e.reshape(32, GRID_W * GRID_W), valid.reshape(1, -1)


def _rpb_expand(rpb, name):
    e, valid = _na_onehot()
    negmask = np.where(valid, 0.0, NEG).astype(np.float32)
    nd = 2 * NA_WIN_R - 1
    r2 = jnp.pad(rpb.reshape(NA_HEADS * nd, 2 * NA_WIN_C - 1), ((0, 128 - NA_HEADS * nd), (0, 1)))

    def body(r_ref, e_ref, m_ref, o_ref):
        o_ref[...] = _dot(r_ref[...], e_ref[...], prec=HI) + m_ref[...]

    t = pl.pallas_call(body, name=name, out_shape=jax.ShapeDtypeStruct((128, GRID_W * GRID_W), F32))(
        r2, jnp.asarray(e), jnp.asarray(negmask))
    t = t[:NA_HEADS * nd].reshape(NA_HEADS, nd, GRID_W, GRID_W)
    return jnp.stack([jnp.concatenate([t[:, b + w] for w in range(NA_WIN_R)], axis=-1) for b in range(NA_WIN_R)], axis=1)


def _rpb_reduce(dbias, name):
    e, _ = _na_onehot()
    nd = 2 * NA_WIN_R - 1
    et = np.zeros((GRID_W * GRID_W, 128), np.float32)
    et[:, :32] = e.T
    sel = np.zeros((128, NA_HEADS * NA_WIN_R * NA_WIN_R), np.float32)
    for h in range(NA_HEADS):
        for b in range(NA_WIN_R):
            for w in range(NA_WIN_R):
                sel[h * nd + b + w, (h * NA_WIN_R + b) * NA_WIN_R + w] = 1.0
    x = dbias.reshape(NA_HEADS, NA_WIN_R, GRID_W, NA_WIN_R, GRID_W).transpose(0, 1, 3, 2, 4).reshape(-1, GRID_W * GRID_W)

    def body(x_ref, et_ref, sel_ref, o_ref):
        g = _dot(x_ref[...], et_ref[...], prec=HI)
        o_ref[...] = _dot(sel_ref[...], g, prec=HI)

    out = pl.pallas_call(body, name=name, out_shape=jax.ShapeDtypeStruct((128, 128), F32))(x, jnp.asarray(et), jnp.asarray(sel))
    return out[:NA_HEADS * nd, :2 * NA_WIN_C - 1].reshape(NA_HEADS, nd, 2 * NA_WIN_C - 1)


def _na_base(r, rows):
    return jnp.clip(r - NA_WIN_R // 2, 0, rows - NA_WIN_R) - r + (NA_WIN_R - 1)


def _na_probs(q_ref, k_ref, bias_ref, P, r0w):
    sl = [slice(128 * pp, 128 * pp + 128) for pp in range(P)]
    m = [_lane_mask(128, 64 * hh, 64) for hh in range(2)]
    kw = [k_ref[r0w, sl[pp]] for pp in range(P)]
    units = [(pp, hh) for pp in range(P) for hh in range(2)]
    qm = {u: (q_ref[:, sl[u[0]]].astype(F32) * m[u[1]]).astype(BF16) for u in units}
    s = {u: _dot(qm[u], kw[u[0]], "nt") + bias_ref[2 * u[0] + u[1], 0] for u in units}
    p = {}
    for u in units:
        e = jnp.exp(s[u] - jnp.max(s[u], axis=-1, keepdims=True))
        p[u] = e / jnp.sum(e, axis=-1, keepdims=True)
    return sl, m, kw, units, qm, p


NA_FWD_PAIRS = 4
NA_BWD_PAIRS = 2


def _na_attn(qn, kn, vb, bias, name, ride=None):
    S = qn.shape[0]
    rows = S // GRID_W
    nk = NA_WIN_R * GRID_W
    P = NA_FWD_PAIRS
    W = 128 * P

    def body(q_ref, k_ref, v_ref, b_ref, o_ref):
        r = pl.program_id(1)
        r0w = pl.ds(pl.multiple_of(jnp.clip(r - NA_WIN_R // 2, 0, rows - NA_WIN_R) * GRID_W, GRID_W), nk)
        sl, m, _, units, _, p = _na_probs(q_ref, k_ref, b_ref, P, r0w)
        o = {u: _dot(p[u].astype(BF16), v_ref[r0w, sl[u[0]]]) for u in units}
        for pp in range(P):
            o_ref[:, sl[pp]] = (o[pp, 0] * m[0] + o[pp, 1] * m[1]).astype(BF16)

    full = pl.BlockSpec((S, W), lambda g, r: (0, g))
    (o,), got = _ride_call(
        body, ride, name=name, grid=(NA_HEADS // (2 * P), rows),
        in_specs=[pl.BlockSpec((GRID_W, W), lambda g, r: (r, g)), full, full,
                  pl.BlockSpec((2 * P, 1, GRID_W, nk), lambda g, r: (g, _na_base(r, rows), 0, 0))],
        out_specs=[pl.BlockSpec((GRID_W, W), lambda g, r: (r, g))],
        out_shape=[jax.ShapeDtypeStruct((S, NA_W), BF16)], args=(qn, kn, vb, bias), semantics=("parallel", "arbitrary"))
    return o, got


def _na_attn_bwd(qn, kn, vb, bias, do, name, ride=None):
    S = qn.shape[0]
    rows = S // GRID_W
    nk = NA_WIN_R * GRID_W
    P = NA_BWD_PAIRS
    W = 128 * P

    def body(q_ref, k_ref, v_ref, b_ref, do_ref, dq_ref, dk_ref, dv_ref, db_ref):
        r = pl.program_id(1)

        @pl.when(r == 0)
        def _():
            dk_ref[...] = jnp.zeros_like(dk_ref)
            dv_ref[...] = jnp.zeros_like(dv_ref)

        r0w = pl.ds(pl.multiple_of(jnp.clip(r - NA_WIN_R // 2, 0, rows - NA_WIN_R) * GRID_W, GRID_W), nk)
        fresh = jnp.logical_or(r <= NA_WIN_R // 2, r > rows - NA_WIN_R // 2)
        sl, m, kw, units, qm, p = _na_probs(q_ref, k_ref, b_ref, P, r0w)
        dom = {u: (do_ref[:, sl[u[0]]].astype(F32) * m[u[1]]).astype(BF16) for u in units}
        dp = {u: _dot(dom[u], v_ref[r0w, sl[u[0]]], "nt") for u in units}
        dvw = {u: _dot(p[u].astype(BF16), dom[u], "tn") for u in units}
        ds = {u: p[u] * (dp[u] - jnp.sum(p[u] * dp[u], axis=-1, keepdims=True)) for u in units}

        @pl.when(fresh)
        def _():
            for u in units:
                db_ref[2 * u[0] + u[1], 0] = ds[u]

        @pl.when(jnp.logical_not(fresh))
        def _():
            for u in units:
                db_ref[2 * u[0] + u[1], 0] += ds[u]

        dsb = {u: ds[u].astype(BF16) for u in units}
        dq = {u: _dot(dsb[u], kw[u[0]]) for u in units}
        dkw = {u: _dot(dsb[u], qm[u], "tn") for u in units}
        for pp in range(P):
            dq_ref[:, sl[pp]] = dq[pp, 0] * m[0] + dq[pp, 1] * m[1]
            dk_ref[r0w, sl[pp]] += dkw[pp, 0] + dkw[pp, 1]
            dv_ref[r0w, sl[pp]] += dvw[pp, 0] + dvw[pp, 1]

    qblk = pl.BlockSpec((GRID_W, W), lambda g, r: (r, g))
    full = pl.BlockSpec((S, W), lambda g, r: (0, g))
    bblk = pl.BlockSpec((2 * P, 1, GRID_W, nk), lambda g, r: (g, _na_base(r, rows), 0, 0))
    return _ride_call(
        body, ride, name=name, grid=(NA_HEADS // (2 * P), rows),
        in_specs=[qblk, full, full, bblk, qblk], out_specs=[qblk, full, full, bblk],
        out_shape=[jax.ShapeDtypeStruct((S, NA_W), F32)] * 3 + [jax.ShapeDtypeStruct((NA_HEADS, NA_WIN_R, GRID_W, nk), F32)],
        args=(qn, kn, vb, bias, do), semantics=("parallel", "arbitrary"))


def _logsig(x):
    return jnp.minimum(x, 0.0) - jnp.log(1.0 + jnp.exp(-jnp.abs(x)))


def _gla_gates(z, c0, wg, bias, name, tm=512):
    S = z.shape[0]
    tm = min(tm, S)
    zv = _V(z, c0, 128)
    W = 2 * GLA_HEADS * GLA_DK

    def body(z_ref, w_ref, b_ref, o_ref):
        pre = _dot(z_ref[...].astype(BF16), w_ref[...]) + b_ref[...]
        o_ref[...] = _logsig(pre) * (1.0 / GLA_TAU)

    return pl.pallas_call(
        body, name=name, grid=(S // tm,),
        in_specs=[zv.spec(tm, 128, lambda i: i, lambda i: 0), pl.BlockSpec((128, W), lambda i: (0, 0)),
                  pl.BlockSpec((1, W), lambda i: (0, 0))],
        out_specs=pl.BlockSpec((tm, W), lambda i: (i, 0)), out_shape=jax.ShapeDtypeStruct((S, W), F32),
        compiler_params=_cparams("parallel"),
    )(z, wg, bias)


def _gla_gates_bwd(z, c0, wg, bias, dg_f, dg_b, name, tm=512):
    S = z.shape[0]
    tm = min(tm, S)
    zv = _V(z, c0, 128)
    W = 2 * GLA_HEADS * GLA_DK

    def body(z_ref, w_ref, b_ref, dgf_ref, dgb_ref, dp_ref, db_ref):
        pre = _dot(z_ref[...].astype(BF16), w_ref[...]) + b_ref[...]
        dg = jnp.concatenate([dgf_ref[...], dgb_ref[...]], axis=-1)
        dpre = dg * (1.0 / GLA_TAU) * jax.nn.sigmoid(-pre)
        dp_ref[...] = dpre.astype(BF16)
        _acc_rows(db_ref, dpre, pl.program_id(0) == 0)

    half = pl.BlockSpec((tm, W // 2), lambda i: (i, 0))
    return pl.pallas_call(
        body, name=name, grid=(S // tm,),
        in_specs=[zv.spec(tm, 128, lambda i: i, lambda i: 0), pl.BlockSpec((128, W), lambda i: (0, 0)),
                  pl.BlockSpec((1, W), lambda i: (0, 0)), half, half],
        out_specs=[pl.BlockSpec((tm, W), lambda i: (i, 0)), pl.BlockSpec((8, W), lambda i: (0, 0))],
        out_shape=[jax.ShapeDtypeStruct((S, W), BF16), jax.ShapeDtypeStruct((8, W), F32)],
        compiler_params=_cparams("arbitrary"),
    )(z, wg, bias, dg_f, dg_b)


def _gla_chunk_terms(zqk, g, p, rev):
    C = GLA_CHUNK
    i, j = _iota((C, C), 0), _iota((C, C), 1)
    cum = jnp.where((j >= i) if rev else (j <= i), 1.0, 0.0).astype(F32)
    q2 = zqk[:, 128 * p:128 * p + 128] * (GLA_DK ** -0.5)
    k2 = zqk[:, 256 + 128 * p:256 + 128 * p + 128]
    b2 = _dot(cum, g[:, 128 * p:128 * p + 128], prec=HI)
    bl2 = b2[0:1] if rev else b2[C - 1:C]
    eb = jnp.exp(b2)
    qe2 = q2 * eb
    ke2 = k2 * jnp.exp(-b2)
    kend2 = k2 * jnp.exp(bl2 - b2)
    dec2 = jnp.exp(bl2)
    tri = (j > i) if rev else (j <= i)
    return b2, bl2, eb, qe2, ke2, kend2, dec2, tri


def _row_to_col(row):
    eye = _iota((128, 128), 0) == _iota((128, 128), 1)
    return jnp.sum(jnp.where(eye, row, 0.0), axis=1, keepdims=True)


def _col_to_row(col):
    eye = _iota((128, 128), 0) == _iota((128, 128), 1)
    return jnp.sum(jnp.where(eye, col, 0.0), axis=0, keepdims=True)


GLA_GROUP = 4


def _gla_fwd(z, c_qk, c_v, gfb, name, ride=None):
    S = z.shape[0]
    C = GLA_CHUNK
    n = S // C
    G = math.gcd(GLA_GROUP, n)
    nb, GC = n // G, G * C
    WQK = 2 * GLA_HEADS * GLA_DK
    WV = GLA_HEADS * GLA_DV
    zqk, zvv = _V(z, c_qk, WQK), _V(z, c_v, WV)

    def body(qkf_ref, vf_ref, gf_ref, qkb_ref, vb_ref, gb_ref, of_ref, ob_ref, sf_ref, sb_ref, stf, stb):
        @pl.when(pl.program_id(0) == 0)
        def _():
            stf[...] = jnp.zeros_like(stf)
            stb[...] = jnp.zeros_like(stb)

        dirs = ((False, qkf_ref, vf_ref, gf_ref, of_ref, sf_ref, stf), (True, qkb_ref, vb_ref, gb_ref, ob_ref, sb_ref, stb))
        rows = lambda gi: slice(gi * C, (gi + 1) * C)
        pairs = [(d, gi, p) for d in range(2) for gi in range(G) for p in range(GLA_HEADS // 2)]
        heads = [(d, gi, p, hh) for d, gi, p in pairs for hh in range(2)]
        mask = [_lane_mask(128, 64 * hh, 64) for hh in range(2)]
        terms = {(d, gi, p): _gla_chunk_terms(dirs[d][1][rows(gi), :], dirs[d][3][rows(gi), :], p, dirs[d][0])
                 for d, gi, p in pairs}
        dec_col = {k: _row_to_col(t[6]) for k, t in terms.items()}
        vh = {(d, gi, h): dirs[d][2][rows(gi), 128 * h:128 * h + 128].astype(BF16)
              for d in range(2) for gi in range(G) for h in range(GLA_HEADS)}
        qm = {(d, gi, p, hh): (terms[d, gi, p][3] * mask[hh]).astype(BF16) for d, gi, p, hh in heads}
        a_raw = {(d, gi, p, hh): _dot(qm[d, gi, p, hh], terms[d, gi, p][4].astype(BF16), "nt") for d, gi, p, hh in heads}
        upd = {(d, gi, p, hh): _dot((terms[d, gi, p][5] * mask[hh]).astype(BF16), vh[d, gi, 2 * p + hh], "tn")
               for d, gi, p, hh in heads}
        intra = {(d, gi, p, hh): _dot(jnp.where(terms[d, gi, p][7], a_raw[d, gi, p, hh], 0.0).astype(BF16), vh[d, gi, 2 * p + hh])
                 for d, gi, p, hh in heads}
        state = {(d, h): dirs[d][6][h] for d in range(2) for h in range(GLA_HEADS)}
        for k in range(G):
            for d in range(2):
                gi = G - 1 - k if dirs[d][0] else k
                for p in range(GLA_HEADS // 2):
                    for hh in range(2):
                        h = 2 * p + hh
                        sp = state[d, h]
                        dirs[d][4][rows(gi), 128 * h:128 * h + 128] = intra[d, gi, p, hh] + _dot(qm[d, gi, p, hh], sp.astype(BF16))
                        dirs[d][5][gi, h] = sp
                        state[d, h] = dec_col[d, gi, p] * sp + upd[d, gi, p, hh]
        for d in range(2):
            for h in range(GLA_HEADS):
                dirs[d][6][h] = state[d, h]

    fw = lambda i: i
    bw = lambda i: nb - 1 - i
    zero = lambda i: 0
    in_specs = []
    for ix, col in ((fw, 0), (bw, 1)):
        in_specs += [zqk.spec(GC, WQK, ix, zero), zvv.spec(GC, WV, ix, zero),
                     pl.BlockSpec((GC, WQK // 2), functools.partial(lambda i, ix, col: (ix(i), col), ix=ix, col=col))]
    return _ride_call(
        body, ride, name=name, grid=(nb,), in_specs=in_specs,
        out_specs=[pl.BlockSpec((GC, WV), lambda i: (i, 0)), pl.BlockSpec((GC, WV), lambda i: (nb - 1 - i, 0)),
                   pl.BlockSpec((G, GLA_HEADS, 128, 128), lambda i: (i, 0, 0, 0)),
                   pl.BlockSpec((G, GLA_HEADS, 128, 128), lambda i: (nb - 1 - i, 0, 0, 0))],
        out_shape=[jax.ShapeDtypeStruct((S, WV), F32)] * 2 + [jax.ShapeDtypeStruct((n, GLA_HEADS, 128, 128), F32)] * 2,
        scratch_shapes=[pltpu.VMEM((GLA_HEADS, 128, 128), F32)] * 2, args=(z, z, gfb, z, z, gfb), semantics=("arbitrary",))


def _gla_bwd(z, c_qk, c_v, gfb, do, s_f, s_b, name, ride=None):
    S = z.shape[0]
    C = GLA_CHUNK
    n = S // C
    G = math.gcd(GLA_GROUP, n)
    nb, GC = n // G, G * C
    WQK = 2 * GLA_HEADS * GLA_DK
    WV = GLA_HEADS * GLA_DV
    zqk, zvv = _V(z, c_qk, WQK), _V(z, c_v, WV)

    def body(qkf_ref, vf_ref, gf_ref, dof_ref, sf_ref, qkb_ref, vb_ref, gb_ref, dob_ref, sb_ref,
             dqkf_ref, dvf_ref, dgf_ref, dqkb_ref, dvb_ref, dgb_ref, dstf, dstb):
        @pl.when(pl.program_id(0) == 0)
        def _():
            dstf[...] = jnp.zeros_like(dstf)
            dstb[...] = jnp.zeros_like(dstb)

        dirs = ((False, qkf_ref, vf_ref, gf_ref, dof_ref, sf_ref, dqkf_ref, dvf_ref, dgf_ref, dstf),
                (True, qkb_ref, vb_ref, gb_ref, dob_ref, sb_ref, dqkb_ref, dvb_ref, dgb_ref, dstb))
        rows = lambda gi: slice(gi * C, (gi + 1) * C)
        pairs = [(d, gi, p) for d in range(2) for gi in range(G) for p in range(GLA_HEADS // 2)]
        heads = [(d, gi, p, hh) for d, gi, p in pairs for hh in range(2)]
        mask = [_lane_mask(128, 64 * hh, 64) for hh in range(2)]
        T = {(d, gi, p): _gla_chunk_terms(dirs[d][1][rows(gi), :], dirs[d][3][rows(gi), :], p, dirs[d][0]) for d, gi, p in pairs}
        dec_col = {k: _row_to_col(t[6]) for k, t in T.items()}
        hd = lambda d, gi, p, hh: (d, gi, 2 * p + hh)
        vh = {(d, gi, h): dirs[d][2][rows(gi), 128 * h:128 * h + 128].astype(BF16)
              for d in range(2) for gi in range(G) for h in range(GLA_HEADS)}
        doh = {(d, gi, h): dirs[d][4][rows(gi), 128 * h:128 * h + 128].astype(BF16)
               for d in range(2) for gi in range(G) for h in range(GLA_HEADS)}
        sp = {(d, gi, h): dirs[d][5][gi, h] for d in range(2) for gi in range(G) for h in range(GLA_HEADS)}
        qm = {u: (T[u[:3]][3] * mask[u[3]]).astype(BF16) for u in heads}
        kem = {u: (T[u[:3]][4] * mask[u[3]]).astype(BF16) for u in heads}
        kendm = {u: (T[u[:3]][5] * mask[u[3]]).astype(BF16) for u in heads}
        a_raw = {u: _dot(qm[u], T[u[:3]][4].astype(BF16), "nt") for u in heads}
        da_raw = {u: _dot(doh[hd(*u)], vh[hd(*u)], "nt") for u in heads}
        w_upd = {u: _dot(qm[u], doh[hd(*u)], "tn") for u in heads}
        dqe_s = {u: _dot(doh[hd(*u)], sp[hd(*u)].astype(BF16), "nt") for u in heads}
        a = {u: jnp.where(T[u[:3]][7], a_raw[u], 0.0).astype(BF16) for u in heads}
        da = {u: jnp.where(T[u[:3]][7], da_raw[u], 0.0).astype(BF16) for u in heads}
        dqe = {u: _dot(da[u], kem[u]) + dqe_s[u] for u in heads}
        dke = {u: _dot(da[u], qm[u], "tn") for u in heads}
        dv_a = {u: _dot(a[u], doh[hd(*u)], "tn") for u in heads}
        ds = {}
        for d in range(2):
            cur = [dirs[d][9][h] for h in range(GLA_HEADS)]
            for gi in (range(G) if dirs[d][0] else reversed(range(G))):
                for p in range(GLA_HEADS // 2):
                    for hh in range(2):
                        h = 2 * p + hh
                        ds[d, gi, p, hh] = cur[h]
                        cur[h] = dec_col[d, gi, p] * cur[h] + w_upd[d, gi, p, hh]
            for h in range(GLA_HEADS):
                dirs[d][9][h] = cur[h]
        dsb = {u: ds[u].astype(BF16) for u in heads}
        dv_b = {u: _dot(kendm[u], dsb[u]) for u in heads}
        dkend = {u: _dot(vh[hd(*u)], dsb[u], "nt") * mask[u[3]] for u in heads}
        ddec = {u: _col_to_row(jnp.sum(ds[u] * sp[hd(*u)], axis=1, keepdims=True)) for u in heads}
        for u in heads:
            d, gi, h = hd(*u)
            dirs[d][7][rows(gi), 128 * h:128 * h + 128] = dv_a[u] + dv_b[u]
        i, j = _iota((C, C), 0), _iota((C, C), 1)
        for d, gi, p in pairs:
            rev = dirs[d][0]
            b2, bl2, eb, qe2, ke2, kend2, dec2, _ = T[d, gi, p]
            u0, u1 = (d, gi, p, 0), (d, gi, p, 1)
            dqe2, dke2, dkend2, ddec2 = dqe[u0] + dqe[u1], dke[u0] + dke[u1], dkend[u0] + dkend[u1], ddec[u0] + ddec[u1]
            dirs[d][6][rows(gi), 128 * p:128 * p + 128] = dqe2 * eb * (GLA_DK ** -0.5)
            dirs[d][6][rows(gi), 256 + 128 * p:256 + 128 * p + 128] = dke2 * jnp.exp(-b2) + dkend2 * jnp.exp(bl2 - b2)
            dkk = dkend2 * kend2
            dbl2 = jnp.sum(dkk, axis=0, keepdims=True) + ddec2 * dec2
            edge = _iota((C, 128), 0) == (0 if rev else C - 1)
            db2 = dqe2 * qe2 - dke2 * ke2 - dkk + jnp.where(edge, dbl2, 0.0)
            cum_t = jnp.where((j <= i) if rev else (j >= i), 1.0, 0.0).astype(F32)
            dirs[d][8][rows(gi), 128 * p:128 * p + 128] = _dot(cum_t, db2, prec=HI)

    fw = lambda i: nb - 1 - i
    bw = lambda i: i
    zero = lambda i: 0
    in_specs, out_specs = [], []
    for ix, col in ((fw, 0), (bw, 1)):
        blk = functools.partial(lambda i, ix: (ix(i), 0), ix=ix)
        in_specs += [zqk.spec(GC, WQK, ix, zero), zvv.spec(GC, WV, ix, zero),
                     pl.BlockSpec((GC, WQK // 2), functools.partial(lambda i, ix, col: (ix(i), col), ix=ix, col=col)),
                     pl.BlockSpec((GC, WV), blk),
                     pl.BlockSpec((G, GLA_HEADS, 128, 128), functools.partial(lambda i, ix: (ix(i), 0, 0, 0), ix=ix))]
        out_specs += [pl.BlockSpec((GC, WQK), blk), pl.BlockSpec((GC, WV), blk), pl.BlockSpec((GC, WQK // 2), blk)]
    shapes = [jax.ShapeDtypeStruct((S, WQK), F32), jax.ShapeDtypeStruct((S, WV), F32), jax.ShapeDtypeStruct((S, WQK // 2), F32)]
    return _ride_call(
        body, ride, name=name, grid=(nb,), in_specs=in_specs, out_specs=out_specs, out_shape=shapes * 2,
        scratch_shapes=[pltpu.VMEM((GLA_HEADS, 128, 128), F32)] * 2, args=(z, z, gfb, do, s_f, z, z, gfb, do, s_b),
        semantics=("arbitrary",))


def _gla_post(o_f, o_b, z, c_r, gn, name, tm=512):
    S, WV = o_f.shape
    tm = min(tm, S)
    zr = _V(z, c_r, WV)

    def body(of_ref, ob_ref, r_ref, g_ref, y_ref):
        gr = r_ref[...]
        sil = gr * jax.nn.sigmoid(gr)
        for h in range(GLA_HEADS):
            sl = slice(GLA_DV * h, GLA_DV * (h + 1))
            o = of_ref[:, sl] + ob_ref[:, sl]
            on = o * lax.rsqrt(jnp.mean(o * o, axis=-1, keepdims=True) + EPS) * g_ref[...]
            y_ref[:, sl] = (on * sil[:, sl]).astype(BF16)

    blk = pl.BlockSpec((tm, WV), lambda i: (i, 0))
    return pl.pallas_call(
        body, name=name, grid=(S // tm,),
        in_specs=[blk, blk, zr.spec(tm, WV, lambda i: i, lambda i: 0), pl.BlockSpec((1, GLA_DV), lambda i: (0, 0))],
        out_specs=blk, out_shape=jax.ShapeDtypeStruct((S, WV), BF16), compiler_params=_cparams("parallel"),
    )(o_f, o_b, z, gn)


def _gla_post_bwd(o_f, o_b, z, c_r, gn, dy, name, tm=512):
    S, WV = o_f.shape
    tm = min(tm, S)
    zr = _V(z, c_r, WV)

    def body(of_ref, ob_ref, r_ref, g_ref, dy_ref, do_ref, dr_ref, dg_ref):
        gr = r_ref[...]
        sig = jax.nn.sigmoid(gr)
        sil = gr * sig
        dyv = dy_ref[...].astype(F32)
        dgn = jnp.zeros((tm, GLA_DV), F32)
        for h in range(GLA_HEADS):
            sl = slice(GLA_DV * h, GLA_DV * (h + 1))
            o = of_ref[:, sl] + ob_ref[:, sl]
            rstd = lax.rsqrt(jnp.mean(o * o, axis=-1, keepdims=True) + EPS)
            xhat = o * rstd
            don = dyv[:, sl] * sil[:, sl]
            dr_ref[:, sl] = (dyv[:, sl] * xhat * g_ref[...] * (sig[:, sl] * (1.0 + gr[:, sl] * (1.0 - sig[:, sl])))).astype(BF16)
            dxhat = don * g_ref[...]
            do_ref[:, sl] = rstd * (dxhat - xhat * jnp.mean(dxhat * xhat, axis=-1, keepdims=True))
            dgn = dgn + don * xhat
        _acc_rows(dg_ref, dgn, pl.program_id(0) == 0)

    blk = pl.BlockSpec((tm, WV), lambda i: (i, 0))
    return pl.pallas_call(
        body, name=name, grid=(S // tm,),
        in_specs=[blk, blk, zr.spec(tm, WV, lambda i: i, lambda i: 0), pl.BlockSpec((1, GLA_DV), lambda i: (0, 0)), blk],
        out_specs=[blk, blk, pl.BlockSpec((8, GLA_DV), lambda i: (0, 0))],
        out_shape=[jax.ShapeDtypeStruct((S, WV), F32), jax.ShapeDtypeStruct((S, WV), BF16), jax.ShapeDtypeStruct((8, GLA_DV), F32)],
        compiler_params=_cparams("arbitrary"),
    )(o_f, o_b, z, gn, dy)


def _gla_assemble(dqk_f, dqk_b, dv_f, dv_b, dgr, name, tm=512):
    S = dqk_f.shape[0]
    tm = min(tm, S)

    def body(a_ref, b_ref, c_ref, d_ref, r_ref, o_ref):
        o_ref[:, 0:512] = (a_ref[...] + b_ref[...]).astype(BF16)
        o_ref[:, 512:1024] = (c_ref[...] + d_ref[...]).astype(BF16)
        o_ref[:, 1024:1536] = r_ref[...]

    blk = pl.BlockSpec((tm, 512), lambda i: (i, 0))
    return pl.pallas_call(
        body, name=name, grid=(S // tm,), in_specs=[blk] * 5, out_specs=pl.BlockSpec((tm, 1536), lambda i: (i, 0)),
        out_shape=jax.ShapeDtypeStruct((S, 1536), BF16), compiler_params=_cparams("parallel"),
    )(dqk_f, dqk_b, dv_f, dv_b, dgr)


def _rope(r, cos, sg):
    return r * cos + pltpu.roll(r, 64, 1) * sg


def _unrope(dy, cos, sg):
    return dy * cos + pltpu.roll(dy * sg, 64, 1)


def _mla_prep(z, c_q, c_kr, wuq, wukv, g_cq, g_ckv, g_q, g_k, cos, sg, name, tm=256):
    S = z.shape[0]
    tm = min(tm, S)
    zc, zk = _V(z, c_q, 2 * MLA_RANK), _V(z, c_kr, 128)
    inv = 1.0 / MLA_QK

    def body(zc_ref, zk_ref, wuq_ref, wukv_ref, gcq_ref, gckv_ref, gq_ref, gk_ref, cos_ref, sg_ref,
             q_ref, k_ref, v_ref, cqn_ref, ckvn_ref):
        def norm(xv, gv):
            return (xv * lax.rsqrt(jnp.mean(xv * xv, axis=-1, keepdims=True) + EPS) * gv).astype(BF16)

        cqn = norm(zc_ref[:, 0:MLA_RANK], gcq_ref[...])
        ckvn = norm(zc_ref[:, MLA_RANK:2 * MLA_RANK], gckv_ref[...])
        cqn_ref[...] = cqn
        ckvn_ref[...] = ckvn
        qf = _dot(cqn, wuq_ref[...])
        kv = _dot(ckvn, wukv_ref[...])
        kr = zk_ref[...]
        krss = jnp.sum(kr * kr, axis=-1, keepdims=True)
        cosv, sgv = cos_ref[...], sg_ref[...]
        gq, gk = gq_ref[...], gk_ref[...]
        for h in range(MLA_HEADS):
            qh = qf[:, MLA_SLOT * h:MLA_SLOT * (h + 1)]
            qhn = qh * lax.rsqrt(jnp.sum(qh * qh, axis=-1, keepdims=True) * inv + EPS) * gq
            q_ref[:, MLA_SLOT * h:MLA_SLOT * h + 128] = (qhn[:, 0:128] * MLA_QSCALE).astype(BF16)
            q_ref[:, MLA_SLOT * h + 128:MLA_SLOT * (h + 1)] = (_rope(qhn[:, 128:256], cosv, sgv) * MLA_QSCALE).astype(BF16)
            kn = kv[:, 256 * h:256 * h + 128]
            rstd = lax.rsqrt((jnp.sum(kn * kn, axis=-1, keepdims=True) + krss) * inv + EPS)
            k_ref[:, MLA_SLOT * h:MLA_SLOT * h + 128] = (kn * rstd * gk[:, 0:128]).astype(BF16)
            k_ref[:, MLA_SLOT * h + 128:MLA_SLOT * (h + 1)] = _rope(kr * rstd * gk[:, 128:256], cosv, sgv).astype(BF16)
            v_ref[:, 128 * h:128 * (h + 1)] = kv[:, 256 * h + 128:256 * (h + 1)].astype(BF16)

    row = lambda w: pl.BlockSpec((tm, w), lambda i: (i, 0))
    const = lambda r, w: pl.BlockSpec((r, w), lambda i: (0, 0))
    W = MLA_HEADS * MLA_SLOT
    return pl.pallas_call(
        body, name=name, grid=(S // tm,),
        in_specs=[zc.spec(tm, 2 * MLA_RANK, lambda i: i, lambda i: 0), zk.spec(tm, 128, lambda i: i, lambda i: 0),
                  const(MLA_RANK, W), const(MLA_RANK, W), const(1, MLA_RANK), const(1, MLA_RANK), const(1, MLA_SLOT),
                  const(1, MLA_SLOT), row(128), row(128)],
        out_specs=[row(W), row(W), row(MLA_HEADS * MLA_V), row(MLA_RANK), row(MLA_RANK)],
        out_shape=[jax.ShapeDtypeStruct((S, W), BF16), jax.ShapeDtypeStruct((S, W), BF16),
                   jax.ShapeDtypeStruct((S, MLA_HEADS * MLA_V), BF16), jax.ShapeDtypeStruct((S, MLA_RANK), BF16),
                   jax.ShapeDtypeStruct((S, MLA_RANK), BF16)],
        compiler_params=_cparams("parallel"),
    )(z, z, wuq, wukv, g_cq, g_ckv, g_q, g_k, cos, sg)


def _mla_prep_bwd(z, c_kr, cqn, ckvn, wuq, wukv, g_q, g_k, cos, sg, dq, dk, dv, name, tm=256):
    S = z.shape[0]
    tm = min(tm, S)
    zk = _V(z, c_kr, 128)
    inv = 1.0 / MLA_QK

    def body(zk_ref, cqn_ref, ckvn_ref, wuq_ref, wukv_ref, gq_ref, gk_ref, cos_ref, sg_ref, dq_ref, dk_ref, dv_ref,
             dqf_ref, dkv_ref, dkr_ref, dgq_ref, dgk_ref):
        first = pl.program_id(0) == 0
        qf = _dot(cqn_ref[...], wuq_ref[...])
        kv = _dot(ckvn_ref[...], wukv_ref[...])
        kr = zk_ref[...]
        krss = jnp.sum(kr * kr, axis=-1, keepdims=True)
        cosv, sgv = cos_ref[...], sg_ref[...]
        gq, gk = gq_ref[...], gk_ref[...]
        dkr = jnp.zeros((tm, 128), F32)
        dgq = jnp.zeros((tm, MLA_SLOT), F32)
        dgkn = jnp.zeros((tm, 128), F32)
        dgkr = jnp.zeros((tm, 128), F32)
        for h in range(MLA_HEADS):
            qh = qf[:, MLA_SLOT * h:MLA_SLOT * (h + 1)]
            rstd = lax.rsqrt(jnp.sum(qh * qh, axis=-1, keepdims=True) * inv + EPS)
            xhat = qh * rstd
            dyn = jnp.concatenate([dq_ref[:, MLA_SLOT * h:MLA_SLOT * h + 128],
                                   _unrope(dq_ref[:, MLA_SLOT * h + 128:MLA_SLOT * (h + 1)], cosv, sgv)], axis=-1)
            dxhat = dyn * gq
            dqf_ref[:, MLA_SLOT * h:MLA_SLOT * (h + 1)] = (
                rstd * (dxhat - xhat * (jnp.sum(dxhat * xhat, axis=-1, keepdims=True) * inv))).astype(BF16)
            dgq = dgq + dyn * xhat

            kn = kv[:, 256 * h:256 * h + 128]
            rstd = lax.rsqrt((jnp.sum(kn * kn, axis=-1, keepdims=True) + krss) * inv + EPS)
            xn, xr = kn * rstd, kr * rstd
            dyn_n = dk_ref[:, MLA_SLOT * h:MLA_SLOT * h + 128] * (1.0 / MLA_QSCALE)
            dyn_r = _unrope(dk_ref[:, MLA_SLOT * h + 128:MLA_SLOT * (h + 1)] * (1.0 / MLA_QSCALE), cosv, sgv)
            dxn, dxr = dyn_n * gk[:, 0:128], dyn_r * gk[:, 128:256]
            proj = (jnp.sum(dxn * xn, axis=-1, keepdims=True) + jnp.sum(dxr * xr, axis=-1, keepdims=True)) * inv
            dkv_ref[:, 256 * h:256 * h + 128] = (rstd * (dxn - xn * proj)).astype(BF16)
            dkv_ref[:, 256 * h + 128:256 * (h + 1)] = dv_ref[:, 128 * h:128 * (h + 1)].astype(BF16)
            dkr = dkr + rstd * (dxr - xr * proj)
            dgkn = dgkn + dyn_n * xn
            dgkr = dgkr + dyn_r * xr
        dkr_ref[...] = dkr.astype(BF16)
        _acc_rows(dgq_ref, dgq, first)
        _acc_rows(dgk_ref, jnp.concatenate([dgkn, dgkr], axis=-1), first)

    row = lambda w: pl.BlockSpec((tm, w), lambda i: (i, 0))
    const = lambda r, w: pl.BlockSpec((r, w), lambda i: (0, 0))
    W = MLA_HEADS * MLA_SLOT
    return pl.pallas_call(
        body, name=name, grid=(S // tm,),
        in_specs=[zk.spec(tm, 128, lambda i: i, lambda i: 0), row(MLA_RANK), row(MLA_RANK), const(MLA_RANK, W),
                  const(MLA_RANK, W), const(1, MLA_SLOT), const(1, MLA_SLOT), row(128), row(128), row(W), row(W),
                  row(MLA_HEADS * MLA_V)],
        out_specs=[row(W), row(W), row(128), const(8, MLA_SLOT), const(8, MLA_SLOT)],
        out_shape=[jax.ShapeDtypeStruct((S, W), BF16), jax.ShapeDtypeStruct((S, W), BF16), jax.ShapeDtypeStruct((S, 128), BF16),
                   jax.ShapeDtypeStruct((8, MLA_SLOT), F32), jax.ShapeDtypeStruct((8, MLA_SLOT), F32)],
        compiler_params=_cparams("arbitrary"),
    )(z, cqn, ckvn, wuq, wukv, g_q, g_k, cos, sg, dq, dk, dv)


def _exp2_rows(s2):
    e = jnp.exp2(s2 - jnp.max(s2, axis=-1, keepdims=True))
    return e, 1.0 / jnp.sum(e, axis=-1, keepdims=True)


MLA_SUB = 256


def _mla_attn(q, k, v, name, tq=512, ride=None):
    S = q.shape[0]
    tq = min(tq, S)
    sub = math.gcd(MLA_SUB, tq)
    rows = lambda i: slice(i * sub, (i + 1) * sub)

    def body(q_ref, k_ref, v_ref, o_ref):
        def rest(i, s2):
            e, rl = _exp2_rows(s2)
            o_ref[rows(i), :] = (_dot(e.astype(BF16), v_ref[...]) * rl).astype(BF16)

        _one_ahead(tq // sub, lambda i: _dot(q_ref[rows(i), :], k_ref[...], "nt"), rest)

    (o,), got = _ride_call(
        body, ride, name=name, grid=(MLA_HEADS, S // tq),
        in_specs=[pl.BlockSpec((tq, MLA_SLOT), lambda h, i: (i, h)), pl.BlockSpec((S, MLA_SLOT), lambda h, i: (0, h)),
                  pl.BlockSpec((S, MLA_V), lambda h, i: (0, h))],
        out_specs=[pl.BlockSpec((tq, MLA_V), lambda h, i: (i, h))],
        out_shape=[jax.ShapeDtypeStruct((S, MLA_HEADS * MLA_V), BF16)], args=(q, k, v), semantics=("parallel", "parallel"))
    return o, got


def _mla_attn_bwd(q, k, v, do, name, tq=512, ride=None):
    S = q.shape[0]
    tq = min(tq, S)
    sub = math.gcd(MLA_SUB, tq)
    rows = lambda i: slice(i * sub, (i + 1) * sub)
    scale = MLA_QK ** -0.5

    def body(q_ref, k_ref, v_ref, do_ref, dq_ref, dk_ref, dv_ref):
        @pl.when(pl.program_id(1) == 0)
        def _():
            dk_ref[...] = jnp.zeros_like(dk_ref)
            dv_ref[...] = jnp.zeros_like(dv_ref)

        def matmuls(i):
            return _dot(q_ref[rows(i), :], k_ref[...], "nt"), _dot(do_ref[rows(i), :], v_ref[...], "nt")

        def rest(i, s2_dp):
            s2, dp = s2_dp
            e, rl = _exp2_rows(s2)
            dp = dp * (scale * rl)
            ds = (e * (dp - jnp.sum(e * dp, axis=-1, keepdims=True) * rl)).astype(BF16)
            dq_ref[rows(i), :] = _dot(ds, k_ref[...])
            dk_ref[...] += _dot(ds, q_ref[rows(i), :], "tn")
            dv_ref[...] += _dot(e.astype(BF16), (do_ref[rows(i), :].astype(F32) * rl).astype(BF16), "tn")

        _one_ahead(tq // sub, matmuls, rest)

    W = MLA_HEADS * MLA_SLOT
    return _ride_call(
        body, ride, name=name, grid=(MLA_HEADS, S // tq),
        in_specs=[pl.BlockSpec((tq, MLA_SLOT), lambda h, i: (i, h)), pl.BlockSpec((S, MLA_SLOT), lambda h, i: (0, h)),
                  pl.BlockSpec((S, MLA_V), lambda h, i: (0, h)), pl.BlockSpec((tq, MLA_V), lambda h, i: (i, h))],
        out_specs=[pl.BlockSpec((tq, MLA_SLOT), lambda h, i: (i, h)), pl.BlockSpec((S, MLA_SLOT), lambda h, i: (0, h)),
                   pl.BlockSpec((S, MLA_V), lambda h, i: (0, h))],
        out_shape=[jax.ShapeDtypeStruct((S, W), F32), jax.ShapeDtypeStruct((S, W), F32),
                   jax.ShapeDtypeStruct((S, MLA_HEADS * MLA_V), F32)],
        args=(q, k, v, do), semantics=("parallel", "arbitrary"))


def _merge(ys, ws, z, name, tm=256):
    S = z.shape[0]
    D = ws[0].shape[1]
    tm = min(tm, S)
    zg = _V(z, 0, 3 * D)

    def body(y0, y1, y2, w0, w1, w2, g_ref, m_ref, p0, p1, p2):
        acc = jnp.zeros((tm, D), F32)
        for i, (y_ref, w_ref, p_ref) in enumerate(((y0, w0, p0), (y1, w1, p1), (y2, w2, p2))):
            pv = _dot(y_ref[...], w_ref[...])
            p_ref[...] = pv.astype(BF16)
            acc = acc + jax.nn.sigmoid(g_ref[:, D * i:D * (i + 1)]) * pv
        m_ref[...] = acc.astype(BF16)

    yb = pl.BlockSpec((tm, ys[0].shape[1]), lambda i: (i, 0))
    wb = pl.BlockSpec(ws[0].shape, lambda i: (0, 0))
    ob = pl.BlockSpec((tm, D), lambda i: (i, 0))
    return pl.pallas_call(
        body, name=name, grid=(S // tm,), in_specs=[yb] * 3 + [wb] * 3 + [zg.spec(tm, 3 * D, lambda i: i, lambda i: 0)],
        out_specs=[ob] * 4, out_shape=[jax.ShapeDtypeStruct((S, D), BF16)] * 4, compiler_params=_cparams("parallel"),
    )(*ys, *ws, z)


def _merge_bwd(dmixed, ps, z, name, tm=256):
    S, D = dmixed.shape
    tm = min(tm, S)
    zg = _V(z, 0, 3 * D)

    def body(dm_ref, p0, p1, p2, g_ref, d0, d1, d2, dg_ref):
        dm = dm_ref[...]
        for i, (p_ref, d_ref) in enumerate(((p0, d0), (p1, d1), (p2, d2))):
            gt = jax.nn.sigmoid(g_ref[:, D * i:D * (i + 1)])
            d_ref[...] = (dm * gt).astype(BF16)
            dg_ref[:, D * i:D * (i + 1)] = (dm * p_ref[...].astype(F32) * gt * (1.0 - gt)).astype(BF16)

    ob = pl.BlockSpec((tm, D), lambda i: (i, 0))
    return pl.pallas_call(
        body, name=name, grid=(S // tm,), in_specs=[ob] * 4 + [zg.spec(tm, 3 * D, lambda i: i, lambda i: 0)],
        out_specs=[ob] * 3 + [pl.BlockSpec((tm, 3 * D), lambda i: (i, 0))],
        out_shape=[jax.ShapeDtypeStruct((S, D), BF16)] * 3 + [jax.ShapeDtypeStruct((S, 3 * D), BF16)],
        compiler_params=_cparams("parallel"),
    )(dmixed, *ps, z)


def _loss_head(y, target, name, tm=512):
    S, D = y.shape
    tm = min(tm, S)

    def body(y_ref, t_ref, dy_ref, l_ref):
        e = y_ref[...] - t_ref[...]
        dy_ref[...] = e * (1.0 / D)
        sq = e * e
        part = jnp.sum(sq.reshape(tm // 8, 8, D), axis=0)
        part = jnp.sum(part.reshape(8, D // 128, 128), axis=1) * (0.5 / D)

        @pl.when(pl.program_id(0) == 0)
        def _():
            l_ref[...] = part

        @pl.when(pl.program_id(0) != 0)
        def _():
            l_ref[...] += part

    blk = pl.BlockSpec((tm, D), lambda i: (i, 0))
    return pl.pallas_call(
        body, name=name, grid=(S // tm,), in_specs=[blk, blk], out_specs=[blk, pl.BlockSpec((8, 128), lambda i: (0, 0))],
        out_shape=[jax.ShapeDtypeStruct((S, D), F32), jax.ShapeDtypeStruct((8, 128), F32)],
        compiler_params=_cparams("arbitrary"),
    )(y, target)


def _fold(parts, name, fold=None):
    L, _, W = parts.shape
    assert L <= 8

    def body(*refs):
        p_ref, o_ref = refs[0], refs[-1]
        rows = [jnp.sum(p_ref[l], axis=0, keepdims=True) for l in range(L)]
        rows += [jnp.zeros((1, W), F32)] * (8 - L)
        sums = jnp.concatenate(rows, axis=0)
        o_ref[...] = sums if fold is None else _dot(sums, refs[1][...], prec=HI)

    args = (parts,) if fold is None else (parts, jnp.asarray(fold))
    wout = W if fold is None else 128
    return pl.pallas_call(body, name=name, out_shape=jax.ShapeDtypeStruct((8, wout), F32))(*args)[:L]


def _adamw(w, g, m, v, name, q=None, ride=None):
    R, C = w.shape
    tr = R
    for cand in (512, 256, 128, 64, 32, 16, 8):
        if R % cand == 0 and cand * C * 4 <= 2 * 2**20:
            tr = cand
            break

    def body(*refs):
        if q is None:
            w_ref, g_ref, m_ref, v_ref, d_ref, nm_ref, nv_ref = refs
            gv = g_ref[...]
        else:
            w_ref, g_ref, q_ref, m_ref, v_ref, go_ref, d_ref, nm_ref, nv_ref = refs
            gv = g_ref[...] + q_ref[...]
            go_ref[...] = gv
        mn = ADAM_B1 * m_ref[...] + (1.0 - ADAM_B1) * gv
        vn = ADAM_B2 * v_ref[...] + (1.0 - ADAM_B2) * (gv * gv)
        nm_ref[...] = mn
        nv_ref[...] = vn
        m_hat = mn / (1.0 - ADAM_B1 ** ADAM_STEP)
        v_hat = vn / (1.0 - ADAM_B2 ** ADAM_STEP)
        d_ref[...] = -ADAM_LR * (m_hat / (jnp.sqrt(v_hat) + ADAM_EPS) + ADAM_WD * w_ref[...])

    blk = pl.BlockSpec((tr, C), lambda i: (i, 0))
    args = (w, g, m, v) if q is None else (w, g, q, m, v)
    nout = 3 if q is None else 4
    outs, got = _ride_call(
        body, ride, name=name, grid=(R // tr,), in_specs=[blk] * len(args), out_specs=[blk] * nout,
        out_shape=[jax.ShapeDtypeStruct((R, C), F32)] * nout, args=args, semantics=("parallel",))
    return outs if ride is None else (outs, got)


def _sibling_exchange(srcs, name, ride=()):
    n, k = len(srcs), len(ride)

    def body(*refs):
        src_refs, dst_refs = refs[:n], refs[n + 2 * k:2 * n + 2 * k]
        send_sems, recv_sems = refs[2 * n + 3 * k:2 * n + 3 * k + 2]
        x, y, c = lax.axis_index("x"), lax.axis_index("y"), lax.axis_index("c")
        if k:
            start, finish = _ride_ops(ride, refs[n:n + k], refs[2 * n + 2 * k:2 * n + 3 * k], *refs[2 * n + 3 * k + 2:])
            start()
        copies = [pltpu.make_async_remote_copy(src_ref=src_refs[t], dst_ref=dst_refs[t], send_sem=send_sems.at[t],
                                               recv_sem=recv_sems.at[t], device_id=(x, y, 1 - c), device_id_type=MESH)
                  for t in range(n)]
        for cp in copies:
            cp.start()
        for cp in copies:
            cp.wait()
        if k:
            finish()

    res = pl.pallas_call(
        body, name=name, in_specs=[_ANY] * (n + 2 * k), out_specs=[_ANY] * (n + k),
        out_shape=[jax.ShapeDtypeStruct(s.shape, s.dtype) for s in srcs] + [jax.ShapeDtypeStruct(it.dst.shape, it.dst.dtype) for it in ride],
        input_output_aliases={n + k + t: n + t for t in range(k)},
        scratch_shapes=[pltpu.SemaphoreType.DMA((n,)), pltpu.SemaphoreType.DMA((n,))] + (_RIDE_SEMS(k) if k else []),
    )(*srcs, *[it.src for it in ride], *[it.dst for it in ride])
    return res[:n], {it.name: o for it, o in zip(ride, res[n:])}


def _allreduce_small(v, name):
    R = v.shape[0]

    def body(v_ref, o_ref, slots, send_sems, recv_sems):
        x, y, c = lax.axis_index("x"), lax.axis_index("y"), lax.axis_index("c")
        me = 4 * x + 2 * y + c
        slots[me] = v_ref[...]
        sent = []
        for r in range(1, 8):
            fx, fy, fc = (r >> 2) & 1, (r >> 1) & 1, r & 1
            px, py, pc = (1 - x) if fx else x, (1 - y) if fy else y, (1 - c) if fc else c
            peer = 4 * px + 2 * py + pc

            def copy(slot, r=r, px=px, py=py, pc=pc):
                return pltpu.make_async_remote_copy(
                    src_ref=v_ref, dst_ref=slots.at[slot], send_sem=send_sems.at[r - 1], recv_sem=recv_sems.at[r - 1],
                    device_id=(px, py, pc), device_id_type=MESH)

            cp = copy(me)
            cp.start()
            sent.append((cp, copy(peer)))
        for cp, arrival in sent:
            cp.wait_send()
            arrival.wait_recv()
        acc = slots[0]
        for k in range(1, 8):
            acc = acc + slots[k]
        o_ref[...] = acc

    vm = pl.BlockSpec(memory_space=pltpu.VMEM)
    return pl.pallas_call(
        body, name=name, in_specs=[vm], out_specs=vm, out_shape=jax.ShapeDtypeStruct((R, 128), F32),
        scratch_shapes=[pltpu.VMEM((8, R, 128), F32), pltpu.SemaphoreType.DMA((7,)), pltpu.SemaphoreType.DMA((7,))],
    )(v)


def _sum4(recv, name, tr=512):
    _, R, W = recv.shape
    tr = _tile(R, tr)
    assert R % tr == 0

    def body(r_ref, o_ref):
        o_ref[...] = ((r_ref[0].astype(F32) + r_ref[1].astype(F32)) + r_ref[2].astype(F32)) + r_ref[3].astype(F32)

    return pl.pallas_call(
        body, name=name, grid=(R // tr,), in_specs=[pl.BlockSpec((4, tr, W), lambda i: (0, i, 0))],
        out_specs=pl.BlockSpec((tr, W), lambda i: (i, 0)), out_shape=jax.ShapeDtypeStruct((R, W), F32),
        compiler_params=_cparams("parallel"),
    )(recv)


W_NAMES = ("ffn1_norm", "ffn1_w1", "ffn1_w3", "ffn1_w2", "mix_norm", "w_in", "na_q_norm", "na_k_norm", "na_rpb",
           "gla_gf_up", "gla_gf_bias", "gla_gb_up", "gla_gb_bias", "gla_out_norm", "mla_cq_norm", "mla_ckv_norm",
           "mla_w_uq", "mla_w_ukv", "mla_q_norm", "mla_k_norm", "w_br_na", "w_br_gla", "w_br_mla", "w_out",
           "ffn2_norm", "ffn2_w1", "ffn2_w3", "ffn2_w2")
SHARDED = {"ffn1_w1": 2, "ffn1_w3": 2, "ffn1_w2": 1, "w_in": 2, "gla_gf_up": 2, "gla_gb_up": 2, "mla_w_uq": 2,
           "mla_w_ukv": 2, "w_br_na": 2, "w_br_gla": 2, "w_br_mla": 2, "w_out": 1, "ffn2_w1": 2, "ffn2_w3": 2,
           "ffn2_w2": 1}
REPLICATED = tuple(n for n in W_NAMES if n not in SHARDED)
FFN_W = ("ffn1_w1", "ffn1_w3", "ffn1_w2", "ffn2_w1", "ffn2_w3", "ffn2_w2")


def _cols_of(parts, lo, hi):
    out, off = [], 0
    for a in parts:
        w = a.shape[-1]
        s, e = max(lo, off), min(hi, off + w)
        if s < e:
            out.append(a[..., s - off:e - off])
        off += w
    return out


def _win_layout(pieces, D):
    z = lambda n: [jnp.zeros(pieces[0].shape[:-1] + (n,), pieces[0].dtype)]
    c = lambda lo, hi: _cols_of(pieces, lo, hi)
    return jnp.concatenate(c(O_GATES, O_GATES + 3 * D) + c(0, O_GFL) + c(O_CQ, O_KR) + c(O_GFL, O_CQ) + z(96)
                           + c(O_KR, O_KR + 32) + z(32) + c(O_KR + 32, O_KR + 64) + z(32), axis=-1)


def _win_split4(dw_segs, D):
    gates, na, gla, cq, ckv, lr, kr = dw_segs
    parts = [na, gla, lr[:, 0:32], cq, ckv, kr[:, 0:32], kr[:, 64:96], gates]
    n = (O_GATES + 3 * D) // 4
    return jnp.stack([jnp.concatenate(_cols_of(parts, j * n, (j + 1) * n), axis=1) for j in range(4)]).astype(BF16)


def _uq_layout(w):
    s = w.shape[:-1]
    w = w.reshape(s + (MLA_HEADS, MLA_QK))
    z = jnp.zeros(s + (MLA_HEADS, 32), w.dtype)
    return jnp.concatenate([w[..., :160], z, w[..., 160:], z], axis=-1).reshape(s + (MLA_HEADS * MLA_SLOT,))


def _uq_unlayout(dw):
    s = dw.shape[:-1]
    dw = dw.reshape(s + (MLA_HEADS, MLA_SLOT))
    return jnp.concatenate([dw[..., :160], dw[..., 192:224]], axis=-1).reshape(s + (MLA_HEADS * MLA_QK,))


def _slot_layout(g):
    z = jnp.zeros(g.shape[:-1] + (32,), g.dtype)
    return jnp.concatenate([g[..., :160], z, g[..., 160:], z], axis=-1)


def _slot_unlayout(g):
    return jnp.concatenate([g[..., :160], g[..., 192:224]], axis=-1)


def _layer_fwd(x, w, cos, sg, rides, mixer=None):
    D = x.shape[1]
    NA, GL, ML, LR, KR = 3 * D, 3 * D + 1536, 3 * D + 3072, 3 * D + 3584, 3 * D + 3712
    got = {}
    x1, f1, arrived = _ffn_fwd(x, w["ffn1_norm"], w["ffn1_w1"], w["ffn1_w3"], w["ffn1_w2"], "ffn1", ride=rides.get("ffn1_up"))
    got.update(arrived)
    if mixer is not None:
        w = {**w, **mixer(arrived)}
    h = _rms_fwd(x1, w["mix_norm"], "mix_rms")
    nz = w["w_in"].shape[1]
    z, arrived = _mm([(h, w["w_in"])], "nn", F32, "w_in", tm=512, tn=_tile(nz, 1280), ride=rides.get("w_in", []))
    got.update(arrived)
    qn, kn, vb = _na_prep(z, NA, w["na_gq"], w["na_gk"], "na_prep")
    bias = _rpb_expand(w["na_rpb"], "rpb_expand")
    y_na, arrived = _na_attn(qn, kn, vb, bias, "na_attn", ride=rides.get("na_attn"))
    got.update(arrived)
    gfb = _gla_gates(z, LR, w["gla_wg"], w["gla_gbias"], "gla_gates")
    (o_f, o_b, s_f, s_b), arrived = _gla_fwd(z, GL, GL + 512, gfb, "gla_fwd", ride=rides.get("gla_fwd"))
    got.update(arrived)
    y_gla = _gla_post(o_f, o_b, z, GL + 1024, w["gla_out_norm"], "gla_post")
    q, k, v, cqn, ckvn = _mla_prep(z, ML, KR, w["mla_wuq"], w["mla_w_ukv"], w["mla_cq_norm"], w["mla_ckv_norm"],
                                   w["mla_gq"], w["mla_gk"], cos, sg, "mla_prep")
    y_mla, arrived = _mla_attn(q, k, v, "mla_attn", ride=rides.get("mla_attn"))
    got.update(arrived)
    mixed, p0, p1, p2 = _merge([y_na, y_gla, y_mla], [w["w_br_na"], w["w_br_gla"], w["w_br_mla"]], z, "merge")
    x2 = _mm([(mixed, w["w_out"])], "nn", F32, "w_out", tm=512, tn=1024, res=x1)
    x3, f2, _ = _ffn_fwd(x2, w["ffn2_norm"], got["ffn2_w1"], got["ffn2_w3"], got["ffn2_w2"], "ffn2")
    saved = dict(x=x, x1=x1, x2=x2, f1=f1, f2=f2, h=h, z=z, qn=qn, kn=kn, vb=vb, bias=bias, y_na=y_na, gfb=gfb, o_f=o_f,
                 o_b=o_b, s_f=s_f, s_b=s_b, y_gla=y_gla, q=q, k=k, v=v, cqn=cqn, ckvn=ckvn, y_mla=y_mla, mixed=mixed,
                 p0=p0, p1=p1, p2=p2)
    return x3, saved, got, w


def _split4(a, axis):
    n = a.shape[axis] // 4
    return jnp.stack([lax.slice_in_dim(a, j * n, (j + 1) * n, axis=axis) for j in range(4)]).astype(BF16)


def _layer_bwd(dx3, w, sv, cos, sg, bufs, recv, layer, prev, flush=False):
    D = dx3.shape[1]
    at_layer = lambda chip: (chip, layer)
    pick = lambda *names: [prev[n] for n in names if n in prev]
    recv = dict(recv)
    NA, GL, ML, LR, KR = 3 * D, 3 * D + 1536, 3 * D + 3072, 3 * D + 3584, 3 * D + 3712
    z = sv["z"]
    g = {}
    dx2, g["ffn2_norm"], (g["ffn2_w1"], g["ffn2_w3"], g["ffn2_w2"]), got = _ffn_bwd(
        dx3, sv["x2"], w["ffn2_norm"], w["ffn2_w1"], w["ffn2_w3"], w["ffn2_w2"], sv["f2"], "ffn2",
        (bufs["ffn2_w1"], bufs["ffn2_w3"], bufs["ffn2_w2"]), layer, ride_down=pick("ffn1_w1"), ride_dh=pick("ffn1_w3"))
    recv.update(got)
    dmixed = _mm([(dx2, w["w_out"])], "nt", F32, "w_out_dx", tm=512, tn=512)
    g["w_out"] = _mm([(sv["mixed"], dx2)], "tn", F32, "w_out_dw", tm=D, tn=256)
    d0, d1, d2, dgates = _merge_bwd(dmixed, [sv["p0"], sv["p1"], sv["p2"]], z, "merge_bwd")
    dys = []
    for d, y, nm, dt in ((d0, sv["y_na"], "w_br_na", BF16), (d1, sv["y_gla"], "w_br_gla", F32), (d2, sv["y_mla"], "w_br_mla", BF16)):
        dys.append(_mm([(d, w[nm])], "nt", dt, nm + "_dy", tm=512, tn=512))
        g[nm] = _mm([(y, d)], "tn", F32, nm + "_dw", tm=512, tn=512)
    (dqn, dkn, dvn, dbias), got = _na_attn_bwd(sv["qn"], sv["kn"], sv["vb"], sv["bias"], dys[0], "na_attn_bwd",
                                               ride=pick("ffn1_w2", "mla_w_uq", "mla_w_ukv", "gla_gf_up", "gla_gb_up"))
    recv.update(got)
    dz_na, g["na_gq"], g["na_gk"] = _na_prep_bwd(z, NA, w["na_gq"], w["na_gk"], dqn, dkn, dvn, "na_prep_bwd")
    g["na_rpb"] = _rpb_reduce(dbias, "rpb_reduce")
    do, dgr, g["gla_out_norm"] = _gla_post_bwd(sv["o_f"], sv["o_b"], z, GL + 1024, w["gla_out_norm"], dys[1], "gla_post_bwd")
    (dqk_f, dv_f, dg_f, dqk_b, dv_b, dg_b), _ = _gla_bwd(z, GL, GL + 512, sv["gfb"], do, sv["s_f"], sv["s_b"], "gla_bwd")
    dz_gla = _gla_assemble(dqk_f, dqk_b, dv_f, dv_b, dgr, "gla_assemble")
    dpre, g["gla_gbias"] = _gla_gates_bwd(z, LR, w["gla_wg"], w["gla_gbias"], dg_f, dg_b, "gla_gates_bwd")
    g["gla_wg"] = _mm([(_V(z, LR, 128), dpre)], "tn", F32, "gla_wg_dw", tm=128, tn=512)
    dz_lr = _mm([(dpre, w["gla_wg"])], "nt", BF16, "gla_wg_dz", tm=512, tn=128)
    own = lambda n: _Ride(n, g[n], at_layer, recv[n], at_layer)
    (dq, dk, dv), got = _mla_attn_bwd(sv["q"], sv["k"], sv["v"], dys[2], "mla_attn_bwd",
                                      ride=pick("w_in") + [own("ffn2_w1"), own("ffn2_w3")])
    recv.update(got)
    dqf, dkv, dz_kr, g["mla_gq"], g["mla_gk"] = _mla_prep_bwd(
        z, KR, sv["cqn"], sv["ckvn"], w["mla_wuq"], w["mla_w_ukv"], w["mla_gq"], w["mla_gk"], cos, sg, dq, dk, dv, "mla_prep_bwd")
    g["mla_wuq"] = _mm([(sv["cqn"], dqf)], "tn", F32, "mla_wuq_dw", tm=256, tn=512)
    g["mla_w_ukv"] = _mm([(sv["ckvn"], dkv)], "tn", F32, "mla_wukv_dw", tm=256, tn=512)
    dcqn = _mm([(dqf, w["mla_wuq"])], "nt", F32, "mla_wuq_dx", tm=512, tn=256)
    dckvn = _mm([(dkv, w["mla_w_ukv"])], "nt", F32, "mla_wukv_dx", tm=512, tn=256)
    dz_cq, dg_cq = _rms_bwd(_V(z, ML, MLA_RANK), w["mla_cq_norm"], dcqn, "mla_cq_rms_bwd", out_dtype=BF16)
    dz_ckv, dg_ckv = _rms_bwd(_V(z, ML + MLA_RANK, MLA_RANK), w["mla_ckv_norm"], dckvn, "mla_ckv_rms_bwd", out_dtype=BF16)
    g["mla_cq_norm"], g["mla_ckv_norm"] = dg_cq[0:1], dg_ckv[0:1]
    segs = ((dgates, 0, 3 * D), (dz_na, NA, 1536), (dz_gla, GL, 1536), (dz_cq, ML, MLA_RANK), (dz_ckv, ML + MLA_RANK, MLA_RANK),
            (dz_lr, LR, 128), (dz_kr, KR, 128))
    dh, got = _mm([(dz, _V(w["w_in"], c0, wd)) for dz, c0, wd in segs], "nt", F32, "w_in_dx", tm=512, tn=512,
                  ride=[own("ffn2_w2")])
    recv.update(got)
    dw_in = [_mm([(sv["h"], dz)], "tn", F32, f"w_in_dw{i}", tm=D, tn=_tile(wd, 256)) for i, (dz, _, wd) in enumerate(segs)]
    dx1, dg_mix = _rms_bwd(sv["x1"], w["mix_norm"], dh, "mix_rms_bwd", dres=dx2)
    g["mix_norm"] = dg_mix[0:1]
    late = dict(w_in=_win_split4(dw_in, D), mla_w_uq=_split4(_uq_unlayout(g["mla_wuq"]), 1),
                mla_w_ukv=_split4(g["mla_w_ukv"], 1), gla_gf_up=_split4(g["gla_wg"][0:GLA_RANK, 0:256], 1),
                gla_gb_up=_split4(g["gla_wg"][GLA_RANK:2 * GLA_RANK, 256:512], 1))
    ride = [_Ride(n, _split4(g[n], SHARDED[n] - 1), lambda chip: (chip,), recv[n], at_layer)
            for n in ("w_out", "w_br_na", "w_br_gla", "w_br_mla")]
    ride_dh = None
    if flush:
        ride_dh, late = [_Ride(n, late[n], lambda chip: (chip,), recv[n], at_layer) for n in late], {}
    dx, g["ffn1_norm"], (g["ffn1_w1"], g["ffn1_w3"], g["ffn1_w2"]), got = _ffn_bwd(
        dx1, sv["x"], w["ffn1_norm"], w["ffn1_w1"], w["ffn1_w3"], w["ffn1_w2"], sv["f1"], "ffn1",
        (bufs["ffn1_w1"], bufs["ffn1_w3"], bufs["ffn1_w2"]), layer, ride_down=ride, ride_dh=ride_dh)
    recv.update(got)
    return dx, g, late, recv


def _head_fold(width, period, lo=0):
    f = np.zeros((width, 128), np.float32)
    f[np.arange(width), lo + np.arange(width) % period] = 1.0
    return f


def kernel(x, ffn1_norm, ffn1_w1, ffn1_w3, ffn1_w2, mix_norm, w_in, na_q_norm, na_k_norm, na_rpb, gla_gf_up, gla_gf_bias,
           gla_gb_up, gla_gb_bias, gla_out_norm, mla_cq_norm, mla_ckv_norm, mla_w_uq, mla_w_ukv, mla_q_norm, mla_k_norm,
           w_br_na, w_br_gla, w_br_mla, w_out, ffn2_norm, ffn2_w1, ffn2_w3, ffn2_w2, loss_target, m_ffn1_norm, m_ffn1_w1,
           m_ffn1_w3, m_ffn1_w2, m_mix_norm, m_w_in, m_na_q_norm, m_na_k_norm, m_na_rpb, m_gla_gf_up, m_gla_gf_bias,
           m_gla_gb_up, m_gla_gb_bias, m_gla_out_norm, m_mla_cq_norm, m_mla_ckv_norm, m_mla_w_uq, m_mla_w_ukv,
           m_mla_q_norm, m_mla_k_norm, m_w_br_na, m_w_br_gla, m_w_br_mla, m_w_out, m_ffn2_norm, m_ffn2_w1, m_ffn2_w3,
           m_ffn2_w2, v_ffn1_norm, v_ffn1_w1, v_ffn1_w3, v_ffn1_w2, v_mix_norm, v_w_in, v_na_q_norm, v_na_k_norm,
           v_na_rpb, v_gla_gf_up, v_gla_gf_bias, v_gla_gb_up, v_gla_gb_bias, v_gla_out_norm, v_mla_cq_norm,
           v_mla_ckv_norm, v_mla_w_uq, v_mla_w_ukv, v_mla_q_norm, v_mla_k_norm, v_w_br_na, v_w_br_gla, v_w_br_mla,
           v_w_out, v_ffn2_norm, v_ffn2_w1, v_ffn2_w3, v_ffn2_w2):
    given = dict(locals())
    wts = {n: given[n] for n in W_NAMES}
    mom = {n: given["m_" + n] for n in W_NAMES}
    var = {n: given["v_" + n] for n in W_NAMES}
    xs, target = x[0], loss_target[0]
    S, D = xs.shape
    L = ffn1_norm.shape[0]

    sh_names = tuple(SHARDED)
    LATE = ("ffn2_w1", "ffn2_w3", "ffn2_w2")
    HEAVY = ("ffn1_w1", "ffn1_w3", "ffn1_w2", "w_in")
    LIGHT = tuple(n for n in sh_names if n not in LATE + HEAVY)
    shard_shape = lambda n: tuple(wts[n].shape[1:])

    def gather_items(names, l):
        return [_Ride(n, wts[n][l].astype(BF16), lambda chip: (), lax.empty((4,) + shard_shape(n), BF16), lambda chip: (chip,),
                      halves=shard_shape(n)[0] % 32 == 0) for n in names]

    cols = lambda p: jnp.concatenate([p[j] for j in range(4)], axis=-1)

    def ffn1_weights(gl, l):
        return dict(ffn1_norm=ffn1_norm[l][None], ffn1_w1=gl["ffn1_w1"], ffn1_w3=gl["ffn1_w3"], ffn1_w2=gl["ffn1_w2"])

    def mixer_weights(gl, l):
        r1 = lambda a: a[l][None]
        wg = jnp.zeros((128, 2 * GLA_HEADS * GLA_DK), BF16)
        wg = wg.at[0:GLA_RANK, 0:256].set(cols(gl["gla_gf_up"])).at[GLA_RANK:2 * GLA_RANK, 256:512].set(cols(gl["gla_gb_up"]))
        return dict(
            mix_norm=r1(mix_norm), w_in=_win_layout([gl["w_in"][j] for j in range(4)], D),
            na_gq=jnp.tile(na_q_norm[l], NA_HEADS)[None], na_gk=jnp.tile(na_k_norm[l], NA_HEADS)[None], na_rpb=na_rpb[l],
            gla_wg=wg, gla_gbias=jnp.concatenate([gla_gf_bias[l], gla_gb_bias[l]])[None], gla_out_norm=r1(gla_out_norm),
            mla_cq_norm=r1(mla_cq_norm), mla_ckv_norm=r1(mla_ckv_norm), mla_wuq=_uq_layout(cols(gl["mla_w_uq"])),
            mla_w_ukv=cols(gl["mla_w_ukv"]), mla_gq=_slot_layout(mla_q_norm[l])[None], mla_gk=_slot_layout(mla_k_norm[l])[None],
            w_br_na=cols(gl["w_br_na"]), w_br_gla=cols(gl["w_br_gla"]), w_br_mla=cols(gl["w_br_mla"]),
            w_out=gl["w_out"].reshape(D, D), ffn2_norm=r1(ffn2_norm))

    half = MLA_ROPE // 2
    inv = ROPE_THETA ** (-jnp.arange(half, dtype=F32) / half)
    ang = jnp.arange(S, dtype=F32)[:, None] * inv[None, :]
    cos = jnp.tile(jnp.cos(ang), (1, 4))
    sg = jnp.concatenate([-jnp.sin(ang), -jnp.sin(ang), jnp.sin(ang), jnp.sin(ang)], axis=1)

    FFN1 = ("ffn1_w1", "ffn1_w3", "ffn1_w2")
    arrived = _exchange(gather_items(FFN1, 0), "weights_all_gather")
    xc, saved, layers = xs, [], []
    for l in range(L):
        w = ffn1_weights(arrived, l)
        if l == 0:
            rides = {"ffn1_up": gather_items(("w_in",) + LIGHT, 0), "w_in": gather_items(LATE, 0)}
            mixer = lambda got, l=l: mixer_weights(got, l)
        else:
            rides = {"ffn1_up": gather_items(("ffn2_w1", "ffn2_w3"), l), "w_in": gather_items(("ffn2_w2",), l)}
            w.update(mixer_weights(arrived, l))
            mixer = None
        if l + 1 < L:
            rides["na_attn"] = gather_items(("ffn1_w1", "ffn1_w3"), l + 1)
            rides["gla_fwd"] = gather_items(("ffn1_w2",) + LIGHT, l + 1)
            rides["mla_attn"] = gather_items(("w_in",), l + 1)
        xc, sv, arrived, w = _layer_fwd(xc, w, cos, sg, rides, mixer)
        saved.append(sv)
        layers.append({**w, **{n: arrived[n] for n in LATE}})
    dy, loss_part = _loss_head(xc, target, "loss_head")

    bufs = {n: lax.empty((4, L) + shard_shape(n), BF16) for n in FFN_W}
    recv = {n: lax.empty((4, L) + shard_shape(n), BF16) for n in sh_names}
    dx, g, prev = dy, [None] * L, {}
    for l in reversed(range(L)):
        dx, g[l], late, recv = _layer_bwd(dx, layers[l], saved[l], cos, sg, bufs, recv, l, prev, flush=l == 0)
        bufs = {n: g[l][n] for n in FFN_W}
        at_l = functools.partial(lambda chip, l: (chip, l), l=l)
        prev = {n: _Ride(n, bufs[n], at_l, recv[n], at_l) for n in ("ffn1_w1", "ffn1_w3", "ffn1_w2")}
        prev.update({n: _Ride(n, late[n], lambda chip: (chip,), recv[n], at_l) for n in late})

    stk = lambda n: jnp.stack([g[l][n] for l in range(L)])
    gs = {n: stk(n)[:, 0] for n in ("ffn1_norm", "mix_norm", "mla_cq_norm", "mla_ckv_norm", "ffn2_norm")}
    gs["na_q_norm"] = _fold(stk("na_gq"), "na_gq_fold", _head_fold(NA_W, NA_DH))[:, :NA_DH]
    gs["na_k_norm"] = _fold(stk("na_gk"), "na_gk_fold", _head_fold(NA_W, NA_DH))[:, :NA_DH]
    gs["na_rpb"] = stk("na_rpb")
    gbias = _fold(stk("gla_gbias"), "gla_gbias_fold")
    gs["gla_gf_bias"], gs["gla_gb_bias"] = gbias[:, :256], gbias[:, 256:]
    gs["gla_out_norm"] = _fold(stk("gla_out_norm"), "gla_out_norm_fold")
    gs["mla_q_norm"] = _slot_unlayout(_fold(stk("mla_gq"), "mla_gq_fold"))
    gs["mla_k_norm"] = _slot_unlayout(_fold(stk("mla_gk"), "mla_gk_fold"))

    as2d = lambda a: a.reshape(-1, a.shape[-1])
    gsh, upd = {}, {}
    DONE = ("ffn2_w1", "ffn2_w3", "ffn2_w2", "w_out", "w_br_na", "w_br_gla", "w_br_mla")

    def reduce_and_update(names, with_exchange):
        mine = [_sum4(recv[n].reshape(4, -1, recv[n].shape[-1]), "grads_chip_sum_" + n) for n in names]
        other, arrived = _sibling_exchange(mine, "grads_sibling_exchange", ride=with_exchange)
        for n, p, q in zip(names, mine, other):
            outs = [o.reshape(wts[n].shape) for o in _adamw(as2d(wts[n]), p, as2d(mom[n]), as2d(var[n]), "adamw_" + n, q=q)]
            gsh[n], upd[n] = outs[0], outs[1:]
        return arrived

    recv.update(reduce_and_update(DONE, list(prev.values())))
    reduce_and_update(tuple(n for n in sh_names if n not in DONE), [])

    small_shapes = [wts[n].shape[1:] for n in REPLICATED]
    n_small = sum(int(np.prod(s)) for s in small_shapes) * L
    flat = jnp.concatenate([gs[n].reshape(-1) for n in REPLICATED] + [loss_part.reshape(-1)])
    pad = -flat.shape[0] % 1024
    red = _allreduce_small(jnp.pad(flat, (0, pad)).reshape(-1, 128), "small_all_reduce").reshape(-1)
    loss = jnp.sum(red[n_small:n_small + 1024])
    off = 0
    for n, s in zip(REPLICATED, small_shapes):
        cnt = int(np.prod(s)) * L
        gsh[n] = red[off:off + cnt].reshape((L,) + tuple(s))
        off += cnt

    pk = lambda d: jnp.pad(jnp.concatenate([d[n].reshape(-1) for n in REPLICATED]), (0, -n_small % 1024)).reshape(-1, 128)
    small = _adamw(pk(wts), pk(gsh), pk(mom), pk(var), "adamw_replicated")
    off = 0
    for n, s in zip(REPLICATED, small_shapes):
        cnt = int(np.prod(s)) * L
        upd[n] = [o.reshape(-1)[off:off + cnt].reshape((L,) + tuple(s)) for o in small]
        off += cnt

    return (loss, dx[None], *[gsh[n] for n in W_NAMES], *[upd[n][0] for n in W_NAMES], *[upd[n][1] for n in W_NAMES],
            *[upd[n][2] for n in W_NAMES])
```

```python
import functools
import math

import numpy as np
import jax
import jax.numpy as jnp
from jax import lax
from jax.experimental import pallas as pl
from jax.experimental.pallas import tpu as pltpu

F32 = jnp.float32
BF16 = jnp.bfloat16
HI = lax.Precision.HIGHEST
MESH = pl.DeviceIdType.MESH

EPS = 1e-6
GRID_W = 64
NA_HEADS, NA_DH, NA_WIN_R, NA_WIN_C = 8, 64, 8, 16
NA_W = NA_HEADS * NA_DH
GLA_HEADS, GLA_DK, GLA_DV, GLA_RANK, GLA_TAU, GLA_CHUNK = 4, 64, 128, 16, 16.0, 64
MLA_HEADS, MLA_RANK, MLA_NOPE, MLA_ROPE, MLA_V = 4, 256, 128, 64, 128
MLA_QK = MLA_NOPE + MLA_ROPE
MLA_SLOT = 256
MLA_QSCALE = MLA_QK ** -0.5 * math.log2(math.e)
ROPE_THETA = 10000.0
ADAM_LR, ADAM_B1, ADAM_B2, ADAM_EPS, ADAM_WD, ADAM_STEP = 0.001, 0.9, 0.999, 1e-08, 0.01, 10

V7X_VMEM_BYTES = 64 * 2**20
VMEM_LIMIT = V7X_VMEM_BYTES - 12 * 2**20
NEG = -1e30

O_GQ, O_GFL, O_CQ, O_KR, O_GATES = 1536, 3072, 3104, 3616, 3680


_ANY = pl.BlockSpec(memory_space=pl.ANY)


def _cparams(*sem):
    return pltpu.CompilerParams(dimension_semantics=sem, vmem_limit_bytes=VMEM_LIMIT)


class _V:
    def __init__(self, arr, c0=0, w=None, lead=()):
        self.arr, self.c0, self.lead = arr, c0, tuple(lead)
        assert arr.ndim == 2 + len(self.lead), (arr.shape, lead)
        self.w = arr.shape[-1] if w is None else w

    @property
    def rows(self):
        return self.arr.shape[-2]

    def spec(self, br, bc, rfn, cfn):
        assert self.c0 % bc == 0 and self.w % bc == 0, (self.c0, self.w, bc)
        off, lead = self.c0 // bc, self.lead

        def index(*g):
            return tuple(g[0] if e == "b" else e for e in lead) + (rfn(*g), off + cfn(*g))

        return pl.BlockSpec((None,) * len(lead) + (br, bc), index)


def _v(x):
    return x if isinstance(x, _V) else _V(x)


_DN = {"nn": (((1,), (0,)), ((), ())), "nt": (((1,), (1,)), ((), ())), "tn": (((0,), (0,)), ((), ()))}


def _dot(a, b, mode="nn", prec=None):
    return lax.dot_general(a, b, _DN[mode], preferred_element_type=F32, precision=prec)


def _tile(n, cap):
    if n <= cap:
        return n
    for t in range(cap - cap % 128, 0, -128):
        if n % t == 0:
            return t
    return n


def _mm(pairs, mode, out_dtype, name, *, tm, tn, res=None, scale=None, batch=1, into=None, ride=None):
    pairs = [(_v(a), _v(b)) for a, b in pairs]
    a0, b0 = pairs[0]
    M = a0.w if mode == "tn" else a0.rows
    N = b0.rows if mode == "nt" else b0.w
    tm, tn = _tile(M, tm), _tile(N, tn)
    assert M % tm == 0 and N % tn == 0, (name, M, N, tm, tn)
    n = len(pairs)

    def body(*refs):
        o_ref = refs[-1]
        acc = None
        for i in range(n):
            d = _dot(refs[2 * i][...].astype(BF16), refs[2 * i + 1][...].astype(BF16), mode)
            acc = d if acc is None else acc + d
        if scale is not None:
            acc = acc * scale
        if res is not None:
            acc = acc + refs[2 * n][...]
        o_ref[...] = acc.astype(o_ref.dtype)

    zero = lambda b, i, j: 0
    row = lambda b, i, j: i
    col = lambda b, i, j: j
    in_specs, args = [], []
    for a, b in pairs:
        in_specs.append(a.spec(a.rows, tm, zero, row) if mode == "tn" else a.spec(tm, a.w, row, zero))
        in_specs.append(b.spec(tn, b.w, col, zero) if mode == "nt" else b.spec(b.rows, tn, zero, col))
        args += [a.arr, b.arr]
    if res is not None:
        in_specs.append(pl.BlockSpec((tm, tn), lambda b, i, j: (i, j)))
        args.append(res)
    aliases = {}
    if into is None:
        out = jax.ShapeDtypeStruct(((batch,) if batch > 1 else ()) + (M, N), out_dtype)
        out_view = _V(out, lead=("b",) if batch > 1 else ())
    else:
        buf, lead = into
        assert buf.shape[-2:] == (M, N) and buf.dtype == out_dtype, (name, buf.shape, M, N)
        out = jax.ShapeDtypeStruct(buf.shape, buf.dtype)
        out_view = _V(out, lead=lead)
        aliases = {len(args): 0}
        in_specs.append(_ANY)
        args.append(buf)
    (res,), got = _ride_call(
        body, ride, name=name, grid=(batch, M // tm, N // tn), in_specs=in_specs, out_specs=[out_view.spec(tm, tn, row, col)],
        out_shape=[out], aliases=aliases, args=args, semantics=("parallel", "parallel", "parallel"))
    return res if ride is None else (res, got)


def _rms_fwd(x, g, name, tm=512):
    x = _v(x)
    S, D = x.rows, x.w
    tm = min(tm, S)

    def body(x_ref, g_ref, o_ref):
        xv = x_ref[...]
        y = xv * lax.rsqrt(jnp.mean(xv * xv, axis=-1, keepdims=True) + EPS)
        o_ref[...] = (y * g_ref[...]).astype(o_ref.dtype)

    return pl.pallas_call(
        body, name=name, grid=(S // tm,),
        in_specs=[x.spec(tm, D, lambda i: i, lambda i: 0), pl.BlockSpec((1, D), lambda i: (0, 0))],
        out_specs=pl.BlockSpec((tm, D), lambda i: (i, 0)),
        out_shape=jax.ShapeDtypeStruct((S, D), BF16), compiler_params=_cparams("parallel"),
    )(x.arr, g)


def _rms_bwd(x, g, dh, name, dres=None, out_dtype=F32, tm=512):
    x = _v(x)
    S, D = x.rows, x.w
    tm = min(tm, S)

    def body(*refs):
        if dres is None:
            x_ref, g_ref, dh_ref, dx_ref, dg_ref = refs
        else:
            x_ref, g_ref, dh_ref, dr_ref, dx_ref, dg_ref = refs
        xv = x_ref[...]
        rstd = lax.rsqrt(jnp.mean(xv * xv, axis=-1, keepdims=True) + EPS)
        xhat = xv * rstd
        dhv = dh_ref[...].astype(F32)
        dxhat = dhv * g_ref[...]
        dx = rstd * (dxhat - xhat * jnp.mean(dxhat * xhat, axis=-1, keepdims=True))
        if dres is not None:
            dx = dx + dr_ref[...]
        dx_ref[...] = dx.astype(dx_ref.dtype)

        @pl.when(pl.program_id(0) == 0)
        def _():
            dg_ref[...] = jnp.zeros_like(dg_ref)

        dg_ref[0:1, :] += jnp.sum(dhv * xhat, axis=0, keepdims=True)

    in_specs = [x.spec(tm, D, lambda i: i, lambda i: 0), pl.BlockSpec((1, D), lambda i: (0, 0)),
                pl.BlockSpec((tm, D), lambda i: (i, 0))]
    args = [x.arr, g, dh]
    if dres is not None:
        in_specs.append(pl.BlockSpec((tm, D), lambda i: (i, 0)))
        args.append(dres)
    return pl.pallas_call(
        body, name=name, grid=(S // tm,), in_specs=in_specs,
        out_specs=[pl.BlockSpec((tm, D), lambda i: (i, 0)), pl.BlockSpec((8, D), lambda i: (0, 0))],
        out_shape=[jax.ShapeDtypeStruct((S, D), out_dtype), jax.ShapeDtypeStruct((8, D), F32)],
        compiler_params=_cparams("arbitrary"),
    )(*args)


FFN_SUB = 256


def _one_ahead(n, matmuls, rest):
    res = matmuls(0)
    for i in range(1, n):
        nxt = matmuls(i)
        rest(i - 1, res)
        res = nxt
    rest(n - 1, res)


def _ffn_up(h, w1, w3, name, tm=1024, ride=None):
    S, D = h.shape
    NC, _, F4 = w1.shape
    tm = min(tm, S)
    sub = math.gcd(FFN_SUB, tm)
    rows = lambda i: slice(i * sub, (i + 1) * sub)

    def body(h_ref, w1_ref, w3_ref, a_ref, b_ref, u_ref):
        def matmuls(i):
            hv = h_ref[rows(i), :]
            return _dot(hv, w1_ref[...]), _dot(hv, w3_ref[...])

        def rest(i, ab):
            a, b = ab
            a_ref[rows(i), :] = a.astype(BF16)
            b_ref[rows(i), :] = b.astype(BF16)
            u_ref[rows(i), :] = (a * jax.nn.sigmoid(a) * b).astype(BF16)

        _one_ahead(tm // sub, matmuls, rest)

    blk = pl.BlockSpec((None, tm, F4), lambda i, j: (j, i, 0))
    wblk = pl.BlockSpec((None, D, F4), lambda i, j: (j, 0, 0))
    return _ride_call(
        body, ride, name=name, grid=(S // tm, NC), in_specs=[pl.BlockSpec((tm, D), lambda i, j: (i, 0)), wblk, wblk],
        out_specs=[blk, blk, blk], out_shape=[jax.ShapeDtypeStruct((NC, S, F4), BF16)] * 3, args=(h, w1, w3),
        semantics=("parallel", "parallel"))


def _ffn_down_bwd(dxo, w2, a, b, name, tm=1024, ride=None):
    S, D = dxo.shape
    NC, F4, _ = w2.shape
    tm = min(tm, S)
    sub = math.gcd(FFN_SUB, tm)
    rows = lambda i: slice(i * sub, (i + 1) * sub)

    def body(dx_ref, w2_ref, a_ref, b_ref, da_ref, db_ref):
        def matmuls(i):
            return _dot(dx_ref[rows(i), :].astype(BF16), w2_ref[...], "nt")

        def rest(i, du):
            du = du * 0.5
            av = a_ref[rows(i), :].astype(F32)
            sig = jax.nn.sigmoid(av)
            da_ref[rows(i), :] = (du * b_ref[rows(i), :].astype(F32) * (sig * (1.0 + av * (1.0 - sig)))).astype(BF16)
            db_ref[rows(i), :] = (du * av * sig).astype(BF16)

        _one_ahead(tm // sub, matmuls, rest)

    blk = pl.BlockSpec((None, tm, F4), lambda i, j: (j, i, 0))
    return _ride_call(
        body, ride, name=name, grid=(S // tm, NC),
        in_specs=[pl.BlockSpec((tm, D), lambda i, j: (i, 0)), pl.BlockSpec((None, F4, D), lambda i, j: (j, 0, 0)), blk, blk],
        out_specs=[blk, blk], out_shape=[jax.ShapeDtypeStruct((NC, S, F4), BF16)] * 2, args=(dxo, w2, a, b),
        semantics=("parallel", "parallel"))


def _ffn_fwd(x, g, w1, w3, w2, tag, ride=None):
    h = _rms_fwd(x, g, f"{tag}_rms")
    (a, b, u), got = _ffn_up(h, w1, w3, f"{tag}_up", ride=ride)
    nc = w2.shape[0]
    y = _mm([(_V(u, lead=(j,)), _V(w2, lead=(j,))) for j in range(nc)], "nn", F32, f"{tag}_down", tm=512, tn=1024, res=x, scale=0.5)
    return y, (h, a, b, u), got


def _ffn_bwd(dxo, x, g, w1, w3, w2, saved, tag, bufs, layer, ride_down=None, ride_dh=None):
    h, a, b, u = saved
    nc, D, F4 = w1.shape
    (da, db), got = _ffn_down_bwd(dxo, w2, a, b, f"{tag}_down_bwd", ride=ride_down)
    into = lambda k: (bufs[k], ("b", layer))
    dw2 = _mm([(_V(u, lead=("b",)), dxo)], "tn", BF16, f"{tag}_dw2", tm=F4, tn=512, scale=0.5, batch=nc, into=into(2))
    dw1 = _mm([(h, _V(da, lead=("b",)))], "tn", BF16, f"{tag}_dw1", tm=D, tn=F4, batch=nc, into=into(0))
    dw3 = _mm([(h, _V(db, lead=("b",)))], "tn", BF16, f"{tag}_dw3", tm=D, tn=F4, batch=nc, into=into(1))
    pairs = [(_V(da, lead=(j,)), _V(w1, lead=(j,))) for j in range(nc)] + [(_V(db, lead=(j,)), _V(w3, lead=(j,))) for j in range(nc)]
    dh, got_dh = _mm(pairs, "nt", F32, f"{tag}_dh", tm=512, tn=1024, ride=ride_dh or [])
    dx, dg = _rms_bwd(x, g, dh, f"{tag}_rms_bwd", dres=dxo)
    return dx, dg[0:1], (dw1, dw3, dw2), {**got, **got_dh}


def _iota(shape, dim):
    return lax.broadcasted_iota(jnp.int32, shape, dim)


def _head_block_ones(n, shift):
    return jnp.where((_iota((n, n), 0) >> shift) == (_iota((n, n), 1) >> shift), 1.0, 0.0).astype(BF16)


def _dot_split(x, ones01):
    hi = x.astype(BF16)
    lo = (x - hi.astype(F32)).astype(BF16)
    return _dot(hi, ones01) + _dot(lo, ones01)


def _lane_mask(width, lo, size):
    l = _iota((1, width), 1)
    return jnp.where((l >= lo) & (l < lo + size), 1.0, 0.0).astype(F32)


def _acc_rows(acc_ref, val, first):
    r = val.shape[0]
    part = jnp.sum(val.reshape(r // 8, 8, val.shape[1]), axis=0)

    @pl.when(first)
    def _():
        acc_ref[...] = part

    @pl.when(jnp.logical_not(first))
    def _():
        acc_ref[...] += part


_FLIPS = ((1, 0), (0, 1), (1, 1))


class _Ride:
    def __init__(self, name, src, src_at, dst, dst_at, halves=False):
        self.name, self.src, self.src_at, self.dst, self.dst_at, self.halves = name, src, src_at, dst, dst_at, halves
        assert not halves or (src.ndim == 2 and src.shape[0] % 32 == 0), (name, src.shape)


_RIDE_SEMS = lambda n: [pltpu.SemaphoreType.DMA((6, n)), pltpu.SemaphoreType.DMA((6, n)), pltpu.SemaphoreType.DMA((n,))]


def _ride_ops(ride, srcs, dsts, send_sems, recv_sems, local_sems):
    x, y, c = lax.axis_index("x"), lax.axis_index("y"), lax.axis_index("c")
    me = 2 * x + y
    at = lambda ref, idx: ref.at[idx] if idx else ref
    local, sends, arrivals, passes = [], [], [], []
    for t, it in enumerate(ride):
        local.append(pltpu.make_async_copy(at(srcs[t], it.src_at(me)), at(dsts[t], it.dst_at(me)), local_sems.at[t]))
    for r, (fx, fy) in enumerate(_FLIPS):
        px, py = (1 - x) if fx else x, (1 - y) if fy else y
        peer = 2 * px + py
        for t, it in enumerate(ride):
            if it.halves:
                h = it.src.shape[0] // 2
                mine = pl.ds(pl.multiple_of(c * h, 16), h)
                theirs = pl.ds(pl.multiple_of((1 - c) * h, 16), h)
                far = dict(send_sem=send_sems.at[r, t], recv_sem=recv_sems.at[r, t], device_id=(px, py, c), device_id_type=MESH)
                near = dict(send_sem=send_sems.at[3 + r, t], recv_sem=recv_sems.at[3 + r, t], device_id=(x, y, 1 - c),
                            device_id_type=MESH)
                sends.append(pltpu.make_async_remote_copy(src_ref=srcs[t].at[mine], dst_ref=dsts[t].at[me, mine], **far))
                arrivals.append(pltpu.make_async_remote_copy(src_ref=srcs[t].at[mine], dst_ref=dsts[t].at[peer, mine], **far))
                passes.append((pltpu.make_async_remote_copy(src_ref=dsts[t].at[peer, mine], dst_ref=dsts[t].at[peer, mine], **near),
                               pltpu.make_async_remote_copy(src_ref=dsts[t].at[peer, theirs], dst_ref=dsts[t].at[peer, theirs], **near)))
            else:
                far = dict(src_ref=at(srcs[t], it.src_at(peer)), send_sem=send_sems.at[r, t], recv_sem=recv_sems.at[r, t],
                           device_id=(px, py, c), device_id_type=MESH)
                sends.append(pltpu.make_async_remote_copy(dst_ref=at(dsts[t], it.dst_at(me)), **far))
                arrivals.append(pltpu.make_async_remote_copy(dst_ref=at(dsts[t], it.dst_at(peer)), **far))
                passes.append(None)

    def start():
        for cp in local + sends:
            cp.start()

    def finish():
        for cp, arrival, onward in zip(sends, arrivals, passes):
            cp.wait_send()
            arrival.wait_recv()
            if onward is not None:
                onward[0].start()
        for onward in passes:
            if onward is not None:
                onward[0].wait_send()
                onward[1].wait_recv()
        for cp in local:
            cp.wait()

    return start, finish


def _grid_edges(*ns):
    def edges():
        first = last = None
        for d, n in enumerate(ns):
            i = pl.program_id(d)
            f, l = i == 0, i == n - 1
            first = f if first is None else jnp.logical_and(first, f)
            last = l if last is None else jnp.logical_and(last, l)
        return first, last
    return edges


def _ride_call(body, ride, *, name, grid, in_specs, out_specs, out_shape, args, scratch_shapes=(), semantics=(), aliases=None):
    scratch_shapes, aliases = list(scratch_shapes), dict(aliases or {})
    if not ride:
        outs = pl.pallas_call(body, name=name, grid=grid, in_specs=in_specs, out_specs=out_specs, out_shape=out_shape,
                              scratch_shapes=scratch_shapes, input_output_aliases=aliases,
                              compiler_params=_cparams(*semantics))(*args)
        return outs, {}
    n_in, n_out, n_sc, n = len(in_specs), len(out_specs), len(scratch_shapes), len(ride)
    edges = _grid_edges(*grid)

    def wrapped(*refs):
        ins, srcs = refs[:n_in], refs[n_in:n_in + n]
        o0 = n_in + 2 * n
        outs, dsts = refs[o0:o0 + n_out], refs[o0 + n_out:o0 + n_out + n]
        scratch = refs[o0 + n_out + n:o0 + n_out + n + n_sc]
        start, finish = _ride_ops(ride, srcs, dsts, *refs[o0 + n_out + n + n_sc:])
        first, last = edges()
        pl.when(first)(start)
        body(*ins, *outs, *scratch)
        pl.when(last)(finish)

    aliases.update({n_in + n + t: n_out + t for t in range(n)})
    res = pl.pallas_call(
        wrapped, name=name, grid=grid, in_specs=list(in_specs) + [_ANY] * (2 * n), out_specs=list(out_specs) + [_ANY] * n,
        out_shape=list(out_shape) + [jax.ShapeDtypeStruct(it.dst.shape, it.dst.dtype) for it in ride],
        input_output_aliases=aliases, scratch_shapes=scratch_shapes + _RIDE_SEMS(n),
        compiler_params=_cparams(*(["arbitrary"] * len(grid))),
    )(*args, *[it.src for it in ride], *[it.dst for it in ride])
    return res[:n_out], {it.name: o for it, o in zip(ride, res[n_out:])}


def _exchange(ride, name):
    n = len(ride)

    def body(*refs):
        start, finish = _ride_ops(ride, refs[:n], refs[2 * n:3 * n], *refs[3 * n:])
        start()
        finish()

    res = pl.pallas_call(
        body, name=name, in_specs=[_ANY] * (2 * n), out_specs=[_ANY] * n,
        out_shape=[jax.ShapeDtypeStruct(it.dst.shape, it.dst.dtype) for it in ride],
        input_output_aliases={n + t: t for t in range(n)}, scratch_shapes=_RIDE_SEMS(n),
    )(*[it.src for it in ride], *[it.dst for it in ride])
    return {it.name: o for it, o in zip(ride, res)}


def _na_prep(z, c0, gq, gk, name, tm=512):
    S = z.shape[0]
    tm = min(tm, S)
    zv = _V(z, c0, 3 * NA_W)

    def body(z_ref, gq_ref, gk_ref, q_ref, k_ref, v_ref):
        bd = _head_block_ones(NA_W, 6)

        def norm(xv, gv):
            ms = _dot_split(xv * xv, bd) * (1.0 / NA_DH)
            return xv * lax.rsqrt(ms + EPS) * gv

        q_ref[...] = (norm(z_ref[:, 0:NA_W], gq_ref[...]) * (NA_DH ** -0.5)).astype(BF16)
        k_ref[...] = norm(z_ref[:, NA_W:2 * NA_W], gk_ref[...]).astype(BF16)
        v_ref[...] = z_ref[:, 2 * NA_W:3 * NA_W].astype(BF16)

    blk = pl.BlockSpec((tm, NA_W), lambda i: (i, 0))
    gspec = pl.BlockSpec((1, NA_W), lambda i: (0, 0))
    return pl.pallas_call(
        body, name=name, grid=(S // tm,),
        in_specs=[zv.spec(tm, 3 * NA_W, lambda i: i, lambda i: 0), gspec, gspec],
        out_specs=[blk, blk, blk], out_shape=[jax.ShapeDtypeStruct((S, NA_W), BF16)] * 3,
        compiler_params=_cparams("parallel"),
    )(z, gq, gk)


def _na_prep_bwd(z, c0, gq, gk, dqn, dkn, dv, name, tm=512):
    S = z.shape[0]
    tm = min(tm, S)
    zv = _V(z, c0, 3 * NA_W)

    def body(z_ref, gq_ref, gk_ref, dq_ref, dk_ref, dv_ref, dz_ref, dgq_ref, dgk_ref):
        bd = _head_block_ones(NA_W, 6)
        first = pl.program_id(0) == 0

        def norm_bwd(xv, gv, dy, dg_ref):
            ms = _dot_split(xv * xv, bd) * (1.0 / NA_DH)
            rstd = lax.rsqrt(ms + EPS)
            xhat = xv * rstd
            dxhat = dy * gv
            proj = _dot_split(dxhat * xhat, bd) * (1.0 / NA_DH)
            _acc_rows(dg_ref, dy * xhat, first)
            return rstd * (dxhat - xhat * proj)

        dz_ref[:, 0:NA_W] = norm_bwd(z_ref[:, 0:NA_W], gq_ref[...], dq_ref[...] * (NA_DH ** -0.5), dgq_ref).astype(BF16)
        dz_ref[:, NA_W:2 * NA_W] = norm_bwd(z_ref[:, NA_W:2 * NA_W], gk_ref[...], dk_ref[...], dgk_ref).astype(BF16)
        dz_ref[:, 2 * NA_W:3 * NA_W] = dv_ref[...].astype(BF16)

    blk = pl.BlockSpec((tm, NA_W), lambda i: (i, 0))
    gspec = pl.BlockSpec((1, NA_W), lambda i: (0, 0))
    acc = pl.BlockSpec((8, NA_W), lambda i: (0, 0))
    return pl.pallas_call(
        body, name=name, grid=(S // tm,),
        in_specs=[zv.spec(tm, 3 * NA_W, lambda i: i, lambda i: 0), gspec, gspec, blk, blk, blk],
        out_specs=[pl.BlockSpec((tm, 3 * NA_W), lambda i: (i, 0)), acc, acc],
        out_shape=[jax.ShapeDtypeStruct((S, 3 * NA_W), BF16), jax.ShapeDtypeStruct((8, NA_W), F32),
                   jax.ShapeDtypeStruct((8, NA_W), F32)],
        compiler_params=_cparams("arbitrary"),
    )(z, gq, gk, dqn, dkn, dv)


def _na_onehot():
    qc = np.arange(GRID_W)[:, None]
    kc = np.arange(GRID_W)[None, :]
    c0 = np.clip(qc - NA_WIN_C // 2, 0, GRID_W - NA_WIN_C)
    valid = (kc >= c0) & (kc < c0 + NA_WIN_C)
    dc = kc - qc + (NA_WIN_C - 1)
    e = np.zeros((32, GRID_W, GRID_W), np.float32)
    for d in range(2 * NA_WIN_C - 1):
        e[d] = valid & (dc == d)
    return e.reshape(32, GRID_W * GRID_W), valid.reshape(1, -1)


def _rpb_expand(rpb, name):
    e, valid = _na_onehot()
    negmask = np.where(valid, 0.0, NEG).astype(np.float32)
    nd = 2 * NA_WIN_R - 1
    r2 = jnp.pad(rpb.reshape(NA_HEADS * nd, 2 * NA_WIN_C - 1), ((0, 128 - NA_HEADS * nd), (0, 1)))

    def body(r_ref, e_ref, m_ref, o_ref):
        o_ref[...] = _dot(r_ref[...], e_ref[...], prec=HI) + m_ref[...]

    t = pl.pallas_call(body, name=name, out_shape=jax.ShapeDtypeStruct((128, GRID_W * GRID_W), F32))(
        r2, jnp.asarray(e), jnp.asarray(negmask))
    t = t[:NA_HEADS * nd].reshape(NA_HEADS, nd, GRID_W, GRID_W)
    return jnp.stack([jnp.concatenate([t[:, b + w] for w in range(NA_WIN_R)], axis=-1) for b in range(NA_WIN_R)], axis=1)


def _rpb_reduce(dbias, name):
    e, _ = _na_onehot()
    nd = 2 * NA_WIN_R - 1
    et = np.zeros((GRID_W * GRID_W, 128), np.float32)
    et[:, :32] = e.T
    sel = np.zeros((128, NA_HEADS * NA_WIN_R * NA_WIN_R), np.float32)
    for h in range(NA_HEADS):
        for b in range(NA_WIN_R):
            for w in range(NA_WIN_R):
                sel[h * nd + b + w, (h * NA_WIN_R + b) * NA_WIN_R + w] = 1.0
    x = dbias.reshape(NA_HEADS, NA_WIN_R, GRID_W, NA_WIN_R, GRID_W).transpose(0, 1, 3, 2, 4).reshape(-1, GRID_W * GRID_W)

    def body(x_ref, et_ref, sel_ref, o_ref):
        g = _dot(x_ref[...], et_ref[...], prec=HI)
        o_ref[...] = _dot(sel_ref[...], g, prec=HI)

    out = pl.pallas_call(body, name=name, out_shape=jax.ShapeDtypeStruct((128, 128), F32))(x, jnp.asarray(et), jnp.asarray(sel))
    return out[:NA_HEADS * nd, :2 * NA_WIN_C - 1].reshape(NA_HEADS, nd, 2 * NA_WIN_C - 1)


def _na_base(r, rows):
    return jnp.clip(r - NA_WIN_R // 2, 0, rows - NA_WIN_R) - r + (NA_WIN_R - 1)


def _na_probs(q_ref, k_ref, bias_ref, P, r0w):
    sl = [slice(128 * pp, 128 * pp + 128) for pp in range(P)]
    m = [_lane_mask(128, 64 * hh, 64) for hh in range(2)]
    kw = [k_ref[r0w, sl[pp]] for pp in range(P)]
    units = [(pp, hh) for pp in range(P) for hh in range(2)]
    qm = {u: (q_ref[:, sl[u[0]]].astype(F32) * m[u[1]]).astype(BF16) for u in units}
    s = {u: _dot(qm[u], kw[u[0]], "nt") + bias_ref[2 * u[0] + u[1], 0] for u in units}
    p = {}
    for u in units:
        e = jnp.exp(s[u] - jnp.max(s[u], axis=-1, keepdims=True))
        p[u] = e / jnp.sum(e, axis=-1, keepdims=True)
    return sl, m, kw, units, qm, p


NA_FWD_PAIRS = 4
NA_BWD_PAIRS = 2


def _na_attn(qn, kn, vb, bias, name, ride=None):
    S = qn.shape[0]
    rows = S // GRID_W
    nk = NA_WIN_R * GRID_W
    P = NA_FWD_PAIRS
    W = 128 * P

    def body(q_ref, k_ref, v_ref, b_ref, o_ref):
        r = pl.program_id(1)
        r0w = pl.ds(pl.multiple_of(jnp.clip(r - NA_WIN_R // 2, 0, rows - NA_WIN_R) * GRID_W, GRID_W), nk)
        sl, m, _, units, _, p = _na_probs(q_ref, k_ref, b_ref, P, r0w)
        o = {u: _dot(p[u].astype(BF16), v_ref[r0w, sl[u[0]]]) for u in units}
        for pp in range(P):
            o_ref[:, sl[pp]] = (o[pp, 0] * m[0] + o[pp, 1] * m[1]).astype(BF16)

    full = pl.BlockSpec((S, W), lambda g, r: (0, g))
    (o,), got = _ride_call(
        body, ride, name=name, grid=(NA_HEADS // (2 * P), rows),
        in_specs=[pl.BlockSpec((GRID_W, W), lambda g, r: (r, g)), full, full,
                  pl.BlockSpec((2 * P, 1, GRID_W, nk), lambda g, r: (g, _na_base(r, rows), 0, 0))],
        out_specs=[pl.BlockSpec((GRID_W, W), lambda g, r: (r, g))],
        out_shape=[jax.ShapeDtypeStruct((S, NA_W), BF16)], args=(qn, kn, vb, bias), semantics=("parallel", "arbitrary"))
    return o, got


def _na_attn_bwd(qn, kn, vb, bias, do, name, ride=None):
    S = qn.shape[0]
    rows = S // GRID_W
    nk = NA_WIN_R * GRID_W
    P = NA_BWD_PAIRS
    W = 128 * P

    def body(q_ref, k_ref, v_ref, b_ref, do_ref, dq_ref, dk_ref, dv_ref, db_ref):
        r = pl.program_id(1)

        @pl.when(r == 0)
        def _():
            dk_ref[...] = jnp.zeros_like(dk_ref)
            dv_ref[...] = jnp.zeros_like(dv_ref)

        r0w = pl.ds(pl.multiple_of(jnp.clip(r - NA_WIN_R // 2, 0, rows - NA_WIN_R) * GRID_W, GRID_W), nk)
        fresh = jnp.logical_or(r <= NA_WIN_R // 2, r > rows - NA_WIN_R // 2)
        sl, m, kw, units, qm, p = _na_probs(q_ref, k_ref, b_ref, P, r0w)
        dom = {u: (do_ref[:, sl[u[0]]].astype(F32) * m[u[1]]).astype(BF16) for u in units}
        dp = {u: _dot(dom[u], v_ref[r0w, sl[u[0]]], "nt") for u in units}
        dvw = {u: _dot(p[u].astype(BF16), dom[u], "tn") for u in units}
        ds = {u: p[u] * (dp[u] - jnp.sum(p[u] * dp[u], axis=-1, keepdims=True)) for u in units}

        @pl.when(fresh)
        def _():
            for u in units:
                db_ref[2 * u[0] + u[1], 0] = ds[u]

        @pl.when(jnp.logical_not(fresh))
        def _():
            for u in units:
                db_ref[2 * u[0] + u[1], 0] += ds[u]

        dsb = {u: ds[u].astype(BF16) for u in units}
        dq = {u: _dot(dsb[u], kw[u[0]]) for u in units}
        dkw = {u: _dot(dsb[u], qm[u], "tn") for u in units}
        for pp in range(P):
            dq_ref[:, sl[pp]] = dq[pp, 0] * m[0] + dq[pp, 1] * m[1]
            dk_ref[r0w, sl[pp]] += dkw[pp, 0] + dkw[pp, 1]
            dv_ref[r0w, sl[pp]] += dvw[pp, 0] + dvw[pp, 1]

    qblk = pl.BlockSpec((GRID_W, W), lambda g, r: (r, g))
    full = pl.BlockSpec((S, W), lambda g, r: (0, g))
    bblk = pl.BlockSpec((2 * P, 1, GRID_W, nk), lambda g, r: (g, _na_base(r, rows), 0, 0))
    return _ride_call(
        body, ride, name=name, grid=(NA_HEADS // (2 * P), rows),
        in_specs=[qblk, full, full, bblk, qblk], out_specs=[qblk, full, full, bblk],
        out_shape=[jax.ShapeDtypeStruct((S, NA_W), F32)] * 3 + [jax.ShapeDtypeStruct((NA_HEADS, NA_WIN_R, GRID_W, nk), F32)],
        args=(qn, kn, vb, bias, do), semantics=("parallel", "arbitrary"))


def _logsig(x):
    return jnp.minimum(x, 0.0) - jnp.log(1.0 + jnp.exp(-jnp.abs(x)))


def _gla_gates(z, c0, wg, bias, name, tm=512):
    S = z.shape[0]
    tm = min(tm, S)
    zv = _V(z, c0, 128)
    W = 2 * GLA_HEADS * GLA_DK

    def body(z_ref, w_ref, b_ref, o_ref):
        pre = _dot(z_ref[...].astype(BF16), w_ref[...]) + b_ref[...]
        o_ref[...] = _logsig(pre) * (1.0 / GLA_TAU)

    return pl.pallas_call(
        body, name=name, grid=(S // tm,),
        in_specs=[zv.spec(tm, 128, lambda i: i, lambda i: 0), pl.BlockSpec((128, W), lambda i: (0, 0)),
                  pl.BlockSpec((1, W), lambda i: (0, 0))],
        out_specs=pl.BlockSpec((tm, W), lambda i: (i, 0)), out_shape=jax.ShapeDtypeStruct((S, W), F32),
        compiler_params=_cparams("parallel"),
    )(z, wg, bias)


def _gla_gates_bwd(z, c0, wg, bias, dg_f, dg_b, name, tm=512):
    S = z.shape[0]
    tm = min(tm, S)
    zv = _V(z, c0, 128)
    W = 2 * GLA_HEADS * GLA_DK

    def body(z_ref, w_ref, b_ref, dgf_ref, dgb_ref, dp_ref, db_ref):
        pre = _dot(z_ref[...].astype(BF16), w_ref[...]) + b_ref[...]
        dg = jnp.concatenate([dgf_ref[...], dgb_ref[...]], axis=-1)
        dpre = dg * (1.0 / GLA_TAU) * jax.nn.sigmoid(-pre)
        dp_ref[...] = dpre.astype(BF16)
        _acc_rows(db_ref, dpre, pl.program_id(0) == 0)

    half = pl.BlockSpec((tm, W // 2), lambda i: (i, 0))
    return pl.pallas_call(
        body, name=name, grid=(S // tm,),
        in_specs=[zv.spec(tm, 128, lambda i: i, lambda i: 0), pl.BlockSpec((128, W), lambda i: (0, 0)),
                  pl.BlockSpec((1, W), lambda i: (0, 0)), half, half],
        out_specs=[pl.BlockSpec((tm, W), lambda i: (i, 0)), pl.BlockSpec((8, W), lambda i: (0, 0))],
        out_shape=[jax.ShapeDtypeStruct((S, W), BF16), jax.ShapeDtypeStruct((8, W), F32)],
        compiler_params=_cparams("arbitrary"),
    )(z, wg, bias, dg_f, dg_b)


def _gla_chunk_terms(zqk, g, p, rev):
    C = GLA_CHUNK
    i, j = _iota((C, C), 0), _iota((C, C), 1)
    cum = jnp.where((j >= i) if rev else (j <= i), 1.0, 0.0).astype(F32)
    q2 = zqk[:, 128 * p:128 * p + 128] * (GLA_DK ** -0.5)
    k2 = zqk[:, 256 + 128 * p:256 + 128 * p + 128]
    b2 = _dot(cum, g[:, 128 * p:128 * p + 128], prec=HI)
    bl2 = b2[0:1] if rev else b2[C - 1:C]
    eb = jnp.exp(b2)
    qe2 = q2 * eb
    ke2 = k2 * jnp.exp(-b2)
    kend2 = k2 * jnp.exp(bl2 - b2)
    dec2 = jnp.exp(bl2)
    tri = (j > i) if rev else (j <= i)
    return b2, bl2, eb, qe2, ke2, kend2, dec2, tri


def _row_to_col(row):
    eye = _iota((128, 128), 0) == _iota((128, 128), 1)
    return jnp.sum(jnp.where(eye, row, 0.0), axis=1, keepdims=True)


def _col_to_row(col):
    eye = _iota((128, 128), 0) == _iota((128, 128), 1)
    return jnp.sum(jnp.where(eye, col, 0.0), axis=0, keepdims=True)


GLA_GROUP = 4


def _gla_fwd(z, c_qk, c_v, gfb, name, ride=None):
    S = z.shape[0]
    C = GLA_CHUNK
    n = S // C
    G = math.gcd(GLA_GROUP, n)
    nb, GC = n // G, G * C
    WQK = 2 * GLA_HEADS * GLA_DK
    WV = GLA_HEADS * GLA_DV
    zqk, zvv = _V(z, c_qk, WQK), _V(z, c_v, WV)

    def body(qkf_ref, vf_ref, gf_ref, qkb_ref, vb_ref, gb_ref, of_ref, ob_ref, sf_ref, sb_ref, stf, stb):
        @pl.when(pl.program_id(0) == 0)
        def _():
            stf[...] = jnp.zeros_like(stf)
            stb[...] = jnp.zeros_like(stb)

        dirs = ((False, qkf_ref, vf_ref, gf_ref, of_ref, sf_ref, stf), (True, qkb_ref, vb_ref, gb_ref, ob_ref, sb_ref, stb))
        rows = lambda gi: slice(gi * C, (gi + 1) * C)
        pairs = [(d, gi, p) for d in range(2) for gi in range(G) for p in range(GLA_HEADS // 2)]
        heads = [(d, gi, p, hh) for d, gi, p in pairs for hh in range(2)]
        mask = [_lane_mask(128, 64 * hh, 64) for hh in range(2)]
        terms = {(d, gi, p): _gla_chunk_terms(dirs[d][1][rows(gi), :], dirs[d][3][rows(gi), :], p, dirs[d][0])
                 for d, gi, p in pairs}
        dec_col = {k: _row_to_col(t[6]) for k, t in terms.items()}
        vh = {(d, gi, h): dirs[d][2][rows(gi), 128 * h:128 * h + 128].astype(BF16)
              for d in range(2) for gi in range(G) for h in range(GLA_HEADS)}
        qm = {(d, gi, p, hh): (terms[d, gi, p][3] * mask[hh]).astype(BF16) for d, gi, p, hh in heads}
        a_raw = {(d, gi, p, hh): _dot(qm[d, gi, p, hh], terms[d, gi, p][4].astype(BF16), "nt") for d, gi, p, hh in heads}
        upd = {(d, gi, p, hh): _dot((terms[d, gi, p][5] * mask[hh]).astype(BF16), vh[d, gi, 2 * p + hh], "tn")
               for d, gi, p, hh in heads}
        intra = {(d, gi, p, hh): _dot(jnp.where(terms[d, gi, p][7], a_raw[d, gi, p, hh], 0.0).astype(BF16), vh[d, gi, 2 * p + hh])
                 for d, gi, p, hh in heads}
        state = {(d, h): dirs[d][6][h] for d in range(2) for h in range(GLA_HEADS)}
        for k in range(G):
            for d in range(2):
                gi = G - 1 - k if dirs[d][0] else k
                for p in range(GLA_HEADS // 2):
                    for hh in range(2):
                        h = 2 * p + hh
                        sp = state[d, h]
                        dirs[d][4][rows(gi), 128 * h:128 * h + 128] = intra[d, gi, p, hh] + _dot(qm[d, gi, p, hh], sp.astype(BF16))
                        dirs[d][5][gi, h] = sp
                        state[d, h] = dec_col[d, gi, p] * sp + upd[d, gi, p, hh]
        for d in range(2):
            for h in range(GLA_HEADS):
                dirs[d][6][h] = state[d, h]

    fw = lambda i: i
    bw = lambda i: nb - 1 - i
    zero = lambda i: 0
    in_specs = []
    for ix, col in ((fw, 0), (bw, 1)):
        in_specs += [zqk.spec(GC, WQK, ix, zero), zvv.spec(GC, WV, ix, zero),
                     pl.BlockSpec((GC, WQK // 2), functools.partial(lambda i, ix, col: (ix(i), col), ix=ix, col=col))]
    return _ride_call(
        body, ride, name=name, grid=(nb,), in_specs=in_specs,
        out_specs=[pl.BlockSpec((GC, WV), lambda i: (i, 0)), pl.BlockSpec((GC, WV), lambda i: (nb - 1 - i, 0)),
                   pl.BlockSpec((G, GLA_HEADS, 128, 128), lambda i: (i, 0, 0, 0)),
                   pl.BlockSpec((G, GLA_HEADS, 128, 128), lambda i: (nb - 1 - i, 0, 0, 0))],
        out_shape=[jax.ShapeDtypeStruct((S, WV), F32)] * 2 + [jax.ShapeDtypeStruct((n, GLA_HEADS, 128, 128), F32)] * 2,
        scratch_shapes=[pltpu.VMEM((GLA_HEADS, 128, 128), F32)] * 2, args=(z, z, gfb, z, z, gfb), semantics=("arbitrary",))


def _gla_bwd(z, c_qk, c_v, gfb, do, s_f, s_b, name, ride=None):
    S = z.shape[0]
    C = GLA_CHUNK
    n = S // C
    G = math.gcd(GLA_GROUP, n)
    nb, GC = n // G, G * C
    WQK = 2 * GLA_HEADS * GLA_DK
    WV = GLA_HEADS * GLA_DV
    zqk, zvv = _V(z, c_qk, WQK), _V(z, c_v, WV)

    def body(qkf_ref, vf_ref, gf_ref, dof_ref, sf_ref, qkb_ref, vb_ref, gb_ref, dob_ref, sb_ref,
             dqkf_ref, dvf_ref, dgf_ref, dqkb_ref, dvb_ref, dgb_ref, dstf, dstb):
        @pl.when(pl.program_id(0) == 0)
        def _():
            dstf[...] = jnp.zeros_like(dstf)
            dstb[...] = jnp.zeros_like(dstb)

        dirs = ((False, qkf_ref, vf_ref, gf_ref, dof_ref, sf_ref, dqkf_ref, dvf_ref, dgf_ref, dstf),
                (True, qkb_ref, vb_ref, gb_ref, dob_ref, sb_ref, dqkb_ref, dvb_ref, dgb_ref, dstb))
        rows = lambda gi: slice(gi * C, (gi + 1) * C)
        pairs = [(d, gi, p) for d in range(2) for gi in range(G) for p in range(GLA_HEADS // 2)]
        heads = [(d, gi, p, hh) for d, gi, p in pairs for hh in range(2)]
        mask = [_lane_mask(128, 64 * hh, 64) for hh in range(2)]
        T = {(d, gi, p): _gla_chunk_terms(dirs[d][1][rows(gi), :], dirs[d][3][rows(gi), :], p, dirs[d][0]) for d, gi, p in pairs}
        dec_col = {k: _row_to_col(t[6]) for k, t in T.items()}
        hd = lambda d, gi, p, hh: (d, gi, 2 * p + hh)
        vh = {(d, gi, h): dirs[d][2][rows(gi), 128 * h:128 * h + 128].astype(BF16)
              for d in range(2) for gi in range(G) for h in range(GLA_HEADS)}
        doh = {(d, gi, h): dirs[d][4][rows(gi), 128 * h:128 * h + 128].astype(BF16)
               for d in range(2) for gi in range(G) for h in range(GLA_HEADS)}
        sp = {(d, gi, h): dirs[d][5][gi, h] for d in range(2) for gi in range(G) for h in range(GLA_HEADS)}
        qm = {u: (T[u[:3]][3] * mask[u[3]]).astype(BF16) for u in heads}
        kem = {u: (T[u[:3]][4] * mask[u[3]]).astype(BF16) for u in heads}
        kendm = {u: (T[u[:3]][5] * mask[u[3]]).astype(BF16) for u in heads}
        a_raw = {u: _dot(qm[u], T[u[:3]][4].astype(BF16), "nt") for u in heads}
        da_raw = {u: _dot(doh[hd(*u)], vh[hd(*u)], "nt") for u in heads}
        w_upd = {u: _dot(qm[u], doh[hd(*u)], "tn") for u in heads}
        dqe_s = {u: _dot(doh[hd(*u)], sp[hd(*u)].astype(BF16), "nt") for u in heads}
        a = {u: jnp.where(T[u[:3]][7], a_raw[u], 0.0).astype(BF16) for u in heads}
        da = {u: jnp.where(T[u[:3]][7], da_raw[u], 0.0).astype(BF16) for u in heads}
        dqe = {u: _dot(da[u], kem[u]) + dqe_s[u] for u in heads}
        dke = {u: _dot(da[u], qm[u], "tn") for u in heads}
        dv_a = {u: _dot(a[u], doh[hd(*u)], "tn") for u in heads}
        ds = {}
        for d in range(2):
            cur = [dirs[d][9][h] for h in range(GLA_HEADS)]
            for gi in (range(G) if dirs[d][0] else reversed(range(G))):
                for p in range(GLA_HEADS // 2):
                    for hh in range(2):
                        h = 2 * p + hh
                        ds[d, gi, p, hh] = cur[h]
                        cur[h] = dec_col[d, gi, p] * cur[h] + w_upd[d, gi, p, hh]
            for h in range(GLA_HEADS):
                dirs[d][9][h] = cur[h]
        dsb = {u: ds[u].astype(BF16) for u in heads}
        dv_b = {u: _dot(kendm[u], dsb[u]) for u in heads}
        dkend = {u: _dot(vh[hd(*u)], dsb[u], "nt") * mask[u[3]] for u in heads}
        ddec = {u: _col_to_row(jnp.sum(ds[u] * sp[hd(*u)], axis=1, keepdims=True)) for u in heads}
        for u in heads:
            d, gi, h = hd(*u)
            dirs[d][7][rows(gi), 128 * h:128 * h + 128] = dv_a[u] + dv_b[u]
        i, j = _iota((C, C), 0), _iota((C, C), 1)
        for d, gi, p in pairs:
            rev = dirs[d][0]
            b2, bl2, eb, qe2, ke2, kend2, dec2, _ = T[d, gi, p]
            u0, u1 = (d, gi, p, 0), (d, gi, p, 1)
            dqe2, dke2, dkend2, ddec2 = dqe[u0] + dqe[u1], dke[u0] + dke[u1], dkend[u0] + dkend[u1], ddec[u0] + ddec[u1]
            dirs[d][6][rows(gi), 128 * p:128 * p + 128] = dqe2 * eb * (GLA_DK ** -0.5)
            dirs[d][6][rows(gi), 256 + 128 * p:256 + 128 * p + 128] = dke2 * jnp.exp(-b2) + dkend2 * jnp.exp(bl2 - b2)
            dkk = dkend2 * kend2
            dbl2 = jnp.sum(dkk, axis=0, keepdims=True) + ddec2 * dec2
            edge = _iota((C, 128), 0) == (0 if rev else C - 1)
            db2 = dqe2 * qe2 - dke2 * ke2 - dkk + jnp.where(edge, dbl2, 0.0)
            cum_t = jnp.where((j <= i) if rev else (j >= i), 1.0, 0.0).astype(F32)
            dirs[d][8][rows(gi), 128 * p:128 * p + 128] = _dot(cum_t, db2, prec=HI)

    fw = lambda i: nb - 1 - i
    bw = lambda i: i
    zero = lambda i: 0
    in_specs, out_specs = [], []
    for ix, col in ((fw, 0), (bw, 1)):
        blk = functools.partial(lambda i, ix: (ix(i), 0), ix=ix)
        in_specs += [zqk.spec(GC, WQK, ix, zero), zvv.spec(GC, WV, ix, zero),
                     pl.BlockSpec((GC, WQK // 2), functools.partial(lambda i, ix, col: (ix(i), col), ix=ix, col=col)),
                     pl.BlockSpec((GC, WV), blk),
                     pl.BlockSpec((G, GLA_HEADS, 128, 128), functools.partial(lambda i, ix: (ix(i), 0, 0, 0), ix=ix))]
        out_specs += [pl.BlockSpec((GC, WQK), blk), pl.BlockSpec((GC, WV), blk), pl.BlockSpec((GC, WQK // 2), blk)]
    shapes = [jax.ShapeDtypeStruct((S, WQK), F32), jax.ShapeDtypeStruct((S, WV), F32), jax.ShapeDtypeStruct((S, WQK // 2), F32)]
    return _ride_call(
        body, ride, name=name, grid=(nb,), in_specs=in_specs, out_specs=out_specs, out_shape=shapes * 2,
        scratch_shapes=[pltpu.VMEM((GLA_HEADS, 128, 128), F32)] * 2, args=(z, z, gfb, do, s_f, z, z, gfb, do, s_b),
        semantics=("arbitrary",))


def _gla_post(o_f, o_b, z, c_r, gn, name, tm=512):
    S, WV = o_f.shape
    tm = min(tm, S)
    zr = _V(z, c_r, WV)

    def body(of_ref, ob_ref, r_ref, g_ref, y_ref):
        gr = r_ref[...]
        sil = gr * jax.nn.sigmoid(gr)
        for h in range(GLA_HEADS):
            sl = slice(GLA_DV * h, GLA_DV * (h + 1))
            o = of_ref[:, sl] + ob_ref[:, sl]
            on = o * lax.rsqrt(jnp.mean(o * o, axis=-1, keepdims=True) + EPS) * g_ref[...]
            y_ref[:, sl] = (on * sil[:, sl]).astype(BF16)

    blk = pl.BlockSpec((tm, WV), lambda i: (i, 0))
    return pl.pallas_call(
        body, name=name, grid=(S // tm,),
        in_specs=[blk, blk, zr.spec(tm, WV, lambda i: i, lambda i: 0), pl.BlockSpec((1, GLA_DV), lambda i: (0, 0))],
        out_specs=blk, out_shape=jax.ShapeDtypeStruct((S, WV), BF16), compiler_params=_cparams("parallel"),
    )(o_f, o_b, z, gn)


def _gla_post_bwd(o_f, o_b, z, c_r, gn, dy, name, tm=512):
    S, WV = o_f.shape
    tm = min(tm, S)
    zr = _V(z, c_r, WV)

    def body(of_ref, ob_ref, r_ref, g_ref, dy_ref, do_ref, dr_ref, dg_ref):
        gr = r_ref[...]
        sig = jax.nn.sigmoid(gr)
        sil = gr * sig
        dyv = dy_ref[...].astype(F32)
        dgn = jnp.zeros((tm, GLA_DV), F32)
        for h in range(GLA_HEADS):
            sl = slice(GLA_DV * h, GLA_DV * (h + 1))
            o = of_ref[:, sl] + ob_ref[:, sl]
            rstd = lax.rsqrt(jnp.mean(o * o, axis=-1, keepdims=True) + EPS)
            xhat = o * rstd
            don = dyv[:, sl] * sil[:, sl]
            dr_ref[:, sl] = (dyv[:, sl] * xhat * g_ref[...] * (sig[:, sl] * (1.0 + gr[:, sl] * (1.0 - sig[:, sl])))).astype(BF16)
            dxhat = don * g_ref[...]
            do_ref[:, sl] = rstd * (dxhat - xhat * jnp.mean(dxhat * xhat, axis=-1, keepdims=True))
            dgn = dgn + don * xhat
        _acc_rows(dg_ref, dgn, pl.program_id(0) == 0)

    blk = pl.BlockSpec((tm, WV), lambda i: (i, 0))
    return pl.pallas_call(
        body, name=name, grid=(S // tm,),
        in_specs=[blk, blk, zr.spec(tm, WV, lambda i: i, lambda i: 0), pl.BlockSpec((1, GLA_DV), lambda i: (0, 0)), blk],
        out_specs=[blk, blk, pl.BlockSpec((8, GLA_DV), lambda i: (0, 0))],
        out_shape=[jax.ShapeDtypeStruct((S, WV), F32), jax.ShapeDtypeStruct((S, WV), BF16), jax.ShapeDtypeStruct((8, GLA_DV), F32)],
        compiler_params=_cparams("arbitrary"),
    )(o_f, o_b, z, gn, dy)


def _gla_assemble(dqk_f, dqk_b, dv_f, dv_b, dgr, name, tm=512):
    S = dqk_f.shape[0]
    tm = min(tm, S)

    def body(a_ref, b_ref, c_ref, d_ref, r_ref, o_ref):
        o_ref[:, 0:512] = (a_ref[...] + b_ref[...]).astype(BF16)
        o_ref[:, 512:1024] = (c_ref[...] + d_ref[...]).astype(BF16)
        o_ref[:, 1024:1536] = r_ref[...]

    blk = pl.BlockSpec((tm, 512), lambda i: (i, 0))
    return pl.pallas_call(
        body, name=name, grid=(S // tm,), in_specs=[blk] * 5, out_specs=pl.BlockSpec((tm, 1536), lambda i: (i, 0)),
        out_shape=jax.ShapeDtypeStruct((S, 1536), BF16), compiler_params=_cparams("parallel"),
    )(dqk_f, dqk_b, dv_f, dv_b, dgr)


def _rope(r, cos, sg):
    return r * cos + pltpu.roll(r, 64, 1) * sg


def _unrope(dy, cos, sg):
    return dy * cos + pltpu.roll(dy * sg, 64, 1)


def _mla_prep(z, c_q, c_kr, wuq, wukv, g_cq, g_ckv, g_q, g_k, cos, sg, name, tm=256):
    S = z.shape[0]
    tm = min(tm, S)
    zc, zk = _V(z, c_q, 2 * MLA_RANK), _V(z, c_kr, 128)
    inv = 1.0 / MLA_QK

    def body(zc_ref, zk_ref, wuq_ref, wukv_ref, gcq_ref, gckv_ref, gq_ref, gk_ref, cos_ref, sg_ref,
             q_ref, k_ref, v_ref, cqn_ref, ckvn_ref):
        def norm(xv, gv):
            return (xv * lax.rsqrt(jnp.mean(xv * xv, axis=-1, keepdims=True) + EPS) * gv).astype(BF16)

        cqn = norm(zc_ref[:, 0:MLA_RANK], gcq_ref[...])
        ckvn = norm(zc_ref[:, MLA_RANK:2 * MLA_RANK], gckv_ref[...])
        cqn_ref[...] = cqn
        ckvn_ref[...] = ckvn
        qf = _dot(cqn, wuq_ref[...])
        kv = _dot(ckvn, wukv_ref[...])
        kr = zk_ref[...]
        krss = jnp.sum(kr * kr, axis=-1, keepdims=True)
        cosv, sgv = cos_ref[...], sg_ref[...]
        gq, gk = gq_ref[...], gk_ref[...]
        for h in range(MLA_HEADS):
            qh = qf[:, MLA_SLOT * h:MLA_SLOT * (h + 1)]
            qhn = qh * lax.rsqrt(jnp.sum(qh * qh, axis=-1, keepdims=True) * inv + EPS) * gq
            q_ref[:, MLA_SLOT * h:MLA_SLOT * h + 128] = (qhn[:, 0:128] * MLA_QSCALE).astype(BF16)
            q_ref[:, MLA_SLOT * h + 128:MLA_SLOT * (h + 1)] = (_rope(qhn[:, 128:256], cosv, sgv) * MLA_QSCALE).astype(BF16)
            kn = kv[:, 256 * h:256 * h + 128]
            rstd = lax.rsqrt((jnp.sum(kn * kn, axis=-1, keepdims=True) + krss) * inv + EPS)
            k_ref[:, MLA_SLOT * h:MLA_SLOT * h + 128] = (kn * rstd * gk[:, 0:128]).astype(BF16)
            k_ref[:, MLA_SLOT * h + 128:MLA_SLOT * (h + 1)] = _rope(kr * rstd * gk[:, 128:256], cosv, sgv).astype(BF16)
            v_ref[:, 128 * h:128 * (h + 1)] = kv[:, 256 * h + 128:256 * (h + 1)].astype(BF16)

    row = lambda w: pl.BlockSpec((tm, w), lambda i: (i, 0))
    const = lambda r, w: pl.BlockSpec((r, w), lambda i: (0, 0))
    W = MLA_HEADS * MLA_SLOT
    return pl.pallas_call(
        body, name=name, grid=(S // tm,),
        in_specs=[zc.spec(tm, 2 * MLA_RANK, lambda i: i, lambda i: 0), zk.spec(tm, 128, lambda i: i, lambda i: 0),
                  const(MLA_RANK, W), const(MLA_RANK, W), const(1, MLA_RANK), const(1, MLA_RANK), const(1, MLA_SLOT),
                  const(1, MLA_SLOT), row(128), row(128)],
        out_specs=[row(W), row(W), row(MLA_HEADS * MLA_V), row(MLA_RANK), row(MLA_RANK)],
        out_shape=[jax.ShapeDtypeStruct((S, W), BF16), jax.ShapeDtypeStruct((S, W), BF16),
                   jax.ShapeDtypeStruct((S, MLA_HEADS * MLA_V), BF16), jax.ShapeDtypeStruct((S, MLA_RANK), BF16),
                   jax.ShapeDtypeStruct((S, MLA_RANK), BF16)],
        compiler_params=_cparams("parallel"),
    )(z, z, wuq, wukv, g_cq, g_ckv, g_q, g_k, cos, sg)


def _mla_prep_bwd(z, c_kr, cqn, ckvn, wuq, wukv, g_q, g_k, cos, sg, dq, dk, dv, name, tm=256):
    S = z.shape[0]
    tm = min(tm, S)
    zk = _V(z, c_kr, 128)
    inv = 1.0 / MLA_QK

    def body(zk_ref, cqn_ref, ckvn_ref, wuq_ref, wukv_ref, gq_ref, gk_ref, cos_ref, sg_ref, dq_ref, dk_ref, dv_ref,
             dqf_ref, dkv_ref, dkr_ref, dgq_ref, dgk_ref):
        first = pl.program_id(0) == 0
        qf = _dot(cqn_ref[...], wuq_ref[...])
        kv = _dot(ckvn_ref[...], wukv_ref[...])
        kr = zk_ref[...]
        krss = jnp.sum(kr * kr, axis=-1, keepdims=True)
        cosv, sgv = cos_ref[...], sg_ref[...]
        gq, gk = gq_ref[...], gk_ref[...]
        dkr = jnp.zeros((tm, 128), F32)
        dgq = jnp.zeros((tm, MLA_SLOT), F32)
        dgkn = jnp.zeros((tm, 128), F32)
        dgkr = jnp.zeros((tm, 128), F32)
        for h in range(MLA_HEADS):
            qh = qf[:, MLA_SLOT * h:MLA_SLOT * (h + 1)]
            rstd = lax.rsqrt(jnp.sum(qh * qh, axis=-1, keepdims=True) * inv + EPS)
            xhat = qh * rstd
            dyn = jnp.concatenate([dq_ref[:, MLA_SLOT * h:MLA_SLOT * h + 128],
                                   _unrope(dq_ref[:, MLA_SLOT * h + 128:MLA_SLOT * (h + 1)], cosv, sgv)], axis=-1)
            dxhat = dyn * gq
            dqf_ref[:, MLA_SLOT * h:MLA_SLOT * (h + 1)] = (
                rstd * (dxhat - xhat * (jnp.sum(dxhat * xhat, axis=-1, keepdims=True) * inv))).astype(BF16)
            dgq = dgq + dyn * xhat

            kn = kv[:, 256 * h:256 * h + 128]
            rstd = lax.rsqrt((jnp.sum(kn * kn, axis=-1, keepdims=True) + krss) * inv + EPS)
            xn, xr = kn * rstd, kr * rstd
            dyn_n = dk_ref[:, MLA_SLOT * h:MLA_SLOT * h + 128] * (1.0 / MLA_QSCALE)
            dyn_r = _unrope(dk_ref[:, MLA_SLOT * h + 128:MLA_SLOT * (h + 1)] * (1.0 / MLA_QSCALE), cosv, sgv)
            dxn, dxr = dyn_n * gk[:, 0:128], dyn_r * gk[:, 128:256]
            proj = (jnp.sum(dxn * xn, axis=-1, keepdims=True) + jnp.sum(dxr * xr, axis=-1, keepdims=True)) * inv
            dkv_ref[:, 256 * h:256 * h + 128] = (rstd * (dxn - xn * proj)).astype(BF16)
            dkv_ref[:, 256 * h + 128:256 * (h + 1)] = dv_ref[:, 128 * h:128 * (h + 1)].astype(BF16)
            dkr = dkr + rstd * (dxr - xr * proj)
            dgkn = dgkn + dyn_n * xn
            dgkr = dgkr + dyn_r * xr
        dkr_ref[...] = dkr.astype(BF16)
        _acc_rows(dgq_ref, dgq, first)
        _acc_rows(dgk_ref, jnp.concatenate([dgkn, dgkr], axis=-1), first)

    row = lambda w: pl.BlockSpec((tm, w), lambda i: (i, 0))
    const = lambda r, w: pl.BlockSpec((r, w), lambda i: (0, 0))
    W = MLA_HEADS * MLA_SLOT
    return pl.pallas_call(
        body, name=name, grid=(S // tm,),
        in_specs=[zk.spec(tm, 128, lambda i: i, lambda i: 0), row(MLA_RANK), row(MLA_RANK), const(MLA_RANK, W),
                  const(MLA_RANK, W), const(1, MLA_SLOT), const(1, MLA_SLOT), row(128), row(128), row(W), row(W),
                  row(MLA_HEADS * MLA_V)],
        out_specs=[row(W), row(W), row(128), const(8, MLA_SLOT), const(8, MLA_SLOT)],
        out_shape=[jax.ShapeDtypeStruct((S, W), BF16), jax.ShapeDtypeStruct((S, W), BF16), jax.ShapeDtypeStruct((S, 128), BF16),
                   jax.ShapeDtypeStruct((8, MLA_SLOT), F32), jax.ShapeDtypeStruct((8, MLA_SLOT), F32)],
        compiler_params=_cparams("arbitrary"),
    )(z, cqn, ckvn, wuq, wukv, g_q, g_k, cos, sg, dq, dk, dv)


def _exp2_rows(s2):
    e = jnp.exp2(s2 - jnp.max(s2, axis=-1, keepdims=True))
    return e, 1.0 / jnp.sum(e, axis=-1, keepdims=True)


MLA_SUB = 256


def _mla_attn(q, k, v, name, tq=512, ride=None):
    S = q.shape[0]
    tq = min(tq, S)
    sub = math.gcd(MLA_SUB, tq)
    rows = lambda i: slice(i * sub, (i + 1) * sub)

    def body(q_ref, k_ref, v_ref, o_ref):
        def rest(i, s2):
            e, rl = _exp2_rows(s2)
            o_ref[rows(i), :] = (_dot(e.astype(BF16), v_ref[...]) * rl).astype(BF16)

        _one_ahead(tq // sub, lambda i: _dot(q_ref[rows(i), :], k_ref[...], "nt"), rest)

    (o,), got = _ride_call(
        body, ride, name=name, grid=(MLA_HEADS, S // tq),
        in_specs=[pl.BlockSpec((tq, MLA_SLOT), lambda h, i: (i, h)), pl.BlockSpec((S, MLA_SLOT), lambda h, i: (0, h)),
                  pl.BlockSpec((S, MLA_V), lambda h, i: (0, h))],
        out_specs=[pl.BlockSpec((tq, MLA_V), lambda h, i: (i, h))],
        out_shape=[jax.ShapeDtypeStruct((S, MLA_HEADS * MLA_V), BF16)], args=(q, k, v), semantics=("parallel", "parallel"))
    return o, got


def _mla_attn_bwd(q, k, v, do, name, tq=512, ride=None):
    S = q.shape[0]
    tq = min(tq, S)
    sub = math.gcd(MLA_SUB, tq)
    rows = lambda i: slice(i * sub, (i + 1) * sub)
    scale = MLA_QK ** -0.5

    def body(q_ref, k_ref, v_ref, do_ref, dq_ref, dk_ref, dv_ref):
        @pl.when(pl.program_id(1) == 0)
        def _():
            dk_ref[...] = jnp.zeros_like(dk_ref)
            dv_ref[...] = jnp.zeros_like(dv_ref)

        def matmuls(i):
            return _dot(q_ref[rows(i), :], k_ref[...], "nt"), _dot(do_ref[rows(i), :], v_ref[...], "nt")

        def rest(i, s2_dp):
            s2, dp = s2_dp
            e, rl = _exp2_rows(s2)
            dp = dp * (scale * rl)
            ds = (e * (dp - jnp.sum(e * dp, axis=-1, keepdims=True) * rl)).astype(BF16)
            dq_ref[rows(i), :] = _dot(ds, k_ref[...])
            dk_ref[...] += _dot(ds, q_ref[rows(i), :], "tn")
            dv_ref[...] += _dot(e.astype(BF16), (do_ref[rows(i), :].astype(F32) * rl).astype(BF16), "tn")

        _one_ahead(tq // sub, matmuls, rest)

    W = MLA_HEADS * MLA_SLOT
    return _ride_call(
        body, ride, name=name, grid=(MLA_HEADS, S // tq),
        in_specs=[pl.BlockSpec((tq, MLA_SLOT), lambda h, i: (i, h)), pl.BlockSpec((S, MLA_SLOT), lambda h, i: (0, h)),
                  pl.BlockSpec((S, MLA_V), lambda h, i: (0, h)), pl.BlockSpec((tq, MLA_V), lambda h, i: (i, h))],
        out_specs=[pl.BlockSpec((tq, MLA_SLOT), lambda h, i: (i, h)), pl.BlockSpec((S, MLA_SLOT), lambda h, i: (0, h)),
                   pl.BlockSpec((S, MLA_V), lambda h, i: (0, h))],
        out_shape=[jax.ShapeDtypeStruct((S, W), F32), jax.ShapeDtypeStruct((S, W), F32),
                   jax.ShapeDtypeStruct((S, MLA_HEADS * MLA_V), F32)],
        args=(q, k, v, do), semantics=("parallel", "arbitrary"))


def _merge(ys, ws, z, name, tm=256):
    S = z.shape[0]
    D = ws[0].shape[1]
    tm = min(tm, S)
    zg = _V(z, 0, 3 * D)

    def body(y0, y1, y2, w0, w1, w2, g_ref, m_ref, p0, p1, p2):
        acc = jnp.zeros((tm, D), F32)
        for i, (y_ref, w_ref, p_ref) in enumerate(((y0, w0, p0), (y1, w1, p1), (y2, w2, p2))):
            pv = _dot(y_ref[...], w_ref[...])
            p_ref[...] = pv.astype(BF16)
            acc = acc + jax.nn.sigmoid(g_ref[:, D * i:D * (i + 1)]) * pv
        m_ref[...] = acc.astype(BF16)

    yb = pl.BlockSpec((tm, ys[0].shape[1]), lambda i: (i, 0))
    wb = pl.BlockSpec(ws[0].shape, lambda i: (0, 0))
    ob = pl.BlockSpec((tm, D), lambda i: (i, 0))
    return pl.pallas_call(
        body, name=name, grid=(S // tm,), in_specs=[yb] * 3 + [wb] * 3 + [zg.spec(tm, 3 * D, lambda i: i, lambda i: 0)],
        out_specs=[ob] * 4, out_shape=[jax.ShapeDtypeStruct((S, D), BF16)] * 4, compiler_params=_cparams("parallel"),
    )(*ys, *ws, z)


def _merge_bwd(dmixed, ps, z, name, tm=256):
    S, D = dmixed.shape
    tm = min(tm, S)
    zg = _V(z, 0, 3 * D)

    def body(dm_ref, p0, p1, p2, g_ref, d0, d1, d2, dg_ref):
        dm = dm_ref[...]
        for i, (p_ref, d_ref) in enumerate(((p0, d0), (p1, d1), (p2, d2))):
            gt = jax.nn.sigmoid(g_ref[:, D * i:D * (i + 1)])
            d_ref[...] = (dm * gt).astype(BF16)
            dg_ref[:, D * i:D * (i + 1)] = (dm * p_ref[...].astype(F32) * gt * (1.0 - gt)).astype(BF16)

    ob = pl.BlockSpec((tm, D), lambda i: (i, 0))
    return pl.pallas_call(
        body, name=name, grid=(S // tm,), in_specs=[ob] * 4 + [zg.spec(tm, 3 * D, lambda i: i, lambda i: 0)],
        out_specs=[ob] * 3 + [pl.BlockSpec((tm, 3 * D), lambda i: (i, 0))],
        out_shape=[jax.ShapeDtypeStruct((S, D), BF16)] * 3 + [jax.ShapeDtypeStruct((S, 3 * D), BF16)],
        compiler_params=_cparams("parallel"),
    )(dmixed, *ps, z)


def _loss_head(y, target, name, tm=512):
    S, D = y.shape
    tm = min(tm, S)

    def body(y_ref, t_ref, dy_ref, l_ref):
        e = y_ref[...] - t_ref[...]
        dy_ref[...] = e * (1.0 / D)
        sq = e * e
        part = jnp.sum(sq.reshape(tm // 8, 8, D), axis=0)
        part = jnp.sum(part.reshape(8, D // 128, 128), axis=1) * (0.5 / D)

        @pl.when(pl.program_id(0) == 0)
        def _():
            l_ref[...] = part

        @pl.when(pl.program_id(0) != 0)
        def _():
            l_ref[...] += part

    blk = pl.BlockSpec((tm, D), lambda i: (i, 0))
    return pl.pallas_call(
        body, name=name, grid=(S // tm,), in_specs=[blk, blk], out_specs=[blk, pl.BlockSpec((8, 128), lambda i: (0, 0))],
        out_shape=[jax.ShapeDtypeStruct((S, D), F32), jax.ShapeDtypeStruct((8, 128), F32)],
        compiler_params=_cparams("arbitrary"),
    )(y, target)


def _fold(parts, name, fold=None):
    L, _, W = parts.shape
    assert L <= 8

    def body(*refs):
        p_ref, o_ref = refs[0], refs[-1]
        rows = [jnp.sum(p_ref[l], axis=0, keepdims=True) for l in range(L)]
        rows += [jnp.zeros((1, W), F32)] * (8 - L)
        sums = jnp.concatenate(rows, axis=0)
        o_ref[...] = sums if fold is None else _dot(sums, refs[1][...], prec=HI)

    args = (parts,) if fold is None else (parts, jnp.asarray(fold))
    wout = W if fold is None else 128
    return pl.pallas_call(body, name=name, out_shape=jax.ShapeDtypeStruct((8, wout), F32))(*args)[:L]


def _adamw(w, g, m, v, name, q=None, ride=None):
    R, C = w.shape
    tr = R
    for cand in (512, 256, 128, 64, 32, 16, 8):
        if R % cand == 0 and cand * C * 4 <= 2 * 2**20:
            tr = cand
            break

    def body(*refs):
        if q is None:
            w_ref, g_ref, m_ref, v_ref, d_ref, nm_ref, nv_ref = refs
            gv = g_ref[...]
        else:
            w_ref, g_ref, q_ref, m_ref, v_ref, go_ref, d_ref, nm_ref, nv_ref = refs
            gv = g_ref[...] + q_ref[...]
            go_ref[...] = gv
        mn = ADAM_B1 * m_ref[...] + (1.0 - ADAM_B1) * gv
        vn = ADAM_B2 * v_ref[...] + (1.0 - ADAM_B2) * (gv * gv)
        nm_ref[...] = mn
        nv_ref[...] = vn
        m_hat = mn / (1.0 - ADAM_B1 ** ADAM_STEP)
        v_hat = vn / (1.0 - ADAM_B2 ** ADAM_STEP)
        d_ref[...] = -ADAM_LR * (m_hat / (jnp.sqrt(v_hat) + ADAM_EPS) + ADAM_WD * w_ref[...])

    blk = pl.BlockSpec((tr, C), lambda i: (i, 0))
    args = (w, g, m, v) if q is None else (w, g, q, m, v)
    nout = 3 if q is None else 4
    outs, got = _ride_call(
        body, ride, name=name, grid=(R // tr,), in_specs=[blk] * len(args), out_specs=[blk] * nout,
        out_shape=[jax.ShapeDtypeStruct((R, C), F32)] * nout, args=args, semantics=("parallel",))
    return outs if ride is None else (outs, got)


def _sibling_exchange(srcs, name, ride=()):
    n, k = len(srcs), len(ride)

    def body(*refs):
        src_refs, dst_refs = refs[:n], refs[n + 2 * k:2 * n + 2 * k]
        send_sems, recv_sems = refs[2 * n + 3 * k:2 * n + 3 * k + 2]
        x, y, c = lax.axis_index("x"), lax.axis_index("y"), lax.axis_index("c")
        if k:
            start, finish = _ride_ops(ride, refs[n:n + k], refs[2 * n + 2 * k:2 * n + 3 * k], *refs[2 * n + 3 * k + 2:])
            start()
        copies = [pltpu.make_async_remote_copy(src_ref=src_refs[t], dst_ref=dst_refs[t], send_sem=send_sems.at[t],
                                               recv_sem=recv_sems.at[t], device_id=(x, y, 1 - c), device_id_type=MESH)
                  for t in range(n)]
        for cp in copies:
            cp.start()
        for cp in copies:
            cp.wait()
        if k:
            finish()

    res = pl.pallas_call(
        body, name=name, in_specs=[_ANY] * (n + 2 * k), out_specs=[_ANY] * (n + k),
        out_shape=[jax.ShapeDtypeStruct(s.shape, s.dtype) for s in srcs] + [jax.ShapeDtypeStruct(it.dst.shape, it.dst.dtype) for it in ride],
        input_output_aliases={n + k + t: n + t for t in range(k)},
        scratch_shapes=[pltpu.SemaphoreType.DMA((n,)), pltpu.SemaphoreType.DMA((n,))] + (_RIDE_SEMS(k) if k else []),
    )(*srcs, *[it.src for it in ride], *[it.dst for it in ride])
    return res[:n], {it.name: o for it, o in zip(ride, res[n:])}


def _allreduce_small(v, name):
    R = v.shape[0]

    def body(v_ref, o_ref, slots, send_sems, recv_sems):
        x, y, c = lax.axis_index("x"), lax.axis_index("y"), lax.axis_index("c")
        me = 4 * x + 2 * y + c
        slots[me] = v_ref[...]
        sent = []
        for r in range(1, 8):
            fx, fy, fc = (r >> 2) & 1, (r >> 1) & 1, r & 1
            px, py, pc = (1 - x) if fx else x, (1 - y) if fy else y, (1 - c) if fc else c
            peer = 4 * px + 2 * py + pc

            def copy(slot, r=r, px=px, py=py, pc=pc):
                return pltpu.make_async_remote_copy(
                    src_ref=v_ref, dst_ref=slots.at[slot], send_sem=send_sems.at[r - 1], recv_sem=recv_sems.at[r - 1],
                    device_id=(px, py, pc), device_id_type=MESH)

            cp = copy(me)
            cp.start()
            sent.append((cp, copy(peer)))
        for cp, arrival in sent:
            cp.wait_send()
            arrival.wait_recv()
        acc = slots[0]
        for k in range(1, 8):
            acc = acc + slots[k]
        o_ref[...] = acc

    vm = pl.BlockSpec(memory_space=pltpu.VMEM)
    return pl.pallas_call(
        body, name=name, in_specs=[vm], out_specs=vm, out_shape=jax.ShapeDtypeStruct((R, 128), F32),
        scratch_shapes=[pltpu.VMEM((8, R, 128), F32), pltpu.SemaphoreType.DMA((7,)), pltpu.SemaphoreType.DMA((7,))],
    )(v)


def _sum4(recv, name, tr=512):
    _, R, W = recv.shape
    tr = _tile(R, tr)
    assert R % tr == 0

    def body(r_ref, o_ref):
        o_ref[...] = ((r_ref[0].astype(F32) + r_ref[1].astype(F32)) + r_ref[2].astype(F32)) + r_ref[3].astype(F32)

    return pl.pallas_call(
        body, name=name, grid=(R // tr,), in_specs=[pl.BlockSpec((4, tr, W), lambda i: (0, i, 0))],
        out_specs=pl.BlockSpec((tr, W), lambda i: (i, 0)), out_shape=jax.ShapeDtypeStruct((R, W), F32),
        compiler_params=_cparams("parallel"),
    )(recv)


W_NAMES = ("ffn1_norm", "ffn1_w1", "ffn1_w3", "ffn1_w2", "mix_norm", "w_in", "na_q_norm", "na_k_norm", "na_rpb",
           "gla_gf_up", "gla_gf_bias", "gla_gb_up", "gla_gb_bias", "gla_out_norm", "mla_cq_norm", "mla_ckv_norm",
           "mla_w_uq", "mla_w_ukv", "mla_q_norm", "mla_k_norm", "w_br_na", "w_br_gla", "w_br_mla", "w_out",
           "ffn2_norm", "ffn2_w1", "ffn2_w3", "ffn2_w2")
SHARDED = {"ffn1_w1": 2, "ffn1_w3": 2, "ffn1_w2": 1, "w_in": 2, "gla_gf_up": 2, "gla_gb_up": 2, "mla_w_uq": 2,
           "mla_w_ukv": 2, "w_br_na": 2, "w_br_gla": 2, "w_br_mla": 2, "w_out": 1, "ffn2_w1": 2, "ffn2_w3": 2,
           "ffn2_w2": 1}
REPLICATED = tuple(n for n in W_NAMES if n not in SHARDED)
FFN_W = ("ffn1_w1", "ffn1_w3", "ffn1_w2", "ffn2_w1", "ffn2_w3", "ffn2_w2")


def _cols_of(parts, lo, hi):
    out, off = [], 0
    for a in parts:
        w = a.shape[-1]
        s, e = max(lo, off), min(hi, off + w)
        if s < e:
            out.append(a[..., s - off:e - off])
        off += w
    return out


def _win_layout(pieces, D):
    z = lambda n: [jnp.zeros(pieces[0].shape[:-1] + (n,), pieces[0].dtype)]
    c = lambda lo, hi: _cols_of(pieces, lo, hi)
    return jnp.concatenate(c(O_GATES, O_GATES + 3 * D) + c(0, O_GFL) + c(O_CQ, O_KR) + c(O_GFL, O_CQ) + z(96)
                           + c(O_KR, O_KR + 32) + z(32) + c(O_KR + 32, O_KR + 64) + z(32), axis=-1)


def _win_split4(dw_segs, D):
    gates, na, gla, cq, ckv, lr, kr = dw_segs
    parts = [na, gla, lr[:, 0:32], cq, ckv, kr[:, 0:32], kr[:, 64:96], gates]
    n = (O_GATES + 3 * D) // 4
    return jnp.stack([jnp.concatenate(_cols_of(parts, j * n, (j + 1) * n), axis=1) for j in range(4)]).astype(BF16)


def _uq_layout(w):
    s = w.shape[:-1]
    w = w.reshape(s + (MLA_HEADS, MLA_QK))
    z = jnp.zeros(s + (MLA_HEADS, 32), w.dtype)
    return jnp.concatenate([w[..., :160], z, w[..., 160:], z], axis=-1).reshape(s + (MLA_HEADS * MLA_SLOT,))


def _uq_unlayout(dw):
    s = dw.shape[:-1]
    dw = dw.reshape(s + (MLA_HEADS, MLA_SLOT))
    return jnp.concatenate([dw[..., :160], dw[..., 192:224]], axis=-1).reshape(s + (MLA_HEADS * MLA_QK,))


def _slot_layout(g):
    z = jnp.zeros(g.shape[:-1] + (32,), g.dtype)
    return jnp.concatenate([g[..., :160], z, g[..., 160:], z], axis=-1)


def _slot_unlayout(g):
    return jnp.concatenate([g[..., :160], g[..., 192:224]], axis=-1)


def _layer_fwd(x, w, cos, sg, rides, mixer=None):
    D = x.shape[1]
    NA, GL, ML, LR, KR = 3 * D, 3 * D + 1536, 3 * D + 3072, 3 * D + 3584, 3 * D + 3712
    got = {}
    x1, f1, arrived = _ffn_fwd(x, w["ffn1_norm"], w["ffn1_w1"], w["ffn1_w3"], w["ffn1_w2"], "ffn1", ride=rides.get("ffn1_up"))
    got.update(arrived)
    if mixer is not None:
        w = {**w, **mixer(arrived)}
    h = _rms_fwd(x1, w["mix_norm"], "mix_rms")
    nz = w["w_in"].shape[1]
    z, arrived = _mm([(h, w["w_in"])], "nn", F32, "w_in", tm=1024, tn=_tile(nz, 1280), ride=rides.get("w_in", []))
    got.update(arrived)
    qn, kn, vb = _na_prep(z, NA, w["na_gq"], w["na_gk"], "na_prep")
    bias = _rpb_expand(w["na_rpb"], "rpb_expand")
    y_na, arrived = _na_attn(qn, kn, vb, bias, "na_attn", ride=rides.get("na_attn"))
    got.update(arrived)
    gfb = _gla_gates(z, LR, w["gla_wg"], w["gla_gbias"], "gla_gates")
    (o_f, o_b, s_f, s_b), arrived = _gla_fwd(z, GL, GL + 512, gfb, "gla_fwd", ride=rides.get("gla_fwd"))
    got.update(arrived)
    y_gla = _gla_post(o_f, o_b, z, GL + 1024, w["gla_out_norm"], "gla_post")
    q, k, v, cqn, ckvn = _mla_prep(z, ML, KR, w["mla_wuq"], w["mla_w_ukv"], w["mla_cq_norm"], w["mla_ckv_norm"],
                                   w["mla_gq"], w["mla_gk"], cos, sg, "mla_prep")
    y_mla, arrived = _mla_attn(q, k, v, "mla_attn", ride=rides.get("mla_attn"))
    got.update(arrived)
    mixed, p0, p1, p2 = _merge([y_na, y_gla, y_mla], [w["w_br_na"], w["w_br_gla"], w["w_br_mla"]], z, "merge")
    x2 = _mm([(mixed, w["w_out"])], "nn", F32, "w_out", tm=512, tn=1024, res=x1)
    x3, f2, _ = _ffn_fwd(x2, w["ffn2_norm"], got["ffn2_w1"], got["ffn2_w3"], got["ffn2_w2"], "ffn2")
    saved = dict(x=x, x1=x1, x2=x2, f1=f1, f2=f2, h=h, z=z, qn=qn, kn=kn, vb=vb, bias=bias, y_na=y_na, gfb=gfb, o_f=o_f,
                 o_b=o_b, s_f=s_f, s_b=s_b, y_gla=y_gla, q=q, k=k, v=v, cqn=cqn, ckvn=ckvn, y_mla=y_mla, mixed=mixed,
                 p0=p0, p1=p1, p2=p2)
    return x3, saved, got, w


def _split4(a, axis):
    n = a.shape[axis] // 4
    return jnp.stack([lax.slice_in_dim(a, j * n, (j + 1) * n, axis=axis) for j in range(4)]).astype(BF16)


def _layer_bwd(dx3, w, sv, cos, sg, bufs, recv, layer, prev, flush=False):
    D = dx3.shape[1]
    at_layer = lambda chip: (chip, layer)
    pick = lambda *names: [prev[n] for n in names if n in prev]
    recv = dict(recv)
    NA, GL, ML, LR, KR = 3 * D, 3 * D + 1536, 3 * D + 3072, 3 * D + 3584, 3 * D + 3712
    z = sv["z"]
    g = {}
    dx2, g["ffn2_norm"], (g["ffn2_w1"], g["ffn2_w3"], g["ffn2_w2"]), got = _ffn_bwd(
        dx3, sv["x2"], w["ffn2_norm"], w["ffn2_w1"], w["ffn2_w3"], w["ffn2_w2"], sv["f2"], "ffn2",
        (bufs["ffn2_w1"], bufs["ffn2_w3"], bufs["ffn2_w2"]), layer, ride_down=pick("ffn1_w1"), ride_dh=pick("ffn1_w3"))
    recv.update(got)
    dmixed = _mm([(dx2, w["w_out"])], "nt", F32, "w_out_dx", tm=512, tn=512)
    g["w_out"] = _mm([(sv["mixed"], dx2)], "tn", F32, "w_out_dw", tm=D, tn=256)
    d0, d1, d2, dgates = _merge_bwd(dmixed, [sv["p0"], sv["p1"], sv["p2"]], z, "merge_bwd")
    dys = []
    for d, y, nm, dt in ((d0, sv["y_na"], "w_br_na", BF16), (d1, sv["y_gla"], "w_br_gla", F32), (d2, sv["y_mla"], "w_br_mla", BF16)):
        dys.append(_mm([(d, w[nm])], "nt", dt, nm + "_dy", tm=512, tn=512))
        g[nm] = _mm([(y, d)], "tn", F32, nm + "_dw", tm=512, tn=512)
    (dqn, dkn, dvn, dbias), got = _na_attn_bwd(sv["qn"], sv["kn"], sv["vb"], sv["bias"], dys[0], "na_attn_bwd",
                                               ride=pick("ffn1_w2", "mla_w_uq", "mla_w_ukv", "gla_gf_up", "gla_gb_up"))
    recv.update(got)
    dz_na, g["na_gq"], g["na_gk"] = _na_prep_bwd(z, NA, w["na_gq"], w["na_gk"], dqn, dkn, dvn, "na_prep_bwd")
    g["na_rpb"] = _rpb_reduce(dbias, "rpb_reduce")
    do, dgr, g["gla_out_norm"] = _gla_post_bwd(sv["o_f"], sv["o_b"], z, GL + 1024, w["gla_out_norm"], dys[1], "gla_post_bwd")
    (dqk_f, dv_f, dg_f, dqk_b, dv_b, dg_b), _ = _gla_bwd(z, GL, GL + 512, sv["gfb"], do, sv["s_f"], sv["s_b"], "gla_bwd")
    dz_gla = _gla_assemble(dqk_f, dqk_b, dv_f, dv_b, dgr, "gla_assemble")
    dpre, g["gla_gbias"] = _gla_gates_bwd(z, LR, w["gla_wg"], w["gla_gbias"], dg_f, dg_b, "gla_gates_bwd")
    g["gla_wg"] = _mm([(_V(z, LR, 128), dpre)], "tn", F32, "gla_wg_dw", tm=128, tn=512)
    dz_lr = _mm([(dpre, w["gla_wg"])], "nt", BF16, "gla_wg_dz", tm=512, tn=128)
    own = lambda n: _Ride(n, g[n], at_layer, recv[n], at_layer)
    (dq, dk, dv), got = _mla_attn_bwd(sv["q"], sv["k"], sv["v"], dys[2], "mla_attn_bwd",
                                      ride=pick("w_in") + [own("ffn2_w1"), own("ffn2_w3")])
    recv.update(got)
    dqf, dkv, dz_kr, g["mla_gq"], g["mla_gk"] = _mla_prep_bwd(
        z, KR, sv["cqn"], sv["ckvn"], w["mla_wuq"], w["mla_w_ukv"], w["mla_gq"], w["mla_gk"], cos, sg, dq, dk, dv, "mla_prep_bwd")
    g["mla_wuq"] = _mm([(sv["cqn"], dqf)], "tn", F32, "mla_wuq_dw", tm=256, tn=512)
    g["mla_w_ukv"] = _mm([(sv["ckvn"], dkv)], "tn", F32, "mla_wukv_dw", tm=256, tn=512)
    dcqn = _mm([(dqf, w["mla_wuq"])], "nt", F32, "mla_wuq_dx", tm=512, tn=256)
    dckvn = _mm([(dkv, w["mla_w_ukv"])], "nt", F32, "mla_wukv_dx", tm=512, tn=256)
    dz_cq, dg_cq = _rms_bwd(_V(z, ML, MLA_RANK), w["mla_cq_norm"], dcqn, "mla_cq_rms_bwd", out_dtype=BF16)
    dz_ckv, dg_ckv = _rms_bwd(_V(z, ML + MLA_RANK, MLA_RANK), w["mla_ckv_norm"], dckvn, "mla_ckv_rms_bwd", out_dtype=BF16)
    g["mla_cq_norm"], g["mla_ckv_norm"] = dg_cq[0:1], dg_ckv[0:1]
    segs = ((dgates, 0, 3 * D), (dz_na, NA, 1536), (dz_gla, GL, 1536), (dz_cq, ML, MLA_RANK), (dz_ckv, ML + MLA_RANK, MLA_RANK),
            (dz_lr, LR, 128), (dz_kr, KR, 128))
    dh, got = _mm([(dz, _V(w["w_in"], c0, wd)) for dz, c0, wd in segs], "nt", F32, "w_in_dx", tm=512, tn=512,
                  ride=[own("ffn2_w2")])
    recv.update(got)
    dw_in = [_mm([(sv["h"], dz)], "tn", F32, f"w_in_dw{i}", tm=D, tn=_tile(wd, 256)) for i, (dz, _, wd) in enumerate(segs)]
    dx1, dg_mix = _rms_bwd(sv["x1"], w["mix_norm"], dh, "mix_rms_bwd", dres=dx2)
    g["mix_norm"] = dg_mix[0:1]
    late = dict(w_in=_win_split4(dw_in, D), mla_w_uq=_split4(_uq_unlayout(g["mla_wuq"]), 1),
                mla_w_ukv=_split4(g["mla_w_ukv"], 1), gla_gf_up=_split4(g["gla_wg"][0:GLA_RANK, 0:256], 1),
                gla_gb_up=_split4(g["gla_wg"][GLA_RANK:2 * GLA_RANK, 256:512], 1))
    ride = [_Ride(n, _split4(g[n], SHARDED[n] - 1), lambda chip: (chip,), recv[n], at_layer)
            for n in ("w_out", "w_br_na", "w_br_gla", "w_br_mla")]
    ride_dh = None
    if flush:
        ride_dh, late = [_Ride(n, late[n], lambda chip: (chip,), recv[n], at_layer) for n in late], {}
    dx, g["ffn1_norm"], (g["ffn1_w1"], g["ffn1_w3"], g["ffn1_w2"]), got = _ffn_bwd(
        dx1, sv["x"], w["ffn1_norm"], w["ffn1_w1"], w["ffn1_w3"], w["ffn1_w2"], sv["f1"], "ffn1",
        (bufs["ffn1_w1"], bufs["ffn1_w3"], bufs["ffn1_w2"]), layer, ride_down=ride, ride_dh=ride_dh)
    recv.update(got)
    return dx, g, late, recv


def _head_fold(width, period, lo=0):
    f = np.zeros((width, 128), np.float32)
    f[np.arange(width), lo + np.arange(width) % period] = 1.0
    return f


def kernel(x, ffn1_norm, ffn1_w1, ffn1_w3, ffn1_w2, mix_norm, w_in, na_q_norm, na_k_norm, na_rpb, gla_gf_up, gla_gf_bias,
           gla_gb_up, gla_gb_bias, gla_out_norm, mla_cq_norm, mla_ckv_norm, mla_w_uq, mla_w_ukv, mla_q_norm, mla_k_norm,
           w_br_na, w_br_gla, w_br_mla, w_out, ffn2_norm, ffn2_w1, ffn2_w3, ffn2_w2, loss_target, m_ffn1_norm, m_ffn1_w1,
           m_ffn1_w3, m_ffn1_w2, m_mix_norm, m_w_in, m_na_q_norm, m_na_k_norm, m_na_rpb, m_gla_gf_up, m_gla_gf_bias,
           m_gla_gb_up, m_gla_gb_bias, m_gla_out_norm, m_mla_cq_norm, m_mla_ckv_norm, m_mla_w_uq, m_mla_w_ukv,
           m_mla_q_norm, m_mla_k_norm, m_w_br_na, m_w_br_gla, m_w_br_mla, m_w_out, m_ffn2_norm, m_ffn2_w1, m_ffn2_w3,
           m_ffn2_w2, v_ffn1_norm, v_ffn1_w1, v_ffn1_w3, v_ffn1_w2, v_mix_norm, v_w_in, v_na_q_norm, v_na_k_norm,
           v_na_rpb, v_gla_gf_up, v_gla_gf_bias, v_gla_gb_up, v_gla_gb_bias, v_gla_out_norm, v_mla_cq_norm,
           v_mla_ckv_norm, v_mla_w_uq, v_mla_w_ukv, v_mla_q_norm, v_mla_k_norm, v_w_br_na, v_w_br_gla, v_w_br_mla,
           v_w_out, v_ffn2_norm, v_ffn2_w1, v_ffn2_w3, v_ffn2_w2):
    given = dict(locals())
    wts = {n: given[n] for n in W_NAMES}
    mom = {n: given["m_" + n] for n in W_NAMES}
    var = {n: given["v_" + n] for n in W_NAMES}
    xs, target = x[0], loss_target[0]
    S, D = xs.shape
    L = ffn1_norm.shape[0]

    sh_names = tuple(SHARDED)
    LATE = ("ffn2_w1", "ffn2_w3", "ffn2_w2")
    HEAVY = ("ffn1_w1", "ffn1_w3", "ffn1_w2", "w_in")
    LIGHT = tuple(n for n in sh_names if n not in LATE + HEAVY)
    shard_shape = lambda n: tuple(wts[n].shape[1:])

    def gather_items(names, l):
        return [_Ride(n, wts[n][l].astype(BF16), lambda chip: (), lax.empty((4,) + shard_shape(n), BF16), lambda chip: (chip,),
                      halves=shard_shape(n)[0] % 32 == 0) for n in names]

    cols = lambda p: jnp.concatenate([p[j] for j in range(4)], axis=-1)

    def ffn1_weights(gl, l):
        return dict(ffn1_norm=ffn1_norm[l][None], ffn1_w1=gl["ffn1_w1"], ffn1_w3=gl["ffn1_w3"], ffn1_w2=gl["ffn1_w2"])

    def mixer_weights(gl, l):
        r1 = lambda a: a[l][None]
        wg = jnp.zeros((128, 2 * GLA_HEADS * GLA_DK), BF16)
        wg = wg.at[0:GLA_RANK, 0:256].set(cols(gl["gla_gf_up"])).at[GLA_RANK:2 * GLA_RANK, 256:512].set(cols(gl["gla_gb_up"]))
        return dict(
            mix_norm=r1(mix_norm), w_in=_win_layout([gl["w_in"][j] for j in range(4)], D),
            na_gq=jnp.tile(na_q_norm[l], NA_HEADS)[None], na_gk=jnp.tile(na_k_norm[l], NA_HEADS)[None], na_rpb=na_rpb[l],
            gla_wg=wg, gla_gbias=jnp.concatenate([gla_gf_bias[l], gla_gb_bias[l]])[None], gla_out_norm=r1(gla_out_norm),
            mla_cq_norm=r1(mla_cq_norm), mla_ckv_norm=r1(mla_ckv_norm), mla_wuq=_uq_layout(cols(gl["mla_w_uq"])),
            mla_w_ukv=cols(gl["mla_w_ukv"]), mla_gq=_slot_layout(mla_q_norm[l])[None], mla_gk=_slot_layout(mla_k_norm[l])[None],
            w_br_na=cols(gl["w_br_na"]), w_br_gla=cols(gl["w_br_gla"]), w_br_mla=cols(gl["w_br_mla"]),
            w_out=gl["w_out"].reshape(D, D), ffn2_norm=r1(ffn2_norm))

    half = MLA_ROPE // 2
    inv = ROPE_THETA ** (-jnp.arange(half, dtype=F32) / half)
    ang = jnp.arange(S, dtype=F32)[:, None] * inv[None, :]
    cos = jnp.tile(jnp.cos(ang), (1, 4))
    sg = jnp.concatenate([-jnp.sin(ang), -jnp.sin(ang), jnp.sin(ang), jnp.sin(ang)], axis=1)

    FFN1 = ("ffn1_w1", "ffn1_w3", "ffn1_w2")
    arrived = _exchange(gather_items(FFN1, 0), "weights_all_gather")
    xc, saved, layers = xs, [], []
    for l in range(L):
        w = ffn1_weights(arrived, l)
        if l == 0:
            rides = {"ffn1_up": gather_items(("w_in",) + LIGHT, 0), "w_in": gather_items(LATE, 0)}
            mixer = lambda got, l=l: mixer_weights(got, l)
        else:
            rides = {"ffn1_up": gather_items(("ffn2_w1", "ffn2_w3"), l), "w_in": gather_items(("ffn2_w2",), l)}
            w.update(mixer_weights(arrived, l))
            mixer = None
        if l + 1 < L:
            rides["na_attn"] = gather_items(("ffn1_w1", "ffn1_w3"), l + 1)
            rides["gla_fwd"] = gather_items(("ffn1_w2",), l + 1)
            rides["mla_attn"] = gather_items(("w_in",) + LIGHT, l + 1)
        xc, sv, arrived, w = _layer_fwd(xc, w, cos, sg, rides, mixer)
        saved.append(sv)
        layers.append({**w, **{n: arrived[n] for n in LATE}})
    dy, loss_part = _loss_head(xc, target, "loss_head")

    bufs = {n: lax.empty((4, L) + shard_shape(n), BF16) for n in FFN_W}
    recv = {n: lax.empty((4, L) + shard_shape(n), BF16) for n in sh_names}
    dx, g, prev = dy, [None] * L, {}
    for l in reversed(range(L)):
        dx, g[l], late, recv = _layer_bwd(dx, layers[l], saved[l], cos, sg, bufs, recv, l, prev, flush=l == 0)
        bufs = {n: g[l][n] for n in FFN_W}
        at_l = functools.partial(lambda chip, l: (chip, l), l=l)
        prev = {n: _Ride(n, bufs[n], at_l, recv[n], at_l) for n in ("ffn1_w1", "ffn1_w3", "ffn1_w2")}
        prev.update({n: _Ride(n, late[n], lambda chip: (chip,), recv[n], at_l) for n in late})

    stk = lambda n: jnp.stack([g[l][n] for l in range(L)])
    gs = {n: stk(n)[:, 0] for n in ("ffn1_norm", "mix_norm", "mla_cq_norm", "mla_ckv_norm", "ffn2_norm")}
    gs["na_q_norm"] = _fold(stk("na_gq"), "na_gq_fold", _head_fold(NA_W, NA_DH))[:, :NA_DH]
    gs["na_k_norm"] = _fold(stk("na_gk"), "na_gk_fold", _head_fold(NA_W, NA_DH))[:, :NA_DH]
    gs["na_rpb"] = stk("na_rpb")
    gbias = _fold(stk("gla_gbias"), "gla_gbias_fold")
    gs["gla_gf_bias"], gs["gla_gb_bias"] = gbias[:, :256], gbias[:, 256:]
    gs["gla_out_norm"] = _fold(stk("gla_out_norm"), "gla_out_norm_fold")
    gs["mla_q_norm"] = _slot_unlayout(_fold(stk("mla_gq"), "mla_gq_fold"))
    gs["mla_k_norm"] = _slot_unlayout(_fold(stk("mla_gk"), "mla_gk_fold"))

    as2d = lambda a: a.reshape(-1, a.shape[-1])
    gsh, upd = {}, {}
    DONE = ("ffn2_w1", "ffn2_w3", "ffn2_w2", "w_out", "w_br_na", "w_br_gla", "w_br_mla")

    def reduce_and_update(names, with_exchange):
        mine = [_sum4(recv[n].reshape(4, -1, recv[n].shape[-1]), "grads_chip_sum_" + n) for n in names]
        other, arrived = _sibling_exchange(mine, "grads_sibling_exchange", ride=with_exchange)
        for n, p, q in zip(names, mine, other):
            outs = [o.reshape(wts[n].shape) for o in _adamw(as2d(wts[n]), p, as2d(mom[n]), as2d(var[n]), "adamw_" + n, q=q)]
            gsh[n], upd[n] = outs[0], outs[1:]
        return arrived

    recv.update(reduce_and_update(DONE, list(prev.values())))
    reduce_and_update(tuple(n for n in sh_names if n not in DONE), [])

    small_shapes = [wts[n].shape[1:] for n in REPLICATED]
    n_small = sum(int(np.prod(s)) for s in small_shapes) * L
    flat = jnp.concatenate([gs[n].reshape(-1) for n in REPLICATED] + [loss_part.reshape(-1)])
    pad = -flat.shape[0] % 1024
    red = _allreduce_small(jnp.pad(flat, (0, pad)).reshape(-1, 128), "small_all_reduce").reshape(-1)
    loss = jnp.sum(red[n_small:n_small + 1024])
    off = 0
    for n, s in zip(REPLICATED, small_shapes):
        cnt = int(np.prod(s)) * L
        gsh[n] = red[off:off + cnt].reshape((L,) + tuple(s))
        off += cnt

    pk = lambda d: jnp.pad(jnp.concatenate([d[n].reshape(-1) for n in REPLICATED]), (0, -n_small % 1024)).reshape(-1, 128)
    small = _adamw(pk(wts), pk(gsh), pk(mom), pk(var), "adamw_replicated")
    off = 0
    for n, s in zip(REPLICATED, small_shapes):
        cnt = int(np.prod(s)) * L
        upd[n] = [o.reshape(-1)[off:off + cnt].reshape((L,) + tuple(s)) for o in small]
        off += cnt

    return (loss, dx[None], *[gsh[n] for n in W_NAMES], *[upd[n][0] for n in W_NAMES], *[upd[n][1] for n in W_NAMES],
            *[upd[n][2] for n in W_NAMES])
```

```python
import functools
import math

import numpy as np
import jax
import jax.numpy as jnp
from jax import lax
from jax.experimental import pallas as pl
from jax.experimental.pallas import tpu as pltpu

F32 = jnp.float32
BF16 = jnp.bfloat16
HI = lax.Precision.HIGHEST
MESH = pl.DeviceIdType.MESH

EPS = 1e-6
GRID_W = 64
NA_HEADS, NA_DH, NA_WIN_R, NA_WIN_C = 8, 64, 8, 16
NA_W = NA_HEADS * NA_DH
GLA_HEADS, GLA_DK, GLA_DV, GLA_RANK, GLA_TAU, GLA_CHUNK = 4, 64, 128, 16, 16.0, 64
MLA_HEADS, MLA_RANK, MLA_NOPE, MLA_ROPE, MLA_V = 4, 256, 128, 64, 128
MLA_QK = MLA_NOPE + MLA_ROPE
MLA_SLOT = 256
MLA_QSCALE = MLA_QK ** -0.5 * math.log2(math.e)
ROPE_THETA = 10000.0
ADAM_LR, ADAM_B1, ADAM_B2, ADAM_EPS, ADAM_WD, ADAM_STEP = 0.001, 0.9, 0.999, 1e-08, 0.01, 10

V7X_VMEM_BYTES = 64 * 2**20
VMEM_LIMIT = V7X_VMEM_BYTES - 12 * 2**20
NEG = -1e30

O_GQ, O_GFL, O_CQ, O_KR, O_GATES = 1536, 3072, 3104, 3616, 3680


_ANY = pl.BlockSpec(memory_space=pl.ANY)


def _cparams(*sem):
    return pltpu.CompilerParams(dimension_semantics=sem, vmem_limit_bytes=VMEM_LIMIT)


class _V:
    def __init__(self, arr, c0=0, w=None, lead=()):
        self.arr, self.c0, self.lead = arr, c0, tuple(lead)
        assert arr.ndim == 2 + len(self.lead), (arr.shape, lead)
        self.w = arr.shape[-1] if w is None else w

    @property
    def rows(self):
        return self.arr.shape[-2]

    def spec(self, br, bc, rfn, cfn):
        assert self.c0 % bc == 0 and self.w % bc == 0, (self.c0, self.w, bc)
        off, lead = self.c0 // bc, self.lead

        def index(*g):
            return tuple(g[0] if e == "b" else e for e in lead) + (rfn(*g), off + cfn(*g))

        return pl.BlockSpec((None,) * len(lead) + (br, bc), index)


def _v(x):
    return x if isinstance(x, _V) else _V(x)


_DN = {"nn": (((1,), (0,)), ((), ())), "nt": (((1,), (1,)), ((), ())), "tn": (((0,), (0,)), ((), ()))}


def _dot(a, b, mode="nn", prec=None):
    return lax.dot_general(a, b, _DN[mode], preferred_element_type=F32, precision=prec)


def _tile(n, cap):
    if n <= cap:
        return n
    for t in range(cap - cap % 128, 0, -128):
        if n % t == 0:
            return t
    return n


def _mm(pairs, mode, out_dtype, name, *, tm, tn, res=None, scale=None, batch=1, into=None, ride=None):
    pairs = [(_v(a), _v(b)) for a, b in pairs]
    a0, b0 = pairs[0]
    M = a0.w if mode == "tn" else a0.rows
    N = b0.rows if mode == "nt" else b0.w
    tm, tn = _tile(M, tm), _tile(N, tn)
    assert M % tm == 0 and N % tn == 0, (name, M, N, tm, tn)
    n = len(pairs)

    def body(*refs):
        o_ref = refs[-1]
        acc = None
        for i in range(n):
            d = _dot(refs[2 * i][...].astype(BF16), refs[2 * i + 1][...].astype(BF16), mode)
            acc = d if acc is None else acc + d
        if scale is not None:
            acc = acc * scale
        if res is not None:
            acc = acc + refs[2 * n][...]
        o_ref[...] = acc.astype(o_ref.dtype)

    zero = lambda b, i, j: 0
    row = lambda b, i, j: i
    col = lambda b, i, j: j
    in_specs, args = [], []
    for a, b in pairs:
        in_specs.append(a.spec(a.rows, tm, zero, row) if mode == "tn" else a.spec(tm, a.w, row, zero))
        in_specs.append(b.spec(tn, b.w, col, zero) if mode == "nt" else b.spec(b.rows, tn, zero, col))
        args += [a.arr, b.arr]
    if res is not None:
        in_specs.append(pl.BlockSpec((tm, tn), lambda b, i, j: (i, j)))
        args.append(res)
    aliases = {}
    if into is None:
        out = jax.ShapeDtypeStruct(((batch,) if batch > 1 else ()) + (M, N), out_dtype)
        out_view = _V(out, lead=("b",) if batch > 1 else ())
    else:
        buf, lead = into
        assert buf.shape[-2:] == (M, N) and buf.dtype == out_dtype, (name, buf.shape, M, N)
        out = jax.ShapeDtypeStruct(buf.shape, buf.dtype)
        out_view = _V(out, lead=lead)
        aliases = {len(args): 0}
        in_specs.append(_ANY)
        args.append(buf)
    (res,), got = _ride_call(
        body, ride, name=name, grid=(batch, M // tm, N // tn), in_specs=in_specs, out_specs=[out_view.spec(tm, tn, row, col)],
        out_shape=[out], aliases=aliases, args=args, semantics=("parallel", "parallel", "parallel"))
    return res if ride is None else (res, got)


def _rms_fwd(x, g, name, tm=512):
    x = _v(x)
    S, D = x.rows, x.w
    tm = min(tm, S)

    def body(x_ref, g_ref, o_ref):
        xv = x_ref[...]
        y = xv * lax.rsqrt(jnp.mean(xv * xv, axis=-1, keepdims=True) + EPS)
        o_ref[...] = (y * g_ref[...]).astype(o_ref.dtype)

    return pl.pallas_call(
        body, name=name, grid=(S // tm,),
        in_specs=[x.spec(tm, D, lambda i: i, lambda i: 0), pl.BlockSpec((1, D), lambda i: (0, 0))],
        out_specs=pl.BlockSpec((tm, D), lambda i: (i, 0)),
        out_shape=jax.ShapeDtypeStruct((S, D), BF16), compiler_params=_cparams("parallel"),
    )(x.arr, g)


def _rms_bwd(x, g, dh, name, dres=None, out_dtype=F32, tm=512):
    x = _v(x)
    S, D = x.rows, x.w
    tm = min(tm, S)

    def body(*refs):
        if dres is None:
            x_ref, g_ref, dh_ref, dx_ref, dg_ref = refs
        else:
            x_ref, g_ref, dh_ref, dr_ref, dx_ref, dg_ref = refs
        xv = x_ref[...]
        rstd = lax.rsqrt(jnp.mean(xv * xv, axis=-1, keepdims=True) + EPS)
        xhat = xv * rstd
        dhv = dh_ref[...].astype(F32)
        dxhat = dhv * g_ref[...]
        dx = rstd * (dxhat - xhat * jnp.mean(dxhat * xhat, axis=-1, keepdims=True))
        if dres is not None:
            dx = dx + dr_ref[...]
        dx_ref[...] = dx.astype(dx_ref.dtype)

        @pl.when(pl.program_id(0) == 0)
        def _():
            dg_ref[...] = jnp.zeros_like(dg_ref)

        dg_ref[0:1, :] += jnp.sum(dhv * xhat, axis=0, keepdims=True)

    in_specs = [x.spec(tm, D, lambda i: i, lambda i: 0), pl.BlockSpec((1, D), lambda i: (0, 0)),
                pl.BlockSpec((tm, D), lambda i: (i, 0))]
    args = [x.arr, g, dh]
    if dres is not None:
        in_specs.append(pl.BlockSpec((tm, D), lambda i: (i, 0)))
        args.append(dres)
    return pl.pallas_call(
        body, name=name, grid=(S // tm,), in_specs=in_specs,
        out_specs=[pl.BlockSpec((tm, D), lambda i: (i, 0)), pl.BlockSpec((8, D), lambda i: (0, 0))],
        out_shape=[jax.ShapeDtypeStruct((S, D), out_dtype), jax.ShapeDtypeStruct((8, D), F32)],
        compiler_params=_cparams("arbitrary"),
    )(*args)


FFN_SUB = 256


def _one_ahead(n, matmuls, rest):
    res = matmuls(0)
    for i in range(1, n):
        nxt = matmuls(i)
        rest(i - 1, res)
        res = nxt
    rest(n - 1, res)


def _ffn_up(h, w1, w3, name, tm=1024, ride=None):
    S, D = h.shape
    NC, _, F4 = w1.shape
    tm = min(tm, S)
    sub = math.gcd(FFN_SUB, tm)
    rows = lambda i: slice(i * sub, (i + 1) * sub)

    def body(h_ref, w1_ref, w3_ref, a_ref, b_ref, u_ref):
        def matmuls(i):
            hv = h_ref[rows(i), :]
            return _dot(hv, w1_ref[...]), _dot(hv, w3_ref[...])

        def rest(i, ab):
            a, b = ab
            a_ref[rows(i), :] = a.astype(BF16)
            b_ref[rows(i), :] = b.astype(BF16)
            u_ref[rows(i), :] = (a * jax.nn.sigmoid(a) * b).astype(BF16)

        _one_ahead(tm // sub, matmuls, rest)

    blk = pl.BlockSpec((None, tm, F4), lambda i, j: (j, i, 0))
    wblk = pl.BlockSpec((None, D, F4), lambda i, j: (j, 0, 0))
    return _ride_call(
        body, ride, name=name, grid=(S // tm, NC), in_specs=[pl.BlockSpec((tm, D), lambda i, j: (i, 0)), wblk, wblk],
        out_specs=[blk, blk, blk], out_shape=[jax.ShapeDtypeStruct((NC, S, F4), BF16)] * 3, args=(h, w1, w3),
        semantics=("parallel", "parallel"))


def _ffn_down_bwd(dxo, w2, a, b, name, tm=1024, ride=None):
    S, D = dxo.shape
    NC, F4, _ = w2.shape
    tm = min(tm, S)
    sub = math.gcd(FFN_SUB, tm)
    rows = lambda i: slice(i * sub, (i + 1) * sub)

    def body(dx_ref, w2_ref, a_ref, b_ref, da_ref, db_ref):
        def matmuls(i):
            return _dot(dx_ref[rows(i), :].astype(BF16), w2_ref[...], "nt")

        def rest(i, du):
            du = du * 0.5
            av = a_ref[rows(i), :].astype(F32)
            sig = jax.nn.sigmoid(av)
            da_ref[rows(i), :] = (du * b_ref[rows(i), :].astype(F32) * (sig * (1.0 + av * (1.0 - sig)))).astype(BF16)
            db_ref[rows(i), :] = (du * av * sig).astype(BF16)

        _one_ahead(tm // sub, matmuls, rest)

    blk = pl.BlockSpec((None, tm, F4), lambda i, j: (j, i, 0))
    return _ride_call(
        body, ride, name=name, grid=(S // tm, NC),
        in_specs=[pl.BlockSpec((tm, D), lambda i, j: (i, 0)), pl.BlockSpec((None, F4, D), lambda i, j: (j, 0, 0)), blk, blk],
        out_specs=[blk, blk], out_shape=[jax.ShapeDtypeStruct((NC, S, F4), BF16)] * 2, args=(dxo, w2, a, b),
        semantics=("parallel", "parallel"))


def _ffn_fwd(x, g, w1, w3, w2, tag, ride=None):
    h = _rms_fwd(x, g, f"{tag}_rms")
    (a, b, u), got = _ffn_up(h, w1, w3, f"{tag}_up", ride=ride)
    nc = w2.shape[0]
    y = _mm([(_V(u, lead=(j,)), _V(w2, lead=(j,))) for j in range(nc)], "nn", F32, f"{tag}_down", tm=1024, tn=1024, res=x, scale=0.5)
    return y, (h, a, b, u), got


def _ffn_bwd(dxo, x, g, w1, w3, w2, saved, tag, bufs, layer, ride_down=None, ride_dh=None):
    h, a, b, u = saved
    nc, D, F4 = w1.shape
    (da, db), got = _ffn_down_bwd(dxo, w2, a, b, f"{tag}_down_bwd", ride=ride_down)
    into = lambda k: (bufs[k], ("b", layer))
    dw2 = _mm([(_V(u, lead=("b",)), dxo)], "tn", BF16, f"{tag}_dw2", tm=F4, tn=512, scale=0.5, batch=nc, into=into(2))
    dw1 = _mm([(h, _V(da, lead=("b",)))], "tn", BF16, f"{tag}_dw1", tm=D, tn=F4, batch=nc, into=into(0))
    dw3 = _mm([(h, _V(db, lead=("b",)))], "tn", BF16, f"{tag}_dw3", tm=D, tn=F4, batch=nc, into=into(1))
    pairs = [(_V(da, lead=(j,)), _V(w1, lead=(j,))) for j in range(nc)] + [(_V(db, lead=(j,)), _V(w3, lead=(j,))) for j in range(nc)]
    dh, got_dh = _mm(pairs, "nt", F32, f"{tag}_dh", tm=512, tn=1024, ride=ride_dh or [])
    dx, dg = _rms_bwd(x, g, dh, f"{tag}_rms_bwd", dres=dxo)
    return dx, dg[0:1], (dw1, dw3, dw2), {**got, **got_dh}


def _iota(shape, dim):
    return lax.broadcasted_iota(jnp.int32, shape, dim)


def _head_block_ones(n, shift):
    return jnp.where((_iota((n, n), 0) >> shift) == (_iota((n, n), 1) >> shift), 1.0, 0.0).astype(BF16)


def _dot_split(x, ones01):
    hi = x.astype(BF16)
    lo = (x - hi.astype(F32)).astype(BF16)
    return _dot(hi, ones01) + _dot(lo, ones01)


def _lane_mask(width, lo, size):
    l = _iota((1, width), 1)
    return jnp.where((l >= lo) & (l < lo + size), 1.0, 0.0).astype(F32)


def _acc_rows(acc_ref, val, first):
    r = val.shape[0]
    part = jnp.sum(val.reshape(r // 8, 8, val.shape[1]), axis=0)

    @pl.when(first)
    def _():
        acc_ref[...] = part

    @pl.when(jnp.logical_not(first))
    def _():
        acc_ref[...] += part


_FLIPS = ((1, 0), (0, 1), (1, 1))


class _Ride:
    def __init__(self, name, src, src_at, dst, dst_at, halves=False):
        self.name, self.src, self.src_at, self.dst, self.dst_at, self.halves = name, src, src_at, dst, dst_at, halves
        assert not halves or (src.ndim == 2 and src.shape[0] % 32 == 0), (name, src.shape)


_RIDE_SEMS = lambda n: [pltpu.SemaphoreType.DMA((6, n)), pltpu.SemaphoreType.DMA((6, n)), pltpu.SemaphoreType.DMA((n,))]


def _ride_ops(ride, srcs, dsts, send_sems, recv_sems, local_sems):
    x, y, c = lax.axis_index("x"), lax.axis_index("y"), lax.axis_index("c")
    me = 2 * x + y
    at = lambda ref, idx: ref.at[idx] if idx else ref
    local, sends, arrivals, passes = [], [], [], []
    for t, it in enumerate(ride):
        local.append(pltpu.make_async_copy(at(srcs[t], it.src_at(me)), at(dsts[t], it.dst_at(me)), local_sems.at[t]))
    for r, (fx, fy) in enumerate(_FLIPS):
        px, py = (1 - x) if fx else x, (1 - y) if fy else y
        peer = 2 * px + py
        for t, it in enumerate(ride):
            if it.halves:
                h = it.src.shape[0] // 2
                mine = pl.ds(pl.multiple_of(c * h, 16), h)
                theirs = pl.ds(pl.multiple_of((1 - c) * h, 16), h)
                far = dict(send_sem=send_sems.at[r, t], recv_sem=recv_sems.at[r, t], device_id=(px, py, c), device_id_type=MESH)
                near = dict(send_sem=send_sems.at[3 + r, t], recv_sem=recv_sems.at[3 + r, t], device_id=(x, y, 1 - c),
                            device_id_type=MESH)
                sends.append(pltpu.make_async_remote_copy(src_ref=srcs[t].at[mine], dst_ref=dsts[t].at[me, mine], **far))
                arrivals.append(pltpu.make_async_remote_copy(src_ref=srcs[t].at[mine], dst_ref=dsts[t].at[peer, mine], **far))
                passes.append((pltpu.make_async_remote_copy(src_ref=dsts[t].at[peer, mine], dst_ref=dsts[t].at[peer, mine], **near),
                               pltpu.make_async_remote_copy(src_ref=dsts[t].at[peer, theirs], dst_ref=dsts[t].at[peer, theirs], **near)))
            else:
                far = dict(src_ref=at(srcs[t], it.src_at(peer)), send_sem=send_sems.at[r, t], recv_sem=recv_sems.at[r, t],
                           device_id=(px, py, c), device_id_type=MESH)
                sends.append(pltpu.make_async_remote_copy(dst_ref=at(dsts[t], it.dst_at(me)), **far))
                arrivals.append(pltpu.make_async_remote_copy(dst_ref=at(dsts[t], it.dst_at(peer)), **far))
                passes.append(None)

    def start():
        for cp in local + sends:
            cp.start()

    def finish():
        for cp, arrival, onward in zip(sends, arrivals, passes):
            cp.wait_send()
            arrival.wait_recv()
            if onward is not None:
                onward[0].start()
        for onward in passes:
            if onward is not None:
                onward[0].wait_send()
                onward[1].wait_recv()
        for cp in local:
            cp.wait()

    return start, finish


def _grid_edges(*ns):
    def edges():
        first = last = None
        for d, n in enumerate(ns):
            i = pl.program_id(d)
            f, l = i == 0, i == n - 1
            first = f if first is None else jnp.logical_and(first, f)
            last = l if last is None else jnp.logical_and(last, l)
        return first, last
    return edges


def _ride_call(body, ride, *, name, grid, in_specs, out_specs, out_shape, args, scratch_shapes=(), semantics=(), aliases=None):
    scratch_shapes, aliases = list(scratch_shapes), dict(aliases or {})
    if not ride:
        outs = pl.pallas_call(body, name=name, grid=grid, in_specs=in_specs, out_specs=out_specs, out_shape=out_shape,
                              scratch_shapes=scratch_shapes, input_output_aliases=aliases,
                              compiler_params=_cparams(*semantics))(*args)
        return outs, {}
    n_in, n_out, n_sc, n = len(in_specs), len(out_specs), len(scratch_shapes), len(ride)
    edges = _grid_edges(*grid)

    def wrapped(*refs):
        ins, srcs = refs[:n_in], refs[n_in:n_in + n]
        o0 = n_in + 2 * n
        outs, dsts = refs[o0:o0 + n_out], refs[o0 + n_out:o0 + n_out + n]
        scratch = refs[o0 + n_out + n:o0 + n_out + n + n_sc]
        start, finish = _ride_ops(ride, srcs, dsts, *refs[o0 + n_out + n + n_sc:])
        first, last = edges()
        pl.when(first)(start)
        body(*ins, *outs, *scratch)
        pl.when(last)(finish)

    aliases.update({n_in + n + t: n_out + t for t in range(n)})
    res = pl.pallas_call(
        wrapped, name=name, grid=grid, in_specs=list(in_specs) + [_ANY] * (2 * n), out_specs=list(out_specs) + [_ANY] * n,
        out_shape=list(out_shape) + [jax.ShapeDtypeStruct(it.dst.shape, it.dst.dtype) for it in ride],
        input_output_aliases=aliases, scratch_shapes=scratch_shapes + _RIDE_SEMS(n),
        compiler_params=_cparams(*(["arbitrary"] * len(grid))),
    )(*args, *[it.src for it in ride], *[it.dst for it in ride])
    return res[:n_out], {it.name: o for it, o in zip(ride, res[n_out:])}


def _exchange(ride, name):
    n = len(ride)

    def body(*refs):
        start, finish = _ride_ops(ride, refs[:n], refs[2 * n:3 * n], *refs[3 * n:])
        start()
        finish()

    res = pl.pallas_call(
        body, name=name, in_specs=[_ANY] * (2 * n), out_specs=[_ANY] * n,
        out_shape=[jax.ShapeDtypeStruct(it.dst.shape, it.dst.dtype) for it in ride],
        input_output_aliases={n + t: t for t in range(n)}, scratch_shapes=_RIDE_SEMS(n),
    )(*[it.src for it in ride], *[it.dst for it in ride])
    return {it.name: o for it, o in zip(ride, res)}


def _na_prep(z, c0, gq, gk, name, tm=512):
    S = z.shape[0]
    tm = min(tm, S)
    zv = _V(z, c0, 3 * NA_W)

    def body(z_ref, gq_ref, gk_ref, q_ref, k_ref, v_ref):
        bd = _head_block_ones(NA_W, 6)

        def norm(xv, gv):
            ms = _dot_split(xv * xv, bd) * (1.0 / NA_DH)
            return xv * lax.rsqrt(ms + EPS) * gv

        q_ref[...] = (norm(z_ref[:, 0:NA_W], gq_ref[...]) * (NA_DH ** -0.5)).astype(BF16)
        k_ref[...] = norm(z_ref[:, NA_W:2 * NA_W], gk_ref[...]).astype(BF16)
        v_ref[...] = z_ref[:, 2 * NA_W:3 * NA_W].astype(BF16)

    blk = pl.BlockSpec((tm, NA_W), lambda i: (i, 0))
    gspec = pl.BlockSpec((1, NA_W), lambda i: (0, 0))
    return pl.pallas_call(
        body, name=name, grid=(S // tm,),
        in_specs=[zv.spec(tm, 3 * NA_W, lambda i: i, lambda i: 0), gspec, gspec],
        out_specs=[blk, blk, blk], out_shape=[jax.ShapeDtypeStruct((S, NA_W), BF16)] * 3,
        compiler_params=_cparams("parallel"),
    )(z, gq, gk)


def _na_prep_bwd(z, c0, gq, gk, dqn, dkn, dv, name, tm=512):
    S = z.shape[0]
    tm = min(tm, S)
    zv = _V(z, c0, 3 * NA_W)

    def body(z_ref, gq_ref, gk_ref, dq_ref, dk_ref, dv_ref, dz_ref, dgq_ref, dgk_ref):
        bd = _head_block_ones(NA_W, 6)
        first = pl.program_id(0) == 0

        def norm_bwd(xv, gv, dy, dg_ref):
            ms = _dot_split(xv * xv, bd) * (1.0 / NA_DH)
            rstd = lax.rsqrt(ms + EPS)
            xhat = xv * rstd
            dxhat = dy * gv
            proj = _dot_split(dxhat * xhat, bd) * (1.0 / NA_DH)
            _acc_rows(dg_ref, dy * xhat, first)
            return rstd * (dxhat - xhat * proj)

        dz_ref[:, 0:NA_W] = norm_bwd(z_ref[:, 0:NA_W], gq_ref[...], dq_ref[...] * (NA_DH ** -0.5), dgq_ref).astype(BF16)
        dz_ref[:, NA_W:2 * NA_W] = norm_bwd(z_ref[:, NA_W:2 * NA_W], gk_ref[...], dk_ref[...], dgk_ref).astype(BF16)
        dz_ref[:, 2 * NA_W:3 * NA_W] = dv_ref[...].astype(BF16)

    blk = pl.BlockSpec((tm, NA_W), lambda i: (i, 0))
    gspec = pl.BlockSpec((1, NA_W), lambda i: (0, 0))
    acc = pl.BlockSpec((8, NA_W), lambda i: (0, 0))
    return pl.pallas_call(
        body, name=name, grid=(S // tm,),
        in_specs=[zv.spec(tm, 3 * NA_W, lambda i: i, lambda i: 0), gspec, gspec, blk, blk, blk],
        out_specs=[pl.BlockSpec((tm, 3 * NA_W), lambda i: (i, 0)), acc, acc],
        out_shape=[jax.ShapeDtypeStruct((S, 3 * NA_W), BF16), jax.ShapeDtypeStruct((8, NA_W), F32),
                   jax.ShapeDtypeStruct((8, NA_W), F32)],
        compiler_params=_cparams("arbitrary"),
    )(z, gq, gk, dqn, dkn, dv)


def _na_onehot():
    qc = np.arange(GRID_W)[:, None]
    kc = np.arange(GRID_W)[None, :]
    c0 = np.clip(qc - NA_WIN_C // 2, 0, GRID_W - NA_WIN_C)
    valid = (kc >= c0) & (kc < c0 + NA_WIN_C)
    dc = kc - qc + (NA_WIN_C - 1)
    e = np.zeros((32, GRID_W, GRID_W), np.float32)
    for d in range(2 * NA_WIN_C - 1):
        e[d] = valid & (dc == d)
    return e.reshape(32, GRID_W * GRID_W), valid.reshape(1, -1)


def _rpb_expand(rpb, name):
    e, valid = _na_onehot()
    negmask = np.where(valid, 0.0, NEG).astype(np.float32)
    nd = 2 * NA_WIN_R - 1
    r2 = jnp.pad(rpb.reshape(NA_HEADS * nd, 2 * NA_WIN_C - 1), ((0, 128 - NA_HEADS * nd), (0, 1)))

    def body(r_ref, e_ref, m_ref, o_ref):
        o_ref[...] = _dot(r_ref[...], e_ref[...], prec=HI) + m_ref[...]

    t = pl.pallas_call(body, name=name, out_shape=jax.ShapeDtypeStruct((128, GRID_W * GRID_W), F32))(
        r2, jnp.asarray(e), jnp.asarray(negmask))
    t = t[:NA_HEADS * nd].reshape(NA_HEADS, nd, GRID_W, GRID_W)
    return jnp.stack([jnp.concatenate([t[:, b + w] for w in range(NA_WIN_R)], axis=-1) for b in range(NA_WIN_R)], axis=1)


def _rpb_reduce(dbias, name):
    e, _ = _na_onehot()
    nd = 2 * NA_WIN_R - 1
    et = np.zeros((GRID_W * GRID_W, 128), np.float32)
    et[:, :32] = e.T
    sel = np.zeros((128, NA_HEADS * NA_WIN_R * NA_WIN_R), np.float32)
    for h in range(NA_HEADS):
        for b in range(NA_WIN_R):
            for w in range(NA_WIN_R):
                sel[h * nd + b + w, (h * NA_WIN_R + b) * NA_WIN_R + w] = 1.0
    x = dbias.reshape(NA_HEADS, NA_WIN_R, GRID_W, NA_WIN_R, GRID_W).transpose(0, 1, 3, 2, 4).reshape(-1, GRID_W * GRID_W)

    def body(x_ref, et_ref, sel_ref, o_ref):
        g = _dot(x_ref[...], et_ref[...], prec=HI)
        o_ref[...] = _dot(sel_ref[...], g, prec=HI)

    out = pl.pallas_call(body, name=name, out_shape=jax.ShapeDtypeStruct((128, 128), F32))(x, jnp.asarray(et), jnp.asarray(sel))
    return out[:NA_HEADS * nd, :2 * NA_WIN_C - 1].reshape(NA_HEADS, nd, 2 * NA_WIN_C - 1)


def _na_base(r, rows):
    return jnp.clip(r - NA_WIN_R // 2, 0, rows - NA_WIN_R) - r + (NA_WIN_R - 1)


def _na_probs(q_ref, k_ref, bias_ref, P, r0w):
    sl = [slice(128 * pp, 128 * pp + 128) for pp in range(P)]
    m = [_lane_mask(128, 64 * hh, 64) for hh in range(2)]
    kw = [k_ref[r0w, sl[pp]] for pp in range(P)]
    units = [(pp, hh) for pp in range(P) for hh in range(2)]
    qm = {u: (q_ref[:, sl[u[0]]].astype(F32) * m[u[1]]).astype(BF16) for u in units}
    s = {u: _dot(qm[u], kw[u[0]], "nt") + bias_ref[2 * u[0] + u[1], 0] for u in units}
    p = {}
    for u in units:
        e = jnp.exp(s[u] - jnp.max(s[u], axis=-1, keepdims=True))
        p[u] = e / jnp.sum(e, axis=-1, keepdims=True)
    return sl, m, kw, units, qm, p


NA_FWD_PAIRS = 4
NA_BWD_PAIRS = 2


def _na_attn(qn, kn, vb, bias, name, ride=None):
    S = qn.shape[0]
    rows = S // GRID_W
    nk = NA_WIN_R * GRID_W
    P = NA_FWD_PAIRS
    W = 128 * P

    def body(q_ref, k_ref, v_ref, b_ref, o_ref):
        r = pl.program_id(1)
        r0w = pl.ds(pl.multiple_of(jnp.clip(r - NA_WIN_R // 2, 0, rows - NA_WIN_R) * GRID_W, GRID_W), nk)
        sl, m, _, units, _, p = _na_probs(q_ref, k_ref, b_ref, P, r0w)
        o = {u: _dot(p[u].astype(BF16), v_ref[r0w, sl[u[0]]]) for u in units}
        for pp in range(P):
            o_ref[:, sl[pp]] = (o[pp, 0] * m[0] + o[pp, 1] * m[1]).astype(BF16)

    full = pl.BlockSpec((S, W), lambda g, r: (0, g))
    (o,), got = _ride_call(
        body, ride, name=name, grid=(NA_HEADS // (2 * P), rows),
        in_specs=[pl.BlockSpec((GRID_W, W), lambda g, r: (r, g)), full, full,
                  pl.BlockSpec((2 * P, 1, GRID_W, nk), lambda g, r: (g, _na_base(r, rows), 0, 0))],
        out_specs=[pl.BlockSpec((GRID_W, W), lambda g, r: (r, g))],
        out_shape=[jax.ShapeDtypeStruct((S, NA_W), BF16)], args=(qn, kn, vb, bias), semantics=("parallel", "arbitrary"))
    return o, got


def _na_attn_bwd(qn, kn, vb, bias, do, name, ride=None):
    S = qn.shape[0]
    rows = S // GRID_W
    nk = NA_WIN_R * GRID_W
    P = NA_BWD_PAIRS
    W = 128 * P

    def body(q_ref, k_ref, v_ref, b_ref, do_ref, dq_ref, dk_ref, dv_ref, db_ref):
        r = pl.program_id(1)

        @pl.when(r == 0)
        def _():
            dk_ref[...] = jnp.zeros_like(dk_ref)
            dv_ref[...] = jnp.zeros_like(dv_ref)

        r0w = pl.ds(pl.multiple_of(jnp.clip(r - NA_WIN_R // 2, 0, rows - NA_WIN_R) * GRID_W, GRID_W), nk)
        fresh = jnp.logical_or(r <= NA_WIN_R // 2, r > rows - NA_WIN_R // 2)
        sl, m, kw, units, qm, p = _na_probs(q_ref, k_ref, b_ref, P, r0w)
        dom = {u: (do_ref[:, sl[u[0]]].astype(F32) * m[u[1]]).astype(BF16) for u in units}
        dp = {u: _dot(dom[u], v_ref[r0w, sl[u[0]]], "nt") for u in units}
        dvw = {u: _dot(p[u].astype(BF16), dom[u], "tn") for u in units}
        ds = {u: p[u] * (dp[u] - jnp.sum(p[u] * dp[u], axis=-1, keepdims=True)) for u in units}

        @pl.when(fresh)
        def _():
            for u in units:
                db_ref[2 * u[0] + u[1], 0] = ds[u]

        @pl.when(jnp.logical_not(fresh))
        def _():
            for u in units:
                db_ref[2 * u[0] + u[1], 0] += ds[u]

        dsb = {u: ds[u].astype(BF16) for u in units}
        dq = {u: _dot(dsb[u], kw[u[0]]) for u in units}
        dkw = {u: _dot(dsb[u], qm[u], "tn") for u in units}
        for pp in range(P):
            dq_ref[:, sl[pp]] = dq[pp, 0] * m[0] + dq[pp, 1] * m[1]
            dk_ref[r0w, sl[pp]] += dkw[pp, 0] + dkw[pp, 1]
            dv_ref[r0w, sl[pp]] += dvw[pp, 0] + dvw[pp, 1]

    qblk = pl.BlockSpec((GRID_W, W), lambda g, r: (r, g))
    full = pl.BlockSpec((S, W), lambda g, r: (0, g))
    bblk = pl.BlockSpec((2 * P, 1, GRID_W, nk), lambda g, r: (g, _na_base(r, rows), 0, 0))
    return _ride_call(
        body, ride, name=name, grid=(NA_HEADS // (2 * P), rows),
        in_specs=[qblk, full, full, bblk, qblk], out_specs=[qblk, full, full, bblk],
        out_shape=[jax.ShapeDtypeStruct((S, NA_W), F32)] * 3 + [jax.ShapeDtypeStruct((NA_HEADS, NA_WIN_R, GRID_W, nk), F32)],
        args=(qn, kn, vb, bias, do), semantics=("parallel", "arbitrary"))


def _logsig(x):
    return jnp.minimum(x, 0.0) - jnp.log(1.0 + jnp.exp(-jnp.abs(x)))


def _gla_gates(z, c0, wg, bias, name, tm=512):
    S = z.shape[0]
    tm = min(tm, S)
    zv = _V(z, c0, 128)
    W = 2 * GLA_HEADS * GLA_DK

    def body(z_ref, w_ref, b_ref, o_ref):
        pre = _dot(z_ref[...].astype(BF16), w_ref[...]) + b_ref[...]
        o_ref[...] = _logsig(pre) * (1.0 / GLA_TAU)

    return pl.pallas_call(
        body, name=name, grid=(S // tm,),
        in_specs=[zv.spec(tm, 128, lambda i: i, lambda i: 0), pl.BlockSpec((128, W), lambda i: (0, 0)),
                  pl.BlockSpec((1, W), lambda i: (0, 0))],
        out_specs=pl.BlockSpec((tm, W), lambda i: (i, 0)), out_shape=jax.ShapeDtypeStruct((S, W), F32),
        compiler_params=_cparams("parallel"),
    )(z, wg, bias)


def _gla_gates_bwd(z, c0, wg, bias, dg_f, dg_b, name, tm=512):
    S = z.shape[0]
    tm = min(tm, S)
    zv = _V(z, c0, 128)
    W = 2 * GLA_HEADS * GLA_DK

    def body(z_ref, w_ref, b_ref, dgf_ref, dgb_ref, dp_ref, db_ref):
        pre = _dot(z_ref[...].astype(BF16), w_ref[...]) + b_ref[...]
        dg = jnp.concatenate([dgf_ref[...], dgb_ref[...]], axis=-1)
        dpre = dg * (1.0 / GLA_TAU) * jax.nn.sigmoid(-pre)
        dp_ref[...] = dpre.astype(BF16)
        _acc_rows(db_ref, dpre, pl.program_id(0) == 0)

    half = pl.BlockSpec((tm, W // 2), lambda i: (i, 0))
    return pl.pallas_call(
        body, name=name, grid=(S // tm,),
        in_specs=[zv.spec(tm, 128, lambda i: i, lambda i: 0), pl.BlockSpec((128, W), lambda i: (0, 0)),
                  pl.BlockSpec((1, W), lambda i: (0, 0)), half, half],
        out_specs=[pl.BlockSpec((tm, W), lambda i: (i, 0)), pl.BlockSpec((8, W), lambda i: (0, 0))],
        out_shape=[jax.ShapeDtypeStruct((S, W), BF16), jax.ShapeDtypeStruct((8, W), F32)],
        compiler_params=_cparams("arbitrary"),
    )(z, wg, bias, dg_f, dg_b)


def _gla_chunk_terms(zqk, g, p, rev):
    C = GLA_CHUNK
    i, j = _iota((C, C), 0), _iota((C, C), 1)
    cum = jnp.where((j >= i) if rev else (j <= i), 1.0, 0.0).astype(F32)
    q2 = zqk[:, 128 * p:128 * p + 128] * (GLA_DK ** -0.5)
    k2 = zqk[:, 256 + 128 * p:256 + 128 * p + 128]
    b2 = _dot(cum, g[:, 128 * p:128 * p + 128], prec=HI)
    bl2 = b2[0:1] if rev else b2[C - 1:C]
    eb = jnp.exp(b2)
    qe2 = q2 * eb
    ke2 = k2 * jnp.exp(-b2)
    kend2 = k2 * jnp.exp(bl2 - b2)
    dec2 = jnp.exp(bl2)
    tri = (j > i) if rev else (j <= i)
    return b2, bl2, eb, qe2, ke2, kend2, dec2, tri


def _row_to_col(row):
    eye = _iota((128, 128), 0) == _iota((128, 128), 1)
    return jnp.sum(jnp.where(eye, row, 0.0), axis=1, keepdims=True)


def _col_to_row(col):
    eye = _iota((128, 128), 0) == _iota((128, 128), 1)
    return jnp.sum(jnp.where(eye, col, 0.0), axis=0, keepdims=True)


GLA_GROUP = 4


def _gla_fwd(z, c_qk, c_v, gfb, name, ride=None):
    S = z.shape[0]
    C = GLA_CHUNK
    n = S // C
    G = math.gcd(GLA_GROUP, n)
    nb, GC = n // G, G * C
    WQK = 2 * GLA_HEADS * GLA_DK
    WV = GLA_HEADS * GLA_DV
    zqk, zvv = _V(z, c_qk, WQK), _V(z, c_v, WV)

    def body(qkf_ref, vf_ref, gf_ref, qkb_ref, vb_ref, gb_ref, of_ref, ob_ref, sf_ref, sb_ref, stf, stb):
        @pl.when(pl.program_id(0) == 0)
        def _():
            stf[...] = jnp.zeros_like(stf)
            stb[...] = jnp.zeros_like(stb)

        dirs = ((False, qkf_ref, vf_ref, gf_ref, of_ref, sf_ref, stf), (True, qkb_ref, vb_ref, gb_ref, ob_ref, sb_ref, stb))
        rows = lambda gi: slice(gi * C, (gi + 1) * C)
        pairs = [(d, gi, p) for d in range(2) for gi in range(G) for p in range(GLA_HEADS // 2)]
        heads = [(d, gi, p, hh) for d, gi, p in pairs for hh in range(2)]
        mask = [_lane_mask(128, 64 * hh, 64) for hh in range(2)]
        terms = {(d, gi, p): _gla_chunk_terms(dirs[d][1][rows(gi), :], dirs[d][3][rows(gi), :], p, dirs[d][0])
                 for d, gi, p in pairs}
        dec_col = {k: _row_to_col(t[6]) for k, t in terms.items()}
        vh = {(d, gi, h): dirs[d][2][rows(gi), 128 * h:128 * h + 128].astype(BF16)
              for d in range(2) for gi in range(G) for h in range(GLA_HEADS)}
        qm = {(d, gi, p, hh): (terms[d, gi, p][3] * mask[hh]).astype(BF16) for d, gi, p, hh in heads}
        a_raw = {(d, gi, p, hh): _dot(qm[d, gi, p, hh], terms[d, gi, p][4].astype(BF16), "nt") for d, gi, p, hh in heads}
        upd = {(d, gi, p, hh): _dot((terms[d, gi, p][5] * mask[hh]).astype(BF16), vh[d, gi, 2 * p + hh], "tn")
               for d, gi, p, hh in heads}
        intra = {(d, gi, p, hh): _dot(jnp.where(terms[d, gi, p][7], a_raw[d, gi, p, hh], 0.0).astype(BF16), vh[d, gi, 2 * p + hh])
                 for d, gi, p, hh in heads}
        state = {(d, h): dirs[d][6][h] for d in range(2) for h in range(GLA_HEADS)}
        for k in range(G):
            for d in range(2):
                gi = G - 1 - k if dirs[d][0] else k
                for p in range(GLA_HEADS // 2):
                    for hh in range(2):
                        h = 2 * p + hh
                        sp = state[d, h]
                        dirs[d][4][rows(gi), 128 * h:128 * h + 128] = intra[d, gi, p, hh] + _dot(qm[d, gi, p, hh], sp.astype(BF16))
                        dirs[d][5][gi, h] = sp
                        state[d, h] = dec_col[d, gi, p] * sp + upd[d, gi, p, hh]
        for d in range(2):
            for h in range(GLA_HEADS):
                dirs[d][6][h] = state[d, h]

    fw = lambda i: i
    bw = lambda i: nb - 1 - i
    zero = lambda i: 0
    in_specs = []
    for ix, col in ((fw, 0), (bw, 1)):
        in_specs += [zqk.spec(GC, WQK, ix, zero), zvv.spec(GC, WV, ix, zero),
                     pl.BlockSpec((GC, WQK // 2), functools.partial(lambda i, ix, col: (ix(i), col), ix=ix, col=col))]
    return _ride_call(
        body, ride, name=name, grid=(nb,), in_specs=in_specs,
        out_specs=[pl.BlockSpec((GC, WV), lambda i: (i, 0)), pl.BlockSpec((GC, WV), lambda i: (nb - 1 - i, 0)),
                   pl.BlockSpec((G, GLA_HEADS, 128, 128), lambda i: (i, 0, 0, 0)),
                   pl.BlockSpec((G, GLA_HEADS, 128, 128), lambda i: (nb - 1 - i, 0, 0, 0))],
        out_shape=[jax.ShapeDtypeStruct((S, WV), F32)] * 2 + [jax.ShapeDtypeStruct((n, GLA_HEADS, 128, 128), F32)] * 2,
        scratch_shapes=[pltpu.VMEM((GLA_HEADS, 128, 128), F32)] * 2, args=(z, z, gfb, z, z, gfb), semantics=("arbitrary",))


def _gla_bwd(z, c_qk, c_v, gfb, do, s_f, s_b, name, ride=None):
    S = z.shape[0]
    C = GLA_CHUNK
    n = S // C
    G = math.gcd(GLA_GROUP, n)
    nb, GC = n // G, G * C
    WQK = 2 * GLA_HEADS * GLA_DK
    WV = GLA_HEADS * GLA_DV
    zqk, zvv = _V(z, c_qk, WQK), _V(z, c_v, WV)

    def body(qkf_ref, vf_ref, gf_ref, dof_ref, sf_ref, qkb_ref, vb_ref, gb_ref, dob_ref, sb_ref,
             dqkf_ref, dvf_ref, dgf_ref, dqkb_ref, dvb_ref, dgb_ref, dstf, dstb):
        @pl.when(pl.program_id(0) == 0)
        def _():
            dstf[...] = jnp.zeros_like(dstf)
            dstb[...] = jnp.zeros_like(dstb)

        dirs = ((False, qkf_ref, vf_ref, gf_ref, dof_ref, sf_ref, dqkf_ref, dvf_ref, dgf_ref, dstf),
                (True, qkb_ref, vb_ref, gb_ref, dob_ref, sb_ref, dqkb_ref, dvb_ref, dgb_ref, dstb))
        rows = lambda gi: slice(gi * C, (gi + 1) * C)
        pairs = [(d, gi, p) for d in range(2) for gi in range(G) for p in range(GLA_HEADS // 2)]
        heads = [(d, gi, p, hh) for d, gi, p in pairs for hh in range(2)]
        mask = [_lane_mask(128, 64 * hh, 64) for hh in range(2)]
        T = {(d, gi, p): _gla_chunk_terms(dirs[d][1][rows(gi), :], dirs[d][3][rows(gi), :], p, dirs[d][0]) for d, gi, p in pairs}
        dec_col = {k: _row_to_col(t[6]) for k, t in T.items()}
        hd = lambda d, gi, p, hh: (d, gi, 2 * p + hh)
        vh = {(d, gi, h): dirs[d][2][rows(gi), 128 * h:128 * h + 128].astype(BF16)
              for d in range(2) for gi in range(G) for h in range(GLA_HEADS)}
        doh = {(d, gi, h): dirs[d][4][rows(gi), 128 * h:128 * h + 128].astype(BF16)
               for d in range(2) for gi in range(G) for h in range(GLA_HEADS)}
        sp = {(d, gi, h): dirs[d][5][gi, h] for d in range(2) for gi in range(G) for h in range(GLA_HEADS)}
        qm = {u: (T[u[:3]][3] * mask[u[3]]).astype(BF16) for u in heads}
        kem = {u: (T[u[:3]][4] * mask[u[3]]).astype(BF16) for u in heads}
        kendm = {u: (T[u[:3]][5] * mask[u[3]]).astype(BF16) for u in heads}
        a_raw = {u: _dot(qm[u], T[u[:3]][4].astype(BF16), "nt") for u in heads}
        da_raw = {u: _dot(doh[hd(*u)], vh[hd(*u)], "nt") for u in heads}
        w_upd = {u: _dot(qm[u], doh[hd(*u)], "tn") for u in heads}
        dqe_s = {u: _dot(doh[hd(*u)], sp[hd(*u)].astype(BF16), "nt") for u in heads}
        a = {u: jnp.where(T[u[:3]][7], a_raw[u], 0.0).astype(BF16) for u in heads}
        da = {u: jnp.where(T[u[:3]][7], da_raw[u], 0.0).astype(BF16) for u in heads}
        dqe = {u: _dot(da[u], kem[u]) + dqe_s[u] for u in heads}
        dke = {u: _dot(da[u], qm[u], "tn") for u in heads}
        dv_a = {u: _dot(a[u], doh[hd(*u)], "tn") for u in heads}
        ds = {}
        for d in range(2):
            cur = [dirs[d][9][h] for h in range(GLA_HEADS)]
            for gi in (range(G) if dirs[d][0] else reversed(range(G))):
                for p in range(GLA_HEADS // 2):
                    for hh in range(2):
                        h = 2 * p + hh
                        ds[d, gi, p, hh] = cur[h]
                        cur[h] = dec_col[d, gi, p] * cur[h] + w_upd[d, gi, p, hh]
            for h in range(GLA_HEADS):
                dirs[d][9][h] = cur[h]
        dsb = {u: ds[u].astype(BF16) for u in heads}
        dv_b = {u: _dot(kendm[u], dsb[u]) for u in heads}
        dkend = {u: _dot(vh[hd(*u)], dsb[u], "nt") * mask[u[3]] for u in heads}
        ddec = {u: _col_to_row(jnp.sum(ds[u] * sp[hd(*u)], axis=1, keepdims=True)) for u in heads}
        for u in heads:
            d, gi, h = hd(*u)
            dirs[d][7][rows(gi), 128 * h:128 * h + 128] = dv_a[u] + dv_b[u]
        i, j = _iota((C, C), 0), _iota((C, C), 1)
        for d, gi, p in pairs:
            rev = dirs[d][0]
            b2, bl2, eb, qe2, ke2, kend2, dec2, _ = T[d, gi, p]
            u0, u1 = (d, gi, p, 0), (d, gi, p, 1)
            dqe2, dke2, dkend2, ddec2 = dqe[u0] + dqe[u1], dke[u0] + dke[u1], dkend[u0] + dkend[u1], ddec[u0] + ddec[u1]
            dirs[d][6][rows(gi), 128 * p:128 * p + 128] = dqe2 * eb * (GLA_DK ** -0.5)
            dirs[d][6][rows(gi), 256 + 128 * p:256 + 128 * p + 128] = dke2 * jnp.exp(-b2) + dkend2 * jnp.exp(bl2 - b2)
            dkk = dkend2 * kend2
            dbl2 = jnp.sum(dkk, axis=0, keepdims=True) + ddec2 * dec2
            edge = _iota((C, 128), 0) == (0 if rev else C - 1)
            db2 = dqe2 * qe2 - dke2 * ke2 - dkk + jnp.where(edge, dbl2, 0.0)
            cum_t = jnp.where((j <= i) if rev else (j >= i), 1.0, 0.0).astype(F32)
            dirs[d][8][rows(gi), 128 * p:128 * p + 128] = _dot(cum_t, db2, prec=HI)

    fw = lambda i: nb - 1 - i
    bw = lambda i: i
    zero = lambda i: 0
    in_specs, out_specs = [], []
    for ix, col in ((fw, 0), (bw, 1)):
        blk = functools.partial(lambda i, ix: (ix(i), 0), ix=ix)
        in_specs += [zqk.spec(GC, WQK, ix, zero), zvv.spec(GC, WV, ix, zero),
                     pl.BlockSpec((GC, WQK // 2), functools.partial(lambda i, ix, col: (ix(i), col), ix=ix, col=col)),
                     pl.BlockSpec((GC, WV), blk),
                     pl.BlockSpec((G, GLA_HEADS, 128, 128), functools.partial(lambda i, ix: (ix(i), 0, 0, 0), ix=ix))]
        out_specs += [pl.BlockSpec((GC, WQK), blk), pl.BlockSpec((GC, WV), blk), pl.BlockSpec((GC, WQK // 2), blk)]
    shapes = [jax.ShapeDtypeStruct((S, WQK), F32), jax.ShapeDtypeStruct((S, WV), F32), jax.ShapeDtypeStruct((S, WQK // 2), F32)]
    return _ride_call(
        body, ride, name=name, grid=(nb,), in_specs=in_specs, out_specs=out_specs, out_shape=shapes * 2,
        scratch_shapes=[pltpu.VMEM((GLA_HEADS, 128, 128), F32)] * 2, args=(z, z, gfb, do, s_f, z, z, gfb, do, s_b),
        semantics=("arbitrary",))


def _gla_post(o_f, o_b, z, c_r, gn, name, tm=512):
    S, WV = o_f.shape
    tm = min(tm, S)
    zr = _V(z, c_r, WV)

    def body(of_ref, ob_ref, r_ref, g_ref, y_ref):
        gr = r_ref[...]
        sil = gr * jax.nn.sigmoid(gr)
        for h in range(GLA_HEADS):
            sl = slice(GLA_DV * h, GLA_DV * (h + 1))
            o = of_ref[:, sl] + ob_ref[:, sl]
            on = o * lax.rsqrt(jnp.mean(o * o, axis=-1, keepdims=True) + EPS) * g_ref[...]
            y_ref[:, sl] = (on * sil[:, sl]).astype(BF16)

    blk = pl.BlockSpec((tm, WV), lambda i: (i, 0))
    return pl.pallas_call(
        body, name=name, grid=(S // tm,),
        in_specs=[blk, blk, zr.spec(tm, WV, lambda i: i, lambda i: 0), pl.BlockSpec((1, GLA_DV), lambda i: (0, 0))],
        out_specs=blk, out_shape=jax.ShapeDtypeStruct((S, WV), BF16), compiler_params=_cparams("parallel"),
    )(o_f, o_b, z, gn)


def _gla_post_bwd(o_f, o_b, z, c_r, gn, dy, name, tm=512):
    S, WV = o_f.shape
    tm = min(tm, S)
    zr = _V(z, c_r, WV)

    def body(of_ref, ob_ref, r_ref, g_ref, dy_ref, do_ref, dr_ref, dg_ref):
        gr = r_ref[...]
        sig = jax.nn.sigmoid(gr)
        sil = gr * sig
        dyv = dy_ref[...].astype(F32)
        dgn = jnp.zeros((tm, GLA_DV), F32)
        for h in range(GLA_HEADS):
            sl = slice(GLA_DV * h, GLA_DV * (h + 1))
            o = of_ref[:, sl] + ob_ref[:, sl]
            rstd = lax.rsqrt(jnp.mean(o * o, axis=-1, keepdims=True) + EPS)
            xhat = o * rstd
            don = dyv[:, sl] * sil[:, sl]
            dr_ref[:, sl] = (dyv[:, sl] * xhat * g_ref[...] * (sig[:, sl] * (1.0 + gr[:, sl] * (1.0 - sig[:, sl])))).astype(BF16)
            dxhat = don * g_ref[...]
            do_ref[:, sl] = rstd * (dxhat - xhat * jnp.mean(dxhat * xhat, axis=-1, keepdims=True))
            dgn = dgn + don * xhat
        _acc_rows(dg_ref, dgn, pl.program_id(0) == 0)

    blk = pl.BlockSpec((tm, WV), lambda i: (i, 0))
    return pl.pallas_call(
        body, name=name, grid=(S // tm,),
        in_specs=[blk, blk, zr.spec(tm, WV, lambda i: i, lambda i: 0), pl.BlockSpec((1, GLA_DV), lambda i: (0, 0)), blk],
        out_specs=[blk, blk, pl.BlockSpec((8, GLA_DV), lambda i: (0, 0))],
        out_shape=[jax.ShapeDtypeStruct((S, WV), F32), jax.ShapeDtypeStruct((S, WV), BF16), jax.ShapeDtypeStruct((8, GLA_DV), F32)],
        compiler_params=_cparams("arbitrary"),
    )(o_f, o_b, z, gn, dy)


def _gla_assemble(dqk_f, dqk_b, dv_f, dv_b, dgr, name, tm=512):
    S = dqk_f.shape[0]
    tm = min(tm, S)

    def body(a_ref, b_ref, c_ref, d_ref, r_ref, o_ref):
        o_ref[:, 0:512] = (a_ref[...] + b_ref[...]).astype(BF16)
        o_ref[:, 512:1024] = (c_ref[...] + d_ref[...]).astype(BF16)
        o_ref[:, 1024:1536] = r_ref[...]

    blk = pl.BlockSpec((tm, 512), lambda i: (i, 0))
    return pl.pallas_call(
        body, name=name, grid=(S // tm,), in_specs=[blk] * 5, out_specs=pl.BlockSpec((tm, 1536), lambda i: (i, 0)),
        out_shape=jax.ShapeDtypeStruct((S, 1536), BF16), compiler_params=_cparams("parallel"),
    )(dqk_f, dqk_b, dv_f, dv_b, dgr)


def _rope(r, cos, sg):
    return r * cos + pltpu.roll(r, 64, 1) * sg


def _unrope(dy, cos, sg):
    return dy * cos + pltpu.roll(dy * sg, 64, 1)


def _mla_prep(z, c_q, c_kr, wuq, wukv, g_cq, g_ckv, g_q, g_k, cos, sg, name, tm=256):
    S = z.shape[0]
    tm = min(tm, S)
    zc, zk = _V(z, c_q, 2 * MLA_RANK), _V(z, c_kr, 128)
    inv = 1.0 / MLA_QK

    def body(zc_ref, zk_ref, wuq_ref, wukv_ref, gcq_ref, gckv_ref, gq_ref, gk_ref, cos_ref, sg_ref,
             q_ref, k_ref, v_ref, cqn_ref, ckvn_ref):
        def norm(xv, gv):
            return (xv * lax.rsqrt(jnp.mean(xv * xv, axis=-1, keepdims=True) + EPS) * gv).astype(BF16)

        cqn = norm(zc_ref[:, 0:MLA_RANK], gcq_ref[...])
        ckvn = norm(zc_ref[:, MLA_RANK:2 * MLA_RANK], gckv_ref[...])
        cqn_ref[...] = cqn
        ckvn_ref[...] = ckvn
        qf = _dot(cqn, wuq_ref[...])
        kv = _dot(ckvn, wukv_ref[...])
        kr = zk_ref[...]
        krss = jnp.sum(kr * kr, axis=-1, keepdims=True)
        cosv, sgv = cos_ref[...], sg_ref[...]
        gq, gk = gq_ref[...], gk_ref[...]
        for h in range(MLA_HEADS):
            qh = qf[:, MLA_SLOT * h:MLA_SLOT * (h + 1)]
            qhn = qh * lax.rsqrt(jnp.sum(qh * qh, axis=-1, keepdims=True) * inv + EPS) * gq
            q_ref[:, MLA_SLOT * h:MLA_SLOT * h + 128] = (qhn[:, 0:128] * MLA_QSCALE).astype(BF16)
            q_ref[:, MLA_SLOT * h + 128:MLA_SLOT * (h + 1)] = (_rope(qhn[:, 128:256], cosv, sgv) * MLA_QSCALE).astype(BF16)
            kn = kv[:, 256 * h:256 * h + 128]
            rstd = lax.rsqrt((jnp.sum(kn * kn, axis=-1, keepdims=True) + krss) * inv + EPS)
            k_ref[:, MLA_SLOT * h:MLA_SLOT * h + 128] = (kn * rstd * gk[:, 0:128]).astype(BF16)
            k_ref[:, MLA_SLOT * h + 128:MLA_SLOT * (h + 1)] = _rope(kr * rstd * gk[:, 128:256], cosv, sgv).astype(BF16)
            v_ref[:, 128 * h:128 * (h + 1)] = kv[:, 256 * h + 128:256 * (h + 1)].astype(BF16)

    row = lambda w: pl.BlockSpec((tm, w), lambda i: (i, 0))
    const = lambda r, w: pl.BlockSpec((r, w), lambda i: (0, 0))
    W = MLA_HEADS * MLA_SLOT
    return pl.pallas_call(
        body, name=name, grid=(S // tm,),
        in_specs=[zc.spec(tm, 2 * MLA_RANK, lambda i: i, lambda i: 0), zk.spec(tm, 128, lambda i: i, lambda i: 0),
                  const(MLA_RANK, W), const(MLA_RANK, W), const(1, MLA_RANK), const(1, MLA_RANK), const(1, MLA_SLOT),
                  const(1, MLA_SLOT), row(128), row(128)],
        out_specs=[row(W), row(W), row(MLA_HEADS * MLA_V), row(MLA_RANK), row(MLA_RANK)],
        out_shape=[jax.ShapeDtypeStruct((S, W), BF16), jax.ShapeDtypeStruct((S, W), BF16),
                   jax.ShapeDtypeStruct((S, MLA_HEADS * MLA_V), BF16), jax.ShapeDtypeStruct((S, MLA_RANK), BF16),
                   jax.ShapeDtypeStruct((S, MLA_RANK), BF16)],
        compiler_params=_cparams("parallel"),
    )(z, z, wuq, wukv, g_cq, g_ckv, g_q, g_k, cos, sg)


def _mla_prep_bwd(z, c_kr, cqn, ckvn, wuq, wukv, g_q, g_k, cos, sg, dq, dk, dv, name, tm=256):
    S = z.shape[0]
    tm = min(tm, S)
    zk = _V(z, c_kr, 128)
    inv = 1.0 / MLA_QK

    def body(zk_ref, cqn_ref, ckvn_ref, wuq_ref, wukv_ref, gq_ref, gk_ref, cos_ref, sg_ref, dq_ref, dk_ref, dv_ref,
             dqf_ref, dkv_ref, dkr_ref, dgq_ref, dgk_ref):
        first = pl.program_id(0) == 0
        qf = _dot(cqn_ref[...], wuq_ref[...])
        kv = _dot(ckvn_ref[...], wukv_ref[...])
        kr = zk_ref[...]
        krss = jnp.sum(kr * kr, axis=-1, keepdims=True)
        cosv, sgv = cos_ref[...], sg_ref[...]
        gq, gk = gq_ref[...], gk_ref[...]
        dkr = jnp.zeros((tm, 128), F32)
        dgq = jnp.zeros((tm, MLA_SLOT), F32)
        dgkn = jnp.zeros((tm, 128), F32)
        dgkr = jnp.zeros((tm, 128), F32)
        for h in range(MLA_HEADS):
            qh = qf[:, MLA_SLOT * h:MLA_SLOT * (h + 1)]
            rstd = lax.rsqrt(jnp.sum(qh * qh, axis=-1, keepdims=True) * inv + EPS)
            xhat = qh * rstd
            dyn = jnp.concatenate([dq_ref[:, MLA_SLOT * h:MLA_SLOT * h + 128],
                                   _unrope(dq_ref[:, MLA_SLOT * h + 128:MLA_SLOT * (h + 1)], cosv, sgv)], axis=-1)
            dxhat = dyn * gq
            dqf_ref[:, MLA_SLOT * h:MLA_SLOT * (h + 1)] = (
                rstd * (dxhat - xhat * (jnp.sum(dxhat * xhat, axis=-1, keepdims=True) * inv))).astype(BF16)
            dgq = dgq + dyn * xhat

            kn = kv[:, 256 * h:256 * h + 128]
            rstd = lax.rsqrt((jnp.sum(kn * kn, axis=-1, keepdims=True) + krss) * inv + EPS)
            xn, xr = kn * rstd, kr * rstd
            dyn_n = dk_ref[:, MLA_SLOT * h:MLA_SLOT * h + 128] * (1.0 / MLA_QSCALE)
            dyn_r = _unrope(dk_ref[:, MLA_SLOT * h + 128:MLA_SLOT * (h + 1)] * (1.0 / MLA_QSCALE), cosv, sgv)
            dxn, dxr = dyn_n * gk[:, 0:128], dyn_r * gk[:, 128:256]
            proj = (jnp.sum(dxn * xn, axis=-1, keepdims=True) + jnp.sum(dxr * xr, axis=-1, keepdims=True)) * inv
            dkv_ref[:, 256 * h:256 * h + 128] = (rstd * (dxn - xn * proj)).astype(BF16)
            dkv_ref[:, 256 * h + 128:256 * (h + 1)] = dv_ref[:, 128 * h:128 * (h + 1)].astype(BF16)
            dkr = dkr + rstd * (dxr - xr * proj)
            dgkn = dgkn + dyn_n * xn
            dgkr = dgkr + dyn_r * xr
        dkr_ref[...] = dkr.astype(BF16)
        _acc_rows(dgq_ref, dgq, first)
        _acc_rows(dgk_ref, jnp.concatenate([dgkn, dgkr], axis=-1), first)

    row = lambda w: pl.BlockSpec((tm, w), lambda i: (i, 0))
    const = lambda r, w: pl.BlockSpec((r, w), lambda i: (0, 0))
    W = MLA_HEADS * MLA_SLOT
    return pl.pallas_call(
        body, name=name, grid=(S // tm,),
        in_specs=[zk.spec(tm, 128, lambda i: i, lambda i: 0), row(MLA_RANK), row(MLA_RANK), const(MLA_RANK, W),
                  const(MLA_RANK, W), const(1, MLA_SLOT), const(1, MLA_SLOT), row(128), row(128), row(W), row(W),
                  row(MLA_HEADS * MLA_V)],
        out_specs=[row(W), row(W), row(128), const(8, MLA_SLOT), const(8, MLA_SLOT)],
        out_shape=[jax.ShapeDtypeStruct((S, W), BF16), jax.ShapeDtypeStruct((S, W), BF16), jax.ShapeDtypeStruct((S, 128), BF16),
                   jax.ShapeDtypeStruct((8, MLA_SLOT), F32), jax.ShapeDtypeStruct((8, MLA_SLOT), F32)],
        compiler_params=_cparams("arbitrary"),
    )(z, cqn, ckvn, wuq, wukv, g_q, g_k, cos, sg, dq, dk, dv)


def _exp2_rows(s2):
    e = jnp.exp2(s2 - jnp.max(s2, axis=-1, keepdims=True))
    return e, 1.0 / jnp.sum(e, axis=-1, keepdims=True)


MLA_SUB = 256


def _mla_attn(q, k, v, name, tq=512, ride=None):
    S = q.shape[0]
    tq = min(tq, S)
    sub = math.gcd(MLA_SUB, tq)
    rows = lambda i: slice(i * sub, (i + 1) * sub)

    def body(q_ref, k_ref, v_ref, o_ref):
        def rest(i, s2):
            e, rl = _exp2_rows(s2)
            o_ref[rows(i), :] = (_dot(e.astype(BF16), v_ref[...]) * rl).astype(BF16)

        _one_ahead(tq // sub, lambda i: _dot(q_ref[rows(i), :], k_ref[...], "nt"), rest)

    (o,), got = _ride_call(
        body, ride, name=name, grid=(MLA_HEADS, S // tq),
        in_specs=[pl.BlockSpec((tq, MLA_SLOT), lambda h, i: (i, h)), pl.BlockSpec((S, MLA_SLOT), lambda h, i: (0, h)),
                  pl.BlockSpec((S, MLA_V), lambda h, i: (0, h))],
        out_specs=[pl.BlockSpec((tq, MLA_V), lambda h, i: (i, h))],
        out_shape=[jax.ShapeDtypeStruct((S, MLA_HEADS * MLA_V), BF16)], args=(q, k, v), semantics=("parallel", "parallel"))
    return o, got


def _mla_attn_bwd(q, k, v, do, name, tq=512, ride=None):
    S = q.shape[0]
    tq = min(tq, S)
    sub = math.gcd(MLA_SUB, tq)
    rows = lambda i: slice(i * sub, (i + 1) * sub)
    scale = MLA_QK ** -0.5

    def body(q_ref, k_ref, v_ref, do_ref, dq_ref, dk_ref, dv_ref):
        @pl.when(pl.program_id(1) == 0)
        def _():
            dk_ref[...] = jnp.zeros_like(dk_ref)
            dv_ref[...] = jnp.zeros_like(dv_ref)

        def matmuls(i):
            return _dot(q_ref[rows(i), :], k_ref[...], "nt"), _dot(do_ref[rows(i), :], v_ref[...], "nt")

        def rest(i, s2_dp):
            s2, dp = s2_dp
            e, rl = _exp2_rows(s2)
            dp = dp * (scale * rl)
            ds = (e * (dp - jnp.sum(e * dp, axis=-1, keepdims=True) * rl)).astype(BF16)
            dq_ref[rows(i), :] = _dot(ds, k_ref[...])
            dk_ref[...] += _dot(ds, q_ref[rows(i), :], "tn")
            dv_ref[...] += _dot(e.astype(BF16), (do_ref[rows(i), :].astype(F32) * rl).astype(BF16), "tn")

        _one_ahead(tq // sub, matmuls, rest)

    W = MLA_HEADS * MLA_SLOT
    return _ride_call(
        body, ride, name=name, grid=(MLA_HEADS, S // tq),
        in_specs=[pl.BlockSpec((tq, MLA_SLOT), lambda h, i: (i, h)), pl.BlockSpec((S, MLA_SLOT), lambda h, i: (0, h)),
                  pl.BlockSpec((S, MLA_V), lambda h, i: (0, h)), pl.BlockSpec((tq, MLA_V), lambda h, i: (i, h))],
        out_specs=[pl.BlockSpec((tq, MLA_SLOT), lambda h, i: (i, h)), pl.BlockSpec((S, MLA_SLOT), lambda h, i: (0, h)),
                   pl.BlockSpec((S, MLA_V), lambda h, i: (0, h))],
        out_shape=[jax.ShapeDtypeStruct((S, W), F32), jax.ShapeDtypeStruct((S, W), F32),
                   jax.ShapeDtypeStruct((S, MLA_HEADS * MLA_V), F32)],
        args=(q, k, v, do), semantics=("parallel", "arbitrary"))


def _merge(ys, ws, z, name, tm=512):
    S = z.shape[0]
    D = ws[0].shape[1]
    tm = min(tm, S)
    zg = _V(z, 0, 3 * D)

    def body(y0, y1, y2, w0, w1, w2, g_ref, m_ref, p0, p1, p2):
        acc = jnp.zeros((tm, D), F32)
        for i, (y_ref, w_ref, p_ref) in enumerate(((y0, w0, p0), (y1, w1, p1), (y2, w2, p2))):
            pv = _dot(y_ref[...], w_ref[...])
            p_ref[...] = pv.astype(BF16)
            acc = acc + jax.nn.sigmoid(g_ref[:, D * i:D * (i + 1)]) * pv
        m_ref[...] = acc.astype(BF16)

    yb = pl.BlockSpec((tm, ys[0].shape[1]), lambda i: (i, 0))
    wb = pl.BlockSpec(ws[0].shape, lambda i: (0, 0))
    ob = pl.BlockSpec((tm, D), lambda i: (i, 0))
    return pl.pallas_call(
        body, name=name, grid=(S // tm,), in_specs=[yb] * 3 + [wb] * 3 + [zg.spec(tm, 3 * D, lambda i: i, lambda i: 0)],
        out_specs=[ob] * 4, out_shape=[jax.ShapeDtypeStruct((S, D), BF16)] * 4, compiler_params=_cparams("parallel"),
    )(*ys, *ws, z)


def _merge_bwd(dmixed, ps, z, name, tm=512):
    S, D = dmixed.shape
    tm = min(tm, S)
    zg = _V(z, 0, 3 * D)

    def body(dm_ref, p0, p1, p2, g_ref, d0, d1, d2, dg_ref):
        dm = dm_ref[...]
        for i, (p_ref, d_ref) in enumerate(((p0, d0), (p1, d1), (p2, d2))):
            gt = jax.nn.sigmoid(g_ref[:, D * i:D * (i + 1)])
            d_ref[...] = (dm * gt).astype(BF16)
            dg_ref[:, D * i:D * (i + 1)] = (dm * p_ref[...].astype(F32) * gt * (1.0 - gt)).astype(BF16)

    ob = pl.BlockSpec((tm, D), lambda i: (i, 0))
    return pl.pallas_call(
        body, name=name, grid=(S // tm,), in_specs=[ob] * 4 + [zg.spec(tm, 3 * D, lambda i: i, lambda i: 0)],
        out_specs=[ob] * 3 + [pl.BlockSpec((tm, 3 * D), lambda i: (i, 0))],
        out_shape=[jax.ShapeDtypeStruct((S, D), BF16)] * 3 + [jax.ShapeDtypeStruct((S, 3 * D), BF16)],
        compiler_params=_cparams("parallel"),
    )(dmixed, *ps, z)


def _loss_head(y, target, name, tm=512):
    S, D = y.shape
    tm = min(tm, S)

    def body(y_ref, t_ref, dy_ref, l_ref):
        e = y_ref[...] - t_ref[...]
        dy_ref[...] = e * (1.0 / D)
        sq = e * e
        part = jnp.sum(sq.reshape(tm // 8, 8, D), axis=0)
        part = jnp.sum(part.reshape(8, D // 128, 128), axis=1) * (0.5 / D)

        @pl.when(pl.program_id(0) == 0)
        def _():
            l_ref[...] = part

        @pl.when(pl.program_id(0) != 0)
        def _():
            l_ref[...] += part

    blk = pl.BlockSpec((tm, D), lambda i: (i, 0))
    return pl.pallas_call(
        body, name=name, grid=(S // tm,), in_specs=[blk, blk], out_specs=[blk, pl.BlockSpec((8, 128), lambda i: (0, 0))],
        out_shape=[jax.ShapeDtypeStruct((S, D), F32), jax.ShapeDtypeStruct((8, 128), F32)],
        compiler_params=_cparams("arbitrary"),
    )(y, target)


def _fold(parts, name, fold=None):
    L, _, W = parts.shape
    assert L <= 8

    def body(*refs):
        p_ref, o_ref = refs[0], refs[-1]
        rows = [jnp.sum(p_ref[l], axis=0, keepdims=True) for l in range(L)]
        rows += [jnp.zeros((1, W), F32)] * (8 - L)
        sums = jnp.concatenate(rows, axis=0)
        o_ref[...] = sums if fold is None else _dot(sums, refs[1][...], prec=HI)

    args = (parts,) if fold is None else (parts, jnp.asarray(fold))
    wout = W if fold is None else 128
    return pl.pallas_call(body, name=name, out_shape=jax.ShapeDtypeStruct((8, wout), F32))(*args)[:L]


def _adamw(w, g, m, v, name, q=None, ride=None):
    R, C = w.shape
    tr = R
    for cand in (512, 256, 128, 64, 32, 16, 8):
        if R % cand == 0 and cand * C * 4 <= 2 * 2**20:
            tr = cand
            break

    def body(*refs):
        if q is None:
            w_ref, g_ref, m_ref, v_ref, d_ref, nm_ref, nv_ref = refs
            gv = g_ref[...]
        else:
            w_ref, g_ref, q_ref, m_ref, v_ref, go_ref, d_ref, nm_ref, nv_ref = refs
            gv = g_ref[...] + q_ref[...]
            go_ref[...] = gv
        mn = ADAM_B1 * m_ref[...] + (1.0 - ADAM_B1) * gv
        vn = ADAM_B2 * v_ref[...] + (1.0 - ADAM_B2) * (gv * gv)
        nm_ref[...] = mn
        nv_ref[...] = vn
        m_hat = mn / (1.0 - ADAM_B1 ** ADAM_STEP)
        v_hat = vn / (1.0 - ADAM_B2 ** ADAM_STEP)
        d_ref[...] = -ADAM_LR * (m_hat / (jnp.sqrt(v_hat) + ADAM_EPS) + ADAM_WD * w_ref[...])

    blk = pl.BlockSpec((tr, C), lambda i: (i, 0))
    args = (w, g, m, v) if q is None else (w, g, q, m, v)
    nout = 3 if q is None else 4
    outs, got = _ride_call(
        body, ride, name=name, grid=(R // tr,), in_specs=[blk] * len(args), out_specs=[blk] * nout,
        out_shape=[jax.ShapeDtypeStruct((R, C), F32)] * nout, args=args, semantics=("parallel",))
    return outs if ride is None else (outs, got)


def _sibling_exchange(srcs, name, ride=()):
    n, k = len(srcs), len(ride)

    def body(*refs):
        src_refs, dst_refs = refs[:n], refs[n + 2 * k:2 * n + 2 * k]
        send_sems, recv_sems = refs[2 * n + 3 * k:2 * n + 3 * k + 2]
        x, y, c = lax.axis_index("x"), lax.axis_index("y"), lax.axis_index("c")
        if k:
            start, finish = _ride_ops(ride, refs[n:n + k], refs[2 * n + 2 * k:2 * n + 3 * k], *refs[2 * n + 3 * k + 2:])
            start()
        copies = [pltpu.make_async_remote_copy(src_ref=src_refs[t], dst_ref=dst_refs[t], send_sem=send_sems.at[t],
                                               recv_sem=recv_sems.at[t], device_id=(x, y, 1 - c), device_id_type=MESH)
                  for t in range(n)]
        for cp in copies:
            cp.start()
        for cp in copies:
            cp.wait()
        if k:
            finish()

    res = pl.pallas_call(
        body, name=name, in_specs=[_ANY] * (n + 2 * k), out_specs=[_ANY] * (n + k),
        out_shape=[jax.ShapeDtypeStruct(s.shape, s.dtype) for s in srcs] + [jax.ShapeDtypeStruct(it.dst.shape, it.dst.dtype) for it in ride],
        input_output_aliases={n + k + t: n + t for t in range(k)},
        scratch_shapes=[pltpu.SemaphoreType.DMA((n,)), pltpu.SemaphoreType.DMA((n,))] + (_RIDE_SEMS(k) if k else []),
    )(*srcs, *[it.src for it in ride], *[it.dst for it in ride])
    return res[:n], {it.name: o for it, o in zip(ride, res[n:])}


def _allreduce_small(v, name):
    R = v.shape[0]

    def body(v_ref, o_ref, slots, send_sems, recv_sems):
        x, y, c = lax.axis_index("x"), lax.axis_index("y"), lax.axis_index("c")
        me = 4 * x + 2 * y + c
        slots[me] = v_ref[...]
        sent = []
        for r in range(1, 8):
            fx, fy, fc = (r >> 2) & 1, (r >> 1) & 1, r & 1
            px, py, pc = (1 - x) if fx else x, (1 - y) if fy else y, (1 - c) if fc else c
            peer = 4 * px + 2 * py + pc

            def copy(slot, r=r, px=px, py=py, pc=pc):
                return pltpu.make_async_remote_copy(
                    src_ref=v_ref, dst_ref=slots.at[slot], send_sem=send_sems.at[r - 1], recv_sem=recv_sems.at[r - 1],
                    device_id=(px, py, pc), device_id_type=MESH)

            cp = copy(me)
            cp.start()
            sent.append((cp, copy(peer)))
        for cp, arrival in sent:
            cp.wait_send()
            arrival.wait_recv()
        acc = slots[0]
        for k in range(1, 8):
            acc = acc + slots[k]
        o_ref[...] = acc

    vm = pl.BlockSpec(memory_space=pltpu.VMEM)
    return pl.pallas_call(
        body, name=name, in_specs=[vm], out_specs=vm, out_shape=jax.ShapeDtypeStruct((R, 128), F32),
        scratch_shapes=[pltpu.VMEM((8, R, 128), F32), pltpu.SemaphoreType.DMA((7,)), pltpu.SemaphoreType.DMA((7,))],
    )(v)


def _sum4(recv, name, tr=512):
    _, R, W = recv.shape
    tr = _tile(R, tr)
    assert R % tr == 0

    def body(r_ref, o_ref):
        o_ref[...] = ((r_ref[0].astype(F32) + r_ref[1].astype(F32)) + r_ref[2].astype(F32)) + r_ref[3].astype(F32)

    return pl.pallas_call(
        body, name=name, grid=(R // tr,), in_specs=[pl.BlockSpec((4, tr, W), lambda i: (0, i, 0))],
        out_specs=pl.BlockSpec((tr, W), lambda i: (i, 0)), out_shape=jax.ShapeDtypeStruct((R, W), F32),
        compiler_params=_cparams("parallel"),
    )(recv)


W_NAMES = ("ffn1_norm", "ffn1_w1", "ffn1_w3", "ffn1_w2", "mix_norm", "w_in", "na_q_norm", "na_k_norm", "na_rpb",
           "gla_gf_up", "gla_gf_bias", "gla_gb_up", "gla_gb_bias", "gla_out_norm", "mla_cq_norm", "mla_ckv_norm",
           "mla_w_uq", "mla_w_ukv", "mla_q_norm", "mla_k_norm", "w_br_na", "w_br_gla", "w_br_mla", "w_out",
           "ffn2_norm", "ffn2_w1", "ffn2_w3", "ffn2_w2")
SHARDED = {"ffn1_w1": 2, "ffn1_w3": 2, "ffn1_w2": 1, "w_in": 2, "gla_gf_up": 2, "gla_gb_up": 2, "mla_w_uq": 2,
           "mla_w_ukv": 2, "w_br_na": 2, "w_br_gla": 2, "w_br_mla": 2, "w_out": 1, "ffn2_w1": 2, "ffn2_w3": 2,
           "ffn2_w2": 1}
REPLICATED = tuple(n for n in W_NAMES if n not in SHARDED)
FFN_W = ("ffn1_w1", "ffn1_w3", "ffn1_w2", "ffn2_w1", "ffn2_w3", "ffn2_w2")


def _cols_of(parts, lo, hi):
    out, off = [], 0
    for a in parts:
        w = a.shape[-1]
        s, e = max(lo, off), min(hi, off + w)
        if s < e:
            out.append(a[..., s - off:e - off])
        off += w
    return out


def _win_layout(pieces, D):
    z = lambda n: [jnp.zeros(pieces[0].shape[:-1] + (n,), pieces[0].dtype)]
    c = lambda lo, hi: _cols_of(pieces, lo, hi)
    return jnp.concatenate(c(O_GATES, O_GATES + 3 * D) + c(0, O_GFL) + c(O_CQ, O_KR) + c(O_GFL, O_CQ) + z(96)
                           + c(O_KR, O_KR + 32) + z(32) + c(O_KR + 32, O_KR + 64) + z(32), axis=-1)


def _win_split4(dw_segs, D):
    gates, na, gla, cq, ckv, lr, kr = dw_segs
    parts = [na, gla, lr[:, 0:32], cq, ckv, kr[:, 0:32], kr[:, 64:96], gates]
    n = (O_GATES + 3 * D) // 4
    return jnp.stack([jnp.concatenate(_cols_of(parts, j * n, (j + 1) * n), axis=1) for j in range(4)]).astype(BF16)


def _uq_layout(w):
    s = w.shape[:-1]
    w = w.reshape(s + (MLA_HEADS, MLA_QK))
    z = jnp.zeros(s + (MLA_HEADS, 32), w.dtype)
    return jnp.concatenate([w[..., :160], z, w[..., 160:], z], axis=-1).reshape(s + (MLA_HEADS * MLA_SLOT,))


def _uq_unlayout(dw):
    s = dw.shape[:-1]
    dw = dw.reshape(s + (MLA_HEADS, MLA_SLOT))
    return jnp.concatenate([dw[..., :160], dw[..., 192:224]], axis=-1).reshape(s + (MLA_HEADS * MLA_QK,))


def _slot_layout(g):
    z = jnp.zeros(g.shape[:-1] + (32,), g.dtype)
    return jnp.concatenate([g[..., :160], z, g[..., 160:], z], axis=-1)


def _slot_unlayout(g):
    return jnp.concatenate([g[..., :160], g[..., 192:224]], axis=-1)


def _layer_fwd(x, w, cos, sg, rides, mixer=None):
    D = x.shape[1]
    NA, GL, ML, LR, KR = 3 * D, 3 * D + 1536, 3 * D + 3072, 3 * D + 3584, 3 * D + 3712
    got = {}
    x1, f1, arrived = _ffn_fwd(x, w["ffn1_norm"], w["ffn1_w1"], w["ffn1_w3"], w["ffn1_w2"], "ffn1", ride=rides.get("ffn1_up"))
    got.update(arrived)
    if mixer is not None:
        w = {**w, **mixer(arrived)}
    h = _rms_fwd(x1, w["mix_norm"], "mix_rms")
    nz = w["w_in"].shape[1]
    z, arrived = _mm([(h, w["w_in"])], "nn", F32, "w_in", tm=1024, tn=_tile(nz, 1280), ride=rides.get("w_in", []))
    got.update(arrived)
    qn, kn, vb = _na_prep(z, NA, w["na_gq"], w["na_gk"], "na_prep")
    bias = _rpb_expand(w["na_rpb"], "rpb_expand")
    y_na, arrived = _na_attn(qn, kn, vb, bias, "na_attn", ride=rides.get("na_attn"))
    got.update(arrived)
    gfb = _gla_gates(z, LR, w["gla_wg"], w["gla_gbias"], "gla_gates")
    (o_f, o_b, s_f, s_b), arrived = _gla_fwd(z, GL, GL + 512, gfb, "gla_fwd", ride=rides.get("gla_fwd"))
    got.update(arrived)
    y_gla = _gla_post(o_f, o_b, z, GL + 1024, w["gla_out_norm"], "gla_post")
    q, k, v, cqn, ckvn = _mla_prep(z, ML, KR, w["mla_wuq"], w["mla_w_ukv"], w["mla_cq_norm"], w["mla_ckv_norm"],
                                   w["mla_gq"], w["mla_gk"], cos, sg, "mla_prep")
    y_mla, arrived = _mla_attn(q, k, v, "mla_attn", ride=rides.get("mla_attn"))
    got.update(arrived)
    mixed, p0, p1, p2 = _merge([y_na, y_gla, y_mla], [w["w_br_na"], w["w_br_gla"], w["w_br_mla"]], z, "merge")
    x2 = _mm([(mixed, w["w_out"])], "nn", F32, "w_out", tm=512, tn=1024, res=x1)
    x3, f2, _ = _ffn_fwd(x2, w["ffn2_norm"], got["ffn2_w1"], got["ffn2_w3"], got["ffn2_w2"], "ffn2")
    saved = dict(x=x, x1=x1, x2=x2, f1=f1, f2=f2, h=h, z=z, qn=qn, kn=kn, vb=vb, bias=bias, y_na=y_na, gfb=gfb, o_f=o_f,
                 o_b=o_b, s_f=s_f, s_b=s_b, y_gla=y_gla, q=q, k=k, v=v, cqn=cqn, ckvn=ckvn, y_mla=y_mla, mixed=mixed,
                 p0=p0, p1=p1, p2=p2)
    return x3, saved, got, w


def _split4(a, axis):
    n = a.shape[axis] // 4
    return jnp.stack([lax.slice_in_dim(a, j * n, (j + 1) * n, axis=axis) for j in range(4)]).astype(BF16)


def _layer_bwd(dx3, w, sv, cos, sg, bufs, recv, layer, prev, flush=False):
    D = dx3.shape[1]
    at_layer = lambda chip: (chip, layer)
    pick = lambda *names: [prev[n] for n in names if n in prev]
    recv = dict(recv)
    NA, GL, ML, LR, KR = 3 * D, 3 * D + 1536, 3 * D + 3072, 3 * D + 3584, 3 * D + 3712
    z = sv["z"]
    g = {}
    dx2, g["ffn2_norm"], (g["ffn2_w1"], g["ffn2_w3"], g["ffn2_w2"]), got = _ffn_bwd(
        dx3, sv["x2"], w["ffn2_norm"], w["ffn2_w1"], w["ffn2_w3"], w["ffn2_w2"], sv["f2"], "ffn2",
        (bufs["ffn2_w1"], bufs["ffn2_w3"], bufs["ffn2_w2"]), layer, ride_down=pick("ffn1_w1"), ride_dh=pick("ffn1_w3"))
    recv.update(got)
    dmixed = _mm([(dx2, w["w_out"])], "nt", F32, "w_out_dx", tm=512, tn=512)
    g["w_out"] = _mm([(sv["mixed"], dx2)], "tn", F32, "w_out_dw", tm=D, tn=256)
    d0, d1, d2, dgates = _merge_bwd(dmixed, [sv["p0"], sv["p1"], sv["p2"]], z, "merge_bwd")
    dys = []
    for d, y, nm, dt in ((d0, sv["y_na"], "w_br_na", BF16), (d1, sv["y_gla"], "w_br_gla", F32), (d2, sv["y_mla"], "w_br_mla", BF16)):
        dys.append(_mm([(d, w[nm])], "nt", dt, nm + "_dy", tm=512, tn=512))
        g[nm] = _mm([(y, d)], "tn", F32, nm + "_dw", tm=512, tn=512)
    (dqn, dkn, dvn, dbias), got = _na_attn_bwd(sv["qn"], sv["kn"], sv["vb"], sv["bias"], dys[0], "na_attn_bwd",
                                               ride=pick("ffn1_w2", "mla_w_uq", "mla_w_ukv", "gla_gf_up", "gla_gb_up"))
    recv.update(got)
    dz_na, g["na_gq"], g["na_gk"] = _na_prep_bwd(z, NA, w["na_gq"], w["na_gk"], dqn, dkn, dvn, "na_prep_bwd")
    g["na_rpb"] = _rpb_reduce(dbias, "rpb_reduce")
    do, dgr, g["gla_out_norm"] = _gla_post_bwd(sv["o_f"], sv["o_b"], z, GL + 1024, w["gla_out_norm"], dys[1], "gla_post_bwd")
    (dqk_f, dv_f, dg_f, dqk_b, dv_b, dg_b), _ = _gla_bwd(z, GL, GL + 512, sv["gfb"], do, sv["s_f"], sv["s_b"], "gla_bwd")
    dz_gla = _gla_assemble(dqk_f, dqk_b, dv_f, dv_b, dgr, "gla_assemble")
    dpre, g["gla_gbias"] = _gla_gates_bwd(z, LR, w["gla_wg"], w["gla_gbias"], dg_f, dg_b, "gla_gates_bwd")
    g["gla_wg"] = _mm([(_V(z, LR, 128), dpre)], "tn", F32, "gla_wg_dw", tm=128, tn=512)
    dz_lr = _mm([(dpre, w["gla_wg"])], "nt", BF16, "gla_wg_dz", tm=512, tn=128)
    own = lambda n: _Ride(n, g[n], at_layer, recv[n], at_layer)
    (dq, dk, dv), got = _mla_attn_bwd(sv["q"], sv["k"], sv["v"], dys[2], "mla_attn_bwd",
                                      ride=pick("w_in") + [own("ffn2_w1"), own("ffn2_w3")])
    recv.update(got)
    dqf, dkv, dz_kr, g["mla_gq"], g["mla_gk"] = _mla_prep_bwd(
        z, KR, sv["cqn"], sv["ckvn"], w["mla_wuq"], w["mla_w_ukv"], w["mla_gq"], w["mla_gk"], cos, sg, dq, dk, dv, "mla_prep_bwd")
    g["mla_wuq"] = _mm([(sv["cqn"], dqf)], "tn", F32, "mla_wuq_dw", tm=256, tn=512)
    g["mla_w_ukv"] = _mm([(sv["ckvn"], dkv)], "tn", F32, "mla_wukv_dw", tm=256, tn=512)
    dcqn = _mm([(dqf, w["mla_wuq"])], "nt", F32, "mla_wuq_dx", tm=512, tn=256)
    dckvn = _mm([(dkv, w["mla_w_ukv"])], "nt", F32, "mla_wukv_dx", tm=512, tn=256)
    dz_cq, dg_cq = _rms_bwd(_V(z, ML, MLA_RANK), w["mla_cq_norm"], dcqn, "mla_cq_rms_bwd", out_dtype=BF16)
    dz_ckv, dg_ckv = _rms_bwd(_V(z, ML + MLA_RANK, MLA_RANK), w["mla_ckv_norm"], dckvn, "mla_ckv_rms_bwd", out_dtype=BF16)
    g["mla_cq_norm"], g["mla_ckv_norm"] = dg_cq[0:1], dg_ckv[0:1]
    segs = ((dgates, 0, 3 * D), (dz_na, NA, 1536), (dz_gla, GL, 1536), (dz_cq, ML, MLA_RANK), (dz_ckv, ML + MLA_RANK, MLA_RANK),
            (dz_lr, LR, 128), (dz_kr, KR, 128))
    dh, got = _mm([(dz, _V(w["w_in"], c0, wd)) for dz, c0, wd in segs], "nt", F32, "w_in_dx", tm=512, tn=512,
                  ride=[own("ffn2_w2")])
    recv.update(got)
    dw_in = [_mm([(sv["h"], dz)], "tn", F32, f"w_in_dw{i}", tm=D, tn=_tile(wd, 256)) for i, (dz, _, wd) in enumerate(segs)]
    dx1, dg_mix = _rms_bwd(sv["x1"], w["mix_norm"], dh, "mix_rms_bwd", dres=dx2)
    g["mix_norm"] = dg_mix[0:1]
    late = dict(w_in=_win_split4(dw_in, D), mla_w_uq=_split4(_uq_unlayout(g["mla_wuq"]), 1),
                mla_w_ukv=_split4(g["mla_w_ukv"], 1), gla_gf_up=_split4(g["gla_wg"][0:GLA_RANK, 0:256], 1),
                gla_gb_up=_split4(g["gla_wg"][GLA_RANK:2 * GLA_RANK, 256:512], 1))
    ride = [_Ride(n, _split4(g[n], SHARDED[n] - 1), lambda chip: (chip,), recv[n], at_layer)
            for n in ("w_out", "w_br_na", "w_br_gla", "w_br_mla")]
    ride_dh = None
    if flush:
        ride_dh, late = [_Ride(n, late[n], lambda chip: (chip,), recv[n], at_layer) for n in late], {}
    dx, g["ffn1_norm"], (g["ffn1_w1"], g["ffn1_w3"], g["ffn1_w2"]), got = _ffn_bwd(
        dx1, sv["x"], w["ffn1_norm"], w["ffn1_w1"], w["ffn1_w3"], w["ffn1_w2"], sv["f1"], "ffn1",
        (bufs["ffn1_w1"], bufs["ffn1_w3"], bufs["ffn1_w2"]), layer, ride_down=ride, ride_dh=ride_dh)
    recv.update(got)
    return dx, g, late, recv


def _head_fold(width, period, lo=0):
    f = np.zeros((width, 128), np.float32)
    f[np.arange(width), lo + np.arange(width) % period] = 1.0
    return f


def kernel(x, ffn1_norm, ffn1_w1, ffn1_w3, ffn1_w2, mix_norm, w_in, na_q_norm, na_k_norm, na_rpb, gla_gf_up, gla_gf_bias,
           gla_gb_up, gla_gb_bias, gla_out_norm, mla_cq_norm, mla_ckv_norm, mla_w_uq, mla_w_ukv, mla_q_norm, mla_k_norm,
           w_br_na, w_br_gla, w_br_mla, w_out, ffn2_norm, ffn2_w1, ffn2_w3, ffn2_w2, loss_target, m_ffn1_norm, m_ffn1_w1,
           m_ffn1_w3, m_ffn1_w2, m_mix_norm, m_w_in, m_na_q_norm, m_na_k_norm, m_na_rpb, m_gla_gf_up, m_gla_gf_bias,
           m_gla_gb_up, m_gla_gb_bias, m_gla_out_norm, m_mla_cq_norm, m_mla_ckv_norm, m_mla_w_uq, m_mla_w_ukv,
           m_mla_q_norm, m_mla_k_norm, m_w_br_na, m_w_br_gla, m_w_br_mla, m_w_out, m_ffn2_norm, m_ffn2_w1, m_ffn2_w3,
           m_ffn2_w2, v_ffn1_norm, v_ffn1_w1, v_ffn1_w3, v_ffn1_w2, v_mix_norm, v_w_in, v_na_q_norm, v_na_k_norm,
           v_na_rpb, v_gla_gf_up, v_gla_gf_bias, v_gla_gb_up, v_gla_gb_bias, v_gla_out_norm, v_mla_cq_norm,
           v_mla_ckv_norm, v_mla_w_uq, v_mla_w_ukv, v_mla_q_norm, v_mla_k_norm, v_w_br_na, v_w_br_gla, v_w_br_mla,
           v_w_out, v_ffn2_norm, v_ffn2_w1, v_ffn2_w3, v_ffn2_w2):
    given = dict(locals())
    wts = {n: given[n] for n in W_NAMES}
    mom = {n: given["m_" + n] for n in W_NAMES}
    var = {n: given["v_" + n] for n in W_NAMES}
    xs, target = x[0], loss_target[0]
    S, D = xs.shape
    L = ffn1_norm.shape[0]

    sh_names = tuple(SHARDED)
    LATE = ("ffn2_w1", "ffn2_w3", "ffn2_w2")
    HEAVY = ("ffn1_w1", "ffn1_w3", "ffn1_w2", "w_in")
    LIGHT = tuple(n for n in sh_names if n not in LATE + HEAVY)
    shard_shape = lambda n: tuple(wts[n].shape[1:])

    def gather_items(names, l):
        return [_Ride(n, wts[n][l].astype(BF16), lambda chip: (), lax.empty((4,) + shard_shape(n), BF16), lambda chip: (chip,),
                      halves=shard_shape(n)[0] % 32 == 0) for n in names]

    cols = lambda p: jnp.concatenate([p[j] for j in range(4)], axis=-1)

    def ffn1_weights(gl, l):
        return dict(ffn1_norm=ffn1_norm[l][None], ffn1_w1=gl["ffn1_w1"], ffn1_w3=gl["ffn1_w3"], ffn1_w2=gl["ffn1_w2"])

    def mixer_weights(gl, l):
        r1 = lambda a: a[l][None]
        wg = jnp.zeros((128, 2 * GLA_HEADS * GLA_DK), BF16)
        wg = wg.at[0:GLA_RANK, 0:256].set(cols(gl["gla_gf_up"])).at[GLA_RANK:2 * GLA_RANK, 256:512].set(cols(gl["gla_gb_up"]))
        return dict(
            mix_norm=r1(mix_norm), w_in=_win_layout([gl["w_in"][j] for j in range(4)], D),
            na_gq=jnp.tile(na_q_norm[l], NA_HEADS)[None], na_gk=jnp.tile(na_k_norm[l], NA_HEADS)[None], na_rpb=na_rpb[l],
            gla_wg=wg, gla_gbias=jnp.concatenate([gla_gf_bias[l], gla_gb_bias[l]])[None], gla_out_norm=r1(gla_out_norm),
            mla_cq_norm=r1(mla_cq_norm), mla_ckv_norm=r1(mla_ckv_norm), mla_wuq=_uq_layout(cols(gl["mla_w_uq"])),
            mla_w_ukv=cols(gl["mla_w_ukv"]), mla_gq=_slot_layout(mla_q_norm[l])[None], mla_gk=_slot_layout(mla_k_norm[l])[None],
            w_br_na=cols(gl["w_br_na"]), w_br_gla=cols(gl["w_br_gla"]), w_br_mla=cols(gl["w_br_mla"]),
            w_out=gl["w_out"].reshape(D, D), ffn2_norm=r1(ffn2_norm))

    half = MLA_ROPE // 2
    inv = ROPE_THETA ** (-jnp.arange(half, dtype=F32) / half)
    ang = jnp.arange(S, dtype=F32)[:, None] * inv[None, :]
    cos = jnp.tile(jnp.cos(ang), (1, 4))
    sg = jnp.concatenate([-jnp.sin(ang), -jnp.sin(ang), jnp.sin(ang), jnp.sin(ang)], axis=1)

    FFN1 = ("ffn1_w1", "ffn1_w3", "ffn1_w2")
    arrived = _exchange(gather_items(FFN1, 0), "weights_all_gather")
    xc, saved, layers = xs, [], []
    for l in range(L):
        w = ffn1_weights(arrived, l)
        if l == 0:
            rides = {"ffn1_up": gather_items(("w_in",) + LIGHT, 0), "w_in": gather_items(LATE, 0)}
            mixer = lambda got, l=l: mixer_weights(got, l)
        else:
            rides = {"ffn1_up": gather_items(("ffn2_w1", "ffn2_w3"), l), "w_in": gather_items(("ffn2_w2",), l)}
            w.update(mixer_weights(arrived, l))
            mixer = None
        if l + 1 < L:
            rides["na_attn"] = gather_items(("ffn1_w1", "ffn1_w3"), l + 1)
            rides["gla_fwd"] = gather_items(("ffn1_w2",), l + 1)
            rides["mla_attn"] = gather_items(("w_in",) + LIGHT, l + 1)
        xc, sv, arrived, w = _layer_fwd(xc, w, cos, sg, rides, mixer)
        saved.append(sv)
        layers.append({**w, **{n: arrived[n] for n in LATE}})
    dy, loss_part = _loss_head(xc, target, "loss_head")

    bufs = {n: lax.empty((4, L) + shard_shape(n), BF16) for n in FFN_W}
    recv = {n: lax.empty((4, L) + shard_shape(n), BF16) for n in sh_names}
    dx, g, prev = dy, [None] * L, {}
    for l in reversed(range(L)):
        dx, g[l], late, recv = _layer_bwd(dx, layers[l], saved[l], cos, sg, bufs, recv, l, prev, flush=l == 0)
        bufs = {n: g[l][n] for n in FFN_W}
        at_l = functools.partial(lambda chip, l: (chip, l), l=l)
        prev = {n: _Ride(n, bufs[n], at_l, recv[n], at_l) for n in ("ffn1_w1", "ffn1_w3", "ffn1_w2")}
        prev.update({n: _Ride(n, late[n], lambda chip: (chip,), recv[n], at_l) for n in late})

    stk = lambda n: jnp.stack([g[l][n] for l in range(L)])
    gs = {n: stk(n)[:, 0] for n in ("ffn1_norm", "mix_norm", "mla_cq_norm", "mla_ckv_norm", "ffn2_norm")}
    gs["na_q_norm"] = _fold(stk("na_gq"), "na_gq_fold", _head_fold(NA_W, NA_DH))[:, :NA_DH]
    gs["na_k_norm"] = _fold(stk("na_gk"), "na_gk_fold", _head_fold(NA_W, NA_DH))[:, :NA_DH]
    gs["na_rpb"] = stk("na_rpb")
    gbias = _fold(stk("gla_gbias"), "gla_gbias_fold")
    gs["gla_gf_bias"], gs["gla_gb_bias"] = gbias[:, :256], gbias[:, 256:]
    gs["gla_out_norm"] = _fold(stk("gla_out_norm"), "gla_out_norm_fold")
    gs["mla_q_norm"] = _slot_unlayout(_fold(stk("mla_gq"), "mla_gq_fold"))
    gs["mla_k_norm"] = _slot_unlayout(_fold(stk("mla_gk"), "mla_gk_fold"))

    as2d = lambda a: a.reshape(-1, a.shape[-1])
    gsh, upd = {}, {}
    DONE = ("ffn2_w1", "ffn2_w3", "ffn2_w2", "w_out", "w_br_na", "w_br_gla", "w_br_mla")

    def reduce_and_update(names, with_exchange):
        mine = [_sum4(recv[n].reshape(4, -1, recv[n].shape[-1]), "grads_chip_sum_" + n) for n in names]
        other, arrived = _sibling_exchange(mine, "grads_sibling_exchange", ride=with_exchange)
        for n, p, q in zip(names, mine, other):
            outs = [o.reshape(wts[n].shape) for o in _adamw(as2d(wts[n]), p, as2d(mom[n]), as2d(var[n]), "adamw_" + n, q=q)]
            gsh[n], upd[n] = outs[0], outs[1:]
        return arrived

    recv.update(reduce_and_update(DONE, list(prev.values())))
    reduce_and_update(tuple(n for n in sh_names if n not in DONE), [])

    small_shapes = [wts[n].shape[1:] for n in REPLICATED]
    n_small = sum(int(np.prod(s)) for s in small_shapes) * L
    flat = jnp.concatenate([gs[n].reshape(-1) for n in REPLICATED] + [loss_part.reshape(-1)])
    pad = -flat.shape[0] % 1024
    red = _allreduce_small(jnp.pad(flat, (0, pad)).reshape(-1, 128), "small_all_reduce").reshape(-1)
    loss = jnp.sum(red[n_small:n_small + 1024])
    off = 0
    for n, s in zip(REPLICATED, small_shapes):
        cnt = int(np.prod(s)) * L
        gsh[n] = red[off:off + cnt].reshape((L,) + tuple(s))
        off += cnt

    pk = lambda d: jnp.pad(jnp.concatenate([d[n].reshape(-1) for n in REPLICATED]), (0, -n_small % 1024)).reshape(-1, 128)
    small = _adamw(pk(wts), pk(gsh), pk(mom), pk(var), "adamw_replicated")
    off = 0
    for n, s in zip(REPLICATED, small_shapes):
        cnt = int(np.prod(s)) * L
        upd[n] = [o.reshape(-1)[off:off + cnt].reshape((L,) + tuple(s)) for o in small]
        off += cnt

    return (loss, dx[None], *[gsh[n] for n in W_NAMES], *[upd[n][0] for n in W_NAMES], *[upd[n][1] for n in W_NAMES],
            *[upd[n][2] for n in W_NAMES])
```
